```python
import math
import jax
import jax.numpy as jnp
from jax import lax
import numpy as np

D_MODEL = 1024
BATCH = 16
SEQ = 2048
DEPTH = 2
DEC_BATCH = 32
DEC_SEQ = 32
PAST_LEN = 2048

CHUNK = 64
N_EVEN = (DEPTH + 1) // 2
N_ODD = DEPTH // 2
EPS = 1e-6
QBLK = 128
DA_HEADS = 4
DA_DH = 64
DA_DV = 2 * DA_DH
DA_QK = DA_HEADS * 2 * DA_DH
DA_WIDTH = DA_HEADS * DA_DV
REL_BUCKETS = 32
REL_MAX_DIST = 128
HG_HEADS = 4
HG_DK = 128
HG_DV = 128
HG_QK = HG_HEADS * HG_DK
HG_WIDTH = HG_HEADS * HG_DV
IN_EVEN = 2 * DA_QK + DA_WIDTH + 2 * HG_QK + 2 * HG_WIDTH
OUT_EVEN = DA_WIDTH + HG_WIDTH
SGU_CHUNK = 128
C_HALF = 2 * D_MODEL
C_GROUPS = 4
N_GROUPS = 4
EXP_PER_GROUP = 8
N_EXPERTS = N_GROUPS * EXP_PER_GROUP
TOP_K = 2
D_EXPERT = 512
MOE_BLOCK = 256

kernel_name = "hybrid_stream_diffattn_hgrn2_sgu_hmoe_step"


def rms_norm(x, g):
    xf = x.astype(jnp.float32)
    y = xf * lax.rsqrt(jnp.mean(xf * xf, axis=-1, keepdims=True) + EPS)
    return (y * g.astype(jnp.float32)).astype(x.dtype)


def rel_bucket(rel):
    half = REL_BUCKETS // 2
    max_exact = half // 2
    ret = (rel > 0).astype(jnp.int32) * half
    n = jnp.abs(rel)
    nf = jnp.maximum(n, 1).astype(jnp.float32)
    large = max_exact + (jnp.log(nf / max_exact) / math.log(REL_MAX_DIST / max_exact)
                         * (half - max_exact)).astype(jnp.int32)
    large = jnp.minimum(large, half - 1)
    return ret + jnp.where(n < max_exact, n, large)


def diff_attn_block(q, q_pos, k, v, k_pos, rel_bias, lam):
    s = jnp.einsum("bqhcd,bkhcd->bhcqk", q, k).astype(jnp.float32)
    bias = rel_bias[rel_bucket(k_pos[None, :] - q_pos[:, None])]
    s = s + jnp.transpose(bias, (2, 0, 1)).astype(jnp.float32)[None, :, None]
    visible = (k_pos[None, :] // CHUNK) <= (q_pos[:, None] // CHUNK)
    s = jnp.where(visible, s, -jnp.inf)
    p = jax.nn.softmax(s, axis=-1)
    p = p[:, :, 0] - lam * p[:, :, 1]
    return jnp.einsum("bhqk,bkhv->bqhv", p.astype(v.dtype), v)


def diff_attention(q, q_pos, k, v, k_pos, rel_bias, lam):
    B, T = q.shape[:2]
    blk = min(QBLK, T)
    nb = T // blk
    qb = jnp.moveaxis(q.reshape(B, nb, blk, DA_HEADS, 2, DA_DH), 1, 0)
    pb = q_pos.reshape(nb, blk)
    ob = lax.map(lambda a: diff_attn_block(a[0], a[1], k, v, k_pos, rel_bias, lam), (qb, pb))
    return jnp.moveaxis(ob, 0, 1).reshape(B, T, DA_HEADS, DA_DV)


def hgrn2_scan(q, k, i, logf, s0):
    B, T, H, K = q.shape
    C = min(CHUNK, T)
    n = T // C

    def split(a):
        return jnp.moveaxis(a.reshape(B, n, C, *a.shape[2:]), 1, 0)

    causal = jnp.tril(jnp.ones((C, C), bool))[None, :, :, None, None]

    def step(S, inp):
        qc, kc, ic, lc = inp
        b = jnp.cumsum(lc, axis=1)
        inter = jnp.einsum("bthk,bhkv->bthv", qc * jnp.exp(b), S)
        dec = jnp.exp(jnp.where(causal, b[:, :, None] - b[:, None], -jnp.inf))
        att = jnp.einsum("bthk,bshk,btshk->bhts", qc, kc, dec)
        out = inter + jnp.einsum("bhts,bshv->bthv", att, ic)
        bl = b[:, -1]
        S = jnp.exp(bl)[..., None] * S + jnp.einsum("bshk,bshv->bhkv", kc * jnp.exp(bl[:, None] - b), ic)
        return S, out

    S, out = lax.scan(step, s0, (split(q), split(k), split(i), split(logf)))
    return S, jnp.moveaxis(out, 0, 1).reshape(B, T, H, i.shape[-1])


def even_mixer(xn, q_pos, cache_k, cache_v, state, rel_bias, lb, lam_init,
               w_in, w_out, q_gain, k_gain, lq1, lk1, lq2, lk2, sub_gain, hg_gain):
    B, T, _ = xn.shape
    cuts = np.cumsum([DA_QK, DA_QK, DA_WIDTH, HG_QK, HG_QK, HG_WIDTH]).tolist()
    qa, ka, va, qb, fb, ib, gb = jnp.split(xn @ w_in, cuts, axis=-1)
    qa = rms_norm(qa.reshape(B, T, DA_HEADS, 2, DA_DH), q_gain) * (DA_DH ** -0.5)
    ka = rms_norm(ka.reshape(B, T, DA_HEADS, 2, DA_DH), k_gain)
    va = va.reshape(B, T, DA_HEADS, DA_DV)
    if cache_k is None:
        k_all, v_all, k_pos = ka, va, q_pos
    else:
        k_all = jnp.concatenate([cache_k.astype(ka.dtype), ka], axis=1)
        v_all = jnp.concatenate([cache_v.astype(va.dtype), va], axis=1)
        k_pos = jnp.arange(cache_k.shape[1] + T, dtype=jnp.int32)
    f32 = jnp.float32
    lam = (jnp.exp(jnp.sum(lq1.astype(f32) * lk1.astype(f32)))
           - jnp.exp(jnp.sum(lq2.astype(f32) * lk2.astype(f32))) + lam_init)
    oa = diff_attention(qa, q_pos, k_all, v_all, k_pos, rel_bias, lam)
    oa = rms_norm(oa, sub_gain) * (1.0 - lam_init)
    lbh = lb.reshape(HG_HEADS, HG_DK)
    zf = fb.reshape(B, T, HG_HEADS, HG_DK).astype(f32)
    logf = jnp.log(lbh + (1.0 - lbh) * jax.nn.sigmoid(zf))
    kb = (1.0 - lbh) * jax.nn.sigmoid(-zf)
    qh = jax.nn.silu(qb.reshape(B, T, HG_HEADS, HG_DK).astype(f32))
    ih = ib.reshape(B, T, HG_HEADS, HG_DV).astype(f32)
    s0 = jnp.zeros((B, HG_HEADS, HG_DK, HG_DV), f32) if state is None else state.astype(f32)
    s_new, ob = hgrn2_scan(qh, kb, ih, logf, s0)
    ob = rms_norm(ob.astype(xn.dtype), hg_gain) * jax.nn.silu(gb.reshape(B, T, HG_HEADS, HG_DV))
    o = jnp.concatenate([oa.reshape(B, T, DA_WIDTH), ob.reshape(B, T, HG_WIDTH)], axis=-1)
    return o @ w_out, ka, va, s_new.astype(xn.dtype)


def odd_mixer(xn, w_in, v_gain, w_sp, b_sp, w_out):
    B, T, _ = xn.shape
    u, v = jnp.split(jax.nn.gelu(xn @ w_in), 2, axis=-1)
    v = rms_norm(v, v_gain)
    L = min(SGU_CHUNK, T)
    n = T // L
    w = jnp.where(jnp.tril(jnp.ones((L, L), bool))[None], w_sp[:, :L, :L], 0.0)
    vg = v.reshape(B, n, L, C_GROUPS, C_HALF // C_GROUPS)
    s = (jnp.einsum("gts,bnsgc->bntgc", w.astype(v.dtype), vg)
         + jnp.transpose(b_sp[:, :L]).astype(v.dtype)[None, None, :, :, None])
    y = (u * s.reshape(B, T, C_HALF)) @ w_out
    return y, v


def hier_moe(x, wg, bg, we, be, w_gate, w_up, w_down):
    B, T, D = x.shape
    xt = x.reshape(-1, D)
    N = xt.shape[0]
    xf = xt.astype(jnp.float32)
    lg = xf @ wg.astype(jnp.float32) + bg.astype(jnp.float32)
    grp = jnp.argmax(lg, axis=-1)
    gate_g = jnp.take_along_axis(jax.nn.softmax(lg, axis=-1), grp[:, None], axis=1)
    le = jnp.einsum("nd,gde->nge", xf, we.astype(jnp.float32)) + be.astype(jnp.float32)
    le = jnp.take_along_axis(le, grp[:, None, None], axis=1)[:, 0]
    top_v, top_i = lax.top_k(le, TOP_K)
    gates = gate_g * jax.nn.softmax(top_v, axis=-1)
    expert = grp[:, None] * EXP_PER_GROUP + top_i
    A = N * TOP_K
    e_flat = expert.reshape(A)
    order = jnp.argsort(e_flat)
    e_sorted = e_flat[order]
    tok_sorted = order // TOP_K
    g_sorted = gates.reshape(A)[order]
    counts = jnp.bincount(e_flat, length=N_EXPERTS)
    starts = jnp.cumsum(counts) - counts
    padded = (counts + MOE_BLOCK - 1) // MOE_BLOCK * MOE_BLOCK
    pend = jnp.cumsum(padded)
    pstart = pend - padded
    dest = pstart[e_sorted] + jnp.arange(A) - starts[e_sorted]
    n_blocks = -(-A // MOE_BLOCK) + N_EXPERTS
    buf = jnp.zeros((n_blocks * MOE_BLOCK, D), x.dtype).at[dest].set(xt[tok_sorted])
    blk_expert = jnp.minimum(
        jnp.searchsorted(pend, jnp.arange(n_blocks) * MOE_BLOCK, side="right"), N_EXPERTS - 1)

    def expert_block(args):
        xb, e = args
        h = jax.nn.silu(xb @ w_gate[e]) * (xb @ w_up[e])
        return h @ w_down[e]

    yb = lax.map(expert_block, (buf.reshape(n_blocks, MOE_BLOCK, D), blk_expert))
    rows = yb.reshape(-1, D)[dest] * g_sorted[:, None].astype(x.dtype)
    y = jnp.zeros_like(xt).at[tok_sorted].add(rows)
    return y.reshape(B, T, D)


def setup_inputs(seed: int = 0) -> dict:
    key = jax.random.key(seed)
    ks = jax.random.split(key, 32)

    def nrm(k, shape, s):
        return jax.random.normal(k, shape, jnp.float32) * s

    def gain(k, shape):
        return 1.0 + 0.1 * jax.random.normal(k, shape, jnp.float32)

    return {
        "x_prompt": nrm(ks[0], (BATCH, SEQ, D_MODEL), 1.0),
        "x_sample": nrm(ks[1], (DEC_BATCH, DEC_SEQ, D_MODEL), 1.0),
        "cache_attn_k": nrm(ks[2], (N_EVEN, DEC_BATCH, PAST_LEN, DA_HEADS, 2, DA_DH), 1.0),
        "cache_attn_v": nrm(ks[3], (N_EVEN, DEC_BATCH, PAST_LEN, DA_HEADS, DA_DV), 1.0),
        "state_hgrn": nrm(ks[4], (N_EVEN, DEC_BATCH, HG_HEADS, HG_DK, HG_DV), 0.5),
        "rel_bias": nrm(ks[5], (REL_BUCKETS, DA_HEADS), 0.5),
        "norm_mix": gain(ks[6], (DEPTH, D_MODEL)),
        "norm_ffn": gain(ks[7], (DEPTH, D_MODEL)),
        "w_in_even": nrm(ks[8], (N_EVEN, D_MODEL, IN_EVEN), D_MODEL ** -0.5),
        "w_out_even": nrm(ks[9], (N_EVEN, OUT_EVEN, D_MODEL), OUT_EVEN ** -0.5),
        "q_norm_gain": gain(ks[10], (N_EVEN, DA_DH)),
        "k_norm_gain": gain(ks[11], (N_EVEN, DA_DH)),
        "lam_q1": nrm(ks[12], (N_EVEN, DA_DH), 0.1),
        "lam_k1": nrm(ks[13], (N_EVEN, DA_DH), 0.1),
        "lam_q2": nrm(ks[14], (N_EVEN, DA_DH), 0.1),
        "lam_k2": nrm(ks[15], (N_EVEN, DA_DH), 0.1),
        "da_out_gain": gain(ks[16], (N_EVEN, DA_DV)),
        "hgrn_lb_logits": nrm(ks[17], (N_EVEN + 1, HG_QK), 0.5),
        "hgrn_out_gain": gain(ks[18], (N_EVEN, HG_DV)),
        "w_in_odd": nrm(ks[19], (N_ODD, D_MODEL, 2 * C_HALF), D_MODEL ** -0.5),
        "sgu_v_gain": gain(ks[20], (N_ODD, C_HALF)),
        "sgu_w": nrm(ks[21], (N_ODD, C_GROUPS, SGU_CHUNK, SGU_CHUNK), 0.5 * SGU_CHUNK ** -0.5),
        "sgu_b": gain(ks[22], (N_ODD, C_GROUPS, SGU_CHUNK)),
        "w_out_odd": nrm(ks[23], (N_ODD, C_HALF, D_MODEL), C_HALF ** -0.5),
        "router_group_w": nrm(ks[24], (DEPTH, D_MODEL, N_GROUPS), D_MODEL ** -0.5),
        "router_group_b": nrm(ks[25], (DEPTH, N_GROUPS), 0.01),
        "router_expert_w": nrm(ks[26], (DEPTH, N_GROUPS, D_MODEL, EXP_PER_GROUP), D_MODEL ** -0.5),
        "router_expert_b": nrm(ks[27], (DEPTH, N_GROUPS, EXP_PER_GROUP), 0.01),
        "expert_w_gate": nrm(ks[28], (DEPTH, N_EXPERTS, D_MODEL, D_EXPERT), D_MODEL ** -0.5),
        "expert_w_up": nrm(ks[29], (DEPTH, N_EXPERTS, D_MODEL, D_EXPERT), D_MODEL ** -0.5),
        "expert_w_down": nrm(ks[30], (DEPTH, N_EXPERTS, D_EXPERT, D_MODEL), D_EXPERT ** -0.5),
    }


def reference(x_prompt, x_sample, cache_attn_k, cache_attn_v, state_hgrn, rel_bias,
              norm_mix, norm_ffn, w_in_even, w_out_even, q_norm_gain, k_norm_gain,
              lam_q1, lam_k1, lam_q2, lam_k2, da_out_gain, hgrn_lb_logits, hgrn_out_gain,
              w_in_odd, sgu_v_gain, sgu_w, sgu_b, w_out_odd,
              router_group_w, router_group_b, router_expert_w, router_expert_b,
              expert_w_gate, expert_w_up, expert_w_down):
    pos_p = jnp.arange(x_prompt.shape[1], dtype=jnp.int32)
    pos_s = cache_attn_k.shape[2] + jnp.arange(x_sample.shape[1], dtype=jnp.int32)
    lb_all = jnp.cumsum(jax.nn.softmax(hgrn_lb_logits.astype(jnp.float32), axis=0), axis=0)
    kp_l, vp_l, ks_l, vs_l, sp_l, ss_l, sgu_l = [], [], [], [], [], [], []
    xp, xs = x_prompt, x_sample
    for layer in range(DEPTH):
        j = layer // 2
        if layer % 2 == 0:
            lam_init = 0.8 - 0.6 * math.exp(-0.3 * layer)
            prm = (w_in_even[j], w_out_even[j], q_norm_gain[j], k_norm_gain[j],
                   lam_q1[j], lam_k1[j], lam_q2[j], lam_k2[j], da_out_gain[j], hgrn_out_gain[j])
            mp, kp, vp, sp = even_mixer(rms_norm(xp, norm_mix[layer]), pos_p, None, None, None,
                                        rel_bias, lb_all[j], lam_init, *prm)
            ms, kn, vn, sn = even_mixer(rms_norm(xs, norm_mix[layer]), pos_s, cache_attn_k[j],
                                        cache_attn_v[j], state_hgrn[j], rel_bias, lb_all[j],
                                        lam_init, *prm)
            kp_l.append(kp); vp_l.append(vp); sp_l.append(sp)
            ks_l.append(kn); vs_l.append(vn); ss_l.append(sn)
        else:
            prm = (w_in_odd[j], sgu_v_gain[j], sgu_w[j], sgu_b[j], w_out_odd[j])
            mp, _ = odd_mixer(rms_norm(xp, norm_mix[layer]), *prm)
            ms, vsg = odd_mixer(rms_norm(xs, norm_mix[layer]), *prm)
            sgu_l.append(vsg)
        xp = xp + mp
        xs = xs + ms
        moe = (router_group_w[layer], router_group_b[layer], router_expert_w[layer],
               router_expert_b[layer], expert_w_gate[layer], expert_w_up[layer], expert_w_down[layer])
        xp = xp + hier_moe(rms_norm(xp, norm_ffn[layer]), *moe)
        xs = xs + hier_moe(rms_norm(xs, norm_ffn[layer]), *moe)
    return (xp, xs, jnp.stack(kp_l), jnp.stack(vp_l), jnp.stack(ks_l), jnp.stack(vs_l),
            jnp.stack(sp_l), jnp.stack(ss_l), jnp.stack(sgu_l))
```

```python
import functools
import math

import jax
import jax.numpy as jnp
from jax import lax
from jax.experimental import pallas as pl
from jax.experimental.pallas import tpu as pltpu

F32 = jnp.float32
BF16 = jnp.bfloat16
I32 = jnp.int32

EPS = 1e-6
CHUNK = 64
SGU_CHUNK = 128
REL_BUCKETS = 32
REL_MAX_DIST = 128
TOP_K = 2
MOE_BLOCK = 256
HG_SUB = 16

LANES = 128
SUBLANES = 8
VMEM_LIMIT = 56 * 1024 * 1024

NT_DIMS = (((1,), (1,)), ((), ()))
TN_DIMS = (((0,), (0,)), ((), ()))


def _params(*sem):
    return pltpu.CompilerParams(dimension_semantics=sem, vmem_limit_bytes=VMEM_LIMIT)


def _const_spec(shape):
    nd = len(shape)
    return pl.BlockSpec(shape, lambda *_: (0,) * nd, pipeline_mode=pl.Buffered(1))


def _sigmoid(x):
    return 1.0 / (1.0 + jnp.exp(-x))


def _rms(x, g):
    return x * lax.rsqrt(jnp.mean(x * x, axis=-1, keepdims=True) + EPS) * g


def _dot(a, b):
    return jnp.dot(a, b, preferred_element_type=F32)


def rel_bucket(rel):
    half = REL_BUCKETS // 2
    max_exact = half // 2
    ret = (rel > 0).astype(I32) * half
    n = jnp.abs(rel)
    nf = jnp.maximum(n, 1).astype(F32)
    large = max_exact + (jnp.log(nf / max_exact) / math.log(REL_MAX_DIST / max_exact)
                         * (half - max_exact)).astype(I32)
    large = jnp.minimum(large, half - 1)
    return ret + jnp.where(n < max_exact, n, large)


def _in_even_kernel(x_ref, g_ref, w_ref, pm_ref, qg_ref, kg_ref, lb_ref,
                    q_ref, k_ref, v_ref, qh_ref, kb_ref, lf_ref, ih_ref, gs_ref, *, width, q_scale):
    xn = _rms(x_ref[...], g_ref[...]).astype(BF16)

    def proj(c):
        return _dot(xn, w_ref[:, c * width:(c + 1) * width])

    def group_norm(y, gain):
        ms = _dot((y * y).astype(BF16), pm_ref[...])
        return y * lax.rsqrt(ms + EPS) * gain

    q_ref[...] = group_norm(proj(0), qg_ref[...]) * q_scale
    k_ref[...] = group_norm(proj(1), kg_ref[...])
    v_ref[...] = proj(2)
    yq = proj(3)
    qh_ref[...] = yq * _sigmoid(yq)
    zf = proj(4)
    lb = lb_ref[...]
    lf_ref[...] = jnp.log(lb + (1.0 - lb) * _sigmoid(zf))
    kb_ref[...] = (1.0 - lb) * _sigmoid(-zf)
    ih_ref[...] = proj(5)
    yg = proj(6)
    gs_ref[...] = yg * _sigmoid(yg)


def _in_even(x, g_mix, w_bf, pm, qg, kg, lb, *, width, q_scale, tm):
    n, d = x.shape
    assert n % tm == 0
    row = lambda i: (i, 0)
    out = jax.ShapeDtypeStruct((n, width), F32)
    return pl.pallas_call(
        functools.partial(_in_even_kernel, width=width, q_scale=q_scale),
        grid=(n // tm,),
        in_specs=[pl.BlockSpec((tm, d), row), _const_spec((1, d)), _const_spec(w_bf.shape),
                  _const_spec(pm.shape), _const_spec((1, width)), _const_spec((1, width)),
                  _const_spec((1, width))],
        out_specs=[pl.BlockSpec((tm, width), row)] * 8,
        out_shape=[out] * 8,
        compiler_params=_params("arbitrary"),
        name="in_even",
    )(x, g_mix, w_bf, pm, qg, kg, lb)


def _bias_from_buckets(bk, rb_ref, h):
    b = jnp.zeros(bk.shape, F32)
    for u in range(REL_BUCKETS):
        b = jnp.where(bk == u, rb_ref[u, h], b)
    return jnp.where(bk < 0, -jnp.inf, b)


def _lam(lam_ref, lam_init):
    r = lam_ref[...]
    s1 = jnp.sum(r[0:1] * r[1:2], axis=1, keepdims=True)
    s2 = jnp.sum(r[2:3] * r[3:4], axis=1, keepdims=True)
    return jnp.exp(s1) - jnp.exp(s2) + lam_init


def _split_components(q, dh):
    lane = lax.broadcasted_iota(I32, q.shape, 1)
    q0 = jnp.where(lane < dh, q, 0.0)
    q1 = jnp.where(lane >= dh, q, 0.0)
    return jnp.concatenate([q0, q1], axis=0).astype(BF16)


def _attn_prompt_kernel(rb_ref, far_ref, lam_ref, bk_ref, sg_ref, q_ref, k_ref, v_ref, o_ref,
                        kb_s, vb_s, bias_s, m_s, l_s, acc_s, *, qt, dh, lam_init):
    h = pl.program_id(1)
    i = pl.program_id(2)

    @pl.when(i == 0)
    def _():
        kb_s[...] = k_ref[...].astype(BF16)
        vb_s[...] = v_ref[...].astype(BF16)
        for d in range(2):
            b = _bias_from_buckets(bk_ref[d], rb_ref, h)
            bias_s[d] = jnp.concatenate([b, b], axis=0)

    qz = _split_components(q_ref[...], dh)

    def scores(j):
        kj = kb_s[pl.ds(pl.multiple_of(j * qt, qt), qt), :]
        return lax.dot_general(qz, kj, NT_DIMS, preferred_element_type=F32)

    def pv(p, j):
        vj = vb_s[pl.ds(pl.multiple_of(j * qt, qt), qt), :]
        return _dot(p.astype(BF16), vj)

    def update(s, j):
        m_old = m_s[...]
        m_new = jnp.maximum(m_old, jnp.max(s, axis=1, keepdims=True))
        a = jnp.exp(m_old - m_new)
        p = jnp.exp(s - m_new)
        l_s[...] = a * l_s[...] + jnp.sum(p, axis=1, keepdims=True)
        acc_s[...] = a * acc_s[...] + pv(p, j)
        m_s[...] = m_new

    s = scores(i) + bias_s[0]
    m0 = jnp.max(s, axis=1, keepdims=True)
    p = jnp.exp(s - m0)
    m_s[...] = m0
    l_s[...] = jnp.sum(p, axis=1, keepdims=True)
    acc_s[...] = pv(p, i)

    @pl.when(i >= 1)
    def _():
        update(scores(i - 1) + bias_s[1], i - 1)

    far = rb_ref[far_ref[0], h]

    def far_body(j, c):
        update(scores(j) + far, j)
        return c

    lax.fori_loop(0, i - 1, far_body, 0)

    o = acc_s[...] / l_s[...]
    out = o[:qt] - _lam(lam_ref, lam_init) * o[qt:]
    o_ref[...] = _rms(out, sg_ref[...]) * (1.0 - lam_init)


def _attn_prompt(q, k, v, rel_bias, lam4, sub_gain, *, batch, seq, heads, dh, lam_init, qt):
    n, w = q.shape
    dv = w // heads
    assert dv == 2 * dh and seq % qt == 0 and qt % CHUNK == 0
    nq = seq // qt
    qi = jnp.arange(qt, dtype=I32)[:, None]
    kj = jnp.arange(qt, dtype=I32)[None, :]
    bk0 = jnp.where((kj // CHUNK) <= (qi // CHUNK), rel_bucket(kj - qi), -1)
    bk1 = rel_bucket(kj - qi - qt)
    bk = jnp.stack([bk0, bk1]).astype(I32)
    assert qt + 1 >= REL_MAX_DIST
    far = rel_bucket(jnp.full((1,), -(qt + 1), I32))
    smem = pl.BlockSpec(memory_space=pltpu.SMEM)
    return pl.pallas_call(
        functools.partial(_attn_prompt_kernel, qt=qt, dh=dh, lam_init=lam_init),
        grid=(batch, heads, nq),
        in_specs=[smem, smem, _const_spec(lam4.shape), _const_spec(bk.shape), _const_spec((1, dv)),
                  pl.BlockSpec((qt, dv), lambda b, h, i: (b * nq + i, h)),
                  pl.BlockSpec((seq, dv), lambda b, h, i: (b, h)),
                  pl.BlockSpec((seq, dv), lambda b, h, i: (b, h))],
        out_specs=pl.BlockSpec((qt, dv), lambda b, h, i: (b * nq + i, h)),
        out_shape=jax.ShapeDtypeStruct((n, w), F32),
        scratch_shapes=[pltpu.VMEM((seq, dv), BF16), pltpu.VMEM((seq, dv), BF16),
                        pltpu.VMEM((2, 2 * qt, qt), F32), pltpu.VMEM((2 * qt, 1), F32),
                        pltpu.VMEM((2 * qt, 1), F32), pltpu.VMEM((2 * qt, dv), F32)],
        compiler_params=_params("arbitrary", "arbitrary", "arbitrary"),
        name="attn_prompt",
    )(rel_bias, far, lam4, bk, sub_gain, q, k, v)


def _attn_sample_kernel(rb_ref, lam_ref, bkc_ref, bkn_ref, sg_ref, q_ref, kc_ref, vc_ref, kn_ref, vn_ref,
                        o_ref, *, t, dh, lam_init):
    h = pl.program_id(1)
    qz = _split_components(q_ref[...], dh)
    bc = _bias_from_buckets(bkc_ref[...], rb_ref, h)
    bn = _bias_from_buckets(bkn_ref[...], rb_ref, h)
    sc = lax.dot_general(qz, kc_ref[0].astype(BF16), NT_DIMS, preferred_element_type=F32)
    sn = lax.dot_general(qz, kn_ref[...].astype(BF16), NT_DIMS, preferred_element_type=F32)
    sc = sc + jnp.concatenate([bc, bc], axis=0)
    sn = sn + jnp.concatenate([bn, bn], axis=0)
    m = jnp.maximum(jnp.max(sc, axis=1, keepdims=True), jnp.max(sn, axis=1, keepdims=True))
    pc = jnp.exp(sc - m)
    pn = jnp.exp(sn - m)
    l = jnp.sum(pc, axis=1, keepdims=True) + jnp.sum(pn, axis=1, keepdims=True)
    acc = _dot(pc.astype(BF16), vc_ref[0].astype(BF16)) + _dot(pn.astype(BF16), vn_ref[...].astype(BF16))
    o = acc / l
    out = o[:t] - _lam(lam_ref, lam_init) * o[t:]
    o_ref[...] = _rms(out, sg_ref[...]) * (1.0 - lam_init)


def _attn_sample(q, k_new, v_new, cache_k, cache_v, rel_bias, lam4, sub_gain, *, batch, t, heads, dh,
                 lam_init):
    n, w = q.shape
    dv = w // heads
    past = cache_k.shape[1]
    assert past % CHUNK == 0 and t <= CHUNK
    qpos = past + jnp.arange(t, dtype=I32)[:, None]
    bkc = rel_bucket(jnp.arange(past, dtype=I32)[None, :] - qpos).astype(I32)
    bkn = rel_bucket(past + jnp.arange(t, dtype=I32)[None, :] - qpos).astype(I32)
    smem = pl.BlockSpec(memory_space=pltpu.SMEM)
    new = pl.BlockSpec((t, dv), lambda b, h: (b, h))
    old = pl.BlockSpec((1, past, dv), lambda b, h: (b, 0, h))
    return pl.pallas_call(
        functools.partial(_attn_sample_kernel, t=t, dh=dh, lam_init=lam_init),
        grid=(batch, heads),
        in_specs=[smem, _const_spec(lam4.shape), _const_spec(bkc.shape), _const_spec(bkn.shape),
                  _const_spec((1, dv)), new, old, old, new, new],
        out_specs=new,
        out_shape=jax.ShapeDtypeStruct((n, w), F32),
        compiler_params=_params("arbitrary", "arbitrary"),
        name="attn_sample",
    )(rel_bias, lam4, bkc, bkn, sub_gain, q, cache_k, cache_v, k_new, v_new)


def _cumsum_rows(x):
    c = x.shape[0]
    row = lax.broadcasted_iota(I32, x.shape, 0)
    s = 1
    while s < c:
        x = x + jnp.where(row >= s, pltpu.roll(x, s, axis=0), 0.0)
        s *= 2
    return x


def _hgrn_kernel(qh_ref, kb_ref, lf_ref, ih_ref, s0_ref, hg_ref, ob_ref, sf_ref, st_s, *,
                 heads, dk, dv, c, nsb):
    t = pl.program_id(1)

    @pl.when(t == 0)
    def _():
        for h in range(heads):
            st_s[h] = s0_ref[0, h].T

    tb = qh_ref.shape[0]
    row = lax.broadcasted_iota(I32, (c, c), 0)
    col = lax.broadcasted_iota(I32, (c, c), 1)
    causal = col <= row

    def chunk(ci, carry):
        r0 = pl.multiple_of(ci * c, c)
        for h in range(heads):
            rows = pl.ds(r0, c)
            q = qh_ref[rows, h * dk:(h + 1) * dk]
            k = kb_ref[rows, h * dk:(h + 1) * dk]
            v = ih_ref[rows, h * dv:(h + 1) * dv]
            b = _cumsum_rows(lf_ref[rows, h * dk:(h + 1) * dk])
            bl = b[c - 1:c]
            st = st_s[h]
            inter = lax.dot_general((q * jnp.exp(b)).astype(BF16), st.astype(BF16), NT_DIMS,
                                    preferred_element_type=F32)
            qs, ks = [], []
            for j in range(nsb):
                ref = b[j * HG_SUB + HG_SUB // 2:j * HG_SUB + HG_SUB // 2 + 1]
                qs.append(q * jnp.exp(b - ref))
                sub = slice(j * HG_SUB, (j + 1) * HG_SUB)
                ks.append(k[sub] * jnp.exp(ref - b[sub]))
            a_full = lax.dot_general(jnp.concatenate(qs, axis=0).astype(BF16),
                                     jnp.concatenate(ks, axis=0).astype(BF16), NT_DIMS,
                                     preferred_element_type=F32)
            att = jnp.zeros((c, c), F32)
            for j in range(nsb):
                att = jnp.where(col >= j * HG_SUB, a_full[j * c:(j + 1) * c], att)
            att = jnp.where(causal, att, 0.0)
            out = inter + _dot(att.astype(BF16), v.astype(BF16))
            ob_ref[rows, h * dv:(h + 1) * dv] = _rms(out, hg_ref[...])
            kdec = (k * jnp.exp(bl - b)).astype(BF16)
            st_s[h] = jnp.exp(bl) * st + lax.dot_general(v.astype(BF16), kdec, TN_DIMS,
                                                         preferred_element_type=F32)
        return carry

    lax.fori_loop(0, tb // c, chunk, 0)

    @pl.when(t == pl.num_programs(1) - 1)
    def _():
        for h in range(heads):
            sf_ref[0, h] = st_s[h].T


def _hgrn(qh, kb, lf, ih, s0, hg_gain, *, batch, seq, heads, dk, dv, tb):
    n = qh.shape[0]
    c = min(CHUNK, seq)
    assert seq % tb == 0 and tb % c == 0 and c % HG_SUB == 0
    nt = seq // tb
    blk = lambda w: pl.BlockSpec((tb, w), lambda b, t: (b * nt + t, 0))
    st = pl.BlockSpec((1, heads, dk, dv), lambda b, t: (b, 0, 0, 0))
    return pl.pallas_call(
        functools.partial(_hgrn_kernel, heads=heads, dk=dk, dv=dv, c=c, nsb=c // HG_SUB),
        grid=(batch, nt),
        in_specs=[blk(heads * dk), blk(heads * dk), blk(heads * dk), blk(heads * dv), st,
                  _const_spec((1, dv))],
        out_specs=[blk(heads * dv), st],
        out_shape=[jax.ShapeDtypeStruct((n, heads * dv), F32),
                   jax.ShapeDtypeStruct((batch, heads, dk, dv), F32)],
        scratch_shapes=[pltpu.VMEM((heads, dv, dk), F32)],
        compiler_params=_params("arbitrary", "arbitrary"),
        name="hgrn",
    )(qh, kb, lf, ih, s0, hg_gain)


def _ffn_prologue(x1, gf_ref, wrh_ref, wrl_ref, rbias_ref, x1_ref, xn_ref, lg_ref):
    x1_ref[...] = x1
    xn = _rms(x1, gf_ref[...])
    xn_ref[...] = xn
    hi = xn.astype(BF16)
    lo = (xn - hi.astype(F32)).astype(BF16)
    nt = functools.partial(lax.dot_general, dimension_numbers=NT_DIMS, preferred_element_type=F32)
    lg_ref[...] = nt(wrh_ref[...], hi) + nt(wrh_ref[...], lo) + nt(wrl_ref[...], hi) + rbias_ref[...]


def _out_even_kernel(oa_ref, ob_ref, gs_ref, x_ref, w_ref, gf_ref, wrh_ref, wrl_ref, rbias_ref,
                     x1_ref, xn_ref, lg_ref):
    o = jnp.concatenate([oa_ref[...], ob_ref[...] * gs_ref[...]], axis=1).astype(BF16)
    x1 = x_ref[...] + _dot(o, w_ref[...])
    _ffn_prologue(x1, gf_ref, wrh_ref, wrl_ref, rbias_ref, x1_ref, xn_ref, lg_ref)


def _out_even(oa, ob, gs, x, w_bf, g_ffn, wrh, wrl, rbias, *, tm):
    n, d = x.shape
    w = oa.shape[1]
    nr = wrh.shape[0]
    row = lambda i: (i, 0)
    return pl.pallas_call(
        _out_even_kernel,
        grid=(n // tm,),
        in_specs=[pl.BlockSpec((tm, w), row), pl.BlockSpec((tm, w), row), pl.BlockSpec((tm, w), row),
                  pl.BlockSpec((tm, d), row), _const_spec(w_bf.shape), _const_spec((1, d)),
                  _const_spec(wrh.shape), _const_spec(wrl.shape), _const_spec(rbias.shape)],
        out_specs=[pl.BlockSpec((tm, d), row), pl.BlockSpec((tm, d), row),
                   pl.BlockSpec((nr, tm), lambda i: (0, i))],
        out_shape=[jax.ShapeDtypeStruct((n, d), F32), jax.ShapeDtypeStruct((n, d), F32),
                   jax.ShapeDtypeStruct((nr, n), F32)],
        compiler_params=_params("arbitrary"),
        name="out_even",
    )(oa, ob, gs, x, w_bf, g_ffn, wrh, wrl, rbias)


def _gelu(x):
    return 0.5 * x * (1.0 + jnp.tanh(math.sqrt(2.0 / math.pi) * (x + 0.044715 * (x * x * x))))


def _odd_kernel(x_ref, gm_ref, win_ref, vg_ref, wsp_ref, bsp_ref, wout_ref, gf_ref, wrh_ref, wrl_ref,
                rbias_ref, x1_ref, xn_ref, lg_ref, *rest, half, groups, l, emit_v):
    if emit_v:
        vn_ref, u_s, s_s = rest
    else:
        vn_ref = None
        u_s, vn_s, s_s = rest
    tm = x_ref.shape[0]
    x = x_ref[...]
    xn = _rms(x, gm_ref[...]).astype(BF16)
    cw = 512
    vbuf = vn_ref if emit_v else vn_s
    for cidx in range(half // cw):
        u_s[:, cidx * cw:(cidx + 1) * cw] = _gelu(_dot(xn, win_ref[:, cidx * cw:(cidx + 1) * cw]))
        vbuf[:, cidx * cw:(cidx + 1) * cw] = _gelu(
            _dot(xn, win_ref[:, half + cidx * cw:half + (cidx + 1) * cw]))
    vbuf[...] = _rms(vbuf[...], vg_ref[...])
    gw = half // groups
    row = lax.broadcasted_iota(I32, (l, l), 0)
    col = lax.broadcasted_iota(I32, (l, l), 1)
    for g in range(groups):
        wg = jnp.where(col <= row, wsp_ref[g], 0.0).astype(BF16)
        bg = bsp_ref[:, g:g + 1]
        for ci in range(tm // l):
            vv = vbuf[ci * l:(ci + 1) * l, g * gw:(g + 1) * gw].astype(BF16)
            s_s[ci * l:(ci + 1) * l, g * gw:(g + 1) * gw] = _dot(wg, vv) + bg
    y = _dot((u_s[...] * s_s[...]).astype(BF16), wout_ref[...])
    _ffn_prologue(x + y, gf_ref, wrh_ref, wrl_ref, rbias_ref, x1_ref, xn_ref, lg_ref)


def _odd_mixer(x, g_mix, win_bf, v_gain, wsp, bsp_t, wout_bf, g_ffn, wrh, wrl, rbias, *, l, tm, emit_v):
    n, d = x.shape
    half = wout_bf.shape[0]
    groups = wsp.shape[0]
    nr = wrh.shape[0]
    assert n % tm == 0 and tm % l == 0
    row = lambda i: (i, 0)
    out_specs = [pl.BlockSpec((tm, d), row), pl.BlockSpec((tm, d), row),
                 pl.BlockSpec((nr, tm), lambda i: (0, i))]
    out_shape = [jax.ShapeDtypeStruct((n, d), F32), jax.ShapeDtypeStruct((n, d), F32),
                 jax.ShapeDtypeStruct((nr, n), F32)]
    scratch = [pltpu.VMEM((tm, half), F32)]
    if emit_v:
        out_specs.append(pl.BlockSpec((tm, half), row))
        out_shape.append(jax.ShapeDtypeStruct((n, half), F32))
    else:
        scratch.append(pltpu.VMEM((tm, half), F32))
    scratch.append(pltpu.VMEM((tm, half), F32))
    return pl.pallas_call(
        functools.partial(_odd_kernel, half=half, groups=groups, l=l, emit_v=emit_v),
        grid=(n // tm,),
        in_specs=[pl.BlockSpec((tm, d), row), _const_spec((1, d)), _const_spec(win_bf.shape),
                  _const_spec((1, half)), _const_spec(wsp.shape), _const_spec(bsp_t.shape),
                  _const_spec(wout_bf.shape), _const_spec((1, d)), _const_spec(wrh.shape),
                  _const_spec(wrl.shape), _const_spec(rbias.shape)],
        out_specs=out_specs,
        out_shape=out_shape,
        scratch_shapes=scratch,
        compiler_params=_params("arbitrary"),
        name="odd_mixer",
    )(x, g_mix, win_bf, v_gain, wsp, bsp_t, wout_bf, g_ffn, wrh, wrl, rbias)


def _route_kernel(lg_ref, tri_ref, e_ref, g_ref, r_ref, cnt_ref, run_s, *, groups, epg):
    i = pl.program_id(0)

    @pl.when(i == 0)
    def _():
        run_s[...] = jnp.zeros(run_s.shape, F32)

    lg = lg_ref[...]
    tr = lg.shape[1]
    gl = [lg[g:g + 1] for g in range(groups)]
    m = functools.reduce(jnp.maximum, gl)
    grp = jnp.full((1, tr), groups - 1, I32)
    for g in range(groups - 2, -1, -1):
        grp = jnp.where(gl[g] == m, g, grp)
    gate_g = 1.0 / functools.reduce(lambda a, b: a + b, [jnp.exp(x - m) for x in gl])
    sel = lg[SUBLANES + (groups - 1) * epg:SUBLANES + groups * epg]
    for g in range(groups - 2, -1, -1):
        sel = jnp.where(grp == g, lg[SUBLANES + g * epg:SUBLANES + (g + 1) * epg], sel)
    sub = lax.broadcasted_iota(I32, sel.shape, 0)
    v1 = jnp.max(sel, axis=0, keepdims=True)
    i1 = jnp.min(jnp.where(sel == v1, sub, epg), axis=0, keepdims=True)
    sel2 = jnp.where(sub == i1, -jnp.inf, sel)
    v2 = jnp.max(sel2, axis=0, keepdims=True)
    i2 = jnp.min(jnp.where(sel2 == v2, sub, epg), axis=0, keepdims=True)
    tt = jnp.exp(v2 - v1)
    g1 = gate_g / (1.0 + tt)
    g2 = gate_g * tt / (1.0 + tt)
    e1 = grp * epg + i1
    e2 = grp * epg + i2
    ne = groups * epg
    eidx = lax.broadcasted_iota(I32, (ne, tr), 0)
    oh1 = eidx == e1
    oh2 = eidx == e2
    cnt = jnp.where(oh1, 1.0, 0.0) + jnp.where(oh2, 1.0, 0.0)
    before = run_s[:, 0:1] + _dot(cnt.astype(BF16), tri_ref[...])
    r1 = jnp.sum(jnp.where(oh1, before, 0.0), axis=0, keepdims=True)
    r2 = jnp.sum(jnp.where(oh2, before, 0.0), axis=0, keepdims=True)
    run_s[...] = run_s[...] + jnp.sum(cnt, axis=1, keepdims=True)
    rows = lax.broadcasted_iota(I32, (SUBLANES, tr), 0)
    e_ref[...] = jnp.where(rows == 0, e1, jnp.where(rows == 1, e2, 0))
    g_ref[...] = jnp.where(rows == 0, g1, jnp.where(rows == 1, g2, 0.0))
    r_ref[...] = jnp.where(rows == 0, r1, jnp.where(rows == 1, r2, 0.0)).astype(I32)
    cnt_ref[...] = run_s[...]


def _route(lgt, *, groups, epg, tr):
    nr, n = lgt.shape
    assert n % tr == 0 and nr == SUBLANES + groups * epg
    ne = groups * epg
    tri = (jnp.arange(tr)[:, None] < jnp.arange(tr)[None, :]).astype(BF16)
    tok = pl.BlockSpec((SUBLANES, tr), lambda i: (0, i))
    return pl.pallas_call(
        functools.partial(_route_kernel, groups=groups, epg=epg),
        grid=(n // tr,),
        in_specs=[pl.BlockSpec((nr, tr), lambda i: (0, i)), _const_spec((tr, tr))],
        out_specs=[tok, tok, tok, pl.BlockSpec((ne, LANES), lambda i: (0, 0))],
        out_shape=[jax.ShapeDtypeStruct((SUBLANES, n), I32), jax.ShapeDtypeStruct((SUBLANES, n), F32),
                   jax.ShapeDtypeStruct((SUBLANES, n), I32), jax.ShapeDtypeStruct((ne, LANES), F32)],
        scratch_shapes=[pltpu.VMEM((ne, LANES), F32)],
        compiler_params=_params("arbitrary"),
        name="route",
    )(lgt, tri)


def _row_copy(src, dst, sem, s, d):
    return pltpu.make_async_copy(src.at[pl.ds(s, 1)], dst.at[pl.ds(d, 1)], sem)


def _dispatch_kernel(dst_ref, x_ref, buf_in, buf_ref, sem):
    del buf_in
    tp = x_ref.shape[0]

    def issue(r, c):
        for kk in range(TOP_K):
            _row_copy(x_ref, buf_ref, sem, r, dst_ref[0, kk, r]).start()
        return c

    lax.fori_loop(0, tp, issue, 0)

    def drain(r, c):
        for kk in range(TOP_K):
            _row_copy(x_ref, buf_ref, sem, 0, 0).wait()
        return c

    lax.fori_loop(0, tp, drain, 0)


def _dispatch(dest3, xn, n_slots, *, tp):
    n, d = xn.shape
    buf = jnp.zeros((n_slots, d), F32)
    return pl.pallas_call(
        _dispatch_kernel,
        grid=(n // tp,),
        in_specs=[pl.BlockSpec((1, TOP_K, tp), lambda i: (i, 0, 0), memory_space=pltpu.SMEM),
                  pl.BlockSpec((tp, d), lambda i: (i, 0)),
                  pl.BlockSpec(memory_space=pl.ANY)],
        out_specs=pl.BlockSpec(memory_space=pl.ANY),
        out_shape=jax.ShapeDtypeStruct((n_slots, d), F32),
        scratch_shapes=[pltpu.SemaphoreType.DMA(())],
        input_output_aliases={2: 0},
        compiler_params=_params("arbitrary"),
        name="dispatch",
    )(dest3, xn, buf)


def _expert_kernel(be_ref, nu_ref, x_ref, wg_ref, wu_ref, wd_ref, o_ref, wg_s, wu_s, wd_s):
    b = pl.program_id(0)
    prev = be_ref[jnp.maximum(b - 1, 0)]

    @pl.when((b == 0) | (be_ref[b] != prev))
    def _():
        wg_s[...] = wg_ref[0].astype(BF16)
        wu_s[...] = wu_ref[0].astype(BF16)
        wd_s[...] = wd_ref[0].astype(BF16)

    @pl.when(b < nu_ref[0])
    def _():
        xb = x_ref[...].astype(BF16)
        gate = _dot(xb, wg_s[...])
        h = gate * _sigmoid(gate) * _dot(xb, wu_s[...])
        o_ref[...] = _dot(h.astype(BF16), wd_s[...])

    @pl.when(b >= nu_ref[0])
    def _():
        o_ref[...] = jnp.zeros(o_ref.shape, F32)


def _experts(blk_expert, n_used, buf, w_gate, w_up, w_down):
    n_slots, d = buf.shape
    de = w_gate.shape[2]
    n_blocks = n_slots // MOE_BLOCK
    rows = pl.BlockSpec((MOE_BLOCK, d), lambda b, be, nu: (b, 0))
    grid_spec = pltpu.PrefetchScalarGridSpec(
        num_scalar_prefetch=2,
        grid=(n_blocks,),
        in_specs=[rows,
                  pl.BlockSpec((1, d, de), lambda b, be, nu: (be[b], 0, 0)),
                  pl.BlockSpec((1, d, de), lambda b, be, nu: (be[b], 0, 0)),
                  pl.BlockSpec((1, de, d), lambda b, be, nu: (be[b], 0, 0))],
        out_specs=rows,
        scratch_shapes=[pltpu.VMEM((d, de), BF16), pltpu.VMEM((d, de), BF16), pltpu.VMEM((de, d), BF16)],
    )
    return pl.pallas_call(
        _expert_kernel,
        grid_spec=grid_spec,
        out_shape=jax.ShapeDtypeStruct((n_slots, d), F32),
        compiler_params=_params("arbitrary"),
        name="experts",
    )(blk_expert, n_used, buf, w_gate, w_up, w_down)


def _combine_kernel(dst_ref, gate_ref, x_ref, yb_ref, o_ref, rows_s, sem):
    tq = x_ref.shape[0]

    def issue(r, c):
        for kk in range(TOP_K):
            _row_copy(yb_ref, rows_s.at[kk], sem, dst_ref[0, kk, r], r).start()
        return c

    lax.fori_loop(0, tq, issue, 0)

    def drain(r, c):
        for kk in range(TOP_K):
            _row_copy(yb_ref, rows_s.at[kk], sem, 0, 0).wait()
        return c

    lax.fori_loop(0, tq, drain, 0)
    g = gate_ref[...]
    o_ref[...] = x_ref[...] + g[:, 0:1] * rows_s[0] + g[:, 1:2] * rows_s[1]


def _combine(dest3, gates_t, x1, yb, *, tq):
    n, d = x1.shape
    return pl.pallas_call(
        _combine_kernel,
        grid=(n // tq,),
        in_specs=[pl.BlockSpec((1, TOP_K, tq), lambda i: (i, 0, 0), memory_space=pltpu.SMEM),
                  pl.BlockSpec((tq, TOP_K), lambda i: (i, 0)),
                  pl.BlockSpec((tq, d), lambda i: (i, 0)),
                  pl.BlockSpec(memory_space=pl.ANY)],
        out_specs=pl.BlockSpec((tq, d), lambda i: (i, 0)),
        out_shape=jax.ShapeDtypeStruct((n, d), F32),
        scratch_shapes=[pltpu.VMEM((TOP_K, tq, d), F32), pltpu.SemaphoreType.DMA(())],
        compiler_params=_params("arbitrary"),
        name="combine",
    )(dest3, gates_t, x1, yb)


def _moe(x1, xn, lgt, w_gate, w_up, w_down, *, groups, epg):
    n, d = x1.shape
    ne = groups * epg
    tile = min(256, n)
    e8, g8, r8, cnt = _route(lgt, groups=groups, epg=epg, tr=min(512, n))
    counts = cnt[:, 0].astype(I32)
    padded = (counts + MOE_BLOCK - 1) // MOE_BLOCK * MOE_BLOCK
    pend = jnp.cumsum(padded)
    pstart = pend - padded
    n_blocks = -(-(n * TOP_K) // MOE_BLOCK) + ne
    dest = pstart[e8[:TOP_K]] + r8[:TOP_K]
    dest3 = dest.reshape(TOP_K, n // tile, tile).transpose(1, 0, 2)
    blk_expert = jnp.minimum(
        jnp.searchsorted(pend, jnp.arange(n_blocks, dtype=I32) * MOE_BLOCK, side="right"), ne - 1
    ).astype(I32)
    n_used = (pend[-1:] // MOE_BLOCK).astype(I32)
    buf = _dispatch(dest3, xn, n_blocks * MOE_BLOCK, tp=tile)
    yb = _experts(blk_expert, n_used, buf, w_gate, w_up, w_down)
    return _combine(dest3, g8[:TOP_K].T, x1, yb, tq=tile)


def _router_weights(wg, bg, we, be):
    d, groups = wg.shape
    epg = we.shape[2]
    assert groups <= SUBLANES and epg == SUBLANES
    pad = jnp.zeros((SUBLANES - groups, d), F32)
    wr = jnp.concatenate([wg.T, pad, we.transpose(0, 2, 1).reshape(groups * epg, d)], axis=0)
    rb = jnp.concatenate([bg, jnp.zeros((SUBLANES - groups,), F32), be.reshape(-1)])[:, None]
    hi = wr.astype(BF16)
    lo = (wr - hi.astype(F32)).astype(BF16)
    return hi, lo, rb, groups, epg


def kernel(x_prompt, x_sample, cache_attn_k, cache_attn_v, state_hgrn, rel_bias, norm_mix, norm_ffn,
           w_in_even, w_out_even, q_norm_gain, k_norm_gain, lam_q1, lam_k1, lam_q2, lam_k2, da_out_gain,
           hgrn_lb_logits, hgrn_out_gain, w_in_odd, sgu_v_gain, sgu_w, sgu_b, w_out_odd,
           router_group_w, router_group_b, router_expert_w, router_expert_b,
           expert_w_gate, expert_w_up, expert_w_down):
    bp, tp, d = x_prompt.shape
    bs, ts, _ = x_sample.shape
    depth = norm_mix.shape[0]
    _, _, past, da_heads, _, da_dh = cache_attn_k.shape
    da_dv = cache_attn_v.shape[-1]
    _, _, hg_heads, hg_dk, hg_dv = state_hgrn.shape
    width = da_heads * da_dv
    assert width == da_heads * 2 * da_dh == hg_heads * hg_dk == hg_heads * hg_dv
    assert da_dv == LANES and hg_dk == LANES and hg_dv == LANES

    lb_all = jnp.cumsum(jax.nn.softmax(hgrn_lb_logits.astype(F32), axis=0), axis=0)
    gid = jnp.arange(width) // da_dh
    pm = jnp.where(gid[:, None] == gid[None, :], 1.0 / da_dh, 0.0).astype(BF16)

    xs = {"p": x_prompt.reshape(bp * tp, d), "s": x_sample.reshape(bs * ts, d)}
    dims = {"p": (bp, tp), "s": (bs, ts)}
    outs = {"p": {}, "s": {}}
    kp_l, vp_l, ks_l, vs_l, sp_l, ss_l, sgu_l = [], [], [], [], [], [], []

    for layer in range(depth):
        j = layer // 2
        wrh, wrl, rbias, groups, epg = _router_weights(
            router_group_w[layer], router_group_b[layer], router_expert_w[layer], router_expert_b[layer])
        g_mix = norm_mix[layer][None, :]
        g_ffn = norm_ffn[layer][None, :]
        if layer % 2 == 0:
            lam_init = 0.8 - 0.6 * math.exp(-0.3 * layer)
            w_in_bf = w_in_even[j].astype(BF16)
            w_out_bf = w_out_even[j].astype(BF16)
            reps = width // da_dh
            qg = jnp.tile(q_norm_gain[j], reps)[None, :]
            kg = jnp.tile(k_norm_gain[j], reps)[None, :]
            lam4 = jnp.stack([lam_q1[j], lam_k1[j], lam_q2[j], lam_k2[j]])
            sub_gain = da_out_gain[j][None, :]
            hg_gain = hgrn_out_gain[j][None, :]
            lb = lb_all[j][None, :]
            for key in ("p", "s"):
                b, t = dims[key]
                x = xs[key]
                q, k, v, qh, kb, lf, ih, gs = _in_even(
                    x, g_mix, w_in_bf, pm, qg, kg, lb, width=width, q_scale=da_dh ** -0.5, tm=min(512, b * t))
                if key == "p":
                    oa = _attn_prompt(q, k, v, rel_bias, lam4, sub_gain, batch=b, seq=t, heads=da_heads,
                                      dh=da_dh, lam_init=lam_init, qt=min(256, t))
                    s0 = jnp.zeros((b, hg_heads, hg_dk, hg_dv), F32)
                    ob, s_new = _hgrn(qh, kb, lf, ih, s0, hg_gain, batch=b, seq=t, heads=hg_heads,
                                      dk=hg_dk, dv=hg_dv, tb=min(512, t))
                    kp_l.append(k.reshape(b, t, da_heads, 2, da_dh))
                    vp_l.append(v.reshape(b, t, da_heads, da_dv))
                    sp_l.append(s_new)
                else:
                    ck = cache_attn_k[j].reshape(b, past, width)
                    cv = cache_attn_v[j].reshape(b, past, width)
                    oa = _attn_sample(q, k, v, ck, cv, rel_bias, lam4, sub_gain, batch=b, t=t,
                                      heads=da_heads, dh=da_dh, lam_init=lam_init)
                    ob, s_new = _hgrn(qh, kb, lf, ih, state_hgrn[j], hg_gain, batch=b, seq=t,
                                      heads=hg_heads, dk=hg_dk, dv=hg_dv, tb=t)
                    ks_l.append(k.reshape(b, t, da_heads, 2, da_dh))
                    vs_l.append(v.reshape(b, t, da_heads, da_dv))
                    ss_l.append(s_new)
                outs[key] = _out_even(oa, ob, gs, x, w_out_bf, g_ffn, wrh, wrl, rbias, tm=min(512, b * t))
        else:
            w_in_bf = w_in_odd[j].astype(BF16)
            w_out_bf = w_out_odd[j].astype(BF16)
            v_gain = sgu_v_gain[j][None, :]
            for key in ("p", "s"):
                b, t = dims[key]
                l = min(SGU_CHUNK, t)
                res = _odd_mixer(xs[key], g_mix, w_in_bf, v_gain, sgu_w[j][:, :l, :l], sgu_b[j][:, :l].T,
                                 w_out_bf, g_ffn, wrh, wrl, rbias, l=l, tm=min(256, b * t), emit_v=(key == "s"))
                outs[key] = res[:3]
                if key == "s":
                    sgu_l.append(res[3].reshape(b, t, -1))
        for key in ("p", "s"):
            x1, xn, lgt = outs[key]
            xs[key] = _moe(x1, xn, lgt, expert_w_gate[layer], expert_w_up[layer], expert_w_down[layer],
                           groups=groups, epg=epg)

    return (xs["p"].reshape(bp, tp, d), xs["s"].reshape(bs, ts, d), jnp.stack(kp_l), jnp.stack(vp_l),
            jnp.stack(ks_l), jnp.stack(vs_l), jnp.stack(sp_l), jnp.stack(ss_l), jnp.stack(sgu_l))
```

```python
import functools
import math

import jax
import jax.numpy as jnp
from jax import lax
from jax.experimental import pallas as pl
from jax.experimental.pallas import tpu as pltpu

F32 = jnp.float32
BF16 = jnp.bfloat16
I32 = jnp.int32

EPS = 1e-6
CHUNK = 64
SGU_CHUNK = 128
REL_BUCKETS = 32
REL_MAX_DIST = 128
TOP_K = 2
MOE_BLOCK = 256
HG_SUB = 16

LANES = 128
SUBLANES = 8
VMEM_LIMIT = 56 * 1024 * 1024

NT_DIMS = (((1,), (1,)), ((), ()))
TN_DIMS = (((0,), (0,)), ((), ()))


def _params(*sem):
    return pltpu.CompilerParams(dimension_semantics=sem, vmem_limit_bytes=VMEM_LIMIT)


def _const_spec(shape):
    nd = len(shape)
    return pl.BlockSpec(shape, lambda *_: (0,) * nd, pipeline_mode=pl.Buffered(1))


def _sigmoid(x):
    return 1.0 / (1.0 + jnp.exp(-x))


def _rms(x, g):
    return x * lax.rsqrt(jnp.mean(x * x, axis=-1, keepdims=True) + EPS) * g


def _dot(a, b):
    return jnp.dot(a, b, preferred_element_type=F32)


def rel_bucket(rel):
    half = REL_BUCKETS // 2
    max_exact = half // 2
    ret = (rel > 0).astype(I32) * half
    n = jnp.abs(rel)
    nf = jnp.maximum(n, 1).astype(F32)
    large = max_exact + (jnp.log(nf / max_exact) / math.log(REL_MAX_DIST / max_exact)
                         * (half - max_exact)).astype(I32)
    large = jnp.minimum(large, half - 1)
    return ret + jnp.where(n < max_exact, n, large)


def _in_even_kernel(x_ref, g_ref, w_ref, pm_ref, qg_ref, kg_ref, lb_ref,
                    q_ref, k_ref, v_ref, qh_ref, kb_ref, lf_ref, ih_ref, gs_ref, *, width, q_scale):
    xn = _rms(x_ref[...], g_ref[...]).astype(BF16)

    def proj(c):
        return _dot(xn, w_ref[:, c * width:(c + 1) * width])

    def group_norm(y, gain):
        ms = _dot((y * y).astype(BF16), pm_ref[...])
        return y * lax.rsqrt(ms + EPS) * gain

    q_ref[...] = group_norm(proj(0), qg_ref[...]) * q_scale
    k_ref[...] = group_norm(proj(1), kg_ref[...])
    v_ref[...] = proj(2)
    yq = proj(3)
    qh_ref[...] = yq * _sigmoid(yq)
    zf = proj(4)
    lb = lb_ref[...]
    lf_ref[...] = jnp.log(lb + (1.0 - lb) * _sigmoid(zf))
    kb_ref[...] = (1.0 - lb) * _sigmoid(-zf)
    ih_ref[...] = proj(5)
    yg = proj(6)
    gs_ref[...] = yg * _sigmoid(yg)


def _in_even(x, g_mix, w_bf, pm, qg, kg, lb, *, width, q_scale, tm):
    n, d = x.shape
    assert n % tm == 0
    row = lambda i: (i, 0)
    out = jax.ShapeDtypeStruct((n, width), F32)
    return pl.pallas_call(
        functools.partial(_in_even_kernel, width=width, q_scale=q_scale),
        grid=(n // tm,),
        in_specs=[pl.BlockSpec((tm, d), row), _const_spec((1, d)), _const_spec(w_bf.shape),
                  _const_spec(pm.shape), _const_spec((1, width)), _const_spec((1, width)),
                  _const_spec((1, width))],
        out_specs=[pl.BlockSpec((tm, width), row)] * 8,
        out_shape=[out] * 8,
        compiler_params=_params("arbitrary"),
        name="in_even",
    )(x, g_mix, w_bf, pm, qg, kg, lb)


def _bias_from_buckets(bk, rb_ref, h):
    b = jnp.zeros(bk.shape, F32)
    for u in range(REL_BUCKETS):
        b = jnp.where(bk == u, rb_ref[u, h], b)
    return jnp.where(bk < 0, -jnp.inf, b)


def _lam(lam_ref, lam_init):
    r = lam_ref[...]
    s1 = jnp.sum(r[0:1] * r[1:2], axis=1, keepdims=True)
    s2 = jnp.sum(r[2:3] * r[3:4], axis=1, keepdims=True)
    return jnp.exp(s1) - jnp.exp(s2) + lam_init


def _split_components(q, dh):
    lane = lax.broadcasted_iota(I32, q.shape, 1)
    q0 = jnp.where(lane < dh, q, 0.0)
    q1 = jnp.where(lane >= dh, q, 0.0)
    return jnp.concatenate([q0, q1], axis=0).astype(BF16)


def _attn_prompt_kernel(rb_ref, far_ref, lam_ref, bk_ref, sg_ref, q_ref, k_ref, v_ref, o_ref, *,
                        qt, dh, lam_init):
    h = pl.program_id(1)
    t = q_ref.shape[0]
    kb = k_ref[...].astype(BF16)
    vt = v_ref[...].T.astype(BF16)
    qtr = q_ref[...].T
    sub = lax.broadcasted_iota(I32, (2 * dh, qt), 0)
    bias = []
    for d in range(2):
        b = _bias_from_buckets(bk_ref[d], rb_ref, h)
        bias.append(jnp.concatenate([b, b], axis=1))
    far = rb_ref[far_ref[0], h]
    lam = _lam(lam_ref, lam_init)
    gain = sg_ref[...] * (1.0 - lam_init)
    for i in range(t // qt):
        qi = qtr[:, i * qt:(i + 1) * qt]
        qz = jnp.concatenate([jnp.where(sub < dh, qi, 0.0), jnp.where(sub >= dh, qi, 0.0)],
                             axis=1).astype(BF16)
        n = (i + 1) * qt
        parts = [(n - qt, n, bias[0])]
        if i >= 1:
            parts.append((n - 2 * qt, n - qt, bias[1]))
        if i >= 2:
            parts.append((0, n - 2 * qt, far))
        s = [_dot(kb[lo:hi], qz) + b for lo, hi, b in parts]
        m = functools.reduce(jnp.maximum, [jnp.max(x, axis=0, keepdims=True) for x in s])
        p = [jnp.exp(x - m) for x in s]
        l = functools.reduce(lambda a, b: a + b, [jnp.sum(x, axis=0, keepdims=True) for x in p])
        acc = functools.reduce(lambda a, b: a + b,
                               [_dot(vt[:, lo:hi], x.astype(BF16)) for (lo, hi, _), x in zip(parts, p)])
        o = acc / l
        out = o[:, :qt] - lam * o[:, qt:]
        out = out * lax.rsqrt(jnp.mean(out * out, axis=0, keepdims=True) + EPS) * gain
        o_ref[i * qt:(i + 1) * qt, :] = out.T


def _attn_prompt(q, k, v, rel_bias, lam4, sub_gain, *, batch, seq, heads, dh, lam_init, qt):
    n, w = q.shape
    dv = w // heads
    assert dv == 2 * dh and seq % qt == 0 and qt % CHUNK == 0
    kj = jnp.arange(qt, dtype=I32)[:, None]
    qi = jnp.arange(qt, dtype=I32)[None, :]
    bk0 = jnp.where((kj // CHUNK) <= (qi // CHUNK), rel_bucket(kj - qi), -1)
    bk1 = rel_bucket(kj - qi - qt)
    bk = jnp.stack([bk0, bk1]).astype(I32)
    assert qt + 1 >= REL_MAX_DIST
    far = rel_bucket(jnp.full((1,), -(qt + 1), I32))
    smem = pl.BlockSpec(memory_space=pltpu.SMEM)
    seq_blk = pl.BlockSpec((seq, dv), lambda b, h: (b, h))
    return pl.pallas_call(
        functools.partial(_attn_prompt_kernel, qt=qt, dh=dh, lam_init=lam_init),
        grid=(batch, heads),
        in_specs=[smem, smem, _const_spec(lam4.shape), _const_spec(bk.shape), _const_spec((dv, 1)),
                  seq_blk, seq_blk, seq_blk],
        out_specs=seq_blk,
        out_shape=jax.ShapeDtypeStruct((n, w), F32),
        compiler_params=_params("arbitrary", "arbitrary"),
        name="attn_prompt",
    )(rel_bias, far, lam4, bk, sub_gain.T, q, k, v)


def _attn_sample_kernel(rb_ref, lam_ref, bkc_ref, bkn_ref, sg_ref, q_ref, kc_ref, vc_ref, kn_ref, vn_ref,
                        o_ref, *, t, dh, lam_init):
    h = pl.program_id(1)
    qz = _split_components(q_ref[...], dh)
    bc = _bias_from_buckets(bkc_ref[...], rb_ref, h)
    bn = _bias_from_buckets(bkn_ref[...], rb_ref, h)
    sc = lax.dot_general(qz, kc_ref[0].astype(BF16), NT_DIMS, preferred_element_type=F32)
    sn = lax.dot_general(qz, kn_ref[...].astype(BF16), NT_DIMS, preferred_element_type=F32)
    sc = sc + jnp.concatenate([bc, bc], axis=0)
    sn = sn + jnp.concatenate([bn, bn], axis=0)
    m = jnp.maximum(jnp.max(sc, axis=1, keepdims=True), jnp.max(sn, axis=1, keepdims=True))
    pc = jnp.exp(sc - m)
    pn = jnp.exp(sn - m)
    l = jnp.sum(pc, axis=1, keepdims=True) + jnp.sum(pn, axis=1, keepdims=True)
    acc = _dot(pc.astype(BF16), vc_ref[0].astype(BF16)) + _dot(pn.astype(BF16), vn_ref[...].astype(BF16))
    o = acc / l
    out = o[:t] - _lam(lam_ref, lam_init) * o[t:]
    o_ref[...] = _rms(out, sg_ref[...]) * (1.0 - lam_init)


def _attn_sample(q, k_new, v_new, cache_k, cache_v, rel_bias, lam4, sub_gain, *, batch, t, heads, dh,
                 lam_init):
    n, w = q.shape
    dv = w // heads
    past = cache_k.shape[1]
    assert past % CHUNK == 0 and t <= CHUNK
    qpos = past + jnp.arange(t, dtype=I32)[:, None]
    bkc = rel_bucket(jnp.arange(past, dtype=I32)[None, :] - qpos).astype(I32)
    bkn = rel_bucket(past + jnp.arange(t, dtype=I32)[None, :] - qpos).astype(I32)
    smem = pl.BlockSpec(memory_space=pltpu.SMEM)
    new = pl.BlockSpec((t, dv), lambda b, h: (b, h))
    old = pl.BlockSpec((1, past, dv), lambda b, h: (b, 0, h))
    return pl.pallas_call(
        functools.partial(_attn_sample_kernel, t=t, dh=dh, lam_init=lam_init),
        grid=(batch, heads),
        in_specs=[smem, _const_spec(lam4.shape), _const_spec(bkc.shape), _const_spec(bkn.shape),
                  _const_spec((1, dv)), new, old, old, new, new],
        out_specs=new,
        out_shape=jax.ShapeDtypeStruct((n, w), F32),
        compiler_params=_params("arbitrary", "arbitrary"),
        name="attn_sample",
    )(rel_bias, lam4, bkc, bkn, sub_gain, q, cache_k, cache_v, k_new, v_new)


def _cumsum_rows(x):
    c = x.shape[0]
    row = lax.broadcasted_iota(I32, x.shape, 0)
    s = 1
    while s < c:
        x = x + jnp.where(row >= s, pltpu.roll(x, s, axis=0), 0.0)
        s *= 2
    return x


def _hgrn_kernel(qh_ref, kb_ref, lf_ref, ih_ref, s0_ref, hg_ref, ob_ref, sf_ref, st_s, *,
                 heads, dk, dv, c, nsb):
    t = pl.program_id(1)

    @pl.when(t == 0)
    def _():
        for h in range(heads):
            st_s[h] = s0_ref[0, h].T

    tb = qh_ref.shape[0]
    row = lax.broadcasted_iota(I32, (c, c), 0)
    col = lax.broadcasted_iota(I32, (c, c), 1)
    causal = col <= row

    def chunk(ci, carry):
        r0 = pl.multiple_of(ci * c, c)
        for h in range(heads):
            rows = pl.ds(r0, c)
            q = qh_ref[rows, h * dk:(h + 1) * dk]
            k = kb_ref[rows, h * dk:(h + 1) * dk]
            v = ih_ref[rows, h * dv:(h + 1) * dv]
            b = _cumsum_rows(lf_ref[rows, h * dk:(h + 1) * dk])
            bl = b[c - 1:c]
            st = st_s[h]
            inter = lax.dot_general((q * jnp.exp(b)).astype(BF16), st.astype(BF16), NT_DIMS,
                                    preferred_element_type=F32)
            qs, ks = [], []
            for j in range(nsb):
                ref = b[j * HG_SUB + HG_SUB // 2:j * HG_SUB + HG_SUB // 2 + 1]
                qs.append(q * jnp.exp(b - ref))
                sub = slice(j * HG_SUB, (j + 1) * HG_SUB)
                ks.append(k[sub] * jnp.exp(ref - b[sub]))
            a_full = lax.dot_general(jnp.concatenate(qs, axis=0).astype(BF16),
                                     jnp.concatenate(ks, axis=0).astype(BF16), NT_DIMS,
                                     preferred_element_type=F32)
            att = jnp.zeros((c, c), F32)
            for j in range(nsb):
                att = jnp.where(col >= j * HG_SUB, a_full[j * c:(j + 1) * c], att)
            att = jnp.where(causal, att, 0.0)
            out = inter + _dot(att.astype(BF16), v.astype(BF16))
            ob_ref[rows, h * dv:(h + 1) * dv] = _rms(out, hg_ref[...])
            kdec = (k * jnp.exp(bl - b)).astype(BF16)
            st_s[h] = jnp.exp(bl) * st + lax.dot_general(v.astype(BF16), kdec, TN_DIMS,
                                                         preferred_element_type=F32)
        return carry

    lax.fori_loop(0, tb // c, chunk, 0)

    @pl.when(t == pl.num_programs(1) - 1)
    def _():
        for h in range(heads):
            sf_ref[0, h] = st_s[h].T


def _hgrn(qh, kb, lf, ih, s0, hg_gain, *, batch, seq, heads, dk, dv, tb):
    n = qh.shape[0]
    c = min(CHUNK, seq)
    assert seq % tb == 0 and tb % c == 0 and c % HG_SUB == 0
    nt = seq // tb
    blk = lambda w: pl.BlockSpec((tb, w), lambda b, t: (b * nt + t, 0))
    st = pl.BlockSpec((1, heads, dk, dv), lambda b, t: (b, 0, 0, 0))
    return pl.pallas_call(
        functools.partial(_hgrn_kernel, heads=heads, dk=dk, dv=dv, c=c, nsb=c // HG_SUB),
        grid=(batch, nt),
        in_specs=[blk(heads * dk), blk(heads * dk), blk(heads * dk), blk(heads * dv), st,
                  _const_spec((1, dv))],
        out_specs=[blk(heads * dv), st],
        out_shape=[jax.ShapeDtypeStruct((n, heads * dv), F32),
                   jax.ShapeDtypeStruct((batch, heads, dk, dv), F32)],
        scratch_shapes=[pltpu.VMEM((heads, dv, dk), F32)],
        compiler_params=_params("arbitrary", "arbitrary"),
        name="hgrn",
    )(qh, kb, lf, ih, s0, hg_gain)


def _ffn_prologue(x1, gf_ref, wrh_ref, wrl_ref, rbias_ref, x1_ref, xn_ref, lg_ref):
    x1_ref[...] = x1
    xn = _rms(x1, gf_ref[...])
    xn_ref[...] = xn
    hi = xn.astype(BF16)
    lo = (xn - hi.astype(F32)).astype(BF16)
    nt = functools.partial(lax.dot_general, dimension_numbers=NT_DIMS, preferred_element_type=F32)
    lg_ref[...] = nt(wrh_ref[...], hi) + nt(wrh_ref[...], lo) + nt(wrl_ref[...], hi) + rbias_ref[...]


def _out_even_kernel(oa_ref, ob_ref, gs_ref, x_ref, w_ref, gf_ref, wrh_ref, wrl_ref, rbias_ref,
                     x1_ref, xn_ref, lg_ref):
    o = jnp.concatenate([oa_ref[...], ob_ref[...] * gs_ref[...]], axis=1).astype(BF16)
    x1 = x_ref[...] + _dot(o, w_ref[...])
    _ffn_prologue(x1, gf_ref, wrh_ref, wrl_ref, rbias_ref, x1_ref, xn_ref, lg_ref)


def _out_even(oa, ob, gs, x, w_bf, g_ffn, wrh, wrl, rbias, *, tm):
    n, d = x.shape
    w = oa.shape[1]
    nr = wrh.shape[0]
    row = lambda i: (i, 0)
    return pl.pallas_call(
        _out_even_kernel,
        grid=(n // tm,),
        in_specs=[pl.BlockSpec((tm, w), row), pl.BlockSpec((tm, w), row), pl.BlockSpec((tm, w), row),
                  pl.BlockSpec((tm, d), row), _const_spec(w_bf.shape), _const_spec((1, d)),
                  _const_spec(wrh.shape), _const_spec(wrl.shape), _const_spec(rbias.shape)],
        out_specs=[pl.BlockSpec((tm, d), row), pl.BlockSpec((tm, d), row),
                   pl.BlockSpec((nr, tm), lambda i: (0, i))],
        out_shape=[jax.ShapeDtypeStruct((n, d), F32), jax.ShapeDtypeStruct((n, d), F32),
                   jax.ShapeDtypeStruct((nr, n), F32)],
        compiler_params=_params("arbitrary"),
        name="out_even",
    )(oa, ob, gs, x, w_bf, g_ffn, wrh, wrl, rbias)


def _gelu(x):
    return 0.5 * x * (1.0 + jnp.tanh(math.sqrt(2.0 / math.pi) * (x + 0.044715 * (x * x * x))))


def _odd_kernel(x_ref, gm_ref, win_ref, vg_ref, wsp_ref, bsp_ref, wout_ref, gf_ref, wrh_ref, wrl_ref,
                rbias_ref, x1_ref, xn_ref, lg_ref, *rest, half, groups, l, emit_v):
    if emit_v:
        vn_ref, u_s, s_s = rest
    else:
        vn_ref = None
        u_s, vn_s, s_s = rest
    tm = x_ref.shape[0]
    x = x_ref[...]
    xn = _rms(x, gm_ref[...]).astype(BF16)
    cw = 512
    vbuf = vn_ref if emit_v else vn_s
    for cidx in range(half // cw):
        u_s[:, cidx * cw:(cidx + 1) * cw] = _gelu(_dot(xn, win_ref[:, cidx * cw:(cidx + 1) * cw]))
        vbuf[:, cidx * cw:(cidx + 1) * cw] = _gelu(
            _dot(xn, win_ref[:, half + cidx * cw:half + (cidx + 1) * cw]))
    vbuf[...] = _rms(vbuf[...], vg_ref[...])
    gw = half // groups
    row = lax.broadcasted_iota(I32, (l, l), 0)
    col = lax.broadcasted_iota(I32, (l, l), 1)
    for g in range(groups):
        wg = jnp.where(col <= row, wsp_ref[g], 0.0).astype(BF16)
        bg = bsp_ref[:, g:g + 1]
        for ci in range(tm // l):
            vv = vbuf[ci * l:(ci + 1) * l, g * gw:(g + 1) * gw].astype(BF16)
            s_s[ci * l:(ci + 1) * l, g * gw:(g + 1) * gw] = _dot(wg, vv) + bg
    y = _dot((u_s[...] * s_s[...]).astype(BF16), wout_ref[...])
    _ffn_prologue(x + y, gf_ref, wrh_ref, wrl_ref, rbias_ref, x1_ref, xn_ref, lg_ref)


def _odd_mixer(x, g_mix, win_bf, v_gain, wsp, bsp_t, wout_bf, g_ffn, wrh, wrl, rbias, *, l, tm, emit_v):
    n, d = x.shape
    half = wout_bf.shape[0]
    groups = wsp.shape[0]
    nr = wrh.shape[0]
    assert n % tm == 0 and tm % l == 0
    row = lambda i: (i, 0)
    out_specs = [pl.BlockSpec((tm, d), row), pl.BlockSpec((tm, d), row),
                 pl.BlockSpec((nr, tm), lambda i: (0, i))]
    out_shape = [jax.ShapeDtypeStruct((n, d), F32), jax.ShapeDtypeStruct((n, d), F32),
                 jax.ShapeDtypeStruct((nr, n), F32)]
    scratch = [pltpu.VMEM((tm, half), F32)]
    if emit_v:
        out_specs.append(pl.BlockSpec((tm, half), row))
        out_shape.append(jax.ShapeDtypeStruct((n, half), F32))
    else:
        scratch.append(pltpu.VMEM((tm, half), F32))
    scratch.append(pltpu.VMEM((tm, half), F32))
    return pl.pallas_call(
        functools.partial(_odd_kernel, half=half, groups=groups, l=l, emit_v=emit_v),
        grid=(n // tm,),
        in_specs=[pl.BlockSpec((tm, d), row), _const_spec((1, d)), _const_spec(win_bf.shape),
                  _const_spec((1, half)), _const_spec(wsp.shape), _const_spec(bsp_t.shape),
                  _const_spec(wout_bf.shape), _const_spec((1, d)), _const_spec(wrh.shape),
                  _const_spec(wrl.shape), _const_spec(rbias.shape)],
        out_specs=out_specs,
        out_shape=out_shape,
        scratch_shapes=scratch,
        compiler_params=_params("arbitrary"),
        name="odd_mixer",
    )(x, g_mix, win_bf, v_gain, wsp, bsp_t, wout_bf, g_ffn, wrh, wrl, rbias)


def _route_kernel(lg_ref, tri_ref, e_ref, g_ref, r_ref, cnt_ref, run_s, *, groups, epg):
    i = pl.program_id(0)

    @pl.when(i == 0)
    def _():
        run_s[...] = jnp.zeros(run_s.shape, F32)

    lg = lg_ref[...]
    tr = lg.shape[1]
    gl = [lg[g:g + 1] for g in range(groups)]
    m = functools.reduce(jnp.maximum, gl)
    grp = jnp.full((1, tr), groups - 1, I32)
    for g in range(groups - 2, -1, -1):
        grp = jnp.where(gl[g] == m, g, grp)
    gate_g = 1.0 / functools.reduce(lambda a, b: a + b, [jnp.exp(x - m) for x in gl])
    sel = lg[SUBLANES + (groups - 1) * epg:SUBLANES + groups * epg]
    for g in range(groups - 2, -1, -1):
        sel = jnp.where(grp == g, lg[SUBLANES + g * epg:SUBLANES + (g + 1) * epg], sel)
    sub = lax.broadcasted_iota(I32, sel.shape, 0)
    v1 = jnp.max(sel, axis=0, keepdims=True)
    i1 = jnp.min(jnp.where(sel == v1, sub, epg), axis=0, keepdims=True)
    sel2 = jnp.where(sub == i1, -jnp.inf, sel)
    v2 = jnp.max(sel2, axis=0, keepdims=True)
    i2 = jnp.min(jnp.where(sel2 == v2, sub, epg), axis=0, keepdims=True)
    tt = jnp.exp(v2 - v1)
    g1 = gate_g / (1.0 + tt)
    g2 = gate_g * tt / (1.0 + tt)
    e1 = grp * epg + i1
    e2 = grp * epg + i2
    ne = groups * epg
    eidx = lax.broadcasted_iota(I32, (ne, tr), 0)
    oh1 = eidx == e1
    oh2 = eidx == e2
    cnt = jnp.where(oh1, 1.0, 0.0) + jnp.where(oh2, 1.0, 0.0)
    before = run_s[:, 0:1] + _dot(cnt.astype(BF16), tri_ref[...])
    r1 = jnp.sum(jnp.where(oh1, before, 0.0), axis=0, keepdims=True)
    r2 = jnp.sum(jnp.where(oh2, before, 0.0), axis=0, keepdims=True)
    run_s[...] = run_s[...] + jnp.sum(cnt, axis=1, keepdims=True)
    rows = lax.broadcasted_iota(I32, (SUBLANES, tr), 0)
    e_ref[...] = jnp.where(rows == 0, e1, jnp.where(rows == 1, e2, 0))
    g_ref[...] = jnp.where(rows == 0, g1, jnp.where(rows == 1, g2, 0.0))
    r_ref[...] = jnp.where(rows == 0, r1, jnp.where(rows == 1, r2, 0.0)).astype(I32)
    cnt_ref[...] = run_s[...]


def _route(lgt, *, groups, epg, tr):
    nr, n = lgt.shape
    assert n % tr == 0 and nr == SUBLANES + groups * epg
    ne = groups * epg
    tri = (jnp.arange(tr)[:, None] < jnp.arange(tr)[None, :]).astype(BF16)
    tok = pl.BlockSpec((SUBLANES, tr), lambda i: (0, i))
    return pl.pallas_call(
        functools.partial(_route_kernel, groups=groups, epg=epg),
        grid=(n // tr,),
        in_specs=[pl.BlockSpec((nr, tr), lambda i: (0, i)), _const_spec((tr, tr))],
        out_specs=[tok, tok, tok, pl.BlockSpec((ne, LANES), lambda i: (0, 0))],
        out_shape=[jax.ShapeDtypeStruct((SUBLANES, n), I32), jax.ShapeDtypeStruct((SUBLANES, n), F32),
                   jax.ShapeDtypeStruct((SUBLANES, n), I32), jax.ShapeDtypeStruct((ne, LANES), F32)],
        scratch_shapes=[pltpu.VMEM((ne, LANES), F32)],
        compiler_params=_params("arbitrary"),
        name="route",
    )(lgt, tri)


def _row_copy(src, dst, sem, s, d):
    return pltpu.make_async_copy(src.at[pl.ds(s, 1)], dst.at[pl.ds(d, 1)], sem)


def _dispatch_kernel(dst_ref, x_ref, buf_in, buf_ref, sem):
    del buf_in
    tp = x_ref.shape[0]

    def issue(r, c):
        for kk in range(TOP_K):
            _row_copy(x_ref, buf_ref, sem, r, dst_ref[0, kk, r]).start()
        return c

    lax.fori_loop(0, tp, issue, 0)

    def drain(r, c):
        for kk in range(TOP_K):
            _row_copy(x_ref, buf_ref, sem, 0, 0).wait()
        return c

    lax.fori_loop(0, tp, drain, 0)


def _dispatch(dest3, xn, n_slots, *, tp):
    n, d = xn.shape
    buf = jnp.zeros((n_slots, d), F32)
    return pl.pallas_call(
        _dispatch_kernel,
        grid=(n // tp,),
        in_specs=[pl.BlockSpec((1, TOP_K, tp), lambda i: (i, 0, 0), memory_space=pltpu.SMEM),
                  pl.BlockSpec((tp, d), lambda i: (i, 0)),
                  pl.BlockSpec(memory_space=pl.ANY)],
        out_specs=pl.BlockSpec(memory_space=pl.ANY),
        out_shape=jax.ShapeDtypeStruct((n_slots, d), F32),
        scratch_shapes=[pltpu.SemaphoreType.DMA(())],
        input_output_aliases={2: 0},
        compiler_params=_params("arbitrary"),
        name="dispatch",
    )(dest3, xn, buf)


def _expert_kernel(be_ref, nu_ref, x_ref, wg_ref, wu_ref, wd_ref, o_ref, wg_s, wu_s, wd_s):
    b = pl.program_id(0)
    prev = be_ref[jnp.maximum(b - 1, 0)]

    @pl.when((b == 0) | (be_ref[b] != prev))
    def _():
        wg_s[...] = wg_ref[0, 0].astype(BF16)
        wu_s[...] = wu_ref[0, 0].astype(BF16)
        wd_s[...] = wd_ref[0, 0].astype(BF16)

    @pl.when(b < nu_ref[0])
    def _():
        xb = x_ref[...].astype(BF16)
        gate = _dot(xb, wg_s[...])
        h = gate * _sigmoid(gate) * _dot(xb, wu_s[...])
        o_ref[...] = _dot(h.astype(BF16), wd_s[...])

    @pl.when(b >= nu_ref[0])
    def _():
        o_ref[...] = jnp.zeros(o_ref.shape, F32)


def _experts(blk_expert, n_used, buf, w_gate, w_up, w_down, *, layer):
    n_slots, d = buf.shape
    de = w_gate.shape[3]
    n_blocks = n_slots // MOE_BLOCK
    rows = pl.BlockSpec((MOE_BLOCK, d), lambda b, be, nu: (b, 0))
    grid_spec = pltpu.PrefetchScalarGridSpec(
        num_scalar_prefetch=2,
        grid=(n_blocks,),
        in_specs=[rows,
                  pl.BlockSpec((1, 1, d, de), lambda b, be, nu: (layer, be[b], 0, 0)),
                  pl.BlockSpec((1, 1, d, de), lambda b, be, nu: (layer, be[b], 0, 0)),
                  pl.BlockSpec((1, 1, de, d), lambda b, be, nu: (layer, be[b], 0, 0))],
        out_specs=rows,
        scratch_shapes=[pltpu.VMEM((d, de), BF16), pltpu.VMEM((d, de), BF16), pltpu.VMEM((de, d), BF16)],
    )
    return pl.pallas_call(
        _expert_kernel,
        grid_spec=grid_spec,
        out_shape=jax.ShapeDtypeStruct((n_slots, d), F32),
        compiler_params=_params("arbitrary"),
        name="experts",
    )(blk_expert, n_used, buf, w_gate, w_up, w_down)


def _combine_kernel(dst_ref, gate_ref, x_ref, yb_ref, o_ref, rows_s, sem):
    tq = x_ref.shape[0]

    def issue(r, c):
        for kk in range(TOP_K):
            _row_copy(yb_ref, rows_s.at[kk], sem, dst_ref[0, kk, r], r).start()
        return c

    lax.fori_loop(0, tq, issue, 0)

    def drain(r, c):
        for kk in range(TOP_K):
            _row_copy(yb_ref, rows_s.at[kk], sem, 0, 0).wait()
        return c

    lax.fori_loop(0, tq, drain, 0)
    g = gate_ref[...]
    o_ref[...] = x_ref[...] + g[:, 0:1] * rows_s[0] + g[:, 1:2] * rows_s[1]


def _combine(dest3, gates_t, x1, yb, *, tq):
    n, d = x1.shape
    return pl.pallas_call(
        _combine_kernel,
        grid=(n // tq,),
        in_specs=[pl.BlockSpec((1, TOP_K, tq), lambda i: (i, 0, 0), memory_space=pltpu.SMEM),
                  pl.BlockSpec((tq, TOP_K), lambda i: (i, 0)),
                  pl.BlockSpec((tq, d), lambda i: (i, 0)),
                  pl.BlockSpec(memory_space=pl.ANY)],
        out_specs=pl.BlockSpec((tq, d), lambda i: (i, 0)),
        out_shape=jax.ShapeDtypeStruct((n, d), F32),
        scratch_shapes=[pltpu.VMEM((TOP_K, tq, d), F32), pltpu.SemaphoreType.DMA(())],
        compiler_params=_params("arbitrary"),
        name="combine",
    )(dest3, gates_t, x1, yb)


def _moe(x1, xn, lgt, w_gate, w_up, w_down, *, layer, groups, epg):
    n, d = x1.shape
    ne = groups * epg
    tile = min(256, n)
    e8, g8, r8, cnt = _route(lgt, groups=groups, epg=epg, tr=min(512, n))
    counts = cnt[:, 0].astype(I32)
    padded = (counts + MOE_BLOCK - 1) // MOE_BLOCK * MOE_BLOCK
    pend = jnp.cumsum(padded)
    pstart = pend - padded
    n_blocks = -(-(n * TOP_K) // MOE_BLOCK) + ne
    eids = jnp.arange(ne, dtype=I32)
    seg = jnp.sum(jnp.where(e8[:TOP_K, :, None] == eids, pstart, 0), axis=-1)
    dest = seg + r8[:TOP_K]
    dest3 = dest.reshape(TOP_K, n // tile, tile).transpose(1, 0, 2)
    blk_row = jnp.arange(n_blocks, dtype=I32)[:, None] * MOE_BLOCK
    blk_expert = jnp.minimum(jnp.sum((pend[None, :] <= blk_row).astype(I32), axis=1), ne - 1)
    n_used = (pend[-1:] // MOE_BLOCK).astype(I32)
    buf = _dispatch(dest3, xn, n_blocks * MOE_BLOCK, tp=tile)
    yb = _experts(blk_expert, n_used, buf, w_gate, w_up, w_down, layer=layer)
    return _combine(dest3, g8[:TOP_K].T, x1, yb, tq=tile)


def _router_weights(wg, bg, we, be):
    d, groups = wg.shape
    epg = we.shape[2]
    assert groups <= SUBLANES and epg == SUBLANES
    pad = jnp.zeros((SUBLANES - groups, d), F32)
    wr = jnp.concatenate([wg.T, pad, we.transpose(0, 2, 1).reshape(groups * epg, d)], axis=0)
    rb = jnp.concatenate([bg, jnp.zeros((SUBLANES - groups,), F32), be.reshape(-1)])[:, None]
    hi = wr.astype(BF16)
    lo = (wr - hi.astype(F32)).astype(BF16)
    return hi, lo, rb, groups, epg


def kernel(x_prompt, x_sample, cache_attn_k, cache_attn_v, state_hgrn, rel_bias, norm_mix, norm_ffn,
           w_in_even, w_out_even, q_norm_gain, k_norm_gain, lam_q1, lam_k1, lam_q2, lam_k2, da_out_gain,
           hgrn_lb_logits, hgrn_out_gain, w_in_odd, sgu_v_gain, sgu_w, sgu_b, w_out_odd,
           router_group_w, router_group_b, router_expert_w, router_expert_b,
           expert_w_gate, expert_w_up, expert_w_down):
    bp, tp, d = x_prompt.shape
    bs, ts, _ = x_sample.shape
    depth = norm_mix.shape[0]
    _, _, past, da_heads, _, da_dh = cache_attn_k.shape
    da_dv = cache_attn_v.shape[-1]
    _, _, hg_heads, hg_dk, hg_dv = state_hgrn.shape
    width = da_heads * da_dv
    assert width == da_heads * 2 * da_dh == hg_heads * hg_dk == hg_heads * hg_dv
    assert da_dv == LANES and hg_dk == LANES and hg_dv == LANES

    lb_all = jnp.cumsum(jax.nn.softmax(hgrn_lb_logits.astype(F32), axis=0), axis=0)
    gid = jnp.arange(width) // da_dh
    pm = jnp.where(gid[:, None] == gid[None, :], 1.0 / da_dh, 0.0).astype(BF16)

    xs = {"p": x_prompt.reshape(bp * tp, d), "s": x_sample.reshape(bs * ts, d)}
    dims = {"p": (bp, tp), "s": (bs, ts)}
    outs = {"p": {}, "s": {}}
    kp_l, vp_l, ks_l, vs_l, sp_l, ss_l, sgu_l = [], [], [], [], [], [], []

    for layer in range(depth):
        j = layer // 2
        wrh, wrl, rbias, groups, epg = _router_weights(
            router_group_w[layer], router_group_b[layer], router_expert_w[layer], router_expert_b[layer])
        g_mix = norm_mix[layer][None, :]
        g_ffn = norm_ffn[layer][None, :]
        if layer % 2 == 0:
            lam_init = 0.8 - 0.6 * math.exp(-0.3 * layer)
            w_in_bf = w_in_even[j].astype(BF16)
            w_out_bf = w_out_even[j].astype(BF16)
            reps = width // da_dh
            qg = jnp.tile(q_norm_gain[j], reps)[None, :]
            kg = jnp.tile(k_norm_gain[j], reps)[None, :]
            lam4 = jnp.stack([lam_q1[j], lam_k1[j], lam_q2[j], lam_k2[j]])
            sub_gain = da_out_gain[j][None, :]
            hg_gain = hgrn_out_gain[j][None, :]
            lb = lb_all[j][None, :]
            for key in ("p", "s"):
                b, t = dims[key]
                x = xs[key]
                q, k, v, qh, kb, lf, ih, gs = _in_even(
                    x, g_mix, w_in_bf, pm, qg, kg, lb, width=width, q_scale=da_dh ** -0.5, tm=min(512, b * t))
                if key == "p":
                    oa = _attn_prompt(q, k, v, rel_bias, lam4, sub_gain, batch=b, seq=t, heads=da_heads,
                                      dh=da_dh, lam_init=lam_init, qt=min(256, t))
                    s0 = jnp.zeros((b, hg_heads, hg_dk, hg_dv), F32)
                    ob, s_new = _hgrn(qh, kb, lf, ih, s0, hg_gain, batch=b, seq=t, heads=hg_heads,
                                      dk=hg_dk, dv=hg_dv, tb=min(512, t))
                    kp_l.append(k.reshape(b, t, da_heads, 2, da_dh))
                    vp_l.append(v.reshape(b, t, da_heads, da_dv))
                    sp_l.append(s_new)
                else:
                    ck = cache_attn_k[j].reshape(b, past, width)
                    cv = cache_attn_v[j].reshape(b, past, width)
                    oa = _attn_sample(q, k, v, ck, cv, rel_bias, lam4, sub_gain, batch=b, t=t,
                                      heads=da_heads, dh=da_dh, lam_init=lam_init)
                    ob, s_new = _hgrn(qh, kb, lf, ih, state_hgrn[j], hg_gain, batch=b, seq=t,
                                      heads=hg_heads, dk=hg_dk, dv=hg_dv, tb=t)
                    ks_l.append(k.reshape(b, t, da_heads, 2, da_dh))
                    vs_l.append(v.reshape(b, t, da_heads, da_dv))
                    ss_l.append(s_new)
                outs[key] = _out_even(oa, ob, gs, x, w_out_bf, g_ffn, wrh, wrl, rbias, tm=min(512, b * t))
        else:
            w_in_bf = w_in_odd[j].astype(BF16)
            w_out_bf = w_out_odd[j].astype(BF16)
            v_gain = sgu_v_gain[j][None, :]
            for key in ("p", "s"):
                b, t = dims[key]
                l = min(SGU_CHUNK, t)
                res = _odd_mixer(xs[key], g_mix, w_in_bf, v_gain, sgu_w[j][:, :l, :l], sgu_b[j][:, :l].T,
                                 w_out_bf, g_ffn, wrh, wrl, rbias, l=l, tm=min(256, b * t), emit_v=(key == "s"))
                outs[key] = res[:3]
                if key == "s":
                    sgu_l.append(res[3].reshape(b, t, -1))
        for key in ("p", "s"):
            x1, xn, lgt = outs[key]
            xs[key] = _moe(x1, xn, lgt, expert_w_gate, expert_w_up, expert_w_down, layer=layer,
                           groups=groups, epg=epg)

    return (xs["p"].reshape(bp, tp, d), xs["s"].reshape(bs, ts, d), jnp.stack(kp_l), jnp.stack(vp_l),
            jnp.stack(ks_l), jnp.stack(vs_l), jnp.stack(sp_l), jnp.stack(ss_l), jnp.stack(sgu_l))
```

```python
import functools
import math

import jax
import jax.numpy as jnp
from jax import lax
from jax.experimental import pallas as pl
from jax.experimental.pallas import tpu as pltpu

F32 = jnp.float32
BF16 = jnp.bfloat16
I32 = jnp.int32

EPS = 1e-6
CHUNK = 64
SGU_CHUNK = 128
REL_BUCKETS = 32
REL_MAX_DIST = 128
TOP_K = 2
MOE_BLOCK = 256
HG_SUB = 16

LANES = 128
SUBLANES = 8
VMEM_LIMIT = 56 * 1024 * 1024

NT_DIMS = (((1,), (1,)), ((), ()))
TN_DIMS = (((0,), (0,)), ((), ()))


def _params(*sem):
    return pltpu.CompilerParams(dimension_semantics=sem, vmem_limit_bytes=VMEM_LIMIT)


def _const_spec(shape):
    nd = len(shape)
    return pl.BlockSpec(shape, lambda *_: (0,) * nd, pipeline_mode=pl.Buffered(1))


def _sigmoid(x):
    return 1.0 / (1.0 + jnp.exp(-x))


def _rms(x, g):
    return x * lax.rsqrt(jnp.mean(x * x, axis=-1, keepdims=True) + EPS) * g


def _dot(a, b):
    return jnp.dot(a, b, preferred_element_type=F32)


def rel_bucket(rel):
    half = REL_BUCKETS // 2
    max_exact = half // 2
    ret = (rel > 0).astype(I32) * half
    n = jnp.abs(rel)
    nf = jnp.maximum(n, 1).astype(F32)
    large = max_exact + (jnp.log(nf / max_exact) / math.log(REL_MAX_DIST / max_exact)
                         * (half - max_exact)).astype(I32)
    large = jnp.minimum(large, half - 1)
    return ret + jnp.where(n < max_exact, n, large)


def _in_even_kernel(x_ref, g_ref, w_ref, pm_ref, qg_ref, kg_ref, lb_ref,
                    q_ref, k_ref, v_ref, qh_ref, kb_ref, lf_ref, ih_ref, gs_ref, *, width, q_scale):
    xn = _rms(x_ref[...], g_ref[...]).astype(BF16)

    def proj(c):
        return _dot(xn, w_ref[:, c * width:(c + 1) * width])

    def group_norm(y, gain):
        ms = _dot((y * y).astype(BF16), pm_ref[...])
        return y * lax.rsqrt(ms + EPS) * gain

    q_ref[...] = group_norm(proj(0), qg_ref[...]) * q_scale
    k_ref[...] = group_norm(proj(1), kg_ref[...])
    v_ref[...] = proj(2)
    yq = proj(3)
    qh_ref[...] = yq * _sigmoid(yq)
    zf = proj(4)
    lb = lb_ref[...]
    lf_ref[...] = jnp.log(lb + (1.0 - lb) * _sigmoid(zf))
    kb_ref[...] = (1.0 - lb) * _sigmoid(-zf)
    ih_ref[...] = proj(5)
    yg = proj(6)
    gs_ref[...] = yg * _sigmoid(yg)


def _in_even(x, g_mix, w_bf, pm, qg, kg, lb, *, width, q_scale, tm):
    n, d = x.shape
    assert n % tm == 0
    row = lambda i: (i, 0)
    out = jax.ShapeDtypeStruct((n, width), F32)
    return pl.pallas_call(
        functools.partial(_in_even_kernel, width=width, q_scale=q_scale),
        grid=(n // tm,),
        in_specs=[pl.BlockSpec((tm, d), row), _const_spec((1, d)), _const_spec(w_bf.shape),
                  _const_spec(pm.shape), _const_spec((1, width)), _const_spec((1, width)),
                  _const_spec((1, width))],
        out_specs=[pl.BlockSpec((tm, width), row)] * 8,
        out_shape=[out] * 8,
        compiler_params=_params("arbitrary"),
        name="in_even",
    )(x, g_mix, w_bf, pm, qg, kg, lb)


def _bias_from_buckets(bk, rb_ref, h):
    b = jnp.zeros(bk.shape, F32)
    for u in range(REL_BUCKETS):
        b = jnp.where(bk == u, rb_ref[u, h], b)
    return jnp.where(bk < 0, -jnp.inf, b)


def _lam(lam_ref, lam_init):
    r = lam_ref[...]
    s1 = jnp.sum(r[0:1] * r[1:2], axis=1, keepdims=True)
    s2 = jnp.sum(r[2:3] * r[3:4], axis=1, keepdims=True)
    return jnp.exp(s1) - jnp.exp(s2) + lam_init


def _split_components(q, dh):
    lane = lax.broadcasted_iota(I32, q.shape, 1)
    q0 = jnp.where(lane < dh, q, 0.0)
    q1 = jnp.where(lane >= dh, q, 0.0)
    return jnp.concatenate([q0, q1], axis=0).astype(BF16)


def _attn_prompt_kernel(rb_ref, far_ref, lam_ref, bk_ref, sg_ref, q_ref, k_ref, v_ref, o_ref, *,
                        qt, dh, lam_init):
    h = pl.program_id(1)
    t = q_ref.shape[0]
    kb = k_ref[...].astype(BF16)
    vt = v_ref[...].T.astype(BF16)
    qtr = q_ref[...].T
    sub = lax.broadcasted_iota(I32, (2 * dh, qt), 0)
    bias = []
    for d in range(2):
        b = _bias_from_buckets(bk_ref[d], rb_ref, h)
        bias.append(jnp.concatenate([b, b], axis=1))
    far = rb_ref[far_ref[0], h]
    lam = _lam(lam_ref, lam_init)
    gain = sg_ref[...] * (1.0 - lam_init)
    for i in range(t // qt):
        qi = qtr[:, i * qt:(i + 1) * qt]
        qz = jnp.concatenate([jnp.where(sub < dh, qi, 0.0), jnp.where(sub >= dh, qi, 0.0)],
                             axis=1).astype(BF16)
        n = (i + 1) * qt
        parts = [(n - qt, n, bias[0])]
        if i >= 1:
            parts.append((n - 2 * qt, n - qt, bias[1]))
        if i >= 2:
            parts.append((0, n - 2 * qt, far))
        s = [_dot(kb[lo:hi], qz) + b for lo, hi, b in parts]
        m = functools.reduce(jnp.maximum, [jnp.max(x, axis=0, keepdims=True) for x in s])
        p = [jnp.exp(x - m) for x in s]
        l = functools.reduce(lambda a, b: a + b, [jnp.sum(x, axis=0, keepdims=True) for x in p])
        acc = functools.reduce(lambda a, b: a + b,
                               [_dot(vt[:, lo:hi], x.astype(BF16)) for (lo, hi, _), x in zip(parts, p)])
        o = acc / l
        out = o[:, :qt] - lam * o[:, qt:]
        out = out * lax.rsqrt(jnp.mean(out * out, axis=0, keepdims=True) + EPS) * gain
        o_ref[i * qt:(i + 1) * qt, :] = out.T


def _attn_prompt(q, k, v, rel_bias, lam4, sub_gain, *, batch, seq, heads, dh, lam_init, qt):
    n, w = q.shape
    dv = w // heads
    assert dv == 2 * dh and seq % qt == 0 and qt % CHUNK == 0
    kj = jnp.arange(qt, dtype=I32)[:, None]
    qi = jnp.arange(qt, dtype=I32)[None, :]
    bk0 = jnp.where((kj // CHUNK) <= (qi // CHUNK), rel_bucket(kj - qi), -1)
    bk1 = rel_bucket(kj - qi - qt)
    bk = jnp.stack([bk0, bk1]).astype(I32)
    assert qt + 1 >= REL_MAX_DIST
    far = rel_bucket(jnp.full((1,), -(qt + 1), I32))
    smem = pl.BlockSpec(memory_space=pltpu.SMEM)
    seq_blk = pl.BlockSpec((seq, dv), lambda b, h: (b, h))
    return pl.pallas_call(
        functools.partial(_attn_prompt_kernel, qt=qt, dh=dh, lam_init=lam_init),
        grid=(batch, heads),
        in_specs=[smem, smem, _const_spec(lam4.shape), _const_spec(bk.shape), _const_spec((dv, 1)),
                  seq_blk, seq_blk, seq_blk],
        out_specs=seq_blk,
        out_shape=jax.ShapeDtypeStruct((n, w), F32),
        compiler_params=_params("arbitrary", "arbitrary"),
        name="attn_prompt",
    )(rel_bias, far, lam4, bk, sub_gain.T, q, k, v)


def _attn_sample_kernel(rb_ref, lam_ref, bkc_ref, bkn_ref, sg_ref, q_ref, kc_ref, vc_ref, kn_ref, vn_ref,
                        o_ref, *, t, dh, lam_init):
    h = pl.program_id(1)
    qz = _split_components(q_ref[...], dh)
    bc = _bias_from_buckets(bkc_ref[...], rb_ref, h)
    bn = _bias_from_buckets(bkn_ref[...], rb_ref, h)
    sc = lax.dot_general(qz, kc_ref[0].astype(BF16), NT_DIMS, preferred_element_type=F32)
    sn = lax.dot_general(qz, kn_ref[...].astype(BF16), NT_DIMS, preferred_element_type=F32)
    sc = sc + jnp.concatenate([bc, bc], axis=0)
    sn = sn + jnp.concatenate([bn, bn], axis=0)
    m = jnp.maximum(jnp.max(sc, axis=1, keepdims=True), jnp.max(sn, axis=1, keepdims=True))
    pc = jnp.exp(sc - m)
    pn = jnp.exp(sn - m)
    l = jnp.sum(pc, axis=1, keepdims=True) + jnp.sum(pn, axis=1, keepdims=True)
    acc = _dot(pc.astype(BF16), vc_ref[0].astype(BF16)) + _dot(pn.astype(BF16), vn_ref[...].astype(BF16))
    o = acc / l
    out = o[:t] - _lam(lam_ref, lam_init) * o[t:]
    o_ref[...] = _rms(out, sg_ref[...]) * (1.0 - lam_init)


def _attn_sample(q, k_new, v_new, cache_k, cache_v, rel_bias, lam4, sub_gain, *, batch, t, heads, dh,
                 lam_init):
    n, w = q.shape
    dv = w // heads
    past = cache_k.shape[1]
    assert past % CHUNK == 0 and t <= CHUNK
    qpos = past + jnp.arange(t, dtype=I32)[:, None]
    bkc = rel_bucket(jnp.arange(past, dtype=I32)[None, :] - qpos).astype(I32)
    bkn = rel_bucket(past + jnp.arange(t, dtype=I32)[None, :] - qpos).astype(I32)
    smem = pl.BlockSpec(memory_space=pltpu.SMEM)
    new = pl.BlockSpec((t, dv), lambda b, h: (b, h))
    old = pl.BlockSpec((1, past, dv), lambda b, h: (b, 0, h))
    return pl.pallas_call(
        functools.partial(_attn_sample_kernel, t=t, dh=dh, lam_init=lam_init),
        grid=(batch, heads),
        in_specs=[smem, _const_spec(lam4.shape), _const_spec(bkc.shape), _const_spec(bkn.shape),
                  _const_spec((1, dv)), new, old, old, new, new],
        out_specs=new,
        out_shape=jax.ShapeDtypeStruct((n, w), F32),
        compiler_params=_params("arbitrary", "arbitrary"),
        name="attn_sample",
    )(rel_bias, lam4, bkc, bkn, sub_gain, q, cache_k, cache_v, k_new, v_new)


def _cumsum_rows(x):
    c = x.shape[0]
    row = lax.broadcasted_iota(I32, x.shape, 0)
    s = 1
    while s < c:
        x = x + jnp.where(row >= s, pltpu.roll(x, s, axis=0), 0.0)
        s *= 2
    return x


def _hgrn_kernel(qh_ref, kb_ref, lf_ref, ih_ref, s0_ref, hg_ref, ob_ref, sf_ref, st_s, *,
                 heads, dk, dv, c, nsb):
    t = pl.program_id(1)

    @pl.when(t == 0)
    def _():
        for h in range(heads):
            st_s[h] = s0_ref[0, h].T

    tb = qh_ref.shape[0]
    row = lax.broadcasted_iota(I32, (c, c), 0)
    col = lax.broadcasted_iota(I32, (c, c), 1)
    causal = col <= row

    def chunk(ci, carry):
        r0 = pl.multiple_of(ci * c, c)
        for h in range(heads):
            rows = pl.ds(r0, c)
            q = qh_ref[rows, h * dk:(h + 1) * dk]
            k = kb_ref[rows, h * dk:(h + 1) * dk]
            v = ih_ref[rows, h * dv:(h + 1) * dv]
            b = _cumsum_rows(lf_ref[rows, h * dk:(h + 1) * dk])
            bl = b[c - 1:c]
            st = st_s[h]
            inter = lax.dot_general((q * jnp.exp(b)).astype(BF16), st.astype(BF16), NT_DIMS,
                                    preferred_element_type=F32)
            qs, ks = [], []
            for j in range(nsb):
                ref = b[j * HG_SUB + HG_SUB // 2:j * HG_SUB + HG_SUB // 2 + 1]
                qs.append(q * jnp.exp(b - ref))
                sub = slice(j * HG_SUB, (j + 1) * HG_SUB)
                ks.append(k[sub] * jnp.exp(ref - b[sub]))
            a_full = lax.dot_general(jnp.concatenate(qs, axis=0).astype(BF16),
                                     jnp.concatenate(ks, axis=0).astype(BF16), NT_DIMS,
                                     preferred_element_type=F32)
            att = jnp.zeros((c, c), F32)
            for j in range(nsb):
                att = jnp.where(col >= j * HG_SUB, a_full[j * c:(j + 1) * c], att)
            att = jnp.where(causal, att, 0.0)
            out = inter + _dot(att.astype(BF16), v.astype(BF16))
            ob_ref[rows, h * dv:(h + 1) * dv] = _rms(out, hg_ref[...])
            kdec = (k * jnp.exp(bl - b)).astype(BF16)
            st_s[h] = jnp.exp(bl) * st + lax.dot_general(v.astype(BF16), kdec, TN_DIMS,
                                                         preferred_element_type=F32)
        return carry

    lax.fori_loop(0, tb // c, chunk, 0)

    @pl.when(t == pl.num_programs(1) - 1)
    def _():
        for h in range(heads):
            sf_ref[0, h] = st_s[h].T


def _hgrn(qh, kb, lf, ih, s0, hg_gain, *, batch, seq, heads, dk, dv, tb):
    n = qh.shape[0]
    c = min(CHUNK, seq)
    assert seq % tb == 0 and tb % c == 0 and c % HG_SUB == 0
    nt = seq // tb
    blk = lambda w: pl.BlockSpec((tb, w), lambda b, t: (b * nt + t, 0))
    st = pl.BlockSpec((1, heads, dk, dv), lambda b, t: (b, 0, 0, 0))
    return pl.pallas_call(
        functools.partial(_hgrn_kernel, heads=heads, dk=dk, dv=dv, c=c, nsb=c // HG_SUB),
        grid=(batch, nt),
        in_specs=[blk(heads * dk), blk(heads * dk), blk(heads * dk), blk(heads * dv), st,
                  _const_spec((1, dv))],
        out_specs=[blk(heads * dv), st],
        out_shape=[jax.ShapeDtypeStruct((n, heads * dv), F32),
                   jax.ShapeDtypeStruct((batch, heads, dk, dv), F32)],
        scratch_shapes=[pltpu.VMEM((heads, dv, dk), F32)],
        compiler_params=_params("arbitrary", "arbitrary"),
        name="hgrn",
    )(qh, kb, lf, ih, s0, hg_gain)


def _store_token_tiles(ref, x):
    for c in range(ref.shape[1]):
        ref[:, c, :] = x[:, c * LANES:(c + 1) * LANES]


def _load_token_tiles(ref):
    return jnp.concatenate([ref[:, c, :] for c in range(ref.shape[1])], axis=1)


def _ffn_prologue(x1, gf_ref, wrh_ref, wrl_ref, rbias_ref, x1_ref, xn_ref, lg_ref):
    x1_ref[...] = x1
    xn = _rms(x1, gf_ref[...])
    _store_token_tiles(xn_ref, xn)
    hi = xn.astype(BF16)
    lo = (xn - hi.astype(F32)).astype(BF16)
    nt = functools.partial(lax.dot_general, dimension_numbers=NT_DIMS, preferred_element_type=F32)
    lg_ref[...] = nt(wrh_ref[...], hi) + nt(wrh_ref[...], lo) + nt(wrl_ref[...], hi) + rbias_ref[...]


def _out_even_kernel(oa_ref, ob_ref, gs_ref, x_ref, w_ref, gf_ref, wrh_ref, wrl_ref, rbias_ref,
                     x1_ref, xn_ref, lg_ref):
    o = jnp.concatenate([oa_ref[...], ob_ref[...] * gs_ref[...]], axis=1).astype(BF16)
    x1 = x_ref[...] + _dot(o, w_ref[...])
    _ffn_prologue(x1, gf_ref, wrh_ref, wrl_ref, rbias_ref, x1_ref, xn_ref, lg_ref)


def _out_even(oa, ob, gs, x, w_bf, g_ffn, wrh, wrl, rbias, *, tm):
    n, d = x.shape
    w = oa.shape[1]
    nr = wrh.shape[0]
    row = lambda i: (i, 0)
    return pl.pallas_call(
        _out_even_kernel,
        grid=(n // tm,),
        in_specs=[pl.BlockSpec((tm, w), row), pl.BlockSpec((tm, w), row), pl.BlockSpec((tm, w), row),
                  pl.BlockSpec((tm, d), row), _const_spec(w_bf.shape), _const_spec((1, d)),
                  _const_spec(wrh.shape), _const_spec(wrl.shape), _const_spec(rbias.shape)],
        out_specs=[pl.BlockSpec((tm, d), row), pl.BlockSpec((tm, d // LANES, LANES), lambda i: (i, 0, 0)),
                   pl.BlockSpec((nr, tm), lambda i: (0, i))],
        out_shape=[jax.ShapeDtypeStruct((n, d), F32), jax.ShapeDtypeStruct((n, d // LANES, LANES), F32),
                   jax.ShapeDtypeStruct((nr, n), F32)],
        compiler_params=_params("arbitrary"),
        name="out_even",
    )(oa, ob, gs, x, w_bf, g_ffn, wrh, wrl, rbias)


def _gelu(x):
    return 0.5 * x * (1.0 + jnp.tanh(math.sqrt(2.0 / math.pi) * (x + 0.044715 * (x * x * x))))


def _odd_kernel(x_ref, gm_ref, win_ref, vg_ref, wsp_ref, bsp_ref, wout_ref, gf_ref, wrh_ref, wrl_ref,
                rbias_ref, x1_ref, xn_ref, lg_ref, *rest, half, groups, l, emit_v):
    if emit_v:
        vn_ref, u_s, s_s = rest
    else:
        vn_ref = None
        u_s, vn_s, s_s = rest
    tm = x_ref.shape[0]
    x = x_ref[...]
    xn = _rms(x, gm_ref[...]).astype(BF16)
    cw = 512
    vbuf = vn_ref if emit_v else vn_s
    for cidx in range(half // cw):
        u_s[:, cidx * cw:(cidx + 1) * cw] = _gelu(_dot(xn, win_ref[:, cidx * cw:(cidx + 1) * cw]))
        vbuf[:, cidx * cw:(cidx + 1) * cw] = _gelu(
            _dot(xn, win_ref[:, half + cidx * cw:half + (cidx + 1) * cw]))
    vbuf[...] = _rms(vbuf[...], vg_ref[...])
    gw = half // groups
    row = lax.broadcasted_iota(I32, (l, l), 0)
    col = lax.broadcasted_iota(I32, (l, l), 1)
    for g in range(groups):
        wg = jnp.where(col <= row, wsp_ref[g], 0.0).astype(BF16)
        bg = bsp_ref[:, g:g + 1]
        for ci in range(tm // l):
            vv = vbuf[ci * l:(ci + 1) * l, g * gw:(g + 1) * gw].astype(BF16)
            s_s[ci * l:(ci + 1) * l, g * gw:(g + 1) * gw] = _dot(wg, vv) + bg
    y = _dot((u_s[...] * s_s[...]).astype(BF16), wout_ref[...])
    _ffn_prologue(x + y, gf_ref, wrh_ref, wrl_ref, rbias_ref, x1_ref, xn_ref, lg_ref)


def _odd_mixer(x, g_mix, win_bf, v_gain, wsp, bsp_t, wout_bf, g_ffn, wrh, wrl, rbias, *, l, tm, emit_v):
    n, d = x.shape
    half = wout_bf.shape[0]
    groups = wsp.shape[0]
    nr = wrh.shape[0]
    assert n % tm == 0 and tm % l == 0
    row = lambda i: (i, 0)
    out_specs = [pl.BlockSpec((tm, d), row), pl.BlockSpec((tm, d // LANES, LANES), lambda i: (i, 0, 0)),
                 pl.BlockSpec((nr, tm), lambda i: (0, i))]
    out_shape = [jax.ShapeDtypeStruct((n, d), F32), jax.ShapeDtypeStruct((n, d // LANES, LANES), F32),
                 jax.ShapeDtypeStruct((nr, n), F32)]
    scratch = [pltpu.VMEM((tm, half), F32)]
    if emit_v:
        out_specs.append(pl.BlockSpec((tm, half), row))
        out_shape.append(jax.ShapeDtypeStruct((n, half), F32))
    else:
        scratch.append(pltpu.VMEM((tm, half), F32))
    scratch.append(pltpu.VMEM((tm, half), F32))
    return pl.pallas_call(
        functools.partial(_odd_kernel, half=half, groups=groups, l=l, emit_v=emit_v),
        grid=(n // tm,),
        in_specs=[pl.BlockSpec((tm, d), row), _const_spec((1, d)), _const_spec(win_bf.shape),
                  _const_spec((1, half)), _const_spec(wsp.shape), _const_spec(bsp_t.shape),
                  _const_spec(wout_bf.shape), _const_spec((1, d)), _const_spec(wrh.shape),
                  _const_spec(wrl.shape), _const_spec(rbias.shape)],
        out_specs=out_specs,
        out_shape=out_shape,
        scratch_shapes=scratch,
        compiler_params=_params("arbitrary"),
        name="odd_mixer",
    )(x, g_mix, win_bf, v_gain, wsp, bsp_t, wout_bf, g_ffn, wrh, wrl, rbias)


def _route_kernel(lg_ref, tri_ref, e_ref, g_ref, r_ref, cnt_ref, run_s, *, groups, epg):
    i = pl.program_id(0)

    @pl.when(i == 0)
    def _():
        run_s[...] = jnp.zeros(run_s.shape, F32)

    lg = lg_ref[...]
    tr = lg.shape[1]
    gl = [lg[g:g + 1] for g in range(groups)]
    m = functools.reduce(jnp.maximum, gl)
    grp = jnp.full((1, tr), groups - 1, I32)
    for g in range(groups - 2, -1, -1):
        grp = jnp.where(gl[g] == m, g, grp)
    gate_g = 1.0 / functools.reduce(lambda a, b: a + b, [jnp.exp(x - m) for x in gl])
    sel = lg[SUBLANES + (groups - 1) * epg:SUBLANES + groups * epg]
    for g in range(groups - 2, -1, -1):
        sel = jnp.where(grp == g, lg[SUBLANES + g * epg:SUBLANES + (g + 1) * epg], sel)
    sub = lax.broadcasted_iota(I32, sel.shape, 0)
    v1 = jnp.max(sel, axis=0, keepdims=True)
    i1 = jnp.min(jnp.where(sel == v1, sub, epg), axis=0, keepdims=True)
    sel2 = jnp.where(sub == i1, -jnp.inf, sel)
    v2 = jnp.max(sel2, axis=0, keepdims=True)
    i2 = jnp.min(jnp.where(sel2 == v2, sub, epg), axis=0, keepdims=True)
    tt = jnp.exp(v2 - v1)
    g1 = gate_g / (1.0 + tt)
    g2 = gate_g * tt / (1.0 + tt)
    e1 = grp * epg + i1
    e2 = grp * epg + i2
    ne = groups * epg
    eidx = lax.broadcasted_iota(I32, (ne, tr), 0)
    oh1 = eidx == e1
    oh2 = eidx == e2
    cnt = jnp.where(oh1, 1.0, 0.0) + jnp.where(oh2, 1.0, 0.0)
    before = run_s[:, 0:1] + _dot(cnt.astype(BF16), tri_ref[...])
    r1 = jnp.sum(jnp.where(oh1, before, 0.0), axis=0, keepdims=True)
    r2 = jnp.sum(jnp.where(oh2, before, 0.0), axis=0, keepdims=True)
    run_s[...] = run_s[...] + jnp.sum(cnt, axis=1, keepdims=True)
    rows = lax.broadcasted_iota(I32, (SUBLANES, tr), 0)
    e_ref[...] = jnp.where(rows == 0, e1, jnp.where(rows == 1, e2, 0))
    g_ref[...] = jnp.where(rows == 0, g1, jnp.where(rows == 1, g2, 0.0))
    r_ref[...] = jnp.where(rows == 0, r1, jnp.where(rows == 1, r2, 0.0)).astype(I32)
    cnt_ref[...] = run_s[...]


def _route(lgt, *, groups, epg, tr):
    nr, n = lgt.shape
    assert n % tr == 0 and nr == SUBLANES + groups * epg
    ne = groups * epg
    tri = (jnp.arange(tr)[:, None] < jnp.arange(tr)[None, :]).astype(BF16)
    tok = pl.BlockSpec((SUBLANES, tr), lambda i: (0, i))
    return pl.pallas_call(
        functools.partial(_route_kernel, groups=groups, epg=epg),
        grid=(n // tr,),
        in_specs=[pl.BlockSpec((nr, tr), lambda i: (0, i)), _const_spec((tr, tr))],
        out_specs=[tok, tok, tok, pl.BlockSpec((ne, LANES), lambda i: (0, 0))],
        out_shape=[jax.ShapeDtypeStruct((SUBLANES, n), I32), jax.ShapeDtypeStruct((SUBLANES, n), F32),
                   jax.ShapeDtypeStruct((SUBLANES, n), I32), jax.ShapeDtypeStruct((ne, LANES), F32)],
        scratch_shapes=[pltpu.VMEM((ne, LANES), F32)],
        compiler_params=_params("arbitrary"),
        name="route",
    )(lgt, tri)


ISSUE_UNROLL = 8


def _dispatch_kernel(seg_ref, dst_ref, x_ref, buf_ref, zero_s, sem, zsem):
    tp = x_ref.shape[0]

    @pl.when(pl.program_id(0) == 0)
    def _():
        zero_s[...] = jnp.zeros(zero_s.shape, F32)

        def pad_copy(e):
            return pltpu.make_async_copy(
                zero_s, buf_ref.at[pl.ds(seg_ref[0, e] - MOE_BLOCK, MOE_BLOCK)], zsem)

        for e in range(seg_ref.shape[1]):
            @pl.when(seg_ref[1, e] > 0)
            def _():
                pad_copy(e).start()
        for e in range(seg_ref.shape[1]):
            @pl.when(seg_ref[1, e] > 0)
            def _():
                pad_copy(e).wait()

        def tail_copy(b):
            return pltpu.make_async_copy(zero_s, buf_ref.at[pl.ds(b * MOE_BLOCK, MOE_BLOCK)], zsem)

        def tail_start(b, c):
            tail_copy(b).start()
            return c

        def tail_wait(b, c):
            tail_copy(b).wait()
            return c

        first_unused = seg_ref[0, seg_ref.shape[1] - 1] // MOE_BLOCK
        n_blocks = buf_ref.shape[0] // MOE_BLOCK
        lax.fori_loop(first_unused, n_blocks, tail_start, 0)
        lax.fori_loop(first_unused, n_blocks, tail_wait, 0)

    def issue(r0, c):
        for u in range(ISSUE_UNROLL):
            r = r0 * ISSUE_UNROLL + u
            for kk in range(TOP_K):
                pltpu.make_async_copy(x_ref.at[r], buf_ref.at[dst_ref[0, kk, r]], sem).start()
        return c

    lax.fori_loop(0, tp // ISSUE_UNROLL, issue, 0)
    for kk in range(TOP_K):
        pltpu.make_async_copy(x_ref, buf_ref.at[pl.ds(0, tp)], sem).wait()


def _dispatch(seg, dest3, xn, n_slots, *, tp):
    n, s, lanes = xn.shape
    assert tp % ISSUE_UNROLL == 0
    grid_spec = pltpu.PrefetchScalarGridSpec(
        num_scalar_prefetch=1,
        grid=(n // tp,),
        in_specs=[pl.BlockSpec((1, TOP_K, tp), lambda i, sg: (i, 0, 0), memory_space=pltpu.SMEM),
                  pl.BlockSpec((tp, s, lanes), lambda i, sg: (i, 0, 0))],
        out_specs=pl.BlockSpec(memory_space=pl.ANY),
        scratch_shapes=[pltpu.VMEM((MOE_BLOCK, s, lanes), F32), pltpu.SemaphoreType.DMA(()),
                        pltpu.SemaphoreType.DMA(())],
    )
    return pl.pallas_call(
        _dispatch_kernel,
        grid_spec=grid_spec,
        out_shape=jax.ShapeDtypeStruct((n_slots, s, lanes), F32),
        compiler_params=_params("arbitrary"),
        name="dispatch",
    )(seg, dest3, xn)


def _expert_kernel(be_ref, nu_ref, x_ref, wg_ref, wu_ref, wd_ref, o_ref, wg_s, wu_s, wd_s):
    b = pl.program_id(0)
    prev = be_ref[jnp.maximum(b - 1, 0)]

    @pl.when((b == 0) | (be_ref[b] != prev))
    def _():
        wg_s[...] = wg_ref[0, 0].astype(BF16)
        wu_s[...] = wu_ref[0, 0].astype(BF16)
        wd_s[...] = wd_ref[0, 0].astype(BF16)

    @pl.when(b < nu_ref[0])
    def _():
        xb = _load_token_tiles(x_ref).astype(BF16)
        gate = _dot(xb, wg_s[...])
        h = gate * _sigmoid(gate) * _dot(xb, wu_s[...])
        _store_token_tiles(o_ref, _dot(h.astype(BF16), wd_s[...]))

    @pl.when(b >= nu_ref[0])
    def _():
        o_ref[...] = jnp.zeros(o_ref.shape, F32)


def _experts(blk_expert, n_used, buf, w_gate, w_up, w_down, *, layer):
    n_slots, s, lanes = buf.shape
    d = s * lanes
    de = w_gate.shape[3]
    n_blocks = n_slots // MOE_BLOCK
    rows = pl.BlockSpec((MOE_BLOCK, s, lanes), lambda b, be, nu: (b, 0, 0))
    used_rows = pl.BlockSpec((MOE_BLOCK, s, lanes), lambda b, be, nu: (jnp.minimum(b, nu[0] - 1), 0, 0))
    grid_spec = pltpu.PrefetchScalarGridSpec(
        num_scalar_prefetch=2,
        grid=(n_blocks,),
        in_specs=[used_rows,
                  pl.BlockSpec((1, 1, d, de), lambda b, be, nu: (layer, be[b], 0, 0)),
                  pl.BlockSpec((1, 1, d, de), lambda b, be, nu: (layer, be[b], 0, 0)),
                  pl.BlockSpec((1, 1, de, d), lambda b, be, nu: (layer, be[b], 0, 0))],
        out_specs=rows,
        scratch_shapes=[pltpu.VMEM((d, de), BF16), pltpu.VMEM((d, de), BF16), pltpu.VMEM((de, d), BF16)],
    )
    return pl.pallas_call(
        _expert_kernel,
        grid_spec=grid_spec,
        out_shape=jax.ShapeDtypeStruct((n_slots, s, lanes), F32),
        compiler_params=_params("arbitrary"),
        name="experts",
    )(blk_expert, n_used, buf, w_gate, w_up, w_down)


def _combine_kernel(dst_ref, gate_ref, x_ref, yb_ref, o_ref, rows_s, sem):
    tq = x_ref.shape[0]

    def issue(r0, c):
        for u in range(ISSUE_UNROLL):
            r = r0 * ISSUE_UNROLL + u
            for kk in range(TOP_K):
                pltpu.make_async_copy(yb_ref.at[dst_ref[0, kk, r]], rows_s.at[kk, r], sem).start()
        return c

    lax.fori_loop(0, tq // ISSUE_UNROLL, issue, 0)
    for kk in range(TOP_K):
        pltpu.make_async_copy(yb_ref.at[pl.ds(0, tq)], rows_s.at[kk], sem).wait()
    g = gate_ref[...]
    o_ref[...] = (x_ref[...] + g[:, 0:1] * _load_token_tiles(rows_s.at[0])
                  + g[:, 1:2] * _load_token_tiles(rows_s.at[1]))


def _combine(dest3, gates_t, x1, yb, *, tq):
    n, d = x1.shape
    _, s, lanes = yb.shape
    assert tq % ISSUE_UNROLL == 0
    return pl.pallas_call(
        _combine_kernel,
        grid=(n // tq,),
        in_specs=[pl.BlockSpec((1, TOP_K, tq), lambda i: (i, 0, 0), memory_space=pltpu.SMEM),
                  pl.BlockSpec((tq, TOP_K), lambda i: (i, 0)),
                  pl.BlockSpec((tq, d), lambda i: (i, 0)),
                  pl.BlockSpec(memory_space=pl.ANY)],
        out_specs=pl.BlockSpec((tq, d), lambda i: (i, 0)),
        out_shape=jax.ShapeDtypeStruct((n, d), F32),
        scratch_shapes=[pltpu.VMEM((TOP_K, tq, s, lanes), F32), pltpu.SemaphoreType.DMA(())],
        compiler_params=_params("arbitrary"),
        name="combine",
    )(dest3, gates_t, x1, yb)


def _moe(x1, xn, lgt, w_gate, w_up, w_down, *, layer, groups, epg):
    n, d = x1.shape
    ne = groups * epg
    tile = min(256, n)
    e8, g8, r8, cnt = _route(lgt, groups=groups, epg=epg, tr=min(512, n))
    counts = cnt[:, 0].astype(I32)
    padded = (counts + MOE_BLOCK - 1) // MOE_BLOCK * MOE_BLOCK
    pend = jnp.cumsum(padded)
    pstart = pend - padded
    n_blocks = -(-(n * TOP_K) // MOE_BLOCK) + ne
    eids = jnp.arange(ne, dtype=I32)
    seg = jnp.sum(jnp.where(e8[:TOP_K, :, None] == eids, pstart, 0), axis=-1)
    dest = seg + r8[:TOP_K]
    dest3 = dest.reshape(TOP_K, n // tile, tile).transpose(1, 0, 2)
    blk_row = jnp.arange(n_blocks, dtype=I32)[:, None] * MOE_BLOCK
    blk_expert = jnp.minimum(jnp.sum((pend[None, :] <= blk_row).astype(I32), axis=1), ne - 1)
    n_used = (pend[-1:] // MOE_BLOCK).astype(I32)
    buf = _dispatch(jnp.stack([pend, padded]).astype(I32), dest3, xn, n_blocks * MOE_BLOCK, tp=tile)
    yb = _experts(blk_expert, n_used, buf, w_gate, w_up, w_down, layer=layer)
    return _combine(dest3, g8[:TOP_K].T, x1, yb, tq=tile)


def _router_weights(wg, bg, we, be):
    d, groups = wg.shape
    epg = we.shape[2]
    assert groups <= SUBLANES and epg == SUBLANES
    pad = jnp.zeros((SUBLANES - groups, d), F32)
    wr = jnp.concatenate([wg.T, pad, we.transpose(0, 2, 1).reshape(groups * epg, d)], axis=0)
    rb = jnp.concatenate([bg, jnp.zeros((SUBLANES - groups,), F32), be.reshape(-1)])[:, None]
    hi = wr.astype(BF16)
    lo = (wr - hi.astype(F32)).astype(BF16)
    return hi, lo, rb, groups, epg


def kernel(x_prompt, x_sample, cache_attn_k, cache_attn_v, state_hgrn, rel_bias, norm_mix, norm_ffn,
           w_in_even, w_out_even, q_norm_gain, k_norm_gain, lam_q1, lam_k1, lam_q2, lam_k2, da_out_gain,
           hgrn_lb_logits, hgrn_out_gain, w_in_odd, sgu_v_gain, sgu_w, sgu_b, w_out_odd,
           router_group_w, router_group_b, router_expert_w, router_expert_b,
           expert_w_gate, expert_w_up, expert_w_down):
    bp, tp, d = x_prompt.shape
    bs, ts, _ = x_sample.shape
    depth = norm_mix.shape[0]
    _, _, past, da_heads, _, da_dh = cache_attn_k.shape
    da_dv = cache_attn_v.shape[-1]
    _, _, hg_heads, hg_dk, hg_dv = state_hgrn.shape
    width = da_heads * da_dv
    assert width == da_heads * 2 * da_dh == hg_heads * hg_dk == hg_heads * hg_dv
    assert da_dv == LANES and hg_dk == LANES and hg_dv == LANES

    lb_all = jnp.cumsum(jax.nn.softmax(hgrn_lb_logits.astype(F32), axis=0), axis=0)
    gid = jnp.arange(width) // da_dh
    pm = jnp.where(gid[:, None] == gid[None, :], 1.0 / da_dh, 0.0).astype(BF16)

    xs = {"p": x_prompt.reshape(bp * tp, d), "s": x_sample.reshape(bs * ts, d)}
    dims = {"p": (bp, tp), "s": (bs, ts)}
    outs = {"p": {}, "s": {}}
    kp_l, vp_l, ks_l, vs_l, sp_l, ss_l, sgu_l = [], [], [], [], [], [], []

    for layer in range(depth):
        j = layer // 2
        wrh, wrl, rbias, groups, epg = _router_weights(
            router_group_w[layer], router_group_b[layer], router_expert_w[layer], router_expert_b[layer])
        g_mix = norm_mix[layer][None, :]
        g_ffn = norm_ffn[layer][None, :]
        if layer % 2 == 0:
            lam_init = 0.8 - 0.6 * math.exp(-0.3 * layer)
            w_in_bf = w_in_even[j].astype(BF16)
            w_out_bf = w_out_even[j].astype(BF16)
            reps = width // da_dh
            qg = jnp.tile(q_norm_gain[j], reps)[None, :]
            kg = jnp.tile(k_norm_gain[j], reps)[None, :]
            lam4 = jnp.stack([lam_q1[j], lam_k1[j], lam_q2[j], lam_k2[j]])
            sub_gain = da_out_gain[j][None, :]
            hg_gain = hgrn_out_gain[j][None, :]
            lb = lb_all[j][None, :]
            for key in ("p", "s"):
                b, t = dims[key]
                x = xs[key]
                q, k, v, qh, kb, lf, ih, gs = _in_even(
                    x, g_mix, w_in_bf, pm, qg, kg, lb, width=width, q_scale=da_dh ** -0.5, tm=min(512, b * t))
                if key == "p":
                    oa = _attn_prompt(q, k, v, rel_bias, lam4, sub_gain, batch=b, seq=t, heads=da_heads,
                                      dh=da_dh, lam_init=lam_init, qt=min(256, t))
                    s0 = jnp.zeros((b, hg_heads, hg_dk, hg_dv), F32)
                    ob, s_new = _hgrn(qh, kb, lf, ih, s0, hg_gain, batch=b, seq=t, heads=hg_heads,
                                      dk=hg_dk, dv=hg_dv, tb=min(512, t))
                    kp_l.append(k.reshape(b, t, da_heads, 2, da_dh))
                    vp_l.append(v.reshape(b, t, da_heads, da_dv))
                    sp_l.append(s_new)
                else:
                    ck = cache_attn_k[j].reshape(b, past, width)
                    cv = cache_attn_v[j].reshape(b, past, width)
                    oa = _attn_sample(q, k, v, ck, cv, rel_bias, lam4, sub_gain, batch=b, t=t,
                                      heads=da_heads, dh=da_dh, lam_init=lam_init)
                    ob, s_new = _hgrn(qh, kb, lf, ih, state_hgrn[j], hg_gain, batch=b, seq=t,
                                      heads=hg_heads, dk=hg_dk, dv=hg_dv, tb=t)
                    ks_l.append(k.reshape(b, t, da_heads, 2, da_dh))
                    vs_l.append(v.reshape(b, t, da_heads, da_dv))
                    ss_l.append(s_new)
                outs[key] = _out_even(oa, ob, gs, x, w_out_bf, g_ffn, wrh, wrl, rbias, tm=min(512, b * t))
        else:
            w_in_bf = w_in_odd[j].astype(BF16)
            w_out_bf = w_out_odd[j].astype(BF16)
            v_gain = sgu_v_gain[j][None, :]
            for key in ("p", "s"):
                b, t = dims[key]
                l = min(SGU_CHUNK, t)
                res = _odd_mixer(xs[key], g_mix, w_in_bf, v_gain, sgu_w[j][:, :l, :l], sgu_b[j][:, :l].T,
                                 w_out_bf, g_ffn, wrh, wrl, rbias, l=l, tm=min(256, b * t), emit_v=(key == "s"))
                outs[key] = res[:3]
                if key == "s":
                    sgu_l.append(res[3].reshape(b, t, -1))
        for key in ("p", "s"):
            x1, xn, lgt = outs[key]
            xs[key] = _moe(x1, xn, lgt, expert_w_gate, expert_w_up, expert_w_down, layer=layer,
                           groups=groups, epg=epg)

    return (xs["p"].reshape(bp, tp, d), xs["s"].reshape(bs, ts, d), jnp.stack(kp_l), jnp.stack(vp_l),
            jnp.stack(ks_l), jnp.stack(vs_l), jnp.stack(sp_l), jnp.stack(ss_l), jnp.stack(sgu_l))
```

```python
import functools
import math

import jax
import jax.numpy as jnp
from jax import lax
from jax.experimental import pallas as pl
from jax.experimental.pallas import tpu as pltpu

F32 = jnp.float32
BF16 = jnp.bfloat16
I32 = jnp.int32

EPS = 1e-6
CHUNK = 64
SGU_CHUNK = 128
REL_BUCKETS = 32
REL_MAX_DIST = 128
TOP_K = 2
MOE_BLOCK = 256
HG_SUB = 16

LANES = 128
SUBLANES = 8
VMEM_LIMIT = 56 * 1024 * 1024

NT_DIMS = (((1,), (1,)), ((), ()))
TN_DIMS = (((0,), (0,)), ((), ()))


def _params(*sem):
    return pltpu.CompilerParams(dimension_semantics=sem, vmem_limit_bytes=VMEM_LIMIT)


def _const_spec(shape):
    nd = len(shape)
    return pl.BlockSpec(shape, lambda *_: (0,) * nd, pipeline_mode=pl.Buffered(1))


def _sigmoid(x):
    return 1.0 / (1.0 + jnp.exp(-x))


def _rms(x, g):
    return x * lax.rsqrt(jnp.mean(x * x, axis=-1, keepdims=True) + EPS) * g


def _dot(a, b):
    return jnp.dot(a, b, preferred_element_type=F32)


def rel_bucket(rel):
    half = REL_BUCKETS // 2
    max_exact = half // 2
    ret = (rel > 0).astype(I32) * half
    n = jnp.abs(rel)
    nf = jnp.maximum(n, 1).astype(F32)
    large = max_exact + (jnp.log(nf / max_exact) / math.log(REL_MAX_DIST / max_exact)
                         * (half - max_exact)).astype(I32)
    large = jnp.minimum(large, half - 1)
    return ret + jnp.where(n < max_exact, n, large)


def _in_even_kernel(x_ref, g_ref, w_ref, pm_ref, qg_ref, kg_ref, lb_ref,
                    q_ref, k_ref, v_ref, qh_ref, kb_ref, lf_ref, ih_ref, gs_ref, *, width, q_scale):
    xn = _rms(x_ref[...], g_ref[...]).astype(BF16)

    def proj(c):
        return _dot(xn, w_ref[:, c * width:(c + 1) * width])

    def group_norm(y, gain):
        ms = _dot((y * y).astype(BF16), pm_ref[...])
        return y * lax.rsqrt(ms + EPS) * gain

    q_ref[...] = group_norm(proj(0), qg_ref[...]) * q_scale
    k_ref[...] = group_norm(proj(1), kg_ref[...])
    v_ref[...] = proj(2)
    yq = proj(3)
    qh_ref[...] = yq * _sigmoid(yq)
    zf = proj(4)
    lb = lb_ref[...]
    lf_ref[...] = jnp.log(lb + (1.0 - lb) * _sigmoid(zf))
    kb_ref[...] = (1.0 - lb) * _sigmoid(-zf)
    ih_ref[...] = proj(5)
    yg = proj(6)
    gs_ref[...] = yg * _sigmoid(yg)


def _in_even(x, g_mix, w_bf, pm, qg, kg, lb, *, width, q_scale, tm):
    n, d = x.shape
    assert n % tm == 0
    row = lambda i: (i, 0)
    out = jax.ShapeDtypeStruct((n, width), F32)
    return pl.pallas_call(
        functools.partial(_in_even_kernel, width=width, q_scale=q_scale),
        grid=(n // tm,),
        in_specs=[pl.BlockSpec((tm, d), row), _const_spec((1, d)), _const_spec(w_bf.shape),
                  _const_spec(pm.shape), _const_spec((1, width)), _const_spec((1, width)),
                  _const_spec((1, width))],
        out_specs=[pl.BlockSpec((tm, width), row)] * 8,
        out_shape=[out] * 8,
        compiler_params=_params("arbitrary"),
        name="in_even",
    )(x, g_mix, w_bf, pm, qg, kg, lb)


def _bias_from_buckets(bk, rb_ref, h):
    b = jnp.zeros(bk.shape, F32)
    for u in range(REL_BUCKETS):
        b = jnp.where(bk == u, rb_ref[u, h], b)
    return jnp.where(bk < 0, -jnp.inf, b)


def _lam(lam_ref, lam_init):
    r = lam_ref[...]
    s1 = jnp.sum(r[0:1] * r[1:2], axis=1, keepdims=True)
    s2 = jnp.sum(r[2:3] * r[3:4], axis=1, keepdims=True)
    return jnp.exp(s1) - jnp.exp(s2) + lam_init


def _split_components(q, dh):
    lane = lax.broadcasted_iota(I32, q.shape, 1)
    q0 = jnp.where(lane < dh, q, 0.0)
    q1 = jnp.where(lane >= dh, q, 0.0)
    return jnp.concatenate([q0, q1], axis=0).astype(BF16)


def _attn_prompt_kernel(rb_ref, far_ref, lam_ref, bk_ref, sg_ref, q_ref, k_ref, v_ref, o_ref, *,
                        qt, dh, lam_init):
    h = pl.program_id(1)
    t = q_ref.shape[0]
    kb = k_ref[...].astype(BF16)
    vt = v_ref[...].T.astype(BF16)
    qtr = q_ref[...].T
    sub = lax.broadcasted_iota(I32, (2 * dh, qt), 0)
    bias = []
    for d in range(2):
        b = _bias_from_buckets(bk_ref[d], rb_ref, h)
        bias.append(jnp.concatenate([b, b], axis=1))
    far = rb_ref[far_ref[0], h]
    lam = _lam(lam_ref, lam_init)
    gain = sg_ref[...] * (1.0 - lam_init)
    for i in range(t // qt):
        qi = qtr[:, i * qt:(i + 1) * qt]
        qz = jnp.concatenate([jnp.where(sub < dh, qi, 0.0), jnp.where(sub >= dh, qi, 0.0)],
                             axis=1).astype(BF16)
        n = (i + 1) * qt
        parts = [(n - qt, n, bias[0])]
        if i >= 1:
            parts.append((n - 2 * qt, n - qt, bias[1]))
        if i >= 2:
            parts.append((0, n - 2 * qt, far))
        s = [_dot(kb[lo:hi], qz) + b for lo, hi, b in parts]
        m = functools.reduce(jnp.maximum, [jnp.max(x, axis=0, keepdims=True) for x in s])
        p = [jnp.exp(x - m) for x in s]
        l = functools.reduce(lambda a, b: a + b, [jnp.sum(x, axis=0, keepdims=True) for x in p])
        acc = functools.reduce(lambda a, b: a + b,
                               [_dot(vt[:, lo:hi], x.astype(BF16)) for (lo, hi, _), x in zip(parts, p)])
        o = acc / l
        out = o[:, :qt] - lam * o[:, qt:]
        out = out * lax.rsqrt(jnp.mean(out * out, axis=0, keepdims=True) + EPS) * gain
        o_ref[i * qt:(i + 1) * qt, :] = out.T


def _attn_prompt(q, k, v, rel_bias, lam4, sub_gain, *, batch, seq, heads, dh, lam_init, qt):
    n, w = q.shape
    dv = w // heads
    assert dv == 2 * dh and seq % qt == 0 and qt % CHUNK == 0
    kj = jnp.arange(qt, dtype=I32)[:, None]
    qi = jnp.arange(qt, dtype=I32)[None, :]
    bk0 = jnp.where((kj // CHUNK) <= (qi // CHUNK), rel_bucket(kj - qi), -1)
    bk1 = rel_bucket(kj - qi - qt)
    bk = jnp.stack([bk0, bk1]).astype(I32)
    assert qt + 1 >= REL_MAX_DIST
    far = rel_bucket(jnp.full((1,), -(qt + 1), I32))
    smem = pl.BlockSpec(memory_space=pltpu.SMEM)
    seq_blk = pl.BlockSpec((seq, dv), lambda b, h: (b, h))
    return pl.pallas_call(
        functools.partial(_attn_prompt_kernel, qt=qt, dh=dh, lam_init=lam_init),
        grid=(batch, heads),
        in_specs=[smem, smem, _const_spec(lam4.shape), _const_spec(bk.shape), _const_spec((dv, 1)),
                  seq_blk, seq_blk, seq_blk],
        out_specs=seq_blk,
        out_shape=jax.ShapeDtypeStruct((n, w), F32),
        compiler_params=_params("arbitrary", "arbitrary"),
        name="attn_prompt",
    )(rel_bias, far, lam4, bk, sub_gain.T, q, k, v)


def _attn_sample_kernel(rb_ref, lam_ref, bkc_ref, bkn_ref, sg_ref, q_ref, kc_ref, vc_ref, kn_ref, vn_ref,
                        o_ref, *, t, dh, lam_init):
    h = pl.program_id(1)
    qz = _split_components(q_ref[...], dh)
    bc = _bias_from_buckets(bkc_ref[...], rb_ref, h)
    bn = _bias_from_buckets(bkn_ref[...], rb_ref, h)
    sc = lax.dot_general(qz, kc_ref[0].astype(BF16), NT_DIMS, preferred_element_type=F32)
    sn = lax.dot_general(qz, kn_ref[...].astype(BF16), NT_DIMS, preferred_element_type=F32)
    sc = sc + jnp.concatenate([bc, bc], axis=0)
    sn = sn + jnp.concatenate([bn, bn], axis=0)
    m = jnp.maximum(jnp.max(sc, axis=1, keepdims=True), jnp.max(sn, axis=1, keepdims=True))
    pc = jnp.exp(sc - m)
    pn = jnp.exp(sn - m)
    l = jnp.sum(pc, axis=1, keepdims=True) + jnp.sum(pn, axis=1, keepdims=True)
    acc = _dot(pc.astype(BF16), vc_ref[0].astype(BF16)) + _dot(pn.astype(BF16), vn_ref[...].astype(BF16))
    o = acc / l
    out = o[:t] - _lam(lam_ref, lam_init) * o[t:]
    o_ref[...] = _rms(out, sg_ref[...]) * (1.0 - lam_init)


def _attn_sample(q, k_new, v_new, cache_k, cache_v, rel_bias, lam4, sub_gain, *, batch, t, heads, dh,
                 lam_init):
    n, w = q.shape
    dv = w // heads
    past = cache_k.shape[1]
    assert past % CHUNK == 0 and t <= CHUNK
    qpos = past + jnp.arange(t, dtype=I32)[:, None]
    bkc = rel_bucket(jnp.arange(past, dtype=I32)[None, :] - qpos).astype(I32)
    bkn = rel_bucket(past + jnp.arange(t, dtype=I32)[None, :] - qpos).astype(I32)
    smem = pl.BlockSpec(memory_space=pltpu.SMEM)
    new = pl.BlockSpec((t, dv), lambda b, h: (b, h))
    old = pl.BlockSpec((1, past, dv), lambda b, h: (b, 0, h))
    return pl.pallas_call(
        functools.partial(_attn_sample_kernel, t=t, dh=dh, lam_init=lam_init),
        grid=(batch, heads),
        in_specs=[smem, _const_spec(lam4.shape), _const_spec(bkc.shape), _const_spec(bkn.shape),
                  _const_spec((1, dv)), new, old, old, new, new],
        out_specs=new,
        out_shape=jax.ShapeDtypeStruct((n, w), F32),
        compiler_params=_params("arbitrary", "arbitrary"),
        name="attn_sample",
    )(rel_bias, lam4, bkc, bkn, sub_gain, q, cache_k, cache_v, k_new, v_new)


def _cumsum_rows(x):
    c = x.shape[0]
    row = lax.broadcasted_iota(I32, x.shape, 0)
    s = 1
    while s < c:
        x = x + jnp.where(row >= s, pltpu.roll(x, s, axis=0), 0.0)
        s *= 2
    return x


def _hgrn_kernel(qh_ref, kb_ref, lf_ref, ih_ref, s0_ref, hg_ref, ob_ref, sf_ref, st_s, *,
                 heads, dk, dv, c, nsb):
    t = pl.program_id(1)

    @pl.when(t == 0)
    def _():
        for h in range(heads):
            st_s[h] = s0_ref[0, h].T

    tb = qh_ref.shape[0]
    row = lax.broadcasted_iota(I32, (c, c), 0)
    col = lax.broadcasted_iota(I32, (c, c), 1)
    causal = col <= row

    def chunk(ci, carry):
        r0 = pl.multiple_of(ci * c, c)
        for h in range(heads):
            rows = pl.ds(r0, c)
            q = qh_ref[rows, h * dk:(h + 1) * dk]
            k = kb_ref[rows, h * dk:(h + 1) * dk]
            v = ih_ref[rows, h * dv:(h + 1) * dv]
            b = _cumsum_rows(lf_ref[rows, h * dk:(h + 1) * dk])
            bl = b[c - 1:c]
            st = st_s[h]
            inter = lax.dot_general((q * jnp.exp(b)).astype(BF16), st.astype(BF16), NT_DIMS,
                                    preferred_element_type=F32)
            qs, ks = [], []
            for j in range(nsb):
                ref = b[j * HG_SUB + HG_SUB // 2:j * HG_SUB + HG_SUB // 2 + 1]
                qs.append(q * jnp.exp(b - ref))
                sub = slice(j * HG_SUB, (j + 1) * HG_SUB)
                ks.append(k[sub] * jnp.exp(ref - b[sub]))
            a_full = lax.dot_general(jnp.concatenate(qs, axis=0).astype(BF16),
                                     jnp.concatenate(ks, axis=0).astype(BF16), NT_DIMS,
                                     preferred_element_type=F32)
            att = jnp.zeros((c, c), F32)
            for j in range(nsb):
                att = jnp.where(col >= j * HG_SUB, a_full[j * c:(j + 1) * c], att)
            att = jnp.where(causal, att, 0.0)
            out = inter + _dot(att.astype(BF16), v.astype(BF16))
            ob_ref[rows, h * dv:(h + 1) * dv] = _rms(out, hg_ref[...])
            kdec = (k * jnp.exp(bl - b)).astype(BF16)
            st_s[h] = jnp.exp(bl) * st + lax.dot_general(v.astype(BF16), kdec, TN_DIMS,
                                                         preferred_element_type=F32)
        return carry

    lax.fori_loop(0, tb // c, chunk, 0)

    @pl.when(t == pl.num_programs(1) - 1)
    def _():
        for h in range(heads):
            sf_ref[0, h] = st_s[h].T


def _hgrn(qh, kb, lf, ih, s0, hg_gain, *, batch, seq, heads, dk, dv, tb):
    n = qh.shape[0]
    c = min(CHUNK, seq)
    assert seq % tb == 0 and tb % c == 0 and c % HG_SUB == 0
    nt = seq // tb
    blk = lambda w: pl.BlockSpec((tb, w), lambda b, t: (b * nt + t, 0))
    st = pl.BlockSpec((1, heads, dk, dv), lambda b, t: (b, 0, 0, 0))
    return pl.pallas_call(
        functools.partial(_hgrn_kernel, heads=heads, dk=dk, dv=dv, c=c, nsb=c // HG_SUB),
        grid=(batch, nt),
        in_specs=[blk(heads * dk), blk(heads * dk), blk(heads * dk), blk(heads * dv), st,
                  _const_spec((1, dv))],
        out_specs=[blk(heads * dv), st],
        out_shape=[jax.ShapeDtypeStruct((n, heads * dv), F32),
                   jax.ShapeDtypeStruct((batch, heads, dk, dv), F32)],
        scratch_shapes=[pltpu.VMEM((heads, dv, dk), F32)],
        compiler_params=_params("arbitrary", "arbitrary"),
        name="hgrn",
    )(qh, kb, lf, ih, s0, hg_gain)


def _store_token_tiles(ref, x):
    rows, d = x.shape
    s = d // LANES
    for c in range(s):
        ref[pl.ds(c, rows, stride=s), :] = x[:, c * LANES:(c + 1) * LANES]


def _load_token_tiles(ref, rows, s):
    return jnp.concatenate([ref[pl.ds(c, rows, stride=s), :] for c in range(s)], axis=1)


def _token_tile(r, s):
    return pl.ds(pl.multiple_of(r * s, s), s)


def _ffn_prologue(x1, gf_ref, wrh_ref, wrl_ref, rbias_ref, x1_ref, xn_ref, lg_ref):
    x1_ref[...] = x1
    xn = _rms(x1, gf_ref[...])
    _store_token_tiles(xn_ref, xn)
    hi = xn.astype(BF16)
    lo = (xn - hi.astype(F32)).astype(BF16)
    nt = functools.partial(lax.dot_general, dimension_numbers=NT_DIMS, preferred_element_type=F32)
    lg_ref[...] = nt(wrh_ref[...], hi) + nt(wrh_ref[...], lo) + nt(wrl_ref[...], hi) + rbias_ref[...]


def _out_even_kernel(oa_ref, ob_ref, gs_ref, x_ref, w_ref, gf_ref, wrh_ref, wrl_ref, rbias_ref,
                     x1_ref, xn_ref, lg_ref):
    o = jnp.concatenate([oa_ref[...], ob_ref[...] * gs_ref[...]], axis=1).astype(BF16)
    x1 = x_ref[...] + _dot(o, w_ref[...])
    _ffn_prologue(x1, gf_ref, wrh_ref, wrl_ref, rbias_ref, x1_ref, xn_ref, lg_ref)


def _out_even(oa, ob, gs, x, w_bf, g_ffn, wrh, wrl, rbias, *, tm):
    n, d = x.shape
    w = oa.shape[1]
    nr = wrh.shape[0]
    row = lambda i: (i, 0)
    return pl.pallas_call(
        _out_even_kernel,
        grid=(n // tm,),
        in_specs=[pl.BlockSpec((tm, w), row), pl.BlockSpec((tm, w), row), pl.BlockSpec((tm, w), row),
                  pl.BlockSpec((tm, d), row), _const_spec(w_bf.shape), _const_spec((1, d)),
                  _const_spec(wrh.shape), _const_spec(wrl.shape), _const_spec(rbias.shape)],
        out_specs=[pl.BlockSpec((tm, d), row), pl.BlockSpec((tm * (d // LANES), LANES), row),
                   pl.BlockSpec((nr, tm), lambda i: (0, i))],
        out_shape=[jax.ShapeDtypeStruct((n, d), F32), jax.ShapeDtypeStruct((n * (d // LANES), LANES), F32),
                   jax.ShapeDtypeStruct((nr, n), F32)],
        compiler_params=_params("arbitrary"),
        name="out_even",
    )(oa, ob, gs, x, w_bf, g_ffn, wrh, wrl, rbias)


def _gelu(x):
    return 0.5 * x * (1.0 + jnp.tanh(math.sqrt(2.0 / math.pi) * (x + 0.044715 * (x * x * x))))


def _odd_kernel(x_ref, gm_ref, win_ref, vg_ref, wsp_ref, bsp_ref, wout_ref, gf_ref, wrh_ref, wrl_ref,
                rbias_ref, x1_ref, xn_ref, lg_ref, *rest, half, groups, l, emit_v):
    if emit_v:
        vn_ref, u_s, s_s = rest
    else:
        vn_ref = None
        u_s, vn_s, s_s = rest
    tm = x_ref.shape[0]
    x = x_ref[...]
    xn = _rms(x, gm_ref[...]).astype(BF16)
    cw = 512
    vbuf = vn_ref if emit_v else vn_s
    for cidx in range(half // cw):
        u_s[:, cidx * cw:(cidx + 1) * cw] = _gelu(_dot(xn, win_ref[:, cidx * cw:(cidx + 1) * cw]))
        vbuf[:, cidx * cw:(cidx + 1) * cw] = _gelu(
            _dot(xn, win_ref[:, half + cidx * cw:half + (cidx + 1) * cw]))
    vbuf[...] = _rms(vbuf[...], vg_ref[...])
    gw = half // groups
    row = lax.broadcasted_iota(I32, (l, l), 0)
    col = lax.broadcasted_iota(I32, (l, l), 1)
    for g in range(groups):
        wg = jnp.where(col <= row, wsp_ref[g], 0.0).astype(BF16)
        bg = bsp_ref[:, g:g + 1]
        for ci in range(tm // l):
            vv = vbuf[ci * l:(ci + 1) * l, g * gw:(g + 1) * gw].astype(BF16)
            s_s[ci * l:(ci + 1) * l, g * gw:(g + 1) * gw] = _dot(wg, vv) + bg
    y = _dot((u_s[...] * s_s[...]).astype(BF16), wout_ref[...])
    _ffn_prologue(x + y, gf_ref, wrh_ref, wrl_ref, rbias_ref, x1_ref, xn_ref, lg_ref)


def _odd_mixer(x, g_mix, win_bf, v_gain, wsp, bsp_t, wout_bf, g_ffn, wrh, wrl, rbias, *, l, tm, emit_v):
    n, d = x.shape
    half = wout_bf.shape[0]
    groups = wsp.shape[0]
    nr = wrh.shape[0]
    assert n % tm == 0 and tm % l == 0
    row = lambda i: (i, 0)
    out_specs = [pl.BlockSpec((tm, d), row), pl.BlockSpec((tm * (d // LANES), LANES), row),
                 pl.BlockSpec((nr, tm), lambda i: (0, i))]
    out_shape = [jax.ShapeDtypeStruct((n, d), F32), jax.ShapeDtypeStruct((n * (d // LANES), LANES), F32),
                 jax.ShapeDtypeStruct((nr, n), F32)]
    scratch = [pltpu.VMEM((tm, half), F32)]
    if emit_v:
        out_specs.append(pl.BlockSpec((tm, half), row))
        out_shape.append(jax.ShapeDtypeStruct((n, half), F32))
    else:
        scratch.append(pltpu.VMEM((tm, half), F32))
    scratch.append(pltpu.VMEM((tm, half), F32))
    return pl.pallas_call(
        functools.partial(_odd_kernel, half=half, groups=groups, l=l, emit_v=emit_v),
        grid=(n // tm,),
        in_specs=[pl.BlockSpec((tm, d), row), _const_spec((1, d)), _const_spec(win_bf.shape),
                  _const_spec((1, half)), _const_spec(wsp.shape), _const_spec(bsp_t.shape),
                  _const_spec(wout_bf.shape), _const_spec((1, d)), _const_spec(wrh.shape),
                  _const_spec(wrl.shape), _const_spec(rbias.shape)],
        out_specs=out_specs,
        out_shape=out_shape,
        scratch_shapes=scratch,
        compiler_params=_params("arbitrary"),
        name="odd_mixer",
    )(x, g_mix, win_bf, v_gain, wsp, bsp_t, wout_bf, g_ffn, wrh, wrl, rbias)


def _route_kernel(lg_ref, tri_ref, e_ref, g_ref, r_ref, cnt_ref, run_s, *, groups, epg):
    i = pl.program_id(0)

    @pl.when(i == 0)
    def _():
        run_s[...] = jnp.zeros(run_s.shape, F32)

    lg = lg_ref[...]
    tr = lg.shape[1]
    gl = [lg[g:g + 1] for g in range(groups)]
    m = functools.reduce(jnp.maximum, gl)
    grp = jnp.full((1, tr), groups - 1, I32)
    for g in range(groups - 2, -1, -1):
        grp = jnp.where(gl[g] == m, g, grp)
    gate_g = 1.0 / functools.reduce(lambda a, b: a + b, [jnp.exp(x - m) for x in gl])
    sel = lg[SUBLANES + (groups - 1) * epg:SUBLANES + groups * epg]
    for g in range(groups - 2, -1, -1):
        sel = jnp.where(grp == g, lg[SUBLANES + g * epg:SUBLANES + (g + 1) * epg], sel)
    sub = lax.broadcasted_iota(I32, sel.shape, 0)
    v1 = jnp.max(sel, axis=0, keepdims=True)
    i1 = jnp.min(jnp.where(sel == v1, sub, epg), axis=0, keepdims=True)
    sel2 = jnp.where(sub == i1, -jnp.inf, sel)
    v2 = jnp.max(sel2, axis=0, keepdims=True)
    i2 = jnp.min(jnp.where(sel2 == v2, sub, epg), axis=0, keepdims=True)
    tt = jnp.exp(v2 - v1)
    g1 = gate_g / (1.0 + tt)
    g2 = gate_g * tt / (1.0 + tt)
    e1 = grp * epg + i1
    e2 = grp * epg + i2
    ne = groups * epg
    eidx = lax.broadcasted_iota(I32, (ne, tr), 0)
    oh1 = eidx == e1
    oh2 = eidx == e2
    cnt = jnp.where(oh1, 1.0, 0.0) + jnp.where(oh2, 1.0, 0.0)
    before = run_s[:, 0:1] + _dot(cnt.astype(BF16), tri_ref[...])
    r1 = jnp.sum(jnp.where(oh1, before, 0.0), axis=0, keepdims=True)
    r2 = jnp.sum(jnp.where(oh2, before, 0.0), axis=0, keepdims=True)
    run_s[...] = run_s[...] + jnp.sum(cnt, axis=1, keepdims=True)
    rows = lax.broadcasted_iota(I32, (SUBLANES, tr), 0)
    e_ref[...] = jnp.where(rows == 0, e1, jnp.where(rows == 1, e2, 0))
    g_ref[...] = jnp.where(rows == 0, g1, jnp.where(rows == 1, g2, 0.0))
    r_ref[...] = jnp.where(rows == 0, r1, jnp.where(rows == 1, r2, 0.0)).astype(I32)
    cnt_ref[...] = run_s[...]


def _route(lgt, *, groups, epg, tr):
    nr, n = lgt.shape
    assert n % tr == 0 and nr == SUBLANES + groups * epg
    ne = groups * epg
    tri = (jnp.arange(tr)[:, None] < jnp.arange(tr)[None, :]).astype(BF16)
    tok = pl.BlockSpec((SUBLANES, tr), lambda i: (0, i))
    return pl.pallas_call(
        functools.partial(_route_kernel, groups=groups, epg=epg),
        grid=(n // tr,),
        in_specs=[pl.BlockSpec((nr, tr), lambda i: (0, i)), _const_spec((tr, tr))],
        out_specs=[tok, tok, tok, pl.BlockSpec((ne, LANES), lambda i: (0, 0))],
        out_shape=[jax.ShapeDtypeStruct((SUBLANES, n), I32), jax.ShapeDtypeStruct((SUBLANES, n), F32),
                   jax.ShapeDtypeStruct((SUBLANES, n), I32), jax.ShapeDtypeStruct((ne, LANES), F32)],
        scratch_shapes=[pltpu.VMEM((ne, LANES), F32)],
        compiler_params=_params("arbitrary"),
        name="route",
    )(lgt, tri)


ISSUE_UNROLL = 8


def _dispatch_kernel(seg_ref, dst_ref, x_ref, buf_ref, zero_s, sem, zsem, *, s):
    tp = x_ref.shape[0] // s
    blk = MOE_BLOCK * s

    @pl.when(pl.program_id(0) == 0)
    def _():
        zero_s[...] = jnp.zeros(zero_s.shape, F32)

        def block_copy(b):
            return pltpu.make_async_copy(zero_s, buf_ref.at[pl.ds(pl.multiple_of(b * blk, blk), blk)], zsem)

        for e in range(seg_ref.shape[1]):
            @pl.when(seg_ref[1, e] > 0)
            def _():
                block_copy(seg_ref[0, e] // MOE_BLOCK - 1).start()
        for e in range(seg_ref.shape[1]):
            @pl.when(seg_ref[1, e] > 0)
            def _():
                block_copy(0).wait()

        def tail_start(b, c):
            block_copy(b).start()
            return c

        def tail_wait(b, c):
            block_copy(b).wait()
            return c

        first_unused = seg_ref[0, seg_ref.shape[1] - 1] // MOE_BLOCK
        n_blocks = buf_ref.shape[0] // blk
        lax.fori_loop(first_unused, n_blocks, tail_start, 0)
        lax.fori_loop(first_unused, n_blocks, tail_wait, 0)

    def issue(r0, c):
        for u in range(ISSUE_UNROLL):
            r = r0 * ISSUE_UNROLL + u
            for kk in range(TOP_K):
                pltpu.make_async_copy(x_ref.at[_token_tile(r, s)],
                                      buf_ref.at[_token_tile(dst_ref[0, kk, r], s)], sem).start()
        return c

    lax.fori_loop(0, tp // ISSUE_UNROLL, issue, 0)
    for kk in range(TOP_K):
        pltpu.make_async_copy(x_ref, buf_ref.at[pl.ds(0, tp * s)], sem).wait()


def _dispatch(seg, dest3, xn, n_slots, *, tp, s):
    n = xn.shape[0] // s
    assert tp % ISSUE_UNROLL == 0
    grid_spec = pltpu.PrefetchScalarGridSpec(
        num_scalar_prefetch=1,
        grid=(n // tp,),
        in_specs=[pl.BlockSpec((1, TOP_K, tp), lambda i, sg: (i, 0, 0), memory_space=pltpu.SMEM),
                  pl.BlockSpec((tp * s, LANES), lambda i, sg: (i, 0))],
        out_specs=pl.BlockSpec(memory_space=pl.ANY),
        scratch_shapes=[pltpu.VMEM((MOE_BLOCK * s, LANES), F32), pltpu.SemaphoreType.DMA(()),
                        pltpu.SemaphoreType.DMA(())],
    )
    return pl.pallas_call(
        functools.partial(_dispatch_kernel, s=s),
        grid_spec=grid_spec,
        out_shape=jax.ShapeDtypeStruct((n_slots * s, LANES), F32),
        compiler_params=_params("arbitrary"),
        name="dispatch",
    )(seg, dest3, xn)


def _expert_kernel(be_ref, nu_ref, x_ref, wg_ref, wu_ref, wd_ref, o_ref, wg_s, wu_s, wd_s):
    b = pl.program_id(0)
    s = wg_s.shape[0] // LANES
    prev = be_ref[jnp.maximum(b - 1, 0)]

    @pl.when((b == 0) | (be_ref[b] != prev))
    def _():
        wg_s[...] = wg_ref[0, 0].astype(BF16)
        wu_s[...] = wu_ref[0, 0].astype(BF16)
        wd_s[...] = wd_ref[0, 0].astype(BF16)

    @pl.when(b < nu_ref[0])
    def _():
        xb = _load_token_tiles(x_ref, MOE_BLOCK, s).astype(BF16)
        gate = _dot(xb, wg_s[...])
        h = gate * _sigmoid(gate) * _dot(xb, wu_s[...])
        _store_token_tiles(o_ref, _dot(h.astype(BF16), wd_s[...]))

    @pl.when(b >= nu_ref[0])
    def _():
        o_ref[...] = jnp.zeros(o_ref.shape, F32)


def _experts(blk_expert, n_used, buf, w_gate, w_up, w_down, *, layer):
    d, de = w_gate.shape[2:]
    s = d // LANES
    n_blocks = buf.shape[0] // (MOE_BLOCK * s)
    rows = pl.BlockSpec((MOE_BLOCK * s, LANES), lambda b, be, nu: (b, 0))
    used_rows = pl.BlockSpec((MOE_BLOCK * s, LANES), lambda b, be, nu: (jnp.minimum(b, nu[0] - 1), 0))
    grid_spec = pltpu.PrefetchScalarGridSpec(
        num_scalar_prefetch=2,
        grid=(n_blocks,),
        in_specs=[used_rows,
                  pl.BlockSpec((1, 1, d, de), lambda b, be, nu: (layer, be[b], 0, 0)),
                  pl.BlockSpec((1, 1, d, de), lambda b, be, nu: (layer, be[b], 0, 0)),
                  pl.BlockSpec((1, 1, de, d), lambda b, be, nu: (layer, be[b], 0, 0))],
        out_specs=rows,
        scratch_shapes=[pltpu.VMEM((d, de), BF16), pltpu.VMEM((d, de), BF16), pltpu.VMEM((de, d), BF16)],
    )
    return pl.pallas_call(
        _expert_kernel,
        grid_spec=grid_spec,
        out_shape=jax.ShapeDtypeStruct(buf.shape, F32),
        compiler_params=_params("arbitrary"),
        name="experts",
    )(blk_expert, n_used, buf, w_gate, w_up, w_down)


def _combine_kernel(dst_ref, gate_ref, x_ref, yb_ref, o_ref, rows_s, sem):
    tq, d = x_ref.shape
    s = d // LANES

    def issue(r0, c):
        for u in range(ISSUE_UNROLL):
            r = r0 * ISSUE_UNROLL + u
            for kk in range(TOP_K):
                pltpu.make_async_copy(yb_ref.at[_token_tile(dst_ref[0, kk, r], s)],
                                      rows_s.at[kk, _token_tile(r, s)], sem).start()
        return c

    lax.fori_loop(0, tq // ISSUE_UNROLL, issue, 0)
    for kk in range(TOP_K):
        pltpu.make_async_copy(yb_ref.at[pl.ds(0, tq * s)], rows_s.at[kk], sem).wait()
    g = gate_ref[...]
    o_ref[...] = (x_ref[...] + g[:, 0:1] * _load_token_tiles(rows_s.at[0], tq, s)
                  + g[:, 1:2] * _load_token_tiles(rows_s.at[1], tq, s))


def _combine(dest3, gates_t, x1, yb, *, tq):
    n, d = x1.shape
    s = d // LANES
    assert tq % ISSUE_UNROLL == 0
    return pl.pallas_call(
        _combine_kernel,
        grid=(n // tq,),
        in_specs=[pl.BlockSpec((1, TOP_K, tq), lambda i: (i, 0, 0), memory_space=pltpu.SMEM),
                  pl.BlockSpec((tq, TOP_K), lambda i: (i, 0)),
                  pl.BlockSpec((tq, d), lambda i: (i, 0)),
                  pl.BlockSpec(memory_space=pl.ANY)],
        out_specs=pl.BlockSpec((tq, d), lambda i: (i, 0)),
        out_shape=jax.ShapeDtypeStruct((n, d), F32),
        scratch_shapes=[pltpu.VMEM((TOP_K, tq * s, LANES), F32), pltpu.SemaphoreType.DMA(())],
        compiler_params=_params("arbitrary"),
        name="combine",
    )(dest3, gates_t, x1, yb)


def _moe(x1, xn, lgt, w_gate, w_up, w_down, *, layer, groups, epg):
    n, d = x1.shape
    ne = groups * epg
    tile = min(256, n)
    e8, g8, r8, cnt = _route(lgt, groups=groups, epg=epg, tr=min(512, n))
    counts = cnt[:, 0].astype(I32)
    padded = (counts + MOE_BLOCK - 1) // MOE_BLOCK * MOE_BLOCK
    pend = jnp.cumsum(padded)
    pstart = pend - padded
    n_blocks = -(-(n * TOP_K) // MOE_BLOCK) + ne
    eids = jnp.arange(ne, dtype=I32)
    seg = jnp.sum(jnp.where(e8[:TOP_K, :, None] == eids, pstart, 0), axis=-1)
    dest = seg + r8[:TOP_K]
    dest3 = dest.reshape(TOP_K, n // tile, tile).transpose(1, 0, 2)
    blk_row = jnp.arange(n_blocks, dtype=I32)[:, None] * MOE_BLOCK
    blk_expert = jnp.minimum(jnp.sum((pend[None, :] <= blk_row).astype(I32), axis=1), ne - 1)
    n_used = (pend[-1:] // MOE_BLOCK).astype(I32)
    buf = _dispatch(jnp.stack([pend, padded]).astype(I32), dest3, xn, n_blocks * MOE_BLOCK, tp=tile,
                    s=d // LANES)
    yb = _experts(blk_expert, n_used, buf, w_gate, w_up, w_down, layer=layer)
    return _combine(dest3, g8[:TOP_K].T, x1, yb, tq=tile)


def _router_weights(wg, bg, we, be):
    d, groups = wg.shape
    epg = we.shape[2]
    assert groups <= SUBLANES and epg == SUBLANES
    pad = jnp.zeros((SUBLANES - groups, d), F32)
    wr = jnp.concatenate([wg.T, pad, we.transpose(0, 2, 1).reshape(groups * epg, d)], axis=0)
    rb = jnp.concatenate([bg, jnp.zeros((SUBLANES - groups,), F32), be.reshape(-1)])[:, None]
    hi = wr.astype(BF16)
    lo = (wr - hi.astype(F32)).astype(BF16)
    return hi, lo, rb, groups, epg


def kernel(x_prompt, x_sample, cache_attn_k, cache_attn_v, state_hgrn, rel_bias, norm_mix, norm_ffn,
           w_in_even, w_out_even, q_norm_gain, k_norm_gain, lam_q1, lam_k1, lam_q2, lam_k2, da_out_gain,
           hgrn_lb_logits, hgrn_out_gain, w_in_odd, sgu_v_gain, sgu_w, sgu_b, w_out_odd,
           router_group_w, router_group_b, router_expert_w, router_expert_b,
           expert_w_gate, expert_w_up, expert_w_down):
    bp, tp, d = x_prompt.shape
    bs, ts, _ = x_sample.shape
    depth = norm_mix.shape[0]
    _, _, past, da_heads, _, da_dh = cache_attn_k.shape
    da_dv = cache_attn_v.shape[-1]
    _, _, hg_heads, hg_dk, hg_dv = state_hgrn.shape
    width = da_heads * da_dv
    assert width == da_heads * 2 * da_dh == hg_heads * hg_dk == hg_heads * hg_dv
    assert da_dv == LANES and hg_dk == LANES and hg_dv == LANES

    lb_all = jnp.cumsum(jax.nn.softmax(hgrn_lb_logits.astype(F32), axis=0), axis=0)
    gid = jnp.arange(width) // da_dh
    pm = jnp.where(gid[:, None] == gid[None, :], 1.0 / da_dh, 0.0).astype(BF16)

    xs = {"p": x_prompt.reshape(bp * tp, d), "s": x_sample.reshape(bs * ts, d)}
    dims = {"p": (bp, tp), "s": (bs, ts)}
    outs = {"p": {}, "s": {}}
    kp_l, vp_l, ks_l, vs_l, sp_l, ss_l, sgu_l = [], [], [], [], [], [], []

    for layer in range(depth):
        j = layer // 2
        wrh, wrl, rbias, groups, epg = _router_weights(
            router_group_w[layer], router_group_b[layer], router_expert_w[layer], router_expert_b[layer])
        g_mix = norm_mix[layer][None, :]
        g_ffn = norm_ffn[layer][None, :]
        if layer % 2 == 0:
            lam_init = 0.8 - 0.6 * math.exp(-0.3 * layer)
            w_in_bf = w_in_even[j].astype(BF16)
            w_out_bf = w_out_even[j].astype(BF16)
            reps = width // da_dh
            qg = jnp.tile(q_norm_gain[j], reps)[None, :]
            kg = jnp.tile(k_norm_gain[j], reps)[None, :]
            lam4 = jnp.stack([lam_q1[j], lam_k1[j], lam_q2[j], lam_k2[j]])
            sub_gain = da_out_gain[j][None, :]
            hg_gain = hgrn_out_gain[j][None, :]
            lb = lb_all[j][None, :]
            for key in ("p", "s"):
                b, t = dims[key]
                x = xs[key]
                q, k, v, qh, kb, lf, ih, gs = _in_even(
                    x, g_mix, w_in_bf, pm, qg, kg, lb, width=width, q_scale=da_dh ** -0.5, tm=min(512, b * t))
                if key == "p":
                    oa = _attn_prompt(q, k, v, rel_bias, lam4, sub_gain, batch=b, seq=t, heads=da_heads,
                                      dh=da_dh, lam_init=lam_init, qt=min(256, t))
                    s0 = jnp.zeros((b, hg_heads, hg_dk, hg_dv), F32)
                    ob, s_new = _hgrn(qh, kb, lf, ih, s0, hg_gain, batch=b, seq=t, heads=hg_heads,
                                      dk=hg_dk, dv=hg_dv, tb=min(512, t))
                    kp_l.append(k.reshape(b, t, da_heads, 2, da_dh))
                    vp_l.append(v.reshape(b, t, da_heads, da_dv))
                    sp_l.append(s_new)
                else:
                    ck = cache_attn_k[j].reshape(b, past, width)
                    cv = cache_attn_v[j].reshape(b, past, width)
                    oa = _attn_sample(q, k, v, ck, cv, rel_bias, lam4, sub_gain, batch=b, t=t,
                                      heads=da_heads, dh=da_dh, lam_init=lam_init)
                    ob, s_new = _hgrn(qh, kb, lf, ih, state_hgrn[j], hg_gain, batch=b, seq=t,
                                      heads=hg_heads, dk=hg_dk, dv=hg_dv, tb=t)
                    ks_l.append(k.reshape(b, t, da_heads, 2, da_dh))
                    vs_l.append(v.reshape(b, t, da_heads, da_dv))
                    ss_l.append(s_new)
                outs[key] = _out_even(oa, ob, gs, x, w_out_bf, g_ffn, wrh, wrl, rbias, tm=min(512, b * t))
        else:
            w_in_bf = w_in_odd[j].astype(BF16)
            w_out_bf = w_out_odd[j].astype(BF16)
            v_gain = sgu_v_gain[j][None, :]
            for key in ("p", "s"):
                b, t = dims[key]
                l = min(SGU_CHUNK, t)
                res = _odd_mixer(xs[key], g_mix, w_in_bf, v_gain, sgu_w[j][:, :l, :l], sgu_b[j][:, :l].T,
                                 w_out_bf, g_ffn, wrh, wrl, rbias, l=l, tm=min(256, b * t), emit_v=(key == "s"))
                outs[key] = res[:3]
                if key == "s":
                    sgu_l.append(res[3].reshape(b, t, -1))
        for key in ("p", "s"):
            x1, xn, lgt = outs[key]
            xs[key] = _moe(x1, xn, lgt, expert_w_gate, expert_w_up, expert_w_down, layer=layer,
                           groups=groups, epg=epg)

    return (xs["p"].reshape(bp, tp, d), xs["s"].reshape(bs, ts, d), jnp.stack(kp_l), jnp.stack(vp_l),
            jnp.stack(ks_l), jnp.stack(vs_l), jnp.stack(sp_l), jnp.stack(ss_l), jnp.stack(sgu_l))
```

```python
import functools
import math

import jax
import jax.numpy as jnp
from jax import lax
from jax.experimental import pallas as pl
from jax.experimental.pallas import tpu as pltpu

F32 = jnp.float32
BF16 = jnp.bfloat16
I32 = jnp.int32

EPS = 1e-6
CHUNK = 64
SGU_CHUNK = 128
REL_BUCKETS = 32
REL_MAX_DIST = 128
TOP_K = 2
MOE_BLOCK = 512
MOE_BLOCK_SMALL = 128
HG_SUB = 16

LANES = 128
SUBLANES = 8
VMEM_LIMIT = 56 * 1024 * 1024

NT_DIMS = (((1,), (1,)), ((), ()))
TN_DIMS = (((0,), (0,)), ((), ()))


def _params(*sem):
    return pltpu.CompilerParams(dimension_semantics=sem, vmem_limit_bytes=VMEM_LIMIT)


def _const_spec(shape):
    nd = len(shape)
    return pl.BlockSpec(shape, lambda *_: (0,) * nd, pipeline_mode=pl.Buffered(1))


def _sigmoid(x):
    return 1.0 / (1.0 + jnp.exp(-x))


def _rms(x, g):
    return x * lax.rsqrt(jnp.mean(x * x, axis=-1, keepdims=True) + EPS) * g


def _dot(a, b):
    return jnp.dot(a, b, preferred_element_type=F32)


def rel_bucket(rel):
    half = REL_BUCKETS // 2
    max_exact = half // 2
    ret = (rel > 0).astype(I32) * half
    n = jnp.abs(rel)
    nf = jnp.maximum(n, 1).astype(F32)
    large = max_exact + (jnp.log(nf / max_exact) / math.log(REL_MAX_DIST / max_exact)
                         * (half - max_exact)).astype(I32)
    large = jnp.minimum(large, half - 1)
    return ret + jnp.where(n < max_exact, n, large)


def _in_even_kernel(x_ref, g_ref, w_ref, pm_ref, qg_ref, kg_ref, lb_ref,
                    q_ref, k_ref, v_ref, qh_ref, kb_ref, lf_ref, ih_ref, gs_ref, *, width, q_scale):
    xn = _rms(x_ref[...], g_ref[...]).astype(BF16)

    def proj(c):
        return _dot(xn, w_ref[:, c * width:(c + 1) * width])

    def group_norm(y, gain):
        ms = _dot((y * y).astype(BF16), pm_ref[...])
        return y * lax.rsqrt(ms + EPS) * gain

    q_ref[...] = group_norm(proj(0), qg_ref[...]) * q_scale
    k_ref[...] = group_norm(proj(1), kg_ref[...])
    v_ref[...] = proj(2)
    yq = proj(3)
    qh_ref[...] = yq * _sigmoid(yq)
    zf = proj(4)
    lb = lb_ref[...]
    lf_ref[...] = jnp.log(lb + (1.0 - lb) * _sigmoid(zf))
    kb_ref[...] = (1.0 - lb) * _sigmoid(-zf)
    ih_ref[...] = proj(5)
    yg = proj(6)
    gs_ref[...] = yg * _sigmoid(yg)


def _in_even(x, g_mix, w_bf, pm, qg, kg, lb, *, width, q_scale, tm):
    n, d = x.shape
    assert n % tm == 0
    row = lambda i: (i, 0)
    out = jax.ShapeDtypeStruct((n, width), F32)
    return pl.pallas_call(
        functools.partial(_in_even_kernel, width=width, q_scale=q_scale),
        grid=(n // tm,),
        in_specs=[pl.BlockSpec((tm, d), row), _const_spec((1, d)), _const_spec(w_bf.shape),
                  _const_spec(pm.shape), _const_spec((1, width)), _const_spec((1, width)),
                  _const_spec((1, width))],
        out_specs=[pl.BlockSpec((tm, width), row)] * 8,
        out_shape=[out] * 8,
        compiler_params=_params("arbitrary"),
        name="in_even",
    )(x, g_mix, w_bf, pm, qg, kg, lb)


def _bias_from_buckets(bk, rb_ref, h):
    b = jnp.zeros(bk.shape, F32)
    for u in range(REL_BUCKETS):
        b = jnp.where(bk == u, rb_ref[u, h], b)
    return jnp.where(bk < 0, -jnp.inf, b)


def _lam(lam_ref, lam_init):
    r = lam_ref[...]
    s1 = jnp.sum(r[0:1] * r[1:2], axis=1, keepdims=True)
    s2 = jnp.sum(r[2:3] * r[3:4], axis=1, keepdims=True)
    return jnp.exp(s1) - jnp.exp(s2) + lam_init


def _split_components(q, dh):
    lane = lax.broadcasted_iota(I32, q.shape, 1)
    q0 = jnp.where(lane < dh, q, 0.0)
    q1 = jnp.where(lane >= dh, q, 0.0)
    return jnp.concatenate([q0, q1], axis=0).astype(BF16)


def _attn_prompt_kernel(rb_ref, far_ref, lam_ref, bk_ref, sg_ref, q_ref, k_ref, v_ref, o_ref, *,
                        qt, dh, lam_init):
    h = pl.program_id(1)
    t = q_ref.shape[0]
    kb = k_ref[...].astype(BF16)
    vt = v_ref[...].T.astype(BF16)
    qtr = q_ref[...].T
    sub = lax.broadcasted_iota(I32, (2 * dh, qt), 0)
    bias = []
    for d in range(2):
        b = _bias_from_buckets(bk_ref[d], rb_ref, h)
        bias.append(jnp.concatenate([b, b], axis=1))
    far = rb_ref[far_ref[0], h]
    lam = _lam(lam_ref, lam_init)
    gain = sg_ref[...] * (1.0 - lam_init)
    for i in range(t // qt):
        qi = qtr[:, i * qt:(i + 1) * qt]
        qz = jnp.concatenate([jnp.where(sub < dh, qi, 0.0), jnp.where(sub >= dh, qi, 0.0)],
                             axis=1).astype(BF16)
        n = (i + 1) * qt
        parts = [(n - qt, n, bias[0])]
        if i >= 1:
            parts.append((n - 2 * qt, n - qt, bias[1]))
        if i >= 2:
            parts.append((0, n - 2 * qt, far))
        s = [_dot(kb[lo:hi], qz) + b for lo, hi, b in parts]
        m = functools.reduce(jnp.maximum, [jnp.max(x, axis=0, keepdims=True) for x in s])
        p = [jnp.exp(x - m) for x in s]
        l = functools.reduce(lambda a, b: a + b, [jnp.sum(x, axis=0, keepdims=True) for x in p])
        acc = functools.reduce(lambda a, b: a + b,
                               [_dot(vt[:, lo:hi], x.astype(BF16)) for (lo, hi, _), x in zip(parts, p)])
        o = acc / l
        out = o[:, :qt] - lam * o[:, qt:]
        out = out * lax.rsqrt(jnp.mean(out * out, axis=0, keepdims=True) + EPS) * gain
        o_ref[i * qt:(i + 1) * qt, :] = out.T


def _attn_prompt(q, k, v, rel_bias, lam4, sub_gain, *, batch, seq, heads, dh, lam_init, qt):
    n, w = q.shape
    dv = w // heads
    assert dv == 2 * dh and seq % qt == 0 and qt % CHUNK == 0
    kj = jnp.arange(qt, dtype=I32)[:, None]
    qi = jnp.arange(qt, dtype=I32)[None, :]
    bk0 = jnp.where((kj // CHUNK) <= (qi // CHUNK), rel_bucket(kj - qi), -1)
    bk1 = rel_bucket(kj - qi - qt)
    bk = jnp.stack([bk0, bk1]).astype(I32)
    assert qt + 1 >= REL_MAX_DIST
    far = rel_bucket(jnp.full((1,), -(qt + 1), I32))
    smem = pl.BlockSpec(memory_space=pltpu.SMEM)
    seq_blk = pl.BlockSpec((seq, dv), lambda b, h: (b, h))
    return pl.pallas_call(
        functools.partial(_attn_prompt_kernel, qt=qt, dh=dh, lam_init=lam_init),
        grid=(batch, heads),
        in_specs=[smem, smem, _const_spec(lam4.shape), _const_spec(bk.shape), _const_spec((dv, 1)),
                  seq_blk, seq_blk, seq_blk],
        out_specs=seq_blk,
        out_shape=jax.ShapeDtypeStruct((n, w), F32),
        compiler_params=_params("arbitrary", "arbitrary"),
        name="attn_prompt",
    )(rel_bias, far, lam4, bk, sub_gain.T, q, k, v)


def _attn_sample_kernel(rb_ref, lam_ref, bkc_ref, bkn_ref, sg_ref, q_ref, kc_ref, vc_ref, kn_ref, vn_ref,
                        o_ref, *, t, dh, lam_init):
    h = pl.program_id(1)
    qz = _split_components(q_ref[...], dh)
    bc = _bias_from_buckets(bkc_ref[...], rb_ref, h)
    bn = _bias_from_buckets(bkn_ref[...], rb_ref, h)
    sc = lax.dot_general(qz, kc_ref[0].astype(BF16), NT_DIMS, preferred_element_type=F32)
    sn = lax.dot_general(qz, kn_ref[...].astype(BF16), NT_DIMS, preferred_element_type=F32)
    sc = sc + jnp.concatenate([bc, bc], axis=0)
    sn = sn + jnp.concatenate([bn, bn], axis=0)
    m = jnp.maximum(jnp.max(sc, axis=1, keepdims=True), jnp.max(sn, axis=1, keepdims=True))
    pc = jnp.exp(sc - m)
    pn = jnp.exp(sn - m)
    l = jnp.sum(pc, axis=1, keepdims=True) + jnp.sum(pn, axis=1, keepdims=True)
    acc = _dot(pc.astype(BF16), vc_ref[0].astype(BF16)) + _dot(pn.astype(BF16), vn_ref[...].astype(BF16))
    o = acc / l
    out = o[:t] - _lam(lam_ref, lam_init) * o[t:]
    o_ref[...] = _rms(out, sg_ref[...]) * (1.0 - lam_init)


def _attn_sample(q, k_new, v_new, cache_k, cache_v, rel_bias, lam4, sub_gain, *, batch, t, heads, dh,
                 lam_init):
    n, w = q.shape
    dv = w // heads
    past = cache_k.shape[1]
    assert past % CHUNK == 0 and t <= CHUNK
    qpos = past + jnp.arange(t, dtype=I32)[:, None]
    bkc = rel_bucket(jnp.arange(past, dtype=I32)[None, :] - qpos).astype(I32)
    bkn = rel_bucket(past + jnp.arange(t, dtype=I32)[None, :] - qpos).astype(I32)
    smem = pl.BlockSpec(memory_space=pltpu.SMEM)
    new = pl.BlockSpec((t, dv), lambda b, h: (b, h))
    old = pl.BlockSpec((1, past, dv), lambda b, h: (b, 0, h))
    return pl.pallas_call(
        functools.partial(_attn_sample_kernel, t=t, dh=dh, lam_init=lam_init),
        grid=(batch, heads),
        in_specs=[smem, _const_spec(lam4.shape), _const_spec(bkc.shape), _const_spec(bkn.shape),
                  _const_spec((1, dv)), new, old, old, new, new],
        out_specs=new,
        out_shape=jax.ShapeDtypeStruct((n, w), F32),
        compiler_params=_params("arbitrary", "arbitrary"),
        name="attn_sample",
    )(rel_bias, lam4, bkc, bkn, sub_gain, q, cache_k, cache_v, k_new, v_new)


def _cumsum_rows(x):
    c = x.shape[0]
    row = lax.broadcasted_iota(I32, x.shape, 0)
    s = 1
    while s < c:
        x = x + jnp.where(row >= s, pltpu.roll(x, s, axis=0), 0.0)
        s *= 2
    return x


def _hgrn_kernel(qh_ref, kb_ref, lf_ref, ih_ref, s0_ref, hg_ref, ob_ref, sf_ref, st_s, *,
                 heads, dk, dv, c, nsb):
    t = pl.program_id(1)

    @pl.when(t == 0)
    def _():
        for h in range(heads):
            st_s[h] = s0_ref[0, h].T

    tb = qh_ref.shape[0]
    row = lax.broadcasted_iota(I32, (c, c), 0)
    col = lax.broadcasted_iota(I32, (c, c), 1)
    causal = col <= row

    def chunk(ci, carry):
        r0 = pl.multiple_of(ci * c, c)
        for h in range(heads):
            rows = pl.ds(r0, c)
            q = qh_ref[rows, h * dk:(h + 1) * dk]
            k = kb_ref[rows, h * dk:(h + 1) * dk]
            v = ih_ref[rows, h * dv:(h + 1) * dv]
            b = _cumsum_rows(lf_ref[rows, h * dk:(h + 1) * dk])
            bl = b[c - 1:c]
            st = st_s[h]
            inter = lax.dot_general((q * jnp.exp(b)).astype(BF16), st.astype(BF16), NT_DIMS,
                                    preferred_element_type=F32)
            qs, ks = [], []
            for j in range(nsb):
                ref = b[j * HG_SUB + HG_SUB // 2:j * HG_SUB + HG_SUB // 2 + 1]
                qs.append(q * jnp.exp(b - ref))
                sub = slice(j * HG_SUB, (j + 1) * HG_SUB)
                ks.append(k[sub] * jnp.exp(ref - b[sub]))
            a_full = lax.dot_general(jnp.concatenate(qs, axis=0).astype(BF16),
                                     jnp.concatenate(ks, axis=0).astype(BF16), NT_DIMS,
                                     preferred_element_type=F32)
            att = jnp.zeros((c, c), F32)
            for j in range(nsb):
                att = jnp.where(col >= j * HG_SUB, a_full[j * c:(j + 1) * c], att)
            att = jnp.where(causal, att, 0.0)
            out = inter + _dot(att.astype(BF16), v.astype(BF16))
            ob_ref[rows, h * dv:(h + 1) * dv] = _rms(out, hg_ref[...])
            kdec = (k * jnp.exp(bl - b)).astype(BF16)
            st_s[h] = jnp.exp(bl) * st + lax.dot_general(v.astype(BF16), kdec, TN_DIMS,
                                                         preferred_element_type=F32)
        return carry

    lax.fori_loop(0, tb // c, chunk, 0)

    @pl.when(t == pl.num_programs(1) - 1)
    def _():
        for h in range(heads):
            sf_ref[0, h] = st_s[h].T


def _hgrn(qh, kb, lf, ih, s0, hg_gain, *, batch, seq, heads, dk, dv, tb):
    n = qh.shape[0]
    c = min(CHUNK, seq)
    assert seq % tb == 0 and tb % c == 0 and c % HG_SUB == 0
    nt = seq // tb
    blk = lambda w: pl.BlockSpec((tb, w), lambda b, t: (b * nt + t, 0))
    st = pl.BlockSpec((1, heads, dk, dv), lambda b, t: (b, 0, 0, 0))
    return pl.pallas_call(
        functools.partial(_hgrn_kernel, heads=heads, dk=dk, dv=dv, c=c, nsb=c // HG_SUB),
        grid=(batch, nt),
        in_specs=[blk(heads * dk), blk(heads * dk), blk(heads * dk), blk(heads * dv), st,
                  _const_spec((1, dv))],
        out_specs=[blk(heads * dv), st],
        out_shape=[jax.ShapeDtypeStruct((n, heads * dv), F32),
                   jax.ShapeDtypeStruct((batch, heads, dk, dv), F32)],
        scratch_shapes=[pltpu.VMEM((heads, dv, dk), F32)],
        compiler_params=_params("arbitrary", "arbitrary"),
        name="hgrn",
    )(qh, kb, lf, ih, s0, hg_gain)


def _store_token_tiles(ref, x):
    rows, d = x.shape
    s = d // LANES
    for c in range(s):
        ref[pl.ds(c, rows, stride=s), :] = x[:, c * LANES:(c + 1) * LANES]


def _load_token_tiles(ref, rows, s):
    return jnp.concatenate([ref[pl.ds(c, rows, stride=s), :] for c in range(s)], axis=1)


def _token_tile(r, s):
    return pl.ds(pl.multiple_of(r * s, s), s)


def _ffn_prologue(x1, gf_ref, wrh_ref, wrl_ref, rbias_ref, x1_ref, xn_ref, lg_ref):
    x1_ref[...] = x1
    xn = _rms(x1, gf_ref[...])
    _store_token_tiles(xn_ref, xn)
    hi = xn.astype(BF16)
    lo = (xn - hi.astype(F32)).astype(BF16)
    nt = functools.partial(lax.dot_general, dimension_numbers=NT_DIMS, preferred_element_type=F32)
    lg_ref[...] = nt(wrh_ref[...], hi) + nt(wrh_ref[...], lo) + nt(wrl_ref[...], hi) + rbias_ref[...]


def _out_even_kernel(oa_ref, ob_ref, gs_ref, x_ref, w_ref, gf_ref, wrh_ref, wrl_ref, rbias_ref,
                     x1_ref, xn_ref, lg_ref):
    o = jnp.concatenate([oa_ref[...], ob_ref[...] * gs_ref[...]], axis=1).astype(BF16)
    x1 = x_ref[...] + _dot(o, w_ref[...])
    _ffn_prologue(x1, gf_ref, wrh_ref, wrl_ref, rbias_ref, x1_ref, xn_ref, lg_ref)


def _out_even(oa, ob, gs, x, w_bf, g_ffn, wrh, wrl, rbias, *, tm):
    n, d = x.shape
    w = oa.shape[1]
    nr = wrh.shape[0]
    row = lambda i: (i, 0)
    return pl.pallas_call(
        _out_even_kernel,
        grid=(n // tm,),
        in_specs=[pl.BlockSpec((tm, w), row), pl.BlockSpec((tm, w), row), pl.BlockSpec((tm, w), row),
                  pl.BlockSpec((tm, d), row), _const_spec(w_bf.shape), _const_spec((1, d)),
                  _const_spec(wrh.shape), _const_spec(wrl.shape), _const_spec(rbias.shape)],
        out_specs=[pl.BlockSpec((tm, d), row), pl.BlockSpec((tm * (d // LANES), LANES), row),
                   pl.BlockSpec((nr, tm), lambda i: (0, i))],
        out_shape=[jax.ShapeDtypeStruct((n, d), F32), jax.ShapeDtypeStruct((n * (d // LANES), LANES), F32),
                   jax.ShapeDtypeStruct((nr, n), F32)],
        compiler_params=_params("arbitrary"),
        name="out_even",
    )(oa, ob, gs, x, w_bf, g_ffn, wrh, wrl, rbias)


def _gelu(x):
    return 0.5 * x * (1.0 + jnp.tanh(math.sqrt(2.0 / math.pi) * (x + 0.044715 * (x * x * x))))


def _odd_kernel(x_ref, gm_ref, win_ref, vg_ref, wsp_ref, bsp_ref, wout_ref, gf_ref, wrh_ref, wrl_ref,
                rbias_ref, x1_ref, xn_ref, lg_ref, *rest, half, groups, l, emit_v):
    if emit_v:
        vn_ref, u_s, s_s = rest
    else:
        vn_ref = None
        u_s, vn_s, s_s = rest
    tm = x_ref.shape[0]
    x = x_ref[...]
    xn = _rms(x, gm_ref[...]).astype(BF16)
    cw = 512
    vbuf = vn_ref if emit_v else vn_s
    for cidx in range(half // cw):
        u_s[:, cidx * cw:(cidx + 1) * cw] = _gelu(_dot(xn, win_ref[:, cidx * cw:(cidx + 1) * cw]))
        vbuf[:, cidx * cw:(cidx + 1) * cw] = _gelu(
            _dot(xn, win_ref[:, half + cidx * cw:half + (cidx + 1) * cw]))
    vbuf[...] = _rms(vbuf[...], vg_ref[...])
    gw = half // groups
    row = lax.broadcasted_iota(I32, (l, l), 0)
    col = lax.broadcasted_iota(I32, (l, l), 1)
    for g in range(groups):
        wg = jnp.where(col <= row, wsp_ref[g], 0.0).astype(BF16)
        bg = bsp_ref[:, g:g + 1]
        for ci in range(tm // l):
            vv = vbuf[ci * l:(ci + 1) * l, g * gw:(g + 1) * gw].astype(BF16)
            s_s[ci * l:(ci + 1) * l, g * gw:(g + 1) * gw] = _dot(wg, vv) + bg
    y = _dot((u_s[...] * s_s[...]).astype(BF16), wout_ref[...])
    _ffn_prologue(x + y, gf_ref, wrh_ref, wrl_ref, rbias_ref, x1_ref, xn_ref, lg_ref)


def _odd_mixer(x, g_mix, win_bf, v_gain, wsp, bsp_t, wout_bf, g_ffn, wrh, wrl, rbias, *, l, tm, emit_v):
    n, d = x.shape
    half = wout_bf.shape[0]
    groups = wsp.shape[0]
    nr = wrh.shape[0]
    assert n % tm == 0 and tm % l == 0
    row = lambda i: (i, 0)
    out_specs = [pl.BlockSpec((tm, d), row), pl.BlockSpec((tm * (d // LANES), LANES), row),
                 pl.BlockSpec((nr, tm), lambda i: (0, i))]
    out_shape = [jax.ShapeDtypeStruct((n, d), F32), jax.ShapeDtypeStruct((n * (d // LANES), LANES), F32),
                 jax.ShapeDtypeStruct((nr, n), F32)]
    scratch = [pltpu.VMEM((tm, half), F32)]
    if emit_v:
        out_specs.append(pl.BlockSpec((tm, half), row))
        out_shape.append(jax.ShapeDtypeStruct((n, half), F32))
    else:
        scratch.append(pltpu.VMEM((tm, half), F32))
    scratch.append(pltpu.VMEM((tm, half), F32))
    return pl.pallas_call(
        functools.partial(_odd_kernel, half=half, groups=groups, l=l, emit_v=emit_v),
        grid=(n // tm,),
        in_specs=[pl.BlockSpec((tm, d), row), _const_spec((1, d)), _const_spec(win_bf.shape),
                  _const_spec((1, half)), _const_spec(wsp.shape), _const_spec(bsp_t.shape),
                  _const_spec(wout_bf.shape), _const_spec((1, d)), _const_spec(wrh.shape),
                  _const_spec(wrl.shape), _const_spec(rbias.shape)],
        out_specs=out_specs,
        out_shape=out_shape,
        scratch_shapes=scratch,
        compiler_params=_params("arbitrary"),
        name="odd_mixer",
    )(x, g_mix, win_bf, v_gain, wsp, bsp_t, wout_bf, g_ffn, wrh, wrl, rbias)


def _route_kernel(lg_ref, tri_ref, e_ref, g_ref, r_ref, cnt_ref, run_s, *, groups, epg):
    i = pl.program_id(0)

    @pl.when(i == 0)
    def _():
        run_s[...] = jnp.zeros(run_s.shape, F32)

    lg = lg_ref[...]
    tr = lg.shape[1]
    gl = [lg[g:g + 1] for g in range(groups)]
    m = functools.reduce(jnp.maximum, gl)
    grp = jnp.full((1, tr), groups - 1, I32)
    for g in range(groups - 2, -1, -1):
        grp = jnp.where(gl[g] == m, g, grp)
    gate_g = 1.0 / functools.reduce(lambda a, b: a + b, [jnp.exp(x - m) for x in gl])
    sel = lg[SUBLANES + (groups - 1) * epg:SUBLANES + groups * epg]
    for g in range(groups - 2, -1, -1):
        sel = jnp.where(grp == g, lg[SUBLANES + g * epg:SUBLANES + (g + 1) * epg], sel)
    sub = lax.broadcasted_iota(I32, sel.shape, 0)
    v1 = jnp.max(sel, axis=0, keepdims=True)
    i1 = jnp.min(jnp.where(sel == v1, sub, epg), axis=0, keepdims=True)
    sel2 = jnp.where(sub == i1, -jnp.inf, sel)
    v2 = jnp.max(sel2, axis=0, keepdims=True)
    i2 = jnp.min(jnp.where(sel2 == v2, sub, epg), axis=0, keepdims=True)
    tt = jnp.exp(v2 - v1)
    g1 = gate_g / (1.0 + tt)
    g2 = gate_g * tt / (1.0 + tt)
    e1 = grp * epg + i1
    e2 = grp * epg + i2
    ne = groups * epg
    eidx = lax.broadcasted_iota(I32, (ne, tr), 0)
    oh1 = eidx == e1
    oh2 = eidx == e2
    cnt = jnp.where(oh1, 1.0, 0.0) + jnp.where(oh2, 1.0, 0.0)
    before = run_s[:, 0:1] + _dot(cnt.astype(BF16), tri_ref[...])
    r1 = jnp.sum(jnp.where(oh1, before, 0.0), axis=0, keepdims=True)
    r2 = jnp.sum(jnp.where(oh2, before, 0.0), axis=0, keepdims=True)
    run_s[...] = run_s[...] + jnp.sum(cnt, axis=1, keepdims=True)
    rows = lax.broadcasted_iota(I32, (SUBLANES, tr), 0)
    e_ref[...] = jnp.where(rows == 0, e1, jnp.where(rows == 1, e2, 0))
    g_ref[...] = jnp.where(rows == 0, g1, jnp.where(rows == 1, g2, 0.0))
    r_ref[...] = jnp.where(rows == 0, r1, jnp.where(rows == 1, r2, 0.0)).astype(I32)
    cnt_ref[...] = run_s[...]


def _route(lgt, *, groups, epg, tr):
    nr, n = lgt.shape
    assert n % tr == 0 and nr == SUBLANES + groups * epg
    ne = groups * epg
    tri = (jnp.arange(tr)[:, None] < jnp.arange(tr)[None, :]).astype(BF16)
    tok = pl.BlockSpec((SUBLANES, tr), lambda i: (0, i))
    return pl.pallas_call(
        functools.partial(_route_kernel, groups=groups, epg=epg),
        grid=(n // tr,),
        in_specs=[pl.BlockSpec((nr, tr), lambda i: (0, i)), _const_spec((tr, tr))],
        out_specs=[tok, tok, tok, pl.BlockSpec((ne, LANES), lambda i: (0, 0))],
        out_shape=[jax.ShapeDtypeStruct((SUBLANES, n), I32), jax.ShapeDtypeStruct((SUBLANES, n), F32),
                   jax.ShapeDtypeStruct((SUBLANES, n), I32), jax.ShapeDtypeStruct((ne, LANES), F32)],
        scratch_shapes=[pltpu.VMEM((ne, LANES), F32)],
        compiler_params=_params("arbitrary"),
        name="route",
    )(lgt, tri)


ISSUE_UNROLL = 8


def _dispatch_kernel(seg_ref, dst_ref, x_ref, buf_ref, zero_s, sem, zsem, *, s, bm):
    tp = x_ref.shape[0] // s
    blk = bm * s

    @pl.when(pl.program_id(0) == 0)
    def _():
        zero_s[...] = jnp.zeros(zero_s.shape, F32)

        def block_copy(b):
            return pltpu.make_async_copy(zero_s, buf_ref.at[pl.ds(pl.multiple_of(b * blk, blk), blk)], zsem)

        for e in range(seg_ref.shape[1]):
            @pl.when(seg_ref[1, e] > 0)
            def _():
                block_copy(seg_ref[0, e] // bm - 1).start()
        for e in range(seg_ref.shape[1]):
            @pl.when(seg_ref[1, e] > 0)
            def _():
                block_copy(0).wait()

        def tail_start(b, c):
            block_copy(b).start()
            return c

        def tail_wait(b, c):
            block_copy(b).wait()
            return c

        first_unused = seg_ref[0, seg_ref.shape[1] - 1] // bm
        n_blocks = buf_ref.shape[0] // blk
        lax.fori_loop(first_unused, n_blocks, tail_start, 0)
        lax.fori_loop(first_unused, n_blocks, tail_wait, 0)

    def issue(r0, c):
        for u in range(ISSUE_UNROLL):
            r = r0 * ISSUE_UNROLL + u
            for kk in range(TOP_K):
                pltpu.make_async_copy(x_ref.at[_token_tile(r, s)],
                                      buf_ref.at[_token_tile(dst_ref[0, kk, r], s)], sem
                                      ).start(priority=kk % 2)
        return c

    lax.fori_loop(0, tp // ISSUE_UNROLL, issue, 0)
    for kk in range(TOP_K):
        pltpu.make_async_copy(x_ref, buf_ref.at[pl.ds(0, tp * s)], sem).wait()


def _dispatch(seg, dest3, xn, n_slots, *, tp, s, bm):
    n = xn.shape[0] // s
    assert tp % ISSUE_UNROLL == 0
    grid_spec = pltpu.PrefetchScalarGridSpec(
        num_scalar_prefetch=1,
        grid=(n // tp,),
        in_specs=[pl.BlockSpec((1, TOP_K, tp), lambda i, sg: (i, 0, 0), memory_space=pltpu.SMEM),
                  pl.BlockSpec((tp * s, LANES), lambda i, sg: (i, 0))],
        out_specs=pl.BlockSpec(memory_space=pl.ANY),
        scratch_shapes=[pltpu.VMEM((bm * s, LANES), F32), pltpu.SemaphoreType.DMA(()),
                        pltpu.SemaphoreType.DMA(())],
    )
    return pl.pallas_call(
        functools.partial(_dispatch_kernel, s=s, bm=bm),
        grid_spec=grid_spec,
        out_shape=jax.ShapeDtypeStruct((n_slots * s, LANES), F32),
        compiler_params=_params("arbitrary"),
        name="dispatch",
    )(seg, dest3, xn)


def _expert_kernel(be_ref, nu_ref, x_ref, wg_ref, wu_ref, wd_ref, o_ref, wg_s, wu_s, wd_s):
    b = pl.program_id(0)
    s = wg_s.shape[0] // LANES
    prev = be_ref[jnp.maximum(b - 1, 0)]

    @pl.when((b == 0) | (be_ref[b] != prev))
    def _():
        wg_s[...] = wg_ref[0, 0].astype(BF16)
        wu_s[...] = wu_ref[0, 0].astype(BF16)
        wd_s[...] = wd_ref[0, 0].astype(BF16)

    @pl.when(b < nu_ref[0])
    def _():
        xb = _load_token_tiles(x_ref, x_ref.shape[0] // s, s).astype(BF16)
        gate = _dot(xb, wg_s[...])
        h = gate * _sigmoid(gate) * _dot(xb, wu_s[...])
        _store_token_tiles(o_ref, _dot(h.astype(BF16), wd_s[...]))

    @pl.when(b >= nu_ref[0])
    def _():
        o_ref[...] = jnp.zeros(o_ref.shape, F32)


def _experts(blk_expert, n_used, buf, w_gate, w_up, w_down, *, layer, bm):
    d, de = w_gate.shape[2:]
    s = d // LANES
    n_blocks = buf.shape[0] // (bm * s)
    rows = pl.BlockSpec((bm * s, LANES), lambda b, be, nu: (b, 0))
    used_rows = pl.BlockSpec((bm * s, LANES), lambda b, be, nu: (jnp.minimum(b, nu[0] - 1), 0))
    grid_spec = pltpu.PrefetchScalarGridSpec(
        num_scalar_prefetch=2,
        grid=(n_blocks,),
        in_specs=[used_rows,
                  pl.BlockSpec((1, 1, d, de), lambda b, be, nu: (layer, be[b], 0, 0)),
                  pl.BlockSpec((1, 1, d, de), lambda b, be, nu: (layer, be[b], 0, 0)),
                  pl.BlockSpec((1, 1, de, d), lambda b, be, nu: (layer, be[b], 0, 0))],
        out_specs=rows,
        scratch_shapes=[pltpu.VMEM((d, de), BF16), pltpu.VMEM((d, de), BF16), pltpu.VMEM((de, d), BF16)],
    )
    return pl.pallas_call(
        _expert_kernel,
        grid_spec=grid_spec,
        out_shape=jax.ShapeDtypeStruct(buf.shape, F32),
        compiler_params=_params("arbitrary"),
        name="experts",
    )(blk_expert, n_used, buf, w_gate, w_up, w_down)


def _combine_kernel(dst_ref, gate_ref, x_ref, yb_ref, o_ref, rows_s, sem):
    tq, d = x_ref.shape
    s = d // LANES

    def issue(r0, c):
        for u in range(ISSUE_UNROLL):
            r = r0 * ISSUE_UNROLL + u
            for kk in range(TOP_K):
                pltpu.make_async_copy(yb_ref.at[_token_tile(dst_ref[0, kk, r], s)],
                                      rows_s.at[kk, _token_tile(r, s)], sem
                                      ).start(priority=kk % 2)
        return c

    lax.fori_loop(0, tq // ISSUE_UNROLL, issue, 0)
    for kk in range(TOP_K):
        pltpu.make_async_copy(yb_ref.at[pl.ds(0, tq * s)], rows_s.at[kk], sem).wait()
    g = gate_ref[...]
    o_ref[...] = (x_ref[...] + g[:, 0:1] * _load_token_tiles(rows_s.at[0], tq, s)
                  + g[:, 1:2] * _load_token_tiles(rows_s.at[1], tq, s))


def _combine(dest3, gates_t, x1, yb, *, tq):
    n, d = x1.shape
    s = d // LANES
    assert tq % ISSUE_UNROLL == 0
    return pl.pallas_call(
        _combine_kernel,
        grid=(n // tq,),
        in_specs=[pl.BlockSpec((1, TOP_K, tq), lambda i: (i, 0, 0), memory_space=pltpu.SMEM),
                  pl.BlockSpec((tq, TOP_K), lambda i: (i, 0)),
                  pl.BlockSpec((tq, d), lambda i: (i, 0)),
                  pl.BlockSpec(memory_space=pl.ANY)],
        out_specs=pl.BlockSpec((tq, d), lambda i: (i, 0)),
        out_shape=jax.ShapeDtypeStruct((n, d), F32),
        scratch_shapes=[pltpu.VMEM((TOP_K, tq * s, LANES), F32), pltpu.SemaphoreType.DMA(())],
        compiler_params=_params("arbitrary"),
        name="combine",
    )(dest3, gates_t, x1, yb)


def _moe(x1, xn, lgt, w_gate, w_up, w_down, *, layer, groups, epg):
    n, d = x1.shape
    ne = groups * epg
    tile = min(256, n)
    e8, g8, r8, cnt = _route(lgt, groups=groups, epg=epg, tr=min(512, n))
    counts = cnt[:, 0].astype(I32)
    bm = MOE_BLOCK if n * TOP_K >= 2 * ne * MOE_BLOCK else MOE_BLOCK_SMALL
    padded = (counts + bm - 1) // bm * bm
    pend = jnp.cumsum(padded)
    pstart = pend - padded
    n_blocks = -(-(n * TOP_K) // bm) + ne
    eids = jnp.arange(ne, dtype=I32)
    seg = jnp.sum(jnp.where(e8[:TOP_K, :, None] == eids, pstart, 0), axis=-1)
    dest = seg + r8[:TOP_K]
    dest3 = dest.reshape(TOP_K, n // tile, tile).transpose(1, 0, 2)
    blk_row = jnp.arange(n_blocks, dtype=I32)[:, None] * bm
    blk_expert = jnp.minimum(jnp.sum((pend[None, :] <= blk_row).astype(I32), axis=1), ne - 1)
    n_used = (pend[-1:] // bm).astype(I32)
    buf = _dispatch(jnp.stack([pend, padded]).astype(I32), dest3, xn, n_blocks * bm, tp=tile,
                    s=d // LANES, bm=bm)
    yb = _experts(blk_expert, n_used, buf, w_gate, w_up, w_down, layer=layer, bm=bm)
    return _combine(dest3, g8[:TOP_K].T, x1, yb, tq=tile)


def _router_weights(wg, bg, we, be):
    d, groups = wg.shape
    epg = we.shape[2]
    assert groups <= SUBLANES and epg == SUBLANES
    pad = jnp.zeros((SUBLANES - groups, d), F32)
    wr = jnp.concatenate([wg.T, pad, we.transpose(0, 2, 1).reshape(groups * epg, d)], axis=0)
    rb = jnp.concatenate([bg, jnp.zeros((SUBLANES - groups,), F32), be.reshape(-1)])[:, None]
    hi = wr.astype(BF16)
    lo = (wr - hi.astype(F32)).astype(BF16)
    return hi, lo, rb, groups, epg


def kernel(x_prompt, x_sample, cache_attn_k, cache_attn_v, state_hgrn, rel_bias, norm_mix, norm_ffn,
           w_in_even, w_out_even, q_norm_gain, k_norm_gain, lam_q1, lam_k1, lam_q2, lam_k2, da_out_gain,
           hgrn_lb_logits, hgrn_out_gain, w_in_odd, sgu_v_gain, sgu_w, sgu_b, w_out_odd,
           router_group_w, router_group_b, router_expert_w, router_expert_b,
           expert_w_gate, expert_w_up, expert_w_down):
    bp, tp, d = x_prompt.shape
    bs, ts, _ = x_sample.shape
    depth = norm_mix.shape[0]
    _, _, past, da_heads, _, da_dh = cache_attn_k.shape
    da_dv = cache_attn_v.shape[-1]
    _, _, hg_heads, hg_dk, hg_dv = state_hgrn.shape
    width = da_heads * da_dv
    assert width == da_heads * 2 * da_dh == hg_heads * hg_dk == hg_heads * hg_dv
    assert da_dv == LANES and hg_dk == LANES and hg_dv == LANES

    lb_all = jnp.cumsum(jax.nn.softmax(hgrn_lb_logits.astype(F32), axis=0), axis=0)
    gid = jnp.arange(width) // da_dh
    pm = jnp.where(gid[:, None] == gid[None, :], 1.0 / da_dh, 0.0).astype(BF16)

    xs = {"p": x_prompt.reshape(bp * tp, d), "s": x_sample.reshape(bs * ts, d)}
    dims = {"p": (bp, tp), "s": (bs, ts)}
    outs = {"p": {}, "s": {}}
    kp_l, vp_l, ks_l, vs_l, sp_l, ss_l, sgu_l = [], [], [], [], [], [], []

    for layer in range(depth):
        j = layer // 2
        wrh, wrl, rbias, groups, epg = _router_weights(
            router_group_w[layer], router_group_b[layer], router_expert_w[layer], router_expert_b[layer])
        g_mix = norm_mix[layer][None, :]
        g_ffn = norm_ffn[layer][None, :]
        if layer % 2 == 0:
            lam_init = 0.8 - 0.6 * math.exp(-0.3 * layer)
            w_in_bf = w_in_even[j].astype(BF16)
            w_out_bf = w_out_even[j].astype(BF16)
            reps = width // da_dh
            qg = jnp.tile(q_norm_gain[j], reps)[None, :]
            kg = jnp.tile(k_norm_gain[j], reps)[None, :]
            lam4 = jnp.stack([lam_q1[j], lam_k1[j], lam_q2[j], lam_k2[j]])
            sub_gain = da_out_gain[j][None, :]
            hg_gain = hgrn_out_gain[j][None, :]
            lb = lb_all[j][None, :]
            for key in ("p", "s"):
                b, t = dims[key]
                x = xs[key]
                q, k, v, qh, kb, lf, ih, gs = _in_even(
                    x, g_mix, w_in_bf, pm, qg, kg, lb, width=width, q_scale=da_dh ** -0.5, tm=min(512, b * t))
                if key == "p":
                    oa = _attn_prompt(q, k, v, rel_bias, lam4, sub_gain, batch=b, seq=t, heads=da_heads,
                                      dh=da_dh, lam_init=lam_init, qt=min(256, t))
                    s0 = jnp.zeros((b, hg_heads, hg_dk, hg_dv), F32)
                    ob, s_new = _hgrn(qh, kb, lf, ih, s0, hg_gain, batch=b, seq=t, heads=hg_heads,
                                      dk=hg_dk, dv=hg_dv, tb=min(512, t))
                    kp_l.append(k.reshape(b, t, da_heads, 2, da_dh))
                    vp_l.append(v.reshape(b, t, da_heads, da_dv))
                    sp_l.append(s_new)
                else:
                    ck = cache_attn_k[j].reshape(b, past, width)
                    cv = cache_attn_v[j].reshape(b, past, width)
                    oa = _attn_sample(q, k, v, ck, cv, rel_bias, lam4, sub_gain, batch=b, t=t,
                                      heads=da_heads, dh=da_dh, lam_init=lam_init)
                    ob, s_new = _hgrn(qh, kb, lf, ih, state_hgrn[j], hg_gain, batch=b, seq=t,
                                      heads=hg_heads, dk=hg_dk, dv=hg_dv, tb=t)
                    ks_l.append(k.reshape(b, t, da_heads, 2, da_dh))
                    vs_l.append(v.reshape(b, t, da_heads, da_dv))
                    ss_l.append(s_new)
                outs[key] = _out_even(oa, ob, gs, x, w_out_bf, g_ffn, wrh, wrl, rbias, tm=min(512, b * t))
        else:
            w_in_bf = w_in_odd[j].astype(BF16)
            w_out_bf = w_out_odd[j].astype(BF16)
            v_gain = sgu_v_gain[j][None, :]
            for key in ("p", "s"):
                b, t = dims[key]
                l = min(SGU_CHUNK, t)
                res = _odd_mixer(xs[key], g_mix, w_in_bf, v_gain, sgu_w[j][:, :l, :l], sgu_b[j][:, :l].T,
                                 w_out_bf, g_ffn, wrh, wrl, rbias, l=l, tm=min(512, b * t), emit_v=(key == "s"))
                outs[key] = res[:3]
                if key == "s":
                    sgu_l.append(res[3].reshape(b, t, -1))
        for key in ("p", "s"):
            x1, xn, lgt = outs[key]
            xs[key] = _moe(x1, xn, lgt, expert_w_gate, expert_w_up, expert_w_down, layer=layer,
                           groups=groups, epg=epg)

    return (xs["p"].reshape(bp, tp, d), xs["s"].reshape(bs, ts, d), jnp.stack(kp_l), jnp.stack(vp_l),
            jnp.stack(ks_l), jnp.stack(vs_l), jnp.stack(sp_l), jnp.stack(ss_l), jnp.stack(sgu_l))
```

```python
import functools
import math

import jax
import jax.numpy as jnp
from jax import lax
from jax.experimental import pallas as pl
from jax.experimental.pallas import tpu as pltpu

F32 = jnp.float32
BF16 = jnp.bfloat16
I32 = jnp.int32

EPS = 1e-6
LOG2E = math.log2(math.e)
CHUNK = 64
SGU_CHUNK = 128
REL_BUCKETS = 32
REL_MAX_DIST = 128
TOP_K = 2
MOE_BLOCK = 512
MOE_BLOCK_SMALL = 128
HG_SUB = 16

LANES = 128
SUBLANES = 8
VMEM_LIMIT = 56 * 1024 * 1024

NT_DIMS = (((1,), (1,)), ((), ()))
TN_DIMS = (((0,), (0,)), ((), ()))


def _params(*sem):
    return pltpu.CompilerParams(dimension_semantics=sem, vmem_limit_bytes=VMEM_LIMIT)


def _const_spec(shape):
    nd = len(shape)
    return pl.BlockSpec(shape, lambda *_: (0,) * nd, pipeline_mode=pl.Buffered(1))


def _sigmoid(x):
    return 1.0 / (1.0 + jnp.exp(-x))


def _rms(x, g):
    return x * lax.rsqrt(jnp.mean(x * x, axis=-1, keepdims=True) + EPS) * g


def _dot(a, b):
    return jnp.dot(a, b, preferred_element_type=F32)


def rel_bucket(rel):
    half = REL_BUCKETS // 2
    max_exact = half // 2
    ret = (rel > 0).astype(I32) * half
    n = jnp.abs(rel)
    nf = jnp.maximum(n, 1).astype(F32)
    large = max_exact + (jnp.log(nf / max_exact) / math.log(REL_MAX_DIST / max_exact)
                         * (half - max_exact)).astype(I32)
    large = jnp.minimum(large, half - 1)
    return ret + jnp.where(n < max_exact, n, large)


def _in_even_kernel(x_ref, g_ref, w_ref, pm_ref, qg_ref, kg_ref, lb_ref,
                    q_ref, k_ref, v_ref, qh_ref, kb_ref, lf_ref, ih_ref, gs_ref, *, width, q_scale):
    xn = _rms(x_ref[...], g_ref[...]).astype(BF16)

    def proj(c):
        return _dot(xn, w_ref[:, c * width:(c + 1) * width])

    def group_norm(y, gain):
        ms = _dot((y * y).astype(BF16), pm_ref[...])
        return y * lax.rsqrt(ms + EPS) * gain

    q_ref[...] = group_norm(proj(0), qg_ref[...]) * q_scale
    k_ref[...] = group_norm(proj(1), kg_ref[...])
    v_ref[...] = proj(2)
    yq = proj(3)
    qh_ref[...] = yq * _sigmoid(yq)
    zf = proj(4)
    lb = lb_ref[...]
    lf_ref[...] = jnp.log(lb + (1.0 - lb) * _sigmoid(zf))
    kb_ref[...] = (1.0 - lb) * _sigmoid(-zf)
    ih_ref[...] = proj(5)
    yg = proj(6)
    gs_ref[...] = yg * _sigmoid(yg)


def _in_even(x, g_mix, w_bf, pm, qg, kg, lb, *, width, q_scale, tm):
    n, d = x.shape
    assert n % tm == 0
    row = lambda i: (i, 0)
    out = jax.ShapeDtypeStruct((n, width), F32)
    return pl.pallas_call(
        functools.partial(_in_even_kernel, width=width, q_scale=q_scale),
        grid=(n // tm,),
        in_specs=[pl.BlockSpec((tm, d), row), _const_spec((1, d)), _const_spec(w_bf.shape),
                  _const_spec(pm.shape), _const_spec((1, width)), _const_spec((1, width)),
                  _const_spec((1, width))],
        out_specs=[pl.BlockSpec((tm, width), row)] * 8,
        out_shape=[out] * 8,
        compiler_params=_params("arbitrary"),
        name="in_even",
    )(x, g_mix, w_bf, pm, qg, kg, lb)


def _bias_from_buckets(bk, rb_ref, h):
    b = jnp.zeros(bk.shape, F32)
    for u in range(REL_BUCKETS):
        b = jnp.where(bk == u, rb_ref[u, h], b)
    return jnp.where(bk < 0, -jnp.inf, b)


def _lam(lam_ref, lam_init):
    r = lam_ref[...]
    s1 = jnp.sum(r[0:1] * r[1:2], axis=1, keepdims=True)
    s2 = jnp.sum(r[2:3] * r[3:4], axis=1, keepdims=True)
    return jnp.exp(s1) - jnp.exp(s2) + lam_init


def _split_components(q, dh):
    lane = lax.broadcasted_iota(I32, q.shape, 1)
    q0 = jnp.where(lane < dh, q, 0.0)
    q1 = jnp.where(lane >= dh, q, 0.0)
    return jnp.concatenate([q0, q1], axis=0).astype(BF16)


def _attn_prompt_kernel(rb_ref, far_ref, lam_ref, bk_ref, sg_ref, q_ref, k_ref, v_ref, o_ref, *,
                        qt, dh, lam_init):
    h = pl.program_id(1)
    t = q_ref.shape[0]
    kb = k_ref[...].astype(BF16)
    vt = v_ref[...].T.astype(BF16)
    qtr = q_ref[...].T
    sub = lax.broadcasted_iota(I32, (2 * dh, qt), 0)
    bias = []
    for d in range(2):
        b = _bias_from_buckets(bk_ref[d], rb_ref, h) * LOG2E
        bias.append(jnp.concatenate([b, b], axis=1))
    far = rb_ref[far_ref[0], h] * LOG2E
    lam = _lam(lam_ref, lam_init)
    gain = sg_ref[...] * (1.0 - lam_init)
    for i in range(t // qt):
        qi = qtr[:, i * qt:(i + 1) * qt]
        qz = jnp.concatenate([jnp.where(sub < dh, qi, 0.0), jnp.where(sub >= dh, qi, 0.0)],
                             axis=1).astype(BF16)
        n = (i + 1) * qt
        parts = [(n - qt, n)]
        s = [_dot(kb[n - qt:n], qz) + bias[0]]
        shift = [0.0]
        if i >= 1:
            parts.append((n - 2 * qt, n - qt))
            s.append(_dot(kb[n - 2 * qt:n - qt], qz) + bias[1])
            shift.append(0.0)
        if i >= 2:
            parts.append((0, n - 2 * qt))
            s.append(_dot(kb[:n - 2 * qt], qz))
            shift.append(far)
        m = functools.reduce(jnp.maximum, [jnp.max(x, axis=0, keepdims=True) + c for x, c in zip(s, shift)])
        p = [jnp.exp2(x - (m - c)) for x, c in zip(s, shift)]
        l = functools.reduce(lambda a, b: a + b, [jnp.sum(x, axis=0, keepdims=True) for x in p])
        acc = functools.reduce(lambda a, b: a + b,
                               [_dot(vt[:, lo:hi], x.astype(BF16)) for (lo, hi), x in zip(parts, p)])
        o = acc / l
        out = o[:, :qt] - lam * o[:, qt:]
        out = out * lax.rsqrt(jnp.mean(out * out, axis=0, keepdims=True) + EPS) * gain
        o_ref[i * qt:(i + 1) * qt, :] = out.T


def _attn_prompt(q, k, v, rel_bias, lam4, sub_gain, *, batch, seq, heads, dh, lam_init, qt):
    n, w = q.shape
    dv = w // heads
    assert dv == 2 * dh and seq % qt == 0 and qt % CHUNK == 0
    kj = jnp.arange(qt, dtype=I32)[:, None]
    qi = jnp.arange(qt, dtype=I32)[None, :]
    bk0 = jnp.where((kj // CHUNK) <= (qi // CHUNK), rel_bucket(kj - qi), -1)
    bk1 = rel_bucket(kj - qi - qt)
    bk = jnp.stack([bk0, bk1]).astype(I32)
    assert qt + 1 >= REL_MAX_DIST
    far = rel_bucket(jnp.full((1,), -(qt + 1), I32))
    smem = pl.BlockSpec(memory_space=pltpu.SMEM)
    seq_blk = pl.BlockSpec((seq, dv), lambda b, h: (b, h))
    return pl.pallas_call(
        functools.partial(_attn_prompt_kernel, qt=qt, dh=dh, lam_init=lam_init),
        grid=(batch, heads),
        in_specs=[smem, smem, _const_spec(lam4.shape), _const_spec(bk.shape), _const_spec((dv, 1)),
                  seq_blk, seq_blk, seq_blk],
        out_specs=seq_blk,
        out_shape=jax.ShapeDtypeStruct((n, w), F32),
        compiler_params=_params("arbitrary", "arbitrary"),
        name="attn_prompt",
    )(rel_bias, far, lam4, bk, sub_gain.T, q, k, v)


def _attn_sample_kernel(rb_ref, lam_ref, bkc_ref, bkn_ref, sg_ref, q_ref, kc_ref, vc_ref, kn_ref, vn_ref,
                        o_ref, *, t, dh, lam_init):
    h = pl.program_id(1)
    qz = _split_components(q_ref[...], dh)
    bc = _bias_from_buckets(bkc_ref[...], rb_ref, h)
    bn = _bias_from_buckets(bkn_ref[...], rb_ref, h)
    sc = lax.dot_general(qz, kc_ref[0].astype(BF16), NT_DIMS, preferred_element_type=F32)
    sn = lax.dot_general(qz, kn_ref[...].astype(BF16), NT_DIMS, preferred_element_type=F32)
    sc = sc + jnp.concatenate([bc, bc], axis=0) * LOG2E
    sn = sn + jnp.concatenate([bn, bn], axis=0) * LOG2E
    m = jnp.maximum(jnp.max(sc, axis=1, keepdims=True), jnp.max(sn, axis=1, keepdims=True))
    pc = jnp.exp2(sc - m)
    pn = jnp.exp2(sn - m)
    l = jnp.sum(pc, axis=1, keepdims=True) + jnp.sum(pn, axis=1, keepdims=True)
    acc = _dot(pc.astype(BF16), vc_ref[0].astype(BF16)) + _dot(pn.astype(BF16), vn_ref[...].astype(BF16))
    o = acc / l
    out = o[:t] - _lam(lam_ref, lam_init) * o[t:]
    o_ref[...] = _rms(out, sg_ref[...]) * (1.0 - lam_init)


def _attn_sample(q, k_new, v_new, cache_k, cache_v, rel_bias, lam4, sub_gain, *, batch, t, heads, dh,
                 lam_init):
    n, w = q.shape
    dv = w // heads
    past = cache_k.shape[1]
    assert past % CHUNK == 0 and t <= CHUNK
    qpos = past + jnp.arange(t, dtype=I32)[:, None]
    bkc = rel_bucket(jnp.arange(past, dtype=I32)[None, :] - qpos).astype(I32)
    bkn = rel_bucket(past + jnp.arange(t, dtype=I32)[None, :] - qpos).astype(I32)
    smem = pl.BlockSpec(memory_space=pltpu.SMEM)
    new = pl.BlockSpec((t, dv), lambda b, h: (b, h))
    old = pl.BlockSpec((1, past, dv), lambda b, h: (b, 0, h))
    return pl.pallas_call(
        functools.partial(_attn_sample_kernel, t=t, dh=dh, lam_init=lam_init),
        grid=(batch, heads),
        in_specs=[smem, _const_spec(lam4.shape), _const_spec(bkc.shape), _const_spec(bkn.shape),
                  _const_spec((1, dv)), new, old, old, new, new],
        out_specs=new,
        out_shape=jax.ShapeDtypeStruct((n, w), F32),
        compiler_params=_params("arbitrary", "arbitrary"),
        name="attn_sample",
    )(rel_bias, lam4, bkc, bkn, sub_gain, q, cache_k, cache_v, k_new, v_new)


def _cumsum_rows(x):
    c = x.shape[0]
    row = lax.broadcasted_iota(I32, x.shape, 0)
    s = 1
    while s < c:
        x = x + jnp.where(row >= s, pltpu.roll(x, s, axis=0), 0.0)
        s *= 2
    return x


def _hgrn_kernel(qh_ref, kb_ref, lf_ref, ih_ref, s0_ref, hg_ref, ob_ref, sf_ref, st_s, *,
                 heads, dk, dv, c, nsb):
    t = pl.program_id(1)

    @pl.when(t == 0)
    def _():
        for h in range(heads):
            st_s[h] = s0_ref[0, h].T

    tb = qh_ref.shape[0]
    row = lax.broadcasted_iota(I32, (c, c), 0)
    col = lax.broadcasted_iota(I32, (c, c), 1)
    causal = col <= row

    def chunk(ci, carry):
        r0 = pl.multiple_of(ci * c, c)
        for h in range(heads):
            rows = pl.ds(r0, c)
            q = qh_ref[rows, h * dk:(h + 1) * dk]
            k = kb_ref[rows, h * dk:(h + 1) * dk]
            v = ih_ref[rows, h * dv:(h + 1) * dv]
            b = _cumsum_rows(lf_ref[rows, h * dk:(h + 1) * dk])
            bl = b[c - 1:c]
            st = st_s[h]
            inter = lax.dot_general((q * jnp.exp(b)).astype(BF16), st.astype(BF16), NT_DIMS,
                                    preferred_element_type=F32)
            qs, ks = [], []
            for j in range(nsb):
                ref = b[j * HG_SUB + HG_SUB // 2:j * HG_SUB + HG_SUB // 2 + 1]
                qs.append(q * jnp.exp(b - ref))
                sub = slice(j * HG_SUB, (j + 1) * HG_SUB)
                ks.append(k[sub] * jnp.exp(ref - b[sub]))
            a_full = lax.dot_general(jnp.concatenate(qs, axis=0).astype(BF16),
                                     jnp.concatenate(ks, axis=0).astype(BF16), NT_DIMS,
                                     preferred_element_type=F32)
            att = jnp.zeros((c, c), F32)
            for j in range(nsb):
                att = jnp.where(col >= j * HG_SUB, a_full[j * c:(j + 1) * c], att)
            att = jnp.where(causal, att, 0.0)
            out = inter + _dot(att.astype(BF16), v.astype(BF16))
            ob_ref[rows, h * dv:(h + 1) * dv] = _rms(out, hg_ref[...])
            kdec = (k * jnp.exp(bl - b)).astype(BF16)
            st_s[h] = jnp.exp(bl) * st + lax.dot_general(v.astype(BF16), kdec, TN_DIMS,
                                                         preferred_element_type=F32)
        return carry

    lax.fori_loop(0, tb // c, chunk, 0, unroll=min(4, tb // c))

    @pl.when(t == pl.num_programs(1) - 1)
    def _():
        for h in range(heads):
            sf_ref[0, h] = st_s[h].T


def _hgrn(qh, kb, lf, ih, s0, hg_gain, *, batch, seq, heads, dk, dv, tb):
    n = qh.shape[0]
    c = min(CHUNK, seq)
    assert seq % tb == 0 and tb % c == 0 and c % HG_SUB == 0
    nt = seq // tb
    blk = lambda w: pl.BlockSpec((tb, w), lambda b, t: (b * nt + t, 0))
    st = pl.BlockSpec((1, heads, dk, dv), lambda b, t: (b, 0, 0, 0))
    return pl.pallas_call(
        functools.partial(_hgrn_kernel, heads=heads, dk=dk, dv=dv, c=c, nsb=c // HG_SUB),
        grid=(batch, nt),
        in_specs=[blk(heads * dk), blk(heads * dk), blk(heads * dk), blk(heads * dv), st,
                  _const_spec((1, dv))],
        out_specs=[blk(heads * dv), st],
        out_shape=[jax.ShapeDtypeStruct((n, heads * dv), F32),
                   jax.ShapeDtypeStruct((batch, heads, dk, dv), F32)],
        scratch_shapes=[pltpu.VMEM((heads, dv, dk), F32)],
        compiler_params=_params("arbitrary", "arbitrary"),
        name="hgrn",
    )(qh, kb, lf, ih, s0, hg_gain)


def _store_token_tiles(ref, x):
    rows, d = x.shape
    s = d // LANES
    for c in range(s):
        ref[pl.ds(c, rows, stride=s), :] = x[:, c * LANES:(c + 1) * LANES]


def _load_token_tiles(ref, rows, s):
    return jnp.concatenate([ref[pl.ds(c, rows, stride=s), :] for c in range(s)], axis=1)


def _token_tile(r, s):
    return pl.ds(pl.multiple_of(r * s, s), s)


def _ffn_prologue(x1, gf_ref, wrh_ref, wrl_ref, rbias_ref, x1_ref, xn_ref, lg_ref):
    x1_ref[...] = x1
    xn = _rms(x1, gf_ref[...])
    _store_token_tiles(xn_ref, xn)
    hi = xn.astype(BF16)
    lo = (xn - hi.astype(F32)).astype(BF16)
    nt = functools.partial(lax.dot_general, dimension_numbers=NT_DIMS, preferred_element_type=F32)
    lg_ref[...] = nt(wrh_ref[...], hi) + nt(wrh_ref[...], lo) + nt(wrl_ref[...], hi) + rbias_ref[...]


def _out_even_kernel(oa_ref, ob_ref, gs_ref, x_ref, w_ref, gf_ref, wrh_ref, wrl_ref, rbias_ref,
                     x1_ref, xn_ref, lg_ref):
    o = jnp.concatenate([oa_ref[...], ob_ref[...] * gs_ref[...]], axis=1).astype(BF16)
    x1 = x_ref[...] + _dot(o, w_ref[...])
    _ffn_prologue(x1, gf_ref, wrh_ref, wrl_ref, rbias_ref, x1_ref, xn_ref, lg_ref)


def _out_even(oa, ob, gs, x, w_bf, g_ffn, wrh, wrl, rbias, *, tm):
    n, d = x.shape
    w = oa.shape[1]
    nr = wrh.shape[0]
    row = lambda i: (i, 0)
    return pl.pallas_call(
        _out_even_kernel,
        grid=(n // tm,),
        in_specs=[pl.BlockSpec((tm, w), row), pl.BlockSpec((tm, w), row), pl.BlockSpec((tm, w), row),
                  pl.BlockSpec((tm, d), row), _const_spec(w_bf.shape), _const_spec((1, d)),
                  _const_spec(wrh.shape), _const_spec(wrl.shape), _const_spec(rbias.shape)],
        out_specs=[pl.BlockSpec((tm, d), row), pl.BlockSpec((tm * (d // LANES), LANES), row),
                   pl.BlockSpec((nr, tm), lambda i: (0, i))],
        out_shape=[jax.ShapeDtypeStruct((n, d), F32), jax.ShapeDtypeStruct((n * (d // LANES), LANES), F32),
                   jax.ShapeDtypeStruct((nr, n), F32)],
        compiler_params=_params("arbitrary"),
        name="out_even",
    )(oa, ob, gs, x, w_bf, g_ffn, wrh, wrl, rbias)


def _gelu(x):
    return 0.5 * x * (1.0 + jnp.tanh(math.sqrt(2.0 / math.pi) * (x + 0.044715 * (x * x * x))))


def _odd_kernel(x_ref, gm_ref, win_ref, vg_ref, wsp_ref, bsp_ref, wout_ref, gf_ref, wrh_ref, wrl_ref,
                rbias_ref, x1_ref, xn_ref, lg_ref, *rest, half, groups, l, emit_v):
    if emit_v:
        vn_ref, u_s, s_s = rest
    else:
        vn_ref = None
        u_s, vn_s, s_s = rest
    tm = x_ref.shape[0]
    x = x_ref[...]
    xn = _rms(x, gm_ref[...]).astype(BF16)
    cw = 512
    vbuf = vn_ref if emit_v else vn_s
    for cidx in range(half // cw):
        u_s[:, cidx * cw:(cidx + 1) * cw] = _gelu(_dot(xn, win_ref[:, cidx * cw:(cidx + 1) * cw]))
        vbuf[:, cidx * cw:(cidx + 1) * cw] = _gelu(
            _dot(xn, win_ref[:, half + cidx * cw:half + (cidx + 1) * cw]))
    vbuf[...] = _rms(vbuf[...], vg_ref[...])
    gw = half // groups
    row = lax.broadcasted_iota(I32, (l, l), 0)
    col = lax.broadcasted_iota(I32, (l, l), 1)
    for g in range(groups):
        wg = jnp.where(col <= row, wsp_ref[g], 0.0).astype(BF16)
        bg = bsp_ref[:, g:g + 1]
        for ci in range(tm // l):
            vv = vbuf[ci * l:(ci + 1) * l, g * gw:(g + 1) * gw].astype(BF16)
            s_s[ci * l:(ci + 1) * l, g * gw:(g + 1) * gw] = _dot(wg, vv) + bg
    y = _dot((u_s[...] * s_s[...]).astype(BF16), wout_ref[...])
    _ffn_prologue(x + y, gf_ref, wrh_ref, wrl_ref, rbias_ref, x1_ref, xn_ref, lg_ref)


def _odd_mixer(x, g_mix, win_bf, v_gain, wsp, bsp_t, wout_bf, g_ffn, wrh, wrl, rbias, *, l, tm, emit_v):
    n, d = x.shape
    half = wout_bf.shape[0]
    groups = wsp.shape[0]
    nr = wrh.shape[0]
    assert n % tm == 0 and tm % l == 0
    row = lambda i: (i, 0)
    out_specs = [pl.BlockSpec((tm, d), row), pl.BlockSpec((tm * (d // LANES), LANES), row),
                 pl.BlockSpec((nr, tm), lambda i: (0, i))]
    out_shape = [jax.ShapeDtypeStruct((n, d), F32), jax.ShapeDtypeStruct((n * (d // LANES), LANES), F32),
                 jax.ShapeDtypeStruct((nr, n), F32)]
    scratch = [pltpu.VMEM((tm, half), F32)]
    if emit_v:
        out_specs.append(pl.BlockSpec((tm, half), row))
        out_shape.append(jax.ShapeDtypeStruct((n, half), F32))
    else:
        scratch.append(pltpu.VMEM((tm, half), F32))
    scratch.append(pltpu.VMEM((tm, half), F32))
    return pl.pallas_call(
        functools.partial(_odd_kernel, half=half, groups=groups, l=l, emit_v=emit_v),
        grid=(n // tm,),
        in_specs=[pl.BlockSpec((tm, d), row), _const_spec((1, d)), _const_spec(win_bf.shape),
                  _const_spec((1, half)), _const_spec(wsp.shape), _const_spec(bsp_t.shape),
                  _const_spec(wout_bf.shape), _const_spec((1, d)), _const_spec(wrh.shape),
                  _const_spec(wrl.shape), _const_spec(rbias.shape)],
        out_specs=out_specs,
        out_shape=out_shape,
        scratch_shapes=scratch,
        compiler_params=_params("arbitrary"),
        name="odd_mixer",
    )(x, g_mix, win_bf, v_gain, wsp, bsp_t, wout_bf, g_ffn, wrh, wrl, rbias)


def _route_kernel(lg_ref, tri_ref, e_ref, g_ref, r_ref, cnt_ref, run_s, *, groups, epg):
    i = pl.program_id(0)

    @pl.when(i == 0)
    def _():
        run_s[...] = jnp.zeros(run_s.shape, F32)

    lg = lg_ref[...]
    tr = lg.shape[1]
    gl = [lg[g:g + 1] for g in range(groups)]
    m = functools.reduce(jnp.maximum, gl)
    grp = jnp.full((1, tr), groups - 1, I32)
    for g in range(groups - 2, -1, -1):
        grp = jnp.where(gl[g] == m, g, grp)
    gate_g = 1.0 / functools.reduce(lambda a, b: a + b, [jnp.exp(x - m) for x in gl])
    sel = lg[SUBLANES + (groups - 1) * epg:SUBLANES + groups * epg]
    for g in range(groups - 2, -1, -1):
        sel = jnp.where(grp == g, lg[SUBLANES + g * epg:SUBLANES + (g + 1) * epg], sel)
    sub = lax.broadcasted_iota(I32, sel.shape, 0)
    v1 = jnp.max(sel, axis=0, keepdims=True)
    i1 = jnp.min(jnp.where(sel == v1, sub, epg), axis=0, keepdims=True)
    sel2 = jnp.where(sub == i1, -jnp.inf, sel)
    v2 = jnp.max(sel2, axis=0, keepdims=True)
    i2 = jnp.min(jnp.where(sel2 == v2, sub, epg), axis=0, keepdims=True)
    tt = jnp.exp(v2 - v1)
    g1 = gate_g / (1.0 + tt)
    g2 = gate_g * tt / (1.0 + tt)
    e1 = grp * epg + i1
    e2 = grp * epg + i2
    ne = groups * epg
    eidx = lax.broadcasted_iota(I32, (ne, tr), 0)
    oh1 = eidx == e1
    oh2 = eidx == e2
    cnt = jnp.where(oh1, 1.0, 0.0) + jnp.where(oh2, 1.0, 0.0)
    before = run_s[:, 0:1] + _dot(cnt.astype(BF16), tri_ref[...])
    r1 = jnp.sum(jnp.where(oh1, before, 0.0), axis=0, keepdims=True)
    r2 = jnp.sum(jnp.where(oh2, before, 0.0), axis=0, keepdims=True)
    run_s[...] = run_s[...] + jnp.sum(cnt, axis=1, keepdims=True)
    rows = lax.broadcasted_iota(I32, (SUBLANES, tr), 0)
    e_ref[...] = jnp.where(rows == 0, e1, jnp.where(rows == 1, e2, 0))
    g_ref[...] = jnp.where(rows == 0, g1, jnp.where(rows == 1, g2, 0.0))
    r_ref[...] = jnp.where(rows == 0, r1, jnp.where(rows == 1, r2, 0.0)).astype(I32)
    cnt_ref[...] = run_s[...]


def _route(lgt, *, groups, epg, tr):
    nr, n = lgt.shape
    assert n % tr == 0 and nr == SUBLANES + groups * epg
    ne = groups * epg
    tri = (jnp.arange(tr)[:, None] < jnp.arange(tr)[None, :]).astype(BF16)
    tok = pl.BlockSpec((SUBLANES, tr), lambda i: (0, i))
    return pl.pallas_call(
        functools.partial(_route_kernel, groups=groups, epg=epg),
        grid=(n // tr,),
        in_specs=[pl.BlockSpec((nr, tr), lambda i: (0, i)), _const_spec((tr, tr))],
        out_specs=[tok, tok, tok, pl.BlockSpec((ne, LANES), lambda i: (0, 0))],
        out_shape=[jax.ShapeDtypeStruct((SUBLANES, n), I32), jax.ShapeDtypeStruct((SUBLANES, n), F32),
                   jax.ShapeDtypeStruct((SUBLANES, n), I32), jax.ShapeDtypeStruct((ne, LANES), F32)],
        scratch_shapes=[pltpu.VMEM((ne, LANES), F32)],
        compiler_params=_params("arbitrary"),
        name="route",
    )(lgt, tri)


ISSUE_UNROLL = 8


def _dispatch_kernel(seg_ref, dst_ref, x_ref, buf_ref, zero_s, sem, zsem, *, s, bm):
    tp = x_ref.shape[0] // s
    blk = bm * s

    @pl.when(pl.program_id(0) == 0)
    def _():
        zero_s[...] = jnp.zeros(zero_s.shape, F32)

        def block_copy(b):
            return pltpu.make_async_copy(zero_s, buf_ref.at[pl.ds(pl.multiple_of(b * blk, blk), blk)], zsem)

        for e in range(seg_ref.shape[1]):
            @pl.when(seg_ref[1, e] > 0)
            def _():
                block_copy(seg_ref[0, e] // bm - 1).start()
        for e in range(seg_ref.shape[1]):
            @pl.when(seg_ref[1, e] > 0)
            def _():
                block_copy(0).wait()

        def tail_start(b, c):
            block_copy(b).start()
            return c

        def tail_wait(b, c):
            block_copy(b).wait()
            return c

        first_unused = seg_ref[0, seg_ref.shape[1] - 1] // bm
        n_blocks = buf_ref.shape[0] // blk
        lax.fori_loop(first_unused, n_blocks, tail_start, 0)
        lax.fori_loop(first_unused, n_blocks, tail_wait, 0)

    def issue(r0, c):
        for u in range(ISSUE_UNROLL):
            r = r0 * ISSUE_UNROLL + u
            for kk in range(TOP_K):
                pltpu.make_async_copy(x_ref.at[_token_tile(r, s)],
                                      buf_ref.at[_token_tile(dst_ref[0, kk, r], s)], sem
                                      ).start(priority=kk % 2)
        return c

    lax.fori_loop(0, tp // ISSUE_UNROLL, issue, 0)
    for kk in range(TOP_K):
        pltpu.make_async_copy(x_ref, buf_ref.at[pl.ds(0, tp * s)], sem).wait()


def _dispatch(seg, dest3, xn, n_slots, *, tp, s, bm):
    n = xn.shape[0] // s
    assert tp % ISSUE_UNROLL == 0
    grid_spec = pltpu.PrefetchScalarGridSpec(
        num_scalar_prefetch=1,
        grid=(n // tp,),
        in_specs=[pl.BlockSpec((1, TOP_K, tp), lambda i, sg: (i, 0, 0), memory_space=pltpu.SMEM),
                  pl.BlockSpec((tp * s, LANES), lambda i, sg: (i, 0))],
        out_specs=pl.BlockSpec(memory_space=pl.ANY),
        scratch_shapes=[pltpu.VMEM((bm * s, LANES), F32), pltpu.SemaphoreType.DMA(()),
                        pltpu.SemaphoreType.DMA(())],
    )
    return pl.pallas_call(
        functools.partial(_dispatch_kernel, s=s, bm=bm),
        grid_spec=grid_spec,
        out_shape=jax.ShapeDtypeStruct((n_slots * s, LANES), F32),
        compiler_params=_params("arbitrary"),
        name="dispatch",
    )(seg, dest3, xn)


def _expert_kernel(be_ref, nu_ref, x_ref, wg_ref, wu_ref, wd_ref, o_ref, wg_s, wu_s, wd_s):
    b = pl.program_id(0)
    s = wg_s.shape[0] // LANES
    prev = be_ref[jnp.maximum(b - 1, 0)]

    @pl.when((b == 0) | (be_ref[b] != prev))
    def _():
        wg_s[...] = wg_ref[0, 0].astype(BF16)
        wu_s[...] = wu_ref[0, 0].astype(BF16)
        wd_s[...] = wd_ref[0, 0].astype(BF16)

    @pl.when(b < nu_ref[0])
    def _():
        xb = _load_token_tiles(x_ref, x_ref.shape[0] // s, s).astype(BF16)
        gate = _dot(xb, wg_s[...])
        h = gate * _sigmoid(gate) * _dot(xb, wu_s[...])
        _store_token_tiles(o_ref, _dot(h.astype(BF16), wd_s[...]))

    @pl.when(b >= nu_ref[0])
    def _():
        o_ref[...] = jnp.zeros(o_ref.shape, F32)


def _experts(blk_expert, n_used, buf, w_gate, w_up, w_down, *, layer, bm):
    d, de = w_gate.shape[2:]
    s = d // LANES
    n_blocks = buf.shape[0] // (bm * s)
    rows = pl.BlockSpec((bm * s, LANES), lambda b, be, nu: (b, 0))
    used_rows = pl.BlockSpec((bm * s, LANES), lambda b, be, nu: (jnp.minimum(b, nu[0] - 1), 0))
    grid_spec = pltpu.PrefetchScalarGridSpec(
        num_scalar_prefetch=2,
        grid=(n_blocks,),
        in_specs=[used_rows,
                  pl.BlockSpec((1, 1, d, de), lambda b, be, nu: (layer, be[b], 0, 0)),
                  pl.BlockSpec((1, 1, d, de), lambda b, be, nu: (layer, be[b], 0, 0)),
                  pl.BlockSpec((1, 1, de, d), lambda b, be, nu: (layer, be[b], 0, 0))],
        out_specs=rows,
        scratch_shapes=[pltpu.VMEM((d, de), BF16), pltpu.VMEM((d, de), BF16), pltpu.VMEM((de, d), BF16)],
    )
    return pl.pallas_call(
        _expert_kernel,
        grid_spec=grid_spec,
        out_shape=jax.ShapeDtypeStruct(buf.shape, F32),
        compiler_params=_params("arbitrary"),
        name="experts",
    )(blk_expert, n_used, buf, w_gate, w_up, w_down)


def _combine_kernel(dst_ref, gate_ref, x_ref, yb_ref, o_ref, rows_s, sem):
    tq, d = x_ref.shape
    s = d // LANES

    def issue(r0, c):
        for u in range(ISSUE_UNROLL):
            r = r0 * ISSUE_UNROLL + u
            for kk in range(TOP_K):
                pltpu.make_async_copy(yb_ref.at[_token_tile(dst_ref[0, kk, r], s)],
                                      rows_s.at[kk, _token_tile(r, s)], sem
                                      ).start(priority=kk % 2)
        return c

    lax.fori_loop(0, tq // ISSUE_UNROLL, issue, 0)
    for kk in range(TOP_K):
        pltpu.make_async_copy(yb_ref.at[pl.ds(0, tq * s)], rows_s.at[kk], sem).wait()
    g = gate_ref[...]
    o_ref[...] = (x_ref[...] + g[:, 0:1] * _load_token_tiles(rows_s.at[0], tq, s)
                  + g[:, 1:2] * _load_token_tiles(rows_s.at[1], tq, s))


def _combine(dest3, gates_t, x1, yb, *, tq):
    n, d = x1.shape
    s = d // LANES
    assert tq % ISSUE_UNROLL == 0
    return pl.pallas_call(
        _combine_kernel,
        grid=(n // tq,),
        in_specs=[pl.BlockSpec((1, TOP_K, tq), lambda i: (i, 0, 0), memory_space=pltpu.SMEM),
                  pl.BlockSpec((tq, TOP_K), lambda i: (i, 0)),
                  pl.BlockSpec((tq, d), lambda i: (i, 0)),
                  pl.BlockSpec(memory_space=pl.ANY)],
        out_specs=pl.BlockSpec((tq, d), lambda i: (i, 0)),
        out_shape=jax.ShapeDtypeStruct((n, d), F32),
        scratch_shapes=[pltpu.VMEM((TOP_K, tq * s, LANES), F32), pltpu.SemaphoreType.DMA(())],
        compiler_params=_params("arbitrary"),
        name="combine",
    )(dest3, gates_t, x1, yb)


def _moe(x1, xn, lgt, w_gate, w_up, w_down, *, layer, groups, epg):
    n, d = x1.shape
    ne = groups * epg
    tile = min(256, n)
    e8, g8, r8, cnt = _route(lgt, groups=groups, epg=epg, tr=min(512, n))
    counts = cnt[:, 0].astype(I32)
    bm = MOE_BLOCK if n * TOP_K >= 2 * ne * MOE_BLOCK else MOE_BLOCK_SMALL
    padded = (counts + bm - 1) // bm * bm
    pend = jnp.cumsum(padded)
    pstart = pend - padded
    n_blocks = -(-(n * TOP_K) // bm) + ne
    eids = jnp.arange(ne, dtype=I32)
    seg = jnp.sum(jnp.where(e8[:TOP_K, :, None] == eids, pstart, 0), axis=-1)
    dest = seg + r8[:TOP_K]
    dest3 = dest.reshape(TOP_K, n // tile, tile).transpose(1, 0, 2)
    blk_row = jnp.arange(n_blocks, dtype=I32)[:, None] * bm
    blk_expert = jnp.minimum(jnp.sum((pend[None, :] <= blk_row).astype(I32), axis=1), ne - 1)
    n_used = (pend[-1:] // bm).astype(I32)
    buf = _dispatch(jnp.stack([pend, padded]).astype(I32), dest3, xn, n_blocks * bm, tp=tile,
                    s=d // LANES, bm=bm)
    yb = _experts(blk_expert, n_used, buf, w_gate, w_up, w_down, layer=layer, bm=bm)
    return _combine(dest3, g8[:TOP_K].T, x1, yb, tq=tile)


def _router_weights(wg, bg, we, be):
    d, groups = wg.shape
    epg = we.shape[2]
    assert groups <= SUBLANES and epg == SUBLANES
    pad = jnp.zeros((SUBLANES - groups, d), F32)
    wr = jnp.concatenate([wg.T, pad, we.transpose(0, 2, 1).reshape(groups * epg, d)], axis=0)
    rb = jnp.concatenate([bg, jnp.zeros((SUBLANES - groups,), F32), be.reshape(-1)])[:, None]
    hi = wr.astype(BF16)
    lo = (wr - hi.astype(F32)).astype(BF16)
    return hi, lo, rb, groups, epg


def kernel(x_prompt, x_sample, cache_attn_k, cache_attn_v, state_hgrn, rel_bias, norm_mix, norm_ffn,
           w_in_even, w_out_even, q_norm_gain, k_norm_gain, lam_q1, lam_k1, lam_q2, lam_k2, da_out_gain,
           hgrn_lb_logits, hgrn_out_gain, w_in_odd, sgu_v_gain, sgu_w, sgu_b, w_out_odd,
           router_group_w, router_group_b, router_expert_w, router_expert_b,
           expert_w_gate, expert_w_up, expert_w_down):
    bp, tp, d = x_prompt.shape
    bs, ts, _ = x_sample.shape
    depth = norm_mix.shape[0]
    _, _, past, da_heads, _, da_dh = cache_attn_k.shape
    da_dv = cache_attn_v.shape[-1]
    _, _, hg_heads, hg_dk, hg_dv = state_hgrn.shape
    width = da_heads * da_dv
    assert width == da_heads * 2 * da_dh == hg_heads * hg_dk == hg_heads * hg_dv
    assert da_dv == LANES and hg_dk == LANES and hg_dv == LANES

    lb_all = jnp.cumsum(jax.nn.softmax(hgrn_lb_logits.astype(F32), axis=0), axis=0)
    gid = jnp.arange(width) // da_dh
    pm = jnp.where(gid[:, None] == gid[None, :], 1.0 / da_dh, 0.0).astype(BF16)

    xs = {"p": x_prompt.reshape(bp * tp, d), "s": x_sample.reshape(bs * ts, d)}
    dims = {"p": (bp, tp), "s": (bs, ts)}
    outs = {"p": {}, "s": {}}
    kp_l, vp_l, ks_l, vs_l, sp_l, ss_l, sgu_l = [], [], [], [], [], [], []

    for layer in range(depth):
        j = layer // 2
        wrh, wrl, rbias, groups, epg = _router_weights(
            router_group_w[layer], router_group_b[layer], router_expert_w[layer], router_expert_b[layer])
        g_mix = norm_mix[layer][None, :]
        g_ffn = norm_ffn[layer][None, :]
        if layer % 2 == 0:
            lam_init = 0.8 - 0.6 * math.exp(-0.3 * layer)
            w_in_bf = w_in_even[j].astype(BF16)
            w_out_bf = w_out_even[j].astype(BF16)
            reps = width // da_dh
            qg = jnp.tile(q_norm_gain[j], reps)[None, :]
            kg = jnp.tile(k_norm_gain[j], reps)[None, :]
            lam4 = jnp.stack([lam_q1[j], lam_k1[j], lam_q2[j], lam_k2[j]])
            sub_gain = da_out_gain[j][None, :]
            hg_gain = hgrn_out_gain[j][None, :]
            lb = lb_all[j][None, :]
            for key in ("p", "s"):
                b, t = dims[key]
                x = xs[key]
                q, k, v, qh, kb, lf, ih, gs = _in_even(
                    x, g_mix, w_in_bf, pm, qg, kg, lb, width=width, q_scale=da_dh ** -0.5 * LOG2E, tm=min(512, b * t))
                if key == "p":
                    oa = _attn_prompt(q, k, v, rel_bias, lam4, sub_gain, batch=b, seq=t, heads=da_heads,
                                      dh=da_dh, lam_init=lam_init, qt=min(256, t))
                    s0 = jnp.zeros((b, hg_heads, hg_dk, hg_dv), F32)
                    ob, s_new = _hgrn(qh, kb, lf, ih, s0, hg_gain, batch=b, seq=t, heads=hg_heads,
                                      dk=hg_dk, dv=hg_dv, tb=min(512, t))
                    kp_l.append(k.reshape(b, t, da_heads, 2, da_dh))
                    vp_l.append(v.reshape(b, t, da_heads, da_dv))
                    sp_l.append(s_new)
                else:
                    ck = cache_attn_k[j].reshape(b, past, width)
                    cv = cache_attn_v[j].reshape(b, past, width)
                    oa = _attn_sample(q, k, v, ck, cv, rel_bias, lam4, sub_gain, batch=b, t=t,
                                      heads=da_heads, dh=da_dh, lam_init=lam_init)
                    ob, s_new = _hgrn(qh, kb, lf, ih, state_hgrn[j], hg_gain, batch=b, seq=t,
                                      heads=hg_heads, dk=hg_dk, dv=hg_dv, tb=t)
                    ks_l.append(k.reshape(b, t, da_heads, 2, da_dh))
                    vs_l.append(v.reshape(b, t, da_heads, da_dv))
                    ss_l.append(s_new)
                outs[key] = _out_even(oa, ob, gs, x, w_out_bf, g_ffn, wrh, wrl, rbias, tm=min(512, b * t))
        else:
            w_in_bf = w_in_odd[j].astype(BF16)
            w_out_bf = w_out_odd[j].astype(BF16)
            v_gain = sgu_v_gain[j][None, :]
            for key in ("p", "s"):
                b, t = dims[key]
                l = min(SGU_CHUNK, t)
                res = _odd_mixer(xs[key], g_mix, w_in_bf, v_gain, sgu_w[j][:, :l, :l], sgu_b[j][:, :l].T,
                                 w_out_bf, g_ffn, wrh, wrl, rbias, l=l, tm=min(512, b * t), emit_v=(key == "s"))
                outs[key] = res[:3]
                if key == "s":
                    sgu_l.append(res[3].reshape(b, t, -1))
        for key in ("p", "s"):
            x1, xn, lgt = outs[key]
            xs[key] = _moe(x1, xn, lgt, expert_w_gate, expert_w_up, expert_w_down, layer=layer,
                           groups=groups, epg=epg)

    return (xs["p"].reshape(bp, tp, d), xs["s"].reshape(bs, ts, d), jnp.stack(kp_l), jnp.stack(vp_l),
            jnp.stack(ks_l), jnp.stack(vs_l), jnp.stack(sp_l), jnp.stack(ss_l), jnp.stack(sgu_l))
```

```python
import functools
import math

import jax
import jax.numpy as jnp
from jax import lax
from jax.experimental import pallas as pl
from jax.experimental.pallas import tpu as pltpu

F32 = jnp.float32
BF16 = jnp.bfloat16
I32 = jnp.int32

EPS = 1e-6
LOG2E = math.log2(math.e)
CHUNK = 64
SGU_CHUNK = 128
REL_BUCKETS = 32
REL_MAX_DIST = 128
TOP_K = 2
MOE_BLOCK = 512
MOE_BLOCK_SMALL = 128
RUN_CHUNK = 16
HG_SUB = 16

LANES = 128
SUBLANES = 8
VMEM_LIMIT = 56 * 1024 * 1024

NT_DIMS = (((1,), (1,)), ((), ()))
TN_DIMS = (((0,), (0,)), ((), ()))


def _params(*sem):
    return pltpu.CompilerParams(dimension_semantics=sem, vmem_limit_bytes=VMEM_LIMIT)


def _const_spec(shape):
    nd = len(shape)
    return pl.BlockSpec(shape, lambda *_: (0,) * nd, pipeline_mode=pl.Buffered(1))


def _sigmoid(x):
    return 1.0 / (1.0 + jnp.exp(-x))


def _rms(x, g):
    return x * lax.rsqrt(jnp.mean(x * x, axis=-1, keepdims=True) + EPS) * g


def _dot(a, b):
    return jnp.dot(a, b, preferred_element_type=F32)


def rel_bucket(rel):
    half = REL_BUCKETS // 2
    max_exact = half // 2
    ret = (rel > 0).astype(I32) * half
    n = jnp.abs(rel)
    nf = jnp.maximum(n, 1).astype(F32)
    large = max_exact + (jnp.log(nf / max_exact) / math.log(REL_MAX_DIST / max_exact)
                         * (half - max_exact)).astype(I32)
    large = jnp.minimum(large, half - 1)
    return ret + jnp.where(n < max_exact, n, large)


def _in_even_kernel(x_ref, g_ref, w_ref, pm_ref, qg_ref, kg_ref, lb_ref,
                    q_ref, k_ref, v_ref, qh_ref, kb_ref, lf_ref, ih_ref, gs_ref, *, width, q_scale):
    xn = _rms(x_ref[...], g_ref[...]).astype(BF16)

    def proj(c):
        return _dot(xn, w_ref[:, c * width:(c + 1) * width])

    def group_norm(y, gain):
        ms = _dot((y * y).astype(BF16), pm_ref[...])
        return y * lax.rsqrt(ms + EPS) * gain

    q_ref[...] = group_norm(proj(0), qg_ref[...]) * q_scale
    k_ref[...] = group_norm(proj(1), kg_ref[...])
    v_ref[...] = proj(2)
    yq = proj(3)
    qh_ref[...] = yq * _sigmoid(yq)
    zf = proj(4)
    lb = lb_ref[...]
    lf_ref[...] = jnp.log(lb + (1.0 - lb) * _sigmoid(zf))
    kb_ref[...] = (1.0 - lb) * _sigmoid(-zf)
    ih_ref[...] = proj(5)
    yg = proj(6)
    gs_ref[...] = yg * _sigmoid(yg)


def _in_even(x, g_mix, w_bf, pm, qg, kg, lb, *, width, q_scale, tm):
    n, d = x.shape
    assert n % tm == 0
    row = lambda i: (i, 0)
    out = jax.ShapeDtypeStruct((n, width), F32)
    return pl.pallas_call(
        functools.partial(_in_even_kernel, width=width, q_scale=q_scale),
        grid=(n // tm,),
        in_specs=[pl.BlockSpec((tm, d), row), _const_spec((1, d)), _const_spec(w_bf.shape),
                  _const_spec(pm.shape), _const_spec((1, width)), _const_spec((1, width)),
                  _const_spec((1, width))],
        out_specs=[pl.BlockSpec((tm, width), row)] * 8,
        out_shape=[out] * 8,
        compiler_params=_params("arbitrary"),
        name="in_even",
    )(x, g_mix, w_bf, pm, qg, kg, lb)


def _bias_from_buckets(bk, rb_ref, h):
    b = jnp.zeros(bk.shape, F32)
    for u in range(REL_BUCKETS):
        b = jnp.where(bk == u, rb_ref[u, h], b)
    return jnp.where(bk < 0, -jnp.inf, b)


def _lam(lam_ref, lam_init):
    r = lam_ref[...]
    s1 = jnp.sum(r[0:1] * r[1:2], axis=1, keepdims=True)
    s2 = jnp.sum(r[2:3] * r[3:4], axis=1, keepdims=True)
    return jnp.exp(s1) - jnp.exp(s2) + lam_init


def _split_components(q, dh):
    lane = lax.broadcasted_iota(I32, q.shape, 1)
    q0 = jnp.where(lane < dh, q, 0.0)
    q1 = jnp.where(lane >= dh, q, 0.0)
    return jnp.concatenate([q0, q1], axis=0).astype(BF16)


def _attn_prompt_kernel(rb_ref, far_ref, lam_ref, bk_ref, sg_ref, q_ref, k_ref, v_ref, o_ref, *,
                        qt, dh, lam_init):
    h = pl.program_id(1)
    t = q_ref.shape[0]
    kb = k_ref[...].astype(BF16)
    vt = v_ref[...].T.astype(BF16)
    qtr = q_ref[...].T
    sub = lax.broadcasted_iota(I32, (2 * dh, qt), 0)
    bias = []
    for d in range(2):
        b = _bias_from_buckets(bk_ref[d], rb_ref, h) * LOG2E
        bias.append(jnp.concatenate([b, b], axis=1))
    far = rb_ref[far_ref[0], h] * LOG2E
    lam = _lam(lam_ref, lam_init)
    gain = sg_ref[...] * (1.0 - lam_init)
    for i in range(t // qt):
        qi = qtr[:, i * qt:(i + 1) * qt]
        qz = jnp.concatenate([jnp.where(sub < dh, qi, 0.0), jnp.where(sub >= dh, qi, 0.0)],
                             axis=1).astype(BF16)
        n = (i + 1) * qt
        parts = [(n - qt, n)]
        s = [_dot(kb[n - qt:n], qz) + bias[0]]
        shift = [0.0]
        if i >= 1:
            parts.append((n - 2 * qt, n - qt))
            s.append(_dot(kb[n - 2 * qt:n - qt], qz) + bias[1])
            shift.append(0.0)
        if i >= 2:
            parts.append((0, n - 2 * qt))
            s.append(_dot(kb[:n - 2 * qt], qz))
            shift.append(far)
        m = functools.reduce(jnp.maximum, [jnp.max(x, axis=0, keepdims=True) + c for x, c in zip(s, shift)])
        p = [jnp.exp2(x - (m - c)) for x, c in zip(s, shift)]
        l = functools.reduce(lambda a, b: a + b, [jnp.sum(x, axis=0, keepdims=True) for x in p])
        acc = functools.reduce(lambda a, b: a + b,
                               [_dot(vt[:, lo:hi], x.astype(BF16)) for (lo, hi), x in zip(parts, p)])
        o = acc / l
        out = o[:, :qt] - lam * o[:, qt:]
        out = out * lax.rsqrt(jnp.mean(out * out, axis=0, keepdims=True) + EPS) * gain
        o_ref[i * qt:(i + 1) * qt, :] = out.T


def _attn_prompt(q, k, v, rel_bias, lam4, sub_gain, *, batch, seq, heads, dh, lam_init, qt):
    n, w = q.shape
    dv = w // heads
    assert dv == 2 * dh and seq % qt == 0 and qt % CHUNK == 0
    kj = jnp.arange(qt, dtype=I32)[:, None]
    qi = jnp.arange(qt, dtype=I32)[None, :]
    bk0 = jnp.where((kj // CHUNK) <= (qi // CHUNK), rel_bucket(kj - qi), -1)
    bk1 = rel_bucket(kj - qi - qt)
    bk = jnp.stack([bk0, bk1]).astype(I32)
    assert qt + 1 >= REL_MAX_DIST
    far = rel_bucket(jnp.full((1,), -(qt + 1), I32))
    smem = pl.BlockSpec(memory_space=pltpu.SMEM)
    seq_blk = pl.BlockSpec((seq, dv), lambda b, h: (b, h))
    return pl.pallas_call(
        functools.partial(_attn_prompt_kernel, qt=qt, dh=dh, lam_init=lam_init),
        grid=(batch, heads),
        in_specs=[smem, smem, _const_spec(lam4.shape), _const_spec(bk.shape), _const_spec((dv, 1)),
                  seq_blk, seq_blk, seq_blk],
        out_specs=seq_blk,
        out_shape=jax.ShapeDtypeStruct((n, w), F32),
        compiler_params=_params("arbitrary", "arbitrary"),
        name="attn_prompt",
    )(rel_bias, far, lam4, bk, sub_gain.T, q, k, v)


def _attn_sample_kernel(rb_ref, lam_ref, bkc_ref, bkn_ref, sg_ref, q_ref, kc_ref, vc_ref, kn_ref, vn_ref,
                        o_ref, *, t, dh, lam_init):
    h = pl.program_id(1)
    qz = _split_components(q_ref[...], dh)
    bc = _bias_from_buckets(bkc_ref[...], rb_ref, h)
    bn = _bias_from_buckets(bkn_ref[...], rb_ref, h)
    sc = lax.dot_general(qz, kc_ref[0].astype(BF16), NT_DIMS, preferred_element_type=F32)
    sn = lax.dot_general(qz, kn_ref[...].astype(BF16), NT_DIMS, preferred_element_type=F32)
    sc = sc + jnp.concatenate([bc, bc], axis=0) * LOG2E
    sn = sn + jnp.concatenate([bn, bn], axis=0) * LOG2E
    m = jnp.maximum(jnp.max(sc, axis=1, keepdims=True), jnp.max(sn, axis=1, keepdims=True))
    pc = jnp.exp2(sc - m)
    pn = jnp.exp2(sn - m)
    l = jnp.sum(pc, axis=1, keepdims=True) + jnp.sum(pn, axis=1, keepdims=True)
    acc = _dot(pc.astype(BF16), vc_ref[0].astype(BF16)) + _dot(pn.astype(BF16), vn_ref[...].astype(BF16))
    o = acc / l
    out = o[:t] - _lam(lam_ref, lam_init) * o[t:]
    o_ref[...] = _rms(out, sg_ref[...]) * (1.0 - lam_init)


def _attn_sample(q, k_new, v_new, cache_k, cache_v, rel_bias, lam4, sub_gain, *, batch, t, heads, dh,
                 lam_init):
    n, w = q.shape
    dv = w // heads
    past = cache_k.shape[1]
    assert past % CHUNK == 0 and t <= CHUNK
    qpos = past + jnp.arange(t, dtype=I32)[:, None]
    bkc = rel_bucket(jnp.arange(past, dtype=I32)[None, :] - qpos).astype(I32)
    bkn = rel_bucket(past + jnp.arange(t, dtype=I32)[None, :] - qpos).astype(I32)
    smem = pl.BlockSpec(memory_space=pltpu.SMEM)
    new = pl.BlockSpec((t, dv), lambda b, h: (b, h))
    old = pl.BlockSpec((1, past, dv), lambda b, h: (b, 0, h))
    return pl.pallas_call(
        functools.partial(_attn_sample_kernel, t=t, dh=dh, lam_init=lam_init),
        grid=(batch, heads),
        in_specs=[smem, _const_spec(lam4.shape), _const_spec(bkc.shape), _const_spec(bkn.shape),
                  _const_spec((1, dv)), new, old, old, new, new],
        out_specs=new,
        out_shape=jax.ShapeDtypeStruct((n, w), F32),
        compiler_params=_params("arbitrary", "arbitrary"),
        name="attn_sample",
    )(rel_bias, lam4, bkc, bkn, sub_gain, q, cache_k, cache_v, k_new, v_new)


def _cumsum_rows(x):
    c = x.shape[0]
    row = lax.broadcasted_iota(I32, x.shape, 0)
    s = 1
    while s < c:
        x = x + jnp.where(row >= s, pltpu.roll(x, s, axis=0), 0.0)
        s *= 2
    return x


def _hgrn_kernel(qh_ref, kb_ref, lf_ref, ih_ref, s0_ref, hg_ref, ob_ref, sf_ref, st_s, *,
                 heads, dk, dv, c, nsb):
    t = pl.program_id(1)

    @pl.when(t == 0)
    def _():
        for h in range(heads):
            st_s[h] = s0_ref[0, h].T

    tb = qh_ref.shape[0]
    row = lax.broadcasted_iota(I32, (c, c), 0)
    col = lax.broadcasted_iota(I32, (c, c), 1)
    causal = col <= row

    def chunk(ci, carry):
        r0 = pl.multiple_of(ci * c, c)
        for h in range(heads):
            rows = pl.ds(r0, c)
            q = qh_ref[rows, h * dk:(h + 1) * dk]
            k = kb_ref[rows, h * dk:(h + 1) * dk]
            v = ih_ref[rows, h * dv:(h + 1) * dv]
            b = _cumsum_rows(lf_ref[rows, h * dk:(h + 1) * dk])
            bl = b[c - 1:c]
            st = st_s[h]
            inter = lax.dot_general((q * jnp.exp(b)).astype(BF16), st.astype(BF16), NT_DIMS,
                                    preferred_element_type=F32)
            qs, ks = [], []
            for j in range(nsb):
                ref = b[j * HG_SUB + HG_SUB // 2:j * HG_SUB + HG_SUB // 2 + 1]
                qs.append(q * jnp.exp(b - ref))
                sub = slice(j * HG_SUB, (j + 1) * HG_SUB)
                ks.append(k[sub] * jnp.exp(ref - b[sub]))
            a_full = lax.dot_general(jnp.concatenate(qs, axis=0).astype(BF16),
                                     jnp.concatenate(ks, axis=0).astype(BF16), NT_DIMS,
                                     preferred_element_type=F32)
            att = jnp.zeros((c, c), F32)
            for j in range(nsb):
                att = jnp.where(col >= j * HG_SUB, a_full[j * c:(j + 1) * c], att)
            att = jnp.where(causal, att, 0.0)
            out = inter + _dot(att.astype(BF16), v.astype(BF16))
            ob_ref[rows, h * dv:(h + 1) * dv] = _rms(out, hg_ref[...])
            kdec = (k * jnp.exp(bl - b)).astype(BF16)
            st_s[h] = jnp.exp(bl) * st + lax.dot_general(v.astype(BF16), kdec, TN_DIMS,
                                                         preferred_element_type=F32)
        return carry

    lax.fori_loop(0, tb // c, chunk, 0, unroll=min(4, tb // c))

    @pl.when(t == pl.num_programs(1) - 1)
    def _():
        for h in range(heads):
            sf_ref[0, h] = st_s[h].T


def _hgrn(qh, kb, lf, ih, s0, hg_gain, *, batch, seq, heads, dk, dv, tb):
    n = qh.shape[0]
    c = min(CHUNK, seq)
    assert seq % tb == 0 and tb % c == 0 and c % HG_SUB == 0
    nt = seq // tb
    blk = lambda w: pl.BlockSpec((tb, w), lambda b, t: (b * nt + t, 0))
    st = pl.BlockSpec((1, heads, dk, dv), lambda b, t: (b, 0, 0, 0))
    return pl.pallas_call(
        functools.partial(_hgrn_kernel, heads=heads, dk=dk, dv=dv, c=c, nsb=c // HG_SUB),
        grid=(batch, nt),
        in_specs=[blk(heads * dk), blk(heads * dk), blk(heads * dk), blk(heads * dv), st,
                  _const_spec((1, dv))],
        out_specs=[blk(heads * dv), st],
        out_shape=[jax.ShapeDtypeStruct((n, heads * dv), F32),
                   jax.ShapeDtypeStruct((batch, heads, dk, dv), F32)],
        scratch_shapes=[pltpu.VMEM((heads, dv, dk), F32)],
        compiler_params=_params("arbitrary", "arbitrary"),
        name="hgrn",
    )(qh, kb, lf, ih, s0, hg_gain)


def _store_token_tiles(ref, x):
    rows, d = x.shape
    s = d // LANES
    for c in range(s):
        ref[pl.ds(c, rows, stride=s), :] = x[:, c * LANES:(c + 1) * LANES]


def _load_token_tiles(ref, rows, s):
    return jnp.concatenate([ref[pl.ds(c, rows, stride=s), :] for c in range(s)], axis=1)


def _token_tile(r, s):
    return pl.ds(pl.multiple_of(r * s, s), s)


def _ffn_prologue(x1, gf_ref, wrh_ref, wrl_ref, rbias_ref, x1_ref, xn_ref, lg_ref):
    x1_ref[...] = x1
    xn = _rms(x1, gf_ref[...])
    _store_token_tiles(xn_ref, xn)
    hi = xn.astype(BF16)
    lo = (xn - hi.astype(F32)).astype(BF16)
    nt = functools.partial(lax.dot_general, dimension_numbers=NT_DIMS, preferred_element_type=F32)
    lg_ref[...] = nt(wrh_ref[...], hi) + nt(wrh_ref[...], lo) + nt(wrl_ref[...], hi) + rbias_ref[...]


def _out_even_kernel(oa_ref, ob_ref, gs_ref, x_ref, w_ref, gf_ref, wrh_ref, wrl_ref, rbias_ref,
                     x1_ref, xn_ref, lg_ref):
    o = jnp.concatenate([oa_ref[...], ob_ref[...] * gs_ref[...]], axis=1).astype(BF16)
    x1 = x_ref[...] + _dot(o, w_ref[...])
    _ffn_prologue(x1, gf_ref, wrh_ref, wrl_ref, rbias_ref, x1_ref, xn_ref, lg_ref)


def _out_even(oa, ob, gs, x, w_bf, g_ffn, wrh, wrl, rbias, *, tm):
    n, d = x.shape
    w = oa.shape[1]
    nr = wrh.shape[0]
    row = lambda i: (i, 0)
    return pl.pallas_call(
        _out_even_kernel,
        grid=(n // tm,),
        in_specs=[pl.BlockSpec((tm, w), row), pl.BlockSpec((tm, w), row), pl.BlockSpec((tm, w), row),
                  pl.BlockSpec((tm, d), row), _const_spec(w_bf.shape), _const_spec((1, d)),
                  _const_spec(wrh.shape), _const_spec(wrl.shape), _const_spec(rbias.shape)],
        out_specs=[pl.BlockSpec((tm, d), row), pl.BlockSpec((tm * (d // LANES), LANES), row),
                   pl.BlockSpec((nr, tm), lambda i: (0, i))],
        out_shape=[jax.ShapeDtypeStruct((n, d), F32), jax.ShapeDtypeStruct((n * (d // LANES), LANES), F32),
                   jax.ShapeDtypeStruct((nr, n), F32)],
        compiler_params=_params("arbitrary"),
        name="out_even",
    )(oa, ob, gs, x, w_bf, g_ffn, wrh, wrl, rbias)


def _gelu(x):
    return 0.5 * x * (1.0 + jnp.tanh(math.sqrt(2.0 / math.pi) * (x + 0.044715 * (x * x * x))))


def _odd_kernel(x_ref, gm_ref, win_ref, vg_ref, wsp_ref, bsp_ref, wout_ref, gf_ref, wrh_ref, wrl_ref,
                rbias_ref, x1_ref, xn_ref, lg_ref, *rest, half, groups, l, emit_v):
    if emit_v:
        vn_ref, u_s, s_s = rest
    else:
        vn_ref = None
        u_s, vn_s, s_s = rest
    tm = x_ref.shape[0]
    x = x_ref[...]
    xn = _rms(x, gm_ref[...]).astype(BF16)
    cw = 512
    vbuf = vn_ref if emit_v else vn_s
    for cidx in range(half // cw):
        u_s[:, cidx * cw:(cidx + 1) * cw] = _gelu(_dot(xn, win_ref[:, cidx * cw:(cidx + 1) * cw]))
        vbuf[:, cidx * cw:(cidx + 1) * cw] = _gelu(
            _dot(xn, win_ref[:, half + cidx * cw:half + (cidx + 1) * cw]))
    vbuf[...] = _rms(vbuf[...], vg_ref[...])
    gw = half // groups
    row = lax.broadcasted_iota(I32, (l, l), 0)
    col = lax.broadcasted_iota(I32, (l, l), 1)
    for g in range(groups):
        wg = jnp.where(col <= row, wsp_ref[g], 0.0).astype(BF16)
        bg = bsp_ref[:, g:g + 1]
        for ci in range(tm // l):
            vv = vbuf[ci * l:(ci + 1) * l, g * gw:(g + 1) * gw].astype(BF16)
            s_s[ci * l:(ci + 1) * l, g * gw:(g + 1) * gw] = _dot(wg, vv) + bg
    y = _dot((u_s[...] * s_s[...]).astype(BF16), wout_ref[...])
    _ffn_prologue(x + y, gf_ref, wrh_ref, wrl_ref, rbias_ref, x1_ref, xn_ref, lg_ref)


def _odd_mixer(x, g_mix, win_bf, v_gain, wsp, bsp_t, wout_bf, g_ffn, wrh, wrl, rbias, *, l, tm, emit_v):
    n, d = x.shape
    half = wout_bf.shape[0]
    groups = wsp.shape[0]
    nr = wrh.shape[0]
    assert n % tm == 0 and tm % l == 0
    row = lambda i: (i, 0)
    out_specs = [pl.BlockSpec((tm, d), row), pl.BlockSpec((tm * (d // LANES), LANES), row),
                 pl.BlockSpec((nr, tm), lambda i: (0, i))]
    out_shape = [jax.ShapeDtypeStruct((n, d), F32), jax.ShapeDtypeStruct((n * (d // LANES), LANES), F32),
                 jax.ShapeDtypeStruct((nr, n), F32)]
    scratch = [pltpu.VMEM((tm, half), F32)]
    if emit_v:
        out_specs.append(pl.BlockSpec((tm, half), row))
        out_shape.append(jax.ShapeDtypeStruct((n, half), F32))
    else:
        scratch.append(pltpu.VMEM((tm, half), F32))
    scratch.append(pltpu.VMEM((tm, half), F32))
    return pl.pallas_call(
        functools.partial(_odd_kernel, half=half, groups=groups, l=l, emit_v=emit_v),
        grid=(n // tm,),
        in_specs=[pl.BlockSpec((tm, d), row), _const_spec((1, d)), _const_spec(win_bf.shape),
                  _const_spec((1, half)), _const_spec(wsp.shape), _const_spec(bsp_t.shape),
                  _const_spec(wout_bf.shape), _const_spec((1, d)), _const_spec(wrh.shape),
                  _const_spec(wrl.shape), _const_spec(rbias.shape)],
        out_specs=out_specs,
        out_shape=out_shape,
        scratch_shapes=scratch,
        compiler_params=_params("arbitrary"),
        name="odd_mixer",
    )(x, g_mix, win_bf, v_gain, wsp, bsp_t, wout_bf, g_ffn, wrh, wrl, rbias)


def _route_kernel(lg_ref, tri_ref, e_ref, g_ref, r_ref, p_ref, cnt_ref, base_ref, tcnt_ref, run_s, *,
                  groups, epg):
    i = pl.program_id(0)

    @pl.when(i == 0)
    def _():
        run_s[...] = jnp.zeros(run_s.shape, F32)

    lg = lg_ref[...]
    tr = lg.shape[1]
    gl = [lg[g:g + 1] for g in range(groups)]
    m = functools.reduce(jnp.maximum, gl)
    grp = jnp.full((1, tr), groups - 1, I32)
    for g in range(groups - 2, -1, -1):
        grp = jnp.where(gl[g] == m, g, grp)
    gate_g = 1.0 / functools.reduce(lambda a, b: a + b, [jnp.exp(x - m) for x in gl])
    sel = lg[SUBLANES + (groups - 1) * epg:SUBLANES + groups * epg]
    for g in range(groups - 2, -1, -1):
        sel = jnp.where(grp == g, lg[SUBLANES + g * epg:SUBLANES + (g + 1) * epg], sel)
    sub = lax.broadcasted_iota(I32, sel.shape, 0)
    v1 = jnp.max(sel, axis=0, keepdims=True)
    i1 = jnp.min(jnp.where(sel == v1, sub, epg), axis=0, keepdims=True)
    sel2 = jnp.where(sub == i1, -jnp.inf, sel)
    v2 = jnp.max(sel2, axis=0, keepdims=True)
    i2 = jnp.min(jnp.where(sel2 == v2, sub, epg), axis=0, keepdims=True)
    tt = jnp.exp(v2 - v1)
    g1 = gate_g / (1.0 + tt)
    g2 = gate_g * tt / (1.0 + tt)
    e1 = grp * epg + i1
    e2 = grp * epg + i2
    ne = groups * epg
    eidx = lax.broadcasted_iota(I32, (ne, tr), 0)
    oh1 = eidx == e1
    oh2 = eidx == e2
    cnt = jnp.where(oh1, 1.0, 0.0) + jnp.where(oh2, 1.0, 0.0)
    local = _dot(cnt.astype(BF16), tri_ref[...])
    before = run_s[:, 0:1] + local
    r1 = jnp.sum(jnp.where(oh1, before, 0.0), axis=0, keepdims=True)
    r2 = jnp.sum(jnp.where(oh2, before, 0.0), axis=0, keepdims=True)
    tile_cnt = jnp.broadcast_to(jnp.sum(cnt, axis=1, keepdims=True), run_s.shape)
    padded = jnp.ceil(tile_cnt * (1.0 / RUN_CHUNK)) * RUN_CHUNK
    offset = _cumsum_rows(padded) - padded
    where_local = offset[:, 0:1] + local
    p1 = jnp.sum(jnp.where(oh1, where_local, 0.0), axis=0, keepdims=True)
    p2 = jnp.sum(jnp.where(oh2, where_local, 0.0), axis=0, keepdims=True)
    base_ref[...] = run_s[...]
    tcnt_ref[...] = tile_cnt
    run_s[...] = run_s[...] + tile_cnt
    rows = lax.broadcasted_iota(I32, (SUBLANES, tr), 0)
    e_ref[...] = jnp.where(rows == 0, e1, jnp.where(rows == 1, e2, 0))
    g_ref[...] = jnp.where(rows == 0, g1, jnp.where(rows == 1, g2, 0.0))
    r_ref[...] = jnp.where(rows == 0, r1, jnp.where(rows == 1, r2, 0.0)).astype(I32)
    p_ref[...] = jnp.where(rows == 0, p1, jnp.where(rows == 1, p2, 0.0)).astype(I32)
    cnt_ref[...] = run_s[...]


def _route(lgt, *, groups, epg, tr):
    nr, n = lgt.shape
    assert n % tr == 0 and nr == SUBLANES + groups * epg
    ne = groups * epg
    tri = (jnp.arange(tr)[:, None] < jnp.arange(tr)[None, :]).astype(BF16)
    tok = pl.BlockSpec((SUBLANES, tr), lambda i: (0, i))
    per_tile = pl.BlockSpec((ne, LANES), lambda i: (i, 0))
    tile_tab = jax.ShapeDtypeStruct((n // tr * ne, LANES), F32)
    return pl.pallas_call(
        functools.partial(_route_kernel, groups=groups, epg=epg),
        grid=(n // tr,),
        in_specs=[pl.BlockSpec((nr, tr), lambda i: (0, i)), _const_spec((tr, tr))],
        out_specs=[tok, tok, tok, tok, pl.BlockSpec((ne, LANES), lambda i: (0, 0)), per_tile, per_tile],
        out_shape=[jax.ShapeDtypeStruct((SUBLANES, n), I32), jax.ShapeDtypeStruct((SUBLANES, n), F32),
                   jax.ShapeDtypeStruct((SUBLANES, n), I32), jax.ShapeDtypeStruct((SUBLANES, n), I32),
                   jax.ShapeDtypeStruct((ne, LANES), F32), tile_tab, tile_tab],
        scratch_shapes=[pltpu.VMEM((ne, LANES), F32)],
        compiler_params=_params("arbitrary"),
        name="route",
    )(lgt, tri)


ISSUE_UNROLL = 8


def _dispatch_kernel(seg_ref, dst_ref, x_ref, buf_ref, zero_s, sem, zsem, *, s, bm):
    tp = x_ref.shape[0] // s
    blk = bm * s

    @pl.when(pl.program_id(0) == 0)
    def _():
        zero_s[...] = jnp.zeros(zero_s.shape, F32)

        def block_copy(b):
            return pltpu.make_async_copy(zero_s, buf_ref.at[pl.ds(pl.multiple_of(b * blk, blk), blk)], zsem)

        for e in range(seg_ref.shape[1]):
            @pl.when(seg_ref[1, e] > 0)
            def _():
                block_copy(seg_ref[0, e] // bm - 1).start()
        for e in range(seg_ref.shape[1]):
            @pl.when(seg_ref[1, e] > 0)
            def _():
                block_copy(0).wait()

        def tail_start(b, c):
            block_copy(b).start()
            return c

        def tail_wait(b, c):
            block_copy(b).wait()
            return c

        first_unused = seg_ref[0, seg_ref.shape[1] - 1] // bm
        n_blocks = buf_ref.shape[0] // blk
        lax.fori_loop(first_unused, n_blocks, tail_start, 0)
        lax.fori_loop(first_unused, n_blocks, tail_wait, 0)

    def issue(r0, c):
        for u in range(ISSUE_UNROLL):
            r = r0 * ISSUE_UNROLL + u
            for kk in range(TOP_K):
                pltpu.make_async_copy(x_ref.at[_token_tile(r, s)],
                                      buf_ref.at[_token_tile(dst_ref[0, kk, r], s)], sem
                                      ).start(priority=kk % 2)
        return c

    lax.fori_loop(0, tp // ISSUE_UNROLL, issue, 0)
    for kk in range(TOP_K):
        pltpu.make_async_copy(x_ref, buf_ref.at[pl.ds(0, tp * s)], sem).wait()


def _dispatch(seg, dest3, xn, n_slots, *, tp, s, bm):
    n = xn.shape[0] // s
    assert tp % ISSUE_UNROLL == 0
    grid_spec = pltpu.PrefetchScalarGridSpec(
        num_scalar_prefetch=1,
        grid=(n // tp,),
        in_specs=[pl.BlockSpec((1, TOP_K, tp), lambda i, sg: (i, 0, 0), memory_space=pltpu.SMEM),
                  pl.BlockSpec((tp * s, LANES), lambda i, sg: (i, 0))],
        out_specs=pl.BlockSpec(memory_space=pl.ANY),
        scratch_shapes=[pltpu.VMEM((bm * s, LANES), F32), pltpu.SemaphoreType.DMA(()),
                        pltpu.SemaphoreType.DMA(())],
    )
    return pl.pallas_call(
        functools.partial(_dispatch_kernel, s=s, bm=bm),
        grid_spec=grid_spec,
        out_shape=jax.ShapeDtypeStruct((n_slots * s, LANES), F32),
        compiler_params=_params("arbitrary"),
        name="dispatch",
    )(seg, dest3, xn)


def _expert_kernel(be_ref, nu_ref, x_ref, wg_ref, wu_ref, wd_ref, o_ref, wg_s, wu_s, wd_s):
    b = pl.program_id(0)
    s = wg_s.shape[0] // LANES
    prev = be_ref[jnp.maximum(b - 1, 0)]

    @pl.when((b == 0) | (be_ref[b] != prev))
    def _():
        wg_s[...] = wg_ref[0, 0].astype(BF16)
        wu_s[...] = wu_ref[0, 0].astype(BF16)
        wd_s[...] = wd_ref[0, 0].astype(BF16)

    @pl.when(b < nu_ref[0])
    def _():
        xb = _load_token_tiles(x_ref, x_ref.shape[0] // s, s).astype(BF16)
        gate = _dot(xb, wg_s[...])
        h = gate * _sigmoid(gate) * _dot(xb, wu_s[...])
        _store_token_tiles(o_ref, _dot(h.astype(BF16), wd_s[...]))

    @pl.when(b >= nu_ref[0])
    def _():
        o_ref[...] = jnp.zeros(o_ref.shape, F32)


def _experts(blk_expert, n_used, buf, w_gate, w_up, w_down, *, layer, bm):
    d, de = w_gate.shape[2:]
    s = d // LANES
    n_blocks = buf.shape[0] // (bm * s)
    rows = pl.BlockSpec((bm * s, LANES), lambda b, be, nu: (b, 0))
    used_rows = pl.BlockSpec((bm * s, LANES), lambda b, be, nu: (jnp.minimum(b, nu[0] - 1), 0))
    grid_spec = pltpu.PrefetchScalarGridSpec(
        num_scalar_prefetch=2,
        grid=(n_blocks,),
        in_specs=[used_rows,
                  pl.BlockSpec((1, 1, d, de), lambda b, be, nu: (layer, be[b], 0, 0)),
                  pl.BlockSpec((1, 1, d, de), lambda b, be, nu: (layer, be[b], 0, 0)),
                  pl.BlockSpec((1, 1, de, d), lambda b, be, nu: (layer, be[b], 0, 0))],
        out_specs=rows,
        scratch_shapes=[pltpu.VMEM((d, de), BF16), pltpu.VMEM((d, de), BF16), pltpu.VMEM((de, d), BF16)],
    )
    return pl.pallas_call(
        _expert_kernel,
        grid_spec=grid_spec,
        out_shape=jax.ShapeDtypeStruct(buf.shape, F32),
        compiler_params=_params("arbitrary"),
        name="experts",
    )(blk_expert, n_used, buf, w_gate, w_up, w_down)


def _combine_kernel(tab_ref, nxt_ref, pos_ref, gate_ref, x_ref, yb_ref, o_ref, stg_s, tt_s, sem):
    i = pl.program_id(0)
    tq, d = x_ref.shape
    s = d // LANES
    piece = RUN_CHUNK * s
    slot = i % 2

    def run_copies(ref, dst_slot, go):
        for e in range(ref.shape[2]):
            def body(j, c):
                src = pl.multiple_of((ref[0, 0, e] + j * RUN_CHUNK) * s, s)
                dst = pl.multiple_of((ref[0, 2, e] + j * RUN_CHUNK) * s, piece)
                cp = pltpu.make_async_copy(yb_ref.at[pl.ds(src, piece)],
                                           stg_s.at[dst_slot, pl.ds(dst, piece)], sem.at[dst_slot])
                cp.start() if go else cp.wait()
                return c
            lax.fori_loop(0, ref[0, 1, e], body, 0)

    @pl.when(i == 0)
    def _():
        run_copies(tab_ref, 0, True)

    @pl.when(i + 1 < pl.num_programs(0))
    def _():
        run_copies(nxt_ref, 1 - slot, True)

    run_copies(tab_ref, slot, False)

    def assemble(r0, c):
        for u in range(ISSUE_UNROLL):
            r = r0 * ISSUE_UNROLL + u
            acc = gate_ref[0, 0, r] * stg_s[slot, _token_tile(pos_ref[0, 0, r], s), :]
            for kk in range(1, TOP_K):
                acc = acc + gate_ref[0, kk, r] * stg_s[slot, _token_tile(pos_ref[0, kk, r], s), :]
            tt_s[_token_tile(r, s), :] = acc
        return c

    lax.fori_loop(0, tq // ISSUE_UNROLL, assemble, 0)
    o_ref[...] = x_ref[...] + _load_token_tiles(tt_s, tq, s)


def _combine(tab, pos3, gates3, x1, yb, *, tq):
    n, d = x1.shape
    s = d // LANES
    nt = n // tq
    ne = tab.shape[2]
    assert tq % ISSUE_UNROLL == 0
    stage_rows = tq * TOP_K + ne * RUN_CHUNK
    smem = lambda shape, imap: pl.BlockSpec(shape, imap, memory_space=pltpu.SMEM)
    return pl.pallas_call(
        _combine_kernel,
        grid=(nt,),
        in_specs=[smem((1, 3, ne), lambda i: (i, 0, 0)),
                  smem((1, 3, ne), lambda i: (jnp.minimum(i + 1, nt - 1), 0, 0)),
                  smem((1, TOP_K, tq), lambda i: (i, 0, 0)),
                  smem((1, TOP_K, tq), lambda i: (i, 0, 0)),
                  pl.BlockSpec((tq, d), lambda i: (i, 0)),
                  pl.BlockSpec(memory_space=pl.ANY)],
        out_specs=pl.BlockSpec((tq, d), lambda i: (i, 0)),
        out_shape=jax.ShapeDtypeStruct((n, d), F32),
        scratch_shapes=[pltpu.VMEM((2, stage_rows * s, LANES), F32), pltpu.VMEM((tq * s, LANES), F32),
                        pltpu.SemaphoreType.DMA((2,))],
        compiler_params=_params("arbitrary"),
        name="combine",
    )(tab, tab, pos3, gates3, x1, yb)


def _moe(x1, xn, lgt, w_gate, w_up, w_down, *, layer, groups, epg):
    n, d = x1.shape
    ne = groups * epg
    tile = min(256, n)
    rtile = min(512, n)
    e8, g8, r8, p8, cnt, base, tcnt = _route(lgt, groups=groups, epg=epg, tr=rtile)
    counts = cnt[:, 0].astype(I32)
    bm = MOE_BLOCK if n * TOP_K >= 2 * ne * MOE_BLOCK else MOE_BLOCK_SMALL
    padded = (counts + RUN_CHUNK + bm - 1) // bm * bm
    pend = jnp.cumsum(padded)
    pstart = pend - padded
    n_blocks = -(-(n * TOP_K + ne * RUN_CHUNK) // bm) + ne
    eids = jnp.arange(ne, dtype=I32)
    seg = jnp.sum(jnp.where(e8[:TOP_K, :, None] == eids, pstart, 0), axis=-1)
    dest = seg + r8[:TOP_K]
    dest3 = dest.reshape(TOP_K, n // tile, tile).transpose(1, 0, 2)
    blk_row = jnp.arange(n_blocks, dtype=I32)[:, None] * bm
    blk_expert = jnp.minimum(jnp.sum((pend[None, :] <= blk_row).astype(I32), axis=1), ne - 1)
    n_used = (pend[-1:] // bm).astype(I32)
    buf = _dispatch(jnp.stack([pend, padded]).astype(I32), dest3, xn, n_blocks * bm, tp=tile,
                    s=d // LANES, bm=bm)
    yb = _experts(blk_expert, n_used, buf, w_gate, w_up, w_down, layer=layer, bm=bm)
    nrt = n // rtile
    run_start = pstart[None, :] + base.reshape(nrt, ne, LANES)[:, :, 0].astype(I32)
    pieces = (tcnt.reshape(nrt, ne, LANES)[:, :, 0].astype(I32) + RUN_CHUNK - 1) // RUN_CHUNK
    stage = (jnp.cumsum(pieces, axis=1) - pieces) * RUN_CHUNK
    tab = jnp.stack([run_start, pieces, stage], axis=1)
    by_tile = lambda a: a[:TOP_K].reshape(TOP_K, nrt, rtile).transpose(1, 0, 2)
    return _combine(tab, by_tile(p8), by_tile(g8), x1, yb, tq=rtile)


def _router_weights(wg, bg, we, be):
    d, groups = wg.shape
    epg = we.shape[2]
    assert groups <= SUBLANES and epg == SUBLANES
    pad = jnp.zeros((SUBLANES - groups, d), F32)
    wr = jnp.concatenate([wg.T, pad, we.transpose(0, 2, 1).reshape(groups * epg, d)], axis=0)
    rb = jnp.concatenate([bg, jnp.zeros((SUBLANES - groups,), F32), be.reshape(-1)])[:, None]
    hi = wr.astype(BF16)
    lo = (wr - hi.astype(F32)).astype(BF16)
    return hi, lo, rb, groups, epg


def kernel(x_prompt, x_sample, cache_attn_k, cache_attn_v, state_hgrn, rel_bias, norm_mix, norm_ffn,
           w_in_even, w_out_even, q_norm_gain, k_norm_gain, lam_q1, lam_k1, lam_q2, lam_k2, da_out_gain,
           hgrn_lb_logits, hgrn_out_gain, w_in_odd, sgu_v_gain, sgu_w, sgu_b, w_out_odd,
           router_group_w, router_group_b, router_expert_w, router_expert_b,
           expert_w_gate, expert_w_up, expert_w_down):
    bp, tp, d = x_prompt.shape
    bs, ts, _ = x_sample.shape
    depth = norm_mix.shape[0]
    _, _, past, da_heads, _, da_dh = cache_attn_k.shape
    da_dv = cache_attn_v.shape[-1]
    _, _, hg_heads, hg_dk, hg_dv = state_hgrn.shape
    width = da_heads * da_dv
    assert width == da_heads * 2 * da_dh == hg_heads * hg_dk == hg_heads * hg_dv
    assert da_dv == LANES and hg_dk == LANES and hg_dv == LANES

    lb_all = jnp.cumsum(jax.nn.softmax(hgrn_lb_logits.astype(F32), axis=0), axis=0)
    gid = jnp.arange(width) // da_dh
    pm = jnp.where(gid[:, None] == gid[None, :], 1.0 / da_dh, 0.0).astype(BF16)

    xs = {"p": x_prompt.reshape(bp * tp, d), "s": x_sample.reshape(bs * ts, d)}
    dims = {"p": (bp, tp), "s": (bs, ts)}
    outs = {"p": {}, "s": {}}
    kp_l, vp_l, ks_l, vs_l, sp_l, ss_l, sgu_l = [], [], [], [], [], [], []

    for layer in range(depth):
        j = layer // 2
        wrh, wrl, rbias, groups, epg = _router_weights(
            router_group_w[layer], router_group_b[layer], router_expert_w[layer], router_expert_b[layer])
        g_mix = norm_mix[layer][None, :]
        g_ffn = norm_ffn[layer][None, :]
        if layer % 2 == 0:
            lam_init = 0.8 - 0.6 * math.exp(-0.3 * layer)
            w_in_bf = w_in_even[j].astype(BF16)
            w_out_bf = w_out_even[j].astype(BF16)
            reps = width // da_dh
            qg = jnp.tile(q_norm_gain[j], reps)[None, :]
            kg = jnp.tile(k_norm_gain[j], reps)[None, :]
            lam4 = jnp.stack([lam_q1[j], lam_k1[j], lam_q2[j], lam_k2[j]])
            sub_gain = da_out_gain[j][None, :]
            hg_gain = hgrn_out_gain[j][None, :]
            lb = lb_all[j][None, :]
            for key in ("p", "s"):
                b, t = dims[key]
                x = xs[key]
                q, k, v, qh, kb, lf, ih, gs = _in_even(
                    x, g_mix, w_in_bf, pm, qg, kg, lb, width=width, q_scale=da_dh ** -0.5 * LOG2E, tm=min(512, b * t))
                if key == "p":
                    oa = _attn_prompt(q, k, v, rel_bias, lam4, sub_gain, batch=b, seq=t, heads=da_heads,
                                      dh=da_dh, lam_init=lam_init, qt=min(256, t))
                    s0 = jnp.zeros((b, hg_heads, hg_dk, hg_dv), F32)
                    ob, s_new = _hgrn(qh, kb, lf, ih, s0, hg_gain, batch=b, seq=t, heads=hg_heads,
                                      dk=hg_dk, dv=hg_dv, tb=min(512, t))
                    kp_l.append(k.reshape(b, t, da_heads, 2, da_dh))
                    vp_l.append(v.reshape(b, t, da_heads, da_dv))
                    sp_l.append(s_new)
                else:
                    ck = cache_attn_k[j].reshape(b, past, width)
                    cv = cache_attn_v[j].reshape(b, past, width)
                    oa = _attn_sample(q, k, v, ck, cv, rel_bias, lam4, sub_gain, batch=b, t=t,
                                      heads=da_heads, dh=da_dh, lam_init=lam_init)
                    ob, s_new = _hgrn(qh, kb, lf, ih, state_hgrn[j], hg_gain, batch=b, seq=t,
                                      heads=hg_heads, dk=hg_dk, dv=hg_dv, tb=t)
                    ks_l.append(k.reshape(b, t, da_heads, 2, da_dh))
                    vs_l.append(v.reshape(b, t, da_heads, da_dv))
                    ss_l.append(s_new)
                outs[key] = _out_even(oa, ob, gs, x, w_out_bf, g_ffn, wrh, wrl, rbias, tm=min(512, b * t))
        else:
            w_in_bf = w_in_odd[j].astype(BF16)
            w_out_bf = w_out_odd[j].astype(BF16)
            v_gain = sgu_v_gain[j][None, :]
            for key in ("p", "s"):
                b, t = dims[key]
                l = min(SGU_CHUNK, t)
                res = _odd_mixer(xs[key], g_mix, w_in_bf, v_gain, sgu_w[j][:, :l, :l], sgu_b[j][:, :l].T,
                                 w_out_bf, g_ffn, wrh, wrl, rbias, l=l, tm=min(512, b * t), emit_v=(key == "s"))
                outs[key] = res[:3]
                if key == "s":
                    sgu_l.append(res[3].reshape(b, t, -1))
        for key in ("p", "s"):
            x1, xn, lgt = outs[key]
            xs[key] = _moe(x1, xn, lgt, expert_w_gate, expert_w_up, expert_w_down, layer=layer,
                           groups=groups, epg=epg)

    return (xs["p"].reshape(bp, tp, d), xs["s"].reshape(bs, ts, d), jnp.stack(kp_l), jnp.stack(vp_l),
            jnp.stack(ks_l), jnp.stack(vs_l), jnp.stack(sp_l), jnp.stack(ss_l), jnp.stack(sgu_l))
```

```python
import functools
import math

import jax
import jax.numpy as jnp
from jax import lax
from jax.experimental import pallas as pl
from jax.experimental.pallas import tpu as pltpu

F32 = jnp.float32
BF16 = jnp.bfloat16
I32 = jnp.int32

EPS = 1e-6
LOG2E = math.log2(math.e)
CHUNK = 64
SGU_CHUNK = 128
REL_BUCKETS = 32
REL_MAX_DIST = 128
TOP_K = 2
MOE_BLOCK = 512
MOE_BLOCK_SMALL = 128
RUN_CHUNK = 16
HG_SUB = 16

LANES = 128
SUBLANES = 8
VMEM_LIMIT = 56 * 1024 * 1024

NT_DIMS = (((1,), (1,)), ((), ()))
TN_DIMS = (((0,), (0,)), ((), ()))


def _params(*sem):
    return pltpu.CompilerParams(dimension_semantics=sem, vmem_limit_bytes=VMEM_LIMIT)


def _const_spec(shape):
    nd = len(shape)
    return pl.BlockSpec(shape, lambda *_: (0,) * nd, pipeline_mode=pl.Buffered(1))


def _sigmoid(x):
    return 1.0 / (1.0 + jnp.exp(-x))


def _rms(x, g):
    return x * lax.rsqrt(jnp.mean(x * x, axis=-1, keepdims=True) + EPS) * g


def _dot(a, b):
    return jnp.dot(a, b, preferred_element_type=F32)


def rel_bucket(rel):
    half = REL_BUCKETS // 2
    max_exact = half // 2
    ret = (rel > 0).astype(I32) * half
    n = jnp.abs(rel)
    nf = jnp.maximum(n, 1).astype(F32)
    large = max_exact + (jnp.log(nf / max_exact) / math.log(REL_MAX_DIST / max_exact)
                         * (half - max_exact)).astype(I32)
    large = jnp.minimum(large, half - 1)
    return ret + jnp.where(n < max_exact, n, large)


def _in_even_kernel(x_ref, g_ref, w_ref, pm_ref, qg_ref, kg_ref, lb_ref,
                    q_ref, k_ref, v_ref, qh_ref, kb_ref, lf_ref, ih_ref, gs_ref, *, width, q_scale):
    xn = _rms(x_ref[...], g_ref[...]).astype(BF16)

    def proj(c):
        return _dot(xn, w_ref[:, c * width:(c + 1) * width])

    def group_norm(y, gain):
        ms = _dot((y * y).astype(BF16), pm_ref[...])
        return y * lax.rsqrt(ms + EPS) * gain

    q_ref[...] = group_norm(proj(0), qg_ref[...]) * q_scale
    k_ref[...] = group_norm(proj(1), kg_ref[...])
    v_ref[...] = proj(2)
    yq = proj(3)
    qh_ref[...] = yq * _sigmoid(yq)
    zf = proj(4)
    lb = lb_ref[...]
    lf_ref[...] = jnp.log(lb + (1.0 - lb) * _sigmoid(zf))
    kb_ref[...] = (1.0 - lb) * _sigmoid(-zf)
    ih_ref[...] = proj(5)
    yg = proj(6)
    gs_ref[...] = yg * _sigmoid(yg)


def _in_even(x, g_mix, w_bf, pm, qg, kg, lb, *, width, q_scale, tm):
    n, d = x.shape
    assert n % tm == 0
    row = lambda i: (i, 0)
    out = jax.ShapeDtypeStruct((n, width), F32)
    return pl.pallas_call(
        functools.partial(_in_even_kernel, width=width, q_scale=q_scale),
        grid=(n // tm,),
        in_specs=[pl.BlockSpec((tm, d), row), _const_spec((1, d)), _const_spec(w_bf.shape),
                  _const_spec(pm.shape), _const_spec((1, width)), _const_spec((1, width)),
                  _const_spec((1, width))],
        out_specs=[pl.BlockSpec((tm, width), row)] * 8,
        out_shape=[out] * 8,
        compiler_params=_params("arbitrary"),
        name="in_even",
    )(x, g_mix, w_bf, pm, qg, kg, lb)


def _bias_from_buckets(bk, rb_ref, h):
    b = jnp.zeros(bk.shape, F32)
    for u in range(REL_BUCKETS):
        b = jnp.where(bk == u, rb_ref[u, h], b)
    return jnp.where(bk < 0, -jnp.inf, b)


def _lam(lam_ref, lam_init):
    r = lam_ref[...]
    s1 = jnp.sum(r[0:1] * r[1:2], axis=1, keepdims=True)
    s2 = jnp.sum(r[2:3] * r[3:4], axis=1, keepdims=True)
    return jnp.exp(s1) - jnp.exp(s2) + lam_init


def _split_components(q, dh):
    lane = lax.broadcasted_iota(I32, q.shape, 1)
    q0 = jnp.where(lane < dh, q, 0.0)
    q1 = jnp.where(lane >= dh, q, 0.0)
    return jnp.concatenate([q0, q1], axis=0).astype(BF16)


def _attn_prompt_kernel(rb_ref, far_ref, lam_ref, bk_ref, sg_ref, q_ref, k_ref, v_ref, o_ref, *,
                        qt, dh, lam_init):
    h = pl.program_id(1)
    t = q_ref.shape[0]
    kb = k_ref[...].astype(BF16)
    vt = v_ref[...].T.astype(BF16)
    qtr = q_ref[...].T
    sub = lax.broadcasted_iota(I32, (2 * dh, qt), 0)
    bias = []
    for d in range(2):
        b = _bias_from_buckets(bk_ref[d], rb_ref, h) * LOG2E
        bias.append(jnp.concatenate([b, b], axis=1))
    far = rb_ref[far_ref[0], h] * LOG2E
    lam = _lam(lam_ref, lam_init)
    gain = sg_ref[...] * (1.0 - lam_init)
    for i in range(t // qt):
        qi = qtr[:, i * qt:(i + 1) * qt]
        qz = jnp.concatenate([jnp.where(sub < dh, qi, 0.0), jnp.where(sub >= dh, qi, 0.0)],
                             axis=1).astype(BF16)
        n = (i + 1) * qt
        parts = [(n - qt, n)]
        s = [_dot(kb[n - qt:n], qz) + bias[0]]
        shift = [0.0]
        if i >= 1:
            parts.append((n - 2 * qt, n - qt))
            s.append(_dot(kb[n - 2 * qt:n - qt], qz) + bias[1])
            shift.append(0.0)
        if i >= 2:
            parts.append((0, n - 2 * qt))
            s.append(_dot(kb[:n - 2 * qt], qz))
            shift.append(far)
        m = functools.reduce(jnp.maximum, [jnp.max(x, axis=0, keepdims=True) + c for x, c in zip(s, shift)])
        p = [jnp.exp2(x - (m - c)) for x, c in zip(s, shift)]
        l = functools.reduce(lambda a, b: a + b, [jnp.sum(x, axis=0, keepdims=True) for x in p])
        acc = functools.reduce(lambda a, b: a + b,
                               [_dot(vt[:, lo:hi], x.astype(BF16)) for (lo, hi), x in zip(parts, p)])
        o = acc / l
        out = o[:, :qt] - lam * o[:, qt:]
        out = out * lax.rsqrt(jnp.mean(out * out, axis=0, keepdims=True) + EPS) * gain
        o_ref[i * qt:(i + 1) * qt, :] = out.T


def _attn_prompt(q, k, v, rel_bias, lam4, sub_gain, *, batch, seq, heads, dh, lam_init, qt):
    n, w = q.shape
    dv = w // heads
    assert dv == 2 * dh and seq % qt == 0 and qt % CHUNK == 0
    kj = jnp.arange(qt, dtype=I32)[:, None]
    qi = jnp.arange(qt, dtype=I32)[None, :]
    bk0 = jnp.where((kj // CHUNK) <= (qi // CHUNK), rel_bucket(kj - qi), -1)
    bk1 = rel_bucket(kj - qi - qt)
    bk = jnp.stack([bk0, bk1]).astype(I32)
    assert qt + 1 >= REL_MAX_DIST
    far = rel_bucket(jnp.full((1,), -(qt + 1), I32))
    smem = pl.BlockSpec(memory_space=pltpu.SMEM)
    seq_blk = pl.BlockSpec((seq, dv), lambda b, h: (b, h))
    return pl.pallas_call(
        functools.partial(_attn_prompt_kernel, qt=qt, dh=dh, lam_init=lam_init),
        grid=(batch, heads),
        in_specs=[smem, smem, _const_spec(lam4.shape), _const_spec(bk.shape), _const_spec((dv, 1)),
                  seq_blk, seq_blk, seq_blk],
        out_specs=seq_blk,
        out_shape=jax.ShapeDtypeStruct((n, w), F32),
        compiler_params=_params("arbitrary", "arbitrary"),
        name="attn_prompt",
    )(rel_bias, far, lam4, bk, sub_gain.T, q, k, v)


def _attn_sample_kernel(rb_ref, lam_ref, bkc_ref, bkn_ref, sg_ref, q_ref, kc_ref, vc_ref, kn_ref, vn_ref,
                        o_ref, *, t, dh, lam_init):
    h = pl.program_id(1)
    qz = _split_components(q_ref[...], dh)
    bc = _bias_from_buckets(bkc_ref[...], rb_ref, h)
    bn = _bias_from_buckets(bkn_ref[...], rb_ref, h)
    sc = lax.dot_general(qz, kc_ref[0].astype(BF16), NT_DIMS, preferred_element_type=F32)
    sn = lax.dot_general(qz, kn_ref[...].astype(BF16), NT_DIMS, preferred_element_type=F32)
    sc = sc + jnp.concatenate([bc, bc], axis=0) * LOG2E
    sn = sn + jnp.concatenate([bn, bn], axis=0) * LOG2E
    m = jnp.maximum(jnp.max(sc, axis=1, keepdims=True), jnp.max(sn, axis=1, keepdims=True))
    pc = jnp.exp2(sc - m)
    pn = jnp.exp2(sn - m)
    l = jnp.sum(pc, axis=1, keepdims=True) + jnp.sum(pn, axis=1, keepdims=True)
    acc = _dot(pc.astype(BF16), vc_ref[0].astype(BF16)) + _dot(pn.astype(BF16), vn_ref[...].astype(BF16))
    o = acc / l
    out = o[:t] - _lam(lam_ref, lam_init) * o[t:]
    o_ref[...] = _rms(out, sg_ref[...]) * (1.0 - lam_init)


def _attn_sample(q, k_new, v_new, cache_k, cache_v, rel_bias, lam4, sub_gain, *, batch, t, heads, dh,
                 lam_init):
    n, w = q.shape
    dv = w // heads
    past = cache_k.shape[1]
    assert past % CHUNK == 0 and t <= CHUNK
    qpos = past + jnp.arange(t, dtype=I32)[:, None]
    bkc = rel_bucket(jnp.arange(past, dtype=I32)[None, :] - qpos).astype(I32)
    bkn = rel_bucket(past + jnp.arange(t, dtype=I32)[None, :] - qpos).astype(I32)
    smem = pl.BlockSpec(memory_space=pltpu.SMEM)
    new = pl.BlockSpec((t, dv), lambda b, h: (b, h))
    old = pl.BlockSpec((1, past, dv), lambda b, h: (b, 0, h))
    return pl.pallas_call(
        functools.partial(_attn_sample_kernel, t=t, dh=dh, lam_init=lam_init),
        grid=(batch, heads),
        in_specs=[smem, _const_spec(lam4.shape), _const_spec(bkc.shape), _const_spec(bkn.shape),
                  _const_spec((1, dv)), new, old, old, new, new],
        out_specs=new,
        out_shape=jax.ShapeDtypeStruct((n, w), F32),
        compiler_params=_params("arbitrary", "arbitrary"),
        name="attn_sample",
    )(rel_bias, lam4, bkc, bkn, sub_gain, q, cache_k, cache_v, k_new, v_new)


def _cumsum_rows(x):
    c = x.shape[0]
    row = lax.broadcasted_iota(I32, x.shape, 0)
    s = 1
    while s < c:
        x = x + jnp.where(row >= s, pltpu.roll(x, s, axis=0), 0.0)
        s *= 2
    return x


def _hgrn_kernel(qh_ref, kb_ref, lf_ref, ih_ref, s0_ref, hg_ref, ob_ref, sf_ref, st_s, *,
                 heads, dk, dv, c, nsb):
    t = pl.program_id(1)

    @pl.when(t == 0)
    def _():
        for h in range(heads):
            st_s[h] = s0_ref[0, h].T

    tb = qh_ref.shape[0]
    row = lax.broadcasted_iota(I32, (c, c), 0)
    col = lax.broadcasted_iota(I32, (c, c), 1)
    causal = col <= row

    def chunk(ci, carry):
        r0 = pl.multiple_of(ci * c, c)
        for h in range(heads):
            rows = pl.ds(r0, c)
            q = qh_ref[rows, h * dk:(h + 1) * dk]
            k = kb_ref[rows, h * dk:(h + 1) * dk]
            v = ih_ref[rows, h * dv:(h + 1) * dv]
            b = _cumsum_rows(lf_ref[rows, h * dk:(h + 1) * dk])
            bl = b[c - 1:c]
            st = st_s[h]
            inter = lax.dot_general((q * jnp.exp(b)).astype(BF16), st.astype(BF16), NT_DIMS,
                                    preferred_element_type=F32)
            qs, ks = [], []
            for j in range(nsb):
                ref = b[j * HG_SUB + HG_SUB // 2:j * HG_SUB + HG_SUB // 2 + 1]
                qs.append(q * jnp.exp(b - ref))
                sub = slice(j * HG_SUB, (j + 1) * HG_SUB)
                ks.append(k[sub] * jnp.exp(ref - b[sub]))
            a_full = lax.dot_general(jnp.concatenate(qs, axis=0).astype(BF16),
                                     jnp.concatenate(ks, axis=0).astype(BF16), NT_DIMS,
                                     preferred_element_type=F32)
            att = jnp.zeros((c, c), F32)
            for j in range(nsb):
                att = jnp.where(col >= j * HG_SUB, a_full[j * c:(j + 1) * c], att)
            att = jnp.where(causal, att, 0.0)
            out = inter + _dot(att.astype(BF16), v.astype(BF16))
            ob_ref[rows, h * dv:(h + 1) * dv] = _rms(out, hg_ref[...])
            kdec = (k * jnp.exp(bl - b)).astype(BF16)
            st_s[h] = jnp.exp(bl) * st + lax.dot_general(v.astype(BF16), kdec, TN_DIMS,
                                                         preferred_element_type=F32)
        return carry

    lax.fori_loop(0, tb // c, chunk, 0, unroll=min(4, tb // c))

    @pl.when(t == pl.num_programs(1) - 1)
    def _():
        for h in range(heads):
            sf_ref[0, h] = st_s[h].T


def _hgrn(qh, kb, lf, ih, s0, hg_gain, *, batch, seq, heads, dk, dv, tb):
    n = qh.shape[0]
    c = min(CHUNK, seq)
    assert seq % tb == 0 and tb % c == 0 and c % HG_SUB == 0
    nt = seq // tb
    blk = lambda w: pl.BlockSpec((tb, w), lambda b, t: (b * nt + t, 0))
    st = pl.BlockSpec((1, heads, dk, dv), lambda b, t: (b, 0, 0, 0))
    return pl.pallas_call(
        functools.partial(_hgrn_kernel, heads=heads, dk=dk, dv=dv, c=c, nsb=c // HG_SUB),
        grid=(batch, nt),
        in_specs=[blk(heads * dk), blk(heads * dk), blk(heads * dk), blk(heads * dv), st,
                  _const_spec((1, dv))],
        out_specs=[blk(heads * dv), st],
        out_shape=[jax.ShapeDtypeStruct((n, heads * dv), F32),
                   jax.ShapeDtypeStruct((batch, heads, dk, dv), F32)],
        scratch_shapes=[pltpu.VMEM((heads, dv, dk), F32)],
        compiler_params=_params("arbitrary", "arbitrary"),
        name="hgrn",
    )(qh, kb, lf, ih, s0, hg_gain)


def _store_token_tiles(ref, x):
    rows, d = x.shape
    s = d // LANES
    for c in range(s):
        ref[pl.ds(c, rows, stride=s), :] = x[:, c * LANES:(c + 1) * LANES]


def _load_token_tiles(ref, rows, s):
    return jnp.concatenate([ref[pl.ds(c, rows, stride=s), :] for c in range(s)], axis=1)


def _token_tile(r, s):
    return pl.ds(pl.multiple_of(r * s, s), s)


def _ffn_prologue(x1, gf_ref, wrh_ref, wrl_ref, rbias_ref, x1_ref, xn_ref, lg_ref):
    x1_ref[...] = x1
    xn = _rms(x1, gf_ref[...])
    _store_token_tiles(xn_ref, xn)
    hi = xn.astype(BF16)
    lo = (xn - hi.astype(F32)).astype(BF16)
    nt = functools.partial(lax.dot_general, dimension_numbers=NT_DIMS, preferred_element_type=F32)
    lg_ref[...] = nt(wrh_ref[...], hi) + nt(wrh_ref[...], lo) + nt(wrl_ref[...], hi) + rbias_ref[...]


def _out_even_kernel(oa_ref, ob_ref, gs_ref, x_ref, w_ref, gf_ref, wrh_ref, wrl_ref, rbias_ref,
                     x1_ref, xn_ref, lg_ref):
    o = jnp.concatenate([oa_ref[...], ob_ref[...] * gs_ref[...]], axis=1).astype(BF16)
    x1 = x_ref[...] + _dot(o, w_ref[...])
    _ffn_prologue(x1, gf_ref, wrh_ref, wrl_ref, rbias_ref, x1_ref, xn_ref, lg_ref)


def _out_even(oa, ob, gs, x, w_bf, g_ffn, wrh, wrl, rbias, *, tm):
    n, d = x.shape
    w = oa.shape[1]
    nr = wrh.shape[0]
    row = lambda i: (i, 0)
    return pl.pallas_call(
        _out_even_kernel,
        grid=(n // tm,),
        in_specs=[pl.BlockSpec((tm, w), row), pl.BlockSpec((tm, w), row), pl.BlockSpec((tm, w), row),
                  pl.BlockSpec((tm, d), row), _const_spec(w_bf.shape), _const_spec((1, d)),
                  _const_spec(wrh.shape), _const_spec(wrl.shape), _const_spec(rbias.shape)],
        out_specs=[pl.BlockSpec((tm, d), row), pl.BlockSpec((tm * (d // LANES), LANES), row),
                   pl.BlockSpec((nr, tm), lambda i: (0, i))],
        out_shape=[jax.ShapeDtypeStruct((n, d), F32), jax.ShapeDtypeStruct((n * (d // LANES), LANES), F32),
                   jax.ShapeDtypeStruct((nr, n), F32)],
        compiler_params=_params("arbitrary"),
        name="out_even",
    )(oa, ob, gs, x, w_bf, g_ffn, wrh, wrl, rbias)


def _gelu(x):
    return 0.5 * x * (1.0 + jnp.tanh(math.sqrt(2.0 / math.pi) * (x + 0.044715 * (x * x * x))))


def _odd_kernel(x_ref, gm_ref, win_ref, vg_ref, wsp_ref, bsp_ref, wout_ref, gf_ref, wrh_ref, wrl_ref,
                rbias_ref, x1_ref, xn_ref, lg_ref, *rest, half, groups, l, emit_v):
    if emit_v:
        vn_ref, u_s, s_s = rest
    else:
        vn_ref = None
        u_s, vn_s, s_s = rest
    tm = x_ref.shape[0]
    x = x_ref[...]
    xn = _rms(x, gm_ref[...]).astype(BF16)
    cw = 512
    vbuf = vn_ref if emit_v else vn_s
    for cidx in range(half // cw):
        u_s[:, cidx * cw:(cidx + 1) * cw] = _gelu(_dot(xn, win_ref[:, cidx * cw:(cidx + 1) * cw]))
        vbuf[:, cidx * cw:(cidx + 1) * cw] = _gelu(
            _dot(xn, win_ref[:, half + cidx * cw:half + (cidx + 1) * cw]))
    vbuf[...] = _rms(vbuf[...], vg_ref[...])
    gw = half // groups
    row = lax.broadcasted_iota(I32, (l, l), 0)
    col = lax.broadcasted_iota(I32, (l, l), 1)
    for g in range(groups):
        wg = jnp.where(col <= row, wsp_ref[g], 0.0).astype(BF16)
        bg = bsp_ref[:, g:g + 1]
        for ci in range(tm // l):
            vv = vbuf[ci * l:(ci + 1) * l, g * gw:(g + 1) * gw].astype(BF16)
            s_s[ci * l:(ci + 1) * l, g * gw:(g + 1) * gw] = _dot(wg, vv) + bg
    y = _dot((u_s[...] * s_s[...]).astype(BF16), wout_ref[...])
    _ffn_prologue(x + y, gf_ref, wrh_ref, wrl_ref, rbias_ref, x1_ref, xn_ref, lg_ref)


def _odd_mixer(x, g_mix, win_bf, v_gain, wsp, bsp_t, wout_bf, g_ffn, wrh, wrl, rbias, *, l, tm, emit_v):
    n, d = x.shape
    half = wout_bf.shape[0]
    groups = wsp.shape[0]
    nr = wrh.shape[0]
    assert n % tm == 0 and tm % l == 0
    row = lambda i: (i, 0)
    out_specs = [pl.BlockSpec((tm, d), row), pl.BlockSpec((tm * (d // LANES), LANES), row),
                 pl.BlockSpec((nr, tm), lambda i: (0, i))]
    out_shape = [jax.ShapeDtypeStruct((n, d), F32), jax.ShapeDtypeStruct((n * (d // LANES), LANES), F32),
                 jax.ShapeDtypeStruct((nr, n), F32)]
    scratch = [pltpu.VMEM((tm, half), F32)]
    if emit_v:
        out_specs.append(pl.BlockSpec((tm, half), row))
        out_shape.append(jax.ShapeDtypeStruct((n, half), F32))
    else:
        scratch.append(pltpu.VMEM((tm, half), F32))
    scratch.append(pltpu.VMEM((tm, half), F32))
    return pl.pallas_call(
        functools.partial(_odd_kernel, half=half, groups=groups, l=l, emit_v=emit_v),
        grid=(n // tm,),
        in_specs=[pl.BlockSpec((tm, d), row), _const_spec((1, d)), _const_spec(win_bf.shape),
                  _const_spec((1, half)), _const_spec(wsp.shape), _const_spec(bsp_t.shape),
                  _const_spec(wout_bf.shape), _const_spec((1, d)), _const_spec(wrh.shape),
                  _const_spec(wrl.shape), _const_spec(rbias.shape)],
        out_specs=out_specs,
        out_shape=out_shape,
        scratch_shapes=scratch,
        compiler_params=_params("arbitrary"),
        name="odd_mixer",
    )(x, g_mix, win_bf, v_gain, wsp, bsp_t, wout_bf, g_ffn, wrh, wrl, rbias)


def _route_kernel(lg_ref, tri_ref, e_ref, g_ref, r_ref, p_ref, cnt_ref, base_ref, tcnt_ref, run_s, *,
                  groups, epg):
    i = pl.program_id(0)

    @pl.when(i == 0)
    def _():
        run_s[...] = jnp.zeros(run_s.shape, F32)

    lg = lg_ref[...]
    tr = lg.shape[1]
    gl = [lg[g:g + 1] for g in range(groups)]
    m = functools.reduce(jnp.maximum, gl)
    grp = jnp.full((1, tr), groups - 1, I32)
    for g in range(groups - 2, -1, -1):
        grp = jnp.where(gl[g] == m, g, grp)
    gate_g = 1.0 / functools.reduce(lambda a, b: a + b, [jnp.exp(x - m) for x in gl])
    sel = lg[SUBLANES + (groups - 1) * epg:SUBLANES + groups * epg]
    for g in range(groups - 2, -1, -1):
        sel = jnp.where(grp == g, lg[SUBLANES + g * epg:SUBLANES + (g + 1) * epg], sel)
    sub = lax.broadcasted_iota(I32, sel.shape, 0)
    v1 = jnp.max(sel, axis=0, keepdims=True)
    i1 = jnp.min(jnp.where(sel == v1, sub, epg), axis=0, keepdims=True)
    sel2 = jnp.where(sub == i1, -jnp.inf, sel)
    v2 = jnp.max(sel2, axis=0, keepdims=True)
    i2 = jnp.min(jnp.where(sel2 == v2, sub, epg), axis=0, keepdims=True)
    tt = jnp.exp(v2 - v1)
    g1 = gate_g / (1.0 + tt)
    g2 = gate_g * tt / (1.0 + tt)
    e1 = grp * epg + i1
    e2 = grp * epg + i2
    ne = groups * epg
    eidx = lax.broadcasted_iota(I32, (ne, tr), 0)
    oh1 = eidx == e1
    oh2 = eidx == e2
    cnt = jnp.where(oh1, 1.0, 0.0) + jnp.where(oh2, 1.0, 0.0)
    local = _dot(cnt.astype(BF16), tri_ref[...])
    before = run_s[:, 0:1] + local
    r1 = jnp.sum(jnp.where(oh1, before, 0.0), axis=0, keepdims=True)
    r2 = jnp.sum(jnp.where(oh2, before, 0.0), axis=0, keepdims=True)
    tile_cnt = jnp.broadcast_to(jnp.sum(cnt, axis=1, keepdims=True), run_s.shape)
    padded = jnp.ceil(tile_cnt * (1.0 / RUN_CHUNK)) * RUN_CHUNK
    offset = _cumsum_rows(padded) - padded
    where_local = offset[:, 0:1] + local
    p1 = jnp.sum(jnp.where(oh1, where_local, 0.0), axis=0, keepdims=True)
    p2 = jnp.sum(jnp.where(oh2, where_local, 0.0), axis=0, keepdims=True)
    base_ref[...] = run_s[...]
    tcnt_ref[...] = tile_cnt
    run_s[...] = run_s[...] + tile_cnt
    rows = lax.broadcasted_iota(I32, (SUBLANES, tr), 0)
    e_ref[...] = jnp.where(rows == 0, e1, jnp.where(rows == 1, e2, 0))
    g_ref[...] = jnp.where(rows == 0, g1, jnp.where(rows == 1, g2, 0.0))
    r_ref[...] = jnp.where(rows == 0, r1, jnp.where(rows == 1, r2, 0.0)).astype(I32)
    p_ref[...] = jnp.where(rows == 0, p1, jnp.where(rows == 1, p2, 0.0)).astype(I32)
    cnt_ref[...] = run_s[...]


def _route(lgt, *, groups, epg, tr):
    nr, n = lgt.shape
    assert n % tr == 0 and nr == SUBLANES + groups * epg
    ne = groups * epg
    tri = (jnp.arange(tr)[:, None] < jnp.arange(tr)[None, :]).astype(BF16)
    tok = pl.BlockSpec((SUBLANES, tr), lambda i: (0, i))
    per_tile = pl.BlockSpec((ne, LANES), lambda i: (i, 0))
    tile_tab = jax.ShapeDtypeStruct((n // tr * ne, LANES), F32)
    return pl.pallas_call(
        functools.partial(_route_kernel, groups=groups, epg=epg),
        grid=(n // tr,),
        in_specs=[pl.BlockSpec((nr, tr), lambda i: (0, i)), _const_spec((tr, tr))],
        out_specs=[tok, tok, tok, tok, pl.BlockSpec((ne, LANES), lambda i: (0, 0)), per_tile, per_tile],
        out_shape=[jax.ShapeDtypeStruct((SUBLANES, n), I32), jax.ShapeDtypeStruct((SUBLANES, n), F32),
                   jax.ShapeDtypeStruct((SUBLANES, n), I32), jax.ShapeDtypeStruct((SUBLANES, n), I32),
                   jax.ShapeDtypeStruct((ne, LANES), F32), tile_tab, tile_tab],
        scratch_shapes=[pltpu.VMEM((ne, LANES), F32)],
        compiler_params=_params("arbitrary"),
        name="route",
    )(lgt, tri)


ISSUE_UNROLL = 8


def _dispatch_kernel(seg_ref, dst_ref, x_ref, buf_ref, zero_s, sem, zsem, *, s, bm):
    tp = x_ref.shape[0] // s
    blk = bm * s

    @pl.when(pl.program_id(0) == 0)
    def _():
        zero_s[...] = jnp.zeros(zero_s.shape, F32)

        def block_copy(b):
            return pltpu.make_async_copy(zero_s, buf_ref.at[pl.ds(pl.multiple_of(b * blk, blk), blk)], zsem)

        for e in range(seg_ref.shape[1]):
            @pl.when(seg_ref[1, e] > 0)
            def _():
                block_copy(seg_ref[0, e] // bm - 1).start()

            @pl.when(seg_ref[1, e] - seg_ref[2, e] > bm)
            def _():
                block_copy(seg_ref[0, e] // bm - 2).start()
        for e in range(seg_ref.shape[1]):
            @pl.when(seg_ref[1, e] > 0)
            def _():
                block_copy(0).wait()

            @pl.when(seg_ref[1, e] - seg_ref[2, e] > bm)
            def _():
                block_copy(0).wait()

        def tail_start(b, c):
            block_copy(b).start()
            return c

        def tail_wait(b, c):
            block_copy(b).wait()
            return c

        first_unused = seg_ref[0, seg_ref.shape[1] - 1] // bm
        n_blocks = buf_ref.shape[0] // blk
        lax.fori_loop(first_unused, n_blocks, tail_start, 0)
        lax.fori_loop(first_unused, n_blocks, tail_wait, 0)

    def issue(r0, c):
        for u in range(ISSUE_UNROLL):
            r = r0 * ISSUE_UNROLL + u
            for kk in range(TOP_K):
                pltpu.make_async_copy(x_ref.at[_token_tile(r, s)],
                                      buf_ref.at[_token_tile(dst_ref[0, kk, r], s)], sem
                                      ).start(priority=kk % 2)
        return c

    lax.fori_loop(0, tp // ISSUE_UNROLL, issue, 0)
    for kk in range(TOP_K):
        pltpu.make_async_copy(x_ref, buf_ref.at[pl.ds(0, tp * s)], sem).wait()


def _dispatch(seg, dest3, xn, n_slots, *, tp, s, bm):
    n = xn.shape[0] // s
    assert tp % ISSUE_UNROLL == 0
    grid_spec = pltpu.PrefetchScalarGridSpec(
        num_scalar_prefetch=1,
        grid=(n // tp,),
        in_specs=[pl.BlockSpec((1, TOP_K, tp), lambda i, sg: (i, 0, 0), memory_space=pltpu.SMEM),
                  pl.BlockSpec((tp * s, LANES), lambda i, sg: (i, 0))],
        out_specs=pl.BlockSpec(memory_space=pl.ANY),
        scratch_shapes=[pltpu.VMEM((bm * s, LANES), F32), pltpu.SemaphoreType.DMA(()),
                        pltpu.SemaphoreType.DMA(())],
    )
    return pl.pallas_call(
        functools.partial(_dispatch_kernel, s=s, bm=bm),
        grid_spec=grid_spec,
        out_shape=jax.ShapeDtypeStruct((n_slots * s, LANES), F32),
        compiler_params=_params("arbitrary"),
        name="dispatch",
    )(seg, dest3, xn)


def _expert_kernel(be_ref, nu_ref, x_ref, wg_ref, wu_ref, wd_ref, o_ref, wg_s, wu_s, wd_s):
    b = pl.program_id(0)
    s = wg_s.shape[0] // LANES
    prev = be_ref[jnp.maximum(b - 1, 0)]

    @pl.when((b == 0) | (be_ref[b] != prev))
    def _():
        wg_s[...] = wg_ref[0, 0].astype(BF16)
        wu_s[...] = wu_ref[0, 0].astype(BF16)
        wd_s[...] = wd_ref[0, 0].astype(BF16)

    @pl.when(b < nu_ref[0])
    def _():
        xb = _load_token_tiles(x_ref, x_ref.shape[0] // s, s).astype(BF16)
        gate = _dot(xb, wg_s[...])
        h = gate * _sigmoid(gate) * _dot(xb, wu_s[...])
        _store_token_tiles(o_ref, _dot(h.astype(BF16), wd_s[...]))

    @pl.when(b >= nu_ref[0])
    def _():
        o_ref[...] = jnp.zeros(o_ref.shape, F32)


def _experts(blk_expert, n_used, buf, w_gate, w_up, w_down, *, layer, bm):
    d, de = w_gate.shape[2:]
    s = d // LANES
    n_blocks = buf.shape[0] // (bm * s)
    rows = pl.BlockSpec((bm * s, LANES), lambda b, be, nu: (b, 0))
    used_rows = pl.BlockSpec((bm * s, LANES), lambda b, be, nu: (jnp.minimum(b, nu[0] - 1), 0))
    grid_spec = pltpu.PrefetchScalarGridSpec(
        num_scalar_prefetch=2,
        grid=(n_blocks,),
        in_specs=[used_rows,
                  pl.BlockSpec((1, 1, d, de), lambda b, be, nu: (layer, be[b], 0, 0)),
                  pl.BlockSpec((1, 1, d, de), lambda b, be, nu: (layer, be[b], 0, 0)),
                  pl.BlockSpec((1, 1, de, d), lambda b, be, nu: (layer, be[b], 0, 0))],
        out_specs=rows,
        scratch_shapes=[pltpu.VMEM((d, de), BF16), pltpu.VMEM((d, de), BF16), pltpu.VMEM((de, d), BF16)],
    )
    return pl.pallas_call(
        _expert_kernel,
        grid_spec=grid_spec,
        out_shape=jax.ShapeDtypeStruct(buf.shape, F32),
        compiler_params=_params("arbitrary"),
        name="experts",
    )(blk_expert, n_used, buf, w_gate, w_up, w_down)


def _combine_kernel(tab_ref, nxt_ref, pos_ref, gate_ref, x_ref, yb_ref, o_ref, stg_s, tt_s, sem):
    i = pl.program_id(0)
    tq, d = x_ref.shape
    s = d // LANES
    piece = RUN_CHUNK * s
    slot = i % 2

    def run_copies(ref, dst_slot, go):
        for e in range(ref.shape[2]):
            def body(j, c):
                src = pl.multiple_of((ref[0, 0, e] + j * RUN_CHUNK) * s, s)
                dst = pl.multiple_of((ref[0, 2, e] + j * RUN_CHUNK) * s, piece)
                cp = pltpu.make_async_copy(yb_ref.at[pl.ds(src, piece)],
                                           stg_s.at[dst_slot, pl.ds(dst, piece)], sem.at[dst_slot])
                cp.start() if go else cp.wait()
                return c
            lax.fori_loop(0, ref[0, 1, e], body, 0)

    @pl.when(i == 0)
    def _():
        run_copies(tab_ref, 0, True)

    @pl.when(i + 1 < pl.num_programs(0))
    def _():
        run_copies(nxt_ref, 1 - slot, True)

    run_copies(tab_ref, slot, False)

    def assemble(r0, c):
        for u in range(ISSUE_UNROLL):
            r = r0 * ISSUE_UNROLL + u
            acc = gate_ref[0, 0, r] * stg_s[slot, _token_tile(pos_ref[0, 0, r], s), :]
            for kk in range(1, TOP_K):
                acc = acc + gate_ref[0, kk, r] * stg_s[slot, _token_tile(pos_ref[0, kk, r], s), :]
            tt_s[_token_tile(r, s), :] = acc
        return c

    lax.fori_loop(0, tq // ISSUE_UNROLL, assemble, 0)
    o_ref[...] = x_ref[...] + _load_token_tiles(tt_s, tq, s)


def _combine(tab, pos3, gates3, x1, yb, *, tq):
    n, d = x1.shape
    s = d // LANES
    nt = n // tq
    ne = tab.shape[2]
    assert tq % ISSUE_UNROLL == 0
    stage_rows = tq * TOP_K + ne * RUN_CHUNK
    smem = lambda shape, imap: pl.BlockSpec(shape, imap, memory_space=pltpu.SMEM)
    return pl.pallas_call(
        _combine_kernel,
        grid=(nt,),
        in_specs=[smem((1, 3, ne), lambda i: (i, 0, 0)),
                  smem((1, 3, ne), lambda i: (jnp.minimum(i + 1, nt - 1), 0, 0)),
                  smem((1, TOP_K, tq), lambda i: (i, 0, 0)),
                  smem((1, TOP_K, tq), lambda i: (i, 0, 0)),
                  pl.BlockSpec((tq, d), lambda i: (i, 0)),
                  pl.BlockSpec(memory_space=pl.ANY)],
        out_specs=pl.BlockSpec((tq, d), lambda i: (i, 0)),
        out_shape=jax.ShapeDtypeStruct((n, d), F32),
        scratch_shapes=[pltpu.VMEM((2, stage_rows * s, LANES), F32), pltpu.VMEM((tq * s, LANES), F32),
                        pltpu.SemaphoreType.DMA((2,))],
        compiler_params=_params("arbitrary"),
        name="combine",
    )(tab, tab, pos3, gates3, x1, yb)


def _moe(x1, xn, lgt, w_gate, w_up, w_down, *, layer, groups, epg):
    n, d = x1.shape
    ne = groups * epg
    rtile = min(512, n)
    tile = rtile
    e8, g8, r8, p8, cnt, base, tcnt = _route(lgt, groups=groups, epg=epg, tr=rtile)
    counts = cnt[:, 0].astype(I32)
    bm = MOE_BLOCK if n * TOP_K >= 2 * ne * MOE_BLOCK else MOE_BLOCK_SMALL
    padded = (counts + RUN_CHUNK + bm - 1) // bm * bm
    pend = jnp.cumsum(padded)
    pstart = pend - padded
    n_blocks = -(-(n * TOP_K + ne * RUN_CHUNK) // bm) + ne
    eids = jnp.arange(ne, dtype=I32)
    seg = jnp.sum(jnp.where(e8[:TOP_K, :, None] == eids, pstart, 0), axis=-1)
    dest = seg + r8[:TOP_K]
    dest3 = dest.reshape(TOP_K, n // tile, tile).transpose(1, 0, 2)
    blk_row = jnp.arange(n_blocks, dtype=I32)[:, None] * bm
    blk_expert = jnp.minimum(jnp.sum((pend[None, :] <= blk_row).astype(I32), axis=1), ne - 1)
    n_used = (pend[-1:] // bm).astype(I32)
    buf = _dispatch(jnp.stack([pend, padded, counts]).astype(I32), dest3, xn, n_blocks * bm, tp=tile,
                    s=d // LANES, bm=bm)
    yb = _experts(blk_expert, n_used, buf, w_gate, w_up, w_down, layer=layer, bm=bm)
    nrt = n // rtile
    run_start = pstart[None, :] + base.reshape(nrt, ne, LANES)[:, :, 0].astype(I32)
    pieces = (tcnt.reshape(nrt, ne, LANES)[:, :, 0].astype(I32) + RUN_CHUNK - 1) // RUN_CHUNK
    stage = (jnp.cumsum(pieces, axis=1) - pieces) * RUN_CHUNK
    tab = jnp.stack([run_start, pieces, stage], axis=1)
    by_tile = lambda a: a[:TOP_K].reshape(TOP_K, nrt, rtile).transpose(1, 0, 2)
    return _combine(tab, by_tile(p8), by_tile(g8), x1, yb, tq=rtile)


def _router_weights(wg, bg, we, be):
    d, groups = wg.shape
    epg = we.shape[2]
    assert groups <= SUBLANES and epg == SUBLANES
    pad = jnp.zeros((SUBLANES - groups, d), F32)
    wr = jnp.concatenate([wg.T, pad, we.transpose(0, 2, 1).reshape(groups * epg, d)], axis=0)
    rb = jnp.concatenate([bg, jnp.zeros((SUBLANES - groups,), F32), be.reshape(-1)])[:, None]
    hi = wr.astype(BF16)
    lo = (wr - hi.astype(F32)).astype(BF16)
    return hi, lo, rb, groups, epg


def kernel(x_prompt, x_sample, cache_attn_k, cache_attn_v, state_hgrn, rel_bias, norm_mix, norm_ffn,
           w_in_even, w_out_even, q_norm_gain, k_norm_gain, lam_q1, lam_k1, lam_q2, lam_k2, da_out_gain,
           hgrn_lb_logits, hgrn_out_gain, w_in_odd, sgu_v_gain, sgu_w, sgu_b, w_out_odd,
           router_group_w, router_group_b, router_expert_w, router_expert_b,
           expert_w_gate, expert_w_up, expert_w_down):
    bp, tp, d = x_prompt.shape
    bs, ts, _ = x_sample.shape
    depth = norm_mix.shape[0]
    _, _, past, da_heads, _, da_dh = cache_attn_k.shape
    da_dv = cache_attn_v.shape[-1]
    _, _, hg_heads, hg_dk, hg_dv = state_hgrn.shape
    width = da_heads * da_dv
    assert width == da_heads * 2 * da_dh == hg_heads * hg_dk == hg_heads * hg_dv
    assert da_dv == LANES and hg_dk == LANES and hg_dv == LANES

    lb_all = jnp.cumsum(jax.nn.softmax(hgrn_lb_logits.astype(F32), axis=0), axis=0)
    gid = jnp.arange(width) // da_dh
    pm = jnp.where(gid[:, None] == gid[None, :], 1.0 / da_dh, 0.0).astype(BF16)

    xs = {"p": x_prompt.reshape(bp * tp, d), "s": x_sample.reshape(bs * ts, d)}
    dims = {"p": (bp, tp), "s": (bs, ts)}
    outs = {"p": {}, "s": {}}
    kp_l, vp_l, ks_l, vs_l, sp_l, ss_l, sgu_l = [], [], [], [], [], [], []

    for layer in range(depth):
        j = layer // 2
        wrh, wrl, rbias, groups, epg = _router_weights(
            router_group_w[layer], router_group_b[layer], router_expert_w[layer], router_expert_b[layer])
        g_mix = norm_mix[layer][None, :]
        g_ffn = norm_ffn[layer][None, :]
        if layer % 2 == 0:
            lam_init = 0.8 - 0.6 * math.exp(-0.3 * layer)
            w_in_bf = w_in_even[j].astype(BF16)
            w_out_bf = w_out_even[j].astype(BF16)
            reps = width // da_dh
            qg = jnp.tile(q_norm_gain[j], reps)[None, :]
            kg = jnp.tile(k_norm_gain[j], reps)[None, :]
            lam4 = jnp.stack([lam_q1[j], lam_k1[j], lam_q2[j], lam_k2[j]])
            sub_gain = da_out_gain[j][None, :]
            hg_gain = hgrn_out_gain[j][None, :]
            lb = lb_all[j][None, :]
            for key in ("p", "s"):
                b, t = dims[key]
                x = xs[key]
                q, k, v, qh, kb, lf, ih, gs = _in_even(
                    x, g_mix, w_in_bf, pm, qg, kg, lb, width=width, q_scale=da_dh ** -0.5 * LOG2E, tm=min(512, b * t))
                if key == "p":
                    oa = _attn_prompt(q, k, v, rel_bias, lam4, sub_gain, batch=b, seq=t, heads=da_heads,
                                      dh=da_dh, lam_init=lam_init, qt=min(256, t))
                    s0 = jnp.zeros((b, hg_heads, hg_dk, hg_dv), F32)
                    ob, s_new = _hgrn(qh, kb, lf, ih, s0, hg_gain, batch=b, seq=t, heads=hg_heads,
                                      dk=hg_dk, dv=hg_dv, tb=min(512, t))
                    kp_l.append(k.reshape(b, t, da_heads, 2, da_dh))
                    vp_l.append(v.reshape(b, t, da_heads, da_dv))
                    sp_l.append(s_new)
                else:
                    ck = cache_attn_k[j].reshape(b, past, width)
                    cv = cache_attn_v[j].reshape(b, past, width)
                    oa = _attn_sample(q, k, v, ck, cv, rel_bias, lam4, sub_gain, batch=b, t=t,
                                      heads=da_heads, dh=da_dh, lam_init=lam_init)
                    ob, s_new = _hgrn(qh, kb, lf, ih, state_hgrn[j], hg_gain, batch=b, seq=t,
                                      heads=hg_heads, dk=hg_dk, dv=hg_dv, tb=t)
                    ks_l.append(k.reshape(b, t, da_heads, 2, da_dh))
                    vs_l.append(v.reshape(b, t, da_heads, da_dv))
                    ss_l.append(s_new)
                outs[key] = _out_even(oa, ob, gs, x, w_out_bf, g_ffn, wrh, wrl, rbias, tm=min(512, b * t))
        else:
            w_in_bf = w_in_odd[j].astype(BF16)
            w_out_bf = w_out_odd[j].astype(BF16)
            v_gain = sgu_v_gain[j][None, :]
            for key in ("p", "s"):
                b, t = dims[key]
                l = min(SGU_CHUNK, t)
                res = _odd_mixer(xs[key], g_mix, w_in_bf, v_gain, sgu_w[j][:, :l, :l], sgu_b[j][:, :l].T,
                                 w_out_bf, g_ffn, wrh, wrl, rbias, l=l, tm=min(512, b * t), emit_v=(key == "s"))
                outs[key] = res[:3]
                if key == "s":
                    sgu_l.append(res[3].reshape(b, t, -1))
        for key in ("p", "s"):
            x1, xn, lgt = outs[key]
            xs[key] = _moe(x1, xn, lgt, expert_w_gate, expert_w_up, expert_w_down, layer=layer,
                           groups=groups, epg=epg)

    return (xs["p"].reshape(bp, tp, d), xs["s"].reshape(bs, ts, d), jnp.stack(kp_l), jnp.stack(vp_l),
            jnp.stack(ks_l), jnp.stack(vs_l), jnp.stack(sp_l), jnp.stack(ss_l), jnp.stack(sgu_l))
```

```python
import functools
import math

import jax
import jax.numpy as jnp
from jax import lax
from jax.experimental import pallas as pl
from jax.experimental.pallas import tpu as pltpu

F32 = jnp.float32
BF16 = jnp.bfloat16
I32 = jnp.int32

EPS = 1e-6
LOG2E = math.log2(math.e)
CHUNK = 64
SGU_CHUNK = 128
REL_BUCKETS = 32
REL_MAX_DIST = 128
TOP_K = 2
MOE_BLOCK = 512
MOE_BLOCK_SMALL = 128
RUN_CHUNK = 16
HG_SUB = 16

LANES = 128
SUBLANES = 8
VMEM_LIMIT = 56 * 1024 * 1024

NT_DIMS = (((1,), (1,)), ((), ()))
TN_DIMS = (((0,), (0,)), ((), ()))


def _params(*sem):
    return pltpu.CompilerParams(dimension_semantics=sem, vmem_limit_bytes=VMEM_LIMIT)


def _const_spec(shape):
    nd = len(shape)
    return pl.BlockSpec(shape, lambda *_: (0,) * nd, pipeline_mode=pl.Buffered(1))


def _sigmoid(x):
    return 1.0 / (1.0 + jnp.exp(-x))


def _rms(x, g):
    return x * lax.rsqrt(jnp.mean(x * x, axis=-1, keepdims=True) + EPS) * g


def _dot(a, b):
    return jnp.dot(a, b, preferred_element_type=F32)


def rel_bucket(rel):
    half = REL_BUCKETS // 2
    max_exact = half // 2
    ret = (rel > 0).astype(I32) * half
    n = jnp.abs(rel)
    nf = jnp.maximum(n, 1).astype(F32)
    large = max_exact + (jnp.log(nf / max_exact) / math.log(REL_MAX_DIST / max_exact)
                         * (half - max_exact)).astype(I32)
    large = jnp.minimum(large, half - 1)
    return ret + jnp.where(n < max_exact, n, large)


def _in_even_kernel(x_ref, g_ref, w_ref, pm_ref, qg_ref, kg_ref, lb_ref,
                    q_ref, k_ref, v_ref, qh_ref, kb_ref, lf_ref, ih_ref, gs_ref, *, width, q_scale):
    xn = _rms(x_ref[...], g_ref[...]).astype(BF16)

    def proj(c):
        return _dot(xn, w_ref[:, c * width:(c + 1) * width])

    def group_norm(y, gain):
        ms = _dot((y * y).astype(BF16), pm_ref[...])
        return y * lax.rsqrt(ms + EPS) * gain

    q_ref[...] = group_norm(proj(0), qg_ref[...]) * q_scale
    k_ref[...] = group_norm(proj(1), kg_ref[...])
    v_ref[...] = proj(2)
    yq = proj(3)
    qh_ref[...] = yq * _sigmoid(yq)
    zf = proj(4)
    lb = lb_ref[...]
    lf_ref[...] = jnp.log(lb + (1.0 - lb) * _sigmoid(zf))
    kb_ref[...] = (1.0 - lb) * _sigmoid(-zf)
    ih_ref[...] = proj(5)
    yg = proj(6)
    gs_ref[...] = yg * _sigmoid(yg)


def _in_even(x, g_mix, w_bf, pm, qg, kg, lb, *, width, q_scale, tm):
    n, d = x.shape
    assert n % tm == 0
    row = lambda i: (i, 0)
    out = jax.ShapeDtypeStruct((n, width), F32)
    return pl.pallas_call(
        functools.partial(_in_even_kernel, width=width, q_scale=q_scale),
        grid=(n // tm,),
        in_specs=[pl.BlockSpec((tm, d), row), _const_spec((1, d)), _const_spec(w_bf.shape),
                  _const_spec(pm.shape), _const_spec((1, width)), _const_spec((1, width)),
                  _const_spec((1, width))],
        out_specs=[pl.BlockSpec((tm, width), row)] * 8,
        out_shape=[out] * 8,
        compiler_params=_params("arbitrary"),
        name="in_even",
    )(x, g_mix, w_bf, pm, qg, kg, lb)


def _bias_from_buckets(bk, rb_ref, h):
    b = jnp.zeros(bk.shape, F32)
    for u in range(REL_BUCKETS):
        b = jnp.where(bk == u, rb_ref[u, h], b)
    return jnp.where(bk < 0, -jnp.inf, b)


def _lam(lam_ref, lam_init):
    r = lam_ref[...]
    s1 = jnp.sum(r[0:1] * r[1:2], axis=1, keepdims=True)
    s2 = jnp.sum(r[2:3] * r[3:4], axis=1, keepdims=True)
    return jnp.exp(s1) - jnp.exp(s2) + lam_init


def _split_components(q, dh):
    lane = lax.broadcasted_iota(I32, q.shape, 1)
    q0 = jnp.where(lane < dh, q, 0.0)
    q1 = jnp.where(lane >= dh, q, 0.0)
    return jnp.concatenate([q0, q1], axis=0).astype(BF16)


def _attn_prompt_kernel(rb_ref, far_ref, lam_ref, bk_ref, sg_ref, q_ref, k_ref, v_ref, o_ref, *,
                        qt, dh, lam_init):
    h = pl.program_id(1)
    t = q_ref.shape[0]
    kb = k_ref[...].astype(BF16)
    vt = v_ref[...].T.astype(BF16)
    qtr = q_ref[...].T
    sub = lax.broadcasted_iota(I32, (2 * dh, qt), 0)
    bias = []
    for d in range(2):
        b = _bias_from_buckets(bk_ref[d], rb_ref, h) * LOG2E
        bias.append(jnp.concatenate([b, b], axis=1))
    far = rb_ref[far_ref[0], h] * LOG2E
    lam = _lam(lam_ref, lam_init)
    gain = sg_ref[...] * (1.0 - lam_init)
    for i in range(t // qt):
        qi = qtr[:, i * qt:(i + 1) * qt]
        qz = jnp.concatenate([jnp.where(sub < dh, qi, 0.0), jnp.where(sub >= dh, qi, 0.0)],
                             axis=1).astype(BF16)
        n = (i + 1) * qt
        parts = [(n - qt, n)]
        s = [_dot(kb[n - qt:n], qz) + bias[0]]
        shift = [0.0]
        if i >= 1:
            parts.append((n - 2 * qt, n - qt))
            s.append(_dot(kb[n - 2 * qt:n - qt], qz) + bias[1])
            shift.append(0.0)
        if i >= 2:
            parts.append((0, n - 2 * qt))
            s.append(_dot(kb[:n - 2 * qt], qz))
            shift.append(far)
        m = functools.reduce(jnp.maximum, [jnp.max(x, axis=0, keepdims=True) + c for x, c in zip(s, shift)])
        p = [jnp.exp2(x - (m - c)) for x, c in zip(s, shift)]
        l = functools.reduce(lambda a, b: a + b, [jnp.sum(x, axis=0, keepdims=True) for x in p])
        acc = functools.reduce(lambda a, b: a + b,
                               [_dot(vt[:, lo:hi], x.astype(BF16)) for (lo, hi), x in zip(parts, p)])
        o = acc / l
        out = o[:, :qt] - lam * o[:, qt:]
        out = out * lax.rsqrt(jnp.mean(out * out, axis=0, keepdims=True) + EPS) * gain
        o_ref[i * qt:(i + 1) * qt, :] = out.T


def _attn_prompt(q, k, v, rel_bias, lam4, sub_gain, *, batch, seq, heads, dh, lam_init, qt):
    n, w = q.shape
    dv = w // heads
    assert dv == 2 * dh and seq % qt == 0 and qt % CHUNK == 0
    kj = jnp.arange(qt, dtype=I32)[:, None]
    qi = jnp.arange(qt, dtype=I32)[None, :]
    bk0 = jnp.where((kj // CHUNK) <= (qi // CHUNK), rel_bucket(kj - qi), -1)
    bk1 = rel_bucket(kj - qi - qt)
    bk = jnp.stack([bk0, bk1]).astype(I32)
    assert qt + 1 >= REL_MAX_DIST
    far = rel_bucket(jnp.full((1,), -(qt + 1), I32))
    smem = pl.BlockSpec(memory_space=pltpu.SMEM)
    seq_blk = pl.BlockSpec((seq, dv), lambda b, h: (b, h))
    return pl.pallas_call(
        functools.partial(_attn_prompt_kernel, qt=qt, dh=dh, lam_init=lam_init),
        grid=(batch, heads),
        in_specs=[smem, smem, _const_spec(lam4.shape), _const_spec(bk.shape), _const_spec((dv, 1)),
                  seq_blk, seq_blk, seq_blk],
        out_specs=seq_blk,
        out_shape=jax.ShapeDtypeStruct((n, w), F32),
        compiler_params=_params("arbitrary", "arbitrary"),
        name="attn_prompt",
    )(rel_bias, far, lam4, bk, sub_gain.T, q, k, v)


def _attn_sample_kernel(rb_ref, lam_ref, bkc_ref, bkn_ref, sg_ref, q_ref, kc_ref, vc_ref, kn_ref, vn_ref,
                        o_ref, *, t, dh, lam_init):
    h = pl.program_id(1)
    qz = _split_components(q_ref[...], dh)
    bc = _bias_from_buckets(bkc_ref[...], rb_ref, h)
    bn = _bias_from_buckets(bkn_ref[...], rb_ref, h)
    sc = lax.dot_general(qz, kc_ref[0].astype(BF16), NT_DIMS, preferred_element_type=F32)
    sn = lax.dot_general(qz, kn_ref[...].astype(BF16), NT_DIMS, preferred_element_type=F32)
    sc = sc + jnp.concatenate([bc, bc], axis=0) * LOG2E
    sn = sn + jnp.concatenate([bn, bn], axis=0) * LOG2E
    m = jnp.maximum(jnp.max(sc, axis=1, keepdims=True), jnp.max(sn, axis=1, keepdims=True))
    pc = jnp.exp2(sc - m)
    pn = jnp.exp2(sn - m)
    l = jnp.sum(pc, axis=1, keepdims=True) + jnp.sum(pn, axis=1, keepdims=True)
    acc = _dot(pc.astype(BF16), vc_ref[0].astype(BF16)) + _dot(pn.astype(BF16), vn_ref[...].astype(BF16))
    o = acc / l
    out = o[:t] - _lam(lam_ref, lam_init) * o[t:]
    o_ref[...] = _rms(out, sg_ref[...]) * (1.0 - lam_init)


def _attn_sample(q, k_new, v_new, cache_k, cache_v, rel_bias, lam4, sub_gain, *, batch, t, heads, dh,
                 lam_init):
    n, w = q.shape
    dv = w // heads
    past = cache_k.shape[1]
    assert past % CHUNK == 0 and t <= CHUNK
    qpos = past + jnp.arange(t, dtype=I32)[:, None]
    bkc = rel_bucket(jnp.arange(past, dtype=I32)[None, :] - qpos).astype(I32)
    bkn = rel_bucket(past + jnp.arange(t, dtype=I32)[None, :] - qpos).astype(I32)
    smem = pl.BlockSpec(memory_space=pltpu.SMEM)
    new = pl.BlockSpec((t, dv), lambda b, h: (b, h))
    old = pl.BlockSpec((1, past, dv), lambda b, h: (b, 0, h))
    return pl.pallas_call(
        functools.partial(_attn_sample_kernel, t=t, dh=dh, lam_init=lam_init),
        grid=(batch, heads),
        in_specs=[smem, _const_spec(lam4.shape), _const_spec(bkc.shape), _const_spec(bkn.shape),
                  _const_spec((1, dv)), new, old, old, new, new],
        out_specs=new,
        out_shape=jax.ShapeDtypeStruct((n, w), F32),
        compiler_params=_params("arbitrary", "arbitrary"),
        name="attn_sample",
    )(rel_bias, lam4, bkc, bkn, sub_gain, q, cache_k, cache_v, k_new, v_new)


def _cumsum_rows(x):
    c = x.shape[0]
    row = lax.broadcasted_iota(I32, x.shape, 0)
    s = 1
    while s < c:
        x = x + jnp.where(row >= s, pltpu.roll(x, s, axis=0), 0.0)
        s *= 2
    return x


def _hgrn_kernel(qh_ref, kb_ref, lf_ref, ih_ref, s0_ref, hg_ref, ob_ref, sf_ref, st_s, *,
                 heads, dk, dv, c, nsb):
    t = pl.program_id(1)

    @pl.when(t == 0)
    def _():
        for h in range(heads):
            st_s[h] = s0_ref[0, h].T

    tb = qh_ref.shape[0]
    row = lax.broadcasted_iota(I32, (c, c), 0)
    col = lax.broadcasted_iota(I32, (c, c), 1)
    causal = col <= row

    def chunk(ci, carry):
        r0 = pl.multiple_of(ci * c, c)
        for h in range(heads):
            rows = pl.ds(r0, c)
            q = qh_ref[rows, h * dk:(h + 1) * dk]
            k = kb_ref[rows, h * dk:(h + 1) * dk]
            v = ih_ref[rows, h * dv:(h + 1) * dv]
            b = _cumsum_rows(lf_ref[rows, h * dk:(h + 1) * dk])
            bl = b[c - 1:c]
            st = st_s[h]
            inter = lax.dot_general((q * jnp.exp(b)).astype(BF16), st.astype(BF16), NT_DIMS,
                                    preferred_element_type=F32)
            qs, ks = [], []
            for j in range(nsb):
                ref = b[j * HG_SUB + HG_SUB // 2:j * HG_SUB + HG_SUB // 2 + 1]
                qs.append(q * jnp.exp(b - ref))
                sub = slice(j * HG_SUB, (j + 1) * HG_SUB)
                ks.append(k[sub] * jnp.exp(ref - b[sub]))
            a_full = lax.dot_general(jnp.concatenate(qs, axis=0).astype(BF16),
                                     jnp.concatenate(ks, axis=0).astype(BF16), NT_DIMS,
                                     preferred_element_type=F32)
            att = jnp.zeros((c, c), F32)
            for j in range(nsb):
                att = jnp.where(col >= j * HG_SUB, a_full[j * c:(j + 1) * c], att)
            att = jnp.where(causal, att, 0.0)
            out = inter + _dot(att.astype(BF16), v.astype(BF16))
            ob_ref[rows, h * dv:(h + 1) * dv] = _rms(out, hg_ref[...])
            kdec = (k * jnp.exp(bl - b)).astype(BF16)
            st_s[h] = jnp.exp(bl) * st + lax.dot_general(v.astype(BF16), kdec, TN_DIMS,
                                                         preferred_element_type=F32)
        return carry

    lax.fori_loop(0, tb // c, chunk, 0, unroll=min(4, tb // c))

    @pl.when(t == pl.num_programs(1) - 1)
    def _():
        for h in range(heads):
            sf_ref[0, h] = st_s[h].T


def _hgrn(qh, kb, lf, ih, s0, hg_gain, *, batch, seq, heads, dk, dv, tb):
    n = qh.shape[0]
    c = min(CHUNK, seq)
    assert seq % tb == 0 and tb % c == 0 and c % HG_SUB == 0
    nt = seq // tb
    blk = lambda w: pl.BlockSpec((tb, w), lambda b, t: (b * nt + t, 0))
    st = pl.BlockSpec((1, heads, dk, dv), lambda b, t: (b, 0, 0, 0))
    return pl.pallas_call(
        functools.partial(_hgrn_kernel, heads=heads, dk=dk, dv=dv, c=c, nsb=c // HG_SUB),
        grid=(batch, nt),
        in_specs=[blk(heads * dk), blk(heads * dk), blk(heads * dk), blk(heads * dv), st,
                  _const_spec((1, dv))],
        out_specs=[blk(heads * dv), st],
        out_shape=[jax.ShapeDtypeStruct((n, heads * dv), F32),
                   jax.ShapeDtypeStruct((batch, heads, dk, dv), F32)],
        scratch_shapes=[pltpu.VMEM((heads, dv, dk), F32)],
        compiler_params=_params("arbitrary", "arbitrary"),
        name="hgrn",
    )(qh, kb, lf, ih, s0, hg_gain)


def _store_token_tiles(ref, x):
    rows, d = x.shape
    s = d // LANES
    for c in range(s):
        ref[pl.ds(c, rows, stride=s), :] = x[:, c * LANES:(c + 1) * LANES]


def _load_token_tiles(ref, rows, s):
    return jnp.concatenate([ref[pl.ds(c, rows, stride=s), :] for c in range(s)], axis=1)


def _token_tile(r, s):
    return pl.ds(pl.multiple_of(r * s, s), s)


def _ffn_prologue(x1, gf_ref, wrh_ref, wrl_ref, rbias_ref, x1_ref, xn_ref, lg_ref):
    x1_ref[...] = x1
    xn = _rms(x1, gf_ref[...])
    _store_token_tiles(xn_ref, xn)
    hi = xn.astype(BF16)
    lo = (xn - hi.astype(F32)).astype(BF16)
    nt = functools.partial(lax.dot_general, dimension_numbers=NT_DIMS, preferred_element_type=F32)
    lg_ref[...] = nt(wrh_ref[...], hi) + nt(wrh_ref[...], lo) + nt(wrl_ref[...], hi) + rbias_ref[...]


def _out_even_kernel(oa_ref, ob_ref, gs_ref, x_ref, w_ref, gf_ref, wrh_ref, wrl_ref, rbias_ref,
                     x1_ref, xn_ref, lg_ref):
    o = jnp.concatenate([oa_ref[...], ob_ref[...] * gs_ref[...]], axis=1).astype(BF16)
    x1 = x_ref[...] + _dot(o, w_ref[...])
    _ffn_prologue(x1, gf_ref, wrh_ref, wrl_ref, rbias_ref, x1_ref, xn_ref, lg_ref)


def _out_even(oa, ob, gs, x, w_bf, g_ffn, wrh, wrl, rbias, *, tm):
    n, d = x.shape
    w = oa.shape[1]
    nr = wrh.shape[0]
    row = lambda i: (i, 0)
    return pl.pallas_call(
        _out_even_kernel,
        grid=(n // tm,),
        in_specs=[pl.BlockSpec((tm, w), row), pl.BlockSpec((tm, w), row), pl.BlockSpec((tm, w), row),
                  pl.BlockSpec((tm, d), row), _const_spec(w_bf.shape), _const_spec((1, d)),
                  _const_spec(wrh.shape), _const_spec(wrl.shape), _const_spec(rbias.shape)],
        out_specs=[pl.BlockSpec((tm, d), row), pl.BlockSpec((tm * (d // LANES), LANES), row),
                   pl.BlockSpec((nr, tm), lambda i: (0, i))],
        out_shape=[jax.ShapeDtypeStruct((n, d), F32), jax.ShapeDtypeStruct((n * (d // LANES), LANES), F32),
                   jax.ShapeDtypeStruct((nr, n), F32)],
        compiler_params=_params("arbitrary"),
        name="out_even",
    )(oa, ob, gs, x, w_bf, g_ffn, wrh, wrl, rbias)


def _gelu(x):
    return 0.5 * x * (1.0 + jnp.tanh(math.sqrt(2.0 / math.pi) * (x + 0.044715 * (x * x * x))))


def _odd_kernel(x_ref, gm_ref, win_ref, vg_ref, wsp_ref, bsp_ref, wout_ref, gf_ref, wrh_ref, wrl_ref,
                rbias_ref, x1_ref, xn_ref, lg_ref, *rest, half, groups, l, emit_v):
    if emit_v:
        vn_ref, u_s, s_s = rest
    else:
        vn_ref = None
        u_s, vn_s, s_s = rest
    tm = x_ref.shape[0]
    x = x_ref[...]
    xn = _rms(x, gm_ref[...]).astype(BF16)
    cw = 512
    vbuf = vn_ref if emit_v else vn_s
    for cidx in range(half // cw):
        u_s[:, cidx * cw:(cidx + 1) * cw] = _gelu(_dot(xn, win_ref[:, cidx * cw:(cidx + 1) * cw]))
        vbuf[:, cidx * cw:(cidx + 1) * cw] = _gelu(
            _dot(xn, win_ref[:, half + cidx * cw:half + (cidx + 1) * cw]))
    vbuf[...] = _rms(vbuf[...], vg_ref[...])
    gw = half // groups
    row = lax.broadcasted_iota(I32, (l, l), 0)
    col = lax.broadcasted_iota(I32, (l, l), 1)
    for g in range(groups):
        wg = jnp.where(col <= row, wsp_ref[g], 0.0).astype(BF16)
        bg = bsp_ref[:, g:g + 1]
        for ci in range(tm // l):
            vv = vbuf[ci * l:(ci + 1) * l, g * gw:(g + 1) * gw].astype(BF16)
            s_s[ci * l:(ci + 1) * l, g * gw:(g + 1) * gw] = _dot(wg, vv) + bg
    y = _dot((u_s[...] * s_s[...]).astype(BF16), wout_ref[...])
    _ffn_prologue(x + y, gf_ref, wrh_ref, wrl_ref, rbias_ref, x1_ref, xn_ref, lg_ref)


def _odd_mixer(x, g_mix, win_bf, v_gain, wsp, bsp_t, wout_bf, g_ffn, wrh, wrl, rbias, *, l, tm, emit_v):
    n, d = x.shape
    half = wout_bf.shape[0]
    groups = wsp.shape[0]
    nr = wrh.shape[0]
    assert n % tm == 0 and tm % l == 0
    row = lambda i: (i, 0)
    out_specs = [pl.BlockSpec((tm, d), row), pl.BlockSpec((tm * (d // LANES), LANES), row),
                 pl.BlockSpec((nr, tm), lambda i: (0, i))]
    out_shape = [jax.ShapeDtypeStruct((n, d), F32), jax.ShapeDtypeStruct((n * (d // LANES), LANES), F32),
                 jax.ShapeDtypeStruct((nr, n), F32)]
    scratch = [pltpu.VMEM((tm, half), F32)]
    if emit_v:
        out_specs.append(pl.BlockSpec((tm, half), row))
        out_shape.append(jax.ShapeDtypeStruct((n, half), F32))
    else:
        scratch.append(pltpu.VMEM((tm, half), F32))
    scratch.append(pltpu.VMEM((tm, half), F32))
    return pl.pallas_call(
        functools.partial(_odd_kernel, half=half, groups=groups, l=l, emit_v=emit_v),
        grid=(n // tm,),
        in_specs=[pl.BlockSpec((tm, d), row), _const_spec((1, d)), _const_spec(win_bf.shape),
                  _const_spec((1, half)), _const_spec(wsp.shape), _const_spec(bsp_t.shape),
                  _const_spec(wout_bf.shape), _const_spec((1, d)), _const_spec(wrh.shape),
                  _const_spec(wrl.shape), _const_spec(rbias.shape)],
        out_specs=out_specs,
        out_shape=out_shape,
        scratch_shapes=scratch,
        compiler_params=_params("arbitrary"),
        name="odd_mixer",
    )(x, g_mix, win_bf, v_gain, wsp, bsp_t, wout_bf, g_ffn, wrh, wrl, rbias)


def _route_kernel(lg_ref, tri_ref, e_ref, g_ref, r_ref, p_ref, cnt_ref, base_ref, tcnt_ref, run_s, *,
                  groups, epg):
    i = pl.program_id(0)

    @pl.when(i == 0)
    def _():
        run_s[...] = jnp.zeros(run_s.shape, F32)

    lg = lg_ref[...]
    tr = lg.shape[1]
    gl = [lg[g:g + 1] for g in range(groups)]
    m = functools.reduce(jnp.maximum, gl)
    grp = jnp.full((1, tr), groups - 1, I32)
    for g in range(groups - 2, -1, -1):
        grp = jnp.where(gl[g] == m, g, grp)
    gate_g = 1.0 / functools.reduce(lambda a, b: a + b, [jnp.exp(x - m) for x in gl])
    sel = lg[SUBLANES + (groups - 1) * epg:SUBLANES + groups * epg]
    for g in range(groups - 2, -1, -1):
        sel = jnp.where(grp == g, lg[SUBLANES + g * epg:SUBLANES + (g + 1) * epg], sel)
    sub = lax.broadcasted_iota(I32, sel.shape, 0)
    v1 = jnp.max(sel, axis=0, keepdims=True)
    i1 = jnp.min(jnp.where(sel == v1, sub, epg), axis=0, keepdims=True)
    sel2 = jnp.where(sub == i1, -jnp.inf, sel)
    v2 = jnp.max(sel2, axis=0, keepdims=True)
    i2 = jnp.min(jnp.where(sel2 == v2, sub, epg), axis=0, keepdims=True)
    tt = jnp.exp(v2 - v1)
    g1 = gate_g / (1.0 + tt)
    g2 = gate_g * tt / (1.0 + tt)
    e1 = grp * epg + i1
    e2 = grp * epg + i2
    ne = groups * epg
    eidx = lax.broadcasted_iota(I32, (ne, tr), 0)
    oh1 = eidx == e1
    oh2 = eidx == e2
    cnt = jnp.where(oh1, 1.0, 0.0) + jnp.where(oh2, 1.0, 0.0)
    local = _dot(cnt.astype(BF16), tri_ref[...])
    before = run_s[:, 0:1] + local
    r1 = jnp.sum(jnp.where(oh1, before, 0.0), axis=0, keepdims=True)
    r2 = jnp.sum(jnp.where(oh2, before, 0.0), axis=0, keepdims=True)
    tile_cnt = jnp.broadcast_to(jnp.sum(cnt, axis=1, keepdims=True), run_s.shape)
    padded = jnp.ceil(tile_cnt * (1.0 / RUN_CHUNK)) * RUN_CHUNK
    offset = _cumsum_rows(padded) - padded
    where_local = offset[:, 0:1] + local
    p1 = jnp.sum(jnp.where(oh1, where_local, 0.0), axis=0, keepdims=True)
    p2 = jnp.sum(jnp.where(oh2, where_local, 0.0), axis=0, keepdims=True)
    base_ref[...] = run_s[...]
    tcnt_ref[...] = tile_cnt
    run_s[...] = run_s[...] + tile_cnt
    rows = lax.broadcasted_iota(I32, (SUBLANES, tr), 0)
    e_ref[...] = jnp.where(rows == 0, e1, jnp.where(rows == 1, e2, 0))
    g_ref[...] = jnp.where(rows == 0, g1, jnp.where(rows == 1, g2, 0.0))
    r_ref[...] = jnp.where(rows == 0, r1, jnp.where(rows == 1, r2, 0.0)).astype(I32)
    p_ref[...] = jnp.where(rows == 0, p1, jnp.where(rows == 1, p2, 0.0)).astype(I32)
    cnt_ref[...] = run_s[...]


def _route(lgt, *, groups, epg, tr):
    nr, n = lgt.shape
    assert n % tr == 0 and nr == SUBLANES + groups * epg
    ne = groups * epg
    tri = (jnp.arange(tr)[:, None] < jnp.arange(tr)[None, :]).astype(BF16)
    tok = pl.BlockSpec((SUBLANES, tr), lambda i: (0, i))
    per_tile = pl.BlockSpec((ne, LANES), lambda i: (i, 0))
    tile_tab = jax.ShapeDtypeStruct((n // tr * ne, LANES), F32)
    return pl.pallas_call(
        functools.partial(_route_kernel, groups=groups, epg=epg),
        grid=(n // tr,),
        in_specs=[pl.BlockSpec((nr, tr), lambda i: (0, i)), _const_spec((tr, tr))],
        out_specs=[tok, tok, tok, tok, pl.BlockSpec((ne, LANES), lambda i: (0, 0)), per_tile, per_tile],
        out_shape=[jax.ShapeDtypeStruct((SUBLANES, n), I32), jax.ShapeDtypeStruct((SUBLANES, n), F32),
                   jax.ShapeDtypeStruct((SUBLANES, n), I32), jax.ShapeDtypeStruct((SUBLANES, n), I32),
                   jax.ShapeDtypeStruct((ne, LANES), F32), tile_tab, tile_tab],
        scratch_shapes=[pltpu.VMEM((ne, LANES), F32)],
        compiler_params=_params("arbitrary"),
        name="route",
    )(lgt, tri)


ISSUE_UNROLL = 8


def _dispatch_kernel(seg_ref, tab_ref, tm1_ref, tm2_ref, pos_ref, x_ref, buf_ref, zero_s, stg_s, sem, zsem,
                     *, s, bm):
    i = pl.program_id(0)
    tp = x_ref.shape[0] // s
    blk = bm * s
    slot = i % 2

    @pl.when(pl.program_id(0) == 0)
    def _():
        zero_s[...] = jnp.zeros(zero_s.shape, F32)

        def block_copy(b):
            return pltpu.make_async_copy(zero_s, buf_ref.at[pl.ds(pl.multiple_of(b * blk, blk), blk)], zsem)

        for e in range(seg_ref.shape[1]):
            @pl.when(seg_ref[1, e] > 0)
            def _():
                block_copy(seg_ref[0, e] // bm - 1).start()

            @pl.when(seg_ref[1, e] - seg_ref[2, e] > bm)
            def _():
                block_copy(seg_ref[0, e] // bm - 2).start()
        for e in range(seg_ref.shape[1]):
            @pl.when(seg_ref[1, e] > 0)
            def _():
                block_copy(0).wait()

            @pl.when(seg_ref[1, e] - seg_ref[2, e] > bm)
            def _():
                block_copy(0).wait()

        def tail_start(b, c):
            block_copy(b).start()
            return c

        def tail_wait(b, c):
            block_copy(b).wait()
            return c

        first_unused = seg_ref[0, seg_ref.shape[1] - 1] // bm
        n_blocks = buf_ref.shape[0] // blk
        lax.fori_loop(first_unused, n_blocks, tail_start, 0)
        lax.fori_loop(first_unused, n_blocks, tail_wait, 0)

    def run_copies(ref, from_slot, go):
        for e in range(ref.shape[2]):
            def piece(off, rows, e=e):
                src = pl.multiple_of((ref[0, 2, e] + off) * s, s)
                dst = pl.multiple_of((ref[0, 0, e] + off) * s, s)
                cp = pltpu.make_async_copy(stg_s.at[from_slot, pl.ds(src, rows * s)],
                                           buf_ref.at[pl.ds(dst, rows * s)], sem.at[from_slot])
                cp.start() if go else cp.wait()

            def body(j, c, piece=piece):
                piece(j * RUN_CHUNK, RUN_CHUNK)
                return c

            cnt = ref[0, 3, e]
            full = cnt // RUN_CHUNK
            lax.fori_loop(0, full, body, 0)
            rem = cnt - full * RUN_CHUNK
            p = RUN_CHUNK // 2
            while p >= 1:
                @pl.when((rem & p) != 0)
                def _(p=p, piece=piece, rem=rem, full=full):
                    piece(full * RUN_CHUNK + (rem & (-2 * p)), p)
                p //= 2

    @pl.when(i >= 2)
    def _():
        run_copies(tm2_ref, slot, False)

    def place(r0, c):
        for u in range(ISSUE_UNROLL):
            r = r0 * ISSUE_UNROLL + u
            row = x_ref[_token_tile(r, s), :]
            for kk in range(TOP_K):
                stg_s[slot, _token_tile(pos_ref[0, kk, r], s), :] = row
        return c

    lax.fori_loop(0, tp // ISSUE_UNROLL, place, 0)
    run_copies(tab_ref, slot, True)

    @pl.when(i == pl.num_programs(0) - 1)
    def _():
        @pl.when(i >= 1)
        def _():
            run_copies(tm1_ref, 1 - slot, False)
        run_copies(tab_ref, slot, False)


def _dispatch(seg, tab, pos3, xn, n_slots, *, tp, s, bm):
    n = xn.shape[0] // s
    ne = tab.shape[2]
    assert tp % ISSUE_UNROLL == 0
    stage_rows = tp * TOP_K + ne * RUN_CHUNK
    smem = lambda shape, imap: pl.BlockSpec(shape, imap, memory_space=pltpu.SMEM)
    grid_spec = pltpu.PrefetchScalarGridSpec(
        num_scalar_prefetch=1,
        grid=(n // tp,),
        in_specs=[smem((1, 4, ne), lambda i, sg: (i, 0, 0)),
                  smem((1, 4, ne), lambda i, sg: (jnp.maximum(i - 1, 0), 0, 0)),
                  smem((1, 4, ne), lambda i, sg: (jnp.maximum(i - 2, 0), 0, 0)),
                  smem((1, TOP_K, tp), lambda i, sg: (i, 0, 0)),
                  pl.BlockSpec((tp * s, LANES), lambda i, sg: (i, 0))],
        out_specs=pl.BlockSpec(memory_space=pl.ANY),
        scratch_shapes=[pltpu.VMEM((bm * s, LANES), F32), pltpu.VMEM((2, stage_rows * s, LANES), F32),
                        pltpu.SemaphoreType.DMA((2,)), pltpu.SemaphoreType.DMA(())],
    )
    return pl.pallas_call(
        functools.partial(_dispatch_kernel, s=s, bm=bm),
        grid_spec=grid_spec,
        out_shape=jax.ShapeDtypeStruct((n_slots * s, LANES), F32),
        compiler_params=_params("arbitrary"),
        name="dispatch",
    )(seg, tab, tab, tab, pos3, xn)


def _expert_kernel(be_ref, nu_ref, x_ref, wg_ref, wu_ref, wd_ref, o_ref, wg_s, wu_s, wd_s):
    b = pl.program_id(0)
    s = wg_s.shape[0] // LANES
    prev = be_ref[jnp.maximum(b - 1, 0)]

    @pl.when((b == 0) | (be_ref[b] != prev))
    def _():
        wg_s[...] = wg_ref[0, 0].astype(BF16)
        wu_s[...] = wu_ref[0, 0].astype(BF16)
        wd_s[...] = wd_ref[0, 0].astype(BF16)

    @pl.when(b < nu_ref[0])
    def _():
        xb = _load_token_tiles(x_ref, x_ref.shape[0] // s, s).astype(BF16)
        gate = _dot(xb, wg_s[...])
        h = gate * _sigmoid(gate) * _dot(xb, wu_s[...])
        _store_token_tiles(o_ref, _dot(h.astype(BF16), wd_s[...]))

    @pl.when(b >= nu_ref[0])
    def _():
        o_ref[...] = jnp.zeros(o_ref.shape, F32)


def _experts(blk_expert, n_used, buf, w_gate, w_up, w_down, *, layer, bm):
    d, de = w_gate.shape[2:]
    s = d // LANES
    n_blocks = buf.shape[0] // (bm * s)
    rows = pl.BlockSpec((bm * s, LANES), lambda b, be, nu: (b, 0))
    used_rows = pl.BlockSpec((bm * s, LANES), lambda b, be, nu: (jnp.minimum(b, nu[0] - 1), 0))
    grid_spec = pltpu.PrefetchScalarGridSpec(
        num_scalar_prefetch=2,
        grid=(n_blocks,),
        in_specs=[used_rows,
                  pl.BlockSpec((1, 1, d, de), lambda b, be, nu: (layer, be[b], 0, 0)),
                  pl.BlockSpec((1, 1, d, de), lambda b, be, nu: (layer, be[b], 0, 0)),
                  pl.BlockSpec((1, 1, de, d), lambda b, be, nu: (layer, be[b], 0, 0))],
        out_specs=rows,
        scratch_shapes=[pltpu.VMEM((d, de), BF16), pltpu.VMEM((d, de), BF16), pltpu.VMEM((de, d), BF16)],
    )
    return pl.pallas_call(
        _expert_kernel,
        grid_spec=grid_spec,
        out_shape=jax.ShapeDtypeStruct(buf.shape, F32),
        compiler_params=_params("arbitrary"),
        name="experts",
    )(blk_expert, n_used, buf, w_gate, w_up, w_down)


def _combine_kernel(tab_ref, nxt_ref, pos_ref, gate_ref, x_ref, yb_ref, o_ref, stg_s, tt_s, sem):
    i = pl.program_id(0)
    tq, d = x_ref.shape
    s = d // LANES
    piece = RUN_CHUNK * s
    slot = i % 2

    def run_copies(ref, dst_slot, go):
        for e in range(ref.shape[2]):
            def body(j, c):
                src = pl.multiple_of((ref[0, 0, e] + j * RUN_CHUNK) * s, s)
                dst = pl.multiple_of((ref[0, 2, e] + j * RUN_CHUNK) * s, piece)
                cp = pltpu.make_async_copy(yb_ref.at[pl.ds(src, piece)],
                                           stg_s.at[dst_slot, pl.ds(dst, piece)], sem.at[dst_slot])
                cp.start() if go else cp.wait()
                return c
            lax.fori_loop(0, ref[0, 1, e], body, 0)

    @pl.when(i == 0)
    def _():
        run_copies(tab_ref, 0, True)

    @pl.when(i + 1 < pl.num_programs(0))
    def _():
        run_copies(nxt_ref, 1 - slot, True)

    run_copies(tab_ref, slot, False)

    def assemble(r0, c):
        for u in range(ISSUE_UNROLL):
            r = r0 * ISSUE_UNROLL + u
            acc = gate_ref[0, 0, r] * stg_s[slot, _token_tile(pos_ref[0, 0, r], s), :]
            for kk in range(1, TOP_K):
                acc = acc + gate_ref[0, kk, r] * stg_s[slot, _token_tile(pos_ref[0, kk, r], s), :]
            tt_s[_token_tile(r, s), :] = acc
        return c

    lax.fori_loop(0, tq // ISSUE_UNROLL, assemble, 0)
    o_ref[...] = x_ref[...] + _load_token_tiles(tt_s, tq, s)


def _combine(tab, pos3, gates3, x1, yb, *, tq):
    n, d = x1.shape
    s = d // LANES
    nt = n // tq
    ne = tab.shape[2]
    assert tq % ISSUE_UNROLL == 0
    stage_rows = tq * TOP_K + ne * RUN_CHUNK
    smem = lambda shape, imap: pl.BlockSpec(shape, imap, memory_space=pltpu.SMEM)
    return pl.pallas_call(
        _combine_kernel,
        grid=(nt,),
        in_specs=[smem((1, 4, ne), lambda i: (i, 0, 0)),
                  smem((1, 4, ne), lambda i: (jnp.minimum(i + 1, nt - 1), 0, 0)),
                  smem((1, TOP_K, tq), lambda i: (i, 0, 0)),
                  smem((1, TOP_K, tq), lambda i: (i, 0, 0)),
                  pl.BlockSpec((tq, d), lambda i: (i, 0)),
                  pl.BlockSpec(memory_space=pl.ANY)],
        out_specs=pl.BlockSpec((tq, d), lambda i: (i, 0)),
        out_shape=jax.ShapeDtypeStruct((n, d), F32),
        scratch_shapes=[pltpu.VMEM((2, stage_rows * s, LANES), F32), pltpu.VMEM((tq * s, LANES), F32),
                        pltpu.SemaphoreType.DMA((2,))],
        compiler_params=_params("arbitrary"),
        name="combine",
    )(tab, tab, pos3, gates3, x1, yb)


def _moe(x1, xn, lgt, w_gate, w_up, w_down, *, layer, groups, epg):
    n, d = x1.shape
    ne = groups * epg
    rtile = min(512, n)
    e8, g8, r8, p8, cnt, base, tcnt = _route(lgt, groups=groups, epg=epg, tr=rtile)
    del e8, r8
    counts = cnt[:, 0].astype(I32)
    bm = MOE_BLOCK if n * TOP_K >= 2 * ne * MOE_BLOCK else MOE_BLOCK_SMALL
    padded = (counts + RUN_CHUNK + bm - 1) // bm * bm
    pend = jnp.cumsum(padded)
    pstart = pend - padded
    n_blocks = -(-(n * TOP_K + ne * RUN_CHUNK) // bm) + ne
    blk_row = jnp.arange(n_blocks, dtype=I32)[:, None] * bm
    blk_expert = jnp.minimum(jnp.sum((pend[None, :] <= blk_row).astype(I32), axis=1), ne - 1)
    n_used = (pend[-1:] // bm).astype(I32)
    nrt = n // rtile
    run_rows = tcnt.reshape(nrt, ne, LANES)[:, :, 0].astype(I32)
    run_start = pstart[None, :] + base.reshape(nrt, ne, LANES)[:, :, 0].astype(I32)
    pieces = (run_rows + RUN_CHUNK - 1) // RUN_CHUNK
    stage = (jnp.cumsum(pieces, axis=1) - pieces) * RUN_CHUNK
    tab = jnp.stack([run_start, pieces, stage, run_rows], axis=1)
    by_tile = lambda a: a[:TOP_K].reshape(TOP_K, nrt, rtile).transpose(1, 0, 2)
    pos3 = by_tile(p8)
    buf = _dispatch(jnp.stack([pend, padded, counts]).astype(I32), tab, pos3, xn, n_blocks * bm, tp=rtile,
                    s=d // LANES, bm=bm)
    yb = _experts(blk_expert, n_used, buf, w_gate, w_up, w_down, layer=layer, bm=bm)
    return _combine(tab, pos3, by_tile(g8), x1, yb, tq=rtile)


def _router_weights(wg, bg, we, be):
    d, groups = wg.shape
    epg = we.shape[2]
    assert groups <= SUBLANES and epg == SUBLANES
    pad = jnp.zeros((SUBLANES - groups, d), F32)
    wr = jnp.concatenate([wg.T, pad, we.transpose(0, 2, 1).reshape(groups * epg, d)], axis=0)
    rb = jnp.concatenate([bg, jnp.zeros((SUBLANES - groups,), F32), be.reshape(-1)])[:, None]
    hi = wr.astype(BF16)
    lo = (wr - hi.astype(F32)).astype(BF16)
    return hi, lo, rb, groups, epg


def kernel(x_prompt, x_sample, cache_attn_k, cache_attn_v, state_hgrn, rel_bias, norm_mix, norm_ffn,
           w_in_even, w_out_even, q_norm_gain, k_norm_gain, lam_q1, lam_k1, lam_q2, lam_k2, da_out_gain,
           hgrn_lb_logits, hgrn_out_gain, w_in_odd, sgu_v_gain, sgu_w, sgu_b, w_out_odd,
           router_group_w, router_group_b, router_expert_w, router_expert_b,
           expert_w_gate, expert_w_up, expert_w_down):
    bp, tp, d = x_prompt.shape
    bs, ts, _ = x_sample.shape
    depth = norm_mix.shape[0]
    _, _, past, da_heads, _, da_dh = cache_attn_k.shape
    da_dv = cache_attn_v.shape[-1]
    _, _, hg_heads, hg_dk, hg_dv = state_hgrn.shape
    width = da_heads * da_dv
    assert width == da_heads * 2 * da_dh == hg_heads * hg_dk == hg_heads * hg_dv
    assert da_dv == LANES and hg_dk == LANES and hg_dv == LANES

    lb_all = jnp.cumsum(jax.nn.softmax(hgrn_lb_logits.astype(F32), axis=0), axis=0)
    gid = jnp.arange(width) // da_dh
    pm = jnp.where(gid[:, None] == gid[None, :], 1.0 / da_dh, 0.0).astype(BF16)

    xs = {"p": x_prompt.reshape(bp * tp, d), "s": x_sample.reshape(bs * ts, d)}
    dims = {"p": (bp, tp), "s": (bs, ts)}
    outs = {"p": {}, "s": {}}
    kp_l, vp_l, ks_l, vs_l, sp_l, ss_l, sgu_l = [], [], [], [], [], [], []

    for layer in range(depth):
        j = layer // 2
        wrh, wrl, rbias, groups, epg = _router_weights(
            router_group_w[layer], router_group_b[layer], router_expert_w[layer], router_expert_b[layer])
        g_mix = norm_mix[layer][None, :]
        g_ffn = norm_ffn[layer][None, :]
        if layer % 2 == 0:
            lam_init = 0.8 - 0.6 * math.exp(-0.3 * layer)
            w_in_bf = w_in_even[j].astype(BF16)
            w_out_bf = w_out_even[j].astype(BF16)
            reps = width // da_dh
            qg = jnp.tile(q_norm_gain[j], reps)[None, :]
            kg = jnp.tile(k_norm_gain[j], reps)[None, :]
            lam4 = jnp.stack([lam_q1[j], lam_k1[j], lam_q2[j], lam_k2[j]])
            sub_gain = da_out_gain[j][None, :]
            hg_gain = hgrn_out_gain[j][None, :]
            lb = lb_all[j][None, :]
            for key in ("p", "s"):
                b, t = dims[key]
                x = xs[key]
                q, k, v, qh, kb, lf, ih, gs = _in_even(
                    x, g_mix, w_in_bf, pm, qg, kg, lb, width=width, q_scale=da_dh ** -0.5 * LOG2E, tm=min(512, b * t))
                if key == "p":
                    oa = _attn_prompt(q, k, v, rel_bias, lam4, sub_gain, batch=b, seq=t, heads=da_heads,
                                      dh=da_dh, lam_init=lam_init, qt=min(256, t))
                    s0 = jnp.zeros((b, hg_heads, hg_dk, hg_dv), F32)
                    ob, s_new = _hgrn(qh, kb, lf, ih, s0, hg_gain, batch=b, seq=t, heads=hg_heads,
                                      dk=hg_dk, dv=hg_dv, tb=min(512, t))
                    kp_l.append(k.reshape(b, t, da_heads, 2, da_dh))
                    vp_l.append(v.reshape(b, t, da_heads, da_dv))
                    sp_l.append(s_new)
                else:
                    ck = cache_attn_k[j].reshape(b, past, width)
                    cv = cache_attn_v[j].reshape(b, past, width)
                    oa = _attn_sample(q, k, v, ck, cv, rel_bias, lam4, sub_gain, batch=b, t=t,
                                      heads=da_heads, dh=da_dh, lam_init=lam_init)
                    ob, s_new = _hgrn(qh, kb, lf, ih, state_hgrn[j], hg_gain, batch=b, seq=t,
                                      heads=hg_heads, dk=hg_dk, dv=hg_dv, tb=t)
                    ks_l.append(k.reshape(b, t, da_heads, 2, da_dh))
                    vs_l.append(v.reshape(b, t, da_heads, da_dv))
                    ss_l.append(s_new)
                outs[key] = _out_even(oa, ob, gs, x, w_out_bf, g_ffn, wrh, wrl, rbias, tm=min(512, b * t))
        else:
            w_in_bf = w_in_odd[j].astype(BF16)
            w_out_bf = w_out_odd[j].astype(BF16)
            v_gain = sgu_v_gain[j][None, :]
            for key in ("p", "s"):
                b, t = dims[key]
                l = min(SGU_CHUNK, t)
                res = _odd_mixer(xs[key], g_mix, w_in_bf, v_gain, sgu_w[j][:, :l, :l], sgu_b[j][:, :l].T,
                                 w_out_bf, g_ffn, wrh, wrl, rbias, l=l, tm=min(512, b * t), emit_v=(key == "s"))
                outs[key] = res[:3]
                if key == "s":
                    sgu_l.append(res[3].reshape(b, t, -1))
        for key in ("p", "s"):
            x1, xn, lgt = outs[key]
            xs[key] = _moe(x1, xn, lgt, expert_w_gate, expert_w_up, expert_w_down, layer=layer,
                           groups=groups, epg=epg)

    return (xs["p"].reshape(bp, tp, d), xs["s"].reshape(bs, ts, d), jnp.stack(kp_l), jnp.stack(vp_l),
            jnp.stack(ks_l), jnp.stack(vs_l), jnp.stack(sp_l), jnp.stack(ss_l), jnp.stack(sgu_l))
```

```python
import functools
import math

import jax
import jax.numpy as jnp
from jax import lax
from jax.experimental import pallas as pl
from jax.experimental.pallas import tpu as pltpu

F32 = jnp.float32
BF16 = jnp.bfloat16
I32 = jnp.int32

EPS = 1e-6
LOG2E = math.log2(math.e)
CHUNK = 64
SGU_CHUNK = 128
REL_BUCKETS = 32
REL_MAX_DIST = 128
TOP_K = 2
MOE_BLOCK = 512
MOE_BLOCK_SMALL = 128
RUN_CHUNK = 16
HG_SUB = 16

LANES = 128
SUBLANES = 8
VMEM_LIMIT = 56 * 1024 * 1024

NT_DIMS = (((1,), (1,)), ((), ()))
TN_DIMS = (((0,), (0,)), ((), ()))


def _params(*sem):
    return pltpu.CompilerParams(dimension_semantics=sem, vmem_limit_bytes=VMEM_LIMIT)


def _const_spec(shape):
    nd = len(shape)
    return pl.BlockSpec(shape, lambda *_: (0,) * nd, pipeline_mode=pl.Buffered(1))


def _sigmoid(x):
    return 1.0 / (1.0 + jnp.exp(-x))


def _rms(x, g):
    return x * lax.rsqrt(jnp.mean(x * x, axis=-1, keepdims=True) + EPS) * g


def _dot(a, b):
    return jnp.dot(a, b, preferred_element_type=F32)


def rel_bucket(rel):
    half = REL_BUCKETS // 2
    max_exact = half // 2
    ret = (rel > 0).astype(I32) * half
    n = jnp.abs(rel)
    nf = jnp.maximum(n, 1).astype(F32)
    large = max_exact + (jnp.log(nf / max_exact) / math.log(REL_MAX_DIST / max_exact)
                         * (half - max_exact)).astype(I32)
    large = jnp.minimum(large, half - 1)
    return ret + jnp.where(n < max_exact, n, large)


def _in_even_kernel(x_ref, g_ref, w_ref, pm_ref, qg_ref, kg_ref, lb_ref,
                    q_ref, k_ref, v_ref, qh_ref, kb_ref, lf_ref, ih_ref, gs_ref, *, width, q_scale):
    xn = _rms(x_ref[...], g_ref[...]).astype(BF16)

    def proj(c):
        return _dot(xn, w_ref[:, c * width:(c + 1) * width])

    def group_norm(y, gain):
        ms = _dot((y * y).astype(BF16), pm_ref[...])
        return y * lax.rsqrt(ms + EPS) * gain

    q_ref[...] = group_norm(proj(0), qg_ref[...]) * q_scale
    k_ref[...] = group_norm(proj(1), kg_ref[...])
    v_ref[...] = proj(2)
    yq = proj(3)
    qh_ref[...] = yq * _sigmoid(yq)
    zf = proj(4)
    lb = lb_ref[...]
    lf_ref[...] = jnp.log(lb + (1.0 - lb) * _sigmoid(zf))
    kb_ref[...] = (1.0 - lb) * _sigmoid(-zf)
    ih_ref[...] = proj(5)
    yg = proj(6)
    gs_ref[...] = yg * _sigmoid(yg)


def _in_even(x, g_mix, w_bf, pm, qg, kg, lb, *, width, q_scale, tm):
    n, d = x.shape
    assert n % tm == 0
    row = lambda i: (i, 0)
    out = jax.ShapeDtypeStruct((n, width), F32)
    return pl.pallas_call(
        functools.partial(_in_even_kernel, width=width, q_scale=q_scale),
        grid=(n // tm,),
        in_specs=[pl.BlockSpec((tm, d), row), _const_spec((1, d)), _const_spec(w_bf.shape),
                  _const_spec(pm.shape), _const_spec((1, width)), _const_spec((1, width)),
                  _const_spec((1, width))],
        out_specs=[pl.BlockSpec((tm, width), row)] * 8,
        out_shape=[out] * 8,
        compiler_params=_params("arbitrary"),
        name="in_even",
    )(x, g_mix, w_bf, pm, qg, kg, lb)


def _k_layout_kernel(k_ref, o_ref, *, heads, dh):
    k = k_ref[...]
    for h in range(heads):
        for c in range(2):
            o_ref[0, :, h, c, :] = k[:, (2 * h + c) * dh:(2 * h + c + 1) * dh]


def _k_layout(k, *, batch, seq, heads, dh, tm):
    per = seq // tm
    return pl.pallas_call(
        functools.partial(_k_layout_kernel, heads=heads, dh=dh),
        grid=(batch * per,),
        in_specs=[pl.BlockSpec((tm, heads * 2 * dh), lambda i: (i, 0))],
        out_specs=pl.BlockSpec((1, tm, heads, 2, dh), lambda i: (i // per, i % per, 0, 0, 0)),
        out_shape=jax.ShapeDtypeStruct((batch, seq, heads, 2, dh), F32),
        compiler_params=_params("arbitrary"),
        name="k_layout",
    )(k)


def _bias_from_buckets(bk, rb_ref, h):
    b = jnp.zeros(bk.shape, F32)
    for u in range(REL_BUCKETS):
        b = jnp.where(bk == u, rb_ref[u, h], b)
    return jnp.where(bk < 0, -jnp.inf, b)


def _lam(lam_ref, lam_init):
    r = lam_ref[...]
    s1 = jnp.sum(r[0:1] * r[1:2], axis=1, keepdims=True)
    s2 = jnp.sum(r[2:3] * r[3:4], axis=1, keepdims=True)
    return jnp.exp(s1) - jnp.exp(s2) + lam_init


def _split_components(q, dh):
    lane = lax.broadcasted_iota(I32, q.shape, 1)
    q0 = jnp.where(lane < dh, q, 0.0)
    q1 = jnp.where(lane >= dh, q, 0.0)
    return jnp.concatenate([q0, q1], axis=0).astype(BF16)


def _attn_prompt_kernel(rb_ref, far_ref, lam_ref, bk_ref, sg_ref, q_ref, k_ref, v_ref, o_ref, bias_s, *,
                        qt, dh, lam_init):
    h = pl.program_id(0)
    t = q_ref.shape[0]
    kb = k_ref[...].astype(BF16)
    vt = v_ref[...].T.astype(BF16)
    qtr = q_ref[...].T
    sub = lax.broadcasted_iota(I32, (2 * dh, qt), 0)

    @pl.when(pl.program_id(1) == 0)
    def _():
        for d in range(2):
            b = _bias_from_buckets(bk_ref[d], rb_ref, h) * LOG2E
            bias_s[d] = jnp.concatenate([b, b], axis=1)

    bias = [bias_s[0], bias_s[1]]
    far = rb_ref[far_ref[0], h] * LOG2E
    lam = _lam(lam_ref, lam_init)
    gain = sg_ref[...] * (1.0 - lam_init)
    for i in range(t // qt):
        qi = qtr[:, i * qt:(i + 1) * qt]
        qz = jnp.concatenate([jnp.where(sub < dh, qi, 0.0), jnp.where(sub >= dh, qi, 0.0)],
                             axis=1).astype(BF16)
        n = (i + 1) * qt
        parts = [(n - qt, n)]
        s = [_dot(kb[n - qt:n], qz) + bias[0]]
        shift = [0.0]
        if i >= 1:
            parts.append((n - 2 * qt, n - qt))
            s.append(_dot(kb[n - 2 * qt:n - qt], qz) + bias[1])
            shift.append(0.0)
        if i >= 2:
            parts.append((0, n - 2 * qt))
            s.append(_dot(kb[:n - 2 * qt], qz))
            shift.append(far)
        m = functools.reduce(jnp.maximum, [jnp.max(x, axis=0, keepdims=True) + c for x, c in zip(s, shift)])
        p = [jnp.exp2(x - (m - c)) for x, c in zip(s, shift)]
        l = functools.reduce(lambda a, b: a + b, [jnp.sum(x, axis=0, keepdims=True) for x in p])
        acc = functools.reduce(lambda a, b: a + b,
                               [_dot(vt[:, lo:hi], x.astype(BF16)) for (lo, hi), x in zip(parts, p)])
        o = acc / l
        out = o[:, :qt] - lam * o[:, qt:]
        out = out * lax.rsqrt(jnp.mean(out * out, axis=0, keepdims=True) + EPS) * gain
        o_ref[i * qt:(i + 1) * qt, :] = out.T


def _attn_prompt(q, k, v, rel_bias, lam4, sub_gain, *, batch, seq, heads, dh, lam_init, qt):
    n, w = q.shape
    dv = w // heads
    assert dv == 2 * dh and seq % qt == 0 and qt % CHUNK == 0
    kj = jnp.arange(qt, dtype=I32)[:, None]
    qi = jnp.arange(qt, dtype=I32)[None, :]
    bk0 = jnp.where((kj // CHUNK) <= (qi // CHUNK), rel_bucket(kj - qi), -1)
    bk1 = rel_bucket(kj - qi - qt)
    bk = jnp.stack([bk0, bk1]).astype(I32)
    assert qt + 1 >= REL_MAX_DIST
    far = rel_bucket(jnp.full((1,), -(qt + 1), I32))
    smem = pl.BlockSpec(memory_space=pltpu.SMEM)
    seq_blk = pl.BlockSpec((seq, dv), lambda h, b: (b, h))
    return pl.pallas_call(
        functools.partial(_attn_prompt_kernel, qt=qt, dh=dh, lam_init=lam_init),
        grid=(heads, batch),
        in_specs=[smem, smem, _const_spec(lam4.shape), _const_spec(bk.shape), _const_spec((dv, 1)),
                  seq_blk, seq_blk, seq_blk],
        out_specs=seq_blk,
        out_shape=jax.ShapeDtypeStruct((n, w), F32),
        scratch_shapes=[pltpu.VMEM((2, qt, 2 * qt), F32)],
        compiler_params=_params("arbitrary", "arbitrary"),
        name="attn_prompt",
    )(rel_bias, far, lam4, bk, sub_gain.T, q, k, v)


def _attn_sample_kernel(rb_ref, lam_ref, bkc_ref, bkn_ref, sg_ref, q_ref, kc_ref, vc_ref, kn_ref, vn_ref,
                        o_ref, bc_s, bn_s, *, t, dh, lam_init):
    h = pl.program_id(0)
    qz = _split_components(q_ref[...], dh)

    @pl.when(pl.program_id(1) == 0)
    def _():
        bc = _bias_from_buckets(bkc_ref[...], rb_ref, h) * LOG2E
        bn = _bias_from_buckets(bkn_ref[...], rb_ref, h) * LOG2E
        bc_s[...] = jnp.concatenate([bc, bc], axis=0)
        bn_s[...] = jnp.concatenate([bn, bn], axis=0)

    sc = lax.dot_general(qz, kc_ref[0].astype(BF16), NT_DIMS, preferred_element_type=F32)
    sn = lax.dot_general(qz, kn_ref[...].astype(BF16), NT_DIMS, preferred_element_type=F32)
    sc = sc + bc_s[...]
    sn = sn + bn_s[...]
    m = jnp.maximum(jnp.max(sc, axis=1, keepdims=True), jnp.max(sn, axis=1, keepdims=True))
    pc = jnp.exp2(sc - m)
    pn = jnp.exp2(sn - m)
    l = jnp.sum(pc, axis=1, keepdims=True) + jnp.sum(pn, axis=1, keepdims=True)
    acc = _dot(pc.astype(BF16), vc_ref[0].astype(BF16)) + _dot(pn.astype(BF16), vn_ref[...].astype(BF16))
    o = acc / l
    out = o[:t] - _lam(lam_ref, lam_init) * o[t:]
    o_ref[...] = _rms(out, sg_ref[...]) * (1.0 - lam_init)


def _attn_sample(q, k_new, v_new, cache_k, cache_v, rel_bias, lam4, sub_gain, *, batch, t, heads, dh,
                 lam_init):
    n, w = q.shape
    dv = w // heads
    past = cache_k.shape[1]
    assert past % CHUNK == 0 and t <= CHUNK
    qpos = past + jnp.arange(t, dtype=I32)[:, None]
    bkc = rel_bucket(jnp.arange(past, dtype=I32)[None, :] - qpos).astype(I32)
    bkn = rel_bucket(past + jnp.arange(t, dtype=I32)[None, :] - qpos).astype(I32)
    smem = pl.BlockSpec(memory_space=pltpu.SMEM)
    new = pl.BlockSpec((t, dv), lambda h, b: (b, h))
    old = pl.BlockSpec((1, past, dv), lambda h, b: (b, 0, h))
    return pl.pallas_call(
        functools.partial(_attn_sample_kernel, t=t, dh=dh, lam_init=lam_init),
        grid=(heads, batch),
        in_specs=[smem, _const_spec(lam4.shape), _const_spec(bkc.shape), _const_spec(bkn.shape),
                  _const_spec((1, dv)), new, old, old, new, new],
        out_specs=new,
        out_shape=jax.ShapeDtypeStruct((n, w), F32),
        scratch_shapes=[pltpu.VMEM((2 * t, past), F32), pltpu.VMEM((2 * t, t), F32)],
        compiler_params=_params("arbitrary", "arbitrary"),
        name="attn_sample",
    )(rel_bias, lam4, bkc, bkn, sub_gain, q, cache_k, cache_v, k_new, v_new)


def _cumsum_rows(x):
    c = x.shape[0]
    row = lax.broadcasted_iota(I32, x.shape, 0)
    s = 1
    while s < c:
        x = x + jnp.where(row >= s, pltpu.roll(x, s, axis=0), 0.0)
        s *= 2
    return x


def _hgrn_kernel(qh_ref, kb_ref, lf_ref, ih_ref, s0_ref, hg_ref, ob_ref, sf_ref, st_s, *,
                 heads, dk, dv, c, nsb):
    t = pl.program_id(1)

    @pl.when(t == 0)
    def _():
        for h in range(heads):
            st_s[h] = s0_ref[0, h].T

    tb = qh_ref.shape[0]
    row = lax.broadcasted_iota(I32, (c, c), 0)
    col = lax.broadcasted_iota(I32, (c, c), 1)
    causal = col <= row

    def chunk(ci, carry):
        r0 = pl.multiple_of(ci * c, c)
        for h in range(heads):
            rows = pl.ds(r0, c)
            q = qh_ref[rows, h * dk:(h + 1) * dk]
            k = kb_ref[rows, h * dk:(h + 1) * dk]
            v = ih_ref[rows, h * dv:(h + 1) * dv]
            b = _cumsum_rows(lf_ref[rows, h * dk:(h + 1) * dk])
            bl = b[c - 1:c]
            st = st_s[h]
            inter = lax.dot_general((q * jnp.exp(b)).astype(BF16), st.astype(BF16), NT_DIMS,
                                    preferred_element_type=F32)
            qs, ks = [], []
            for j in range(nsb):
                ref = b[j * HG_SUB + HG_SUB // 2:j * HG_SUB + HG_SUB // 2 + 1]
                qs.append(q * jnp.exp(b - ref))
                sub = slice(j * HG_SUB, (j + 1) * HG_SUB)
                ks.append(k[sub] * jnp.exp(ref - b[sub]))
            a_full = lax.dot_general(jnp.concatenate(qs, axis=0).astype(BF16),
                                     jnp.concatenate(ks, axis=0).astype(BF16), NT_DIMS,
                                     preferred_element_type=F32)
            att = jnp.zeros((c, c), F32)
            for j in range(nsb):
                att = jnp.where(col >= j * HG_SUB, a_full[j * c:(j + 1) * c], att)
            att = jnp.where(causal, att, 0.0)
            out = inter + _dot(att.astype(BF16), v.astype(BF16))
            ob_ref[rows, h * dv:(h + 1) * dv] = _rms(out, hg_ref[...])
            kdec = (k * jnp.exp(bl - b)).astype(BF16)
            st_s[h] = jnp.exp(bl) * st + lax.dot_general(v.astype(BF16), kdec, TN_DIMS,
                                                         preferred_element_type=F32)
        return carry

    lax.fori_loop(0, tb // c, chunk, 0, unroll=min(4, tb // c))

    @pl.when(t == pl.num_programs(1) - 1)
    def _():
        for h in range(heads):
            sf_ref[0, h] = st_s[h].T


def _hgrn(qh, kb, lf, ih, s0, hg_gain, *, batch, seq, heads, dk, dv, tb):
    n = qh.shape[0]
    c = min(CHUNK, seq)
    assert seq % tb == 0 and tb % c == 0 and c % HG_SUB == 0
    nt = seq // tb
    blk = lambda w: pl.BlockSpec((tb, w), lambda b, t: (b * nt + t, 0))
    st = pl.BlockSpec((1, heads, dk, dv), lambda b, t: (b, 0, 0, 0))
    return pl.pallas_call(
        functools.partial(_hgrn_kernel, heads=heads, dk=dk, dv=dv, c=c, nsb=c // HG_SUB),
        grid=(batch, nt),
        in_specs=[blk(heads * dk), blk(heads * dk), blk(heads * dk), blk(heads * dv), st,
                  _const_spec((1, dv))],
        out_specs=[blk(heads * dv), st],
        out_shape=[jax.ShapeDtypeStruct((n, heads * dv), F32),
                   jax.ShapeDtypeStruct((batch, heads, dk, dv), F32)],
        scratch_shapes=[pltpu.VMEM((heads, dv, dk), F32)],
        compiler_params=_params("arbitrary", "arbitrary"),
        name="hgrn",
    )(qh, kb, lf, ih, s0, hg_gain)


def _store_token_tiles(ref, x):
    rows, d = x.shape
    s = d // LANES
    for c in range(s):
        ref[pl.ds(c, rows, stride=s), :] = x[:, c * LANES:(c + 1) * LANES]


def _load_token_tiles(ref, rows, s):
    return jnp.concatenate([ref[pl.ds(c, rows, stride=s), :] for c in range(s)], axis=1)


def _token_tile(r, s):
    return pl.ds(pl.multiple_of(r * s, s), s)


def _ffn_prologue(x1, gf_ref, wrh_ref, wrl_ref, rbias_ref, x1_ref, xn_ref, lg_ref):
    x1_ref[...] = x1
    xn = _rms(x1, gf_ref[...])
    _store_token_tiles(xn_ref, xn)
    hi = xn.astype(BF16)
    lo = (xn - hi.astype(F32)).astype(BF16)
    nt = functools.partial(lax.dot_general, dimension_numbers=NT_DIMS, preferred_element_type=F32)
    lg_ref[...] = nt(wrh_ref[...], hi) + nt(wrh_ref[...], lo) + nt(wrl_ref[...], hi) + rbias_ref[...]


def _out_even_kernel(oa_ref, ob_ref, gs_ref, x_ref, w_ref, gf_ref, wrh_ref, wrl_ref, rbias_ref,
                     x1_ref, xn_ref, lg_ref):
    o = jnp.concatenate([oa_ref[...], ob_ref[...] * gs_ref[...]], axis=1).astype(BF16)
    x1 = x_ref[...] + _dot(o, w_ref[...])
    _ffn_prologue(x1, gf_ref, wrh_ref, wrl_ref, rbias_ref, x1_ref, xn_ref, lg_ref)


def _out_even(oa, ob, gs, x, w_bf, g_ffn, wrh, wrl, rbias, *, tm):
    n, d = x.shape
    w = oa.shape[1]
    nr = wrh.shape[0]
    row = lambda i: (i, 0)
    return pl.pallas_call(
        _out_even_kernel,
        grid=(n // tm,),
        in_specs=[pl.BlockSpec((tm, w), row), pl.BlockSpec((tm, w), row), pl.BlockSpec((tm, w), row),
                  pl.BlockSpec((tm, d), row), _const_spec(w_bf.shape), _const_spec((1, d)),
                  _const_spec(wrh.shape), _const_spec(wrl.shape), _const_spec(rbias.shape)],
        out_specs=[pl.BlockSpec((tm, d), row), pl.BlockSpec((tm * (d // LANES), LANES), row),
                   pl.BlockSpec((nr, tm), lambda i: (0, i))],
        out_shape=[jax.ShapeDtypeStruct((n, d), F32), jax.ShapeDtypeStruct((n * (d // LANES), LANES), F32),
                   jax.ShapeDtypeStruct((nr, n), F32)],
        compiler_params=_params("arbitrary"),
        name="out_even",
    )(oa, ob, gs, x, w_bf, g_ffn, wrh, wrl, rbias)


def _gelu(x):
    return 0.5 * x * (1.0 + jnp.tanh(math.sqrt(2.0 / math.pi) * (x + 0.044715 * (x * x * x))))


def _odd_kernel(x_ref, gm_ref, win_ref, vg_ref, wsp_ref, bsp_ref, wout_ref, gf_ref, wrh_ref, wrl_ref,
                rbias_ref, x1_ref, xn_ref, lg_ref, *rest, half, groups, l, emit_v):
    if emit_v:
        vn_ref, u_s, s_s = rest
    else:
        vn_ref = None
        u_s, vn_s, s_s = rest
    tm = x_ref.shape[0]
    x = x_ref[...]
    xn = _rms(x, gm_ref[...]).astype(BF16)
    cw = 512
    vbuf = vn_ref if emit_v else vn_s
    for cidx in range(half // cw):
        u_s[:, cidx * cw:(cidx + 1) * cw] = _gelu(_dot(xn, win_ref[:, cidx * cw:(cidx + 1) * cw]))
        vbuf[:, cidx * cw:(cidx + 1) * cw] = _gelu(
            _dot(xn, win_ref[:, half + cidx * cw:half + (cidx + 1) * cw]))
    vbuf[...] = _rms(vbuf[...], vg_ref[...])
    gw = half // groups
    row = lax.broadcasted_iota(I32, (l, l), 0)
    col = lax.broadcasted_iota(I32, (l, l), 1)
    for g in range(groups):
        wg = jnp.where(col <= row, wsp_ref[g], 0.0).astype(BF16)
        bg = bsp_ref[:, g:g + 1]
        for ci in range(tm // l):
            vv = vbuf[ci * l:(ci + 1) * l, g * gw:(g + 1) * gw].astype(BF16)
            s_s[ci * l:(ci + 1) * l, g * gw:(g + 1) * gw] = _dot(wg, vv) + bg
    y = _dot((u_s[...] * s_s[...]).astype(BF16), wout_ref[...])
    _ffn_prologue(x + y, gf_ref, wrh_ref, wrl_ref, rbias_ref, x1_ref, xn_ref, lg_ref)


def _odd_mixer(x, g_mix, win_bf, v_gain, wsp, bsp_t, wout_bf, g_ffn, wrh, wrl, rbias, *, l, tm, emit_v):
    n, d = x.shape
    half = wout_bf.shape[0]
    groups = wsp.shape[0]
    nr = wrh.shape[0]
    assert n % tm == 0 and tm % l == 0
    row = lambda i: (i, 0)
    out_specs = [pl.BlockSpec((tm, d), row), pl.BlockSpec((tm * (d // LANES), LANES), row),
                 pl.BlockSpec((nr, tm), lambda i: (0, i))]
    out_shape = [jax.ShapeDtypeStruct((n, d), F32), jax.ShapeDtypeStruct((n * (d // LANES), LANES), F32),
                 jax.ShapeDtypeStruct((nr, n), F32)]
    scratch = [pltpu.VMEM((tm, half), F32)]
    if emit_v:
        out_specs.append(pl.BlockSpec((tm, half), row))
        out_shape.append(jax.ShapeDtypeStruct((n, half), F32))
    else:
        scratch.append(pltpu.VMEM((tm, half), F32))
    scratch.append(pltpu.VMEM((tm, half), F32))
    return pl.pallas_call(
        functools.partial(_odd_kernel, half=half, groups=groups, l=l, emit_v=emit_v),
        grid=(n // tm,),
        in_specs=[pl.BlockSpec((tm, d), row), _const_spec((1, d)), _const_spec(win_bf.shape),
                  _const_spec((1, half)), _const_spec(wsp.shape), _const_spec(bsp_t.shape),
                  _const_spec(wout_bf.shape), _const_spec((1, d)), _const_spec(wrh.shape),
                  _const_spec(wrl.shape), _const_spec(rbias.shape)],
        out_specs=out_specs,
        out_shape=out_shape,
        scratch_shapes=scratch,
        compiler_params=_params("arbitrary"),
        name="odd_mixer",
    )(x, g_mix, win_bf, v_gain, wsp, bsp_t, wout_bf, g_ffn, wrh, wrl, rbias)


def _route_kernel(lg_ref, tri_ref, e_ref, g_ref, r_ref, p_ref, cnt_ref, base_ref, tcnt_ref, run_s, *,
                  groups, epg):
    i = pl.program_id(0)

    @pl.when(i == 0)
    def _():
        run_s[...] = jnp.zeros(run_s.shape, F32)

    lg = lg_ref[...]
    tr = lg.shape[1]
    gl = [lg[g:g + 1] for g in range(groups)]
    m = functools.reduce(jnp.maximum, gl)
    grp = jnp.full((1, tr), groups - 1, I32)
    for g in range(groups - 2, -1, -1):
        grp = jnp.where(gl[g] == m, g, grp)
    gate_g = 1.0 / functools.reduce(lambda a, b: a + b, [jnp.exp(x - m) for x in gl])
    sel = lg[SUBLANES + (groups - 1) * epg:SUBLANES + groups * epg]
    for g in range(groups - 2, -1, -1):
        sel = jnp.where(grp == g, lg[SUBLANES + g * epg:SUBLANES + (g + 1) * epg], sel)
    sub = lax.broadcasted_iota(I32, sel.shape, 0)
    v1 = jnp.max(sel, axis=0, keepdims=True)
    i1 = jnp.min(jnp.where(sel == v1, sub, epg), axis=0, keepdims=True)
    sel2 = jnp.where(sub == i1, -jnp.inf, sel)
    v2 = jnp.max(sel2, axis=0, keepdims=True)
    i2 = jnp.min(jnp.where(sel2 == v2, sub, epg), axis=0, keepdims=True)
    tt = jnp.exp(v2 - v1)
    g1 = gate_g / (1.0 + tt)
    g2 = gate_g * tt / (1.0 + tt)
    e1 = grp * epg + i1
    e2 = grp * epg + i2
    ne = groups * epg
    eidx = lax.broadcasted_iota(I32, (ne, tr), 0)
    oh1 = eidx == e1
    oh2 = eidx == e2
    cnt = jnp.where(oh1, 1.0, 0.0) + jnp.where(oh2, 1.0, 0.0)
    local = _dot(cnt.astype(BF16), tri_ref[...])
    before = run_s[:, 0:1] + local
    r1 = jnp.sum(jnp.where(oh1, before, 0.0), axis=0, keepdims=True)
    r2 = jnp.sum(jnp.where(oh2, before, 0.0), axis=0, keepdims=True)
    tile_cnt = jnp.broadcast_to(jnp.sum(cnt, axis=1, keepdims=True), run_s.shape)
    padded = jnp.ceil(tile_cnt * (1.0 / RUN_CHUNK)) * RUN_CHUNK
    offset = _cumsum_rows(padded) - padded
    where_local = offset[:, 0:1] + local
    p1 = jnp.sum(jnp.where(oh1, where_local, 0.0), axis=0, keepdims=True)
    p2 = jnp.sum(jnp.where(oh2, where_local, 0.0), axis=0, keepdims=True)
    base_ref[...] = run_s[...]
    tcnt_ref[...] = tile_cnt
    run_s[...] = run_s[...] + tile_cnt
    rows = lax.broadcasted_iota(I32, (SUBLANES, tr), 0)
    e_ref[...] = jnp.where(rows == 0, e1, jnp.where(rows == 1, e2, 0))
    g_ref[...] = jnp.where(rows == 0, g1, jnp.where(rows == 1, g2, 0.0))
    r_ref[...] = jnp.where(rows == 0, r1, jnp.where(rows == 1, r2, 0.0)).astype(I32)
    p_ref[...] = jnp.where(rows == 0, p1, jnp.where(rows == 1, p2, 0.0)).astype(I32)
    cnt_ref[...] = run_s[...]


def _route(lgt, *, groups, epg, tr):
    nr, n = lgt.shape
    assert n % tr == 0 and nr == SUBLANES + groups * epg
    ne = groups * epg
    tri = (jnp.arange(tr)[:, None] < jnp.arange(tr)[None, :]).astype(BF16)
    tok = pl.BlockSpec((SUBLANES, tr), lambda i: (0, i))
    per_tile = pl.BlockSpec((ne, LANES), lambda i: (i, 0))
    tile_tab = jax.ShapeDtypeStruct((n // tr * ne, LANES), F32)
    return pl.pallas_call(
        functools.partial(_route_kernel, groups=groups, epg=epg),
        grid=(n // tr,),
        in_specs=[pl.BlockSpec((nr, tr), lambda i: (0, i)), _const_spec((tr, tr))],
        out_specs=[tok, tok, tok, tok, pl.BlockSpec((ne, LANES), lambda i: (0, 0)), per_tile, per_tile],
        out_shape=[jax.ShapeDtypeStruct((SUBLANES, n), I32), jax.ShapeDtypeStruct((SUBLANES, n), F32),
                   jax.ShapeDtypeStruct((SUBLANES, n), I32), jax.ShapeDtypeStruct((SUBLANES, n), I32),
                   jax.ShapeDtypeStruct((ne, LANES), F32), tile_tab, tile_tab],
        scratch_shapes=[pltpu.VMEM((ne, LANES), F32)],
        compiler_params=_params("arbitrary"),
        name="route",
    )(lgt, tri)


ISSUE_UNROLL = 8


def _dispatch_kernel(seg_ref, dst_ref, x_ref, buf_ref, zero_s, sem, zsem, *, s, bm):
    tp = x_ref.shape[0] // s
    blk = bm * s

    @pl.when(pl.program_id(0) == 0)
    def _():
        zero_s[...] = jnp.zeros(zero_s.shape, F32)

        def block_copy(b):
            return pltpu.make_async_copy(zero_s, buf_ref.at[pl.ds(pl.multiple_of(b * blk, blk), blk)], zsem)

        for e in range(seg_ref.shape[1]):
            @pl.when(seg_ref[1, e] > 0)
            def _():
                block_copy(seg_ref[0, e] // bm - 1).start()

            @pl.when(seg_ref[1, e] - seg_ref[2, e] > bm)
            def _():
                block_copy(seg_ref[0, e] // bm - 2).start()
        for e in range(seg_ref.shape[1]):
            @pl.when(seg_ref[1, e] > 0)
            def _():
                block_copy(0).wait()

            @pl.when(seg_ref[1, e] - seg_ref[2, e] > bm)
            def _():
                block_copy(0).wait()

        def tail_start(b, c):
            block_copy(b).start()
            return c

        def tail_wait(b, c):
            block_copy(b).wait()
            return c

        first_unused = seg_ref[0, seg_ref.shape[1] - 1] // bm
        n_blocks = buf_ref.shape[0] // blk
        lax.fori_loop(first_unused, n_blocks, tail_start, 0)
        lax.fori_loop(first_unused, n_blocks, tail_wait, 0)

    def issue(r0, c):
        for u in range(ISSUE_UNROLL):
            r = r0 * ISSUE_UNROLL + u
            for kk in range(TOP_K):
                pltpu.make_async_copy(x_ref.at[_token_tile(r, s)],
                                      buf_ref.at[_token_tile(dst_ref[0, kk, r], s)], sem
                                      ).start(priority=kk % 2)
        return c

    lax.fori_loop(0, tp // ISSUE_UNROLL, issue, 0)
    for kk in range(TOP_K):
        pltpu.make_async_copy(x_ref, buf_ref.at[pl.ds(0, tp * s)], sem).wait()


def _dispatch(seg, dest3, xn, n_slots, *, tp, s, bm):
    n = xn.shape[0] // s
    assert tp % ISSUE_UNROLL == 0
    grid_spec = pltpu.PrefetchScalarGridSpec(
        num_scalar_prefetch=1,
        grid=(n // tp,),
        in_specs=[pl.BlockSpec((1, TOP_K, tp), lambda i, sg: (i, 0, 0), memory_space=pltpu.SMEM),
                  pl.BlockSpec((tp * s, LANES), lambda i, sg: (i, 0))],
        out_specs=pl.BlockSpec(memory_space=pl.ANY),
        scratch_shapes=[pltpu.VMEM((bm * s, LANES), F32), pltpu.SemaphoreType.DMA(()),
                        pltpu.SemaphoreType.DMA(())],
    )
    return pl.pallas_call(
        functools.partial(_dispatch_kernel, s=s, bm=bm),
        grid_spec=grid_spec,
        out_shape=jax.ShapeDtypeStruct((n_slots * s, LANES), F32),
        compiler_params=_params("arbitrary"),
        name="dispatch",
    )(seg, dest3, xn)


def _expert_kernel(be_ref, nu_ref, x_ref, wg_ref, wu_ref, wd_ref, o_ref, wg_s, wu_s, wd_s):
    b = pl.program_id(0)
    s = wg_s.shape[0] // LANES
    prev = be_ref[jnp.maximum(b - 1, 0)]

    @pl.when((b == 0) | (be_ref[b] != prev))
    def _():
        wg_s[...] = wg_ref[0, 0].astype(BF16)
        wu_s[...] = wu_ref[0, 0].astype(BF16)
        wd_s[...] = wd_ref[0, 0].astype(BF16)

    @pl.when(b < nu_ref[0])
    def _():
        xb = _load_token_tiles(x_ref, x_ref.shape[0] // s, s).astype(BF16)
        gate = _dot(xb, wg_s[...])
        h = gate * _sigmoid(gate) * _dot(xb, wu_s[...])
        _store_token_tiles(o_ref, _dot(h.astype(BF16), wd_s[...]))

    @pl.when(b >= nu_ref[0])
    def _():
        o_ref[...] = jnp.zeros(o_ref.shape, F32)


def _experts(blk_expert, n_used, buf, w_gate, w_up, w_down, *, layer, bm):
    d, de = w_gate.shape[2:]
    s = d // LANES
    n_blocks = buf.shape[0] // (bm * s)
    rows = pl.BlockSpec((bm * s, LANES), lambda b, be, nu: (b, 0))
    used_rows = pl.BlockSpec((bm * s, LANES), lambda b, be, nu: (jnp.minimum(b, nu[0] - 1), 0))
    grid_spec = pltpu.PrefetchScalarGridSpec(
        num_scalar_prefetch=2,
        grid=(n_blocks,),
        in_specs=[used_rows,
                  pl.BlockSpec((1, 1, d, de), lambda b, be, nu: (layer, be[b], 0, 0)),
                  pl.BlockSpec((1, 1, d, de), lambda b, be, nu: (layer, be[b], 0, 0)),
                  pl.BlockSpec((1, 1, de, d), lambda b, be, nu: (layer, be[b], 0, 0))],
        out_specs=rows,
        scratch_shapes=[pltpu.VMEM((d, de), BF16), pltpu.VMEM((d, de), BF16), pltpu.VMEM((de, d), BF16)],
    )
    return pl.pallas_call(
        _expert_kernel,
        grid_spec=grid_spec,
        out_shape=jax.ShapeDtypeStruct(buf.shape, F32),
        compiler_params=_params("arbitrary"),
        name="experts",
    )(blk_expert, n_used, buf, w_gate, w_up, w_down)


def _combine_kernel(tab_ref, nxt_ref, pos_ref, gate_ref, x_ref, yb_ref, o_ref, stg_s, tt_s, sem):
    i = pl.program_id(0)
    tq, d = x_ref.shape
    s = d // LANES
    piece = RUN_CHUNK * s
    slot = i % 2

    def run_copies(ref, dst_slot, go):
        for e in range(ref.shape[2]):
            def body(j, c):
                src = pl.multiple_of((ref[0, 0, e] + j * RUN_CHUNK) * s, s)
                dst = pl.multiple_of((ref[0, 2, e] + j * RUN_CHUNK) * s, piece)
                cp = pltpu.make_async_copy(yb_ref.at[pl.ds(src, piece)],
                                           stg_s.at[dst_slot, pl.ds(dst, piece)], sem.at[dst_slot])
                cp.start() if go else cp.wait()
                return c
            lax.fori_loop(0, ref[0, 1, e], body, 0)

    @pl.when(i == 0)
    def _():
        run_copies(tab_ref, 0, True)

    @pl.when(i + 1 < pl.num_programs(0))
    def _():
        run_copies(nxt_ref, 1 - slot, True)

    run_copies(tab_ref, slot, False)

    def assemble(r0, c):
        for u in range(ISSUE_UNROLL):
            r = r0 * ISSUE_UNROLL + u
            acc = gate_ref[0, 0, r] * stg_s[slot, _token_tile(pos_ref[0, 0, r], s), :]
            for kk in range(1, TOP_K):
                acc = acc + gate_ref[0, kk, r] * stg_s[slot, _token_tile(pos_ref[0, kk, r], s), :]
            tt_s[_token_tile(r, s), :] = acc
        return c

    lax.fori_loop(0, tq // ISSUE_UNROLL, assemble, 0)
    o_ref[...] = x_ref[...] + _load_token_tiles(tt_s, tq, s)


def _combine(tab, pos3, gates3, x1, yb, *, tq):
    n, d = x1.shape
    s = d // LANES
    nt = n // tq
    ne = tab.shape[2]
    assert tq % ISSUE_UNROLL == 0
    stage_rows = tq * TOP_K + ne * RUN_CHUNK
    smem = lambda shape, imap: pl.BlockSpec(shape, imap, memory_space=pltpu.SMEM)
    return pl.pallas_call(
        _combine_kernel,
        grid=(nt,),
        in_specs=[smem((1, 3, ne), lambda i: (i, 0, 0)),
                  smem((1, 3, ne), lambda i: (jnp.minimum(i + 1, nt - 1), 0, 0)),
                  smem((1, TOP_K, tq), lambda i: (i, 0, 0)),
                  smem((1, TOP_K, tq), lambda i: (i, 0, 0)),
                  pl.BlockSpec((tq, d), lambda i: (i, 0)),
                  pl.BlockSpec(memory_space=pl.ANY)],
        out_specs=pl.BlockSpec((tq, d), lambda i: (i, 0)),
        out_shape=jax.ShapeDtypeStruct((n, d), F32),
        scratch_shapes=[pltpu.VMEM((2, stage_rows * s, LANES), F32), pltpu.VMEM((tq * s, LANES), F32),
                        pltpu.SemaphoreType.DMA((2,))],
        compiler_params=_params("arbitrary"),
        name="combine",
    )(tab, tab, pos3, gates3, x1, yb)


def _moe(x1, xn, lgt, w_gate, w_up, w_down, *, layer, groups, epg):
    n, d = x1.shape
    ne = groups * epg
    rtile = min(512, n)
    tile = rtile
    e8, g8, r8, p8, cnt, base, tcnt = _route(lgt, groups=groups, epg=epg, tr=rtile)
    counts = cnt[:, 0].astype(I32)
    bm = MOE_BLOCK if n * TOP_K >= 2 * ne * MOE_BLOCK else MOE_BLOCK_SMALL
    padded = (counts + RUN_CHUNK + bm - 1) // bm * bm
    pend = jnp.cumsum(padded)
    pstart = pend - padded
    n_blocks = -(-(n * TOP_K + ne * RUN_CHUNK) // bm) + ne
    eids = jnp.arange(ne, dtype=I32)
    seg = jnp.sum(jnp.where(e8[:TOP_K, :, None] == eids, pstart, 0), axis=-1)
    dest = seg + r8[:TOP_K]
    dest3 = dest.reshape(TOP_K, n // tile, tile).transpose(1, 0, 2)
    blk_row = jnp.arange(n_blocks, dtype=I32)[:, None] * bm
    blk_expert = jnp.minimum(jnp.sum((pend[None, :] <= blk_row).astype(I32), axis=1), ne - 1)
    n_used = (pend[-1:] // bm).astype(I32)
    buf = _dispatch(jnp.stack([pend, padded, counts]).astype(I32), dest3, xn, n_blocks * bm, tp=tile,
                    s=d // LANES, bm=bm)
    yb = _experts(blk_expert, n_used, buf, w_gate, w_up, w_down, layer=layer, bm=bm)
    nrt = n // rtile
    run_start = pstart[None, :] + base.reshape(nrt, ne, LANES)[:, :, 0].astype(I32)
    pieces = (tcnt.reshape(nrt, ne, LANES)[:, :, 0].astype(I32) + RUN_CHUNK - 1) // RUN_CHUNK
    stage = (jnp.cumsum(pieces, axis=1) - pieces) * RUN_CHUNK
    tab = jnp.stack([run_start, pieces, stage], axis=1)
    by_tile = lambda a: a[:TOP_K].reshape(TOP_K, nrt, rtile).transpose(1, 0, 2)
    return _combine(tab, by_tile(p8), by_tile(g8), x1, yb, tq=rtile)


def _router_weights(wg, bg, we, be):
    d, groups = wg.shape
    epg = we.shape[2]
    assert groups <= SUBLANES and epg == SUBLANES
    pad = jnp.zeros((SUBLANES - groups, d), F32)
    wr = jnp.concatenate([wg.T, pad, we.transpose(0, 2, 1).reshape(groups * epg, d)], axis=0)
    rb = jnp.concatenate([bg, jnp.zeros((SUBLANES - groups,), F32), be.reshape(-1)])[:, None]
    hi = wr.astype(BF16)
    lo = (wr - hi.astype(F32)).astype(BF16)
    return hi, lo, rb, groups, epg


def kernel(x_prompt, x_sample, cache_attn_k, cache_attn_v, state_hgrn, rel_bias, norm_mix, norm_ffn,
           w_in_even, w_out_even, q_norm_gain, k_norm_gain, lam_q1, lam_k1, lam_q2, lam_k2, da_out_gain,
           hgrn_lb_logits, hgrn_out_gain, w_in_odd, sgu_v_gain, sgu_w, sgu_b, w_out_odd,
           router_group_w, router_group_b, router_expert_w, router_expert_b,
           expert_w_gate, expert_w_up, expert_w_down):
    bp, tp, d = x_prompt.shape
    bs, ts, _ = x_sample.shape
    depth = norm_mix.shape[0]
    _, _, past, da_heads, _, da_dh = cache_attn_k.shape
    da_dv = cache_attn_v.shape[-1]
    _, _, hg_heads, hg_dk, hg_dv = state_hgrn.shape
    width = da_heads * da_dv
    assert width == da_heads * 2 * da_dh == hg_heads * hg_dk == hg_heads * hg_dv
    assert da_dv == LANES and hg_dk == LANES and hg_dv == LANES

    lb_all = jnp.cumsum(jax.nn.softmax(hgrn_lb_logits.astype(F32), axis=0), axis=0)
    gid = jnp.arange(width) // da_dh
    pm = jnp.where(gid[:, None] == gid[None, :], 1.0 / da_dh, 0.0).astype(BF16)

    xs = {"p": x_prompt.reshape(bp * tp, d), "s": x_sample.reshape(bs * ts, d)}
    dims = {"p": (bp, tp), "s": (bs, ts)}
    outs = {"p": {}, "s": {}}
    kp_l, vp_l, ks_l, vs_l, sp_l, ss_l, sgu_l = [], [], [], [], [], [], []

    for layer in range(depth):
        j = layer // 2
        wrh, wrl, rbias, groups, epg = _router_weights(
            router_group_w[layer], router_group_b[layer], router_expert_w[layer], router_expert_b[layer])
        g_mix = norm_mix[layer][None, :]
        g_ffn = norm_ffn[layer][None, :]
        if layer % 2 == 0:
            lam_init = 0.8 - 0.6 * math.exp(-0.3 * layer)
            w_in_bf = w_in_even[j].astype(BF16)
            w_out_bf = w_out_even[j].astype(BF16)
            reps = width // da_dh
            qg = jnp.tile(q_norm_gain[j], reps)[None, :]
            kg = jnp.tile(k_norm_gain[j], reps)[None, :]
            lam4 = jnp.stack([lam_q1[j], lam_k1[j], lam_q2[j], lam_k2[j]])
            sub_gain = da_out_gain[j][None, :]
            hg_gain = hgrn_out_gain[j][None, :]
            lb = lb_all[j][None, :]
            for key in ("p", "s"):
                b, t = dims[key]
                x = xs[key]
                q, k, v, qh, kb, lf, ih, gs = _in_even(
                    x, g_mix, w_in_bf, pm, qg, kg, lb, width=width, q_scale=da_dh ** -0.5 * LOG2E, tm=min(512, b * t))
                if key == "p":
                    oa = _attn_prompt(q, k, v, rel_bias, lam4, sub_gain, batch=b, seq=t, heads=da_heads,
                                      dh=da_dh, lam_init=lam_init, qt=min(256, t))
                    s0 = jnp.zeros((b, hg_heads, hg_dk, hg_dv), F32)
                    ob, s_new = _hgrn(qh, kb, lf, ih, s0, hg_gain, batch=b, seq=t, heads=hg_heads,
                                      dk=hg_dk, dv=hg_dv, tb=min(512, t))
                    kp_l.append(_k_layout(k, batch=b, seq=t, heads=da_heads, dh=da_dh, tm=min(512, t)))
                    vp_l.append(v.reshape(b, t, da_heads, da_dv))
                    sp_l.append(s_new)
                else:
                    ck = cache_attn_k[j].reshape(b, past, width)
                    cv = cache_attn_v[j].reshape(b, past, width)
                    oa = _attn_sample(q, k, v, ck, cv, rel_bias, lam4, sub_gain, batch=b, t=t,
                                      heads=da_heads, dh=da_dh, lam_init=lam_init)
                    ob, s_new = _hgrn(qh, kb, lf, ih, state_hgrn[j], hg_gain, batch=b, seq=t,
                                      heads=hg_heads, dk=hg_dk, dv=hg_dv, tb=t)
                    ks_l.append(k.reshape(b, t, da_heads, 2, da_dh))
                    vs_l.append(v.reshape(b, t, da_heads, da_dv))
                    ss_l.append(s_new)
                outs[key] = _out_even(oa, ob, gs, x, w_out_bf, g_ffn, wrh, wrl, rbias, tm=min(512, b * t))
        else:
            w_in_bf = w_in_odd[j].astype(BF16)
            w_out_bf = w_out_odd[j].astype(BF16)
            v_gain = sgu_v_gain[j][None, :]
            for key in ("p", "s"):
                b, t = dims[key]
                l = min(SGU_CHUNK, t)
                res = _odd_mixer(xs[key], g_mix, w_in_bf, v_gain, sgu_w[j][:, :l, :l], sgu_b[j][:, :l].T,
                                 w_out_bf, g_ffn, wrh, wrl, rbias, l=l, tm=min(512, b * t), emit_v=(key == "s"))
                outs[key] = res[:3]
                if key == "s":
                    sgu_l.append(res[3].reshape(b, t, -1))
        for key in ("p", "s"):
            x1, xn, lgt = outs[key]
            xs[key] = _moe(x1, xn, lgt, expert_w_gate, expert_w_up, expert_w_down, layer=layer,
                           groups=groups, epg=epg)

    return (xs["p"].reshape(bp, tp, d), xs["s"].reshape(bs, ts, d), jnp.stack(kp_l), jnp.stack(vp_l),
            jnp.stack(ks_l), jnp.stack(vs_l), jnp.stack(sp_l), jnp.stack(ss_l), jnp.stack(sgu_l))
```

```python
import functools
import math

import jax
import jax.numpy as jnp
from jax import lax
from jax.experimental import pallas as pl
from jax.experimental.pallas import tpu as pltpu

F32 = jnp.float32
BF16 = jnp.bfloat16
I32 = jnp.int32

EPS = 1e-6
LOG2E = math.log2(math.e)
CHUNK = 64
SGU_CHUNK = 128
REL_BUCKETS = 32
REL_MAX_DIST = 128
TOP_K = 2
MOE_BLOCK = 512
MOE_BLOCK_SMALL = 128
RUN_CHUNK = 16
HG_SUB = 16

LANES = 128
SUBLANES = 8
VMEM_LIMIT = 56 * 1024 * 1024

NT_DIMS = (((1,), (1,)), ((), ()))
TN_DIMS = (((0,), (0,)), ((), ()))


def _params(*sem):
    return pltpu.CompilerParams(dimension_semantics=sem, vmem_limit_bytes=VMEM_LIMIT)


def _const_spec(shape):
    nd = len(shape)
    return pl.BlockSpec(shape, lambda *_: (0,) * nd, pipeline_mode=pl.Buffered(1))


def _sigmoid(x):
    return 1.0 / (1.0 + jnp.exp(-x))


def _rms(x, g):
    return x * lax.rsqrt(jnp.mean(x * x, axis=-1, keepdims=True) + EPS) * g


def _dot(a, b):
    return jnp.dot(a, b, preferred_element_type=F32)


def rel_bucket(rel):
    half = REL_BUCKETS // 2
    max_exact = half // 2
    ret = (rel > 0).astype(I32) * half
    n = jnp.abs(rel)
    nf = jnp.maximum(n, 1).astype(F32)
    large = max_exact + (jnp.log(nf / max_exact) / math.log(REL_MAX_DIST / max_exact)
                         * (half - max_exact)).astype(I32)
    large = jnp.minimum(large, half - 1)
    return ret + jnp.where(n < max_exact, n, large)


def _in_even_kernel(x_ref, g_ref, w_ref, pm_ref, qg_ref, kg_ref, lb_ref,
                    q_ref, k_ref, v_ref, qh_ref, kb_ref, lf_ref, ih_ref, gs_ref, k5_ref, v4_ref, *, width,
                    q_scale):
    xn = _rms(x_ref[...], g_ref[...]).astype(BF16)

    def proj(c):
        return _dot(xn, w_ref[:, c * width:(c + 1) * width])

    def group_norm(y, gain):
        ms = _dot((y * y).astype(BF16), pm_ref[...])
        return y * lax.rsqrt(ms + EPS) * gain

    q_ref[...] = group_norm(proj(0), qg_ref[...]) * q_scale
    kn = group_norm(proj(1), kg_ref[...])
    k_ref[...] = kn
    bb, ts, heads, _, dh = k5_ref.shape
    for h in range(heads):
        for c in range(2):
            piece = kn[:, (2 * h + c) * dh:(2 * h + c + 1) * dh]
            k5_ref[:, :, h, c, :] = piece.reshape(bb, ts, dh)
    vv = proj(2)
    v_ref[...] = vv
    for h in range(heads):
        v4_ref[:, :, h, :] = vv[:, h * 2 * dh:(h + 1) * 2 * dh].reshape(bb, ts, 2 * dh)
    yq = proj(3)
    qh_ref[...] = yq * _sigmoid(yq)
    zf = proj(4)
    lb = lb_ref[...]
    lf_ref[...] = jnp.log(lb + (1.0 - lb) * _sigmoid(zf))
    kb_ref[...] = (1.0 - lb) * _sigmoid(-zf)
    ih_ref[...] = proj(5)
    yg = proj(6)
    gs_ref[...] = yg * _sigmoid(yg)


def _in_even(x, g_mix, w_bf, pm, qg, kg, lb, *, width, q_scale, tm, batch, seq, heads, dh):
    n, d = x.shape
    assert n % tm == 0 and (tm % seq == 0 or seq % tm == 0)
    row = lambda i: (i, 0)
    out = jax.ShapeDtypeStruct((n, width), F32)
    bb, ts = max(tm // seq, 1), min(tm, seq)
    per = seq // ts
    k5_spec = pl.BlockSpec((bb, ts, heads, 2, dh), lambda i: (i // per, i % per, 0, 0, 0))
    v4_spec = pl.BlockSpec((bb, ts, heads, 2 * dh), lambda i: (i // per, i % per, 0, 0))
    return pl.pallas_call(
        functools.partial(_in_even_kernel, width=width, q_scale=q_scale),
        grid=(n // tm,),
        in_specs=[pl.BlockSpec((tm, d), row), _const_spec((1, d)), _const_spec(w_bf.shape),
                  _const_spec(pm.shape), _const_spec((1, width)), _const_spec((1, width)),
                  _const_spec((1, width))],
        out_specs=[pl.BlockSpec((tm, width), row)] * 8 + [k5_spec, v4_spec],
        out_shape=[out] * 8 + [jax.ShapeDtypeStruct((batch, seq, heads, 2, dh), F32),
                               jax.ShapeDtypeStruct((batch, seq, heads, 2 * dh), F32)],
        compiler_params=_params("arbitrary"),
        name="in_even",
    )(x, g_mix, w_bf, pm, qg, kg, lb)


def _bias_from_buckets(bk, rb_ref, h):
    b = jnp.zeros(bk.shape, F32)
    for u in range(REL_BUCKETS):
        b = jnp.where(bk == u, rb_ref[u, h], b)
    return jnp.where(bk < 0, -jnp.inf, b)


def _lam(lam_ref, lam_init):
    r = lam_ref[...]
    s1 = jnp.sum(r[0:1] * r[1:2], axis=1, keepdims=True)
    s2 = jnp.sum(r[2:3] * r[3:4], axis=1, keepdims=True)
    return jnp.exp(s1) - jnp.exp(s2) + lam_init


def _split_components(q, dh):
    lane = lax.broadcasted_iota(I32, q.shape, 1)
    q0 = jnp.where(lane < dh, q, 0.0)
    q1 = jnp.where(lane >= dh, q, 0.0)
    return jnp.concatenate([q0, q1], axis=0).astype(BF16)


def _attn_prompt_kernel(rb_ref, far_ref, lam_ref, bk_ref, sg_ref, q_ref, k_ref, v_ref, o_ref, bias_s, *,
                        qt, dh, lam_init):
    h = pl.program_id(0)
    t = q_ref.shape[0]
    kb = k_ref[...].astype(BF16)
    vt = v_ref[...].T.astype(BF16)
    qtr = q_ref[...].T
    sub = lax.broadcasted_iota(I32, (2 * dh, qt), 0)

    @pl.when(pl.program_id(1) == 0)
    def _():
        for d in range(2):
            b = _bias_from_buckets(bk_ref[d], rb_ref, h) * LOG2E
            bias_s[d] = jnp.concatenate([b, b], axis=1)

    bias = [bias_s[0], bias_s[1]]
    far = rb_ref[far_ref[0], h] * LOG2E
    lam = _lam(lam_ref, lam_init)
    gain = sg_ref[...] * (1.0 - lam_init)
    for i in range(t // qt):
        qi = qtr[:, i * qt:(i + 1) * qt]
        qz = jnp.concatenate([jnp.where(sub < dh, qi, 0.0), jnp.where(sub >= dh, qi, 0.0)],
                             axis=1).astype(BF16)
        n = (i + 1) * qt
        parts = [(n - qt, n)]
        s = [_dot(kb[n - qt:n], qz) + bias[0]]
        shift = [0.0]
        if i >= 1:
            parts.append((n - 2 * qt, n - qt))
            s.append(_dot(kb[n - 2 * qt:n - qt], qz) + bias[1])
            shift.append(0.0)
        if i >= 2:
            parts.append((0, n - 2 * qt))
            s.append(_dot(kb[:n - 2 * qt], qz))
            shift.append(far)
        m = functools.reduce(jnp.maximum, [jnp.max(x, axis=0, keepdims=True) + c for x, c in zip(s, shift)])
        p = [jnp.exp2(x - (m - c)) for x, c in zip(s, shift)]
        l = functools.reduce(lambda a, b: a + b, [jnp.sum(x, axis=0, keepdims=True) for x in p])
        acc = functools.reduce(lambda a, b: a + b,
                               [_dot(vt[:, lo:hi], x.astype(BF16)) for (lo, hi), x in zip(parts, p)])
        o = acc / l
        out = o[:, :qt] - lam * o[:, qt:]
        out = out * lax.rsqrt(jnp.mean(out * out, axis=0, keepdims=True) + EPS) * gain
        o_ref[i * qt:(i + 1) * qt, :] = out.T


def _attn_prompt(q, k, v, rel_bias, lam4, sub_gain, *, batch, seq, heads, dh, lam_init, qt):
    n, w = q.shape
    dv = w // heads
    assert dv == 2 * dh and seq % qt == 0 and qt % CHUNK == 0
    kj = jnp.arange(qt, dtype=I32)[:, None]
    qi = jnp.arange(qt, dtype=I32)[None, :]
    bk0 = jnp.where((kj // CHUNK) <= (qi // CHUNK), rel_bucket(kj - qi), -1)
    bk1 = rel_bucket(kj - qi - qt)
    bk = jnp.stack([bk0, bk1]).astype(I32)
    assert qt + 1 >= REL_MAX_DIST
    far = rel_bucket(jnp.full((1,), -(qt + 1), I32))
    smem = pl.BlockSpec(memory_space=pltpu.SMEM)
    seq_blk = pl.BlockSpec((seq, dv), lambda h, b: (b, h))
    return pl.pallas_call(
        functools.partial(_attn_prompt_kernel, qt=qt, dh=dh, lam_init=lam_init),
        grid=(heads, batch),
        in_specs=[smem, smem, _const_spec(lam4.shape), _const_spec(bk.shape), _const_spec((dv, 1)),
                  seq_blk, seq_blk, seq_blk],
        out_specs=seq_blk,
        out_shape=jax.ShapeDtypeStruct((n, w), F32),
        scratch_shapes=[pltpu.VMEM((2, qt, 2 * qt), F32)],
        compiler_params=_params("arbitrary", "arbitrary"),
        name="attn_prompt",
    )(rel_bias, far, lam4, bk, sub_gain.T, q, k, v)


def _attn_sample_kernel(rb_ref, lam_ref, bkc_ref, bkn_ref, sg_ref, q_ref, kc_ref, vc_ref, kn_ref, vn_ref,
                        o_ref, bc_s, bn_s, *, t, dh, lam_init):
    h = pl.program_id(0)
    qz = _split_components(q_ref[...], dh)

    @pl.when(pl.program_id(1) == 0)
    def _():
        bc = _bias_from_buckets(bkc_ref[...], rb_ref, h) * LOG2E
        bn = _bias_from_buckets(bkn_ref[...], rb_ref, h) * LOG2E
        bc_s[...] = jnp.concatenate([bc, bc], axis=0)
        bn_s[...] = jnp.concatenate([bn, bn], axis=0)

    sc = lax.dot_general(qz, kc_ref[0].astype(BF16), NT_DIMS, preferred_element_type=F32)
    sn = lax.dot_general(qz, kn_ref[...].astype(BF16), NT_DIMS, preferred_element_type=F32)
    sc = sc + bc_s[...]
    sn = sn + bn_s[...]
    m = jnp.maximum(jnp.max(sc, axis=1, keepdims=True), jnp.max(sn, axis=1, keepdims=True))
    pc = jnp.exp2(sc - m)
    pn = jnp.exp2(sn - m)
    l = jnp.sum(pc, axis=1, keepdims=True) + jnp.sum(pn, axis=1, keepdims=True)
    acc = _dot(pc.astype(BF16), vc_ref[0].astype(BF16)) + _dot(pn.astype(BF16), vn_ref[...].astype(BF16))
    o = acc / l
    out = o[:t] - _lam(lam_ref, lam_init) * o[t:]
    o_ref[...] = _rms(out, sg_ref[...]) * (1.0 - lam_init)


def _attn_sample(q, k_new, v_new, cache_k, cache_v, rel_bias, lam4, sub_gain, *, batch, t, heads, dh,
                 lam_init):
    n, w = q.shape
    dv = w // heads
    past = cache_k.shape[1]
    assert past % CHUNK == 0 and t <= CHUNK
    qpos = past + jnp.arange(t, dtype=I32)[:, None]
    bkc = rel_bucket(jnp.arange(past, dtype=I32)[None, :] - qpos).astype(I32)
    bkn = rel_bucket(past + jnp.arange(t, dtype=I32)[None, :] - qpos).astype(I32)
    smem = pl.BlockSpec(memory_space=pltpu.SMEM)
    new = pl.BlockSpec((t, dv), lambda h, b: (b, h))
    old = pl.BlockSpec((1, past, dv), lambda h, b: (b, 0, h))
    return pl.pallas_call(
        functools.partial(_attn_sample_kernel, t=t, dh=dh, lam_init=lam_init),
        grid=(heads, batch),
        in_specs=[smem, _const_spec(lam4.shape), _const_spec(bkc.shape), _const_spec(bkn.shape),
                  _const_spec((1, dv)), new, old, old, new, new],
        out_specs=new,
        out_shape=jax.ShapeDtypeStruct((n, w), F32),
        scratch_shapes=[pltpu.VMEM((2 * t, past), F32), pltpu.VMEM((2 * t, t), F32)],
        compiler_params=_params("arbitrary", "arbitrary"),
        name="attn_sample",
    )(rel_bias, lam4, bkc, bkn, sub_gain, q, cache_k, cache_v, k_new, v_new)


def _cumsum_rows(x):
    c = x.shape[0]
    row = lax.broadcasted_iota(I32, x.shape, 0)
    s = 1
    while s < c:
        x = x + jnp.where(row >= s, pltpu.roll(x, s, axis=0), 0.0)
        s *= 2
    return x


def _hgrn_kernel(qh_ref, kb_ref, lf_ref, ih_ref, s0_ref, hg_ref, ob_ref, sf_ref, st_s, *,
                 heads, dk, dv, c, nsb):
    t = pl.program_id(1)

    @pl.when(t == 0)
    def _():
        for h in range(heads):
            st_s[h] = s0_ref[0, h].T

    tb = qh_ref.shape[0]
    row = lax.broadcasted_iota(I32, (c, c), 0)
    col = lax.broadcasted_iota(I32, (c, c), 1)
    causal = col <= row

    def chunk(ci, carry):
        r0 = pl.multiple_of(ci * c, c)
        for h in range(heads):
            rows = pl.ds(r0, c)
            q = qh_ref[rows, h * dk:(h + 1) * dk]
            k = kb_ref[rows, h * dk:(h + 1) * dk]
            v = ih_ref[rows, h * dv:(h + 1) * dv]
            b = _cumsum_rows(lf_ref[rows, h * dk:(h + 1) * dk])
            bl = b[c - 1:c]
            st = st_s[h]
            inter = lax.dot_general((q * jnp.exp(b)).astype(BF16), st.astype(BF16), NT_DIMS,
                                    preferred_element_type=F32)
            qs, ks = [], []
            for j in range(nsb):
                ref = b[j * HG_SUB + HG_SUB // 2:j * HG_SUB + HG_SUB // 2 + 1]
                qs.append(q * jnp.exp(b - ref))
                sub = slice(j * HG_SUB, (j + 1) * HG_SUB)
                ks.append(k[sub] * jnp.exp(ref - b[sub]))
            a_full = lax.dot_general(jnp.concatenate(qs, axis=0).astype(BF16),
                                     jnp.concatenate(ks, axis=0).astype(BF16), NT_DIMS,
                                     preferred_element_type=F32)
            att = jnp.zeros((c, c), F32)
            for j in range(nsb):
                att = jnp.where(col >= j * HG_SUB, a_full[j * c:(j + 1) * c], att)
            att = jnp.where(causal, att, 0.0)
            out = inter + _dot(att.astype(BF16), v.astype(BF16))
            ob_ref[rows, h * dv:(h + 1) * dv] = _rms(out, hg_ref[...])
            kdec = (k * jnp.exp(bl - b)).astype(BF16)
            st_s[h] = jnp.exp(bl) * st + lax.dot_general(v.astype(BF16), kdec, TN_DIMS,
                                                         preferred_element_type=F32)
        return carry

    lax.fori_loop(0, tb // c, chunk, 0, unroll=min(4, tb // c))

    @pl.when(t == pl.num_programs(1) - 1)
    def _():
        for h in range(heads):
            sf_ref[0, h] = st_s[h].T


def _hgrn(qh, kb, lf, ih, s0, hg_gain, *, batch, seq, heads, dk, dv, tb):
    n = qh.shape[0]
    c = min(CHUNK, seq)
    assert seq % tb == 0 and tb % c == 0 and c % HG_SUB == 0
    nt = seq // tb
    blk = lambda w: pl.BlockSpec((tb, w), lambda b, t: (b * nt + t, 0))
    st = pl.BlockSpec((1, heads, dk, dv), lambda b, t: (b, 0, 0, 0))
    return pl.pallas_call(
        functools.partial(_hgrn_kernel, heads=heads, dk=dk, dv=dv, c=c, nsb=c // HG_SUB),
        grid=(batch, nt),
        in_specs=[blk(heads * dk), blk(heads * dk), blk(heads * dk), blk(heads * dv), st,
                  _const_spec((1, dv))],
        out_specs=[blk(heads * dv), st],
        out_shape=[jax.ShapeDtypeStruct((n, heads * dv), F32),
                   jax.ShapeDtypeStruct((batch, heads, dk, dv), F32)],
        scratch_shapes=[pltpu.VMEM((heads, dv, dk), F32)],
        compiler_params=_params("arbitrary", "arbitrary"),
        name="hgrn",
    )(qh, kb, lf, ih, s0, hg_gain)


def _store_token_tiles(ref, x):
    rows, d = x.shape
    s = d // LANES
    for c in range(s):
        ref[pl.ds(c, rows, stride=s), :] = x[:, c * LANES:(c + 1) * LANES]


def _load_token_tiles(ref, rows, s):
    return jnp.concatenate([ref[pl.ds(c, rows, stride=s), :] for c in range(s)], axis=1)


def _token_tile(r, s):
    return pl.ds(pl.multiple_of(r * s, s), s)


def _ffn_prologue(x1, gf_ref, wrh_ref, wrl_ref, rbias_ref, x1_ref, xn_ref, lg_ref):
    x1_ref[...] = x1
    xn = _rms(x1, gf_ref[...])
    _store_token_tiles(xn_ref, xn)
    hi = xn.astype(BF16)
    lo = (xn - hi.astype(F32)).astype(BF16)
    nt = functools.partial(lax.dot_general, dimension_numbers=NT_DIMS, preferred_element_type=F32)
    lg_ref[...] = nt(wrh_ref[...], hi) + nt(wrh_ref[...], lo) + nt(wrl_ref[...], hi) + rbias_ref[...]


def _out_even_kernel(oa_ref, ob_ref, gs_ref, x_ref, w_ref, gf_ref, wrh_ref, wrl_ref, rbias_ref,
                     x1_ref, xn_ref, lg_ref):
    o = jnp.concatenate([oa_ref[...], ob_ref[...] * gs_ref[...]], axis=1).astype(BF16)
    x1 = x_ref[...] + _dot(o, w_ref[...])
    _ffn_prologue(x1, gf_ref, wrh_ref, wrl_ref, rbias_ref, x1_ref, xn_ref, lg_ref)


def _out_even(oa, ob, gs, x, w_bf, g_ffn, wrh, wrl, rbias, *, tm):
    n, d = x.shape
    w = oa.shape[1]
    nr = wrh.shape[0]
    row = lambda i: (i, 0)
    return pl.pallas_call(
        _out_even_kernel,
        grid=(n // tm,),
        in_specs=[pl.BlockSpec((tm, w), row), pl.BlockSpec((tm, w), row), pl.BlockSpec((tm, w), row),
                  pl.BlockSpec((tm, d), row), _const_spec(w_bf.shape), _const_spec((1, d)),
                  _const_spec(wrh.shape), _const_spec(wrl.shape), _const_spec(rbias.shape)],
        out_specs=[pl.BlockSpec((tm, d), row), pl.BlockSpec((tm * (d // LANES), LANES), row),
                   pl.BlockSpec((nr, tm), lambda i: (0, i))],
        out_shape=[jax.ShapeDtypeStruct((n, d), F32), jax.ShapeDtypeStruct((n * (d // LANES), LANES), F32),
                   jax.ShapeDtypeStruct((nr, n), F32)],
        compiler_params=_params("arbitrary"),
        name="out_even",
    )(oa, ob, gs, x, w_bf, g_ffn, wrh, wrl, rbias)


def _gelu(x):
    return 0.5 * x * (1.0 + jnp.tanh(math.sqrt(2.0 / math.pi) * (x + 0.044715 * (x * x * x))))


def _odd_kernel(x_ref, gm_ref, win_ref, vg_ref, wsp_ref, bsp_ref, wout_ref, gf_ref, wrh_ref, wrl_ref,
                rbias_ref, x1_ref, xn_ref, lg_ref, *rest, half, groups, l, emit_v):
    if emit_v:
        vn_ref, u_s, s_s = rest
    else:
        vn_ref = None
        u_s, vn_s, s_s = rest
    tm = x_ref.shape[0]
    x = x_ref[...]
    xn = _rms(x, gm_ref[...]).astype(BF16)
    cw = 512
    vbuf = vn_ref if emit_v else vn_s
    for cidx in range(half // cw):
        u_s[:, cidx * cw:(cidx + 1) * cw] = _gelu(_dot(xn, win_ref[:, cidx * cw:(cidx + 1) * cw]))
        vbuf[:, cidx * cw:(cidx + 1) * cw] = _gelu(
            _dot(xn, win_ref[:, half + cidx * cw:half + (cidx + 1) * cw]))
    vbuf[...] = _rms(vbuf[...], vg_ref[...])
    gw = half // groups
    row = lax.broadcasted_iota(I32, (l, l), 0)
    col = lax.broadcasted_iota(I32, (l, l), 1)
    for g in range(groups):
        wg = jnp.where(col <= row, wsp_ref[g], 0.0).astype(BF16)
        bg = bsp_ref[:, g:g + 1]
        for ci in range(tm // l):
            vv = vbuf[ci * l:(ci + 1) * l, g * gw:(g + 1) * gw].astype(BF16)
            s_s[ci * l:(ci + 1) * l, g * gw:(g + 1) * gw] = _dot(wg, vv) + bg
    y = _dot((u_s[...] * s_s[...]).astype(BF16), wout_ref[...])
    _ffn_prologue(x + y, gf_ref, wrh_ref, wrl_ref, rbias_ref, x1_ref, xn_ref, lg_ref)


def _odd_mixer(x, g_mix, win_bf, v_gain, wsp, bsp_t, wout_bf, g_ffn, wrh, wrl, rbias, *, l, tm, emit_v):
    n, d = x.shape
    half = wout_bf.shape[0]
    groups = wsp.shape[0]
    nr = wrh.shape[0]
    assert n % tm == 0 and tm % l == 0
    row = lambda i: (i, 0)
    out_specs = [pl.BlockSpec((tm, d), row), pl.BlockSpec((tm * (d // LANES), LANES), row),
                 pl.BlockSpec((nr, tm), lambda i: (0, i))]
    out_shape = [jax.ShapeDtypeStruct((n, d), F32), jax.ShapeDtypeStruct((n * (d // LANES), LANES), F32),
                 jax.ShapeDtypeStruct((nr, n), F32)]
    scratch = [pltpu.VMEM((tm, half), F32)]
    if emit_v:
        out_specs.append(pl.BlockSpec((tm, half), row))
        out_shape.append(jax.ShapeDtypeStruct((n, half), F32))
    else:
        scratch.append(pltpu.VMEM((tm, half), F32))
    scratch.append(pltpu.VMEM((tm, half), F32))
    return pl.pallas_call(
        functools.partial(_odd_kernel, half=half, groups=groups, l=l, emit_v=emit_v),
        grid=(n // tm,),
        in_specs=[pl.BlockSpec((tm, d), row), _const_spec((1, d)), _const_spec(win_bf.shape),
                  _const_spec((1, half)), _const_spec(wsp.shape), _const_spec(bsp_t.shape),
                  _const_spec(wout_bf.shape), _const_spec((1, d)), _const_spec(wrh.shape),
                  _const_spec(wrl.shape), _const_spec(rbias.shape)],
        out_specs=out_specs,
        out_shape=out_shape,
        scratch_shapes=scratch,
        compiler_params=_params("arbitrary"),
        name="odd_mixer",
    )(x, g_mix, win_bf, v_gain, wsp, bsp_t, wout_bf, g_ffn, wrh, wrl, rbias)


def _route_kernel(lg_ref, tri_ref, e_ref, g_ref, r_ref, p_ref, cnt_ref, base_ref, tcnt_ref, run_s, *,
                  groups, epg):
    i = pl.program_id(0)

    @pl.when(i == 0)
    def _():
        run_s[...] = jnp.zeros(run_s.shape, F32)

    lg = lg_ref[...]
    tr = lg.shape[1]
    gl = [lg[g:g + 1] for g in range(groups)]
    m = functools.reduce(jnp.maximum, gl)
    grp = jnp.full((1, tr), groups - 1, I32)
    for g in range(groups - 2, -1, -1):
        grp = jnp.where(gl[g] == m, g, grp)
    gate_g = 1.0 / functools.reduce(lambda a, b: a + b, [jnp.exp(x - m) for x in gl])
    sel = lg[SUBLANES + (groups - 1) * epg:SUBLANES + groups * epg]
    for g in range(groups - 2, -1, -1):
        sel = jnp.where(grp == g, lg[SUBLANES + g * epg:SUBLANES + (g + 1) * epg], sel)
    sub = lax.broadcasted_iota(I32, sel.shape, 0)
    v1 = jnp.max(sel, axis=0, keepdims=True)
    i1 = jnp.min(jnp.where(sel == v1, sub, epg), axis=0, keepdims=True)
    sel2 = jnp.where(sub == i1, -jnp.inf, sel)
    v2 = jnp.max(sel2, axis=0, keepdims=True)
    i2 = jnp.min(jnp.where(sel2 == v2, sub, epg), axis=0, keepdims=True)
    tt = jnp.exp(v2 - v1)
    g1 = gate_g / (1.0 + tt)
    g2 = gate_g * tt / (1.0 + tt)
    e1 = grp * epg + i1
    e2 = grp * epg + i2
    ne = groups * epg
    eidx = lax.broadcasted_iota(I32, (ne, tr), 0)
    oh1 = eidx == e1
    oh2 = eidx == e2
    cnt = jnp.where(oh1, 1.0, 0.0) + jnp.where(oh2, 1.0, 0.0)
    local = _dot(cnt.astype(BF16), tri_ref[...])
    before = run_s[:, 0:1] + local
    r1 = jnp.sum(jnp.where(oh1, before, 0.0), axis=0, keepdims=True)
    r2 = jnp.sum(jnp.where(oh2, before, 0.0), axis=0, keepdims=True)
    tile_cnt = jnp.broadcast_to(jnp.sum(cnt, axis=1, keepdims=True), run_s.shape)
    padded = jnp.ceil(tile_cnt * (1.0 / RUN_CHUNK)) * RUN_CHUNK
    offset = _cumsum_rows(padded) - padded
    where_local = offset[:, 0:1] + local
    p1 = jnp.sum(jnp.where(oh1, where_local, 0.0), axis=0, keepdims=True)
    p2 = jnp.sum(jnp.where(oh2, where_local, 0.0), axis=0, keepdims=True)
    base_ref[...] = run_s[...]
    tcnt_ref[...] = tile_cnt
    run_s[...] = run_s[...] + tile_cnt
    rows = lax.broadcasted_iota(I32, (SUBLANES, tr), 0)
    e_ref[...] = jnp.where(rows == 0, e1, jnp.where(rows == 1, e2, 0))
    g_ref[...] = jnp.where(rows == 0, g1, jnp.where(rows == 1, g2, 0.0))
    r_ref[...] = jnp.where(rows == 0, r1, jnp.where(rows == 1, r2, 0.0)).astype(I32)
    p_ref[...] = jnp.where(rows == 0, p1, jnp.where(rows == 1, p2, 0.0)).astype(I32)
    cnt_ref[...] = run_s[...]


def _route(lgt, *, groups, epg, tr):
    nr, n = lgt.shape
    assert n % tr == 0 and nr == SUBLANES + groups * epg
    ne = groups * epg
    tri = (jnp.arange(tr)[:, None] < jnp.arange(tr)[None, :]).astype(BF16)
    tok = pl.BlockSpec((SUBLANES, tr), lambda i: (0, i))
    per_tile = pl.BlockSpec((ne, LANES), lambda i: (i, 0))
    tile_tab = jax.ShapeDtypeStruct((n // tr * ne, LANES), F32)
    return pl.pallas_call(
        functools.partial(_route_kernel, groups=groups, epg=epg),
        grid=(n // tr,),
        in_specs=[pl.BlockSpec((nr, tr), lambda i: (0, i)), _const_spec((tr, tr))],
        out_specs=[tok, tok, tok, tok, pl.BlockSpec((ne, LANES), lambda i: (0, 0)), per_tile, per_tile],
        out_shape=[jax.ShapeDtypeStruct((SUBLANES, n), I32), jax.ShapeDtypeStruct((SUBLANES, n), F32),
                   jax.ShapeDtypeStruct((SUBLANES, n), I32), jax.ShapeDtypeStruct((SUBLANES, n), I32),
                   jax.ShapeDtypeStruct((ne, LANES), F32), tile_tab, tile_tab],
        scratch_shapes=[pltpu.VMEM((ne, LANES), F32)],
        compiler_params=_params("arbitrary"),
        name="route",
    )(lgt, tri)


ISSUE_UNROLL = 8


def _dispatch_kernel(seg_ref, dst_ref, x_ref, buf_ref, zero_s, sem, zsem, *, s, bm):
    tp = x_ref.shape[0] // s
    blk = bm * s

    @pl.when(pl.program_id(0) == 0)
    def _():
        zero_s[...] = jnp.zeros(zero_s.shape, F32)

        def block_copy(b):
            return pltpu.make_async_copy(zero_s, buf_ref.at[pl.ds(pl.multiple_of(b * blk, blk), blk)], zsem)

        for e in range(seg_ref.shape[1]):
            @pl.when(seg_ref[1, e] > 0)
            def _():
                block_copy(seg_ref[0, e] // bm - 1).start()

            @pl.when(seg_ref[1, e] - seg_ref[2, e] > bm)
            def _():
                block_copy(seg_ref[0, e] // bm - 2).start()
        for e in range(seg_ref.shape[1]):
            @pl.when(seg_ref[1, e] > 0)
            def _():
                block_copy(0).wait()

            @pl.when(seg_ref[1, e] - seg_ref[2, e] > bm)
            def _():
                block_copy(0).wait()

        def tail_start(b, c):
            block_copy(b).start()
            return c

        def tail_wait(b, c):
            block_copy(b).wait()
            return c

        first_unused = seg_ref[0, seg_ref.shape[1] - 1] // bm
        n_blocks = buf_ref.shape[0] // blk
        lax.fori_loop(first_unused, n_blocks, tail_start, 0)
        lax.fori_loop(first_unused, n_blocks, tail_wait, 0)

    def issue(r0, c):
        for u in range(ISSUE_UNROLL):
            r = r0 * ISSUE_UNROLL + u
            for kk in range(TOP_K):
                pltpu.make_async_copy(x_ref.at[_token_tile(r, s)],
                                      buf_ref.at[_token_tile(dst_ref[0, kk, r], s)], sem
                                      ).start(priority=kk % 2)
        return c

    lax.fori_loop(0, tp // ISSUE_UNROLL, issue, 0)
    for kk in range(TOP_K):
        pltpu.make_async_copy(x_ref, buf_ref.at[pl.ds(0, tp * s)], sem).wait()


def _dispatch(seg, dest3, xn, n_slots, *, tp, s, bm):
    n = xn.shape[0] // s
    assert tp % ISSUE_UNROLL == 0
    grid_spec = pltpu.PrefetchScalarGridSpec(
        num_scalar_prefetch=1,
        grid=(n // tp,),
        in_specs=[pl.BlockSpec((1, TOP_K, tp), lambda i, sg: (i, 0, 0), memory_space=pltpu.SMEM),
                  pl.BlockSpec((tp * s, LANES), lambda i, sg: (i, 0))],
        out_specs=pl.BlockSpec(memory_space=pl.ANY),
        scratch_shapes=[pltpu.VMEM((bm * s, LANES), F32), pltpu.SemaphoreType.DMA(()),
                        pltpu.SemaphoreType.DMA(())],
    )
    return pl.pallas_call(
        functools.partial(_dispatch_kernel, s=s, bm=bm),
        grid_spec=grid_spec,
        out_shape=jax.ShapeDtypeStruct((n_slots * s, LANES), F32),
        compiler_params=_params("arbitrary"),
        name="dispatch",
    )(seg, dest3, xn)


def _expert_kernel(be_ref, nu_ref, x_ref, wg_ref, wu_ref, wd_ref, o_ref, wg_s, wu_s, wd_s):
    b = pl.program_id(0)
    s = wg_s.shape[0] // LANES
    prev = be_ref[jnp.maximum(b - 1, 0)]

    @pl.when((b == 0) | (be_ref[b] != prev))
    def _():
        wg_s[...] = wg_ref[0, 0].astype(BF16)
        wu_s[...] = wu_ref[0, 0].astype(BF16)
        wd_s[...] = wd_ref[0, 0].astype(BF16)

    @pl.when(b < nu_ref[0])
    def _():
        xb = _load_token_tiles(x_ref, x_ref.shape[0] // s, s).astype(BF16)
        gate = _dot(xb, wg_s[...])
        h = gate * _sigmoid(gate) * _dot(xb, wu_s[...])
        _store_token_tiles(o_ref, _dot(h.astype(BF16), wd_s[...]))

    @pl.when(b >= nu_ref[0])
    def _():
        o_ref[...] = jnp.zeros(o_ref.shape, F32)


def _experts(blk_expert, n_used, buf, w_gate, w_up, w_down, *, layer, bm):
    d, de = w_gate.shape[2:]
    s = d // LANES
    n_blocks = buf.shape[0] // (bm * s)
    rows = pl.BlockSpec((bm * s, LANES), lambda b, be, nu: (b, 0))
    used_rows = pl.BlockSpec((bm * s, LANES), lambda b, be, nu: (jnp.minimum(b, nu[0] - 1), 0))
    grid_spec = pltpu.PrefetchScalarGridSpec(
        num_scalar_prefetch=2,
        grid=(n_blocks,),
        in_specs=[used_rows,
                  pl.BlockSpec((1, 1, d, de), lambda b, be, nu: (layer, be[b], 0, 0)),
                  pl.BlockSpec((1, 1, d, de), lambda b, be, nu: (layer, be[b], 0, 0)),
                  pl.BlockSpec((1, 1, de, d), lambda b, be, nu: (layer, be[b], 0, 0))],
        out_specs=rows,
        scratch_shapes=[pltpu.VMEM((d, de), BF16), pltpu.VMEM((d, de), BF16), pltpu.VMEM((de, d), BF16)],
    )
    return pl.pallas_call(
        _expert_kernel,
        grid_spec=grid_spec,
        out_shape=jax.ShapeDtypeStruct(buf.shape, F32),
        compiler_params=_params("arbitrary"),
        name="experts",
    )(blk_expert, n_used, buf, w_gate, w_up, w_down)


def _combine_kernel(tab_ref, nxt_ref, pos_ref, gate_ref, x_ref, yb_ref, o_ref, stg_s, tt_s, sem):
    i = pl.program_id(0)
    tq, d = x_ref.shape
    s = d // LANES
    piece = RUN_CHUNK * s
    slot = i % 2

    def run_copies(ref, dst_slot, go):
        for e in range(ref.shape[2]):
            def body(j, c):
                src = pl.multiple_of((ref[0, 0, e] + j * RUN_CHUNK) * s, s)
                dst = pl.multiple_of((ref[0, 2, e] + j * RUN_CHUNK) * s, piece)
                cp = pltpu.make_async_copy(yb_ref.at[pl.ds(src, piece)],
                                           stg_s.at[dst_slot, pl.ds(dst, piece)], sem.at[dst_slot])
                cp.start() if go else cp.wait()
                return c
            lax.fori_loop(0, ref[0, 1, e], body, 0)

    @pl.when(i == 0)
    def _():
        run_copies(tab_ref, 0, True)

    @pl.when(i + 1 < pl.num_programs(0))
    def _():
        run_copies(nxt_ref, 1 - slot, True)

    run_copies(tab_ref, slot, False)

    def assemble(r0, c):
        for u in range(ISSUE_UNROLL):
            r = r0 * ISSUE_UNROLL + u
            acc = gate_ref[0, 0, r] * stg_s[slot, _token_tile(pos_ref[0, 0, r], s), :]
            for kk in range(1, TOP_K):
                acc = acc + gate_ref[0, kk, r] * stg_s[slot, _token_tile(pos_ref[0, kk, r], s), :]
            tt_s[_token_tile(r, s), :] = acc
        return c

    lax.fori_loop(0, tq // ISSUE_UNROLL, assemble, 0)
    o_ref[...] = x_ref[...] + _load_token_tiles(tt_s, tq, s)


def _combine(tab, pos3, gates3, x1, yb, *, tq):
    n, d = x1.shape
    s = d // LANES
    nt = n // tq
    ne = tab.shape[2]
    assert tq % ISSUE_UNROLL == 0
    stage_rows = tq * TOP_K + ne * RUN_CHUNK
    smem = lambda shape, imap: pl.BlockSpec(shape, imap, memory_space=pltpu.SMEM)
    return pl.pallas_call(
        _combine_kernel,
        grid=(nt,),
        in_specs=[smem((1, 3, ne), lambda i: (i, 0, 0)),
                  smem((1, 3, ne), lambda i: (jnp.minimum(i + 1, nt - 1), 0, 0)),
                  smem((1, TOP_K, tq), lambda i: (i, 0, 0)),
                  smem((1, TOP_K, tq), lambda i: (i, 0, 0)),
                  pl.BlockSpec((tq, d), lambda i: (i, 0)),
                  pl.BlockSpec(memory_space=pl.ANY)],
        out_specs=pl.BlockSpec((tq, d), lambda i: (i, 0)),
        out_shape=jax.ShapeDtypeStruct((n, d), F32),
        scratch_shapes=[pltpu.VMEM((2, stage_rows * s, LANES), F32), pltpu.VMEM((tq * s, LANES), F32),
                        pltpu.SemaphoreType.DMA((2,))],
        compiler_params=_params("arbitrary"),
        name="combine",
    )(tab, tab, pos3, gates3, x1, yb)


def _moe(x1, xn, lgt, w_gate, w_up, w_down, *, layer, groups, epg):
    n, d = x1.shape
    ne = groups * epg
    rtile = min(512, n)
    tile = rtile
    e8, g8, r8, p8, cnt, base, tcnt = _route(lgt, groups=groups, epg=epg, tr=rtile)
    counts = cnt[:, 0].astype(I32)
    bm = MOE_BLOCK if n * TOP_K >= 2 * ne * MOE_BLOCK else MOE_BLOCK_SMALL
    padded = (counts + RUN_CHUNK + bm - 1) // bm * bm
    pend = jnp.cumsum(padded)
    pstart = pend - padded
    n_blocks = -(-(n * TOP_K + ne * RUN_CHUNK) // bm) + ne
    eids = jnp.arange(ne, dtype=I32)
    seg = jnp.sum(jnp.where(e8[:TOP_K, :, None] == eids, pstart, 0), axis=-1)
    dest = seg + r8[:TOP_K]
    dest3 = dest.reshape(TOP_K, n // tile, tile).transpose(1, 0, 2)
    blk_row = jnp.arange(n_blocks, dtype=I32)[:, None] * bm
    blk_expert = jnp.minimum(jnp.sum((pend[None, :] <= blk_row).astype(I32), axis=1), ne - 1)
    n_used = (pend[-1:] // bm).astype(I32)
    buf = _dispatch(jnp.stack([pend, padded, counts]).astype(I32), dest3, xn, n_blocks * bm, tp=tile,
                    s=d // LANES, bm=bm)
    yb = _experts(blk_expert, n_used, buf, w_gate, w_up, w_down, layer=layer, bm=bm)
    nrt = n // rtile
    run_start = pstart[None, :] + base.reshape(nrt, ne, LANES)[:, :, 0].astype(I32)
    pieces = (tcnt.reshape(nrt, ne, LANES)[:, :, 0].astype(I32) + RUN_CHUNK - 1) // RUN_CHUNK
    stage = (jnp.cumsum(pieces, axis=1) - pieces) * RUN_CHUNK
    tab = jnp.stack([run_start, pieces, stage], axis=1)
    by_tile = lambda a: a[:TOP_K].reshape(TOP_K, nrt, rtile).transpose(1, 0, 2)
    return _combine(tab, by_tile(p8), by_tile(g8), x1, yb, tq=rtile)


def _router_weights(wg, bg, we, be):
    d, groups = wg.shape
    epg = we.shape[2]
    assert groups <= SUBLANES and epg == SUBLANES
    pad = jnp.zeros((SUBLANES - groups, d), F32)
    wr = jnp.concatenate([wg.T, pad, we.transpose(0, 2, 1).reshape(groups * epg, d)], axis=0)
    rb = jnp.concatenate([bg, jnp.zeros((SUBLANES - groups,), F32), be.reshape(-1)])[:, None]
    hi = wr.astype(BF16)
    lo = (wr - hi.astype(F32)).astype(BF16)
    return hi, lo, rb, groups, epg


def kernel(x_prompt, x_sample, cache_attn_k, cache_attn_v, state_hgrn, rel_bias, norm_mix, norm_ffn,
           w_in_even, w_out_even, q_norm_gain, k_norm_gain, lam_q1, lam_k1, lam_q2, lam_k2, da_out_gain,
           hgrn_lb_logits, hgrn_out_gain, w_in_odd, sgu_v_gain, sgu_w, sgu_b, w_out_odd,
           router_group_w, router_group_b, router_expert_w, router_expert_b,
           expert_w_gate, expert_w_up, expert_w_down):
    bp, tp, d = x_prompt.shape
    bs, ts, _ = x_sample.shape
    depth = norm_mix.shape[0]
    _, _, past, da_heads, _, da_dh = cache_attn_k.shape
    da_dv = cache_attn_v.shape[-1]
    _, _, hg_heads, hg_dk, hg_dv = state_hgrn.shape
    width = da_heads * da_dv
    assert width == da_heads * 2 * da_dh == hg_heads * hg_dk == hg_heads * hg_dv
    assert da_dv == LANES and hg_dk == LANES and hg_dv == LANES

    lb_all = jnp.cumsum(jax.nn.softmax(hgrn_lb_logits.astype(F32), axis=0), axis=0)
    gid = jnp.arange(width) // da_dh
    pm = jnp.where(gid[:, None] == gid[None, :], 1.0 / da_dh, 0.0).astype(BF16)

    xs = {"p": x_prompt.reshape(bp * tp, d), "s": x_sample.reshape(bs * ts, d)}
    dims = {"p": (bp, tp), "s": (bs, ts)}
    outs = {"p": {}, "s": {}}
    kp_l, vp_l, ks_l, vs_l, sp_l, ss_l, sgu_l = [], [], [], [], [], [], []

    for layer in range(depth):
        j = layer // 2
        wrh, wrl, rbias, groups, epg = _router_weights(
            router_group_w[layer], router_group_b[layer], router_expert_w[layer], router_expert_b[layer])
        g_mix = norm_mix[layer][None, :]
        g_ffn = norm_ffn[layer][None, :]
        if layer % 2 == 0:
            lam_init = 0.8 - 0.6 * math.exp(-0.3 * layer)
            w_in_bf = w_in_even[j].astype(BF16)
            w_out_bf = w_out_even[j].astype(BF16)
            reps = width // da_dh
            qg = jnp.tile(q_norm_gain[j], reps)[None, :]
            kg = jnp.tile(k_norm_gain[j], reps)[None, :]
            lam4 = jnp.stack([lam_q1[j], lam_k1[j], lam_q2[j], lam_k2[j]])
            sub_gain = da_out_gain[j][None, :]
            hg_gain = hgrn_out_gain[j][None, :]
            lb = lb_all[j][None, :]
            for key in ("p", "s"):
                b, t = dims[key]
                x = xs[key]
                q, k, v, qh, kb, lf, ih, gs, k5, v4 = _in_even(
                    x, g_mix, w_in_bf, pm, qg, kg, lb, width=width, q_scale=da_dh ** -0.5 * LOG2E,
                    tm=min(512, b * t), batch=b, seq=t, heads=da_heads, dh=da_dh)
                if key == "p":
                    oa = _attn_prompt(q, k, v, rel_bias, lam4, sub_gain, batch=b, seq=t, heads=da_heads,
                                      dh=da_dh, lam_init=lam_init, qt=min(256, t))
                    s0 = jnp.zeros((b, hg_heads, hg_dk, hg_dv), F32)
                    ob, s_new = _hgrn(qh, kb, lf, ih, s0, hg_gain, batch=b, seq=t, heads=hg_heads,
                                      dk=hg_dk, dv=hg_dv, tb=min(512, t))
                    kp_l.append(k5)
                    vp_l.append(v4)
                    sp_l.append(s_new)
                else:
                    ck = cache_attn_k[j].reshape(b, past, width)
                    cv = cache_attn_v[j].reshape(b, past, width)
                    oa = _attn_sample(q, k, v, ck, cv, rel_bias, lam4, sub_gain, batch=b, t=t,
                                      heads=da_heads, dh=da_dh, lam_init=lam_init)
                    ob, s_new = _hgrn(qh, kb, lf, ih, state_hgrn[j], hg_gain, batch=b, seq=t,
                                      heads=hg_heads, dk=hg_dk, dv=hg_dv, tb=t)
                    ks_l.append(k5)
                    vs_l.append(v4)
                    ss_l.append(s_new)
                outs[key] = _out_even(oa, ob, gs, x, w_out_bf, g_ffn, wrh, wrl, rbias, tm=min(512, b * t))
        else:
            w_in_bf = w_in_odd[j].astype(BF16)
            w_out_bf = w_out_odd[j].astype(BF16)
            v_gain = sgu_v_gain[j][None, :]
            for key in ("p", "s"):
                b, t = dims[key]
                l = min(SGU_CHUNK, t)
                res = _odd_mixer(xs[key], g_mix, w_in_bf, v_gain, sgu_w[j][:, :l, :l], sgu_b[j][:, :l].T,
                                 w_out_bf, g_ffn, wrh, wrl, rbias, l=l, tm=min(512, b * t), emit_v=(key == "s"))
                outs[key] = res[:3]
                if key == "s":
                    sgu_l.append(res[3].reshape(b, t, -1))
        for key in ("p", "s"):
            x1, xn, lgt = outs[key]
            xs[key] = _moe(x1, xn, lgt, expert_w_gate, expert_w_up, expert_w_down, layer=layer,
                           groups=groups, epg=epg)

    return (xs["p"].reshape(bp, tp, d), xs["s"].reshape(bs, ts, d), jnp.stack(kp_l), jnp.stack(vp_l),
            jnp.stack(ks_l), jnp.stack(vs_l), jnp.stack(sp_l), jnp.stack(ss_l), jnp.stack(sgu_l))
```

```python
import functools
import math

import jax
import jax.numpy as jnp
from jax import lax
from jax.experimental import pallas as pl
from jax.experimental.pallas import tpu as pltpu

F32 = jnp.float32
BF16 = jnp.bfloat16
I32 = jnp.int32

EPS = 1e-6
LOG2E = math.log2(math.e)
CHUNK = 64
SGU_CHUNK = 128
REL_BUCKETS = 32
REL_MAX_DIST = 128
TOP_K = 2
MOE_BLOCK = 512
MOE_BLOCK_SMALL = 128
RUN_CHUNK = 16
HG_SUB = 16

LANES = 128
SUBLANES = 8
VMEM_LIMIT = 56 * 1024 * 1024

NT_DIMS = (((1,), (1,)), ((), ()))
TN_DIMS = (((0,), (0,)), ((), ()))


def _params(*sem):
    return pltpu.CompilerParams(dimension_semantics=sem, vmem_limit_bytes=VMEM_LIMIT)


def _const_spec(shape):
    nd = len(shape)
    return pl.BlockSpec(shape, lambda *_: (0,) * nd, pipeline_mode=pl.Buffered(1))


def _sigmoid(x):
    return 1.0 / (1.0 + jnp.exp(-x))


def _rms(x, g):
    return x * lax.rsqrt(jnp.mean(x * x, axis=-1, keepdims=True) + EPS) * g


def _dot(a, b):
    return jnp.dot(a, b, preferred_element_type=F32)


def rel_bucket(rel):
    half = REL_BUCKETS // 2
    max_exact = half // 2
    ret = (rel > 0).astype(I32) * half
    n = jnp.abs(rel)
    nf = jnp.maximum(n, 1).astype(F32)
    large = max_exact + (jnp.log(nf / max_exact) / math.log(REL_MAX_DIST / max_exact)
                         * (half - max_exact)).astype(I32)
    large = jnp.minimum(large, half - 1)
    return ret + jnp.where(n < max_exact, n, large)


def _in_even_kernel(x_ref, g_ref, w_ref, pm_ref, qg_ref, kg_ref, lb_ref,
                    q_ref, k_ref, v_ref, qh_ref, kb_ref, lf_ref, ih_ref, gs_ref, k5_ref, v4_ref, *, width,
                    q_scale):
    xn = _rms(x_ref[...], g_ref[...]).astype(BF16)

    def proj(c):
        return _dot(xn, w_ref[:, c * width:(c + 1) * width])

    def group_norm(y, gain):
        ms = _dot((y * y).astype(BF16), pm_ref[...])
        return y * lax.rsqrt(ms + EPS) * gain

    q_ref[...] = group_norm(proj(0), qg_ref[...]) * q_scale
    kn = group_norm(proj(1), kg_ref[...])
    k_ref[...] = kn
    bb, ts, heads, _, dh = k5_ref.shape
    for h in range(heads):
        for c in range(2):
            piece = kn[:, (2 * h + c) * dh:(2 * h + c + 1) * dh]
            k5_ref[:, :, h, c, :] = piece.reshape(bb, ts, dh)
    vv = proj(2)
    v_ref[...] = vv
    for h in range(heads):
        v4_ref[:, :, h, :] = vv[:, h * 2 * dh:(h + 1) * 2 * dh].reshape(bb, ts, 2 * dh)
    yq = proj(3)
    qh_ref[...] = yq * _sigmoid(yq)
    zf = proj(4)
    lb = lb_ref[...]
    lf_ref[...] = jnp.log(lb + (1.0 - lb) * _sigmoid(zf))
    kb_ref[...] = (1.0 - lb) * _sigmoid(-zf)
    ih_ref[...] = proj(5)
    yg = proj(6)
    gs_ref[...] = yg * _sigmoid(yg)


def _in_even(x, g_mix, w_bf, pm, qg, kg, lb, *, width, q_scale, tm, batch, seq, heads, dh):
    n, d = x.shape
    assert n % tm == 0 and (tm % seq == 0 or seq % tm == 0)
    row = lambda i: (i, 0)
    out = jax.ShapeDtypeStruct((n, width), F32)
    bb, ts = max(tm // seq, 1), min(tm, seq)
    per = seq // ts
    k5_spec = pl.BlockSpec((bb, ts, heads, 2, dh), lambda i: (i // per, i % per, 0, 0, 0))
    v4_spec = pl.BlockSpec((bb, ts, heads, 2 * dh), lambda i: (i // per, i % per, 0, 0))
    return pl.pallas_call(
        functools.partial(_in_even_kernel, width=width, q_scale=q_scale),
        grid=(n // tm,),
        in_specs=[pl.BlockSpec((tm, d), row), _const_spec((1, d)), _const_spec(w_bf.shape),
                  _const_spec(pm.shape), _const_spec((1, width)), _const_spec((1, width)),
                  _const_spec((1, width))],
        out_specs=[pl.BlockSpec((tm, width), row)] * 8 + [k5_spec, v4_spec],
        out_shape=[out] * 8 + [jax.ShapeDtypeStruct((batch, seq, heads, 2, dh), F32),
                               jax.ShapeDtypeStruct((batch, seq, heads, 2 * dh), F32)],
        compiler_params=_params("arbitrary"),
        name="in_even",
    )(x, g_mix, w_bf, pm, qg, kg, lb)


def _bias_from_buckets(bk, rb_ref, h):
    b = jnp.zeros(bk.shape, F32)
    for u in range(REL_BUCKETS):
        b = jnp.where(bk == u, rb_ref[u, h], b)
    return jnp.where(bk < 0, -jnp.inf, b)


def _lam(lam_ref, lam_init):
    r = lam_ref[...]
    s1 = jnp.sum(r[0:1] * r[1:2], axis=1, keepdims=True)
    s2 = jnp.sum(r[2:3] * r[3:4], axis=1, keepdims=True)
    return jnp.exp(s1) - jnp.exp(s2) + lam_init


def _split_components(q, dh):
    lane = lax.broadcasted_iota(I32, q.shape, 1)
    q0 = jnp.where(lane < dh, q, 0.0)
    q1 = jnp.where(lane >= dh, q, 0.0)
    return jnp.concatenate([q0, q1], axis=0).astype(BF16)


def _attn_prompt_kernel(rb_ref, far_ref, lam_ref, bk_ref, sg_ref, q_ref, k_ref, v_ref, o_ref, bias_s, *,
                        qt, dh, lam_init):
    h = pl.program_id(0)
    t = q_ref.shape[0]
    kb = k_ref[...].astype(BF16)
    vt = v_ref[...].T.astype(BF16)
    qtr = q_ref[...].T
    sub = lax.broadcasted_iota(I32, (2 * dh, qt), 0)

    @pl.when(pl.program_id(1) == 0)
    def _():
        for d in range(2):
            b = _bias_from_buckets(bk_ref[d], rb_ref, h) * LOG2E
            bias_s[d] = jnp.concatenate([b, b], axis=1)

    bias = [bias_s[0], bias_s[1]]
    far = rb_ref[far_ref[0], h] * LOG2E
    lam = _lam(lam_ref, lam_init)
    gain = sg_ref[...] * (1.0 - lam_init)
    for i in range(t // qt):
        qi = qtr[:, i * qt:(i + 1) * qt]
        qz = jnp.concatenate([jnp.where(sub < dh, qi, 0.0), jnp.where(sub >= dh, qi, 0.0)],
                             axis=1).astype(BF16)
        n = (i + 1) * qt
        parts = [(n - qt, n)]
        s = [_dot(kb[n - qt:n], qz) + bias[0]]
        shift = [0.0]
        if i >= 1:
            parts.append((n - 2 * qt, n - qt))
            s.append(_dot(kb[n - 2 * qt:n - qt], qz) + bias[1])
            shift.append(0.0)
        if i >= 2:
            parts.append((0, n - 2 * qt))
            s.append(_dot(kb[:n - 2 * qt], qz))
            shift.append(far)
        m = functools.reduce(jnp.maximum, [jnp.max(x, axis=0, keepdims=True) + c for x, c in zip(s, shift)])
        p = [jnp.exp2(x - (m - c)) for x, c in zip(s, shift)]
        l = functools.reduce(lambda a, b: a + b, [jnp.sum(x, axis=0, keepdims=True) for x in p])
        acc = functools.reduce(lambda a, b: a + b,
                               [_dot(vt[:, lo:hi], x.astype(BF16)) for (lo, hi), x in zip(parts, p)])
        o = acc / l
        out = o[:, :qt] - lam * o[:, qt:]
        out = out * lax.rsqrt(jnp.mean(out * out, axis=0, keepdims=True) + EPS) * gain
        o_ref[i * qt:(i + 1) * qt, :] = out.T


def _attn_prompt(q, k, v, rel_bias, lam4, sub_gain, *, batch, seq, heads, dh, lam_init, qt):
    n, w = q.shape
    dv = w // heads
    assert dv == 2 * dh and seq % qt == 0 and qt % CHUNK == 0
    kj = jnp.arange(qt, dtype=I32)[:, None]
    qi = jnp.arange(qt, dtype=I32)[None, :]
    bk0 = jnp.where((kj // CHUNK) <= (qi // CHUNK), rel_bucket(kj - qi), -1)
    bk1 = rel_bucket(kj - qi - qt)
    bk = jnp.stack([bk0, bk1]).astype(I32)
    assert qt + 1 >= REL_MAX_DIST
    far = rel_bucket(jnp.full((1,), -(qt + 1), I32))
    smem = pl.BlockSpec(memory_space=pltpu.SMEM)
    seq_blk = pl.BlockSpec((seq, dv), lambda h, b: (b, h))
    return pl.pallas_call(
        functools.partial(_attn_prompt_kernel, qt=qt, dh=dh, lam_init=lam_init),
        grid=(heads, batch),
        in_specs=[smem, smem, _const_spec(lam4.shape), _const_spec(bk.shape), _const_spec((dv, 1)),
                  seq_blk, seq_blk, seq_blk],
        out_specs=seq_blk,
        out_shape=jax.ShapeDtypeStruct((n, w), F32),
        scratch_shapes=[pltpu.VMEM((2, qt, 2 * qt), F32)],
        compiler_params=_params("arbitrary", "arbitrary"),
        name="attn_prompt",
    )(rel_bias, far, lam4, bk, sub_gain.T, q, k, v)


def _attn_sample_kernel(rb_ref, lam_ref, bkc_ref, bkn_ref, sg_ref, q_ref, kc_ref, vc_ref, kn_ref, vn_ref,
                        o_ref, bc_s, bn_s, *, t, dh, lam_init):
    h = pl.program_id(0)
    qz = _split_components(q_ref[...], dh)

    @pl.when(pl.program_id(1) == 0)
    def _():
        bc = _bias_from_buckets(bkc_ref[...], rb_ref, h) * LOG2E
        bn = _bias_from_buckets(bkn_ref[...], rb_ref, h) * LOG2E
        bc_s[...] = jnp.concatenate([bc, bc], axis=0)
        bn_s[...] = jnp.concatenate([bn, bn], axis=0)

    sc = lax.dot_general(qz, kc_ref[0].astype(BF16), NT_DIMS, preferred_element_type=F32)
    sn = lax.dot_general(qz, kn_ref[...].astype(BF16), NT_DIMS, preferred_element_type=F32)
    sc = sc + bc_s[...]
    sn = sn + bn_s[...]
    m = jnp.maximum(jnp.max(sc, axis=1, keepdims=True), jnp.max(sn, axis=1, keepdims=True))
    pc = jnp.exp2(sc - m)
    pn = jnp.exp2(sn - m)
    l = jnp.sum(pc, axis=1, keepdims=True) + jnp.sum(pn, axis=1, keepdims=True)
    acc = _dot(pc.astype(BF16), vc_ref[0].astype(BF16)) + _dot(pn.astype(BF16), vn_ref[...].astype(BF16))
    o = acc / l
    out = o[:t] - _lam(lam_ref, lam_init) * o[t:]
    o_ref[...] = _rms(out, sg_ref[...]) * (1.0 - lam_init)


def _attn_sample(q, k_new, v_new, cache_k, cache_v, rel_bias, lam4, sub_gain, *, batch, t, heads, dh,
                 lam_init):
    n, w = q.shape
    dv = w // heads
    past = cache_k.shape[1]
    assert past % CHUNK == 0 and t <= CHUNK
    qpos = past + jnp.arange(t, dtype=I32)[:, None]
    bkc = rel_bucket(jnp.arange(past, dtype=I32)[None, :] - qpos).astype(I32)
    bkn = rel_bucket(past + jnp.arange(t, dtype=I32)[None, :] - qpos).astype(I32)
    smem = pl.BlockSpec(memory_space=pltpu.SMEM)
    new = pl.BlockSpec((t, dv), lambda h, b: (b, h))
    old = pl.BlockSpec((1, past, dv), lambda h, b: (b, 0, h))
    return pl.pallas_call(
        functools.partial(_attn_sample_kernel, t=t, dh=dh, lam_init=lam_init),
        grid=(heads, batch),
        in_specs=[smem, _const_spec(lam4.shape), _const_spec(bkc.shape), _const_spec(bkn.shape),
                  _const_spec((1, dv)), new, old, old, new, new],
        out_specs=new,
        out_shape=jax.ShapeDtypeStruct((n, w), F32),
        scratch_shapes=[pltpu.VMEM((2 * t, past), F32), pltpu.VMEM((2 * t, t), F32)],
        compiler_params=_params("arbitrary", "arbitrary"),
        name="attn_sample",
    )(rel_bias, lam4, bkc, bkn, sub_gain, q, cache_k, cache_v, k_new, v_new)


def _cumsum_rows(x):
    c = x.shape[0]
    row = lax.broadcasted_iota(I32, x.shape, 0)
    s = 1
    while s < c:
        x = x + jnp.where(row >= s, pltpu.roll(x, s, axis=0), 0.0)
        s *= 2
    return x


def _hgrn_kernel(qh_ref, kb_ref, lf_ref, ih_ref, s0_ref, hg_ref, ob_ref, sf_ref, st_s, *,
                 heads, dk, dv, c, nsb):
    t = pl.program_id(1)

    @pl.when(t == 0)
    def _():
        for h in range(heads):
            st_s[h] = s0_ref[0, h].T

    tb = qh_ref.shape[0]
    row = lax.broadcasted_iota(I32, (c, c), 0)
    col = lax.broadcasted_iota(I32, (c, c), 1)
    causal = col <= row

    def chunk(ci, carry):
        r0 = pl.multiple_of(ci * c, c)
        for h in range(heads):
            rows = pl.ds(r0, c)
            q = qh_ref[rows, h * dk:(h + 1) * dk]
            k = kb_ref[rows, h * dk:(h + 1) * dk]
            v = ih_ref[rows, h * dv:(h + 1) * dv]
            b = _cumsum_rows(lf_ref[rows, h * dk:(h + 1) * dk])
            bl = b[c - 1:c]
            st = st_s[h]
            inter = lax.dot_general((q * jnp.exp(b)).astype(BF16), st.astype(BF16), NT_DIMS,
                                    preferred_element_type=F32)
            qs, ks = [], []
            for j in range(nsb):
                ref = b[j * HG_SUB + HG_SUB // 2:j * HG_SUB + HG_SUB // 2 + 1]
                qs.append(q * jnp.exp(b - ref))
                sub = slice(j * HG_SUB, (j + 1) * HG_SUB)
                ks.append(k[sub] * jnp.exp(ref - b[sub]))
            a_full = lax.dot_general(jnp.concatenate(qs, axis=0).astype(BF16),
                                     jnp.concatenate(ks, axis=0).astype(BF16), NT_DIMS,
                                     preferred_element_type=F32)
            att = jnp.zeros((c, c), F32)
            for j in range(nsb):
                att = jnp.where(col >= j * HG_SUB, a_full[j * c:(j + 1) * c], att)
            att = jnp.where(causal, att, 0.0)
            out = inter + _dot(att.astype(BF16), v.astype(BF16))
            ob_ref[rows, h * dv:(h + 1) * dv] = _rms(out, hg_ref[...])
            kdec = (k * jnp.exp(bl - b)).astype(BF16)
            st_s[h] = jnp.exp(bl) * st + lax.dot_general(v.astype(BF16), kdec, TN_DIMS,
                                                         preferred_element_type=F32)
        return carry

    lax.fori_loop(0, tb // c, chunk, 0, unroll=min(4, tb // c))

    @pl.when(t == pl.num_programs(1) - 1)
    def _():
        for h in range(heads):
            sf_ref[0, h] = st_s[h].T


def _hgrn(qh, kb, lf, ih, s0, hg_gain, *, batch, seq, heads, dk, dv, tb):
    n = qh.shape[0]
    c = min(CHUNK, seq)
    assert seq % tb == 0 and tb % c == 0 and c % HG_SUB == 0
    nt = seq // tb
    blk = lambda w: pl.BlockSpec((tb, w), lambda b, t: (b * nt + t, 0))
    st = pl.BlockSpec((1, heads, dk, dv), lambda b, t: (b, 0, 0, 0))
    return pl.pallas_call(
        functools.partial(_hgrn_kernel, heads=heads, dk=dk, dv=dv, c=c, nsb=c // HG_SUB),
        grid=(batch, nt),
        in_specs=[blk(heads * dk), blk(heads * dk), blk(heads * dk), blk(heads * dv), st,
                  _const_spec((1, dv))],
        out_specs=[blk(heads * dv), st],
        out_shape=[jax.ShapeDtypeStruct((n, heads * dv), F32),
                   jax.ShapeDtypeStruct((batch, heads, dk, dv), F32)],
        scratch_shapes=[pltpu.VMEM((heads, dv, dk), F32)],
        compiler_params=_params("arbitrary", "arbitrary"),
        name="hgrn",
    )(qh, kb, lf, ih, s0, hg_gain)


def _store_token_tiles(ref, x):
    rows, d = x.shape
    s = d // LANES
    for c in range(s):
        ref[pl.ds(c, rows, stride=s), :] = x[:, c * LANES:(c + 1) * LANES]


def _load_token_tiles(ref, rows, s):
    return jnp.concatenate([ref[pl.ds(c, rows, stride=s), :] for c in range(s)], axis=1)


def _token_tile(r, s):
    return pl.ds(pl.multiple_of(r * s, s), s)


def _ffn_prologue(x1, gf_ref, wrh_ref, wrl_ref, rbias_ref, x1_ref, xn_ref, lg_ref):
    x1_ref[...] = x1
    xn = _rms(x1, gf_ref[...])
    _store_token_tiles(xn_ref, xn)
    hi = xn.astype(BF16)
    lo = (xn - hi.astype(F32)).astype(BF16)
    nt = functools.partial(lax.dot_general, dimension_numbers=NT_DIMS, preferred_element_type=F32)
    lg_ref[...] = nt(wrh_ref[...], hi) + nt(wrh_ref[...], lo) + nt(wrl_ref[...], hi) + rbias_ref[...]


def _out_even_kernel(oa_ref, ob_ref, gs_ref, x_ref, w_ref, gf_ref, wrh_ref, wrl_ref, rbias_ref,
                     x1_ref, xn_ref, lg_ref):
    o = jnp.concatenate([oa_ref[...], ob_ref[...] * gs_ref[...]], axis=1).astype(BF16)
    x1 = x_ref[...] + _dot(o, w_ref[...])
    _ffn_prologue(x1, gf_ref, wrh_ref, wrl_ref, rbias_ref, x1_ref, xn_ref, lg_ref)


def _out_even(oa, ob, gs, x, w_bf, g_ffn, wrh, wrl, rbias, *, tm):
    n, d = x.shape
    w = oa.shape[1]
    nr = wrh.shape[0]
    row = lambda i: (i, 0)
    return pl.pallas_call(
        _out_even_kernel,
        grid=(n // tm,),
        in_specs=[pl.BlockSpec((tm, w), row), pl.BlockSpec((tm, w), row), pl.BlockSpec((tm, w), row),
                  pl.BlockSpec((tm, d), row), _const_spec(w_bf.shape), _const_spec((1, d)),
                  _const_spec(wrh.shape), _const_spec(wrl.shape), _const_spec(rbias.shape)],
        out_specs=[pl.BlockSpec((tm, d), row), pl.BlockSpec((tm * (d // LANES), LANES), row),
                   pl.BlockSpec((nr, tm), lambda i: (0, i))],
        out_shape=[jax.ShapeDtypeStruct((n, d), F32), jax.ShapeDtypeStruct((n * (d // LANES), LANES), F32),
                   jax.ShapeDtypeStruct((nr, n), F32)],
        compiler_params=_params("arbitrary"),
        name="out_even",
    )(oa, ob, gs, x, w_bf, g_ffn, wrh, wrl, rbias)


def _gelu(x):
    return 0.5 * x * (1.0 + jnp.tanh(math.sqrt(2.0 / math.pi) * (x + 0.044715 * (x * x * x))))


def _odd_kernel(x_ref, gm_ref, win_ref, vg_ref, wsp_ref, bsp_ref, wout_ref, gf_ref, wrh_ref, wrl_ref,
                rbias_ref, x1_ref, xn_ref, lg_ref, *rest, half, groups, l, emit_v):
    if emit_v:
        vn_ref, u_s, s_s = rest
    else:
        vn_ref = None
        u_s, vn_s, s_s = rest
    tm = x_ref.shape[0]
    x = x_ref[...]
    xn = _rms(x, gm_ref[...]).astype(BF16)
    cw = 512
    vbuf = vn_ref if emit_v else vn_s
    for cidx in range(half // cw):
        u_s[:, cidx * cw:(cidx + 1) * cw] = _gelu(_dot(xn, win_ref[:, cidx * cw:(cidx + 1) * cw]))
        vbuf[:, cidx * cw:(cidx + 1) * cw] = _gelu(
            _dot(xn, win_ref[:, half + cidx * cw:half + (cidx + 1) * cw]))
    vbuf[...] = _rms(vbuf[...], vg_ref[...])
    gw = half // groups
    row = lax.broadcasted_iota(I32, (l, l), 0)
    col = lax.broadcasted_iota(I32, (l, l), 1)
    for g in range(groups):
        wg = jnp.where(col <= row, wsp_ref[g], 0.0).astype(BF16)
        bg = bsp_ref[:, g:g + 1]
        for ci in range(tm // l):
            vv = vbuf[ci * l:(ci + 1) * l, g * gw:(g + 1) * gw].astype(BF16)
            s_s[ci * l:(ci + 1) * l, g * gw:(g + 1) * gw] = _dot(wg, vv) + bg
    y = _dot((u_s[...] * s_s[...]).astype(BF16), wout_ref[...])
    _ffn_prologue(x + y, gf_ref, wrh_ref, wrl_ref, rbias_ref, x1_ref, xn_ref, lg_ref)


def _odd_mixer(x, g_mix, win_bf, v_gain, wsp, bsp_t, wout_bf, g_ffn, wrh, wrl, rbias, *, l, tm, emit_v):
    n, d = x.shape
    half = wout_bf.shape[0]
    groups = wsp.shape[0]
    nr = wrh.shape[0]
    assert n % tm == 0 and tm % l == 0
    row = lambda i: (i, 0)
    out_specs = [pl.BlockSpec((tm, d), row), pl.BlockSpec((tm * (d // LANES), LANES), row),
                 pl.BlockSpec((nr, tm), lambda i: (0, i))]
    out_shape = [jax.ShapeDtypeStruct((n, d), F32), jax.ShapeDtypeStruct((n * (d // LANES), LANES), F32),
                 jax.ShapeDtypeStruct((nr, n), F32)]
    scratch = [pltpu.VMEM((tm, half), F32)]
    if emit_v:
        out_specs.append(pl.BlockSpec((tm, half), row))
        out_shape.append(jax.ShapeDtypeStruct((n, half), F32))
    else:
        scratch.append(pltpu.VMEM((tm, half), F32))
    scratch.append(pltpu.VMEM((tm, half), F32))
    return pl.pallas_call(
        functools.partial(_odd_kernel, half=half, groups=groups, l=l, emit_v=emit_v),
        grid=(n // tm,),
        in_specs=[pl.BlockSpec((tm, d), row), _const_spec((1, d)), _const_spec(win_bf.shape),
                  _const_spec((1, half)), _const_spec(wsp.shape), _const_spec(bsp_t.shape),
                  _const_spec(wout_bf.shape), _const_spec((1, d)), _const_spec(wrh.shape),
                  _const_spec(wrl.shape), _const_spec(rbias.shape)],
        out_specs=out_specs,
        out_shape=out_shape,
        scratch_shapes=scratch,
        compiler_params=_params("arbitrary"),
        name="odd_mixer",
    )(x, g_mix, win_bf, v_gain, wsp, bsp_t, wout_bf, g_ffn, wrh, wrl, rbias)


def _route_kernel(lg_ref, tri_ref, e_ref, g_ref, r_ref, p_ref, cnt_ref, base_ref, tcnt_ref, run_s, *,
                  groups, epg):
    i = pl.program_id(0)

    @pl.when(i == 0)
    def _():
        run_s[...] = jnp.zeros(run_s.shape, F32)

    lg = lg_ref[...]
    tr = lg.shape[1]
    gl = [lg[g:g + 1] for g in range(groups)]
    m = functools.reduce(jnp.maximum, gl)
    grp = jnp.full((1, tr), groups - 1, I32)
    for g in range(groups - 2, -1, -1):
        grp = jnp.where(gl[g] == m, g, grp)
    gate_g = 1.0 / functools.reduce(lambda a, b: a + b, [jnp.exp(x - m) for x in gl])
    sel = lg[SUBLANES + (groups - 1) * epg:SUBLANES + groups * epg]
    for g in range(groups - 2, -1, -1):
        sel = jnp.where(grp == g, lg[SUBLANES + g * epg:SUBLANES + (g + 1) * epg], sel)
    sub = lax.broadcasted_iota(I32, sel.shape, 0)
    v1 = jnp.max(sel, axis=0, keepdims=True)
    i1 = jnp.min(jnp.where(sel == v1, sub, epg), axis=0, keepdims=True)
    sel2 = jnp.where(sub == i1, -jnp.inf, sel)
    v2 = jnp.max(sel2, axis=0, keepdims=True)
    i2 = jnp.min(jnp.where(sel2 == v2, sub, epg), axis=0, keepdims=True)
    tt = jnp.exp(v2 - v1)
    g1 = gate_g / (1.0 + tt)
    g2 = gate_g * tt / (1.0 + tt)
    e1 = grp * epg + i1
    e2 = grp * epg + i2
    ne = groups * epg
    eidx = lax.broadcasted_iota(I32, (ne, tr), 0)
    oh1 = eidx == e1
    oh2 = eidx == e2
    cnt = jnp.where(oh1, 1.0, 0.0) + jnp.where(oh2, 1.0, 0.0)
    local = _dot(cnt.astype(BF16), tri_ref[...])
    before = run_s[:, 0:1] + local
    r1 = jnp.sum(jnp.where(oh1, before, 0.0), axis=0, keepdims=True)
    r2 = jnp.sum(jnp.where(oh2, before, 0.0), axis=0, keepdims=True)
    tile_cnt = jnp.broadcast_to(jnp.sum(cnt, axis=1, keepdims=True), run_s.shape)
    padded = jnp.ceil(tile_cnt * (1.0 / RUN_CHUNK)) * RUN_CHUNK
    offset = _cumsum_rows(padded) - padded
    where_local = offset[:, 0:1] + local
    p1 = jnp.sum(jnp.where(oh1, where_local, 0.0), axis=0, keepdims=True)
    p2 = jnp.sum(jnp.where(oh2, where_local, 0.0), axis=0, keepdims=True)
    base_ref[...] = run_s[...]
    tcnt_ref[...] = tile_cnt
    run_s[...] = run_s[...] + tile_cnt
    rows = lax.broadcasted_iota(I32, (SUBLANES, tr), 0)
    e_ref[...] = jnp.where(rows == 0, e1, jnp.where(rows == 1, e2, 0))
    g_ref[...] = jnp.where(rows == 0, g1, jnp.where(rows == 1, g2, 0.0))
    r_ref[...] = jnp.where(rows == 0, r1, jnp.where(rows == 1, r2, 0.0)).astype(I32)
    p_ref[...] = jnp.where(rows == 0, p1, jnp.where(rows == 1, p2, 0.0)).astype(I32)
    cnt_ref[...] = run_s[...]


def _route(lgt, *, groups, epg, tr):
    nr, n = lgt.shape
    assert n % tr == 0 and nr == SUBLANES + groups * epg
    ne = groups * epg
    tri = (jnp.arange(tr)[:, None] < jnp.arange(tr)[None, :]).astype(BF16)
    tok = pl.BlockSpec((SUBLANES, tr), lambda i: (0, i))
    per_tile = pl.BlockSpec((ne, LANES), lambda i: (i, 0))
    tile_tab = jax.ShapeDtypeStruct((n // tr * ne, LANES), F32)
    return pl.pallas_call(
        functools.partial(_route_kernel, groups=groups, epg=epg),
        grid=(n // tr,),
        in_specs=[pl.BlockSpec((nr, tr), lambda i: (0, i)), _const_spec((tr, tr))],
        out_specs=[tok, tok, tok, tok, pl.BlockSpec((ne, LANES), lambda i: (0, 0)), per_tile, per_tile],
        out_shape=[jax.ShapeDtypeStruct((SUBLANES, n), I32), jax.ShapeDtypeStruct((SUBLANES, n), F32),
                   jax.ShapeDtypeStruct((SUBLANES, n), I32), jax.ShapeDtypeStruct((SUBLANES, n), I32),
                   jax.ShapeDtypeStruct((ne, LANES), F32), tile_tab, tile_tab],
        scratch_shapes=[pltpu.VMEM((ne, LANES), F32)],
        compiler_params=_params("arbitrary"),
        name="route",
    )(lgt, tri)


ISSUE_UNROLL = 8


def _dispatch_kernel(seg_ref, dst_ref, x_ref, buf_ref, zero_s, sem, zsem, *, s, bm):
    tp = dst_ref.shape[2]
    blk = bm * s

    @pl.when(pl.program_id(0) == 0)
    def _():
        zero_s[...] = jnp.zeros(zero_s.shape, F32)

        def block_copy(b):
            return pltpu.make_async_copy(zero_s, buf_ref.at[pl.ds(pl.multiple_of(b * blk, blk), blk)], zsem)

        for e in range(seg_ref.shape[1]):
            @pl.when(seg_ref[1, e] > 0)
            def _():
                block_copy(seg_ref[0, e] // bm - 1).start()

            @pl.when(seg_ref[1, e] - seg_ref[2, e] > bm)
            def _():
                block_copy(seg_ref[0, e] // bm - 2).start()
        for e in range(seg_ref.shape[1]):
            @pl.when(seg_ref[1, e] > 0)
            def _():
                block_copy(0).wait()

            @pl.when(seg_ref[1, e] - seg_ref[2, e] > bm)
            def _():
                block_copy(0).wait()

        def tail_start(b, c):
            block_copy(b).start()
            return c

        def tail_wait(b, c):
            block_copy(b).wait()
            return c

        first_unused = seg_ref[0, seg_ref.shape[1] - 1] // bm
        n_blocks = buf_ref.shape[0] // blk
        lax.fori_loop(first_unused, n_blocks, tail_start, 0)
        lax.fori_loop(first_unused, n_blocks, tail_wait, 0)

    i = pl.program_id(0)
    base = i * tp

    def issue(r0, c):
        for u in range(ISSUE_UNROLL):
            r = r0 * ISSUE_UNROLL + u
            for kk in range(TOP_K):
                pltpu.make_async_copy(x_ref.at[_token_tile(base + r, s)],
                                      buf_ref.at[_token_tile(dst_ref[0, kk, r], s)], sem.at[i % 2]
                                      ).start(priority=kk % 2)
        return c

    lax.fori_loop(0, tp // ISSUE_UNROLL, issue, 0)

    def drain(slot):
        for kk in range(TOP_K):
            pltpu.make_async_copy(x_ref.at[pl.ds(0, tp * s)], buf_ref.at[pl.ds(0, tp * s)], sem.at[slot]).wait()

    @pl.when(i >= 1)
    def _():
        drain((i + 1) % 2)

    @pl.when(i == pl.num_programs(0) - 1)
    def _():
        drain(i % 2)


def _dispatch(seg, dest3, xn, n_slots, *, tp, s, bm):
    n = xn.shape[0] // s
    assert tp % ISSUE_UNROLL == 0
    grid_spec = pltpu.PrefetchScalarGridSpec(
        num_scalar_prefetch=1,
        grid=(n // tp,),
        in_specs=[pl.BlockSpec((1, TOP_K, tp), lambda i, sg: (i, 0, 0), memory_space=pltpu.SMEM),
                  pl.BlockSpec(memory_space=pl.ANY)],
        out_specs=pl.BlockSpec(memory_space=pl.ANY),
        scratch_shapes=[pltpu.VMEM((bm * s, LANES), F32), pltpu.SemaphoreType.DMA((2,)),
                        pltpu.SemaphoreType.DMA(())],
    )
    return pl.pallas_call(
        functools.partial(_dispatch_kernel, s=s, bm=bm),
        grid_spec=grid_spec,
        out_shape=jax.ShapeDtypeStruct((n_slots * s, LANES), F32),
        compiler_params=_params("arbitrary"),
        name="dispatch",
    )(seg, dest3, xn)


def _expert_kernel(be_ref, nu_ref, x_ref, wg_ref, wu_ref, wd_ref, o_ref, wg_s, wu_s, wd_s):
    b = pl.program_id(0)
    s = wg_s.shape[0] // LANES
    prev = be_ref[jnp.maximum(b - 1, 0)]

    @pl.when((b == 0) | (be_ref[b] != prev))
    def _():
        wg_s[...] = wg_ref[0, 0].astype(BF16)
        wu_s[...] = wu_ref[0, 0].astype(BF16)
        wd_s[...] = wd_ref[0, 0].astype(BF16)

    @pl.when(b < nu_ref[0])
    def _():
        xb = _load_token_tiles(x_ref, x_ref.shape[0] // s, s).astype(BF16)
        gate = _dot(xb, wg_s[...])
        h = gate * _sigmoid(gate) * _dot(xb, wu_s[...])
        _store_token_tiles(o_ref, _dot(h.astype(BF16), wd_s[...]))

    @pl.when(b >= nu_ref[0])
    def _():
        o_ref[...] = jnp.zeros(o_ref.shape, F32)


def _experts(blk_expert, n_used, buf, w_gate, w_up, w_down, *, layer, bm):
    d, de = w_gate.shape[2:]
    s = d // LANES
    n_blocks = buf.shape[0] // (bm * s)
    rows = pl.BlockSpec((bm * s, LANES), lambda b, be, nu: (b, 0))
    used_rows = pl.BlockSpec((bm * s, LANES), lambda b, be, nu: (jnp.minimum(b, nu[0] - 1), 0))
    grid_spec = pltpu.PrefetchScalarGridSpec(
        num_scalar_prefetch=2,
        grid=(n_blocks,),
        in_specs=[used_rows,
                  pl.BlockSpec((1, 1, d, de), lambda b, be, nu: (layer, be[b], 0, 0)),
                  pl.BlockSpec((1, 1, d, de), lambda b, be, nu: (layer, be[b], 0, 0)),
                  pl.BlockSpec((1, 1, de, d), lambda b, be, nu: (layer, be[b], 0, 0))],
        out_specs=rows,
        scratch_shapes=[pltpu.VMEM((d, de), BF16), pltpu.VMEM((d, de), BF16), pltpu.VMEM((de, d), BF16)],
    )
    return pl.pallas_call(
        _expert_kernel,
        grid_spec=grid_spec,
        out_shape=jax.ShapeDtypeStruct(buf.shape, F32),
        compiler_params=_params("arbitrary"),
        name="experts",
    )(blk_expert, n_used, buf, w_gate, w_up, w_down)


def _combine_kernel(tab_ref, nxt_ref, pos_ref, gate_ref, x_ref, yb_ref, o_ref, stg_s, tt_s, sem):
    i = pl.program_id(0)
    tq, d = x_ref.shape
    s = d // LANES
    piece = RUN_CHUNK * s
    slot = i % 2

    def run_copies(ref, dst_slot, go):
        for e in range(ref.shape[2]):
            def body(j, c):
                src = pl.multiple_of((ref[0, 0, e] + j * RUN_CHUNK) * s, s)
                dst = pl.multiple_of((ref[0, 2, e] + j * RUN_CHUNK) * s, piece)
                cp = pltpu.make_async_copy(yb_ref.at[pl.ds(src, piece)],
                                           stg_s.at[dst_slot, pl.ds(dst, piece)], sem.at[dst_slot])
                cp.start() if go else cp.wait()
                return c
            lax.fori_loop(0, ref[0, 1, e], body, 0)

    @pl.when(i == 0)
    def _():
        run_copies(tab_ref, 0, True)

    @pl.when(i + 1 < pl.num_programs(0))
    def _():
        run_copies(nxt_ref, 1 - slot, True)

    run_copies(tab_ref, slot, False)

    def assemble(r0, c):
        for u in range(ISSUE_UNROLL):
            r = r0 * ISSUE_UNROLL + u
            acc = gate_ref[0, 0, r] * stg_s[slot, _token_tile(pos_ref[0, 0, r], s), :]
            for kk in range(1, TOP_K):
                acc = acc + gate_ref[0, kk, r] * stg_s[slot, _token_tile(pos_ref[0, kk, r], s), :]
            tt_s[_token_tile(r, s), :] = acc
        return c

    lax.fori_loop(0, tq // ISSUE_UNROLL, assemble, 0)
    o_ref[...] = x_ref[...] + _load_token_tiles(tt_s, tq, s)


def _combine(tab, pos3, gates3, x1, yb, *, tq):
    n, d = x1.shape
    s = d // LANES
    nt = n // tq
    ne = tab.shape[2]
    assert tq % ISSUE_UNROLL == 0
    stage_rows = tq * TOP_K + ne * RUN_CHUNK
    smem = lambda shape, imap: pl.BlockSpec(shape, imap, memory_space=pltpu.SMEM)
    return pl.pallas_call(
        _combine_kernel,
        grid=(nt,),
        in_specs=[smem((1, 3, ne), lambda i: (i, 0, 0)),
                  smem((1, 3, ne), lambda i: (jnp.minimum(i + 1, nt - 1), 0, 0)),
                  smem((1, TOP_K, tq), lambda i: (i, 0, 0)),
                  smem((1, TOP_K, tq), lambda i: (i, 0, 0)),
                  pl.BlockSpec((tq, d), lambda i: (i, 0)),
                  pl.BlockSpec(memory_space=pl.ANY)],
        out_specs=pl.BlockSpec((tq, d), lambda i: (i, 0)),
        out_shape=jax.ShapeDtypeStruct((n, d), F32),
        scratch_shapes=[pltpu.VMEM((2, stage_rows * s, LANES), F32), pltpu.VMEM((tq * s, LANES), F32),
                        pltpu.SemaphoreType.DMA((2,))],
        compiler_params=_params("arbitrary"),
        name="combine",
    )(tab, tab, pos3, gates3, x1, yb)


def _moe(x1, xn, lgt, w_gate, w_up, w_down, *, layer, groups, epg):
    n, d = x1.shape
    ne = groups * epg
    rtile = min(512, n)
    tile = rtile
    e8, g8, r8, p8, cnt, base, tcnt = _route(lgt, groups=groups, epg=epg, tr=rtile)
    counts = cnt[:, 0].astype(I32)
    bm = MOE_BLOCK if n * TOP_K >= 2 * ne * MOE_BLOCK else MOE_BLOCK_SMALL
    padded = (counts + RUN_CHUNK + bm - 1) // bm * bm
    pend = jnp.cumsum(padded)
    pstart = pend - padded
    n_blocks = -(-(n * TOP_K + ne * RUN_CHUNK) // bm) + ne
    eids = jnp.arange(ne, dtype=I32)
    seg = jnp.sum(jnp.where(e8[:TOP_K, :, None] == eids, pstart, 0), axis=-1)
    dest = seg + r8[:TOP_K]
    dest3 = dest.reshape(TOP_K, n // tile, tile).transpose(1, 0, 2)
    blk_row = jnp.arange(n_blocks, dtype=I32)[:, None] * bm
    blk_expert = jnp.minimum(jnp.sum((pend[None, :] <= blk_row).astype(I32), axis=1), ne - 1)
    n_used = (pend[-1:] // bm).astype(I32)
    buf = _dispatch(jnp.stack([pend, padded, counts]).astype(I32), dest3, xn, n_blocks * bm, tp=tile,
                    s=d // LANES, bm=bm)
    yb = _experts(blk_expert, n_used, buf, w_gate, w_up, w_down, layer=layer, bm=bm)
    nrt = n // rtile
    run_start = pstart[None, :] + base.reshape(nrt, ne, LANES)[:, :, 0].astype(I32)
    pieces = (tcnt.reshape(nrt, ne, LANES)[:, :, 0].astype(I32) + RUN_CHUNK - 1) // RUN_CHUNK
    stage = (jnp.cumsum(pieces, axis=1) - pieces) * RUN_CHUNK
    tab = jnp.stack([run_start, pieces, stage], axis=1)
    by_tile = lambda a: a[:TOP_K].reshape(TOP_K, nrt, rtile).transpose(1, 0, 2)
    return _combine(tab, by_tile(p8), by_tile(g8), x1, yb, tq=rtile)


def _router_weights(wg, bg, we, be):
    d, groups = wg.shape
    epg = we.shape[2]
    assert groups <= SUBLANES and epg == SUBLANES
    pad = jnp.zeros((SUBLANES - groups, d), F32)
    wr = jnp.concatenate([wg.T, pad, we.transpose(0, 2, 1).reshape(groups * epg, d)], axis=0)
    rb = jnp.concatenate([bg, jnp.zeros((SUBLANES - groups,), F32), be.reshape(-1)])[:, None]
    hi = wr.astype(BF16)
    lo = (wr - hi.astype(F32)).astype(BF16)
    return hi, lo, rb, groups, epg


def kernel(x_prompt, x_sample, cache_attn_k, cache_attn_v, state_hgrn, rel_bias, norm_mix, norm_ffn,
           w_in_even, w_out_even, q_norm_gain, k_norm_gain, lam_q1, lam_k1, lam_q2, lam_k2, da_out_gain,
           hgrn_lb_logits, hgrn_out_gain, w_in_odd, sgu_v_gain, sgu_w, sgu_b, w_out_odd,
           router_group_w, router_group_b, router_expert_w, router_expert_b,
           expert_w_gate, expert_w_up, expert_w_down):
    bp, tp, d = x_prompt.shape
    bs, ts, _ = x_sample.shape
    depth = norm_mix.shape[0]
    _, _, past, da_heads, _, da_dh = cache_attn_k.shape
    da_dv = cache_attn_v.shape[-1]
    _, _, hg_heads, hg_dk, hg_dv = state_hgrn.shape
    width = da_heads * da_dv
    assert width == da_heads * 2 * da_dh == hg_heads * hg_dk == hg_heads * hg_dv
    assert da_dv == LANES and hg_dk == LANES and hg_dv == LANES

    lb_all = jnp.cumsum(jax.nn.softmax(hgrn_lb_logits.astype(F32), axis=0), axis=0)
    gid = jnp.arange(width) // da_dh
    pm = jnp.where(gid[:, None] == gid[None, :], 1.0 / da_dh, 0.0).astype(BF16)

    xs = {"p": x_prompt.reshape(bp * tp, d), "s": x_sample.reshape(bs * ts, d)}
    dims = {"p": (bp, tp), "s": (bs, ts)}
    outs = {"p": {}, "s": {}}
    kp_l, vp_l, ks_l, vs_l, sp_l, ss_l, sgu_l = [], [], [], [], [], [], []

    for layer in range(depth):
        j = layer // 2
        wrh, wrl, rbias, groups, epg = _router_weights(
            router_group_w[layer], router_group_b[layer], router_expert_w[layer], router_expert_b[layer])
        g_mix = norm_mix[layer][None, :]
        g_ffn = norm_ffn[layer][None, :]
        if layer % 2 == 0:
            lam_init = 0.8 - 0.6 * math.exp(-0.3 * layer)
            w_in_bf = w_in_even[j].astype(BF16)
            w_out_bf = w_out_even[j].astype(BF16)
            reps = width // da_dh
            qg = jnp.tile(q_norm_gain[j], reps)[None, :]
            kg = jnp.tile(k_norm_gain[j], reps)[None, :]
            lam4 = jnp.stack([lam_q1[j], lam_k1[j], lam_q2[j], lam_k2[j]])
            sub_gain = da_out_gain[j][None, :]
            hg_gain = hgrn_out_gain[j][None, :]
            lb = lb_all[j][None, :]
            for key in ("p", "s"):
                b, t = dims[key]
                x = xs[key]
                q, k, v, qh, kb, lf, ih, gs, k5, v4 = _in_even(
                    x, g_mix, w_in_bf, pm, qg, kg, lb, width=width, q_scale=da_dh ** -0.5 * LOG2E,
                    tm=min(512, b * t), batch=b, seq=t, heads=da_heads, dh=da_dh)
                if key == "p":
                    oa = _attn_prompt(q, k, v, rel_bias, lam4, sub_gain, batch=b, seq=t, heads=da_heads,
                                      dh=da_dh, lam_init=lam_init, qt=min(256, t))
                    s0 = jnp.zeros((b, hg_heads, hg_dk, hg_dv), F32)
                    ob, s_new = _hgrn(qh, kb, lf, ih, s0, hg_gain, batch=b, seq=t, heads=hg_heads,
                                      dk=hg_dk, dv=hg_dv, tb=min(512, t))
                    kp_l.append(k5)
                    vp_l.append(v4)
                    sp_l.append(s_new)
                else:
                    ck = cache_attn_k[j].reshape(b, past, width)
                    cv = cache_attn_v[j].reshape(b, past, width)
                    oa = _attn_sample(q, k, v, ck, cv, rel_bias, lam4, sub_gain, batch=b, t=t,
                                      heads=da_heads, dh=da_dh, lam_init=lam_init)
                    ob, s_new = _hgrn(qh, kb, lf, ih, state_hgrn[j], hg_gain, batch=b, seq=t,
                                      heads=hg_heads, dk=hg_dk, dv=hg_dv, tb=t)
                    ks_l.append(k5)
                    vs_l.append(v4)
                    ss_l.append(s_new)
                outs[key] = _out_even(oa, ob, gs, x, w_out_bf, g_ffn, wrh, wrl, rbias, tm=min(512, b * t))
        else:
            w_in_bf = w_in_odd[j].astype(BF16)
            w_out_bf = w_out_odd[j].astype(BF16)
            v_gain = sgu_v_gain[j][None, :]
            for key in ("p", "s"):
                b, t = dims[key]
                l = min(SGU_CHUNK, t)
                res = _odd_mixer(xs[key], g_mix, w_in_bf, v_gain, sgu_w[j][:, :l, :l], sgu_b[j][:, :l].T,
                                 w_out_bf, g_ffn, wrh, wrl, rbias, l=l, tm=min(512, b * t), emit_v=(key == "s"))
                outs[key] = res[:3]
                if key == "s":
                    sgu_l.append(res[3].reshape(b, t, -1))
        for key in ("p", "s"):
            x1, xn, lgt = outs[key]
            xs[key] = _moe(x1, xn, lgt, expert_w_gate, expert_w_up, expert_w_down, layer=layer,
                           groups=groups, epg=epg)

    return (xs["p"].reshape(bp, tp, d), xs["s"].reshape(bs, ts, d), jnp.stack(kp_l), jnp.stack(vp_l),
            jnp.stack(ks_l), jnp.stack(vs_l), jnp.stack(sp_l), jnp.stack(ss_l), jnp.stack(sgu_l))
```

```python
import functools
import math

import jax
import jax.numpy as jnp
from jax import lax
from jax.experimental import pallas as pl
from jax.experimental.pallas import tpu as pltpu

F32 = jnp.float32
BF16 = jnp.bfloat16
I32 = jnp.int32

EPS = 1e-6
LOG2E = math.log2(math.e)
CHUNK = 64
SGU_CHUNK = 128
REL_BUCKETS = 32
REL_MAX_DIST = 128
TOP_K = 2
MOE_BLOCK = 512
MOE_BLOCK_SMALL = 128
RUN_CHUNK = 16
HG_SUB = 16

LANES = 128
SUBLANES = 8
VMEM_LIMIT = 56 * 1024 * 1024

NT_DIMS = (((1,), (1,)), ((), ()))
TN_DIMS = (((0,), (0,)), ((), ()))


def _params(*sem):
    return pltpu.CompilerParams(dimension_semantics=sem, vmem_limit_bytes=VMEM_LIMIT)


def _const_spec(shape):
    nd = len(shape)
    return pl.BlockSpec(shape, lambda *_: (0,) * nd, pipeline_mode=pl.Buffered(1))


def _sigmoid(x):
    return 1.0 / (1.0 + jnp.exp(-x))


def _rms(x, g):
    return x * lax.rsqrt(jnp.mean(x * x, axis=-1, keepdims=True) + EPS) * g


def _dot(a, b):
    return jnp.dot(a, b, preferred_element_type=F32)


def rel_bucket(rel):
    half = REL_BUCKETS // 2
    max_exact = half // 2
    ret = (rel > 0).astype(I32) * half
    n = jnp.abs(rel)
    nf = jnp.maximum(n, 1).astype(F32)
    large = max_exact + (jnp.log(nf / max_exact) / math.log(REL_MAX_DIST / max_exact)
                         * (half - max_exact)).astype(I32)
    large = jnp.minimum(large, half - 1)
    return ret + jnp.where(n < max_exact, n, large)


def _in_even_kernel(x_ref, g_ref, w_ref, pm_ref, qg_ref, kg_ref, lb_ref,
                    q_ref, k_ref, v_ref, qh_ref, kb_ref, lf_ref, ih_ref, gs_ref, k5_ref, v4_ref, *, width,
                    q_scale):
    xn = _rms(x_ref[...], g_ref[...]).astype(BF16)

    def proj(c):
        return _dot(xn, w_ref[:, c * width:(c + 1) * width])

    def group_norm(y, gain):
        ms = _dot((y * y).astype(BF16), pm_ref[...])
        return y * lax.rsqrt(ms + EPS) * gain

    q_ref[...] = group_norm(proj(0), qg_ref[...]) * q_scale
    kn = group_norm(proj(1), kg_ref[...])
    k_ref[...] = kn
    bb, ts, heads, _, dh = k5_ref.shape
    for h in range(heads):
        for c in range(2):
            piece = kn[:, (2 * h + c) * dh:(2 * h + c + 1) * dh]
            k5_ref[:, :, h, c, :] = piece.reshape(bb, ts, dh)
    vv = proj(2)
    v_ref[...] = vv
    for h in range(heads):
        v4_ref[:, :, h, :] = vv[:, h * 2 * dh:(h + 1) * 2 * dh].reshape(bb, ts, 2 * dh)
    yq = proj(3)
    qh_ref[...] = yq * _sigmoid(yq)
    zf = proj(4)
    lb = lb_ref[...]
    lf_ref[...] = jnp.log(lb + (1.0 - lb) * _sigmoid(zf))
    kb_ref[...] = (1.0 - lb) * _sigmoid(-zf)
    ih_ref[...] = proj(5)
    yg = proj(6)
    gs_ref[...] = yg * _sigmoid(yg)


def _in_even(x, g_mix, w_bf, pm, qg, kg, lb, *, width, q_scale, tm, batch, seq, heads, dh):
    n, d = x.shape
    assert n % tm == 0 and (tm % seq == 0 or seq % tm == 0)
    row = lambda i: (i, 0)
    out = jax.ShapeDtypeStruct((n, width), F32)
    bb, ts = max(tm // seq, 1), min(tm, seq)
    per = seq // ts
    k5_spec = pl.BlockSpec((bb, ts, heads, 2, dh), lambda i: (i // per, i % per, 0, 0, 0))
    v4_spec = pl.BlockSpec((bb, ts, heads, 2 * dh), lambda i: (i // per, i % per, 0, 0))
    return pl.pallas_call(
        functools.partial(_in_even_kernel, width=width, q_scale=q_scale),
        grid=(n // tm,),
        in_specs=[pl.BlockSpec((tm, d), row), _const_spec((1, d)), _const_spec(w_bf.shape),
                  _const_spec(pm.shape), _const_spec((1, width)), _const_spec((1, width)),
                  _const_spec((1, width))],
        out_specs=[pl.BlockSpec((tm, width), row)] * 8 + [k5_spec, v4_spec],
        out_shape=[out] * 8 + [jax.ShapeDtypeStruct((batch, seq, heads, 2, dh), F32),
                               jax.ShapeDtypeStruct((batch, seq, heads, 2 * dh), F32)],
        compiler_params=_params("arbitrary"),
        name="in_even",
    )(x, g_mix, w_bf, pm, qg, kg, lb)


def _bias_from_buckets(bk, rb_ref, h):
    b = jnp.zeros(bk.shape, F32)
    for u in range(REL_BUCKETS):
        b = jnp.where(bk == u, rb_ref[u, h], b)
    return jnp.where(bk < 0, -jnp.inf, b)


def _lam(lam_ref, lam_init):
    r = lam_ref[...]
    s1 = jnp.sum(r[0:1] * r[1:2], axis=1, keepdims=True)
    s2 = jnp.sum(r[2:3] * r[3:4], axis=1, keepdims=True)
    return jnp.exp(s1) - jnp.exp(s2) + lam_init


def _split_components(q, dh):
    lane = lax.broadcasted_iota(I32, q.shape, 1)
    q0 = jnp.where(lane < dh, q, 0.0)
    q1 = jnp.where(lane >= dh, q, 0.0)
    return jnp.concatenate([q0, q1], axis=0).astype(BF16)


def _attn_prompt_kernel(rb_ref, far_ref, lam_ref, bk_ref, sg_ref, q_ref, k_ref, v_ref, o_ref, bias_s, *,
                        qt, dh, lam_init):
    h = pl.program_id(0)
    t = q_ref.shape[0]
    kb = k_ref[...].astype(BF16)
    vt = v_ref[...].T.astype(BF16)
    qtr = q_ref[...].T
    sub = lax.broadcasted_iota(I32, (2 * dh, qt), 0)

    @pl.when(pl.program_id(1) == 0)
    def _():
        for d in range(2):
            b = _bias_from_buckets(bk_ref[d], rb_ref, h) * LOG2E
            bias_s[d] = jnp.concatenate([b, b], axis=1)

    bias = [bias_s[0], bias_s[1]]
    far = rb_ref[far_ref[0], h] * LOG2E
    lam = _lam(lam_ref, lam_init)
    gain = sg_ref[...] * (1.0 - lam_init)
    for i in range(t // qt):
        qi = qtr[:, i * qt:(i + 1) * qt]
        qz = jnp.concatenate([jnp.where(sub < dh, qi, 0.0), jnp.where(sub >= dh, qi, 0.0)],
                             axis=1).astype(BF16)
        n = (i + 1) * qt
        parts = [(n - qt, n)]
        s = [_dot(kb[n - qt:n], qz) + bias[0]]
        shift = [0.0]
        if i >= 1:
            parts.append((n - 2 * qt, n - qt))
            s.append(_dot(kb[n - 2 * qt:n - qt], qz) + bias[1])
            shift.append(0.0)
        if i >= 2:
            parts.append((0, n - 2 * qt))
            s.append(_dot(kb[:n - 2 * qt], qz))
            shift.append(far)
        m = functools.reduce(jnp.maximum, [jnp.max(x, axis=0, keepdims=True) + c for x, c in zip(s, shift)])
        p = [jnp.exp2(x - (m - c)) for x, c in zip(s, shift)]
        l = functools.reduce(lambda a, b: a + b, [jnp.sum(x, axis=0, keepdims=True) for x in p])
        acc = functools.reduce(lambda a, b: a + b,
                               [_dot(vt[:, lo:hi], x.astype(BF16)) for (lo, hi), x in zip(parts, p)])
        o = acc / l
        out = o[:, :qt] - lam * o[:, qt:]
        out = out * lax.rsqrt(jnp.mean(out * out, axis=0, keepdims=True) + EPS) * gain
        o_ref[i * qt:(i + 1) * qt, :] = out.T


def _attn_prompt(q, k, v, rel_bias, lam4, sub_gain, *, batch, seq, heads, dh, lam_init, qt):
    n, w = q.shape
    dv = w // heads
    assert dv == 2 * dh and seq % qt == 0 and qt % CHUNK == 0
    kj = jnp.arange(qt, dtype=I32)[:, None]
    qi = jnp.arange(qt, dtype=I32)[None, :]
    bk0 = jnp.where((kj // CHUNK) <= (qi // CHUNK), rel_bucket(kj - qi), -1)
    bk1 = rel_bucket(kj - qi - qt)
    bk = jnp.stack([bk0, bk1]).astype(I32)
    assert qt + 1 >= REL_MAX_DIST
    far = rel_bucket(jnp.full((1,), -(qt + 1), I32))
    smem = pl.BlockSpec(memory_space=pltpu.SMEM)
    seq_blk = pl.BlockSpec((seq, dv), lambda h, b: (b, h))
    return pl.pallas_call(
        functools.partial(_attn_prompt_kernel, qt=qt, dh=dh, lam_init=lam_init),
        grid=(heads, batch),
        in_specs=[smem, smem, _const_spec(lam4.shape), _const_spec(bk.shape), _const_spec((dv, 1)),
                  seq_blk, seq_blk, seq_blk],
        out_specs=seq_blk,
        out_shape=jax.ShapeDtypeStruct((n, w), F32),
        scratch_shapes=[pltpu.VMEM((2, qt, 2 * qt), F32)],
        compiler_params=_params("arbitrary", "arbitrary"),
        name="attn_prompt",
    )(rel_bias, far, lam4, bk, sub_gain.T, q, k, v)


def _attn_sample_kernel(rb_ref, lam_ref, bkc_ref, bkn_ref, sg_ref, q_ref, kc_ref, vc_ref, kn_ref, vn_ref,
                        o_ref, bc_s, bn_s, *, t, dh, lam_init):
    h = pl.program_id(0)
    qz = _split_components(q_ref[...], dh)

    @pl.when(pl.program_id(1) == 0)
    def _():
        bc = _bias_from_buckets(bkc_ref[...], rb_ref, h) * LOG2E
        bn = _bias_from_buckets(bkn_ref[...], rb_ref, h) * LOG2E
        bc_s[...] = jnp.concatenate([bc, bc], axis=0)
        bn_s[...] = jnp.concatenate([bn, bn], axis=0)

    sc = lax.dot_general(qz, kc_ref[0].astype(BF16), NT_DIMS, preferred_element_type=F32)
    sn = lax.dot_general(qz, kn_ref[...].astype(BF16), NT_DIMS, preferred_element_type=F32)
    sc = sc + bc_s[...]
    sn = sn + bn_s[...]
    m = jnp.maximum(jnp.max(sc, axis=1, keepdims=True), jnp.max(sn, axis=1, keepdims=True))
    pc = jnp.exp2(sc - m)
    pn = jnp.exp2(sn - m)
    l = jnp.sum(pc, axis=1, keepdims=True) + jnp.sum(pn, axis=1, keepdims=True)
    acc = _dot(pc.astype(BF16), vc_ref[0].astype(BF16)) + _dot(pn.astype(BF16), vn_ref[...].astype(BF16))
    o = acc / l
    out = o[:t] - _lam(lam_ref, lam_init) * o[t:]
    o_ref[...] = _rms(out, sg_ref[...]) * (1.0 - lam_init)


def _attn_sample(q, k_new, v_new, cache_k, cache_v, rel_bias, lam4, sub_gain, *, batch, t, heads, dh,
                 lam_init):
    n, w = q.shape
    dv = w // heads
    past = cache_k.shape[1]
    assert past % CHUNK == 0 and t <= CHUNK
    qpos = past + jnp.arange(t, dtype=I32)[:, None]
    bkc = rel_bucket(jnp.arange(past, dtype=I32)[None, :] - qpos).astype(I32)
    bkn = rel_bucket(past + jnp.arange(t, dtype=I32)[None, :] - qpos).astype(I32)
    smem = pl.BlockSpec(memory_space=pltpu.SMEM)
    new = pl.BlockSpec((t, dv), lambda h, b: (b, h))
    old = pl.BlockSpec((1, past, dv), lambda h, b: (b, 0, h))
    return pl.pallas_call(
        functools.partial(_attn_sample_kernel, t=t, dh=dh, lam_init=lam_init),
        grid=(heads, batch),
        in_specs=[smem, _const_spec(lam4.shape), _const_spec(bkc.shape), _const_spec(bkn.shape),
                  _const_spec((1, dv)), new, old, old, new, new],
        out_specs=new,
        out_shape=jax.ShapeDtypeStruct((n, w), F32),
        scratch_shapes=[pltpu.VMEM((2 * t, past), F32), pltpu.VMEM((2 * t, t), F32)],
        compiler_params=_params("arbitrary", "arbitrary"),
        name="attn_sample",
    )(rel_bias, lam4, bkc, bkn, sub_gain, q, cache_k, cache_v, k_new, v_new)


def _cumsum_rows(x):
    c = x.shape[0]
    row = lax.broadcasted_iota(I32, x.shape, 0)
    s = 1
    while s < c:
        x = x + jnp.where(row >= s, pltpu.roll(x, s, axis=0), 0.0)
        s *= 2
    return x


def _hgrn_kernel(qh_ref, kb_ref, lf_ref, ih_ref, s0_ref, hg_ref, ob_ref, sf_ref, st_s, *,
                 heads, dk, dv, c, nsb):
    t = pl.program_id(1)

    @pl.when(t == 0)
    def _():
        for h in range(heads):
            st_s[h] = s0_ref[0, h].T

    tb = qh_ref.shape[0]
    row = lax.broadcasted_iota(I32, (c, c), 0)
    col = lax.broadcasted_iota(I32, (c, c), 1)
    causal = col <= row

    def chunk(ci, carry):
        r0 = pl.multiple_of(ci * c, c)
        for h in range(heads):
            rows = pl.ds(r0, c)
            q = qh_ref[rows, h * dk:(h + 1) * dk]
            k = kb_ref[rows, h * dk:(h + 1) * dk]
            v = ih_ref[rows, h * dv:(h + 1) * dv]
            b = _cumsum_rows(lf_ref[rows, h * dk:(h + 1) * dk])
            bl = b[c - 1:c]
            st = st_s[h]
            inter = lax.dot_general((q * jnp.exp(b)).astype(BF16), st.astype(BF16), NT_DIMS,
                                    preferred_element_type=F32)
            qs, ks = [], []
            for j in range(nsb):
                ref = b[j * HG_SUB + HG_SUB // 2:j * HG_SUB + HG_SUB // 2 + 1]
                qs.append(q * jnp.exp(b - ref))
                sub = slice(j * HG_SUB, (j + 1) * HG_SUB)
                ks.append(k[sub] * jnp.exp(ref - b[sub]))
            a_full = lax.dot_general(jnp.concatenate(qs, axis=0).astype(BF16),
                                     jnp.concatenate(ks, axis=0).astype(BF16), NT_DIMS,
                                     preferred_element_type=F32)
            att = jnp.zeros((c, c), F32)
            for j in range(nsb):
                att = jnp.where(col >= j * HG_SUB, a_full[j * c:(j + 1) * c], att)
            att = jnp.where(causal, att, 0.0)
            out = inter + _dot(att.astype(BF16), v.astype(BF16))
            ob_ref[rows, h * dv:(h + 1) * dv] = _rms(out, hg_ref[...])
            kdec = (k * jnp.exp(bl - b)).astype(BF16)
            st_s[h] = jnp.exp(bl) * st + lax.dot_general(v.astype(BF16), kdec, TN_DIMS,
                                                         preferred_element_type=F32)
        return carry

    lax.fori_loop(0, tb // c, chunk, 0, unroll=min(4, tb // c))

    @pl.when(t == pl.num_programs(1) - 1)
    def _():
        for h in range(heads):
            sf_ref[0, h] = st_s[h].T


def _hgrn(qh, kb, lf, ih, s0, hg_gain, *, batch, seq, heads, dk, dv, tb):
    n = qh.shape[0]
    c = min(CHUNK, seq)
    assert seq % tb == 0 and tb % c == 0 and c % HG_SUB == 0
    nt = seq // tb
    blk = lambda w: pl.BlockSpec((tb, w), lambda b, t: (b * nt + t, 0))
    st = pl.BlockSpec((1, heads, dk, dv), lambda b, t: (b, 0, 0, 0))
    return pl.pallas_call(
        functools.partial(_hgrn_kernel, heads=heads, dk=dk, dv=dv, c=c, nsb=c // HG_SUB),
        grid=(batch, nt),
        in_specs=[blk(heads * dk), blk(heads * dk), blk(heads * dk), blk(heads * dv), st,
                  _const_spec((1, dv))],
        out_specs=[blk(heads * dv), st],
        out_shape=[jax.ShapeDtypeStruct((n, heads * dv), F32),
                   jax.ShapeDtypeStruct((batch, heads, dk, dv), F32)],
        scratch_shapes=[pltpu.VMEM((heads, dv, dk), F32)],
        compiler_params=_params("arbitrary", "arbitrary"),
        name="hgrn",
    )(qh, kb, lf, ih, s0, hg_gain)


def _store_token_tiles(ref, x):
    rows, d = x.shape
    s = d // LANES
    for c in range(s):
        ref[pl.ds(c, rows, stride=s), :] = x[:, c * LANES:(c + 1) * LANES]


def _load_token_tiles(ref, rows, s):
    return jnp.concatenate([ref[pl.ds(c, rows, stride=s), :] for c in range(s)], axis=1)


def _token_tile(r, s):
    return pl.ds(pl.multiple_of(r * s, s), s)


def _ffn_prologue(x1, gf_ref, wrh_ref, wrl_ref, rbias_ref, x1_ref, xn_ref, lg_ref):
    x1_ref[...] = x1
    xn = _rms(x1, gf_ref[...])
    _store_token_tiles(xn_ref, xn)
    hi = xn.astype(BF16)
    lo = (xn - hi.astype(F32)).astype(BF16)
    nt = functools.partial(lax.dot_general, dimension_numbers=NT_DIMS, preferred_element_type=F32)
    lg_ref[...] = nt(wrh_ref[...], hi) + nt(wrh_ref[...], lo) + nt(wrl_ref[...], hi) + rbias_ref[...]


def _out_even_kernel(oa_ref, ob_ref, gs_ref, x_ref, w_ref, gf_ref, wrh_ref, wrl_ref, rbias_ref,
                     x1_ref, xn_ref, lg_ref):
    o = jnp.concatenate([oa_ref[...], ob_ref[...] * gs_ref[...]], axis=1).astype(BF16)
    x1 = x_ref[...] + _dot(o, w_ref[...])
    _ffn_prologue(x1, gf_ref, wrh_ref, wrl_ref, rbias_ref, x1_ref, xn_ref, lg_ref)


def _out_even(oa, ob, gs, x, w_bf, g_ffn, wrh, wrl, rbias, *, tm):
    n, d = x.shape
    w = oa.shape[1]
    nr = wrh.shape[0]
    row = lambda i: (i, 0)
    return pl.pallas_call(
        _out_even_kernel,
        grid=(n // tm,),
        in_specs=[pl.BlockSpec((tm, w), row), pl.BlockSpec((tm, w), row), pl.BlockSpec((tm, w), row),
                  pl.BlockSpec((tm, d), row), _const_spec(w_bf.shape), _const_spec((1, d)),
                  _const_spec(wrh.shape), _const_spec(wrl.shape), _const_spec(rbias.shape)],
        out_specs=[pl.BlockSpec((tm, d), row), pl.BlockSpec((tm * (d // LANES), LANES), row),
                   pl.BlockSpec((nr, tm), lambda i: (0, i))],
        out_shape=[jax.ShapeDtypeStruct((n, d), F32), jax.ShapeDtypeStruct((n * (d // LANES), LANES), F32),
                   jax.ShapeDtypeStruct((nr, n), F32)],
        compiler_params=_params("arbitrary"),
        name="out_even",
    )(oa, ob, gs, x, w_bf, g_ffn, wrh, wrl, rbias)


def _gelu(x):
    return 0.5 * x * (1.0 + jnp.tanh(math.sqrt(2.0 / math.pi) * (x + 0.044715 * (x * x * x))))


def _odd_kernel(x_ref, gm_ref, win_ref, vg_ref, wsp_ref, bsp_ref, wout_ref, gf_ref, wrh_ref, wrl_ref,
                rbias_ref, x1_ref, xn_ref, lg_ref, *rest, half, groups, l, emit_v):
    if emit_v:
        vn_ref, u_s, s_s = rest
    else:
        vn_ref = None
        u_s, vn_s, s_s = rest
    tm = x_ref.shape[0]
    x = x_ref[...]
    xn = _rms(x, gm_ref[...]).astype(BF16)
    cw = 512
    vbuf = vn_ref if emit_v else vn_s
    for cidx in range(half // cw):
        u_s[:, cidx * cw:(cidx + 1) * cw] = _gelu(_dot(xn, win_ref[:, cidx * cw:(cidx + 1) * cw]))
        vbuf[:, cidx * cw:(cidx + 1) * cw] = _gelu(
            _dot(xn, win_ref[:, half + cidx * cw:half + (cidx + 1) * cw]))
    vbuf[...] = _rms(vbuf[...], vg_ref[...])
    gw = half // groups
    row = lax.broadcasted_iota(I32, (l, l), 0)
    col = lax.broadcasted_iota(I32, (l, l), 1)
    for g in range(groups):
        wg = jnp.where(col <= row, wsp_ref[g], 0.0).astype(BF16)
        bg = bsp_ref[:, g:g + 1]
        for ci in range(tm // l):
            vv = vbuf[ci * l:(ci + 1) * l, g * gw:(g + 1) * gw].astype(BF16)
            s_s[ci * l:(ci + 1) * l, g * gw:(g + 1) * gw] = _dot(wg, vv) + bg
    y = _dot((u_s[...] * s_s[...]).astype(BF16), wout_ref[...])
    _ffn_prologue(x + y, gf_ref, wrh_ref, wrl_ref, rbias_ref, x1_ref, xn_ref, lg_ref)


def _odd_mixer(x, g_mix, win_bf, v_gain, wsp, bsp_t, wout_bf, g_ffn, wrh, wrl, rbias, *, l, tm, emit_v):
    n, d = x.shape
    half = wout_bf.shape[0]
    groups = wsp.shape[0]
    nr = wrh.shape[0]
    assert n % tm == 0 and tm % l == 0
    row = lambda i: (i, 0)
    out_specs = [pl.BlockSpec((tm, d), row), pl.BlockSpec((tm * (d // LANES), LANES), row),
                 pl.BlockSpec((nr, tm), lambda i: (0, i))]
    out_shape = [jax.ShapeDtypeStruct((n, d), F32), jax.ShapeDtypeStruct((n * (d // LANES), LANES), F32),
                 jax.ShapeDtypeStruct((nr, n), F32)]
    scratch = [pltpu.VMEM((tm, half), F32)]
    if emit_v:
        out_specs.append(pl.BlockSpec((tm, half), row))
        out_shape.append(jax.ShapeDtypeStruct((n, half), F32))
    else:
        scratch.append(pltpu.VMEM((tm, half), F32))
    scratch.append(pltpu.VMEM((tm, half), F32))
    return pl.pallas_call(
        functools.partial(_odd_kernel, half=half, groups=groups, l=l, emit_v=emit_v),
        grid=(n // tm,),
        in_specs=[pl.BlockSpec((tm, d), row), _const_spec((1, d)), _const_spec(win_bf.shape),
                  _const_spec((1, half)), _const_spec(wsp.shape), _const_spec(bsp_t.shape),
                  _const_spec(wout_bf.shape), _const_spec((1, d)), _const_spec(wrh.shape),
                  _const_spec(wrl.shape), _const_spec(rbias.shape)],
        out_specs=out_specs,
        out_shape=out_shape,
        scratch_shapes=scratch,
        compiler_params=_params("arbitrary"),
        name="odd_mixer",
    )(x, g_mix, win_bf, v_gain, wsp, bsp_t, wout_bf, g_ffn, wrh, wrl, rbias)


def _route_kernel(lg_ref, tri_ref, e_ref, g_ref, r_ref, p_ref, cnt_ref, base_ref, tcnt_ref, run_s, *,
                  groups, epg):
    i = pl.program_id(0)

    @pl.when(i == 0)
    def _():
        run_s[...] = jnp.zeros(run_s.shape, F32)

    lg = lg_ref[...]
    tr = lg.shape[1]
    gl = [lg[g:g + 1] for g in range(groups)]
    m = functools.reduce(jnp.maximum, gl)
    grp = jnp.full((1, tr), groups - 1, I32)
    for g in range(groups - 2, -1, -1):
        grp = jnp.where(gl[g] == m, g, grp)
    gate_g = 1.0 / functools.reduce(lambda a, b: a + b, [jnp.exp(x - m) for x in gl])
    sel = lg[SUBLANES + (groups - 1) * epg:SUBLANES + groups * epg]
    for g in range(groups - 2, -1, -1):
        sel = jnp.where(grp == g, lg[SUBLANES + g * epg:SUBLANES + (g + 1) * epg], sel)
    sub = lax.broadcasted_iota(I32, sel.shape, 0)
    v1 = jnp.max(sel, axis=0, keepdims=True)
    i1 = jnp.min(jnp.where(sel == v1, sub, epg), axis=0, keepdims=True)
    sel2 = jnp.where(sub == i1, -jnp.inf, sel)
    v2 = jnp.max(sel2, axis=0, keepdims=True)
    i2 = jnp.min(jnp.where(sel2 == v2, sub, epg), axis=0, keepdims=True)
    tt = jnp.exp(v2 - v1)
    g1 = gate_g / (1.0 + tt)
    g2 = gate_g * tt / (1.0 + tt)
    e1 = grp * epg + i1
    e2 = grp * epg + i2
    ne = groups * epg
    eidx = lax.broadcasted_iota(I32, (ne, tr), 0)
    oh1 = eidx == e1
    oh2 = eidx == e2
    cnt = jnp.where(oh1, 1.0, 0.0) + jnp.where(oh2, 1.0, 0.0)
    local = _dot(cnt.astype(BF16), tri_ref[...])
    before = run_s[:, 0:1] + local
    r1 = jnp.sum(jnp.where(oh1, before, 0.0), axis=0, keepdims=True)
    r2 = jnp.sum(jnp.where(oh2, before, 0.0), axis=0, keepdims=True)
    tile_cnt = jnp.broadcast_to(jnp.sum(cnt, axis=1, keepdims=True), run_s.shape)
    padded = jnp.ceil(tile_cnt * (1.0 / RUN_CHUNK)) * RUN_CHUNK
    offset = _cumsum_rows(padded) - padded
    where_local = offset[:, 0:1] + local
    p1 = jnp.sum(jnp.where(oh1, where_local, 0.0), axis=0, keepdims=True)
    p2 = jnp.sum(jnp.where(oh2, where_local, 0.0), axis=0, keepdims=True)
    base_ref[...] = run_s[...]
    tcnt_ref[...] = tile_cnt
    run_s[...] = run_s[...] + tile_cnt
    rows = lax.broadcasted_iota(I32, (SUBLANES, tr), 0)
    e_ref[...] = jnp.where(rows == 0, e1, jnp.where(rows == 1, e2, 0))
    g_ref[...] = jnp.where(rows == 0, g1, jnp.where(rows == 1, g2, 0.0))
    r_ref[...] = jnp.where(rows == 0, r1, jnp.where(rows == 1, r2, 0.0)).astype(I32)
    p_ref[...] = jnp.where(rows == 0, p1, jnp.where(rows == 1, p2, 0.0)).astype(I32)
    cnt_ref[...] = run_s[...]


def _route(lgt, *, groups, epg, tr):
    nr, n = lgt.shape
    assert n % tr == 0 and nr == SUBLANES + groups * epg
    ne = groups * epg
    tri = (jnp.arange(tr)[:, None] < jnp.arange(tr)[None, :]).astype(BF16)
    tok = pl.BlockSpec((SUBLANES, tr), lambda i: (0, i))
    per_tile = pl.BlockSpec((ne, LANES), lambda i: (i, 0))
    tile_tab = jax.ShapeDtypeStruct((n // tr * ne, LANES), F32)
    return pl.pallas_call(
        functools.partial(_route_kernel, groups=groups, epg=epg),
        grid=(n // tr,),
        in_specs=[pl.BlockSpec((nr, tr), lambda i: (0, i)), _const_spec((tr, tr))],
        out_specs=[tok, tok, tok, tok, pl.BlockSpec((ne, LANES), lambda i: (0, 0)), per_tile, per_tile],
        out_shape=[jax.ShapeDtypeStruct((SUBLANES, n), I32), jax.ShapeDtypeStruct((SUBLANES, n), F32),
                   jax.ShapeDtypeStruct((SUBLANES, n), I32), jax.ShapeDtypeStruct((SUBLANES, n), I32),
                   jax.ShapeDtypeStruct((ne, LANES), F32), tile_tab, tile_tab],
        scratch_shapes=[pltpu.VMEM((ne, LANES), F32)],
        compiler_params=_params("arbitrary"),
        name="route",
    )(lgt, tri)


ISSUE_UNROLL = 8
DISPATCH_SLOTS = 3


def _dispatch_kernel(seg_ref, dst_ref, x_ref, buf_ref, zero_s, xin_s, sem, isem, zsem, *, s, bm):
    tp = dst_ref.shape[2]
    blk = bm * s

    @pl.when(pl.program_id(0) == 0)
    def _():
        zero_s[...] = jnp.zeros(zero_s.shape, F32)

        def block_copy(b):
            return pltpu.make_async_copy(zero_s, buf_ref.at[pl.ds(pl.multiple_of(b * blk, blk), blk)], zsem)

        for e in range(seg_ref.shape[1]):
            @pl.when(seg_ref[1, e] > 0)
            def _():
                block_copy(seg_ref[0, e] // bm - 1).start()

            @pl.when(seg_ref[1, e] - seg_ref[2, e] > bm)
            def _():
                block_copy(seg_ref[0, e] // bm - 2).start()
        for e in range(seg_ref.shape[1]):
            @pl.when(seg_ref[1, e] > 0)
            def _():
                block_copy(0).wait()

            @pl.when(seg_ref[1, e] - seg_ref[2, e] > bm)
            def _():
                block_copy(0).wait()

        def tail_start(b, c):
            block_copy(b).start()
            return c

        def tail_wait(b, c):
            block_copy(b).wait()
            return c

        first_unused = seg_ref[0, seg_ref.shape[1] - 1] // bm
        n_blocks = buf_ref.shape[0] // blk
        lax.fori_loop(first_unused, n_blocks, tail_start, 0)
        lax.fori_loop(first_unused, n_blocks, tail_wait, 0)

    i = pl.program_id(0)
    last = pl.num_programs(0) - 1
    rows = tp * s

    def tile_in(t):
        return pltpu.make_async_copy(x_ref.at[pl.ds(pl.multiple_of(t * rows, rows), rows)],
                                     xin_s.at[t % DISPATCH_SLOTS], isem.at[t % DISPATCH_SLOTS])

    def rows_out_wait(t):
        for kk in range(TOP_K):
            pltpu.make_async_copy(xin_s.at[t % DISPATCH_SLOTS], buf_ref.at[pl.ds(0, rows)],
                                  sem.at[t % DISPATCH_SLOTS]).wait()

    @pl.when(i == 0)
    def _():
        tile_in(i).start()

    @pl.when(i >= DISPATCH_SLOTS - 1)
    def _():
        rows_out_wait(i - (DISPATCH_SLOTS - 1))

    @pl.when(i < last)
    def _():
        tile_in(i + 1).start()

    tile_in(i).wait()
    slot = i % DISPATCH_SLOTS

    def issue(r0, c):
        for u in range(ISSUE_UNROLL):
            r = r0 * ISSUE_UNROLL + u
            for kk in range(TOP_K):
                pltpu.make_async_copy(xin_s.at[slot, _token_tile(r, s)],
                                      buf_ref.at[_token_tile(dst_ref[0, kk, r], s)], sem.at[slot]
                                      ).start(priority=kk % 2)
        return c

    lax.fori_loop(0, tp // ISSUE_UNROLL, issue, 0)

    @pl.when(i == last)
    def _():
        for back in range(DISPATCH_SLOTS - 2, -1, -1):
            @pl.when(i >= back)
            def _():
                rows_out_wait(i - back)


def _dispatch(seg, dest3, xn, n_slots, *, tp, s, bm):
    n = xn.shape[0] // s
    assert tp % ISSUE_UNROLL == 0
    grid_spec = pltpu.PrefetchScalarGridSpec(
        num_scalar_prefetch=1,
        grid=(n // tp,),
        in_specs=[pl.BlockSpec((1, TOP_K, tp), lambda i, sg: (i, 0, 0), memory_space=pltpu.SMEM),
                  pl.BlockSpec(memory_space=pl.ANY)],
        out_specs=pl.BlockSpec(memory_space=pl.ANY),
        scratch_shapes=[pltpu.VMEM((bm * s, LANES), F32), pltpu.VMEM((DISPATCH_SLOTS, tp * s, LANES), F32),
                        pltpu.SemaphoreType.DMA((DISPATCH_SLOTS,)), pltpu.SemaphoreType.DMA((DISPATCH_SLOTS,)),
                        pltpu.SemaphoreType.DMA(())],
    )
    return pl.pallas_call(
        functools.partial(_dispatch_kernel, s=s, bm=bm),
        grid_spec=grid_spec,
        out_shape=jax.ShapeDtypeStruct((n_slots * s, LANES), F32),
        compiler_params=_params("arbitrary"),
        name="dispatch",
    )(seg, dest3, xn)


def _expert_kernel(be_ref, nu_ref, x_ref, wg_ref, wu_ref, wd_ref, o_ref, wg_s, wu_s, wd_s):
    b = pl.program_id(0)
    s = wg_s.shape[0] // LANES
    prev = be_ref[jnp.maximum(b - 1, 0)]

    @pl.when((b == 0) | (be_ref[b] != prev))
    def _():
        wg_s[...] = wg_ref[0, 0].astype(BF16)
        wu_s[...] = wu_ref[0, 0].astype(BF16)
        wd_s[...] = wd_ref[0, 0].astype(BF16)

    @pl.when(b < nu_ref[0])
    def _():
        xb = _load_token_tiles(x_ref, x_ref.shape[0] // s, s).astype(BF16)
        gate = _dot(xb, wg_s[...])
        h = gate * _sigmoid(gate) * _dot(xb, wu_s[...])
        _store_token_tiles(o_ref, _dot(h.astype(BF16), wd_s[...]))

    @pl.when(b >= nu_ref[0])
    def _():
        o_ref[...] = jnp.zeros(o_ref.shape, F32)


def _experts(blk_expert, n_used, buf, w_gate, w_up, w_down, *, layer, bm):
    d, de = w_gate.shape[2:]
    s = d // LANES
    n_blocks = buf.shape[0] // (bm * s)
    rows = pl.BlockSpec((bm * s, LANES), lambda b, be, nu: (b, 0))
    used_rows = pl.BlockSpec((bm * s, LANES), lambda b, be, nu: (jnp.minimum(b, nu[0] - 1), 0))
    grid_spec = pltpu.PrefetchScalarGridSpec(
        num_scalar_prefetch=2,
        grid=(n_blocks,),
        in_specs=[used_rows,
                  pl.BlockSpec((1, 1, d, de), lambda b, be, nu: (layer, be[b], 0, 0)),
                  pl.BlockSpec((1, 1, d, de), lambda b, be, nu: (layer, be[b], 0, 0)),
                  pl.BlockSpec((1, 1, de, d), lambda b, be, nu: (layer, be[b], 0, 0))],
        out_specs=rows,
        scratch_shapes=[pltpu.VMEM((d, de), BF16), pltpu.VMEM((d, de), BF16), pltpu.VMEM((de, d), BF16)],
    )
    return pl.pallas_call(
        _expert_kernel,
        grid_spec=grid_spec,
        out_shape=jax.ShapeDtypeStruct(buf.shape, F32),
        compiler_params=_params("arbitrary"),
        name="experts",
    )(blk_expert, n_used, buf, w_gate, w_up, w_down)


def _combine_kernel(tab_ref, nxt_ref, pos_ref, gate_ref, x_ref, yb_ref, o_ref, stg_s, tt_s, sem):
    i = pl.program_id(0)
    tq, d = x_ref.shape
    s = d // LANES
    piece = RUN_CHUNK * s
    slot = i % 2

    def run_copies(ref, dst_slot, go):
        for e in range(ref.shape[2]):
            def body(j, c):
                src = pl.multiple_of((ref[0, 0, e] + j * RUN_CHUNK) * s, s)
                dst = pl.multiple_of((ref[0, 2, e] + j * RUN_CHUNK) * s, piece)
                cp = pltpu.make_async_copy(yb_ref.at[pl.ds(src, piece)],
                                           stg_s.at[dst_slot, pl.ds(dst, piece)], sem.at[dst_slot])
                cp.start() if go else cp.wait()
                return c
            lax.fori_loop(0, ref[0, 1, e], body, 0)

    @pl.when(i == 0)
    def _():
        run_copies(tab_ref, 0, True)

    @pl.when(i + 1 < pl.num_programs(0))
    def _():
        run_copies(nxt_ref, 1 - slot, True)

    run_copies(tab_ref, slot, False)

    def assemble(r0, c):
        for u in range(ISSUE_UNROLL):
            r = r0 * ISSUE_UNROLL + u
            acc = gate_ref[0, 0, r] * stg_s[slot, _token_tile(pos_ref[0, 0, r], s), :]
            for kk in range(1, TOP_K):
                acc = acc + gate_ref[0, kk, r] * stg_s[slot, _token_tile(pos_ref[0, kk, r], s), :]
            tt_s[_token_tile(r, s), :] = acc
        return c

    lax.fori_loop(0, tq // ISSUE_UNROLL, assemble, 0)
    o_ref[...] = x_ref[...] + _load_token_tiles(tt_s, tq, s)


def _combine(tab, pos3, gates3, x1, yb, *, tq):
    n, d = x1.shape
    s = d // LANES
    nt = n // tq
    ne = tab.shape[2]
    assert tq % ISSUE_UNROLL == 0
    stage_rows = tq * TOP_K + ne * RUN_CHUNK
    smem = lambda shape, imap: pl.BlockSpec(shape, imap, memory_space=pltpu.SMEM)
    return pl.pallas_call(
        _combine_kernel,
        grid=(nt,),
        in_specs=[smem((1, 3, ne), lambda i: (i, 0, 0)),
                  smem((1, 3, ne), lambda i: (jnp.minimum(i + 1, nt - 1), 0, 0)),
                  smem((1, TOP_K, tq), lambda i: (i, 0, 0)),
                  smem((1, TOP_K, tq), lambda i: (i, 0, 0)),
                  pl.BlockSpec((tq, d), lambda i: (i, 0)),
                  pl.BlockSpec(memory_space=pl.ANY)],
        out_specs=pl.BlockSpec((tq, d), lambda i: (i, 0)),
        out_shape=jax.ShapeDtypeStruct((n, d), F32),
        scratch_shapes=[pltpu.VMEM((2, stage_rows * s, LANES), F32), pltpu.VMEM((tq * s, LANES), F32),
                        pltpu.SemaphoreType.DMA((2,))],
        compiler_params=_params("arbitrary"),
        name="combine",
    )(tab, tab, pos3, gates3, x1, yb)


def _moe(x1, xn, lgt, w_gate, w_up, w_down, *, layer, groups, epg):
    n, d = x1.shape
    ne = groups * epg
    rtile = min(512, n)
    tile = rtile
    e8, g8, r8, p8, cnt, base, tcnt = _route(lgt, groups=groups, epg=epg, tr=rtile)
    counts = cnt[:, 0].astype(I32)
    bm = MOE_BLOCK if n * TOP_K >= 2 * ne * MOE_BLOCK else MOE_BLOCK_SMALL
    padded = (counts + RUN_CHUNK + bm - 1) // bm * bm
    pend = jnp.cumsum(padded)
    pstart = pend - padded
    n_blocks = -(-(n * TOP_K + ne * RUN_CHUNK) // bm) + ne
    eids = jnp.arange(ne, dtype=I32)
    seg = jnp.sum(jnp.where(e8[:TOP_K, :, None] == eids, pstart, 0), axis=-1)
    dest = seg + r8[:TOP_K]
    dest3 = dest.reshape(TOP_K, n // tile, tile).transpose(1, 0, 2)
    blk_row = jnp.arange(n_blocks, dtype=I32)[:, None] * bm
    blk_expert = jnp.minimum(jnp.sum((pend[None, :] <= blk_row).astype(I32), axis=1), ne - 1)
    n_used = (pend[-1:] // bm).astype(I32)
    buf = _dispatch(jnp.stack([pend, padded, counts]).astype(I32), dest3, xn, n_blocks * bm, tp=tile,
                    s=d // LANES, bm=bm)
    yb = _experts(blk_expert, n_used, buf, w_gate, w_up, w_down, layer=layer, bm=bm)
    nrt = n // rtile
    run_start = pstart[None, :] + base.reshape(nrt, ne, LANES)[:, :, 0].astype(I32)
    pieces = (tcnt.reshape(nrt, ne, LANES)[:, :, 0].astype(I32) + RUN_CHUNK - 1) // RUN_CHUNK
    stage = (jnp.cumsum(pieces, axis=1) - pieces) * RUN_CHUNK
    tab = jnp.stack([run_start, pieces, stage], axis=1)
    by_tile = lambda a: a[:TOP_K].reshape(TOP_K, nrt, rtile).transpose(1, 0, 2)
    return _combine(tab, by_tile(p8), by_tile(g8), x1, yb, tq=rtile)


def _router_weights(wg, bg, we, be):
    d, groups = wg.shape
    epg = we.shape[2]
    assert groups <= SUBLANES and epg == SUBLANES
    pad = jnp.zeros((SUBLANES - groups, d), F32)
    wr = jnp.concatenate([wg.T, pad, we.transpose(0, 2, 1).reshape(groups * epg, d)], axis=0)
    rb = jnp.concatenate([bg, jnp.zeros((SUBLANES - groups,), F32), be.reshape(-1)])[:, None]
    hi = wr.astype(BF16)
    lo = (wr - hi.astype(F32)).astype(BF16)
    return hi, lo, rb, groups, epg


def kernel(x_prompt, x_sample, cache_attn_k, cache_attn_v, state_hgrn, rel_bias, norm_mix, norm_ffn,
           w_in_even, w_out_even, q_norm_gain, k_norm_gain, lam_q1, lam_k1, lam_q2, lam_k2, da_out_gain,
           hgrn_lb_logits, hgrn_out_gain, w_in_odd, sgu_v_gain, sgu_w, sgu_b, w_out_odd,
           router_group_w, router_group_b, router_expert_w, router_expert_b,
           expert_w_gate, expert_w_up, expert_w_down):
    bp, tp, d = x_prompt.shape
    bs, ts, _ = x_sample.shape
    depth = norm_mix.shape[0]
    _, _, past, da_heads, _, da_dh = cache_attn_k.shape
    da_dv = cache_attn_v.shape[-1]
    _, _, hg_heads, hg_dk, hg_dv = state_hgrn.shape
    width = da_heads * da_dv
    assert width == da_heads * 2 * da_dh == hg_heads * hg_dk == hg_heads * hg_dv
    assert da_dv == LANES and hg_dk == LANES and hg_dv == LANES

    lb_all = jnp.cumsum(jax.nn.softmax(hgrn_lb_logits.astype(F32), axis=0), axis=0)
    gid = jnp.arange(width) // da_dh
    pm = jnp.where(gid[:, None] == gid[None, :], 1.0 / da_dh, 0.0).astype(BF16)

    xs = {"p": x_prompt.reshape(bp * tp, d), "s": x_sample.reshape(bs * ts, d)}
    dims = {"p": (bp, tp), "s": (bs, ts)}
    outs = {"p": {}, "s": {}}
    kp_l, vp_l, ks_l, vs_l, sp_l, ss_l, sgu_l = [], [], [], [], [], [], []

    for layer in range(depth):
        j = layer // 2
        wrh, wrl, rbias, groups, epg = _router_weights(
            router_group_w[layer], router_group_b[layer], router_expert_w[layer], router_expert_b[layer])
        g_mix = norm_mix[layer][None, :]
        g_ffn = norm_ffn[layer][None, :]
        if layer % 2 == 0:
            lam_init = 0.8 - 0.6 * math.exp(-0.3 * layer)
            w_in_bf = w_in_even[j].astype(BF16)
            w_out_bf = w_out_even[j].astype(BF16)
            reps = width // da_dh
            qg = jnp.tile(q_norm_gain[j], reps)[None, :]
            kg = jnp.tile(k_norm_gain[j], reps)[None, :]
            lam4 = jnp.stack([lam_q1[j], lam_k1[j], lam_q2[j], lam_k2[j]])
            sub_gain = da_out_gain[j][None, :]
            hg_gain = hgrn_out_gain[j][None, :]
            lb = lb_all[j][None, :]
            for key in ("p", "s"):
                b, t = dims[key]
                x = xs[key]
                q, k, v, qh, kb, lf, ih, gs, k5, v4 = _in_even(
                    x, g_mix, w_in_bf, pm, qg, kg, lb, width=width, q_scale=da_dh ** -0.5 * LOG2E,
                    tm=min(512, b * t), batch=b, seq=t, heads=da_heads, dh=da_dh)
                if key == "p":
                    oa = _attn_prompt(q, k, v, rel_bias, lam4, sub_gain, batch=b, seq=t, heads=da_heads,
                                      dh=da_dh, lam_init=lam_init, qt=min(256, t))
                    s0 = jnp.zeros((b, hg_heads, hg_dk, hg_dv), F32)
                    ob, s_new = _hgrn(qh, kb, lf, ih, s0, hg_gain, batch=b, seq=t, heads=hg_heads,
                                      dk=hg_dk, dv=hg_dv, tb=min(512, t))
                    kp_l.append(k5)
                    vp_l.append(v4)
                    sp_l.append(s_new)
                else:
                    ck = cache_attn_k[j].reshape(b, past, width)
                    cv = cache_attn_v[j].reshape(b, past, width)
                    oa = _attn_sample(q, k, v, ck, cv, rel_bias, lam4, sub_gain, batch=b, t=t,
                                      heads=da_heads, dh=da_dh, lam_init=lam_init)
                    ob, s_new = _hgrn(qh, kb, lf, ih, state_hgrn[j], hg_gain, batch=b, seq=t,
                                      heads=hg_heads, dk=hg_dk, dv=hg_dv, tb=t)
                    ks_l.append(k5)
                    vs_l.append(v4)
                    ss_l.append(s_new)
                outs[key] = _out_even(oa, ob, gs, x, w_out_bf, g_ffn, wrh, wrl, rbias, tm=min(512, b * t))
        else:
            w_in_bf = w_in_odd[j].astype(BF16)
            w_out_bf = w_out_odd[j].astype(BF16)
            v_gain = sgu_v_gain[j][None, :]
            for key in ("p", "s"):
                b, t = dims[key]
                l = min(SGU_CHUNK, t)
                res = _odd_mixer(xs[key], g_mix, w_in_bf, v_gain, sgu_w[j][:, :l, :l], sgu_b[j][:, :l].T,
                                 w_out_bf, g_ffn, wrh, wrl, rbias, l=l, tm=min(512, b * t), emit_v=(key == "s"))
                outs[key] = res[:3]
                if key == "s":
                    sgu_l.append(res[3].reshape(b, t, -1))
        for key in ("p", "s"):
            x1, xn, lgt = outs[key]
            xs[key] = _moe(x1, xn, lgt, expert_w_gate, expert_w_up, expert_w_down, layer=layer,
                           groups=groups, epg=epg)

    return (xs["p"].reshape(bp, tp, d), xs["s"].reshape(bs, ts, d), jnp.stack(kp_l), jnp.stack(vp_l),
            jnp.stack(ks_l), jnp.stack(vs_l), jnp.stack(sp_l), jnp.stack(ss_l), jnp.stack(sgu_l))
```

```python
import functools
import math

import jax
import jax.numpy as jnp
from jax import lax
from jax.experimental import pallas as pl
from jax.experimental.pallas import tpu as pltpu

F32 = jnp.float32
BF16 = jnp.bfloat16
I32 = jnp.int32

EPS = 1e-6
LOG2E = math.log2(math.e)
CHUNK = 64
SGU_CHUNK = 128
REL_BUCKETS = 32
REL_MAX_DIST = 128
TOP_K = 2
MOE_BLOCK = 512
MOE_BLOCK_SMALL = 128
RUN_CHUNK = 16
HG_SUB = 16

LANES = 128
SUBLANES = 8
VMEM_LIMIT = 56 * 1024 * 1024

NT_DIMS = (((1,), (1,)), ((), ()))
TN_DIMS = (((0,), (0,)), ((), ()))


def _params(*sem):
    return pltpu.CompilerParams(dimension_semantics=sem, vmem_limit_bytes=VMEM_LIMIT)


def _const_spec(shape):
    nd = len(shape)
    return pl.BlockSpec(shape, lambda *_: (0,) * nd, pipeline_mode=pl.Buffered(1))


def _sigmoid(x):
    return 1.0 / (1.0 + jnp.exp(-x))


def _rms(x, g):
    return x * lax.rsqrt(jnp.mean(x * x, axis=-1, keepdims=True) + EPS) * g


def _dot(a, b):
    return jnp.dot(a, b, preferred_element_type=F32)


def rel_bucket(rel):
    half = REL_BUCKETS // 2
    max_exact = half // 2
    ret = (rel > 0).astype(I32) * half
    n = jnp.abs(rel)
    nf = jnp.maximum(n, 1).astype(F32)
    large = max_exact + (jnp.log(nf / max_exact) / math.log(REL_MAX_DIST / max_exact)
                         * (half - max_exact)).astype(I32)
    large = jnp.minimum(large, half - 1)
    return ret + jnp.where(n < max_exact, n, large)


def _in_even_kernel(x_ref, g_ref, w_ref, pm_ref, qg_ref, kg_ref, lb_ref,
                    q_ref, k_ref, v_ref, qh_ref, kb_ref, lf_ref, ih_ref, gs_ref, k5_ref, v4_ref, *, width,
                    q_scale):
    xn = _rms(x_ref[...], g_ref[...]).astype(BF16)

    def proj(c):
        return _dot(xn, w_ref[:, c * width:(c + 1) * width])

    def group_norm(y, gain):
        ms = _dot((y * y).astype(BF16), pm_ref[...])
        return y * lax.rsqrt(ms + EPS) * gain

    q_ref[...] = group_norm(proj(0), qg_ref[...]) * q_scale
    kn = group_norm(proj(1), kg_ref[...])
    k_ref[...] = kn
    bb, ts, heads, _, dh = k5_ref.shape
    for h in range(heads):
        for c in range(2):
            piece = kn[:, (2 * h + c) * dh:(2 * h + c + 1) * dh]
            k5_ref[:, :, h, c, :] = piece.reshape(bb, ts, dh)
    vv = proj(2)
    v_ref[...] = vv
    for h in range(heads):
        v4_ref[:, :, h, :] = vv[:, h * 2 * dh:(h + 1) * 2 * dh].reshape(bb, ts, 2 * dh)
    yq = proj(3)
    qh_ref[...] = yq * _sigmoid(yq)
    zf = proj(4)
    lb = lb_ref[...]
    lf_ref[...] = jnp.log(lb + (1.0 - lb) * _sigmoid(zf))
    kb_ref[...] = (1.0 - lb) * _sigmoid(-zf)
    ih_ref[...] = proj(5)
    yg = proj(6)
    gs_ref[...] = yg * _sigmoid(yg)


def _in_even(x, g_mix, w_bf, pm, qg, kg, lb, *, width, q_scale, tm, batch, seq, heads, dh):
    n, d = x.shape
    assert n % tm == 0 and (tm % seq == 0 or seq % tm == 0)
    row = lambda i: (i, 0)
    out = jax.ShapeDtypeStruct((n, width), F32)
    bb, ts = max(tm // seq, 1), min(tm, seq)
    per = seq // ts
    k5_spec = pl.BlockSpec((bb, ts, heads, 2, dh), lambda i: (i // per, i % per, 0, 0, 0))
    v4_spec = pl.BlockSpec((bb, ts, heads, 2 * dh), lambda i: (i // per, i % per, 0, 0))
    return pl.pallas_call(
        functools.partial(_in_even_kernel, width=width, q_scale=q_scale),
        grid=(n // tm,),
        in_specs=[pl.BlockSpec((tm, d), row), _const_spec((1, d)), _const_spec(w_bf.shape),
                  _const_spec(pm.shape), _const_spec((1, width)), _const_spec((1, width)),
                  _const_spec((1, width))],
        out_specs=[pl.BlockSpec((tm, width), row)] * 8 + [k5_spec, v4_spec],
        out_shape=[out] * 8 + [jax.ShapeDtypeStruct((batch, seq, heads, 2, dh), F32),
                               jax.ShapeDtypeStruct((batch, seq, heads, 2 * dh), F32)],
        compiler_params=_params("arbitrary"),
        name="in_even",
    )(x, g_mix, w_bf, pm, qg, kg, lb)


def _bias_from_buckets(bk, rb_ref, h):
    b = jnp.zeros(bk.shape, F32)
    for u in range(REL_BUCKETS):
        b = jnp.where(bk == u, rb_ref[u, h], b)
    return jnp.where(bk < 0, -jnp.inf, b)


def _lam(lam_ref, lam_init):
    r = lam_ref[...]
    s1 = jnp.sum(r[0:1] * r[1:2], axis=1, keepdims=True)
    s2 = jnp.sum(r[2:3] * r[3:4], axis=1, keepdims=True)
    return jnp.exp(s1) - jnp.exp(s2) + lam_init


def _split_components(q, dh):
    lane = lax.broadcasted_iota(I32, q.shape, 1)
    q0 = jnp.where(lane < dh, q, 0.0)
    q1 = jnp.where(lane >= dh, q, 0.0)
    return jnp.concatenate([q0, q1], axis=0).astype(BF16)


def _attn_prompt_kernel(rb_ref, far_ref, lam_ref, bk_ref, sg_ref, q_ref, k_ref, v_ref, o_ref, bias_s, *,
                        qt, dh, lam_init):
    h = pl.program_id(0)
    t = q_ref.shape[0]
    kb = k_ref[...].astype(BF16)
    vt = v_ref[...].T.astype(BF16)
    qtr = q_ref[...].T
    sub = lax.broadcasted_iota(I32, (2 * dh, qt), 0)

    @pl.when(pl.program_id(1) == 0)
    def _():
        for d in range(2):
            b = _bias_from_buckets(bk_ref[d], rb_ref, h) * LOG2E
            bias_s[d] = jnp.concatenate([b, b], axis=1)

    bias = [bias_s[0], bias_s[1]]
    far = rb_ref[far_ref[0], h] * LOG2E
    lam = _lam(lam_ref, lam_init)
    gain = sg_ref[...] * (1.0 - lam_init)
    for i in range(t // qt):
        qi = qtr[:, i * qt:(i + 1) * qt]
        qz = jnp.concatenate([jnp.where(sub < dh, qi, 0.0), jnp.where(sub >= dh, qi, 0.0)],
                             axis=1).astype(BF16)
        n = (i + 1) * qt
        parts = [(n - qt, n)]
        s = [_dot(kb[n - qt:n], qz) + bias[0]]
        shift = [0.0]
        if i >= 1:
            parts.append((n - 2 * qt, n - qt))
            s.append(_dot(kb[n - 2 * qt:n - qt], qz) + bias[1])
            shift.append(0.0)
        if i >= 2:
            parts.append((0, n - 2 * qt))
            s.append(_dot(kb[:n - 2 * qt], qz))
            shift.append(far)
        m = functools.reduce(jnp.maximum, [jnp.max(x, axis=0, keepdims=True) + c for x, c in zip(s, shift)])
        p = [jnp.exp2(x - (m - c)) for x, c in zip(s, shift)]
        l = functools.reduce(lambda a, b: a + b, [jnp.sum(x, axis=0, keepdims=True) for x in p])
        acc = functools.reduce(lambda a, b: a + b,
                               [_dot(vt[:, lo:hi], x.astype(BF16)) for (lo, hi), x in zip(parts, p)])
        o = acc / l
        out = o[:, :qt] - lam * o[:, qt:]
        out = out * lax.rsqrt(jnp.mean(out * out, axis=0, keepdims=True) + EPS) * gain
        o_ref[i * qt:(i + 1) * qt, :] = out.T


def _attn_prompt(q, k, v, rel_bias, lam4, sub_gain, *, batch, seq, heads, dh, lam_init, qt):
    n, w = q.shape
    dv = w // heads
    assert dv == 2 * dh and seq % qt == 0 and qt % CHUNK == 0
    kj = jnp.arange(qt, dtype=I32)[:, None]
    qi = jnp.arange(qt, dtype=I32)[None, :]
    bk0 = jnp.where((kj // CHUNK) <= (qi // CHUNK), rel_bucket(kj - qi), -1)
    bk1 = rel_bucket(kj - qi - qt)
    bk = jnp.stack([bk0, bk1]).astype(I32)
    assert qt + 1 >= REL_MAX_DIST
    far = rel_bucket(jnp.full((1,), -(qt + 1), I32))
    smem = pl.BlockSpec(memory_space=pltpu.SMEM)
    seq_blk = pl.BlockSpec((seq, dv), lambda h, b: (b, h))
    return pl.pallas_call(
        functools.partial(_attn_prompt_kernel, qt=qt, dh=dh, lam_init=lam_init),
        grid=(heads, batch),
        in_specs=[smem, smem, _const_spec(lam4.shape), _const_spec(bk.shape), _const_spec((dv, 1)),
                  seq_blk, seq_blk, seq_blk],
        out_specs=seq_blk,
        out_shape=jax.ShapeDtypeStruct((n, w), F32),
        scratch_shapes=[pltpu.VMEM((2, qt, 2 * qt), F32)],
        compiler_params=_params("arbitrary", "arbitrary"),
        name="attn_prompt",
    )(rel_bias, far, lam4, bk, sub_gain.T, q, k, v)


def _attn_sample_kernel(rb_ref, lam_ref, bkc_ref, bkn_ref, sg_ref, q_ref, kc_ref, vc_ref, kn_ref, vn_ref,
                        o_ref, bc_s, bn_s, *, t, dh, lam_init):
    h = pl.program_id(0)
    qz = _split_components(q_ref[...], dh)

    @pl.when(pl.program_id(1) == 0)
    def _():
        bc = _bias_from_buckets(bkc_ref[...], rb_ref, h) * LOG2E
        bn = _bias_from_buckets(bkn_ref[...], rb_ref, h) * LOG2E
        bc_s[...] = jnp.concatenate([bc, bc], axis=0)
        bn_s[...] = jnp.concatenate([bn, bn], axis=0)

    sc = lax.dot_general(qz, kc_ref[0].astype(BF16), NT_DIMS, preferred_element_type=F32)
    sn = lax.dot_general(qz, kn_ref[...].astype(BF16), NT_DIMS, preferred_element_type=F32)
    sc = sc + bc_s[...]
    sn = sn + bn_s[...]
    m = jnp.maximum(jnp.max(sc, axis=1, keepdims=True), jnp.max(sn, axis=1, keepdims=True))
    pc = jnp.exp2(sc - m)
    pn = jnp.exp2(sn - m)
    l = jnp.sum(pc, axis=1, keepdims=True) + jnp.sum(pn, axis=1, keepdims=True)
    acc = _dot(pc.astype(BF16), vc_ref[0].astype(BF16)) + _dot(pn.astype(BF16), vn_ref[...].astype(BF16))
    o = acc / l
    out = o[:t] - _lam(lam_ref, lam_init) * o[t:]
    o_ref[...] = _rms(out, sg_ref[...]) * (1.0 - lam_init)


def _attn_sample(q, k_new, v_new, cache_k, cache_v, rel_bias, lam4, sub_gain, *, batch, t, heads, dh,
                 lam_init):
    n, w = q.shape
    dv = w // heads
    past = cache_k.shape[1]
    assert past % CHUNK == 0 and t <= CHUNK
    qpos = past + jnp.arange(t, dtype=I32)[:, None]
    bkc = rel_bucket(jnp.arange(past, dtype=I32)[None, :] - qpos).astype(I32)
    bkn = rel_bucket(past + jnp.arange(t, dtype=I32)[None, :] - qpos).astype(I32)
    smem = pl.BlockSpec(memory_space=pltpu.SMEM)
    new = pl.BlockSpec((t, dv), lambda h, b: (b, h))
    old = pl.BlockSpec((1, past, dv), lambda h, b: (b, 0, h))
    return pl.pallas_call(
        functools.partial(_attn_sample_kernel, t=t, dh=dh, lam_init=lam_init),
        grid=(heads, batch),
        in_specs=[smem, _const_spec(lam4.shape), _const_spec(bkc.shape), _const_spec(bkn.shape),
                  _const_spec((1, dv)), new, old, old, new, new],
        out_specs=new,
        out_shape=jax.ShapeDtypeStruct((n, w), F32),
        scratch_shapes=[pltpu.VMEM((2 * t, past), F32), pltpu.VMEM((2 * t, t), F32)],
        compiler_params=_params("arbitrary", "arbitrary"),
        name="attn_sample",
    )(rel_bias, lam4, bkc, bkn, sub_gain, q, cache_k, cache_v, k_new, v_new)


def _cumsum_rows(x):
    c = x.shape[0]
    row = lax.broadcasted_iota(I32, x.shape, 0)
    s = 1
    while s < c:
        x = x + jnp.where(row >= s, pltpu.roll(x, s, axis=0), 0.0)
        s *= 2
    return x


def _hgrn_kernel(qh_ref, kb_ref, lf_ref, ih_ref, s0_ref, hg_ref, ob_ref, sf_ref, st_s, *,
                 heads, dk, dv, c, nsb):
    t = pl.program_id(1)

    @pl.when(t == 0)
    def _():
        for h in range(heads):
            st_s[h] = s0_ref[0, h].T

    tb = qh_ref.shape[0]
    row = lax.broadcasted_iota(I32, (c, c), 0)
    col = lax.broadcasted_iota(I32, (c, c), 1)
    causal = col <= row

    def chunk(ci, carry):
        r0 = pl.multiple_of(ci * c, c)
        for h in range(heads):
            rows = pl.ds(r0, c)
            q = qh_ref[rows, h * dk:(h + 1) * dk]
            k = kb_ref[rows, h * dk:(h + 1) * dk]
            v = ih_ref[rows, h * dv:(h + 1) * dv]
            b = _cumsum_rows(lf_ref[rows, h * dk:(h + 1) * dk])
            bl = b[c - 1:c]
            st = st_s[h]
            inter = lax.dot_general((q * jnp.exp(b)).astype(BF16), st.astype(BF16), NT_DIMS,
                                    preferred_element_type=F32)
            qs, ks = [], []
            for j in range(nsb):
                ref = b[j * HG_SUB + HG_SUB // 2:j * HG_SUB + HG_SUB // 2 + 1]
                qs.append(q * jnp.exp(b - ref))
                sub = slice(j * HG_SUB, (j + 1) * HG_SUB)
                ks.append(k[sub] * jnp.exp(ref - b[sub]))
            a_full = lax.dot_general(jnp.concatenate(qs, axis=0).astype(BF16),
                                     jnp.concatenate(ks, axis=0).astype(BF16), NT_DIMS,
                                     preferred_element_type=F32)
            att = jnp.zeros((c, c), F32)
            for j in range(nsb):
                att = jnp.where(col >= j * HG_SUB, a_full[j * c:(j + 1) * c], att)
            att = jnp.where(causal, att, 0.0)
            out = inter + _dot(att.astype(BF16), v.astype(BF16))
            ob_ref[rows, h * dv:(h + 1) * dv] = _rms(out, hg_ref[...])
            kdec = (k * jnp.exp(bl - b)).astype(BF16)
            st_s[h] = jnp.exp(bl) * st + lax.dot_general(v.astype(BF16), kdec, TN_DIMS,
                                                         preferred_element_type=F32)
        return carry

    lax.fori_loop(0, tb // c, chunk, 0, unroll=min(4, tb // c))

    @pl.when(t == pl.num_programs(1) - 1)
    def _():
        for h in range(heads):
            sf_ref[0, h] = st_s[h].T


def _hgrn(qh, kb, lf, ih, s0, hg_gain, *, batch, seq, heads, dk, dv, tb):
    n = qh.shape[0]
    c = min(CHUNK, seq)
    assert seq % tb == 0 and tb % c == 0 and c % HG_SUB == 0
    nt = seq // tb
    blk = lambda w: pl.BlockSpec((tb, w), lambda b, t: (b * nt + t, 0))
    st = pl.BlockSpec((1, heads, dk, dv), lambda b, t: (b, 0, 0, 0))
    return pl.pallas_call(
        functools.partial(_hgrn_kernel, heads=heads, dk=dk, dv=dv, c=c, nsb=c // HG_SUB),
        grid=(batch, nt),
        in_specs=[blk(heads * dk), blk(heads * dk), blk(heads * dk), blk(heads * dv), st,
                  _const_spec((1, dv))],
        out_specs=[blk(heads * dv), st],
        out_shape=[jax.ShapeDtypeStruct((n, heads * dv), F32),
                   jax.ShapeDtypeStruct((batch, heads, dk, dv), F32)],
        scratch_shapes=[pltpu.VMEM((heads, dv, dk), F32)],
        compiler_params=_params("arbitrary", "arbitrary"),
        name="hgrn",
    )(qh, kb, lf, ih, s0, hg_gain)


def _store_token_tiles(ref, x):
    rows, d = x.shape
    s = d // LANES
    for c in range(s):
        ref[pl.ds(c, rows, stride=s), :] = x[:, c * LANES:(c + 1) * LANES]


def _load_token_tiles(ref, rows, s):
    return jnp.concatenate([ref[pl.ds(c, rows, stride=s), :] for c in range(s)], axis=1)


def _token_tile(r, s):
    return pl.ds(pl.multiple_of(r * s, s), s)


def _ffn_prologue(x1, gf_ref, wrh_ref, wrl_ref, rbias_ref, x1_ref, xn_ref, lg_ref):
    x1_ref[...] = x1
    xn = _rms(x1, gf_ref[...])
    _store_token_tiles(xn_ref, xn)
    hi = xn.astype(BF16)
    lo = (xn - hi.astype(F32)).astype(BF16)
    nt = functools.partial(lax.dot_general, dimension_numbers=NT_DIMS, preferred_element_type=F32)
    lg_ref[...] = nt(wrh_ref[...], hi) + nt(wrh_ref[...], lo) + nt(wrl_ref[...], hi) + rbias_ref[...]


def _out_even_kernel(oa_ref, ob_ref, gs_ref, x_ref, w_ref, gf_ref, wrh_ref, wrl_ref, rbias_ref,
                     x1_ref, xn_ref, lg_ref):
    o = jnp.concatenate([oa_ref[...], ob_ref[...] * gs_ref[...]], axis=1).astype(BF16)
    x1 = x_ref[...] + _dot(o, w_ref[...])
    _ffn_prologue(x1, gf_ref, wrh_ref, wrl_ref, rbias_ref, x1_ref, xn_ref, lg_ref)


def _out_even(oa, ob, gs, x, w_bf, g_ffn, wrh, wrl, rbias, *, tm):
    n, d = x.shape
    w = oa.shape[1]
    nr = wrh.shape[0]
    row = lambda i: (i, 0)
    return pl.pallas_call(
        _out_even_kernel,
        grid=(n // tm,),
        in_specs=[pl.BlockSpec((tm, w), row), pl.BlockSpec((tm, w), row), pl.BlockSpec((tm, w), row),
                  pl.BlockSpec((tm, d), row), _const_spec(w_bf.shape), _const_spec((1, d)),
                  _const_spec(wrh.shape), _const_spec(wrl.shape), _const_spec(rbias.shape)],
        out_specs=[pl.BlockSpec((tm, d), row), pl.BlockSpec((tm * (d // LANES), LANES), row),
                   pl.BlockSpec((nr, tm), lambda i: (0, i))],
        out_shape=[jax.ShapeDtypeStruct((n, d), F32), jax.ShapeDtypeStruct((n * (d // LANES), LANES), F32),
                   jax.ShapeDtypeStruct((nr, n), F32)],
        compiler_params=_params("arbitrary"),
        name="out_even",
    )(oa, ob, gs, x, w_bf, g_ffn, wrh, wrl, rbias)


def _gelu(x):
    return 0.5 * x * (1.0 + jnp.tanh(math.sqrt(2.0 / math.pi) * (x + 0.044715 * (x * x * x))))


def _odd_kernel(x_ref, gm_ref, win_ref, vg_ref, wsp_ref, bsp_ref, wout_ref, gf_ref, wrh_ref, wrl_ref,
                rbias_ref, x1_ref, xn_ref, lg_ref, *rest, half, groups, l, emit_v):
    if emit_v:
        vn_ref, u_s, s_s = rest
    else:
        vn_ref = None
        u_s, vn_s, s_s = rest
    tm = x_ref.shape[0]
    x = x_ref[...]
    xn = _rms(x, gm_ref[...]).astype(BF16)
    cw = 512
    vbuf = vn_ref if emit_v else vn_s
    for cidx in range(half // cw):
        u_s[:, cidx * cw:(cidx + 1) * cw] = _gelu(_dot(xn, win_ref[:, cidx * cw:(cidx + 1) * cw]))
        vbuf[:, cidx * cw:(cidx + 1) * cw] = _gelu(
            _dot(xn, win_ref[:, half + cidx * cw:half + (cidx + 1) * cw]))
    vbuf[...] = _rms(vbuf[...], vg_ref[...])
    gw = half // groups
    row = lax.broadcasted_iota(I32, (l, l), 0)
    col = lax.broadcasted_iota(I32, (l, l), 1)
    for g in range(groups):
        wg = jnp.where(col <= row, wsp_ref[g], 0.0).astype(BF16)
        bg = bsp_ref[:, g:g + 1]
        for ci in range(tm // l):
            vv = vbuf[ci * l:(ci + 1) * l, g * gw:(g + 1) * gw].astype(BF16)
            s_s[ci * l:(ci + 1) * l, g * gw:(g + 1) * gw] = _dot(wg, vv) + bg
    y = _dot((u_s[...] * s_s[...]).astype(BF16), wout_ref[...])
    _ffn_prologue(x + y, gf_ref, wrh_ref, wrl_ref, rbias_ref, x1_ref, xn_ref, lg_ref)


def _odd_mixer(x, g_mix, win_bf, v_gain, wsp, bsp_t, wout_bf, g_ffn, wrh, wrl, rbias, *, l, tm, emit_v):
    n, d = x.shape
    half = wout_bf.shape[0]
    groups = wsp.shape[0]
    nr = wrh.shape[0]
    assert n % tm == 0 and tm % l == 0
    row = lambda i: (i, 0)
    out_specs = [pl.BlockSpec((tm, d), row), pl.BlockSpec((tm * (d // LANES), LANES), row),
                 pl.BlockSpec((nr, tm), lambda i: (0, i))]
    out_shape = [jax.ShapeDtypeStruct((n, d), F32), jax.ShapeDtypeStruct((n * (d // LANES), LANES), F32),
                 jax.ShapeDtypeStruct((nr, n), F32)]
    scratch = [pltpu.VMEM((tm, half), F32)]
    if emit_v:
        out_specs.append(pl.BlockSpec((tm, half), row))
        out_shape.append(jax.ShapeDtypeStruct((n, half), F32))
    else:
        scratch.append(pltpu.VMEM((tm, half), F32))
    scratch.append(pltpu.VMEM((tm, half), F32))
    return pl.pallas_call(
        functools.partial(_odd_kernel, half=half, groups=groups, l=l, emit_v=emit_v),
        grid=(n // tm,),
        in_specs=[pl.BlockSpec((tm, d), row), _const_spec((1, d)), _const_spec(win_bf.shape),
                  _const_spec((1, half)), _const_spec(wsp.shape), _const_spec(bsp_t.shape),
                  _const_spec(wout_bf.shape), _const_spec((1, d)), _const_spec(wrh.shape),
                  _const_spec(wrl.shape), _const_spec(rbias.shape)],
        out_specs=out_specs,
        out_shape=out_shape,
        scratch_shapes=scratch,
        compiler_params=_params("arbitrary"),
        name="odd_mixer",
    )(x, g_mix, win_bf, v_gain, wsp, bsp_t, wout_bf, g_ffn, wrh, wrl, rbias)


def _route_kernel(lg_ref, tri_ref, init_ref, e_ref, g_ref, r_ref, p_ref, cnt_ref, base_ref, tcnt_ref, run_s,
                  *, groups, epg):
    i = pl.program_id(0)

    @pl.when(i == 0)
    def _():
        run_s[...] = init_ref[...]

    lg = lg_ref[...]
    tr = lg.shape[1]
    gl = [lg[g:g + 1] for g in range(groups)]
    m = functools.reduce(jnp.maximum, gl)
    grp = jnp.full((1, tr), groups - 1, I32)
    for g in range(groups - 2, -1, -1):
        grp = jnp.where(gl[g] == m, g, grp)
    gate_g = 1.0 / functools.reduce(lambda a, b: a + b, [jnp.exp(x - m) for x in gl])
    sel = lg[SUBLANES + (groups - 1) * epg:SUBLANES + groups * epg]
    for g in range(groups - 2, -1, -1):
        sel = jnp.where(grp == g, lg[SUBLANES + g * epg:SUBLANES + (g + 1) * epg], sel)
    sub = lax.broadcasted_iota(I32, sel.shape, 0)
    v1 = jnp.max(sel, axis=0, keepdims=True)
    i1 = jnp.min(jnp.where(sel == v1, sub, epg), axis=0, keepdims=True)
    sel2 = jnp.where(sub == i1, -jnp.inf, sel)
    v2 = jnp.max(sel2, axis=0, keepdims=True)
    i2 = jnp.min(jnp.where(sel2 == v2, sub, epg), axis=0, keepdims=True)
    tt = jnp.exp(v2 - v1)
    g1 = gate_g / (1.0 + tt)
    g2 = gate_g * tt / (1.0 + tt)
    e1 = grp * epg + i1
    e2 = grp * epg + i2
    ne = groups * epg
    eidx = lax.broadcasted_iota(I32, (ne, tr), 0)
    oh1 = eidx == e1
    oh2 = eidx == e2
    cnt = jnp.where(oh1, 1.0, 0.0) + jnp.where(oh2, 1.0, 0.0)
    local = _dot(cnt.astype(BF16), tri_ref[...])
    before = run_s[:, 0:1] + local
    r1 = jnp.sum(jnp.where(oh1, before, 0.0), axis=0, keepdims=True)
    r2 = jnp.sum(jnp.where(oh2, before, 0.0), axis=0, keepdims=True)
    tile_cnt = jnp.broadcast_to(jnp.sum(cnt, axis=1, keepdims=True), run_s.shape)
    padded = jnp.ceil(tile_cnt * (1.0 / RUN_CHUNK)) * RUN_CHUNK
    offset = _cumsum_rows(padded) - padded
    where_local = offset[:, 0:1] + local
    p1 = jnp.sum(jnp.where(oh1, where_local, 0.0), axis=0, keepdims=True)
    p2 = jnp.sum(jnp.where(oh2, where_local, 0.0), axis=0, keepdims=True)
    base_ref[...] = run_s[...]
    tcnt_ref[...] = tile_cnt
    run_s[...] = run_s[...] + tile_cnt
    rows = lax.broadcasted_iota(I32, (SUBLANES, tr), 0)
    e_ref[...] = jnp.where(rows == 0, e1, jnp.where(rows == 1, e2, 0))
    g_ref[...] = jnp.where(rows == 0, g1, jnp.where(rows == 1, g2, 0.0))
    r_ref[...] = jnp.where(rows == 0, r1, jnp.where(rows == 1, r2, 0.0)).astype(I32)
    p_ref[...] = jnp.where(rows == 0, p1, jnp.where(rows == 1, p2, 0.0)).astype(I32)
    cnt_ref[...] = run_s[...]


def _route(lgt, init, *, groups, epg, tr):
    nr, n = lgt.shape
    assert n % tr == 0 and nr == SUBLANES + groups * epg
    ne = groups * epg
    tri = (jnp.arange(tr)[:, None] < jnp.arange(tr)[None, :]).astype(BF16)
    tok = pl.BlockSpec((SUBLANES, tr), lambda i: (0, i))
    per_tile = pl.BlockSpec((ne, LANES), lambda i: (i, 0))
    tile_tab = jax.ShapeDtypeStruct((n // tr * ne, LANES), F32)
    return pl.pallas_call(
        functools.partial(_route_kernel, groups=groups, epg=epg),
        grid=(n // tr,),
        in_specs=[pl.BlockSpec((nr, tr), lambda i: (0, i)), _const_spec((tr, tr)), _const_spec((ne, LANES))],
        out_specs=[tok, tok, tok, tok, pl.BlockSpec((ne, LANES), lambda i: (0, 0)), per_tile, per_tile],
        out_shape=[jax.ShapeDtypeStruct((SUBLANES, n), I32), jax.ShapeDtypeStruct((SUBLANES, n), F32),
                   jax.ShapeDtypeStruct((SUBLANES, n), I32), jax.ShapeDtypeStruct((SUBLANES, n), I32),
                   jax.ShapeDtypeStruct((ne, LANES), F32), tile_tab, tile_tab],
        scratch_shapes=[pltpu.VMEM((ne, LANES), F32)],
        compiler_params=_params("arbitrary"),
        name="route",
    )(lgt, tri, init)


ISSUE_UNROLL = 8
DISPATCH_SLOTS = 3


def _dispatch_kernel(seg_ref, dst_ref, x_ref, *rest, s, bm, fresh):
    buf_ref, zero_s, xin_s, sem, isem, zsem = rest if fresh else rest[1:]
    tp = dst_ref.shape[2]
    blk = bm * s

    def zero_fill():
        zero_s[...] = jnp.zeros(zero_s.shape, F32)

        def block_copy(b):
            return pltpu.make_async_copy(zero_s, buf_ref.at[pl.ds(pl.multiple_of(b * blk, blk), blk)], zsem)

        for e in range(seg_ref.shape[1]):
            @pl.when(seg_ref[1, e] > 0)
            def _():
                block_copy(seg_ref[0, e] // bm - 1).start()

            @pl.when(seg_ref[1, e] - seg_ref[2, e] > bm)
            def _():
                block_copy(seg_ref[0, e] // bm - 2).start()
        for e in range(seg_ref.shape[1]):
            @pl.when(seg_ref[1, e] > 0)
            def _():
                block_copy(0).wait()

            @pl.when(seg_ref[1, e] - seg_ref[2, e] > bm)
            def _():
                block_copy(0).wait()

        def tail_start(b, c):
            block_copy(b).start()
            return c

        def tail_wait(b, c):
            block_copy(b).wait()
            return c

        first_unused = seg_ref[0, seg_ref.shape[1] - 1] // bm
        n_blocks = buf_ref.shape[0] // blk
        lax.fori_loop(first_unused, n_blocks, tail_start, 0)
        lax.fori_loop(first_unused, n_blocks, tail_wait, 0)

    if fresh:
        pl.when(pl.program_id(0) == 0)(zero_fill)

    i = pl.program_id(0)
    last = pl.num_programs(0) - 1
    rows = tp * s

    def tile_in(t):
        return pltpu.make_async_copy(x_ref.at[pl.ds(pl.multiple_of(t * rows, rows), rows)],
                                     xin_s.at[t % DISPATCH_SLOTS], isem.at[t % DISPATCH_SLOTS])

    def rows_out_wait(t):
        for kk in range(TOP_K):
            pltpu.make_async_copy(xin_s.at[t % DISPATCH_SLOTS], buf_ref.at[pl.ds(0, rows)],
                                  sem.at[t % DISPATCH_SLOTS]).wait()

    @pl.when(i == 0)
    def _():
        tile_in(i).start()

    @pl.when(i >= DISPATCH_SLOTS - 1)
    def _():
        rows_out_wait(i - (DISPATCH_SLOTS - 1))

    @pl.when(i < last)
    def _():
        tile_in(i + 1).start()

    tile_in(i).wait()
    slot = i % DISPATCH_SLOTS

    def issue(r0, c):
        for u in range(ISSUE_UNROLL):
            r = r0 * ISSUE_UNROLL + u
            for kk in range(TOP_K):
                pltpu.make_async_copy(xin_s.at[slot, _token_tile(r, s)],
                                      buf_ref.at[_token_tile(dst_ref[0, kk, r], s)], sem.at[slot]
                                      ).start(priority=kk % 2)
        return c

    lax.fori_loop(0, tp // ISSUE_UNROLL, issue, 0)

    @pl.when(i == last)
    def _():
        for back in range(DISPATCH_SLOTS - 2, -1, -1):
            @pl.when(i >= back)
            def _():
                rows_out_wait(i - back)


def _dispatch(seg, dest3, xn, n_slots, *, tp, s, bm, prev=None):
    n = xn.shape[0] // s
    assert tp % ISSUE_UNROLL == 0
    fresh = prev is None
    grid_spec = pltpu.PrefetchScalarGridSpec(
        num_scalar_prefetch=1,
        grid=(n // tp,),
        in_specs=[pl.BlockSpec((1, TOP_K, tp), lambda i, sg: (i, 0, 0), memory_space=pltpu.SMEM),
                  pl.BlockSpec(memory_space=pl.ANY)] + ([] if fresh else [pl.BlockSpec(memory_space=pl.ANY)]),
        out_specs=pl.BlockSpec(memory_space=pl.ANY),
        scratch_shapes=[pltpu.VMEM((bm * s, LANES), F32), pltpu.VMEM((DISPATCH_SLOTS, tp * s, LANES), F32),
                        pltpu.SemaphoreType.DMA((DISPATCH_SLOTS,)), pltpu.SemaphoreType.DMA((DISPATCH_SLOTS,)),
                        pltpu.SemaphoreType.DMA(())],
    )
    return pl.pallas_call(
        functools.partial(_dispatch_kernel, s=s, bm=bm, fresh=fresh),
        grid_spec=grid_spec,
        out_shape=jax.ShapeDtypeStruct((n_slots * s, LANES), F32),
        input_output_aliases={} if fresh else {3: 0},
        compiler_params=_params("arbitrary"),
        name="dispatch",
    )(*((seg, dest3, xn) if fresh else (seg, dest3, xn, prev)))


def _expert_kernel(be_ref, nu_ref, x_ref, wg_ref, wu_ref, wd_ref, o_ref, wg_s, wu_s, wd_s):
    b = pl.program_id(0)
    s = wg_s.shape[0] // LANES
    prev = be_ref[jnp.maximum(b - 1, 0)]

    @pl.when((b == 0) | (be_ref[b] != prev))
    def _():
        wg_s[...] = wg_ref[0, 0].astype(BF16)
        wu_s[...] = wu_ref[0, 0].astype(BF16)
        wd_s[...] = wd_ref[0, 0].astype(BF16)

    @pl.when(b < nu_ref[0])
    def _():
        xb = _load_token_tiles(x_ref, x_ref.shape[0] // s, s).astype(BF16)
        gate = _dot(xb, wg_s[...])
        h = gate * _sigmoid(gate) * _dot(xb, wu_s[...])
        _store_token_tiles(o_ref, _dot(h.astype(BF16), wd_s[...]))

    @pl.when(b >= nu_ref[0])
    def _():
        o_ref[...] = jnp.zeros(o_ref.shape, F32)


def _experts(blk_expert, n_used, buf, w_gate, w_up, w_down, *, layer, bm):
    d, de = w_gate.shape[2:]
    s = d // LANES
    n_blocks = buf.shape[0] // (bm * s)
    rows = pl.BlockSpec((bm * s, LANES), lambda b, be, nu: (b, 0))
    used_rows = pl.BlockSpec((bm * s, LANES), lambda b, be, nu: (jnp.minimum(b, nu[0] - 1), 0))
    grid_spec = pltpu.PrefetchScalarGridSpec(
        num_scalar_prefetch=2,
        grid=(n_blocks,),
        in_specs=[used_rows,
                  pl.BlockSpec((1, 1, d, de), lambda b, be, nu: (layer, be[b], 0, 0)),
                  pl.BlockSpec((1, 1, d, de), lambda b, be, nu: (layer, be[b], 0, 0)),
                  pl.BlockSpec((1, 1, de, d), lambda b, be, nu: (layer, be[b], 0, 0))],
        out_specs=rows,
        scratch_shapes=[pltpu.VMEM((d, de), BF16), pltpu.VMEM((d, de), BF16), pltpu.VMEM((de, d), BF16)],
    )
    return pl.pallas_call(
        _expert_kernel,
        grid_spec=grid_spec,
        out_shape=jax.ShapeDtypeStruct(buf.shape, F32),
        compiler_params=_params("arbitrary"),
        name="experts",
    )(blk_expert, n_used, buf, w_gate, w_up, w_down)


def _combine_kernel(tab_ref, nxt_ref, pos_ref, gate_ref, x_ref, yb_ref, o_ref, stg_s, tt_s, sem):
    i = pl.program_id(0)
    tq, d = x_ref.shape
    s = d // LANES
    piece = RUN_CHUNK * s
    slot = i % 2

    def run_copies(ref, dst_slot, go):
        for e in range(ref.shape[2]):
            def body(j, c):
                src = pl.multiple_of((ref[0, 0, e] + j * RUN_CHUNK) * s, s)
                dst = pl.multiple_of((ref[0, 2, e] + j * RUN_CHUNK) * s, piece)
                cp = pltpu.make_async_copy(yb_ref.at[pl.ds(src, piece)],
                                           stg_s.at[dst_slot, pl.ds(dst, piece)], sem.at[dst_slot])
                cp.start() if go else cp.wait()
                return c
            lax.fori_loop(0, ref[0, 1, e], body, 0)

    @pl.when(i == 0)
    def _():
        run_copies(tab_ref, 0, True)

    @pl.when(i + 1 < pl.num_programs(0))
    def _():
        run_copies(nxt_ref, 1 - slot, True)

    run_copies(tab_ref, slot, False)

    def assemble(r0, c):
        for u in range(ISSUE_UNROLL):
            r = r0 * ISSUE_UNROLL + u
            acc = gate_ref[0, 0, r] * stg_s[slot, _token_tile(pos_ref[0, 0, r], s), :]
            for kk in range(1, TOP_K):
                acc = acc + gate_ref[0, kk, r] * stg_s[slot, _token_tile(pos_ref[0, kk, r], s), :]
            tt_s[_token_tile(r, s), :] = acc
        return c

    lax.fori_loop(0, tq // ISSUE_UNROLL, assemble, 0)
    o_ref[...] = x_ref[...] + _load_token_tiles(tt_s, tq, s)


def _combine(tab, pos3, gates3, x1, yb, *, tq):
    n, d = x1.shape
    s = d // LANES
    nt = n // tq
    ne = tab.shape[2]
    assert tq % ISSUE_UNROLL == 0
    stage_rows = tq * TOP_K + ne * RUN_CHUNK
    smem = lambda shape, imap: pl.BlockSpec(shape, imap, memory_space=pltpu.SMEM)
    return pl.pallas_call(
        _combine_kernel,
        grid=(nt,),
        in_specs=[smem((1, 3, ne), lambda i: (i, 0, 0)),
                  smem((1, 3, ne), lambda i: (jnp.minimum(i + 1, nt - 1), 0, 0)),
                  smem((1, TOP_K, tq), lambda i: (i, 0, 0)),
                  smem((1, TOP_K, tq), lambda i: (i, 0, 0)),
                  pl.BlockSpec((tq, d), lambda i: (i, 0)),
                  pl.BlockSpec(memory_space=pl.ANY)],
        out_specs=pl.BlockSpec((tq, d), lambda i: (i, 0)),
        out_shape=jax.ShapeDtypeStruct((n, d), F32),
        scratch_shapes=[pltpu.VMEM((2, stage_rows * s, LANES), F32), pltpu.VMEM((tq * s, LANES), F32),
                        pltpu.SemaphoreType.DMA((2,))],
        compiler_params=_params("arbitrary"),
        name="combine",
    )(tab, tab, pos3, gates3, x1, yb)


def _moe(streams, w_gate, w_up, w_down, *, layer, groups, epg):
    d = streams[0][0].shape[1]
    ne = groups * epg
    routed = []
    taken = jnp.zeros((ne, LANES), F32)
    for x1, _, lgt in streams:
        rtile = min(512, x1.shape[0])
        routed.append(_route(lgt, taken, groups=groups, epg=epg, tr=rtile) + (rtile,))
        taken = routed[-1][4]
    n_rows = sum(x1.shape[0] for x1, _, _ in streams) * TOP_K
    counts = taken[:, 0].astype(I32)
    bm = MOE_BLOCK if n_rows >= 2 * ne * MOE_BLOCK else MOE_BLOCK_SMALL
    padded = (counts + RUN_CHUNK + bm - 1) // bm * bm
    pend = jnp.cumsum(padded)
    pstart = pend - padded
    n_blocks = -(-(n_rows + ne * RUN_CHUNK) // bm) + ne
    eids = jnp.arange(ne, dtype=I32)
    blk_row = jnp.arange(n_blocks, dtype=I32)[:, None] * bm
    blk_expert = jnp.minimum(jnp.sum((pend[None, :] <= blk_row).astype(I32), axis=1), ne - 1)
    n_used = (pend[-1:] // bm).astype(I32)
    seg_tab = jnp.stack([pend, padded, counts]).astype(I32)
    buf = None
    for (x1, xn, _), (e8, g8, r8, p8, cnt, base, tcnt, rtile) in zip(streams, routed):
        n = x1.shape[0]
        seg = jnp.sum(jnp.where(e8[:TOP_K, :, None] == eids, pstart, 0), axis=-1)
        dest = seg + r8[:TOP_K]
        dest3 = dest.reshape(TOP_K, n // rtile, rtile).transpose(1, 0, 2)
        buf = _dispatch(seg_tab, dest3, xn, n_blocks * bm, tp=rtile, s=d // LANES, bm=bm, prev=buf)
    yb = _experts(blk_expert, n_used, buf, w_gate, w_up, w_down, layer=layer, bm=bm)
    outs = []
    for (x1, _, _), (e8, g8, r8, p8, cnt, base, tcnt, rtile) in zip(streams, routed):
        nrt = x1.shape[0] // rtile
        run_start = pstart[None, :] + base.reshape(nrt, ne, LANES)[:, :, 0].astype(I32)
        pieces = (tcnt.reshape(nrt, ne, LANES)[:, :, 0].astype(I32) + RUN_CHUNK - 1) // RUN_CHUNK
        stage = (jnp.cumsum(pieces, axis=1) - pieces) * RUN_CHUNK
        tab = jnp.stack([run_start, pieces, stage], axis=1)
        by_tile = lambda a: a[:TOP_K].reshape(TOP_K, nrt, rtile).transpose(1, 0, 2)
        outs.append(_combine(tab, by_tile(p8), by_tile(g8), x1, yb, tq=rtile))
    return outs


def _router_weights(wg, bg, we, be):
    d, groups = wg.shape
    epg = we.shape[2]
    assert groups <= SUBLANES and epg == SUBLANES
    pad = jnp.zeros((SUBLANES - groups, d), F32)
    wr = jnp.concatenate([wg.T, pad, we.transpose(0, 2, 1).reshape(groups * epg, d)], axis=0)
    rb = jnp.concatenate([bg, jnp.zeros((SUBLANES - groups,), F32), be.reshape(-1)])[:, None]
    hi = wr.astype(BF16)
    lo = (wr - hi.astype(F32)).astype(BF16)
    return hi, lo, rb, groups, epg


def kernel(x_prompt, x_sample, cache_attn_k, cache_attn_v, state_hgrn, rel_bias, norm_mix, norm_ffn,
           w_in_even, w_out_even, q_norm_gain, k_norm_gain, lam_q1, lam_k1, lam_q2, lam_k2, da_out_gain,
           hgrn_lb_logits, hgrn_out_gain, w_in_odd, sgu_v_gain, sgu_w, sgu_b, w_out_odd,
           router_group_w, router_group_b, router_expert_w, router_expert_b,
           expert_w_gate, expert_w_up, expert_w_down):
    bp, tp, d = x_prompt.shape
    bs, ts, _ = x_sample.shape
    depth = norm_mix.shape[0]
    _, _, past, da_heads, _, da_dh = cache_attn_k.shape
    da_dv = cache_attn_v.shape[-1]
    _, _, hg_heads, hg_dk, hg_dv = state_hgrn.shape
    width = da_heads * da_dv
    assert width == da_heads * 2 * da_dh == hg_heads * hg_dk == hg_heads * hg_dv
    assert da_dv == LANES and hg_dk == LANES and hg_dv == LANES

    lb_all = jnp.cumsum(jax.nn.softmax(hgrn_lb_logits.astype(F32), axis=0), axis=0)
    gid = jnp.arange(width) // da_dh
    pm = jnp.where(gid[:, None] == gid[None, :], 1.0 / da_dh, 0.0).astype(BF16)

    xs = {"p": x_prompt.reshape(bp * tp, d), "s": x_sample.reshape(bs * ts, d)}
    dims = {"p": (bp, tp), "s": (bs, ts)}
    outs = {"p": {}, "s": {}}
    kp_l, vp_l, ks_l, vs_l, sp_l, ss_l, sgu_l = [], [], [], [], [], [], []

    for layer in range(depth):
        j = layer // 2
        wrh, wrl, rbias, groups, epg = _router_weights(
            router_group_w[layer], router_group_b[layer], router_expert_w[layer], router_expert_b[layer])
        g_mix = norm_mix[layer][None, :]
        g_ffn = norm_ffn[layer][None, :]
        if layer % 2 == 0:
            lam_init = 0.8 - 0.6 * math.exp(-0.3 * layer)
            w_in_bf = w_in_even[j].astype(BF16)
            w_out_bf = w_out_even[j].astype(BF16)
            reps = width // da_dh
            qg = jnp.tile(q_norm_gain[j], reps)[None, :]
            kg = jnp.tile(k_norm_gain[j], reps)[None, :]
            lam4 = jnp.stack([lam_q1[j], lam_k1[j], lam_q2[j], lam_k2[j]])
            sub_gain = da_out_gain[j][None, :]
            hg_gain = hgrn_out_gain[j][None, :]
            lb = lb_all[j][None, :]
            for key in ("p", "s"):
                b, t = dims[key]
                x = xs[key]
                q, k, v, qh, kb, lf, ih, gs, k5, v4 = _in_even(
                    x, g_mix, w_in_bf, pm, qg, kg, lb, width=width, q_scale=da_dh ** -0.5 * LOG2E,
                    tm=min(512, b * t), batch=b, seq=t, heads=da_heads, dh=da_dh)
                if key == "p":
                    oa = _attn_prompt(q, k, v, rel_bias, lam4, sub_gain, batch=b, seq=t, heads=da_heads,
                                      dh=da_dh, lam_init=lam_init, qt=min(256, t))
                    s0 = jnp.zeros((b, hg_heads, hg_dk, hg_dv), F32)
                    ob, s_new = _hgrn(qh, kb, lf, ih, s0, hg_gain, batch=b, seq=t, heads=hg_heads,
                                      dk=hg_dk, dv=hg_dv, tb=min(512, t))
                    kp_l.append(k5)
                    vp_l.append(v4)
                    sp_l.append(s_new)
                else:
                    ck = cache_attn_k[j].reshape(b, past, width)
                    cv = cache_attn_v[j].reshape(b, past, width)
                    oa = _attn_sample(q, k, v, ck, cv, rel_bias, lam4, sub_gain, batch=b, t=t,
                                      heads=da_heads, dh=da_dh, lam_init=lam_init)
                    ob, s_new = _hgrn(qh, kb, lf, ih, state_hgrn[j], hg_gain, batch=b, seq=t,
                                      heads=hg_heads, dk=hg_dk, dv=hg_dv, tb=t)
                    ks_l.append(k5)
                    vs_l.append(v4)
                    ss_l.append(s_new)
                outs[key] = _out_even(oa, ob, gs, x, w_out_bf, g_ffn, wrh, wrl, rbias, tm=min(512, b * t))
        else:
            w_in_bf = w_in_odd[j].astype(BF16)
            w_out_bf = w_out_odd[j].astype(BF16)
            v_gain = sgu_v_gain[j][None, :]
            for key in ("p", "s"):
                b, t = dims[key]
                l = min(SGU_CHUNK, t)
                res = _odd_mixer(xs[key], g_mix, w_in_bf, v_gain, sgu_w[j][:, :l, :l], sgu_b[j][:, :l].T,
                                 w_out_bf, g_ffn, wrh, wrl, rbias, l=l, tm=min(512, b * t), emit_v=(key == "s"))
                outs[key] = res[:3]
                if key == "s":
                    sgu_l.append(res[3].reshape(b, t, -1))
        xs["p"], xs["s"] = _moe([outs["p"], outs["s"]], expert_w_gate, expert_w_up, expert_w_down,
                                layer=layer, groups=groups, epg=epg)

    return (xs["p"].reshape(bp, tp, d), xs["s"].reshape(bs, ts, d), jnp.stack(kp_l), jnp.stack(vp_l),
            jnp.stack(ks_l), jnp.stack(vs_l), jnp.stack(sp_l), jnp.stack(ss_l), jnp.stack(sgu_l))
```

```python
import functools
import math

import jax
import jax.numpy as jnp
from jax import lax
from jax.experimental import pallas as pl
from jax.experimental.pallas import tpu as pltpu

F32 = jnp.float32
BF16 = jnp.bfloat16
I32 = jnp.int32

EPS = 1e-6
LOG2E = math.log2(math.e)
CHUNK = 64
SGU_CHUNK = 128
REL_BUCKETS = 32
REL_MAX_DIST = 128
TOP_K = 2
MOE_BLOCK = 512
MOE_BLOCK_SMALL = 128
RUN_CHUNK = 16
HG_SUB = 16

LANES = 128
SUBLANES = 8
VMEM_LIMIT = 56 * 1024 * 1024

NT_DIMS = (((1,), (1,)), ((), ()))
TN_DIMS = (((0,), (0,)), ((), ()))


def _params(*sem):
    return pltpu.CompilerParams(dimension_semantics=sem, vmem_limit_bytes=VMEM_LIMIT)


def _const_spec(shape):
    nd = len(shape)
    return pl.BlockSpec(shape, lambda *_: (0,) * nd, pipeline_mode=pl.Buffered(1))


def _sigmoid(x):
    return 1.0 / (1.0 + jnp.exp(-x))


def _rms(x, g):
    return x * lax.rsqrt(jnp.mean(x * x, axis=-1, keepdims=True) + EPS) * g


def _dot(a, b):
    return jnp.dot(a, b, preferred_element_type=F32)


def rel_bucket(rel):
    half = REL_BUCKETS // 2
    max_exact = half // 2
    ret = (rel > 0).astype(I32) * half
    n = jnp.abs(rel)
    nf = jnp.maximum(n, 1).astype(F32)
    large = max_exact + (jnp.log(nf / max_exact) / math.log(REL_MAX_DIST / max_exact)
                         * (half - max_exact)).astype(I32)
    large = jnp.minimum(large, half - 1)
    return ret + jnp.where(n < max_exact, n, large)


def _in_even_kernel(x_ref, g_ref, w_ref, pm_ref, qg_ref, kg_ref, lb_ref,
                    q_ref, k_ref, v_ref, qh_ref, kb_ref, lf_ref, ih_ref, gs_ref, k5_ref, v4_ref, *, width,
                    q_scale):
    xn = _rms(x_ref[...], g_ref[...]).astype(BF16)

    def proj(c):
        return _dot(xn, w_ref[:, c * width:(c + 1) * width])

    def group_norm(y, gain):
        ms = _dot((y * y).astype(BF16), pm_ref[...])
        return y * lax.rsqrt(ms + EPS) * gain

    q_ref[...] = group_norm(proj(0), qg_ref[...]) * q_scale
    kn = group_norm(proj(1), kg_ref[...])
    k_ref[...] = kn
    bb, ts, heads, _, dh = k5_ref.shape
    for h in range(heads):
        for c in range(2):
            piece = kn[:, (2 * h + c) * dh:(2 * h + c + 1) * dh]
            k5_ref[:, :, h, c, :] = piece.reshape(bb, ts, dh)
    vv = proj(2)
    v_ref[...] = vv
    for h in range(heads):
        v4_ref[:, :, h, :] = vv[:, h * 2 * dh:(h + 1) * 2 * dh].reshape(bb, ts, 2 * dh)
    yq = proj(3)
    qh_ref[...] = yq * _sigmoid(yq)
    zf = proj(4)
    lb = lb_ref[...]
    lf_ref[...] = jnp.log(lb + (1.0 - lb) * _sigmoid(zf))
    kb_ref[...] = (1.0 - lb) * _sigmoid(-zf)
    ih_ref[...] = proj(5)
    yg = proj(6)
    gs_ref[...] = yg * _sigmoid(yg)


def _in_even(x, g_mix, w_bf, pm, qg, kg, lb, *, width, q_scale, tm, batch, seq, heads, dh):
    n, d = x.shape
    assert n % tm == 0 and (tm % seq == 0 or seq % tm == 0)
    row = lambda i: (i, 0)
    out = jax.ShapeDtypeStruct((n, width), F32)
    bb, ts = max(tm // seq, 1), min(tm, seq)
    per = seq // ts
    k5_spec = pl.BlockSpec((bb, ts, heads, 2, dh), lambda i: (i // per, i % per, 0, 0, 0))
    v4_spec = pl.BlockSpec((bb, ts, heads, 2 * dh), lambda i: (i // per, i % per, 0, 0))
    return pl.pallas_call(
        functools.partial(_in_even_kernel, width=width, q_scale=q_scale),
        grid=(n // tm,),
        in_specs=[pl.BlockSpec((tm, d), row), _const_spec((1, d)), _const_spec(w_bf.shape),
                  _const_spec(pm.shape), _const_spec((1, width)), _const_spec((1, width)),
                  _const_spec((1, width))],
        out_specs=[pl.BlockSpec((tm, width), row)] * 8 + [k5_spec, v4_spec],
        out_shape=[out] * 8 + [jax.ShapeDtypeStruct((batch, seq, heads, 2, dh), F32),
                               jax.ShapeDtypeStruct((batch, seq, heads, 2 * dh), F32)],
        compiler_params=_params("arbitrary"),
        name="in_even",
    )(x, g_mix, w_bf, pm, qg, kg, lb)


def _bias_from_buckets(bk, rb_ref, h):
    b = jnp.zeros(bk.shape, F32)
    for u in range(REL_BUCKETS):
        b = jnp.where(bk == u, rb_ref[u, h], b)
    return jnp.where(bk < 0, -jnp.inf, b)


def _lam(lam_ref, lam_init):
    r = lam_ref[...]
    s1 = jnp.sum(r[0:1] * r[1:2], axis=1, keepdims=True)
    s2 = jnp.sum(r[2:3] * r[3:4], axis=1, keepdims=True)
    return jnp.exp(s1) - jnp.exp(s2) + lam_init


def _split_components(q, dh):
    lane = lax.broadcasted_iota(I32, q.shape, 1)
    q0 = jnp.where(lane < dh, q, 0.0)
    q1 = jnp.where(lane >= dh, q, 0.0)
    return jnp.concatenate([q0, q1], axis=0).astype(BF16)


def _attn_prompt_kernel(rb_ref, far_ref, lam_ref, bk_ref, sg_ref, q_ref, k_ref, v_ref, o_ref, bias_s, *,
                        qt, dh, lam_init):
    h = pl.program_id(0)
    t = q_ref.shape[0]
    kb = k_ref[...].astype(BF16)
    vt = v_ref[...].T.astype(BF16)
    qtr = q_ref[...].T
    sub = lax.broadcasted_iota(I32, (2 * dh, qt), 0)

    @pl.when(pl.program_id(1) == 0)
    def _():
        for d in range(2):
            b = _bias_from_buckets(bk_ref[d], rb_ref, h) * LOG2E
            bias_s[d] = jnp.concatenate([b, b], axis=1)

    bias = [bias_s[0], bias_s[1]]
    far = rb_ref[far_ref[0], h] * LOG2E
    lam = _lam(lam_ref, lam_init)
    gain = sg_ref[...] * (1.0 - lam_init)
    for i in range(t // qt):
        qi = qtr[:, i * qt:(i + 1) * qt]
        qz = jnp.concatenate([jnp.where(sub < dh, qi, 0.0), jnp.where(sub >= dh, qi, 0.0)],
                             axis=1).astype(BF16)
        n = (i + 1) * qt
        parts = [(n - qt, n)]
        s = [_dot(kb[n - qt:n], qz) + bias[0]]
        shift = [0.0]
        if i >= 1:
            parts.append((n - 2 * qt, n - qt))
            s.append(_dot(kb[n - 2 * qt:n - qt], qz) + bias[1])
            shift.append(0.0)
        if i >= 2:
            parts.append((0, n - 2 * qt))
            s.append(_dot(kb[:n - 2 * qt], qz))
            shift.append(far)
        m = functools.reduce(jnp.maximum, [jnp.max(x, axis=0, keepdims=True) + c for x, c in zip(s, shift)])
        p = [jnp.exp2(x - (m - c)) for x, c in zip(s, shift)]
        l = functools.reduce(lambda a, b: a + b, [jnp.sum(x, axis=0, keepdims=True) for x in p])
        acc = functools.reduce(lambda a, b: a + b,
                               [_dot(vt[:, lo:hi], x.astype(BF16)) for (lo, hi), x in zip(parts, p)])
        o = acc / l
        out = o[:, :qt] - lam * o[:, qt:]
        out = out * lax.rsqrt(jnp.mean(out * out, axis=0, keepdims=True) + EPS) * gain
        o_ref[i * qt:(i + 1) * qt, :] = out.T


def _attn_prompt(q, k, v, rel_bias, lam4, sub_gain, *, batch, seq, heads, dh, lam_init, qt):
    n, w = q.shape
    dv = w // heads
    assert dv == 2 * dh and seq % qt == 0 and qt % CHUNK == 0
    kj = jnp.arange(qt, dtype=I32)[:, None]
    qi = jnp.arange(qt, dtype=I32)[None, :]
    bk0 = jnp.where((kj // CHUNK) <= (qi // CHUNK), rel_bucket(kj - qi), -1)
    bk1 = rel_bucket(kj - qi - qt)
    bk = jnp.stack([bk0, bk1]).astype(I32)
    assert qt + 1 >= REL_MAX_DIST
    far = rel_bucket(jnp.full((1,), -(qt + 1), I32))
    smem = pl.BlockSpec(memory_space=pltpu.SMEM)
    seq_blk = pl.BlockSpec((seq, dv), lambda h, b: (b, h))
    return pl.pallas_call(
        functools.partial(_attn_prompt_kernel, qt=qt, dh=dh, lam_init=lam_init),
        grid=(heads, batch),
        in_specs=[smem, smem, _const_spec(lam4.shape), _const_spec(bk.shape), _const_spec((dv, 1)),
                  seq_blk, seq_blk, seq_blk],
        out_specs=seq_blk,
        out_shape=jax.ShapeDtypeStruct((n, w), F32),
        scratch_shapes=[pltpu.VMEM((2, qt, 2 * qt), F32)],
        compiler_params=_params("arbitrary", "arbitrary"),
        name="attn_prompt",
    )(rel_bias, far, lam4, bk, sub_gain.T, q, k, v)


def _attn_sample_kernel(rb_ref, lam_ref, bkc_ref, bkn_ref, sg_ref, q_ref, kc_ref, vc_ref, kn_ref, vn_ref,
                        o_ref, bc_s, bn_s, *, t, dh, lam_init):
    h = pl.program_id(0)
    qz = _split_components(q_ref[...], dh)

    @pl.when(pl.program_id(1) == 0)
    def _():
        bc = _bias_from_buckets(bkc_ref[...], rb_ref, h) * LOG2E
        bn = _bias_from_buckets(bkn_ref[...], rb_ref, h) * LOG2E
        bc_s[...] = jnp.concatenate([bc, bc], axis=0)
        bn_s[...] = jnp.concatenate([bn, bn], axis=0)

    sc = lax.dot_general(qz, kc_ref[0].astype(BF16), NT_DIMS, preferred_element_type=F32)
    sn = lax.dot_general(qz, kn_ref[...].astype(BF16), NT_DIMS, preferred_element_type=F32)
    sc = sc + bc_s[...]
    sn = sn + bn_s[...]
    m = jnp.maximum(jnp.max(sc, axis=1, keepdims=True), jnp.max(sn, axis=1, keepdims=True))
    pc = jnp.exp2(sc - m)
    pn = jnp.exp2(sn - m)
    l = jnp.sum(pc, axis=1, keepdims=True) + jnp.sum(pn, axis=1, keepdims=True)
    acc = _dot(pc.astype(BF16), vc_ref[0].astype(BF16)) + _dot(pn.astype(BF16), vn_ref[...].astype(BF16))
    o = acc / l
    out = o[:t] - _lam(lam_ref, lam_init) * o[t:]
    o_ref[...] = _rms(out, sg_ref[...]) * (1.0 - lam_init)


def _attn_sample(q, k_new, v_new, cache_k, cache_v, rel_bias, lam4, sub_gain, *, batch, t, heads, dh,
                 lam_init):
    n, w = q.shape
    dv = w // heads
    past = cache_k.shape[1]
    assert past % CHUNK == 0 and t <= CHUNK
    qpos = past + jnp.arange(t, dtype=I32)[:, None]
    bkc = rel_bucket(jnp.arange(past, dtype=I32)[None, :] - qpos).astype(I32)
    bkn = rel_bucket(past + jnp.arange(t, dtype=I32)[None, :] - qpos).astype(I32)
    smem = pl.BlockSpec(memory_space=pltpu.SMEM)
    new = pl.BlockSpec((t, dv), lambda h, b: (b, h))
    old = pl.BlockSpec((1, past, dv), lambda h, b: (b, 0, h))
    return pl.pallas_call(
        functools.partial(_attn_sample_kernel, t=t, dh=dh, lam_init=lam_init),
        grid=(heads, batch),
        in_specs=[smem, _const_spec(lam4.shape), _const_spec(bkc.shape), _const_spec(bkn.shape),
                  _const_spec((1, dv)), new, old, old, new, new],
        out_specs=new,
        out_shape=jax.ShapeDtypeStruct((n, w), F32),
        scratch_shapes=[pltpu.VMEM((2 * t, past), F32), pltpu.VMEM((2 * t, t), F32)],
        compiler_params=_params("arbitrary", "arbitrary"),
        name="attn_sample",
    )(rel_bias, lam4, bkc, bkn, sub_gain, q, cache_k, cache_v, k_new, v_new)


def _cumsum_rows(x):
    c = x.shape[0]
    row = lax.broadcasted_iota(I32, x.shape, 0)
    s = 1
    while s < c:
        x = x + jnp.where(row >= s, pltpu.roll(x, s, axis=0), 0.0)
        s *= 2
    return x


def _hgrn_kernel(qh_ref, kb_ref, lf_ref, ih_ref, s0_ref, hg_ref, ob_ref, sf_ref, st_s, *,
                 heads, dk, dv, c, nsb):
    t = pl.program_id(1)

    @pl.when(t == 0)
    def _():
        for h in range(heads):
            st_s[h] = s0_ref[0, h].T

    tb = qh_ref.shape[0]
    row = lax.broadcasted_iota(I32, (c, c), 0)
    col = lax.broadcasted_iota(I32, (c, c), 1)
    causal = col <= row

    def chunk(ci, carry):
        r0 = pl.multiple_of(ci * c, c)
        for h in range(heads):
            rows = pl.ds(r0, c)
            q = qh_ref[rows, h * dk:(h + 1) * dk]
            k = kb_ref[rows, h * dk:(h + 1) * dk]
            v = ih_ref[rows, h * dv:(h + 1) * dv]
            b = _cumsum_rows(lf_ref[rows, h * dk:(h + 1) * dk])
            bl = b[c - 1:c]
            st = st_s[h]
            inter = lax.dot_general((q * jnp.exp(b)).astype(BF16), st.astype(BF16), NT_DIMS,
                                    preferred_element_type=F32)
            qs, ks = [], []
            for j in range(nsb):
                ref = b[j * HG_SUB + HG_SUB // 2:j * HG_SUB + HG_SUB // 2 + 1]
                qs.append(q * jnp.exp(b - ref))
                sub = slice(j * HG_SUB, (j + 1) * HG_SUB)
                ks.append(k[sub] * jnp.exp(ref - b[sub]))
            a_full = lax.dot_general(jnp.concatenate(qs, axis=0).astype(BF16),
                                     jnp.concatenate(ks, axis=0).astype(BF16), NT_DIMS,
                                     preferred_element_type=F32)
            att = jnp.zeros((c, c), F32)
            for j in range(nsb):
                att = jnp.where(col >= j * HG_SUB, a_full[j * c:(j + 1) * c], att)
            att = jnp.where(causal, att, 0.0)
            out = inter + _dot(att.astype(BF16), v.astype(BF16))
            ob_ref[rows, h * dv:(h + 1) * dv] = _rms(out, hg_ref[...])
            kdec = (k * jnp.exp(bl - b)).astype(BF16)
            st_s[h] = jnp.exp(bl) * st + lax.dot_general(v.astype(BF16), kdec, TN_DIMS,
                                                         preferred_element_type=F32)
        return carry

    lax.fori_loop(0, tb // c, chunk, 0, unroll=min(4, tb // c))

    @pl.when(t == pl.num_programs(1) - 1)
    def _():
        for h in range(heads):
            sf_ref[0, h] = st_s[h].T


def _hgrn(qh, kb, lf, ih, s0, hg_gain, *, batch, seq, heads, dk, dv, tb):
    n = qh.shape[0]
    c = min(CHUNK, seq)
    assert seq % tb == 0 and tb % c == 0 and c % HG_SUB == 0
    nt = seq // tb
    blk = lambda w: pl.BlockSpec((tb, w), lambda b, t: (b * nt + t, 0))
    st = pl.BlockSpec((1, heads, dk, dv), lambda b, t: (b, 0, 0, 0))
    return pl.pallas_call(
        functools.partial(_hgrn_kernel, heads=heads, dk=dk, dv=dv, c=c, nsb=c // HG_SUB),
        grid=(batch, nt),
        in_specs=[blk(heads * dk), blk(heads * dk), blk(heads * dk), blk(heads * dv), st,
                  _const_spec((1, dv))],
        out_specs=[blk(heads * dv), st],
        out_shape=[jax.ShapeDtypeStruct((n, heads * dv), F32),
                   jax.ShapeDtypeStruct((batch, heads, dk, dv), F32)],
        scratch_shapes=[pltpu.VMEM((heads, dv, dk), F32)],
        compiler_params=_params("arbitrary", "arbitrary"),
        name="hgrn",
    )(qh, kb, lf, ih, s0, hg_gain)


def _store_token_tiles(ref, x):
    rows, d = x.shape
    s = d // LANES
    for c in range(s):
        ref[pl.ds(c, rows, stride=s), :] = x[:, c * LANES:(c + 1) * LANES]


def _load_token_tiles(ref, rows, s):
    return jnp.concatenate([ref[pl.ds(c, rows, stride=s), :] for c in range(s)], axis=1)


def _token_tile(r, s):
    return pl.ds(pl.multiple_of(r * s, s), s)


def _ffn_prologue(x1, gf_ref, wrh_ref, wrl_ref, rbias_ref, x1_ref, xn_ref, lg_ref):
    x1_ref[...] = x1
    xn = _rms(x1, gf_ref[...])
    _store_token_tiles(xn_ref, xn)
    hi = xn.astype(BF16)
    lo = (xn - hi.astype(F32)).astype(BF16)
    nt = functools.partial(lax.dot_general, dimension_numbers=NT_DIMS, preferred_element_type=F32)
    lg_ref[...] = nt(wrh_ref[...], hi) + nt(wrh_ref[...], lo) + nt(wrl_ref[...], hi) + rbias_ref[...]


def _out_even_kernel(oa_ref, ob_ref, gs_ref, x_ref, w_ref, gf_ref, wrh_ref, wrl_ref, rbias_ref,
                     x1_ref, xn_ref, lg_ref):
    o = jnp.concatenate([oa_ref[...], ob_ref[...] * gs_ref[...]], axis=1).astype(BF16)
    x1 = x_ref[...] + _dot(o, w_ref[...])
    _ffn_prologue(x1, gf_ref, wrh_ref, wrl_ref, rbias_ref, x1_ref, xn_ref, lg_ref)


def _out_even(oa, ob, gs, x, w_bf, g_ffn, wrh, wrl, rbias, *, tm):
    n, d = x.shape
    w = oa.shape[1]
    nr = wrh.shape[0]
    row = lambda i: (i, 0)
    return pl.pallas_call(
        _out_even_kernel,
        grid=(n // tm,),
        in_specs=[pl.BlockSpec((tm, w), row), pl.BlockSpec((tm, w), row), pl.BlockSpec((tm, w), row),
                  pl.BlockSpec((tm, d), row), _const_spec(w_bf.shape), _const_spec((1, d)),
                  _const_spec(wrh.shape), _const_spec(wrl.shape), _const_spec(rbias.shape)],
        out_specs=[pl.BlockSpec((tm, d), row), pl.BlockSpec((tm * (d // LANES), LANES), row),
                   pl.BlockSpec((nr, tm), lambda i: (0, i))],
        out_shape=[jax.ShapeDtypeStruct((n, d), F32), jax.ShapeDtypeStruct((n * (d // LANES), LANES), F32),
                   jax.ShapeDtypeStruct((nr, n), F32)],
        compiler_params=_params("arbitrary"),
        name="out_even",
    )(oa, ob, gs, x, w_bf, g_ffn, wrh, wrl, rbias)


def _gelu(x):
    return 0.5 * x * (1.0 + jnp.tanh(math.sqrt(2.0 / math.pi) * (x + 0.044715 * (x * x * x))))


def _odd_kernel(x_ref, gm_ref, win_ref, vg_ref, wsp_ref, bsp_ref, wout_ref, gf_ref, wrh_ref, wrl_ref,
                rbias_ref, x1_ref, xn_ref, lg_ref, *rest, half, groups, l, emit_v):
    if emit_v:
        vn_ref, u_s, s_s = rest
    else:
        vn_ref = None
        u_s, vn_s, s_s = rest
    tm = x_ref.shape[0]
    x = x_ref[...]
    xn = _rms(x, gm_ref[...]).astype(BF16)
    cw = 512
    vbuf = vn_ref if emit_v else vn_s
    for cidx in range(half // cw):
        u_s[:, cidx * cw:(cidx + 1) * cw] = _gelu(_dot(xn, win_ref[:, cidx * cw:(cidx + 1) * cw]))
        vbuf[:, cidx * cw:(cidx + 1) * cw] = _gelu(
            _dot(xn, win_ref[:, half + cidx * cw:half + (cidx + 1) * cw]))
    vbuf[...] = _rms(vbuf[...], vg_ref[...])
    gw = half // groups
    row = lax.broadcasted_iota(I32, (l, l), 0)
    col = lax.broadcasted_iota(I32, (l, l), 1)
    for g in range(groups):
        wg = jnp.where(col <= row, wsp_ref[g], 0.0).astype(BF16)
        bg = bsp_ref[:, g:g + 1]
        for ci in range(tm // l):
            vv = vbuf[ci * l:(ci + 1) * l, g * gw:(g + 1) * gw].astype(BF16)
            s_s[ci * l:(ci + 1) * l, g * gw:(g + 1) * gw] = _dot(wg, vv) + bg
    y = _dot((u_s[...] * s_s[...]).astype(BF16), wout_ref[...])
    _ffn_prologue(x + y, gf_ref, wrh_ref, wrl_ref, rbias_ref, x1_ref, xn_ref, lg_ref)


def _odd_mixer(x, g_mix, win_bf, v_gain, wsp, bsp_t, wout_bf, g_ffn, wrh, wrl, rbias, *, l, tm, emit_v):
    n, d = x.shape
    half = wout_bf.shape[0]
    groups = wsp.shape[0]
    nr = wrh.shape[0]
    assert n % tm == 0 and tm % l == 0
    row = lambda i: (i, 0)
    out_specs = [pl.BlockSpec((tm, d), row), pl.BlockSpec((tm * (d // LANES), LANES), row),
                 pl.BlockSpec((nr, tm), lambda i: (0, i))]
    out_shape = [jax.ShapeDtypeStruct((n, d), F32), jax.ShapeDtypeStruct((n * (d // LANES), LANES), F32),
                 jax.ShapeDtypeStruct((nr, n), F32)]
    scratch = [pltpu.VMEM((tm, half), F32)]
    if emit_v:
        out_specs.append(pl.BlockSpec((tm, half), row))
        out_shape.append(jax.ShapeDtypeStruct((n, half), F32))
    else:
        scratch.append(pltpu.VMEM((tm, half), F32))
    scratch.append(pltpu.VMEM((tm, half), F32))
    return pl.pallas_call(
        functools.partial(_odd_kernel, half=half, groups=groups, l=l, emit_v=emit_v),
        grid=(n // tm,),
        in_specs=[pl.BlockSpec((tm, d), row), _const_spec((1, d)), _const_spec(win_bf.shape),
                  _const_spec((1, half)), _const_spec(wsp.shape), _const_spec(bsp_t.shape),
                  _const_spec(wout_bf.shape), _const_spec((1, d)), _const_spec(wrh.shape),
                  _const_spec(wrl.shape), _const_spec(rbias.shape)],
        out_specs=out_specs,
        out_shape=out_shape,
        scratch_shapes=scratch,
        compiler_params=_params("arbitrary"),
        name="odd_mixer",
    )(x, g_mix, win_bf, v_gain, wsp, bsp_t, wout_bf, g_ffn, wrh, wrl, rbias)


def _route_kernel(lg_ref, tri_ref, init_ref, e_ref, g_ref, r_ref, p_ref, cnt_ref, base_ref, tcnt_ref, run_s,
                  *, groups, epg):
    i = pl.program_id(0)

    @pl.when(i == 0)
    def _():
        run_s[...] = init_ref[...]

    lg = lg_ref[...]
    tr = lg.shape[1]
    gl = [lg[g:g + 1] for g in range(groups)]
    m = functools.reduce(jnp.maximum, gl)
    grp = jnp.full((1, tr), groups - 1, I32)
    for g in range(groups - 2, -1, -1):
        grp = jnp.where(gl[g] == m, g, grp)
    gate_g = 1.0 / functools.reduce(lambda a, b: a + b, [jnp.exp(x - m) for x in gl])
    sel = lg[SUBLANES + (groups - 1) * epg:SUBLANES + groups * epg]
    for g in range(groups - 2, -1, -1):
        sel = jnp.where(grp == g, lg[SUBLANES + g * epg:SUBLANES + (g + 1) * epg], sel)
    sub = lax.broadcasted_iota(I32, sel.shape, 0)
    v1 = jnp.max(sel, axis=0, keepdims=True)
    i1 = jnp.min(jnp.where(sel == v1, sub, epg), axis=0, keepdims=True)
    sel2 = jnp.where(sub == i1, -jnp.inf, sel)
    v2 = jnp.max(sel2, axis=0, keepdims=True)
    i2 = jnp.min(jnp.where(sel2 == v2, sub, epg), axis=0, keepdims=True)
    tt = jnp.exp(v2 - v1)
    g1 = gate_g / (1.0 + tt)
    g2 = gate_g * tt / (1.0 + tt)
    e1 = grp * epg + i1
    e2 = grp * epg + i2
    ne = groups * epg
    eidx = lax.broadcasted_iota(I32, (ne, tr), 0)
    oh1 = eidx == e1
    oh2 = eidx == e2
    cnt = jnp.where(oh1, 1.0, 0.0) + jnp.where(oh2, 1.0, 0.0)
    local = _dot(cnt.astype(BF16), tri_ref[...])
    before = run_s[:, 0:1] + local
    r1 = jnp.sum(jnp.where(oh1, before, 0.0), axis=0, keepdims=True)
    r2 = jnp.sum(jnp.where(oh2, before, 0.0), axis=0, keepdims=True)
    tile_cnt = jnp.broadcast_to(jnp.sum(cnt, axis=1, keepdims=True), run_s.shape)
    padded = jnp.ceil(tile_cnt * (1.0 / RUN_CHUNK)) * RUN_CHUNK
    offset = _cumsum_rows(padded) - padded
    where_local = offset[:, 0:1] + local
    p1 = jnp.sum(jnp.where(oh1, where_local, 0.0), axis=0, keepdims=True)
    p2 = jnp.sum(jnp.where(oh2, where_local, 0.0), axis=0, keepdims=True)
    base_ref[...] = run_s[...]
    tcnt_ref[...] = tile_cnt
    run_s[...] = run_s[...] + tile_cnt
    rows = lax.broadcasted_iota(I32, (SUBLANES, tr), 0)
    e_ref[...] = jnp.where(rows == 0, e1, jnp.where(rows == 1, e2, 0))
    g_ref[...] = jnp.where(rows == 0, g1, jnp.where(rows == 1, g2, 0.0))
    r_ref[...] = jnp.where(rows == 0, r1, jnp.where(rows == 1, r2, 0.0)).astype(I32)
    p_ref[...] = jnp.where(rows == 0, p1, jnp.where(rows == 1, p2, 0.0)).astype(I32)
    cnt_ref[...] = run_s[...]


def _route(lgt, init, *, groups, epg, tr):
    nr, n = lgt.shape
    assert n % tr == 0 and nr == SUBLANES + groups * epg
    ne = groups * epg
    tri = (jnp.arange(tr)[:, None] < jnp.arange(tr)[None, :]).astype(BF16)
    tok = pl.BlockSpec((SUBLANES, tr), lambda i: (0, i))
    per_tile = pl.BlockSpec((ne, LANES), lambda i: (i, 0))
    tile_tab = jax.ShapeDtypeStruct((n // tr * ne, LANES), F32)
    return pl.pallas_call(
        functools.partial(_route_kernel, groups=groups, epg=epg),
        grid=(n // tr,),
        in_specs=[pl.BlockSpec((nr, tr), lambda i: (0, i)), _const_spec((tr, tr)), _const_spec((ne, LANES))],
        out_specs=[tok, tok, tok, tok, pl.BlockSpec((ne, LANES), lambda i: (0, 0)), per_tile, per_tile],
        out_shape=[jax.ShapeDtypeStruct((SUBLANES, n), I32), jax.ShapeDtypeStruct((SUBLANES, n), F32),
                   jax.ShapeDtypeStruct((SUBLANES, n), I32), jax.ShapeDtypeStruct((SUBLANES, n), I32),
                   jax.ShapeDtypeStruct((ne, LANES), F32), tile_tab, tile_tab],
        scratch_shapes=[pltpu.VMEM((ne, LANES), F32)],
        compiler_params=_params("arbitrary"),
        name="route",
    )(lgt, tri, init)


ISSUE_UNROLL = 8
DISPATCH_SLOTS = 3


def _dispatch_kernel(seg_ref, dst_ref, *rest, s, bm, tiles):
    x_refs = rest[:len(tiles)]
    buf_ref, zero_s, xin_s, sem, isem, zsem = rest[len(tiles):]
    tp = dst_ref.shape[2]
    blk = bm * s

    def zero_fill():
        zero_s[...] = jnp.zeros(zero_s.shape, F32)

        def block_copy(b):
            return pltpu.make_async_copy(zero_s, buf_ref.at[pl.ds(pl.multiple_of(b * blk, blk), blk)], zsem)

        for e in range(seg_ref.shape[1]):
            @pl.when(seg_ref[1, e] > 0)
            def _():
                block_copy(seg_ref[0, e] // bm - 1).start()

            @pl.when(seg_ref[1, e] - seg_ref[2, e] > bm)
            def _():
                block_copy(seg_ref[0, e] // bm - 2).start()
        for e in range(seg_ref.shape[1]):
            @pl.when(seg_ref[1, e] > 0)
            def _():
                block_copy(0).wait()

            @pl.when(seg_ref[1, e] - seg_ref[2, e] > bm)
            def _():
                block_copy(0).wait()

        def tail_start(b, c):
            block_copy(b).start()
            return c

        def tail_wait(b, c):
            block_copy(b).wait()
            return c

        first_unused = seg_ref[0, seg_ref.shape[1] - 1] // bm
        n_blocks = buf_ref.shape[0] // blk
        lax.fori_loop(first_unused, n_blocks, tail_start, 0)
        lax.fori_loop(first_unused, n_blocks, tail_wait, 0)

    pl.when(pl.program_id(0) == 0)(zero_fill)

    i = pl.program_id(0)
    last = pl.num_programs(0) - 1
    rows = tp * s

    def tile_in(t, go):
        lo = 0
        for x_ref, nt in zip(x_refs, tiles):
            @pl.when((t >= lo) & (t < lo + nt))
            def _(x_ref=x_ref, lo=lo):
                cp = pltpu.make_async_copy(x_ref.at[pl.ds(pl.multiple_of((t - lo) * rows, rows), rows)],
                                           xin_s.at[t % DISPATCH_SLOTS], isem.at[t % DISPATCH_SLOTS])
                cp.start() if go else cp.wait()
            lo += nt

    def rows_out_wait(t):
        for kk in range(TOP_K):
            pltpu.make_async_copy(xin_s.at[t % DISPATCH_SLOTS], buf_ref.at[pl.ds(0, rows)],
                                  sem.at[t % DISPATCH_SLOTS]).wait()

    @pl.when(i == 0)
    def _():
        tile_in(i, True)

    @pl.when(i >= DISPATCH_SLOTS - 1)
    def _():
        rows_out_wait(i - (DISPATCH_SLOTS - 1))

    @pl.when(i < last)
    def _():
        tile_in(i + 1, True)

    tile_in(i, False)
    slot = i % DISPATCH_SLOTS

    def issue(r0, c):
        for u in range(ISSUE_UNROLL):
            r = r0 * ISSUE_UNROLL + u
            for kk in range(TOP_K):
                pltpu.make_async_copy(xin_s.at[slot, _token_tile(r, s)],
                                      buf_ref.at[_token_tile(dst_ref[0, kk, r], s)], sem.at[slot]
                                      ).start(priority=kk % 2)
        return c

    lax.fori_loop(0, tp // ISSUE_UNROLL, issue, 0)

    @pl.when(i == last)
    def _():
        for back in range(DISPATCH_SLOTS - 2, -1, -1):
            @pl.when(i >= back)
            def _():
                rows_out_wait(i - back)


def _dispatch(seg, dest3, xns, n_slots, *, tp, s, bm):
    assert tp % ISSUE_UNROLL == 0 and all(x.shape[0] % (tp * s) == 0 for x in xns)
    tiles = tuple(x.shape[0] // (tp * s) for x in xns)
    grid_spec = pltpu.PrefetchScalarGridSpec(
        num_scalar_prefetch=1,
        grid=(sum(tiles),),
        in_specs=[pl.BlockSpec((1, TOP_K, tp), lambda i, sg: (i, 0, 0), memory_space=pltpu.SMEM)]
        + [pl.BlockSpec(memory_space=pl.ANY)] * len(xns),
        out_specs=pl.BlockSpec(memory_space=pl.ANY),
        scratch_shapes=[pltpu.VMEM((bm * s, LANES), F32), pltpu.VMEM((DISPATCH_SLOTS, tp * s, LANES), F32),
                        pltpu.SemaphoreType.DMA((DISPATCH_SLOTS,)), pltpu.SemaphoreType.DMA((DISPATCH_SLOTS,)),
                        pltpu.SemaphoreType.DMA(())],
    )
    return pl.pallas_call(
        functools.partial(_dispatch_kernel, s=s, bm=bm, tiles=tiles),
        grid_spec=grid_spec,
        out_shape=jax.ShapeDtypeStruct((n_slots * s, LANES), F32),
        compiler_params=_params("arbitrary"),
        name="dispatch",
    )(seg, dest3, *xns)


def _expert_kernel(be_ref, nu_ref, x_ref, wg_ref, wu_ref, wd_ref, o_ref, wg_s, wu_s, wd_s):
    b = pl.program_id(0)
    s = wg_s.shape[0] // LANES
    prev = be_ref[jnp.maximum(b - 1, 0)]

    @pl.when((b == 0) | (be_ref[b] != prev))
    def _():
        wg_s[...] = wg_ref[0, 0].astype(BF16)
        wu_s[...] = wu_ref[0, 0].astype(BF16)
        wd_s[...] = wd_ref[0, 0].astype(BF16)

    @pl.when(b < nu_ref[0])
    def _():
        xb = _load_token_tiles(x_ref, x_ref.shape[0] // s, s).astype(BF16)
        gate = _dot(xb, wg_s[...])
        h = gate * _sigmoid(gate) * _dot(xb, wu_s[...])
        _store_token_tiles(o_ref, _dot(h.astype(BF16), wd_s[...]))

    @pl.when(b >= nu_ref[0])
    def _():
        o_ref[...] = jnp.zeros(o_ref.shape, F32)


def _experts(blk_expert, n_used, buf, w_gate, w_up, w_down, *, layer, bm):
    d, de = w_gate.shape[2:]
    s = d // LANES
    n_blocks = buf.shape[0] // (bm * s)
    rows = pl.BlockSpec((bm * s, LANES), lambda b, be, nu: (b, 0))
    used_rows = pl.BlockSpec((bm * s, LANES), lambda b, be, nu: (jnp.minimum(b, nu[0] - 1), 0))
    grid_spec = pltpu.PrefetchScalarGridSpec(
        num_scalar_prefetch=2,
        grid=(n_blocks,),
        in_specs=[used_rows,
                  pl.BlockSpec((1, 1, d, de), lambda b, be, nu: (layer, be[b], 0, 0)),
                  pl.BlockSpec((1, 1, d, de), lambda b, be, nu: (layer, be[b], 0, 0)),
                  pl.BlockSpec((1, 1, de, d), lambda b, be, nu: (layer, be[b], 0, 0))],
        out_specs=rows,
        scratch_shapes=[pltpu.VMEM((d, de), BF16), pltpu.VMEM((d, de), BF16), pltpu.VMEM((de, d), BF16)],
    )
    return pl.pallas_call(
        _expert_kernel,
        grid_spec=grid_spec,
        out_shape=jax.ShapeDtypeStruct(buf.shape, F32),
        compiler_params=_params("arbitrary"),
        name="experts",
    )(blk_expert, n_used, buf, w_gate, w_up, w_down)


def _combine_kernel(tab_ref, nxt_ref, pos_ref, gate_ref, x_ref, yb_ref, o_ref, stg_s, tt_s, sem, *, ahead):
    i = pl.program_id(0)
    tq, d = x_ref.shape
    s = d // LANES
    piece = RUN_CHUNK * s
    slot = i % 2

    def run_copies(ref, dst_slot, go):
        for e in range(ref.shape[2]):
            def body(j, c):
                src = pl.multiple_of((ref[0, 0, e] + j * RUN_CHUNK) * s, s)
                dst = pl.multiple_of((ref[0, 2, e] + j * RUN_CHUNK) * s, piece)
                cp = pltpu.make_async_copy(yb_ref.at[pl.ds(src, piece)],
                                           stg_s.at[dst_slot, pl.ds(dst, piece)], sem.at[dst_slot])
                cp.start() if go else cp.wait()
                return c
            lax.fori_loop(0, ref[0, 1, e], body, 0)

    if ahead:
        @pl.when(i == 0)
        def _():
            run_copies(tab_ref, 0, True)

        @pl.when(i + 1 < pl.num_programs(0))
        def _():
            run_copies(nxt_ref, 1 - slot, True)
    else:
        run_copies(tab_ref, slot, True)

    run_copies(tab_ref, slot, False)

    def assemble(r0, c):
        for u in range(ISSUE_UNROLL):
            r = r0 * ISSUE_UNROLL + u
            acc = gate_ref[0, 0, r] * stg_s[slot, _token_tile(pos_ref[0, 0, r], s), :]
            for kk in range(1, TOP_K):
                acc = acc + gate_ref[0, kk, r] * stg_s[slot, _token_tile(pos_ref[0, kk, r], s), :]
            tt_s[_token_tile(r, s), :] = acc
        return c

    lax.fori_loop(0, tq // ISSUE_UNROLL, assemble, 0)
    o_ref[...] = x_ref[...] + _load_token_tiles(tt_s, tq, s)


def _combine(tab, pos3, gates3, x1, yb, *, tq):
    n, d = x1.shape
    s = d // LANES
    nt = n // tq
    ne = tab.shape[2]
    assert tq % ISSUE_UNROLL == 0
    stage_rows = tq * TOP_K + ne * RUN_CHUNK
    smem = lambda shape, imap: pl.BlockSpec(shape, imap, memory_space=pltpu.SMEM)
    return pl.pallas_call(
        functools.partial(_combine_kernel, ahead=nt > 2),
        grid=(nt,),
        in_specs=[smem((1, 3, ne), lambda i: (i, 0, 0)),
                  smem((1, 3, ne), lambda i: (jnp.minimum(i + 1, nt - 1), 0, 0)),
                  smem((1, TOP_K, tq), lambda i: (i, 0, 0)),
                  smem((1, TOP_K, tq), lambda i: (i, 0, 0)),
                  pl.BlockSpec((tq, d), lambda i: (i, 0)),
                  pl.BlockSpec(memory_space=pl.ANY)],
        out_specs=pl.BlockSpec((tq, d), lambda i: (i, 0)),
        out_shape=jax.ShapeDtypeStruct((n, d), F32),
        scratch_shapes=[pltpu.VMEM((2, stage_rows * s, LANES), F32), pltpu.VMEM((tq * s, LANES), F32),
                        pltpu.SemaphoreType.DMA((2,))],
        compiler_params=_params("arbitrary"),
        name="combine",
    )(tab, tab, pos3, gates3, x1, yb)


def _moe(streams, w_gate, w_up, w_down, *, layer, groups, epg):
    d = streams[0][0].shape[1]
    ne = groups * epg
    routed = []
    taken = jnp.zeros((ne, LANES), F32)
    for x1, _, lgt in streams:
        rtile = min(512, x1.shape[0])
        routed.append(_route(lgt, taken, groups=groups, epg=epg, tr=rtile) + (rtile,))
        taken = routed[-1][4]
    n_rows = sum(x1.shape[0] for x1, _, _ in streams) * TOP_K
    counts = taken[:, 0].astype(I32)
    bm = MOE_BLOCK if n_rows >= 2 * ne * MOE_BLOCK else MOE_BLOCK_SMALL
    padded = (counts + RUN_CHUNK + bm - 1) // bm * bm
    pend = jnp.cumsum(padded)
    pstart = pend - padded
    n_blocks = -(-(n_rows + ne * RUN_CHUNK) // bm) + ne
    eids = jnp.arange(ne, dtype=I32)
    blk_row = jnp.arange(n_blocks, dtype=I32)[:, None] * bm
    blk_expert = jnp.minimum(jnp.sum((pend[None, :] <= blk_row).astype(I32), axis=1), ne - 1)
    n_used = (pend[-1:] // bm).astype(I32)
    seg_tab = jnp.stack([pend, padded, counts]).astype(I32)
    dtile = min(r[-1] for r in routed)
    dests = []
    for (x1, _, _), (e8, g8, r8, p8, cnt, base, tcnt, rtile) in zip(streams, routed):
        seg = jnp.sum(jnp.where(e8[:TOP_K, :, None] == eids, pstart, 0), axis=-1)
        dest = seg + r8[:TOP_K]
        dests.append(dest.reshape(TOP_K, x1.shape[0] // dtile, dtile).transpose(1, 0, 2))
    buf = _dispatch(seg_tab, jnp.concatenate(dests, axis=0), [xn for _, xn, _ in streams], n_blocks * bm,
                    tp=dtile, s=d // LANES, bm=bm)
    yb = _experts(blk_expert, n_used, buf, w_gate, w_up, w_down, layer=layer, bm=bm)
    outs = []
    for (x1, _, _), (e8, g8, r8, p8, cnt, base, tcnt, rtile) in zip(streams, routed):
        nrt = x1.shape[0] // rtile
        run_start = pstart[None, :] + base.reshape(nrt, ne, LANES)[:, :, 0].astype(I32)
        pieces = (tcnt.reshape(nrt, ne, LANES)[:, :, 0].astype(I32) + RUN_CHUNK - 1) // RUN_CHUNK
        stage = (jnp.cumsum(pieces, axis=1) - pieces) * RUN_CHUNK
        tab = jnp.stack([run_start, pieces, stage], axis=1)
        by_tile = lambda a: a[:TOP_K].reshape(TOP_K, nrt, rtile).transpose(1, 0, 2)
        outs.append(_combine(tab, by_tile(p8), by_tile(g8), x1, yb, tq=rtile))
    return outs


def _router_weights(wg, bg, we, be):
    d, groups = wg.shape
    epg = we.shape[2]
    assert groups <= SUBLANES and epg == SUBLANES
    pad = jnp.zeros((SUBLANES - groups, d), F32)
    wr = jnp.concatenate([wg.T, pad, we.transpose(0, 2, 1).reshape(groups * epg, d)], axis=0)
    rb = jnp.concatenate([bg, jnp.zeros((SUBLANES - groups,), F32), be.reshape(-1)])[:, None]
    hi = wr.astype(BF16)
    lo = (wr - hi.astype(F32)).astype(BF16)
    return hi, lo, rb, groups, epg


def kernel(x_prompt, x_sample, cache_attn_k, cache_attn_v, state_hgrn, rel_bias, norm_mix, norm_ffn,
           w_in_even, w_out_even, q_norm_gain, k_norm_gain, lam_q1, lam_k1, lam_q2, lam_k2, da_out_gain,
           hgrn_lb_logits, hgrn_out_gain, w_in_odd, sgu_v_gain, sgu_w, sgu_b, w_out_odd,
           router_group_w, router_group_b, router_expert_w, router_expert_b,
           expert_w_gate, expert_w_up, expert_w_down):
    bp, tp, d = x_prompt.shape
    bs, ts, _ = x_sample.shape
    depth = norm_mix.shape[0]
    _, _, past, da_heads, _, da_dh = cache_attn_k.shape
    da_dv = cache_attn_v.shape[-1]
    _, _, hg_heads, hg_dk, hg_dv = state_hgrn.shape
    width = da_heads * da_dv
    assert width == da_heads * 2 * da_dh == hg_heads * hg_dk == hg_heads * hg_dv
    assert da_dv == LANES and hg_dk == LANES and hg_dv == LANES

    lb_all = jnp.cumsum(jax.nn.softmax(hgrn_lb_logits.astype(F32), axis=0), axis=0)
    gid = jnp.arange(width) // da_dh
    pm = jnp.where(gid[:, None] == gid[None, :], 1.0 / da_dh, 0.0).astype(BF16)

    xs = {"p": x_prompt.reshape(bp * tp, d), "s": x_sample.reshape(bs * ts, d)}
    dims = {"p": (bp, tp), "s": (bs, ts)}
    outs = {"p": {}, "s": {}}
    kp_l, vp_l, ks_l, vs_l, sp_l, ss_l, sgu_l = [], [], [], [], [], [], []

    for layer in range(depth):
        j = layer // 2
        wrh, wrl, rbias, groups, epg = _router_weights(
            router_group_w[layer], router_group_b[layer], router_expert_w[layer], router_expert_b[layer])
        g_mix = norm_mix[layer][None, :]
        g_ffn = norm_ffn[layer][None, :]
        if layer % 2 == 0:
            lam_init = 0.8 - 0.6 * math.exp(-0.3 * layer)
            w_in_bf = w_in_even[j].astype(BF16)
            w_out_bf = w_out_even[j].astype(BF16)
            reps = width // da_dh
            qg = jnp.tile(q_norm_gain[j], reps)[None, :]
            kg = jnp.tile(k_norm_gain[j], reps)[None, :]
            lam4 = jnp.stack([lam_q1[j], lam_k1[j], lam_q2[j], lam_k2[j]])
            sub_gain = da_out_gain[j][None, :]
            hg_gain = hgrn_out_gain[j][None, :]
            lb = lb_all[j][None, :]
            for key in ("p", "s"):
                b, t = dims[key]
                x = xs[key]
                q, k, v, qh, kb, lf, ih, gs, k5, v4 = _in_even(
                    x, g_mix, w_in_bf, pm, qg, kg, lb, width=width, q_scale=da_dh ** -0.5 * LOG2E,
                    tm=min(512, b * t), batch=b, seq=t, heads=da_heads, dh=da_dh)
                if key == "p":
                    oa = _attn_prompt(q, k, v, rel_bias, lam4, sub_gain, batch=b, seq=t, heads=da_heads,
                                      dh=da_dh, lam_init=lam_init, qt=min(256, t))
                    s0 = jnp.zeros((b, hg_heads, hg_dk, hg_dv), F32)
                    ob, s_new = _hgrn(qh, kb, lf, ih, s0, hg_gain, batch=b, seq=t, heads=hg_heads,
                                      dk=hg_dk, dv=hg_dv, tb=min(512, t))
                    kp_l.append(k5)
                    vp_l.append(v4)
                    sp_l.append(s_new)
                else:
                    ck = cache_attn_k[j].reshape(b, past, width)
                    cv = cache_attn_v[j].reshape(b, past, width)
                    oa = _attn_sample(q, k, v, ck, cv, rel_bias, lam4, sub_gain, batch=b, t=t,
                                      heads=da_heads, dh=da_dh, lam_init=lam_init)
                    ob, s_new = _hgrn(qh, kb, lf, ih, state_hgrn[j], hg_gain, batch=b, seq=t,
                                      heads=hg_heads, dk=hg_dk, dv=hg_dv, tb=t)
                    ks_l.append(k5)
                    vs_l.append(v4)
                    ss_l.append(s_new)
                outs[key] = _out_even(oa, ob, gs, x, w_out_bf, g_ffn, wrh, wrl, rbias, tm=min(512, b * t))
        else:
            w_in_bf = w_in_odd[j].astype(BF16)
            w_out_bf = w_out_odd[j].astype(BF16)
            v_gain = sgu_v_gain[j][None, :]
            for key in ("p", "s"):
                b, t = dims[key]
                l = min(SGU_CHUNK, t)
                res = _odd_mixer(xs[key], g_mix, w_in_bf, v_gain, sgu_w[j][:, :l, :l], sgu_b[j][:, :l].T,
                                 w_out_bf, g_ffn, wrh, wrl, rbias, l=l, tm=min(512, b * t), emit_v=(key == "s"))
                outs[key] = res[:3]
                if key == "s":
                    sgu_l.append(res[3].reshape(b, t, -1))
        xs["p"], xs["s"] = _moe([outs["p"], outs["s"]], expert_w_gate, expert_w_up, expert_w_down,
                                layer=layer, groups=groups, epg=epg)

    return (xs["p"].reshape(bp, tp, d), xs["s"].reshape(bs, ts, d), jnp.stack(kp_l), jnp.stack(vp_l),
            jnp.stack(ks_l), jnp.stack(vs_l), jnp.stack(sp_l), jnp.stack(ss_l), jnp.stack(sgu_l))
```

```python
import functools
import math

import jax
import jax.numpy as jnp
from jax import lax
from jax.experimental import pallas as pl
from jax.experimental.pallas import tpu as pltpu

F32 = jnp.float32
BF16 = jnp.bfloat16
I32 = jnp.int32

EPS = 1e-6
LOG2E = math.log2(math.e)
CHUNK = 64
SGU_CHUNK = 128
REL_BUCKETS = 32
REL_MAX_DIST = 128
TOP_K = 2
MOE_BLOCK = 512
MOE_BLOCK_SMALL = 128
RUN_CHUNK = 16
HG_SUB = 16

LANES = 128
SUBLANES = 8
VMEM_LIMIT = 56 * 1024 * 1024

NT_DIMS = (((1,), (1,)), ((), ()))
TN_DIMS = (((0,), (0,)), ((), ()))


def _params(*sem):
    return pltpu.CompilerParams(dimension_semantics=sem, vmem_limit_bytes=VMEM_LIMIT)


def _const_spec(shape):
    nd = len(shape)
    return pl.BlockSpec(shape, lambda *_: (0,) * nd, pipeline_mode=pl.Buffered(1))


def _sigmoid(x):
    return 1.0 / (1.0 + jnp.exp(-x))


def _rms(x, g):
    return x * lax.rsqrt(jnp.mean(x * x, axis=-1, keepdims=True) + EPS) * g


def _dot(a, b):
    return jnp.dot(a, b, preferred_element_type=F32)


def rel_bucket(rel):
    half = REL_BUCKETS // 2
    max_exact = half // 2
    ret = (rel > 0).astype(I32) * half
    n = jnp.abs(rel)
    nf = jnp.maximum(n, 1).astype(F32)
    large = max_exact + (jnp.log(nf / max_exact) / math.log(REL_MAX_DIST / max_exact)
                         * (half - max_exact)).astype(I32)
    large = jnp.minimum(large, half - 1)
    return ret + jnp.where(n < max_exact, n, large)


def _in_even_kernel(x_ref, g_ref, w_ref, pm_ref, qg_ref, kg_ref, lb_ref,
                    q_ref, k_ref, v_ref, qh_ref, kb_ref, lf_ref, ih_ref, gs_ref, k5_ref, v4_ref, *, width,
                    q_scale):
    xn = _rms(x_ref[...], g_ref[...]).astype(BF16)

    def proj(c):
        return _dot(xn, w_ref[:, c * width:(c + 1) * width])

    def group_norm(y, gain):
        ms = _dot((y * y).astype(BF16), pm_ref[...])
        return y * lax.rsqrt(ms + EPS) * gain

    q_ref[...] = group_norm(proj(0), qg_ref[...]) * q_scale
    kn = group_norm(proj(1), kg_ref[...])
    k_ref[...] = kn
    bb, ts, heads, _, dh = k5_ref.shape
    for h in range(heads):
        for c in range(2):
            piece = kn[:, (2 * h + c) * dh:(2 * h + c + 1) * dh]
            k5_ref[:, :, h, c, :] = piece.reshape(bb, ts, dh)
    vv = proj(2)
    v_ref[...] = vv
    for h in range(heads):
        v4_ref[:, :, h, :] = vv[:, h * 2 * dh:(h + 1) * 2 * dh].reshape(bb, ts, 2 * dh)
    yq = proj(3)
    qh_ref[...] = yq * _sigmoid(yq)
    zf = proj(4)
    lb = lb_ref[...]
    lf_ref[...] = jnp.log(lb + (1.0 - lb) * _sigmoid(zf))
    kb_ref[...] = (1.0 - lb) * _sigmoid(-zf)
    ih_ref[...] = proj(5)
    yg = proj(6)
    gs_ref[...] = yg * _sigmoid(yg)


def _in_even(x, g_mix, w_bf, pm, qg, kg, lb, *, width, q_scale, tm, batch, seq, heads, dh):
    n, d = x.shape
    assert n % tm == 0 and (tm % seq == 0 or seq % tm == 0)
    row = lambda i: (i, 0)
    out = jax.ShapeDtypeStruct((n, width), F32)
    bb, ts = max(tm // seq, 1), min(tm, seq)
    per = seq // ts
    k5_spec = pl.BlockSpec((bb, ts, heads, 2, dh), lambda i: (i // per, i % per, 0, 0, 0))
    v4_spec = pl.BlockSpec((bb, ts, heads, 2 * dh), lambda i: (i // per, i % per, 0, 0))
    return pl.pallas_call(
        functools.partial(_in_even_kernel, width=width, q_scale=q_scale),
        grid=(n // tm,),
        in_specs=[pl.BlockSpec((tm, d), row), _const_spec((1, d)), _const_spec(w_bf.shape),
                  _const_spec(pm.shape), _const_spec((1, width)), _const_spec((1, width)),
                  _const_spec((1, width))],
        out_specs=[pl.BlockSpec((tm, width), row)] * 8 + [k5_spec, v4_spec],
        out_shape=[out] * 8 + [jax.ShapeDtypeStruct((batch, seq, heads, 2, dh), F32),
                               jax.ShapeDtypeStruct((batch, seq, heads, 2 * dh), F32)],
        compiler_params=_params("arbitrary"),
        name="in_even",
    )(x, g_mix, w_bf, pm, qg, kg, lb)


def _bias_from_buckets(bk, rb_ref, h):
    b = jnp.zeros(bk.shape, F32)
    for u in range(REL_BUCKETS):
        b = jnp.where(bk == u, rb_ref[u, h], b)
    return jnp.where(bk < 0, -jnp.inf, b)


def _lam(lam_ref, lam_init):
    r = lam_ref[...]
    s1 = jnp.sum(r[0:1] * r[1:2], axis=1, keepdims=True)
    s2 = jnp.sum(r[2:3] * r[3:4], axis=1, keepdims=True)
    return jnp.exp(s1) - jnp.exp(s2) + lam_init


def _split_components(q, dh):
    lane = lax.broadcasted_iota(I32, q.shape, 1)
    q0 = jnp.where(lane < dh, q, 0.0)
    q1 = jnp.where(lane >= dh, q, 0.0)
    return jnp.concatenate([q0, q1], axis=0).astype(BF16)


def _attn_prompt_kernel(rb_ref, far_ref, lam_ref, bk_ref, sg_ref, q_ref, k_ref, v_ref, o_ref, bias_s, *,
                        qt, dh, lam_init):
    h = pl.program_id(0)
    t = q_ref.shape[0]
    kb = k_ref[...].astype(BF16)
    vt = v_ref[...].T.astype(BF16)
    qtr = q_ref[...].T
    sub = lax.broadcasted_iota(I32, (2 * dh, qt), 0)

    @pl.when(pl.program_id(1) == 0)
    def _():
        for d in range(2):
            b = _bias_from_buckets(bk_ref[d], rb_ref, h) * LOG2E
            bias_s[d] = jnp.concatenate([b, b], axis=1)

    bias = [bias_s[0], bias_s[1]]
    far = rb_ref[far_ref[0], h] * LOG2E
    lam = _lam(lam_ref, lam_init)
    gain = sg_ref[...] * (1.0 - lam_init)
    for i in range(t // qt):
        qi = qtr[:, i * qt:(i + 1) * qt]
        qz = jnp.concatenate([jnp.where(sub < dh, qi, 0.0), jnp.where(sub >= dh, qi, 0.0)],
                             axis=1).astype(BF16)
        n = (i + 1) * qt
        parts = [(n - qt, n)]
        s = [_dot(kb[n - qt:n], qz) + bias[0]]
        shift = [0.0]
        if i >= 1:
            parts.append((n - 2 * qt, n - qt))
            s.append(_dot(kb[n - 2 * qt:n - qt], qz) + bias[1])
            shift.append(0.0)
        if i >= 2:
            parts.append((0, n - 2 * qt))
            s.append(_dot(kb[:n - 2 * qt], qz))
            shift.append(far)
        m = functools.reduce(jnp.maximum, [jnp.max(x, axis=0, keepdims=True) + c for x, c in zip(s, shift)])
        p = [jnp.exp2(x - (m - c)) for x, c in zip(s, shift)]
        l = functools.reduce(lambda a, b: a + b, [jnp.sum(x, axis=0, keepdims=True) for x in p])
        acc = functools.reduce(lambda a, b: a + b,
                               [_dot(vt[:, lo:hi], x.astype(BF16)) for (lo, hi), x in zip(parts, p)])
        o = acc / l
        out = o[:, :qt] - lam * o[:, qt:]
        out = out * lax.rsqrt(jnp.mean(out * out, axis=0, keepdims=True) + EPS) * gain
        o_ref[i * qt:(i + 1) * qt, :] = out.T


def _attn_prompt(q, k, v, rel_bias, lam4, sub_gain, *, batch, seq, heads, dh, lam_init, qt):
    n, w = q.shape
    dv = w // heads
    assert dv == 2 * dh and seq % qt == 0 and qt % CHUNK == 0
    kj = jnp.arange(qt, dtype=I32)[:, None]
    qi = jnp.arange(qt, dtype=I32)[None, :]
    bk0 = jnp.where((kj // CHUNK) <= (qi // CHUNK), rel_bucket(kj - qi), -1)
    bk1 = rel_bucket(kj - qi - qt)
    bk = jnp.stack([bk0, bk1]).astype(I32)
    assert qt + 1 >= REL_MAX_DIST
    far = rel_bucket(jnp.full((1,), -(qt + 1), I32))
    smem = pl.BlockSpec(memory_space=pltpu.SMEM)
    seq_blk = pl.BlockSpec((seq, dv), lambda h, b: (b, h))
    return pl.pallas_call(
        functools.partial(_attn_prompt_kernel, qt=qt, dh=dh, lam_init=lam_init),
        grid=(heads, batch),
        in_specs=[smem, smem, _const_spec(lam4.shape), _const_spec(bk.shape), _const_spec((dv, 1)),
                  seq_blk, seq_blk, seq_blk],
        out_specs=seq_blk,
        out_shape=jax.ShapeDtypeStruct((n, w), F32),
        scratch_shapes=[pltpu.VMEM((2, qt, 2 * qt), F32)],
        compiler_params=_params("arbitrary", "arbitrary"),
        name="attn_prompt",
    )(rel_bias, far, lam4, bk, sub_gain.T, q, k, v)


def _attn_sample_kernel(rb_ref, lam_ref, bkc_ref, bkn_ref, sg_ref, q_ref, kc_ref, vc_ref, kn_ref, vn_ref,
                        o_ref, bc_s, bn_s, *, t, dh, lam_init, heads):
    h = pl.program_id(1)
    qz = _split_components(q_ref[...], dh)

    @pl.when((pl.program_id(0) == 0) & (h == 0))
    def _():
        for hh in range(heads):
            bc = _bias_from_buckets(bkc_ref[...], rb_ref, hh) * LOG2E
            bn = _bias_from_buckets(bkn_ref[...], rb_ref, hh) * LOG2E
            bc_s[hh] = jnp.concatenate([bc, bc], axis=0)
            bn_s[hh] = jnp.concatenate([bn, bn], axis=0)

    sc = lax.dot_general(qz, kc_ref[0].astype(BF16), NT_DIMS, preferred_element_type=F32)
    sn = lax.dot_general(qz, kn_ref[...].astype(BF16), NT_DIMS, preferred_element_type=F32)
    sc = sc + bc_s[h]
    sn = sn + bn_s[h]
    m = jnp.maximum(jnp.max(sc, axis=1, keepdims=True), jnp.max(sn, axis=1, keepdims=True))
    pc = jnp.exp2(sc - m)
    pn = jnp.exp2(sn - m)
    l = jnp.sum(pc, axis=1, keepdims=True) + jnp.sum(pn, axis=1, keepdims=True)
    past = vc_ref.shape[0] // heads
    vc = vc_ref[pl.ds(h, past, stride=heads), :]
    acc = _dot(pc.astype(BF16), vc.astype(BF16)) + _dot(pn.astype(BF16), vn_ref[...].astype(BF16))
    o = acc / l
    out = o[:t] - _lam(lam_ref, lam_init) * o[t:]
    o_ref[...] = _rms(out, sg_ref[...]) * (1.0 - lam_init)


def _attn_sample(q, k_new, v_new, cache_k, cache_v, rel_bias, lam4, sub_gain, *, batch, t, heads, dh,
                 lam_init):
    n, w = q.shape
    dv = w // heads
    past = cache_k.shape[1]
    assert past % CHUNK == 0 and t <= CHUNK
    qpos = past + jnp.arange(t, dtype=I32)[:, None]
    bkc = rel_bucket(jnp.arange(past, dtype=I32)[None, :] - qpos).astype(I32)
    bkn = rel_bucket(past + jnp.arange(t, dtype=I32)[None, :] - qpos).astype(I32)
    smem = pl.BlockSpec(memory_space=pltpu.SMEM)
    new = pl.BlockSpec((t, dv), lambda b, h: (b, h))
    old_k = pl.BlockSpec((1, past, dv), lambda b, h: (b, 0, h))
    old_v = pl.BlockSpec((past * heads, dv), lambda b, h: (b, 0))
    return pl.pallas_call(
        functools.partial(_attn_sample_kernel, t=t, dh=dh, lam_init=lam_init, heads=heads),
        grid=(batch, heads),
        in_specs=[smem, _const_spec(lam4.shape), _const_spec(bkc.shape), _const_spec(bkn.shape),
                  _const_spec((1, dv)), new, old_k, old_v, new, new],
        out_specs=new,
        out_shape=jax.ShapeDtypeStruct((n, w), F32),
        scratch_shapes=[pltpu.VMEM((heads, 2 * t, past), F32), pltpu.VMEM((heads, 2 * t, t), F32)],
        compiler_params=_params("arbitrary", "arbitrary"),
        name="attn_sample",
    )(rel_bias, lam4, bkc, bkn, sub_gain, q, cache_k, cache_v, k_new, v_new)


def _cumsum_rows(x):
    c = x.shape[0]
    row = lax.broadcasted_iota(I32, x.shape, 0)
    s = 1
    while s < c:
        x = x + jnp.where(row >= s, pltpu.roll(x, s, axis=0), 0.0)
        s *= 2
    return x


def _hgrn_kernel(qh_ref, kb_ref, lf_ref, ih_ref, s0_ref, hg_ref, ob_ref, sf_ref, st_s, *,
                 heads, dk, dv, c, nsb):
    t = pl.program_id(1)

    @pl.when(t == 0)
    def _():
        for h in range(heads):
            st_s[h] = s0_ref[0, h].T

    tb = qh_ref.shape[0]
    row = lax.broadcasted_iota(I32, (c, c), 0)
    col = lax.broadcasted_iota(I32, (c, c), 1)
    causal = col <= row

    def chunk(ci, carry):
        r0 = pl.multiple_of(ci * c, c)
        for h in range(heads):
            rows = pl.ds(r0, c)
            q = qh_ref[rows, h * dk:(h + 1) * dk]
            k = kb_ref[rows, h * dk:(h + 1) * dk]
            v = ih_ref[rows, h * dv:(h + 1) * dv]
            b = _cumsum_rows(lf_ref[rows, h * dk:(h + 1) * dk])
            bl = b[c - 1:c]
            st = st_s[h]
            inter = lax.dot_general((q * jnp.exp(b)).astype(BF16), st.astype(BF16), NT_DIMS,
                                    preferred_element_type=F32)
            qs, ks = [], []
            for j in range(nsb):
                ref = b[j * HG_SUB + HG_SUB // 2:j * HG_SUB + HG_SUB // 2 + 1]
                qs.append(q * jnp.exp(b - ref))
                sub = slice(j * HG_SUB, (j + 1) * HG_SUB)
                ks.append(k[sub] * jnp.exp(ref - b[sub]))
            a_full = lax.dot_general(jnp.concatenate(qs, axis=0).astype(BF16),
                                     jnp.concatenate(ks, axis=0).astype(BF16), NT_DIMS,
                                     preferred_element_type=F32)
            att = jnp.zeros((c, c), F32)
            for j in range(nsb):
                att = jnp.where(col >= j * HG_SUB, a_full[j * c:(j + 1) * c], att)
            att = jnp.where(causal, att, 0.0)
            out = inter + _dot(att.astype(BF16), v.astype(BF16))
            ob_ref[rows, h * dv:(h + 1) * dv] = _rms(out, hg_ref[...])
            kdec = (k * jnp.exp(bl - b)).astype(BF16)
            st_s[h] = jnp.exp(bl) * st + lax.dot_general(v.astype(BF16), kdec, TN_DIMS,
                                                         preferred_element_type=F32)
        return carry

    lax.fori_loop(0, tb // c, chunk, 0, unroll=min(4, tb // c))

    @pl.when(t == pl.num_programs(1) - 1)
    def _():
        for h in range(heads):
            sf_ref[0, h] = st_s[h].T


def _hgrn(qh, kb, lf, ih, s0, hg_gain, *, batch, seq, heads, dk, dv, tb):
    n = qh.shape[0]
    c = min(CHUNK, seq)
    assert seq % tb == 0 and tb % c == 0 and c % HG_SUB == 0
    nt = seq // tb
    blk = lambda w: pl.BlockSpec((tb, w), lambda b, t: (b * nt + t, 0))
    st = pl.BlockSpec((1, heads, dk, dv), lambda b, t: (b, 0, 0, 0))
    return pl.pallas_call(
        functools.partial(_hgrn_kernel, heads=heads, dk=dk, dv=dv, c=c, nsb=c // HG_SUB),
        grid=(batch, nt),
        in_specs=[blk(heads * dk), blk(heads * dk), blk(heads * dk), blk(heads * dv), st,
                  _const_spec((1, dv))],
        out_specs=[blk(heads * dv), st],
        out_shape=[jax.ShapeDtypeStruct((n, heads * dv), F32),
                   jax.ShapeDtypeStruct((batch, heads, dk, dv), F32)],
        scratch_shapes=[pltpu.VMEM((heads, dv, dk), F32)],
        compiler_params=_params("arbitrary", "arbitrary"),
        name="hgrn",
    )(qh, kb, lf, ih, s0, hg_gain)


def _store_token_tiles(ref, x):
    rows, d = x.shape
    s = d // LANES
    for c in range(s):
        ref[pl.ds(c, rows, stride=s), :] = x[:, c * LANES:(c + 1) * LANES]


def _load_token_tiles(ref, rows, s):
    return jnp.concatenate([ref[pl.ds(c, rows, stride=s), :] for c in range(s)], axis=1)


def _token_tile(r, s):
    return pl.ds(pl.multiple_of(r * s, s), s)


def _ffn_prologue(x1, gf_ref, wrh_ref, wrl_ref, rbias_ref, x1_ref, xn_ref, lg_ref):
    x1_ref[...] = x1
    xn = _rms(x1, gf_ref[...])
    _store_token_tiles(xn_ref, xn)
    hi = xn.astype(BF16)
    lo = (xn - hi.astype(F32)).astype(BF16)
    nt = functools.partial(lax.dot_general, dimension_numbers=NT_DIMS, preferred_element_type=F32)
    lg_ref[...] = nt(wrh_ref[...], hi) + nt(wrh_ref[...], lo) + nt(wrl_ref[...], hi) + rbias_ref[...]


def _out_even_kernel(oa_ref, ob_ref, gs_ref, x_ref, w_ref, gf_ref, wrh_ref, wrl_ref, rbias_ref,
                     x1_ref, xn_ref, lg_ref):
    o = jnp.concatenate([oa_ref[...], ob_ref[...] * gs_ref[...]], axis=1).astype(BF16)
    x1 = x_ref[...] + _dot(o, w_ref[...])
    _ffn_prologue(x1, gf_ref, wrh_ref, wrl_ref, rbias_ref, x1_ref, xn_ref, lg_ref)


def _out_even(oa, ob, gs, x, w_bf, g_ffn, wrh, wrl, rbias, *, tm):
    n, d = x.shape
    w = oa.shape[1]
    nr = wrh.shape[0]
    row = lambda i: (i, 0)
    return pl.pallas_call(
        _out_even_kernel,
        grid=(n // tm,),
        in_specs=[pl.BlockSpec((tm, w), row), pl.BlockSpec((tm, w), row), pl.BlockSpec((tm, w), row),
                  pl.BlockSpec((tm, d), row), _const_spec(w_bf.shape), _const_spec((1, d)),
                  _const_spec(wrh.shape), _const_spec(wrl.shape), _const_spec(rbias.shape)],
        out_specs=[pl.BlockSpec((tm, d), row), pl.BlockSpec((tm * (d // LANES), LANES), row),
                   pl.BlockSpec((nr, tm), lambda i: (0, i))],
        out_shape=[jax.ShapeDtypeStruct((n, d), F32), jax.ShapeDtypeStruct((n * (d // LANES), LANES), F32),
                   jax.ShapeDtypeStruct((nr, n), F32)],
        compiler_params=_params("arbitrary"),
        name="out_even",
    )(oa, ob, gs, x, w_bf, g_ffn, wrh, wrl, rbias)


def _gelu(x):
    return 0.5 * x * (1.0 + jnp.tanh(math.sqrt(2.0 / math.pi) * (x + 0.044715 * (x * x * x))))


def _odd_kernel(x_ref, gm_ref, win_ref, vg_ref, wsp_ref, bsp_ref, wout_ref, gf_ref, wrh_ref, wrl_ref,
                rbias_ref, x1_ref, xn_ref, lg_ref, *rest, half, groups, l, emit_v):
    if emit_v:
        vn_ref, u_s, s_s = rest
    else:
        vn_ref = None
        u_s, vn_s, s_s = rest
    tm = x_ref.shape[0]
    x = x_ref[...]
    xn = _rms(x, gm_ref[...]).astype(BF16)
    cw = 512
    vbuf = vn_ref if emit_v else vn_s
    for cidx in range(half // cw):
        u_s[:, cidx * cw:(cidx + 1) * cw] = _gelu(_dot(xn, win_ref[:, cidx * cw:(cidx + 1) * cw]))
        vbuf[:, cidx * cw:(cidx + 1) * cw] = _gelu(
            _dot(xn, win_ref[:, half + cidx * cw:half + (cidx + 1) * cw]))
    vbuf[...] = _rms(vbuf[...], vg_ref[...])
    gw = half // groups
    row = lax.broadcasted_iota(I32, (l, l), 0)
    col = lax.broadcasted_iota(I32, (l, l), 1)
    for g in range(groups):
        wg = jnp.where(col <= row, wsp_ref[g], 0.0).astype(BF16)
        bg = bsp_ref[:, g:g + 1]
        for ci in range(tm // l):
            vv = vbuf[ci * l:(ci + 1) * l, g * gw:(g + 1) * gw].astype(BF16)
            s_s[ci * l:(ci + 1) * l, g * gw:(g + 1) * gw] = _dot(wg, vv) + bg
    y = _dot((u_s[...] * s_s[...]).astype(BF16), wout_ref[...])
    _ffn_prologue(x + y, gf_ref, wrh_ref, wrl_ref, rbias_ref, x1_ref, xn_ref, lg_ref)


def _odd_mixer(x, g_mix, win_bf, v_gain, wsp, bsp_t, wout_bf, g_ffn, wrh, wrl, rbias, *, l, tm, emit_v):
    n, d = x.shape
    half = wout_bf.shape[0]
    groups = wsp.shape[0]
    nr = wrh.shape[0]
    assert n % tm == 0 and tm % l == 0
    row = lambda i: (i, 0)
    out_specs = [pl.BlockSpec((tm, d), row), pl.BlockSpec((tm * (d // LANES), LANES), row),
                 pl.BlockSpec((nr, tm), lambda i: (0, i))]
    out_shape = [jax.ShapeDtypeStruct((n, d), F32), jax.ShapeDtypeStruct((n * (d // LANES), LANES), F32),
                 jax.ShapeDtypeStruct((nr, n), F32)]
    scratch = [pltpu.VMEM((tm, half), F32)]
    if emit_v:
        out_specs.append(pl.BlockSpec((tm, half), row))
        out_shape.append(jax.ShapeDtypeStruct((n, half), F32))
    else:
        scratch.append(pltpu.VMEM((tm, half), F32))
    scratch.append(pltpu.VMEM((tm, half), F32))
    return pl.pallas_call(
        functools.partial(_odd_kernel, half=half, groups=groups, l=l, emit_v=emit_v),
        grid=(n // tm,),
        in_specs=[pl.BlockSpec((tm, d), row), _const_spec((1, d)), _const_spec(win_bf.shape),
                  _const_spec((1, half)), _const_spec(wsp.shape), _const_spec(bsp_t.shape),
                  _const_spec(wout_bf.shape), _const_spec((1, d)), _const_spec(wrh.shape),
                  _const_spec(wrl.shape), _const_spec(rbias.shape)],
        out_specs=out_specs,
        out_shape=out_shape,
        scratch_shapes=scratch,
        compiler_params=_params("arbitrary"),
        name="odd_mixer",
    )(x, g_mix, win_bf, v_gain, wsp, bsp_t, wout_bf, g_ffn, wrh, wrl, rbias)


def _route_kernel(lg_ref, tri_ref, init_ref, e_ref, g_ref, r_ref, p_ref, cnt_ref, base_ref, tcnt_ref, run_s,
                  *, groups, epg):
    i = pl.program_id(0)

    @pl.when(i == 0)
    def _():
        run_s[...] = init_ref[...]

    lg = lg_ref[...]
    tr = lg.shape[1]
    gl = [lg[g:g + 1] for g in range(groups)]
    m = functools.reduce(jnp.maximum, gl)
    grp = jnp.full((1, tr), groups - 1, I32)
    for g in range(groups - 2, -1, -1):
        grp = jnp.where(gl[g] == m, g, grp)
    gate_g = 1.0 / functools.reduce(lambda a, b: a + b, [jnp.exp(x - m) for x in gl])
    sel = lg[SUBLANES + (groups - 1) * epg:SUBLANES + groups * epg]
    for g in range(groups - 2, -1, -1):
        sel = jnp.where(grp == g, lg[SUBLANES + g * epg:SUBLANES + (g + 1) * epg], sel)
    sub = lax.broadcasted_iota(I32, sel.shape, 0)
    v1 = jnp.max(sel, axis=0, keepdims=True)
    i1 = jnp.min(jnp.where(sel == v1, sub, epg), axis=0, keepdims=True)
    sel2 = jnp.where(sub == i1, -jnp.inf, sel)
    v2 = jnp.max(sel2, axis=0, keepdims=True)
    i2 = jnp.min(jnp.where(sel2 == v2, sub, epg), axis=0, keepdims=True)
    tt = jnp.exp(v2 - v1)
    g1 = gate_g / (1.0 + tt)
    g2 = gate_g * tt / (1.0 + tt)
    e1 = grp * epg + i1
    e2 = grp * epg + i2
    ne = groups * epg
    eidx = lax.broadcasted_iota(I32, (ne, tr), 0)
    oh1 = eidx == e1
    oh2 = eidx == e2
    cnt = jnp.where(oh1, 1.0, 0.0) + jnp.where(oh2, 1.0, 0.0)
    local = _dot(cnt.astype(BF16), tri_ref[...])
    before = run_s[:, 0:1] + local
    r1 = jnp.sum(jnp.where(oh1, before, 0.0), axis=0, keepdims=True)
    r2 = jnp.sum(jnp.where(oh2, before, 0.0), axis=0, keepdims=True)
    tile_cnt = jnp.broadcast_to(jnp.sum(cnt, axis=1, keepdims=True), run_s.shape)
    padded = jnp.ceil(tile_cnt * (1.0 / RUN_CHUNK)) * RUN_CHUNK
    offset = _cumsum_rows(padded) - padded
    where_local = offset[:, 0:1] + local
    p1 = jnp.sum(jnp.where(oh1, where_local, 0.0), axis=0, keepdims=True)
    p2 = jnp.sum(jnp.where(oh2, where_local, 0.0), axis=0, keepdims=True)
    base_ref[...] = run_s[...]
    tcnt_ref[...] = tile_cnt
    run_s[...] = run_s[...] + tile_cnt
    rows = lax.broadcasted_iota(I32, (SUBLANES, tr), 0)
    e_ref[...] = jnp.where(rows == 0, e1, jnp.where(rows == 1, e2, 0))
    g_ref[...] = jnp.where(rows == 0, g1, jnp.where(rows == 1, g2, 0.0))
    r_ref[...] = jnp.where(rows == 0, r1, jnp.where(rows == 1, r2, 0.0)).astype(I32)
    p_ref[...] = jnp.where(rows == 0, p1, jnp.where(rows == 1, p2, 0.0)).astype(I32)
    cnt_ref[...] = run_s[...]


def _route(lgt, init, *, groups, epg, tr):
    nr, n = lgt.shape
    assert n % tr == 0 and nr == SUBLANES + groups * epg
    ne = groups * epg
    tri = (jnp.arange(tr)[:, None] < jnp.arange(tr)[None, :]).astype(BF16)
    tok = pl.BlockSpec((SUBLANES, tr), lambda i: (0, i))
    per_tile = pl.BlockSpec((ne, LANES), lambda i: (i, 0))
    tile_tab = jax.ShapeDtypeStruct((n // tr * ne, LANES), F32)
    return pl.pallas_call(
        functools.partial(_route_kernel, groups=groups, epg=epg),
        grid=(n // tr,),
        in_specs=[pl.BlockSpec((nr, tr), lambda i: (0, i)), _const_spec((tr, tr)), _const_spec((ne, LANES))],
        out_specs=[tok, tok, tok, tok, pl.BlockSpec((ne, LANES), lambda i: (0, 0)), per_tile, per_tile],
        out_shape=[jax.ShapeDtypeStruct((SUBLANES, n), I32), jax.ShapeDtypeStruct((SUBLANES, n), F32),
                   jax.ShapeDtypeStruct((SUBLANES, n), I32), jax.ShapeDtypeStruct((SUBLANES, n), I32),
                   jax.ShapeDtypeStruct((ne, LANES), F32), tile_tab, tile_tab],
        scratch_shapes=[pltpu.VMEM((ne, LANES), F32)],
        compiler_params=_params("arbitrary"),
        name="route",
    )(lgt, tri, init)


ISSUE_UNROLL = 8
DISPATCH_SLOTS = 3


def _dispatch_kernel(seg_ref, dst_ref, *rest, s, bm, tiles):
    x_refs = rest[:len(tiles)]
    buf_ref, zero_s, xin_s, sem, isem, zsem = rest[len(tiles):]
    tp = dst_ref.shape[2]
    blk = bm * s

    def zero_fill():
        zero_s[...] = jnp.zeros(zero_s.shape, F32)

        def block_copy(b):
            return pltpu.make_async_copy(zero_s, buf_ref.at[pl.ds(pl.multiple_of(b * blk, blk), blk)], zsem)

        for e in range(seg_ref.shape[1]):
            @pl.when(seg_ref[1, e] > 0)
            def _():
                block_copy(seg_ref[0, e] // bm - 1).start()

            @pl.when(seg_ref[1, e] - seg_ref[2, e] > bm)
            def _():
                block_copy(seg_ref[0, e] // bm - 2).start()
        for e in range(seg_ref.shape[1]):
            @pl.when(seg_ref[1, e] > 0)
            def _():
                block_copy(0).wait()

            @pl.when(seg_ref[1, e] - seg_ref[2, e] > bm)
            def _():
                block_copy(0).wait()

        def tail_start(b, c):
            block_copy(b).start()
            return c

        def tail_wait(b, c):
            block_copy(b).wait()
            return c

        first_unused = seg_ref[0, seg_ref.shape[1] - 1] // bm
        n_blocks = buf_ref.shape[0] // blk
        lax.fori_loop(first_unused, n_blocks, tail_start, 0)
        lax.fori_loop(first_unused, n_blocks, tail_wait, 0)

    pl.when(pl.program_id(0) == 0)(zero_fill)

    i = pl.program_id(0)
    last = pl.num_programs(0) - 1
    rows = tp * s

    def tile_in(t, go):
        lo = 0
        for x_ref, nt in zip(x_refs, tiles):
            @pl.when((t >= lo) & (t < lo + nt))
            def _(x_ref=x_ref, lo=lo):
                cp = pltpu.make_async_copy(x_ref.at[pl.ds(pl.multiple_of((t - lo) * rows, rows), rows)],
                                           xin_s.at[t % DISPATCH_SLOTS], isem.at[t % DISPATCH_SLOTS])
                cp.start() if go else cp.wait()
            lo += nt

    def rows_out_wait(t):
        for kk in range(TOP_K):
            pltpu.make_async_copy(xin_s.at[t % DISPATCH_SLOTS], buf_ref.at[pl.ds(0, rows)],
                                  sem.at[t % DISPATCH_SLOTS]).wait()

    @pl.when(i == 0)
    def _():
        tile_in(i, True)

    @pl.when(i >= DISPATCH_SLOTS - 1)
    def _():
        rows_out_wait(i - (DISPATCH_SLOTS - 1))

    @pl.when(i < last)
    def _():
        tile_in(i + 1, True)

    tile_in(i, False)
    slot = i % DISPATCH_SLOTS

    def issue(r0, c):
        for u in range(ISSUE_UNROLL):
            r = r0 * ISSUE_UNROLL + u
            for kk in range(TOP_K):
                pltpu.make_async_copy(xin_s.at[slot, _token_tile(r, s)],
                                      buf_ref.at[_token_tile(dst_ref[0, kk, r], s)], sem.at[slot]
                                      ).start(priority=kk % 2)
        return c

    lax.fori_loop(0, tp // ISSUE_UNROLL, issue, 0)

    @pl.when(i == last)
    def _():
        for back in range(DISPATCH_SLOTS - 2, -1, -1):
            @pl.when(i >= back)
            def _():
                rows_out_wait(i - back)


def _dispatch(seg, dest3, xns, n_slots, *, tp, s, bm):
    assert tp % ISSUE_UNROLL == 0 and all(x.shape[0] % (tp * s) == 0 for x in xns)
    tiles = tuple(x.shape[0] // (tp * s) for x in xns)
    grid_spec = pltpu.PrefetchScalarGridSpec(
        num_scalar_prefetch=1,
        grid=(sum(tiles),),
        in_specs=[pl.BlockSpec((1, TOP_K, tp), lambda i, sg: (i, 0, 0), memory_space=pltpu.SMEM)]
        + [pl.BlockSpec(memory_space=pl.ANY)] * len(xns),
        out_specs=pl.BlockSpec(memory_space=pl.ANY),
        scratch_shapes=[pltpu.VMEM((bm * s, LANES), F32), pltpu.VMEM((DISPATCH_SLOTS, tp * s, LANES), F32),
                        pltpu.SemaphoreType.DMA((DISPATCH_SLOTS,)), pltpu.SemaphoreType.DMA((DISPATCH_SLOTS,)),
                        pltpu.SemaphoreType.DMA(())],
    )
    return pl.pallas_call(
        functools.partial(_dispatch_kernel, s=s, bm=bm, tiles=tiles),
        grid_spec=grid_spec,
        out_shape=jax.ShapeDtypeStruct((n_slots * s, LANES), F32),
        compiler_params=_params("arbitrary"),
        name="dispatch",
    )(seg, dest3, *xns)


def _expert_kernel(be_ref, nu_ref, x_ref, wg_ref, wu_ref, wd_ref, o_ref, wg_s, wu_s, wd_s):
    b = pl.program_id(0)
    s = wg_s.shape[0] // LANES
    prev = be_ref[jnp.maximum(b - 1, 0)]

    @pl.when((b == 0) | (be_ref[b] != prev))
    def _():
        wg_s[...] = wg_ref[0, 0].astype(BF16)
        wu_s[...] = wu_ref[0, 0].astype(BF16)
        wd_s[...] = wd_ref[0, 0].astype(BF16)

    @pl.when(b < nu_ref[0])
    def _():
        xb = _load_token_tiles(x_ref, x_ref.shape[0] // s, s).astype(BF16)
        gate = _dot(xb, wg_s[...])
        h = gate * _sigmoid(gate) * _dot(xb, wu_s[...])
        _store_token_tiles(o_ref, _dot(h.astype(BF16), wd_s[...]))

    @pl.when(b >= nu_ref[0])
    def _():
        o_ref[...] = jnp.zeros(o_ref.shape, F32)


def _experts(blk_expert, n_used, buf, w_gate, w_up, w_down, *, layer, bm):
    d, de = w_gate.shape[2:]
    s = d // LANES
    n_blocks = buf.shape[0] // (bm * s)
    rows = pl.BlockSpec((bm * s, LANES), lambda b, be, nu: (b, 0))
    used_rows = pl.BlockSpec((bm * s, LANES), lambda b, be, nu: (jnp.minimum(b, nu[0] - 1), 0))
    grid_spec = pltpu.PrefetchScalarGridSpec(
        num_scalar_prefetch=2,
        grid=(n_blocks,),
        in_specs=[used_rows,
                  pl.BlockSpec((1, 1, d, de), lambda b, be, nu: (layer, be[b], 0, 0)),
                  pl.BlockSpec((1, 1, d, de), lambda b, be, nu: (layer, be[b], 0, 0)),
                  pl.BlockSpec((1, 1, de, d), lambda b, be, nu: (layer, be[b], 0, 0))],
        out_specs=rows,
        scratch_shapes=[pltpu.VMEM((d, de), BF16), pltpu.VMEM((d, de), BF16), pltpu.VMEM((de, d), BF16)],
    )
    return pl.pallas_call(
        _expert_kernel,
        grid_spec=grid_spec,
        out_shape=jax.ShapeDtypeStruct(buf.shape, F32),
        compiler_params=_params("arbitrary"),
        name="experts",
    )(blk_expert, n_used, buf, w_gate, w_up, w_down)


def _combine_kernel(tab_ref, nxt_ref, pos_ref, gate_ref, x_ref, yb_ref, o_ref, stg_s, tt_s, sem, *, ahead):
    i = pl.program_id(0)
    tq, d = x_ref.shape
    s = d // LANES
    piece = RUN_CHUNK * s
    slot = i % 2

    def run_copies(ref, dst_slot, go):
        for e in range(ref.shape[2]):
            def body(j, c):
                src = pl.multiple_of((ref[0, 0, e] + j * RUN_CHUNK) * s, s)
                dst = pl.multiple_of((ref[0, 2, e] + j * RUN_CHUNK) * s, piece)
                cp = pltpu.make_async_copy(yb_ref.at[pl.ds(src, piece)],
                                           stg_s.at[dst_slot, pl.ds(dst, piece)], sem.at[dst_slot])
                cp.start() if go else cp.wait()
                return c
            lax.fori_loop(0, ref[0, 1, e], body, 0)

    if ahead:
        @pl.when(i == 0)
        def _():
            run_copies(tab_ref, 0, True)

        @pl.when(i + 1 < pl.num_programs(0))
        def _():
            run_copies(nxt_ref, 1 - slot, True)
    else:
        run_copies(tab_ref, slot, True)

    run_copies(tab_ref, slot, False)

    def assemble(r0, c):
        for u in range(ISSUE_UNROLL):
            r = r0 * ISSUE_UNROLL + u
            acc = gate_ref[0, 0, r] * stg_s[slot, _token_tile(pos_ref[0, 0, r], s), :]
            for kk in range(1, TOP_K):
                acc = acc + gate_ref[0, kk, r] * stg_s[slot, _token_tile(pos_ref[0, kk, r], s), :]
            tt_s[_token_tile(r, s), :] = acc
        return c

    lax.fori_loop(0, tq // ISSUE_UNROLL, assemble, 0)
    o_ref[...] = x_ref[...] + _load_token_tiles(tt_s, tq, s)


def _combine(tab, pos3, gates3, x1, yb, *, tq):
    n, d = x1.shape
    s = d // LANES
    nt = n // tq
    ne = tab.shape[2]
    assert tq % ISSUE_UNROLL == 0
    stage_rows = tq * TOP_K + ne * RUN_CHUNK
    smem = lambda shape, imap: pl.BlockSpec(shape, imap, memory_space=pltpu.SMEM)
    return pl.pallas_call(
        functools.partial(_combine_kernel, ahead=nt > 2),
        grid=(nt,),
        in_specs=[smem((1, 3, ne), lambda i: (i, 0, 0)),
                  smem((1, 3, ne), lambda i: (jnp.minimum(i + 1, nt - 1), 0, 0)),
                  smem((1, TOP_K, tq), lambda i: (i, 0, 0)),
                  smem((1, TOP_K, tq), lambda i: (i, 0, 0)),
                  pl.BlockSpec((tq, d), lambda i: (i, 0)),
                  pl.BlockSpec(memory_space=pl.ANY)],
        out_specs=pl.BlockSpec((tq, d), lambda i: (i, 0)),
        out_shape=jax.ShapeDtypeStruct((n, d), F32),
        scratch_shapes=[pltpu.VMEM((2, stage_rows * s, LANES), F32), pltpu.VMEM((tq * s, LANES), F32),
                        pltpu.SemaphoreType.DMA((2,))],
        compiler_params=_params("arbitrary"),
        name="combine",
    )(tab, tab, pos3, gates3, x1, yb)


def _moe(streams, w_gate, w_up, w_down, *, layer, groups, epg):
    d = streams[0][0].shape[1]
    ne = groups * epg
    routed = []
    taken = jnp.zeros((ne, LANES), F32)
    for x1, _, lgt in streams:
        rtile = min(512, x1.shape[0])
        routed.append(_route(lgt, taken, groups=groups, epg=epg, tr=rtile) + (rtile,))
        taken = routed[-1][4]
    n_rows = sum(x1.shape[0] for x1, _, _ in streams) * TOP_K
    counts = taken[:, 0].astype(I32)
    bm = MOE_BLOCK if n_rows >= 2 * ne * MOE_BLOCK else MOE_BLOCK_SMALL
    padded = (counts + RUN_CHUNK + bm - 1) // bm * bm
    pend = jnp.cumsum(padded)
    pstart = pend - padded
    n_blocks = -(-(n_rows + ne * RUN_CHUNK) // bm) + ne
    eids = jnp.arange(ne, dtype=I32)
    blk_row = jnp.arange(n_blocks, dtype=I32)[:, None] * bm
    blk_expert = jnp.minimum(jnp.sum((pend[None, :] <= blk_row).astype(I32), axis=1), ne - 1)
    n_used = (pend[-1:] // bm).astype(I32)
    seg_tab = jnp.stack([pend, padded, counts]).astype(I32)
    dtile = min(r[-1] for r in routed)
    dests = []
    for (x1, _, _), (e8, g8, r8, p8, cnt, base, tcnt, rtile) in zip(streams, routed):
        seg = jnp.sum(jnp.where(e8[:TOP_K, :, None] == eids, pstart, 0), axis=-1)
        dest = seg + r8[:TOP_K]
        dests.append(dest.reshape(TOP_K, x1.shape[0] // dtile, dtile).transpose(1, 0, 2))
    buf = _dispatch(seg_tab, jnp.concatenate(dests, axis=0), [xn for _, xn, _ in streams], n_blocks * bm,
                    tp=dtile, s=d // LANES, bm=bm)
    yb = _experts(blk_expert, n_used, buf, w_gate, w_up, w_down, layer=layer, bm=bm)
    outs = []
    for (x1, _, _), (e8, g8, r8, p8, cnt, base, tcnt, rtile) in zip(streams, routed):
        nrt = x1.shape[0] // rtile
        run_start = pstart[None, :] + base.reshape(nrt, ne, LANES)[:, :, 0].astype(I32)
        pieces = (tcnt.reshape(nrt, ne, LANES)[:, :, 0].astype(I32) + RUN_CHUNK - 1) // RUN_CHUNK
        stage = (jnp.cumsum(pieces, axis=1) - pieces) * RUN_CHUNK
        tab = jnp.stack([run_start, pieces, stage], axis=1)
        by_tile = lambda a: a[:TOP_K].reshape(TOP_K, nrt, rtile).transpose(1, 0, 2)
        outs.append(_combine(tab, by_tile(p8), by_tile(g8), x1, yb, tq=rtile))
    return outs


def _router_weights(wg, bg, we, be):
    d, groups = wg.shape
    epg = we.shape[2]
    assert groups <= SUBLANES and epg == SUBLANES
    pad = jnp.zeros((SUBLANES - groups, d), F32)
    wr = jnp.concatenate([wg.T, pad, we.transpose(0, 2, 1).reshape(groups * epg, d)], axis=0)
    rb = jnp.concatenate([bg, jnp.zeros((SUBLANES - groups,), F32), be.reshape(-1)])[:, None]
    hi = wr.astype(BF16)
    lo = (wr - hi.astype(F32)).astype(BF16)
    return hi, lo, rb, groups, epg


def kernel(x_prompt, x_sample, cache_attn_k, cache_attn_v, state_hgrn, rel_bias, norm_mix, norm_ffn,
           w_in_even, w_out_even, q_norm_gain, k_norm_gain, lam_q1, lam_k1, lam_q2, lam_k2, da_out_gain,
           hgrn_lb_logits, hgrn_out_gain, w_in_odd, sgu_v_gain, sgu_w, sgu_b, w_out_odd,
           router_group_w, router_group_b, router_expert_w, router_expert_b,
           expert_w_gate, expert_w_up, expert_w_down):
    bp, tp, d = x_prompt.shape
    bs, ts, _ = x_sample.shape
    depth = norm_mix.shape[0]
    _, _, past, da_heads, _, da_dh = cache_attn_k.shape
    da_dv = cache_attn_v.shape[-1]
    _, _, hg_heads, hg_dk, hg_dv = state_hgrn.shape
    width = da_heads * da_dv
    assert width == da_heads * 2 * da_dh == hg_heads * hg_dk == hg_heads * hg_dv
    assert da_dv == LANES and hg_dk == LANES and hg_dv == LANES

    lb_all = jnp.cumsum(jax.nn.softmax(hgrn_lb_logits.astype(F32), axis=0), axis=0)
    gid = jnp.arange(width) // da_dh
    pm = jnp.where(gid[:, None] == gid[None, :], 1.0 / da_dh, 0.0).astype(BF16)

    xs = {"p": x_prompt.reshape(bp * tp, d), "s": x_sample.reshape(bs * ts, d)}
    dims = {"p": (bp, tp), "s": (bs, ts)}
    outs = {"p": {}, "s": {}}
    kp_l, vp_l, ks_l, vs_l, sp_l, ss_l, sgu_l = [], [], [], [], [], [], []

    for layer in range(depth):
        j = layer // 2
        wrh, wrl, rbias, groups, epg = _router_weights(
            router_group_w[layer], router_group_b[layer], router_expert_w[layer], router_expert_b[layer])
        g_mix = norm_mix[layer][None, :]
        g_ffn = norm_ffn[layer][None, :]
        if layer % 2 == 0:
            lam_init = 0.8 - 0.6 * math.exp(-0.3 * layer)
            w_in_bf = w_in_even[j].astype(BF16)
            w_out_bf = w_out_even[j].astype(BF16)
            reps = width // da_dh
            qg = jnp.tile(q_norm_gain[j], reps)[None, :]
            kg = jnp.tile(k_norm_gain[j], reps)[None, :]
            lam4 = jnp.stack([lam_q1[j], lam_k1[j], lam_q2[j], lam_k2[j]])
            sub_gain = da_out_gain[j][None, :]
            hg_gain = hgrn_out_gain[j][None, :]
            lb = lb_all[j][None, :]
            for key in ("p", "s"):
                b, t = dims[key]
                x = xs[key]
                q, k, v, qh, kb, lf, ih, gs, k5, v4 = _in_even(
                    x, g_mix, w_in_bf, pm, qg, kg, lb, width=width, q_scale=da_dh ** -0.5 * LOG2E,
                    tm=min(512, b * t), batch=b, seq=t, heads=da_heads, dh=da_dh)
                if key == "p":
                    oa = _attn_prompt(q, k, v, rel_bias, lam4, sub_gain, batch=b, seq=t, heads=da_heads,
                                      dh=da_dh, lam_init=lam_init, qt=min(256, t))
                    s0 = jnp.zeros((b, hg_heads, hg_dk, hg_dv), F32)
                    ob, s_new = _hgrn(qh, kb, lf, ih, s0, hg_gain, batch=b, seq=t, heads=hg_heads,
                                      dk=hg_dk, dv=hg_dv, tb=min(512, t))
                    kp_l.append(k5)
                    vp_l.append(v4)
                    sp_l.append(s_new)
                else:
                    ck = cache_attn_k[j].reshape(b, past, width)
                    cv = cache_attn_v[j].reshape(b * past * da_heads, da_dv)
                    oa = _attn_sample(q, k, v, ck, cv, rel_bias, lam4, sub_gain, batch=b, t=t,
                                      heads=da_heads, dh=da_dh, lam_init=lam_init)
                    ob, s_new = _hgrn(qh, kb, lf, ih, state_hgrn[j], hg_gain, batch=b, seq=t,
                                      heads=hg_heads, dk=hg_dk, dv=hg_dv, tb=t)
                    ks_l.append(k5)
                    vs_l.append(v4)
                    ss_l.append(s_new)
                outs[key] = _out_even(oa, ob, gs, x, w_out_bf, g_ffn, wrh, wrl, rbias, tm=min(512, b * t))
        else:
            w_in_bf = w_in_odd[j].astype(BF16)
            w_out_bf = w_out_odd[j].astype(BF16)
            v_gain = sgu_v_gain[j][None, :]
            for key in ("p", "s"):
                b, t = dims[key]
                l = min(SGU_CHUNK, t)
                res = _odd_mixer(xs[key], g_mix, w_in_bf, v_gain, sgu_w[j][:, :l, :l], sgu_b[j][:, :l].T,
                                 w_out_bf, g_ffn, wrh, wrl, rbias, l=l, tm=min(512, b * t), emit_v=(key == "s"))
                outs[key] = res[:3]
                if key == "s":
                    sgu_l.append(res[3].reshape(b, t, -1))
        xs["p"], xs["s"] = _moe([outs["p"], outs["s"]], expert_w_gate, expert_w_up, expert_w_down,
                                layer=layer, groups=groups, epg=epg)

    return (xs["p"].reshape(bp, tp, d), xs["s"].reshape(bs, ts, d), jnp.stack(kp_l), jnp.stack(vp_l),
            jnp.stack(ks_l), jnp.stack(vs_l), jnp.stack(sp_l), jnp.stack(ss_l), jnp.stack(sgu_l))
```

```python
import functools
import math

import jax
import jax.numpy as jnp
from jax import lax
from jax.experimental import pallas as pl
from jax.experimental.pallas import tpu as pltpu

F32 = jnp.float32
BF16 = jnp.bfloat16
I32 = jnp.int32

EPS = 1e-6
LOG2E = math.log2(math.e)
CHUNK = 64
SGU_CHUNK = 128
REL_BUCKETS = 32
REL_MAX_DIST = 128
TOP_K = 2
MOE_BLOCK = 512
MOE_BLOCK_SMALL = 128
RUN_CHUNK = 16
HG_SUB = 16

LANES = 128
SUBLANES = 8
VMEM_LIMIT = 56 * 1024 * 1024

NT_DIMS = (((1,), (1,)), ((), ()))
TN_DIMS = (((0,), (0,)), ((), ()))


def _params(*sem):
    return pltpu.CompilerParams(dimension_semantics=sem, vmem_limit_bytes=VMEM_LIMIT)


def _const_spec(shape):
    nd = len(shape)
    return pl.BlockSpec(shape, lambda *_: (0,) * nd, pipeline_mode=pl.Buffered(1))


def _sigmoid(x):
    return 1.0 / (1.0 + jnp.exp(-x))


def _rms(x, g):
    return x * lax.rsqrt(jnp.mean(x * x, axis=-1, keepdims=True) + EPS) * g


def _dot(a, b):
    return jnp.dot(a, b, preferred_element_type=F32)


def rel_bucket(rel):
    half = REL_BUCKETS // 2
    max_exact = half // 2
    ret = (rel > 0).astype(I32) * half
    n = jnp.abs(rel)
    nf = jnp.maximum(n, 1).astype(F32)
    large = max_exact + (jnp.log(nf / max_exact) / math.log(REL_MAX_DIST / max_exact)
                         * (half - max_exact)).astype(I32)
    large = jnp.minimum(large, half - 1)
    return ret + jnp.where(n < max_exact, n, large)


def _in_even_kernel(x_ref, g_ref, w_ref, pm_ref, qg_ref, kg_ref, lb_ref,
                    q_ref, k_ref, v_ref, qh_ref, kb_ref, lf_ref, ih_ref, gs_ref, k5_ref, v4_ref, *, width,
                    q_scale):
    xn = _rms(x_ref[...], g_ref[...]).astype(BF16)

    def proj(c):
        return _dot(xn, w_ref[:, c * width:(c + 1) * width])

    def group_norm(y, gain):
        ms = _dot((y * y).astype(BF16), pm_ref[...])
        return y * lax.rsqrt(ms + EPS) * gain

    q_ref[...] = group_norm(proj(0), qg_ref[...]) * q_scale
    kn = group_norm(proj(1), kg_ref[...])
    k_ref[...] = kn
    bb, ts, heads, _, dh = k5_ref.shape
    for h in range(heads):
        for c in range(2):
            piece = kn[:, (2 * h + c) * dh:(2 * h + c + 1) * dh]
            k5_ref[:, :, h, c, :] = piece.reshape(bb, ts, dh)
    vv = proj(2)
    v_ref[...] = vv
    for h in range(heads):
        v4_ref[:, :, h, :] = vv[:, h * 2 * dh:(h + 1) * 2 * dh].reshape(bb, ts, 2 * dh)
    yq = proj(3)
    qh_ref[...] = yq * _sigmoid(yq)
    zf = proj(4)
    lb = lb_ref[...]
    lf_ref[...] = jnp.log(lb + (1.0 - lb) * _sigmoid(zf))
    kb_ref[...] = (1.0 - lb) * _sigmoid(-zf)
    ih_ref[...] = proj(5)
    yg = proj(6)
    gs_ref[...] = yg * _sigmoid(yg)


def _in_even(x, g_mix, w_bf, pm, qg, kg, lb, *, width, q_scale, tm, batch, seq, heads, dh):
    n, d = x.shape
    assert n % tm == 0 and (tm % seq == 0 or seq % tm == 0)
    row = lambda i: (i, 0)
    out = jax.ShapeDtypeStruct((n, width), F32)
    bb, ts = max(tm // seq, 1), min(tm, seq)
    per = seq // ts
    k5_spec = pl.BlockSpec((bb, ts, heads, 2, dh), lambda i: (i // per, i % per, 0, 0, 0))
    v4_spec = pl.BlockSpec((bb, ts, heads, 2 * dh), lambda i: (i // per, i % per, 0, 0))
    return pl.pallas_call(
        functools.partial(_in_even_kernel, width=width, q_scale=q_scale),
        grid=(n // tm,),
        in_specs=[pl.BlockSpec((tm, d), row), _const_spec((1, d)), _const_spec(w_bf.shape),
                  _const_spec(pm.shape), _const_spec((1, width)), _const_spec((1, width)),
                  _const_spec((1, width))],
        out_specs=[pl.BlockSpec((tm, width), row)] * 8 + [k5_spec, v4_spec],
        out_shape=[out] * 8 + [jax.ShapeDtypeStruct((batch, seq, heads, 2, dh), F32),
                               jax.ShapeDtypeStruct((batch, seq, heads, 2 * dh), F32)],
        compiler_params=_params("arbitrary"),
        name="in_even",
    )(x, g_mix, w_bf, pm, qg, kg, lb)


def _bias_from_buckets(bk, rb_ref, h):
    b = jnp.zeros(bk.shape, F32)
    for u in range(REL_BUCKETS):
        b = jnp.where(bk == u, rb_ref[u, h], b)
    return jnp.where(bk < 0, -jnp.inf, b)


def _lam(lam_ref, lam_init):
    r = lam_ref[...]
    s1 = jnp.sum(r[0:1] * r[1:2], axis=1, keepdims=True)
    s2 = jnp.sum(r[2:3] * r[3:4], axis=1, keepdims=True)
    return jnp.exp(s1) - jnp.exp(s2) + lam_init


def _split_components(q, dh):
    lane = lax.broadcasted_iota(I32, q.shape, 1)
    q0 = jnp.where(lane < dh, q, 0.0)
    q1 = jnp.where(lane >= dh, q, 0.0)
    return jnp.concatenate([q0, q1], axis=0).astype(BF16)


def _attn_prompt_kernel(rb_ref, far_ref, lam_ref, bk_ref, sg_ref, q_ref, k_ref, v_ref, o_ref, bias_s, *,
                        qt, dh, lam_init):
    h = pl.program_id(0)
    t = q_ref.shape[0]
    kb = k_ref[...].astype(BF16)
    vt = v_ref[...].T.astype(BF16)
    qtr = q_ref[...].T
    sub = lax.broadcasted_iota(I32, (2 * dh, qt), 0)

    @pl.when(pl.program_id(1) == 0)
    def _():
        for d in range(2):
            b = _bias_from_buckets(bk_ref[d], rb_ref, h) * LOG2E
            bias_s[d] = jnp.concatenate([b, b], axis=1)

    bias = [bias_s[0], bias_s[1]]
    far = rb_ref[far_ref[0], h] * LOG2E
    lam = _lam(lam_ref, lam_init)
    gain = sg_ref[...] * (1.0 - lam_init)
    for i in range(t // qt):
        qi = qtr[:, i * qt:(i + 1) * qt]
        qz = jnp.concatenate([jnp.where(sub < dh, qi, 0.0), jnp.where(sub >= dh, qi, 0.0)],
                             axis=1).astype(BF16)
        n = (i + 1) * qt
        parts = [(n - qt, n)]
        s = [_dot(kb[n - qt:n], qz) + bias[0]]
        shift = [0.0]
        if i >= 1:
            parts.append((n - 2 * qt, n - qt))
            s.append(_dot(kb[n - 2 * qt:n - qt], qz) + bias[1])
            shift.append(0.0)
        if i >= 2:
            parts.append((0, n - 2 * qt))
            s.append(_dot(kb[:n - 2 * qt], qz))
            shift.append(far)
        m = functools.reduce(jnp.maximum, [jnp.max(x, axis=0, keepdims=True) + c for x, c in zip(s, shift)])
        p = [jnp.exp2(x - (m - c)) for x, c in zip(s, shift)]
        l = functools.reduce(lambda a, b: a + b, [jnp.sum(x, axis=0, keepdims=True) for x in p])
        acc = functools.reduce(lambda a, b: a + b,
                               [_dot(vt[:, lo:hi], x.astype(BF16)) for (lo, hi), x in zip(parts, p)])
        o = acc / l
        out = o[:, :qt] - lam * o[:, qt:]
        out = out * lax.rsqrt(jnp.mean(out * out, axis=0, keepdims=True) + EPS) * gain
        o_ref[i * qt:(i + 1) * qt, :] = out.T


def _attn_prompt(q, k, v, rel_bias, lam4, sub_gain, *, batch, seq, heads, dh, lam_init, qt):
    n, w = q.shape
    dv = w // heads
    assert dv == 2 * dh and seq % qt == 0 and qt % CHUNK == 0
    kj = jnp.arange(qt, dtype=I32)[:, None]
    qi = jnp.arange(qt, dtype=I32)[None, :]
    bk0 = jnp.where((kj // CHUNK) <= (qi // CHUNK), rel_bucket(kj - qi), -1)
    bk1 = rel_bucket(kj - qi - qt)
    bk = jnp.stack([bk0, bk1]).astype(I32)
    assert qt + 1 >= REL_MAX_DIST
    far = rel_bucket(jnp.full((1,), -(qt + 1), I32))
    smem = pl.BlockSpec(memory_space=pltpu.SMEM)
    seq_blk = pl.BlockSpec((seq, dv), lambda h, b: (b, h))
    return pl.pallas_call(
        functools.partial(_attn_prompt_kernel, qt=qt, dh=dh, lam_init=lam_init),
        grid=(heads, batch),
        in_specs=[smem, smem, _const_spec(lam4.shape), _const_spec(bk.shape), _const_spec((dv, 1)),
                  seq_blk, seq_blk, seq_blk],
        out_specs=seq_blk,
        out_shape=jax.ShapeDtypeStruct((n, w), F32),
        scratch_shapes=[pltpu.VMEM((2, qt, 2 * qt), F32)],
        compiler_params=_params("arbitrary", "arbitrary"),
        name="attn_prompt",
    )(rel_bias, far, lam4, bk, sub_gain.T, q, k, v)


def _attn_sample_kernel(rb_ref, lam_ref, bkc_ref, bkn_ref, sg_ref, q_ref, kc_ref, vc_ref, kn_ref, vn_ref,
                        o_ref, bc_s, bn_s, *, t, dh, lam_init, heads):
    h = pl.program_id(1)
    qz = _split_components(q_ref[...], dh)

    @pl.when((pl.program_id(0) == 0) & (h == 0))
    def _():
        for hh in range(heads):
            bc = _bias_from_buckets(bkc_ref[...], rb_ref, hh) * LOG2E
            bn = _bias_from_buckets(bkn_ref[...], rb_ref, hh) * LOG2E
            bc_s[hh] = jnp.concatenate([bc, bc], axis=0)
            bn_s[hh] = jnp.concatenate([bn, bn], axis=0)

    past_k = kc_ref.shape[0] // (2 * heads)
    q = q_ref[...].astype(BF16)
    sc = jnp.concatenate(
        [lax.dot_general(q[:, c * dh:(c + 1) * dh],
                         kc_ref[pl.ds(2 * h + c, past_k, stride=2 * heads), :].astype(BF16), NT_DIMS,
                         preferred_element_type=F32) for c in range(2)], axis=0)
    sn = lax.dot_general(qz, kn_ref[...].astype(BF16), NT_DIMS, preferred_element_type=F32)
    sc = sc + bc_s[h]
    sn = sn + bn_s[h]
    m = jnp.maximum(jnp.max(sc, axis=1, keepdims=True), jnp.max(sn, axis=1, keepdims=True))
    pc = jnp.exp2(sc - m)
    pn = jnp.exp2(sn - m)
    l = jnp.sum(pc, axis=1, keepdims=True) + jnp.sum(pn, axis=1, keepdims=True)
    past = vc_ref.shape[0] // heads
    vc = vc_ref[pl.ds(h, past, stride=heads), :]
    acc = _dot(pc.astype(BF16), vc.astype(BF16)) + _dot(pn.astype(BF16), vn_ref[...].astype(BF16))
    o = acc / l
    out = o[:t] - _lam(lam_ref, lam_init) * o[t:]
    o_ref[...] = _rms(out, sg_ref[...]) * (1.0 - lam_init)


def _attn_sample(q, k_new, v_new, cache_k, cache_v, rel_bias, lam4, sub_gain, *, batch, t, heads, dh,
                 lam_init):
    n, w = q.shape
    dv = w // heads
    past = cache_k.shape[0] // (batch * heads * 2)
    assert past % CHUNK == 0 and t <= CHUNK
    qpos = past + jnp.arange(t, dtype=I32)[:, None]
    bkc = rel_bucket(jnp.arange(past, dtype=I32)[None, :] - qpos).astype(I32)
    bkn = rel_bucket(past + jnp.arange(t, dtype=I32)[None, :] - qpos).astype(I32)
    smem = pl.BlockSpec(memory_space=pltpu.SMEM)
    new = pl.BlockSpec((t, dv), lambda b, h: (b, h))
    old_k = pl.BlockSpec((past * heads * 2, dh), lambda b, h: (b, 0))
    old_v = pl.BlockSpec((past * heads, dv), lambda b, h: (b, 0))
    return pl.pallas_call(
        functools.partial(_attn_sample_kernel, t=t, dh=dh, lam_init=lam_init, heads=heads),
        grid=(batch, heads),
        in_specs=[smem, _const_spec(lam4.shape), _const_spec(bkc.shape), _const_spec(bkn.shape),
                  _const_spec((1, dv)), new, old_k, old_v, new, new],
        out_specs=new,
        out_shape=jax.ShapeDtypeStruct((n, w), F32),
        scratch_shapes=[pltpu.VMEM((heads, 2 * t, past), F32), pltpu.VMEM((heads, 2 * t, t), F32)],
        compiler_params=_params("arbitrary", "arbitrary"),
        name="attn_sample",
    )(rel_bias, lam4, bkc, bkn, sub_gain, q, cache_k, cache_v, k_new, v_new)


def _cumsum_rows(x):
    c = x.shape[0]
    row = lax.broadcasted_iota(I32, x.shape, 0)
    s = 1
    while s < c:
        x = x + jnp.where(row >= s, pltpu.roll(x, s, axis=0), 0.0)
        s *= 2
    return x


def _hgrn_kernel(qh_ref, kb_ref, lf_ref, ih_ref, s0_ref, hg_ref, ob_ref, sf_ref, st_s, *,
                 heads, dk, dv, c, nsb):
    t = pl.program_id(1)

    @pl.when(t == 0)
    def _():
        for h in range(heads):
            st_s[h] = s0_ref[0, h].T

    tb = qh_ref.shape[0]
    row = lax.broadcasted_iota(I32, (c, c), 0)
    col = lax.broadcasted_iota(I32, (c, c), 1)
    causal = col <= row

    def chunk(ci, carry):
        r0 = pl.multiple_of(ci * c, c)
        for h in range(heads):
            rows = pl.ds(r0, c)
            q = qh_ref[rows, h * dk:(h + 1) * dk]
            k = kb_ref[rows, h * dk:(h + 1) * dk]
            v = ih_ref[rows, h * dv:(h + 1) * dv]
            b = _cumsum_rows(lf_ref[rows, h * dk:(h + 1) * dk])
            bl = b[c - 1:c]
            st = st_s[h]
            inter = lax.dot_general((q * jnp.exp(b)).astype(BF16), st.astype(BF16), NT_DIMS,
                                    preferred_element_type=F32)
            qs, ks = [], []
            for j in range(nsb):
                ref = b[j * HG_SUB + HG_SUB // 2:j * HG_SUB + HG_SUB // 2 + 1]
                qs.append(q * jnp.exp(b - ref))
                sub = slice(j * HG_SUB, (j + 1) * HG_SUB)
                ks.append(k[sub] * jnp.exp(ref - b[sub]))
            a_full = lax.dot_general(jnp.concatenate(qs, axis=0).astype(BF16),
                                     jnp.concatenate(ks, axis=0).astype(BF16), NT_DIMS,
                                     preferred_element_type=F32)
            att = jnp.zeros((c, c), F32)
            for j in range(nsb):
                att = jnp.where(col >= j * HG_SUB, a_full[j * c:(j + 1) * c], att)
            att = jnp.where(causal, att, 0.0)
            out = inter + _dot(att.astype(BF16), v.astype(BF16))
            ob_ref[rows, h * dv:(h + 1) * dv] = _rms(out, hg_ref[...])
            kdec = (k * jnp.exp(bl - b)).astype(BF16)
            st_s[h] = jnp.exp(bl) * st + lax.dot_general(v.astype(BF16), kdec, TN_DIMS,
                                                         preferred_element_type=F32)
        return carry

    lax.fori_loop(0, tb // c, chunk, 0, unroll=min(4, tb // c))

    @pl.when(t == pl.num_programs(1) - 1)
    def _():
        for h in range(heads):
            sf_ref[0, h] = st_s[h].T


def _hgrn(qh, kb, lf, ih, s0, hg_gain, *, batch, seq, heads, dk, dv, tb):
    n = qh.shape[0]
    c = min(CHUNK, seq)
    assert seq % tb == 0 and tb % c == 0 and c % HG_SUB == 0
    nt = seq // tb
    blk = lambda w: pl.BlockSpec((tb, w), lambda b, t: (b * nt + t, 0))
    st = pl.BlockSpec((1, heads, dk, dv), lambda b, t: (b, 0, 0, 0))
    return pl.pallas_call(
        functools.partial(_hgrn_kernel, heads=heads, dk=dk, dv=dv, c=c, nsb=c // HG_SUB),
        grid=(batch, nt),
        in_specs=[blk(heads * dk), blk(heads * dk), blk(heads * dk), blk(heads * dv), st,
                  _const_spec((1, dv))],
        out_specs=[blk(heads * dv), st],
        out_shape=[jax.ShapeDtypeStruct((n, heads * dv), F32),
                   jax.ShapeDtypeStruct((batch, heads, dk, dv), F32)],
        scratch_shapes=[pltpu.VMEM((heads, dv, dk), F32)],
        compiler_params=_params("arbitrary", "arbitrary"),
        name="hgrn",
    )(qh, kb, lf, ih, s0, hg_gain)


def _store_token_tiles(ref, x):
    rows, d = x.shape
    s = d // LANES
    for c in range(s):
        ref[pl.ds(c, rows, stride=s), :] = x[:, c * LANES:(c + 1) * LANES]


def _load_token_tiles(ref, rows, s):
    return jnp.concatenate([ref[pl.ds(c, rows, stride=s), :] for c in range(s)], axis=1)


def _token_tile(r, s):
    return pl.ds(pl.multiple_of(r * s, s), s)


def _ffn_prologue(x1, gf_ref, wrh_ref, wrl_ref, rbias_ref, x1_ref, xn_ref, lg_ref):
    x1_ref[...] = x1
    xn = _rms(x1, gf_ref[...])
    _store_token_tiles(xn_ref, xn)
    hi = xn.astype(BF16)
    lo = (xn - hi.astype(F32)).astype(BF16)
    nt = functools.partial(lax.dot_general, dimension_numbers=NT_DIMS, preferred_element_type=F32)
    lg_ref[...] = nt(wrh_ref[...], hi) + nt(wrh_ref[...], lo) + nt(wrl_ref[...], hi) + rbias_ref[...]


def _out_even_kernel(oa_ref, ob_ref, gs_ref, x_ref, w_ref, gf_ref, wrh_ref, wrl_ref, rbias_ref,
                     x1_ref, xn_ref, lg_ref):
    o = jnp.concatenate([oa_ref[...], ob_ref[...] * gs_ref[...]], axis=1).astype(BF16)
    x1 = x_ref[...] + _dot(o, w_ref[...])
    _ffn_prologue(x1, gf_ref, wrh_ref, wrl_ref, rbias_ref, x1_ref, xn_ref, lg_ref)


def _out_even(oa, ob, gs, x, w_bf, g_ffn, wrh, wrl, rbias, *, tm):
    n, d = x.shape
    w = oa.shape[1]
    nr = wrh.shape[0]
    row = lambda i: (i, 0)
    return pl.pallas_call(
        _out_even_kernel,
        grid=(n // tm,),
        in_specs=[pl.BlockSpec((tm, w), row), pl.BlockSpec((tm, w), row), pl.BlockSpec((tm, w), row),
                  pl.BlockSpec((tm, d), row), _const_spec(w_bf.shape), _const_spec((1, d)),
                  _const_spec(wrh.shape), _const_spec(wrl.shape), _const_spec(rbias.shape)],
        out_specs=[pl.BlockSpec((tm, d), row), pl.BlockSpec((tm * (d // LANES), LANES), row),
                   pl.BlockSpec((nr, tm), lambda i: (0, i))],
        out_shape=[jax.ShapeDtypeStruct((n, d), F32), jax.ShapeDtypeStruct((n * (d // LANES), LANES), F32),
                   jax.ShapeDtypeStruct((nr, n), F32)],
        compiler_params=_params("arbitrary"),
        name="out_even",
    )(oa, ob, gs, x, w_bf, g_ffn, wrh, wrl, rbias)


def _gelu(x):
    return 0.5 * x * (1.0 + jnp.tanh(math.sqrt(2.0 / math.pi) * (x + 0.044715 * (x * x * x))))


def _odd_kernel(x_ref, gm_ref, win_ref, vg_ref, wsp_ref, bsp_ref, wout_ref, gf_ref, wrh_ref, wrl_ref,
                rbias_ref, x1_ref, xn_ref, lg_ref, *rest, half, groups, l, emit_v):
    if emit_v:
        vn_ref, u_s, s_s = rest
    else:
        vn_ref = None
        u_s, vn_s, s_s = rest
    tm = x_ref.shape[0]
    x = x_ref[...]
    xn = _rms(x, gm_ref[...]).astype(BF16)
    cw = 512
    vbuf = vn_ref if emit_v else vn_s
    for cidx in range(half // cw):
        u_s[:, cidx * cw:(cidx + 1) * cw] = _gelu(_dot(xn, win_ref[:, cidx * cw:(cidx + 1) * cw]))
        vbuf[:, cidx * cw:(cidx + 1) * cw] = _gelu(
            _dot(xn, win_ref[:, half + cidx * cw:half + (cidx + 1) * cw]))
    vbuf[...] = _rms(vbuf[...], vg_ref[...])
    gw = half // groups
    row = lax.broadcasted_iota(I32, (l, l), 0)
    col = lax.broadcasted_iota(I32, (l, l), 1)
    for g in range(groups):
        wg = jnp.where(col <= row, wsp_ref[g], 0.0).astype(BF16)
        bg = bsp_ref[:, g:g + 1]
        for ci in range(tm // l):
            vv = vbuf[ci * l:(ci + 1) * l, g * gw:(g + 1) * gw].astype(BF16)
            s_s[ci * l:(ci + 1) * l, g * gw:(g + 1) * gw] = _dot(wg, vv) + bg
    y = _dot((u_s[...] * s_s[...]).astype(BF16), wout_ref[...])
    _ffn_prologue(x + y, gf_ref, wrh_ref, wrl_ref, rbias_ref, x1_ref, xn_ref, lg_ref)


def _odd_mixer(x, g_mix, win_bf, v_gain, wsp, bsp_t, wout_bf, g_ffn, wrh, wrl, rbias, *, l, tm, emit_v):
    n, d = x.shape
    half = wout_bf.shape[0]
    groups = wsp.shape[0]
    nr = wrh.shape[0]
    assert n % tm == 0 and tm % l == 0
    row = lambda i: (i, 0)
    out_specs = [pl.BlockSpec((tm, d), row), pl.BlockSpec((tm * (d // LANES), LANES), row),
                 pl.BlockSpec((nr, tm), lambda i: (0, i))]
    out_shape = [jax.ShapeDtypeStruct((n, d), F32), jax.ShapeDtypeStruct((n * (d // LANES), LANES), F32),
                 jax.ShapeDtypeStruct((nr, n), F32)]
    scratch = [pltpu.VMEM((tm, half), F32)]
    if emit_v:
        out_specs.append(pl.BlockSpec((tm, half), row))
        out_shape.append(jax.ShapeDtypeStruct((n, half), F32))
    else:
        scratch.append(pltpu.VMEM((tm, half), F32))
    scratch.append(pltpu.VMEM((tm, half), F32))
    return pl.pallas_call(
        functools.partial(_odd_kernel, half=half, groups=groups, l=l, emit_v=emit_v),
        grid=(n // tm,),
        in_specs=[pl.BlockSpec((tm, d), row), _const_spec((1, d)), _const_spec(win_bf.shape),
                  _const_spec((1, half)), _const_spec(wsp.shape), _const_spec(bsp_t.shape),
                  _const_spec(wout_bf.shape), _const_spec((1, d)), _const_spec(wrh.shape),
                  _const_spec(wrl.shape), _const_spec(rbias.shape)],
        out_specs=out_specs,
        out_shape=out_shape,
        scratch_shapes=scratch,
        compiler_params=_params("arbitrary"),
        name="odd_mixer",
    )(x, g_mix, win_bf, v_gain, wsp, bsp_t, wout_bf, g_ffn, wrh, wrl, rbias)


def _route_kernel(lg_ref, tri_ref, init_ref, e_ref, g_ref, r_ref, p_ref, cnt_ref, base_ref, tcnt_ref, run_s,
                  *, groups, epg):
    i = pl.program_id(0)

    @pl.when(i == 0)
    def _():
        run_s[...] = init_ref[...]

    lg = lg_ref[...]
    tr = lg.shape[1]
    gl = [lg[g:g + 1] for g in range(groups)]
    m = functools.reduce(jnp.maximum, gl)
    grp = jnp.full((1, tr), groups - 1, I32)
    for g in range(groups - 2, -1, -1):
        grp = jnp.where(gl[g] == m, g, grp)
    gate_g = 1.0 / functools.reduce(lambda a, b: a + b, [jnp.exp(x - m) for x in gl])
    sel = lg[SUBLANES + (groups - 1) * epg:SUBLANES + groups * epg]
    for g in range(groups - 2, -1, -1):
        sel = jnp.where(grp == g, lg[SUBLANES + g * epg:SUBLANES + (g + 1) * epg], sel)
    sub = lax.broadcasted_iota(I32, sel.shape, 0)
    v1 = jnp.max(sel, axis=0, keepdims=True)
    i1 = jnp.min(jnp.where(sel == v1, sub, epg), axis=0, keepdims=True)
    sel2 = jnp.where(sub == i1, -jnp.inf, sel)
    v2 = jnp.max(sel2, axis=0, keepdims=True)
    i2 = jnp.min(jnp.where(sel2 == v2, sub, epg), axis=0, keepdims=True)
    tt = jnp.exp(v2 - v1)
    g1 = gate_g / (1.0 + tt)
    g2 = gate_g * tt / (1.0 + tt)
    e1 = grp * epg + i1
    e2 = grp * epg + i2
    ne = groups * epg
    eidx = lax.broadcasted_iota(I32, (ne, tr), 0)
    oh1 = eidx == e1
    oh2 = eidx == e2
    cnt = jnp.where(oh1, 1.0, 0.0) + jnp.where(oh2, 1.0, 0.0)
    local = _dot(cnt.astype(BF16), tri_ref[...])
    before = run_s[:, 0:1] + local
    r1 = jnp.sum(jnp.where(oh1, before, 0.0), axis=0, keepdims=True)
    r2 = jnp.sum(jnp.where(oh2, before, 0.0), axis=0, keepdims=True)
    tile_cnt = jnp.broadcast_to(jnp.sum(cnt, axis=1, keepdims=True), run_s.shape)
    padded = jnp.ceil(tile_cnt * (1.0 / RUN_CHUNK)) * RUN_CHUNK
    offset = _cumsum_rows(padded) - padded
    where_local = offset[:, 0:1] + local
    p1 = jnp.sum(jnp.where(oh1, where_local, 0.0), axis=0, keepdims=True)
    p2 = jnp.sum(jnp.where(oh2, where_local, 0.0), axis=0, keepdims=True)
    base_ref[...] = run_s[...]
    tcnt_ref[...] = tile_cnt
    run_s[...] = run_s[...] + tile_cnt
    rows = lax.broadcasted_iota(I32, (SUBLANES, tr), 0)
    e_ref[...] = jnp.where(rows == 0, e1, jnp.where(rows == 1, e2, 0))
    g_ref[...] = jnp.where(rows == 0, g1, jnp.where(rows == 1, g2, 0.0))
    r_ref[...] = jnp.where(rows == 0, r1, jnp.where(rows == 1, r2, 0.0)).astype(I32)
    p_ref[...] = jnp.where(rows == 0, p1, jnp.where(rows == 1, p2, 0.0)).astype(I32)
    cnt_ref[...] = run_s[...]


def _route(lgt, init, *, groups, epg, tr):
    nr, n = lgt.shape
    assert n % tr == 0 and nr == SUBLANES + groups * epg
    ne = groups * epg
    tri = (jnp.arange(tr)[:, None] < jnp.arange(tr)[None, :]).astype(BF16)
    tok = pl.BlockSpec((SUBLANES, tr), lambda i: (0, i))
    per_tile = pl.BlockSpec((ne, LANES), lambda i: (i, 0))
    tile_tab = jax.ShapeDtypeStruct((n // tr * ne, LANES), F32)
    return pl.pallas_call(
        functools.partial(_route_kernel, groups=groups, epg=epg),
        grid=(n // tr,),
        in_specs=[pl.BlockSpec((nr, tr), lambda i: (0, i)), _const_spec((tr, tr)), _const_spec((ne, LANES))],
        out_specs=[tok, tok, tok, tok, pl.BlockSpec((ne, LANES), lambda i: (0, 0)), per_tile, per_tile],
        out_shape=[jax.ShapeDtypeStruct((SUBLANES, n), I32), jax.ShapeDtypeStruct((SUBLANES, n), F32),
                   jax.ShapeDtypeStruct((SUBLANES, n), I32), jax.ShapeDtypeStruct((SUBLANES, n), I32),
                   jax.ShapeDtypeStruct((ne, LANES), F32), tile_tab, tile_tab],
        scratch_shapes=[pltpu.VMEM((ne, LANES), F32)],
        compiler_params=_params("arbitrary"),
        name="route",
    )(lgt, tri, init)


ISSUE_UNROLL = 8
DISPATCH_SLOTS = 3


def _dispatch_kernel(seg_ref, dst_ref, *rest, s, bm, tiles):
    x_refs = rest[:len(tiles)]
    buf_ref, zero_s, xin_s, sem, isem, zsem = rest[len(tiles):]
    tp = dst_ref.shape[2]
    blk = bm * s

    def zero_fill():
        zero_s[...] = jnp.zeros(zero_s.shape, F32)

        def block_copy(b):
            return pltpu.make_async_copy(zero_s, buf_ref.at[pl.ds(pl.multiple_of(b * blk, blk), blk)], zsem)

        for e in range(seg_ref.shape[1]):
            @pl.when(seg_ref[1, e] > 0)
            def _():
                block_copy(seg_ref[0, e] // bm - 1).start()

            @pl.when(seg_ref[1, e] - seg_ref[2, e] > bm)
            def _():
                block_copy(seg_ref[0, e] // bm - 2).start()
        for e in range(seg_ref.shape[1]):
            @pl.when(seg_ref[1, e] > 0)
            def _():
                block_copy(0).wait()

            @pl.when(seg_ref[1, e] - seg_ref[2, e] > bm)
            def _():
                block_copy(0).wait()

        def tail_start(b, c):
            block_copy(b).start()
            return c

        def tail_wait(b, c):
            block_copy(b).wait()
            return c

        first_unused = seg_ref[0, seg_ref.shape[1] - 1] // bm
        n_blocks = buf_ref.shape[0] // blk
        lax.fori_loop(first_unused, n_blocks, tail_start, 0)
        lax.fori_loop(first_unused, n_blocks, tail_wait, 0)

    pl.when(pl.program_id(0) == 0)(zero_fill)

    i = pl.program_id(0)
    last = pl.num_programs(0) - 1
    rows = tp * s

    def tile_in(t, go):
        lo = 0
        for x_ref, nt in zip(x_refs, tiles):
            @pl.when((t >= lo) & (t < lo + nt))
            def _(x_ref=x_ref, lo=lo):
                cp = pltpu.make_async_copy(x_ref.at[pl.ds(pl.multiple_of((t - lo) * rows, rows), rows)],
                                           xin_s.at[t % DISPATCH_SLOTS], isem.at[t % DISPATCH_SLOTS])
                cp.start() if go else cp.wait()
            lo += nt

    def rows_out_wait(t):
        for kk in range(TOP_K):
            pltpu.make_async_copy(xin_s.at[t % DISPATCH_SLOTS], buf_ref.at[pl.ds(0, rows)],
                                  sem.at[t % DISPATCH_SLOTS]).wait()

    @pl.when(i == 0)
    def _():
        tile_in(i, True)

    @pl.when(i >= DISPATCH_SLOTS - 1)
    def _():
        rows_out_wait(i - (DISPATCH_SLOTS - 1))

    @pl.when(i < last)
    def _():
        tile_in(i + 1, True)

    tile_in(i, False)
    slot = i % DISPATCH_SLOTS

    def issue(r0, c):
        for u in range(ISSUE_UNROLL):
            r = r0 * ISSUE_UNROLL + u
            for kk in range(TOP_K):
                pltpu.make_async_copy(xin_s.at[slot, _token_tile(r, s)],
                                      buf_ref.at[_token_tile(dst_ref[0, kk, r], s)], sem.at[slot]
                                      ).start(priority=kk % 2)
        return c

    lax.fori_loop(0, tp // ISSUE_UNROLL, issue, 0)

    @pl.when(i == last)
    def _():
        for back in range(DISPATCH_SLOTS - 2, -1, -1):
            @pl.when(i >= back)
            def _():
                rows_out_wait(i - back)


def _dispatch(seg, dest3, xns, n_slots, *, tp, s, bm):
    assert tp % ISSUE_UNROLL == 0 and all(x.shape[0] % (tp * s) == 0 for x in xns)
    tiles = tuple(x.shape[0] // (tp * s) for x in xns)
    grid_spec = pltpu.PrefetchScalarGridSpec(
        num_scalar_prefetch=1,
        grid=(sum(tiles),),
        in_specs=[pl.BlockSpec((1, TOP_K, tp), lambda i, sg: (i, 0, 0), memory_space=pltpu.SMEM)]
        + [pl.BlockSpec(memory_space=pl.ANY)] * len(xns),
        out_specs=pl.BlockSpec(memory_space=pl.ANY),
        scratch_shapes=[pltpu.VMEM((bm * s, LANES), F32), pltpu.VMEM((DISPATCH_SLOTS, tp * s, LANES), F32),
                        pltpu.SemaphoreType.DMA((DISPATCH_SLOTS,)), pltpu.SemaphoreType.DMA((DISPATCH_SLOTS,)),
                        pltpu.SemaphoreType.DMA(())],
    )
    return pl.pallas_call(
        functools.partial(_dispatch_kernel, s=s, bm=bm, tiles=tiles),
        grid_spec=grid_spec,
        out_shape=jax.ShapeDtypeStruct((n_slots * s, LANES), F32),
        compiler_params=_params("arbitrary"),
        name="dispatch",
    )(seg, dest3, *xns)


def _expert_kernel(be_ref, nu_ref, x_ref, wg_ref, wu_ref, wd_ref, o_ref, wg_s, wu_s, wd_s):
    b = pl.program_id(0)
    s = wg_s.shape[0] // LANES
    prev = be_ref[jnp.maximum(b - 1, 0)]

    @pl.when((b == 0) | (be_ref[b] != prev))
    def _():
        wg_s[...] = wg_ref[0, 0].astype(BF16)
        wu_s[...] = wu_ref[0, 0].astype(BF16)
        wd_s[...] = wd_ref[0, 0].astype(BF16)

    @pl.when(b < nu_ref[0])
    def _():
        xb = _load_token_tiles(x_ref, x_ref.shape[0] // s, s).astype(BF16)
        gate = _dot(xb, wg_s[...])
        h = gate * _sigmoid(gate) * _dot(xb, wu_s[...])
        _store_token_tiles(o_ref, _dot(h.astype(BF16), wd_s[...]))

    @pl.when(b >= nu_ref[0])
    def _():
        o_ref[...] = jnp.zeros(o_ref.shape, F32)


def _experts(blk_expert, n_used, buf, w_gate, w_up, w_down, *, layer, bm):
    d, de = w_gate.shape[2:]
    s = d // LANES
    n_blocks = buf.shape[0] // (bm * s)
    rows = pl.BlockSpec((bm * s, LANES), lambda b, be, nu: (b, 0))
    used_rows = pl.BlockSpec((bm * s, LANES), lambda b, be, nu: (jnp.minimum(b, nu[0] - 1), 0))
    grid_spec = pltpu.PrefetchScalarGridSpec(
        num_scalar_prefetch=2,
        grid=(n_blocks,),
        in_specs=[used_rows,
                  pl.BlockSpec((1, 1, d, de), lambda b, be, nu: (layer, be[b], 0, 0)),
                  pl.BlockSpec((1, 1, d, de), lambda b, be, nu: (layer, be[b], 0, 0)),
                  pl.BlockSpec((1, 1, de, d), lambda b, be, nu: (layer, be[b], 0, 0))],
        out_specs=rows,
        scratch_shapes=[pltpu.VMEM((d, de), BF16), pltpu.VMEM((d, de), BF16), pltpu.VMEM((de, d), BF16)],
    )
    return pl.pallas_call(
        _expert_kernel,
        grid_spec=grid_spec,
        out_shape=jax.ShapeDtypeStruct(buf.shape, F32),
        compiler_params=_params("arbitrary"),
        name="experts",
    )(blk_expert, n_used, buf, w_gate, w_up, w_down)


def _combine_kernel(tab_ref, nxt_ref, pos_ref, gate_ref, x_ref, yb_ref, o_ref, stg_s, tt_s, sem, *, ahead):
    i = pl.program_id(0)
    tq, d = x_ref.shape
    s = d // LANES
    piece = RUN_CHUNK * s
    slot = i % 2

    def run_copies(ref, dst_slot, go):
        for e in range(ref.shape[2]):
            def body(j, c):
                src = pl.multiple_of((ref[0, 0, e] + j * RUN_CHUNK) * s, s)
                dst = pl.multiple_of((ref[0, 2, e] + j * RUN_CHUNK) * s, piece)
                cp = pltpu.make_async_copy(yb_ref.at[pl.ds(src, piece)],
                                           stg_s.at[dst_slot, pl.ds(dst, piece)], sem.at[dst_slot])
                cp.start() if go else cp.wait()
                return c
            lax.fori_loop(0, ref[0, 1, e], body, 0)

    if ahead:
        @pl.when(i == 0)
        def _():
            run_copies(tab_ref, 0, True)

        @pl.when(i + 1 < pl.num_programs(0))
        def _():
            run_copies(nxt_ref, 1 - slot, True)
    else:
        run_copies(tab_ref, slot, True)

    run_copies(tab_ref, slot, False)

    def assemble(r0, c):
        for u in range(ISSUE_UNROLL):
            r = r0 * ISSUE_UNROLL + u
            acc = gate_ref[0, 0, r] * stg_s[slot, _token_tile(pos_ref[0, 0, r], s), :]
            for kk in range(1, TOP_K):
                acc = acc + gate_ref[0, kk, r] * stg_s[slot, _token_tile(pos_ref[0, kk, r], s), :]
            tt_s[_token_tile(r, s), :] = acc
        return c

    lax.fori_loop(0, tq // ISSUE_UNROLL, assemble, 0)
    o_ref[...] = x_ref[...] + _load_token_tiles(tt_s, tq, s)


def _combine(tab, pos3, gates3, x1, yb, *, tq):
    n, d = x1.shape
    s = d // LANES
    nt = n // tq
    ne = tab.shape[2]
    assert tq % ISSUE_UNROLL == 0
    stage_rows = tq * TOP_K + ne * RUN_CHUNK
    smem = lambda shape, imap: pl.BlockSpec(shape, imap, memory_space=pltpu.SMEM)
    return pl.pallas_call(
        functools.partial(_combine_kernel, ahead=nt > 2),
        grid=(nt,),
        in_specs=[smem((1, 3, ne), lambda i: (i, 0, 0)),
                  smem((1, 3, ne), lambda i: (jnp.minimum(i + 1, nt - 1), 0, 0)),
                  smem((1, TOP_K, tq), lambda i: (i, 0, 0)),
                  smem((1, TOP_K, tq), lambda i: (i, 0, 0)),
                  pl.BlockSpec((tq, d), lambda i: (i, 0)),
                  pl.BlockSpec(memory_space=pl.ANY)],
        out_specs=pl.BlockSpec((tq, d), lambda i: (i, 0)),
        out_shape=jax.ShapeDtypeStruct((n, d), F32),
        scratch_shapes=[pltpu.VMEM((2, stage_rows * s, LANES), F32), pltpu.VMEM((tq * s, LANES), F32),
                        pltpu.SemaphoreType.DMA((2,))],
        compiler_params=_params("arbitrary"),
        name="combine",
    )(tab, tab, pos3, gates3, x1, yb)


def _moe(streams, w_gate, w_up, w_down, *, layer, groups, epg):
    d = streams[0][0].shape[1]
    ne = groups * epg
    routed = []
    taken = jnp.zeros((ne, LANES), F32)
    for x1, _, lgt in streams:
        rtile = min(512, x1.shape[0])
        routed.append(_route(lgt, taken, groups=groups, epg=epg, tr=rtile) + (rtile,))
        taken = routed[-1][4]
    n_rows = sum(x1.shape[0] for x1, _, _ in streams) * TOP_K
    counts = taken[:, 0].astype(I32)
    bm = MOE_BLOCK if n_rows >= 2 * ne * MOE_BLOCK else MOE_BLOCK_SMALL
    padded = (counts + RUN_CHUNK + bm - 1) // bm * bm
    pend = jnp.cumsum(padded)
    pstart = pend - padded
    n_blocks = -(-(n_rows + ne * RUN_CHUNK) // bm) + ne
    eids = jnp.arange(ne, dtype=I32)
    blk_row = jnp.arange(n_blocks, dtype=I32)[:, None] * bm
    blk_expert = jnp.minimum(jnp.sum((pend[None, :] <= blk_row).astype(I32), axis=1), ne - 1)
    n_used = (pend[-1:] // bm).astype(I32)
    seg_tab = jnp.stack([pend, padded, counts]).astype(I32)
    dtile = min(r[-1] for r in routed)
    dests = []
    for (x1, _, _), (e8, g8, r8, p8, cnt, base, tcnt, rtile) in zip(streams, routed):
        seg = jnp.sum(jnp.where(e8[:TOP_K, :, None] == eids, pstart, 0), axis=-1)
        dest = seg + r8[:TOP_K]
        dests.append(dest.reshape(TOP_K, x1.shape[0] // dtile, dtile).transpose(1, 0, 2))
    buf = _dispatch(seg_tab, jnp.concatenate(dests, axis=0), [xn for _, xn, _ in streams], n_blocks * bm,
                    tp=dtile, s=d // LANES, bm=bm)
    yb = _experts(blk_expert, n_used, buf, w_gate, w_up, w_down, layer=layer, bm=bm)
    outs = []
    for (x1, _, _), (e8, g8, r8, p8, cnt, base, tcnt, rtile) in zip(streams, routed):
        nrt = x1.shape[0] // rtile
        run_start = pstart[None, :] + base.reshape(nrt, ne, LANES)[:, :, 0].astype(I32)
        pieces = (tcnt.reshape(nrt, ne, LANES)[:, :, 0].astype(I32) + RUN_CHUNK - 1) // RUN_CHUNK
        stage = (jnp.cumsum(pieces, axis=1) - pieces) * RUN_CHUNK
        tab = jnp.stack([run_start, pieces, stage], axis=1)
        by_tile = lambda a: a[:TOP_K].reshape(TOP_K, nrt, rtile).transpose(1, 0, 2)
        outs.append(_combine(tab, by_tile(p8), by_tile(g8), x1, yb, tq=rtile))
    return outs


def _router_weights(wg, bg, we, be):
    d, groups = wg.shape
    epg = we.shape[2]
    assert groups <= SUBLANES and epg == SUBLANES
    pad = jnp.zeros((SUBLANES - groups, d), F32)
    wr = jnp.concatenate([wg.T, pad, we.transpose(0, 2, 1).reshape(groups * epg, d)], axis=0)
    rb = jnp.concatenate([bg, jnp.zeros((SUBLANES - groups,), F32), be.reshape(-1)])[:, None]
    hi = wr.astype(BF16)
    lo = (wr - hi.astype(F32)).astype(BF16)
    return hi, lo, rb, groups, epg


def kernel(x_prompt, x_sample, cache_attn_k, cache_attn_v, state_hgrn, rel_bias, norm_mix, norm_ffn,
           w_in_even, w_out_even, q_norm_gain, k_norm_gain, lam_q1, lam_k1, lam_q2, lam_k2, da_out_gain,
           hgrn_lb_logits, hgrn_out_gain, w_in_odd, sgu_v_gain, sgu_w, sgu_b, w_out_odd,
           router_group_w, router_group_b, router_expert_w, router_expert_b,
           expert_w_gate, expert_w_up, expert_w_down):
    bp, tp, d = x_prompt.shape
    bs, ts, _ = x_sample.shape
    depth = norm_mix.shape[0]
    _, _, past, da_heads, _, da_dh = cache_attn_k.shape
    da_dv = cache_attn_v.shape[-1]
    _, _, hg_heads, hg_dk, hg_dv = state_hgrn.shape
    width = da_heads * da_dv
    assert width == da_heads * 2 * da_dh == hg_heads * hg_dk == hg_heads * hg_dv
    assert da_dv == LANES and hg_dk == LANES and hg_dv == LANES

    lb_all = jnp.cumsum(jax.nn.softmax(hgrn_lb_logits.astype(F32), axis=0), axis=0)
    gid = jnp.arange(width) // da_dh
    pm = jnp.where(gid[:, None] == gid[None, :], 1.0 / da_dh, 0.0).astype(BF16)

    xs = {"p": x_prompt.reshape(bp * tp, d), "s": x_sample.reshape(bs * ts, d)}
    dims = {"p": (bp, tp), "s": (bs, ts)}
    outs = {"p": {}, "s": {}}
    kp_l, vp_l, ks_l, vs_l, sp_l, ss_l, sgu_l = [], [], [], [], [], [], []

    for layer in range(depth):
        j = layer // 2
        wrh, wrl, rbias, groups, epg = _router_weights(
            router_group_w[layer], router_group_b[layer], router_expert_w[layer], router_expert_b[layer])
        g_mix = norm_mix[layer][None, :]
        g_ffn = norm_ffn[layer][None, :]
        if layer % 2 == 0:
            lam_init = 0.8 - 0.6 * math.exp(-0.3 * layer)
            w_in_bf = w_in_even[j].astype(BF16)
            w_out_bf = w_out_even[j].astype(BF16)
            reps = width // da_dh
            qg = jnp.tile(q_norm_gain[j], reps)[None, :]
            kg = jnp.tile(k_norm_gain[j], reps)[None, :]
            lam4 = jnp.stack([lam_q1[j], lam_k1[j], lam_q2[j], lam_k2[j]])
            sub_gain = da_out_gain[j][None, :]
            hg_gain = hgrn_out_gain[j][None, :]
            lb = lb_all[j][None, :]
            for key in ("p", "s"):
                b, t = dims[key]
                x = xs[key]
                q, k, v, qh, kb, lf, ih, gs, k5, v4 = _in_even(
                    x, g_mix, w_in_bf, pm, qg, kg, lb, width=width, q_scale=da_dh ** -0.5 * LOG2E,
                    tm=min(512, b * t), batch=b, seq=t, heads=da_heads, dh=da_dh)
                if key == "p":
                    oa = _attn_prompt(q, k, v, rel_bias, lam4, sub_gain, batch=b, seq=t, heads=da_heads,
                                      dh=da_dh, lam_init=lam_init, qt=min(256, t))
                    s0 = jnp.zeros((b, hg_heads, hg_dk, hg_dv), F32)
                    ob, s_new = _hgrn(qh, kb, lf, ih, s0, hg_gain, batch=b, seq=t, heads=hg_heads,
                                      dk=hg_dk, dv=hg_dv, tb=min(512, t))
                    kp_l.append(k5)
                    vp_l.append(v4)
                    sp_l.append(s_new)
                else:
                    ck = cache_attn_k[j].reshape(b * past * da_heads * 2, da_dh)
                    cv = cache_attn_v[j].reshape(b * past * da_heads, da_dv)
                    oa = _attn_sample(q, k, v, ck, cv, rel_bias, lam4, sub_gain, batch=b, t=t,
                                      heads=da_heads, dh=da_dh, lam_init=lam_init)
                    ob, s_new = _hgrn(qh, kb, lf, ih, state_hgrn[j], hg_gain, batch=b, seq=t,
                                      heads=hg_heads, dk=hg_dk, dv=hg_dv, tb=t)
                    ks_l.append(k5)
                    vs_l.append(v4)
                    ss_l.append(s_new)
                outs[key] = _out_even(oa, ob, gs, x, w_out_bf, g_ffn, wrh, wrl, rbias, tm=min(512, b * t))
        else:
            w_in_bf = w_in_odd[j].astype(BF16)
            w_out_bf = w_out_odd[j].astype(BF16)
            v_gain = sgu_v_gain[j][None, :]
            for key in ("p", "s"):
                b, t = dims[key]
                l = min(SGU_CHUNK, t)
                res = _odd_mixer(xs[key], g_mix, w_in_bf, v_gain, sgu_w[j][:, :l, :l], sgu_b[j][:, :l].T,
                                 w_out_bf, g_ffn, wrh, wrl, rbias, l=l, tm=min(512, b * t), emit_v=(key == "s"))
                outs[key] = res[:3]
                if key == "s":
                    sgu_l.append(res[3].reshape(b, t, -1))
        xs["p"], xs["s"] = _moe([outs["p"], outs["s"]], expert_w_gate, expert_w_up, expert_w_down,
                                layer=layer, groups=groups, epg=epg)

    return (xs["p"].reshape(bp, tp, d), xs["s"].reshape(bs, ts, d), jnp.stack(kp_l), jnp.stack(vp_l),
            jnp.stack(ks_l), jnp.stack(vs_l), jnp.stack(sp_l), jnp.stack(ss_l), jnp.stack(sgu_l))
```

```python
import functools
import math

import jax
import jax.numpy as jnp
from jax import lax
from jax.experimental import pallas as pl
from jax.experimental.pallas import tpu as pltpu

F32 = jnp.float32
BF16 = jnp.bfloat16
I32 = jnp.int32

EPS = 1e-6
LOG2E = math.log2(math.e)
CHUNK = 64
SGU_CHUNK = 128
REL_BUCKETS = 32
REL_MAX_DIST = 128
TOP_K = 2
MOE_BLOCK = 512
MOE_BLOCK_SMALL = 128
RUN_CHUNK = 32
HG_SUB = 16

LANES = 128
SUBLANES = 8
VMEM_LIMIT = 56 * 1024 * 1024

NT_DIMS = (((1,), (1,)), ((), ()))
TN_DIMS = (((0,), (0,)), ((), ()))


def _params(*sem):
    return pltpu.CompilerParams(dimension_semantics=sem, vmem_limit_bytes=VMEM_LIMIT)


def _const_spec(shape):
    nd = len(shape)
    return pl.BlockSpec(shape, lambda *_: (0,) * nd, pipeline_mode=pl.Buffered(1))


def _sigmoid(x):
    return 1.0 / (1.0 + jnp.exp(-x))


def _rms(x, g):
    return x * lax.rsqrt(jnp.mean(x * x, axis=-1, keepdims=True) + EPS) * g


def _dot(a, b):
    return jnp.dot(a, b, preferred_element_type=F32)


def rel_bucket(rel):
    half = REL_BUCKETS // 2
    max_exact = half // 2
    ret = (rel > 0).astype(I32) * half
    n = jnp.abs(rel)
    nf = jnp.maximum(n, 1).astype(F32)
    large = max_exact + (jnp.log(nf / max_exact) / math.log(REL_MAX_DIST / max_exact)
                         * (half - max_exact)).astype(I32)
    large = jnp.minimum(large, half - 1)
    return ret + jnp.where(n < max_exact, n, large)


def _in_even_kernel(x_ref, g_ref, w_ref, pm_ref, qg_ref, kg_ref, lb_ref,
                    q_ref, k_ref, v_ref, qh_ref, kb_ref, lf_ref, ih_ref, gs_ref, k5_ref, v4_ref, *, width,
                    q_scale):
    xn = _rms(x_ref[...], g_ref[...]).astype(BF16)

    def proj(c):
        return _dot(xn, w_ref[:, c * width:(c + 1) * width])

    def group_norm(y, gain):
        ms = _dot((y * y).astype(BF16), pm_ref[...])
        return y * lax.rsqrt(ms + EPS) * gain

    q_ref[...] = group_norm(proj(0), qg_ref[...]) * q_scale
    kn = group_norm(proj(1), kg_ref[...])
    k_ref[...] = kn
    bb, ts, heads, _, dh = k5_ref.shape
    for h in range(heads):
        for c in range(2):
            piece = kn[:, (2 * h + c) * dh:(2 * h + c + 1) * dh]
            k5_ref[:, :, h, c, :] = piece.reshape(bb, ts, dh)
    vv = proj(2)
    v_ref[...] = vv
    for h in range(heads):
        v4_ref[:, :, h, :] = vv[:, h * 2 * dh:(h + 1) * 2 * dh].reshape(bb, ts, 2 * dh)
    yq = proj(3)
    qh_ref[...] = yq * _sigmoid(yq)
    zf = proj(4)
    lb = lb_ref[...]
    lf_ref[...] = jnp.log(lb + (1.0 - lb) * _sigmoid(zf))
    kb_ref[...] = (1.0 - lb) * _sigmoid(-zf)
    ih_ref[...] = proj(5)
    yg = proj(6)
    gs_ref[...] = yg * _sigmoid(yg)


def _in_even(x, g_mix, w_bf, pm, qg, kg, lb, *, width, q_scale, tm, batch, seq, heads, dh):
    n, d = x.shape
    assert n % tm == 0 and (tm % seq == 0 or seq % tm == 0)
    row = lambda i: (i, 0)
    out = jax.ShapeDtypeStruct((n, width), F32)
    bb, ts = max(tm // seq, 1), min(tm, seq)
    per = seq // ts
    k5_spec = pl.BlockSpec((bb, ts, heads, 2, dh), lambda i: (i // per, i % per, 0, 0, 0))
    v4_spec = pl.BlockSpec((bb, ts, heads, 2 * dh), lambda i: (i // per, i % per, 0, 0))
    return pl.pallas_call(
        functools.partial(_in_even_kernel, width=width, q_scale=q_scale),
        grid=(n // tm,),
        in_specs=[pl.BlockSpec((tm, d), row), _const_spec((1, d)), _const_spec(w_bf.shape),
                  _const_spec(pm.shape), _const_spec((1, width)), _const_spec((1, width)),
                  _const_spec((1, width))],
        out_specs=[pl.BlockSpec((tm, width), row)] * 8 + [k5_spec, v4_spec],
        out_shape=[out] * 8 + [jax.ShapeDtypeStruct((batch, seq, heads, 2, dh), F32),
                               jax.ShapeDtypeStruct((batch, seq, heads, 2 * dh), F32)],
        compiler_params=_params("arbitrary"),
        name="in_even",
    )(x, g_mix, w_bf, pm, qg, kg, lb)


def _bias_from_buckets(bk, rb_ref, h):
    b = jnp.zeros(bk.shape, F32)
    for u in range(REL_BUCKETS):
        b = jnp.where(bk == u, rb_ref[u, h], b)
    return jnp.where(bk < 0, -jnp.inf, b)


def _lam(lam_ref, lam_init):
    r = lam_ref[...]
    s1 = jnp.sum(r[0:1] * r[1:2], axis=1, keepdims=True)
    s2 = jnp.sum(r[2:3] * r[3:4], axis=1, keepdims=True)
    return jnp.exp(s1) - jnp.exp(s2) + lam_init


def _split_components(q, dh):
    lane = lax.broadcasted_iota(I32, q.shape, 1)
    q0 = jnp.where(lane < dh, q, 0.0)
    q1 = jnp.where(lane >= dh, q, 0.0)
    return jnp.concatenate([q0, q1], axis=0).astype(BF16)


def _attn_prompt_kernel(rb_ref, far_ref, lam_ref, bk_ref, sg_ref, q_ref, k_ref, v_ref, o_ref, bias_s, *,
                        qt, dh, lam_init):
    h = pl.program_id(0)
    t = q_ref.shape[0]
    kb = k_ref[...].astype(BF16)
    vt = v_ref[...].T.astype(BF16)
    qtr = q_ref[...].T
    sub = lax.broadcasted_iota(I32, (2 * dh, qt), 0)

    @pl.when(pl.program_id(1) == 0)
    def _():
        for d in range(2):
            b = _bias_from_buckets(bk_ref[d], rb_ref, h) * LOG2E
            bias_s[d] = jnp.concatenate([b, b], axis=1)

    bias = [bias_s[0], bias_s[1]]
    far = rb_ref[far_ref[0], h] * LOG2E
    lam = _lam(lam_ref, lam_init)
    gain = sg_ref[...] * (1.0 - lam_init)
    for i in range(t // qt):
        qi = qtr[:, i * qt:(i + 1) * qt]
        qz = jnp.concatenate([jnp.where(sub < dh, qi, 0.0), jnp.where(sub >= dh, qi, 0.0)],
                             axis=1).astype(BF16)
        n = (i + 1) * qt
        parts = [(n - qt, n)]
        s = [_dot(kb[n - qt:n], qz) + bias[0]]
        shift = [0.0]
        if i >= 1:
            parts.append((n - 2 * qt, n - qt))
            s.append(_dot(kb[n - 2 * qt:n - qt], qz) + bias[1])
            shift.append(0.0)
        if i >= 2:
            parts.append((0, n - 2 * qt))
            s.append(_dot(kb[:n - 2 * qt], qz))
            shift.append(far)
        m = functools.reduce(jnp.maximum, [jnp.max(x, axis=0, keepdims=True) + c for x, c in zip(s, shift)])
        p = [jnp.exp2(x - (m - c)) for x, c in zip(s, shift)]
        l = functools.reduce(lambda a, b: a + b, [jnp.sum(x, axis=0, keepdims=True) for x in p])
        acc = functools.reduce(lambda a, b: a + b,
                               [_dot(vt[:, lo:hi], x.astype(BF16)) for (lo, hi), x in zip(parts, p)])
        o = acc / l
        out = o[:, :qt] - lam * o[:, qt:]
        out = out * lax.rsqrt(jnp.mean(out * out, axis=0, keepdims=True) + EPS) * gain
        o_ref[i * qt:(i + 1) * qt, :] = out.T


def _attn_prompt(q, k, v, rel_bias, lam4, sub_gain, *, batch, seq, heads, dh, lam_init, qt):
    n, w = q.shape
    dv = w // heads
    assert dv == 2 * dh and seq % qt == 0 and qt % CHUNK == 0
    kj = jnp.arange(qt, dtype=I32)[:, None]
    qi = jnp.arange(qt, dtype=I32)[None, :]
    bk0 = jnp.where((kj // CHUNK) <= (qi // CHUNK), rel_bucket(kj - qi), -1)
    bk1 = rel_bucket(kj - qi - qt)
    bk = jnp.stack([bk0, bk1]).astype(I32)
    assert qt + 1 >= REL_MAX_DIST
    far = rel_bucket(jnp.full((1,), -(qt + 1), I32))
    smem = pl.BlockSpec(memory_space=pltpu.SMEM)
    seq_blk = pl.BlockSpec((seq, dv), lambda h, b: (b, h))
    return pl.pallas_call(
        functools.partial(_attn_prompt_kernel, qt=qt, dh=dh, lam_init=lam_init),
        grid=(heads, batch),
        in_specs=[smem, smem, _const_spec(lam4.shape), _const_spec(bk.shape), _const_spec((dv, 1)),
                  seq_blk, seq_blk, seq_blk],
        out_specs=seq_blk,
        out_shape=jax.ShapeDtypeStruct((n, w), F32),
        scratch_shapes=[pltpu.VMEM((2, qt, 2 * qt), F32)],
        compiler_params=_params("arbitrary", "arbitrary"),
        name="attn_prompt",
    )(rel_bias, far, lam4, bk, sub_gain.T, q, k, v)


def _attn_sample_kernel(rb_ref, lam_ref, bkc_ref, bkn_ref, sg_ref, q_ref, kc_ref, vc_ref, kn_ref, vn_ref,
                        o_ref, bc_s, bn_s, *, t, dh, lam_init, heads):
    h = pl.program_id(1)
    qz = _split_components(q_ref[...], dh)

    @pl.when((pl.program_id(0) == 0) & (h == 0))
    def _():
        for hh in range(heads):
            bc = _bias_from_buckets(bkc_ref[...], rb_ref, hh) * LOG2E
            bn = _bias_from_buckets(bkn_ref[...], rb_ref, hh) * LOG2E
            bc_s[hh] = jnp.concatenate([bc, bc], axis=0)
            bn_s[hh] = jnp.concatenate([bn, bn], axis=0)

    sc = lax.dot_general(qz, kc_ref[0].astype(BF16), NT_DIMS, preferred_element_type=F32)
    sn = lax.dot_general(qz, kn_ref[...].astype(BF16), NT_DIMS, preferred_element_type=F32)
    sc = sc + bc_s[h]
    sn = sn + bn_s[h]
    m = jnp.maximum(jnp.max(sc, axis=1, keepdims=True), jnp.max(sn, axis=1, keepdims=True))
    pc = jnp.exp2(sc - m)
    pn = jnp.exp2(sn - m)
    l = jnp.sum(pc, axis=1, keepdims=True) + jnp.sum(pn, axis=1, keepdims=True)
    past = vc_ref.shape[0] // heads
    vc = vc_ref[pl.ds(h, past, stride=heads), :]
    acc = _dot(pc.astype(BF16), vc.astype(BF16)) + _dot(pn.astype(BF16), vn_ref[...].astype(BF16))
    o = acc / l
    out = o[:t] - _lam(lam_ref, lam_init) * o[t:]
    o_ref[...] = _rms(out, sg_ref[...]) * (1.0 - lam_init)


def _attn_sample(q, k_new, v_new, cache_k, cache_v, rel_bias, lam4, sub_gain, *, batch, t, heads, dh,
                 lam_init):
    n, w = q.shape
    dv = w // heads
    past = cache_k.shape[1]
    assert past % CHUNK == 0 and t <= CHUNK
    qpos = past + jnp.arange(t, dtype=I32)[:, None]
    bkc = rel_bucket(jnp.arange(past, dtype=I32)[None, :] - qpos).astype(I32)
    bkn = rel_bucket(past + jnp.arange(t, dtype=I32)[None, :] - qpos).astype(I32)
    smem = pl.BlockSpec(memory_space=pltpu.SMEM)
    new = pl.BlockSpec((t, dv), lambda b, h: (b, h))
    old_k = pl.BlockSpec((1, past, dv), lambda b, h: (b, 0, h))
    old_v = pl.BlockSpec((past * heads, dv), lambda b, h: (b, 0))
    return pl.pallas_call(
        functools.partial(_attn_sample_kernel, t=t, dh=dh, lam_init=lam_init, heads=heads),
        grid=(batch, heads),
        in_specs=[smem, _const_spec(lam4.shape), _const_spec(bkc.shape), _const_spec(bkn.shape),
                  _const_spec((1, dv)), new, old_k, old_v, new, new],
        out_specs=new,
        out_shape=jax.ShapeDtypeStruct((n, w), F32),
        scratch_shapes=[pltpu.VMEM((heads, 2 * t, past), F32), pltpu.VMEM((heads, 2 * t, t), F32)],
        compiler_params=_params("arbitrary", "arbitrary"),
        name="attn_sample",
    )(rel_bias, lam4, bkc, bkn, sub_gain, q, cache_k, cache_v, k_new, v_new)


def _cumsum_rows(x):
    c = x.shape[0]
    row = lax.broadcasted_iota(I32, x.shape, 0)
    s = 1
    while s < c:
        x = x + jnp.where(row >= s, pltpu.roll(x, s, axis=0), 0.0)
        s *= 2
    return x


def _hgrn_kernel(qh_ref, kb_ref, lf_ref, ih_ref, s0_ref, hg_ref, ob_ref, sf_ref, st_s, *,
                 heads, dk, dv, c, nsb):
    t = pl.program_id(1)

    @pl.when(t == 0)
    def _():
        for h in range(heads):
            st_s[h] = s0_ref[0, h].T

    tb = qh_ref.shape[0]
    row = lax.broadcasted_iota(I32, (c, c), 0)
    col = lax.broadcasted_iota(I32, (c, c), 1)
    causal = col <= row

    def chunk(ci, carry):
        r0 = pl.multiple_of(ci * c, c)
        for h in range(heads):
            rows = pl.ds(r0, c)
            q = qh_ref[rows, h * dk:(h + 1) * dk]
            k = kb_ref[rows, h * dk:(h + 1) * dk]
            v = ih_ref[rows, h * dv:(h + 1) * dv]
            b = _cumsum_rows(lf_ref[rows, h * dk:(h + 1) * dk])
            bl = b[c - 1:c]
            st = st_s[h]
            inter = lax.dot_general((q * jnp.exp(b)).astype(BF16), st.astype(BF16), NT_DIMS,
                                    preferred_element_type=F32)
            qs, ks = [], []
            for j in range(nsb):
                ref = b[j * HG_SUB + HG_SUB // 2:j * HG_SUB + HG_SUB // 2 + 1]
                qs.append(q * jnp.exp(b - ref))
                sub = slice(j * HG_SUB, (j + 1) * HG_SUB)
                ks.append(k[sub] * jnp.exp(ref - b[sub]))
            a_full = lax.dot_general(jnp.concatenate(qs, axis=0).astype(BF16),
                                     jnp.concatenate(ks, axis=0).astype(BF16), NT_DIMS,
                                     preferred_element_type=F32)
            att = jnp.zeros((c, c), F32)
            for j in range(nsb):
                att = jnp.where(col >= j * HG_SUB, a_full[j * c:(j + 1) * c], att)
            att = jnp.where(causal, att, 0.0)
            out = inter + _dot(att.astype(BF16), v.astype(BF16))
            ob_ref[rows, h * dv:(h + 1) * dv] = _rms(out, hg_ref[...])
            kdec = (k * jnp.exp(bl - b)).astype(BF16)
            st_s[h] = jnp.exp(bl) * st + lax.dot_general(v.astype(BF16), kdec, TN_DIMS,
                                                         preferred_element_type=F32)
        return carry

    lax.fori_loop(0, tb // c, chunk, 0, unroll=min(4, tb // c))

    @pl.when(t == pl.num_programs(1) - 1)
    def _():
        for h in range(heads):
            sf_ref[0, h] = st_s[h].T


def _hgrn(qh, kb, lf, ih, s0, hg_gain, *, batch, seq, heads, dk, dv, tb):
    n = qh.shape[0]
    c = min(CHUNK, seq)
    assert seq % tb == 0 and tb % c == 0 and c % HG_SUB == 0
    nt = seq // tb
    blk = lambda w: pl.BlockSpec((tb, w), lambda b, t: (b * nt + t, 0))
    st = pl.BlockSpec((1, heads, dk, dv), lambda b, t: (b, 0, 0, 0))
    return pl.pallas_call(
        functools.partial(_hgrn_kernel, heads=heads, dk=dk, dv=dv, c=c, nsb=c // HG_SUB),
        grid=(batch, nt),
        in_specs=[blk(heads * dk), blk(heads * dk), blk(heads * dk), blk(heads * dv), st,
                  _const_spec((1, dv))],
        out_specs=[blk(heads * dv), st],
        out_shape=[jax.ShapeDtypeStruct((n, heads * dv), F32),
                   jax.ShapeDtypeStruct((batch, heads, dk, dv), F32)],
        scratch_shapes=[pltpu.VMEM((heads, dv, dk), F32)],
        compiler_params=_params("arbitrary", "arbitrary"),
        name="hgrn",
    )(qh, kb, lf, ih, s0, hg_gain)


def _store_token_tiles(ref, x):
    rows, d = x.shape
    s = d // LANES
    for c in range(s):
        ref[pl.ds(c, rows, stride=s), :] = x[:, c * LANES:(c + 1) * LANES]


def _load_token_tiles(ref, rows, s):
    return jnp.concatenate([ref[pl.ds(c, rows, stride=s), :] for c in range(s)], axis=1)


def _token_tile(r, s):
    return pl.ds(pl.multiple_of(r * s, s), s)


def _ffn_prologue(x1, gf_ref, wrh_ref, wrl_ref, rbias_ref, x1_ref, xn_ref, lg_ref):
    x1_ref[...] = x1
    xn = _rms(x1, gf_ref[...])
    _store_token_tiles(xn_ref, xn)
    hi = xn.astype(BF16)
    lo = (xn - hi.astype(F32)).astype(BF16)
    nt = functools.partial(lax.dot_general, dimension_numbers=NT_DIMS, preferred_element_type=F32)
    lg_ref[...] = nt(wrh_ref[...], hi) + nt(wrh_ref[...], lo) + nt(wrl_ref[...], hi) + rbias_ref[...]


def _out_even_kernel(oa_ref, ob_ref, gs_ref, x_ref, w_ref, gf_ref, wrh_ref, wrl_ref, rbias_ref,
                     x1_ref, xn_ref, lg_ref):
    o = jnp.concatenate([oa_ref[...], ob_ref[...] * gs_ref[...]], axis=1).astype(BF16)
    x1 = x_ref[...] + _dot(o, w_ref[...])
    _ffn_prologue(x1, gf_ref, wrh_ref, wrl_ref, rbias_ref, x1_ref, xn_ref, lg_ref)


def _out_even(oa, ob, gs, x, w_bf, g_ffn, wrh, wrl, rbias, *, tm):
    n, d = x.shape
    w = oa.shape[1]
    nr = wrh.shape[0]
    row = lambda i: (i, 0)
    return pl.pallas_call(
        _out_even_kernel,
        grid=(n // tm,),
        in_specs=[pl.BlockSpec((tm, w), row), pl.BlockSpec((tm, w), row), pl.BlockSpec((tm, w), row),
                  pl.BlockSpec((tm, d), row), _const_spec(w_bf.shape), _const_spec((1, d)),
                  _const_spec(wrh.shape), _const_spec(wrl.shape), _const_spec(rbias.shape)],
        out_specs=[pl.BlockSpec((tm, d), row), pl.BlockSpec((tm * (d // LANES), LANES), row),
                   pl.BlockSpec((nr, tm), lambda i: (0, i))],
        out_shape=[jax.ShapeDtypeStruct((n, d), F32), jax.ShapeDtypeStruct((n * (d // LANES), LANES), F32),
                   jax.ShapeDtypeStruct((nr, n), F32)],
        compiler_params=_params("arbitrary"),
        name="out_even",
    )(oa, ob, gs, x, w_bf, g_ffn, wrh, wrl, rbias)


def _gelu(x):
    return 0.5 * x * (1.0 + jnp.tanh(math.sqrt(2.0 / math.pi) * (x + 0.044715 * (x * x * x))))


def _odd_kernel(x_ref, gm_ref, win_ref, vg_ref, wsp_ref, bsp_ref, wout_ref, gf_ref, wrh_ref, wrl_ref,
                rbias_ref, x1_ref, xn_ref, lg_ref, *rest, half, groups, l, emit_v):
    if emit_v:
        vn_ref, u_s, s_s = rest
    else:
        vn_ref = None
        u_s, vn_s, s_s = rest
    tm = x_ref.shape[0]
    x = x_ref[...]
    xn = _rms(x, gm_ref[...]).astype(BF16)
    cw = 512
    vbuf = vn_ref if emit_v else vn_s
    for cidx in range(half // cw):
        u_s[:, cidx * cw:(cidx + 1) * cw] = _gelu(_dot(xn, win_ref[:, cidx * cw:(cidx + 1) * cw]))
        vbuf[:, cidx * cw:(cidx + 1) * cw] = _gelu(
            _dot(xn, win_ref[:, half + cidx * cw:half + (cidx + 1) * cw]))
    vbuf[...] = _rms(vbuf[...], vg_ref[...])
    gw = half // groups
    row = lax.broadcasted_iota(I32, (l, l), 0)
    col = lax.broadcasted_iota(I32, (l, l), 1)
    for g in range(groups):
        wg = jnp.where(col <= row, wsp_ref[g], 0.0).astype(BF16)
        bg = bsp_ref[:, g:g + 1]
        for ci in range(tm // l):
            vv = vbuf[ci * l:(ci + 1) * l, g * gw:(g + 1) * gw].astype(BF16)
            s_s[ci * l:(ci + 1) * l, g * gw:(g + 1) * gw] = _dot(wg, vv) + bg
    y = _dot((u_s[...] * s_s[...]).astype(BF16), wout_ref[...])
    _ffn_prologue(x + y, gf_ref, wrh_ref, wrl_ref, rbias_ref, x1_ref, xn_ref, lg_ref)


def _odd_mixer(x, g_mix, win_bf, v_gain, wsp, bsp_t, wout_bf, g_ffn, wrh, wrl, rbias, *, l, tm, emit_v):
    n, d = x.shape
    half = wout_bf.shape[0]
    groups = wsp.shape[0]
    nr = wrh.shape[0]
    assert n % tm == 0 and tm % l == 0
    row = lambda i: (i, 0)
    out_specs = [pl.BlockSpec((tm, d), row), pl.BlockSpec((tm * (d // LANES), LANES), row),
                 pl.BlockSpec((nr, tm), lambda i: (0, i))]
    out_shape = [jax.ShapeDtypeStruct((n, d), F32), jax.ShapeDtypeStruct((n * (d // LANES), LANES), F32),
                 jax.ShapeDtypeStruct((nr, n), F32)]
    scratch = [pltpu.VMEM((tm, half), F32)]
    if emit_v:
        out_specs.append(pl.BlockSpec((tm, half), row))
        out_shape.append(jax.ShapeDtypeStruct((n, half), F32))
    else:
        scratch.append(pltpu.VMEM((tm, half), F32))
    scratch.append(pltpu.VMEM((tm, half), F32))
    return pl.pallas_call(
        functools.partial(_odd_kernel, half=half, groups=groups, l=l, emit_v=emit_v),
        grid=(n // tm,),
        in_specs=[pl.BlockSpec((tm, d), row), _const_spec((1, d)), _const_spec(win_bf.shape),
                  _const_spec((1, half)), _const_spec(wsp.shape), _const_spec(bsp_t.shape),
                  _const_spec(wout_bf.shape), _const_spec((1, d)), _const_spec(wrh.shape),
                  _const_spec(wrl.shape), _const_spec(rbias.shape)],
        out_specs=out_specs,
        out_shape=out_shape,
        scratch_shapes=scratch,
        compiler_params=_params("arbitrary"),
        name="odd_mixer",
    )(x, g_mix, win_bf, v_gain, wsp, bsp_t, wout_bf, g_ffn, wrh, wrl, rbias)


def _route_kernel(lg_ref, tri_ref, init_ref, e_ref, g_ref, r_ref, p_ref, cnt_ref, base_ref, tcnt_ref, run_s,
                  *, groups, epg):
    i = pl.program_id(0)

    @pl.when(i == 0)
    def _():
        run_s[...] = init_ref[...]

    lg = lg_ref[...]
    tr = lg.shape[1]
    gl = [lg[g:g + 1] for g in range(groups)]
    m = functools.reduce(jnp.maximum, gl)
    grp = jnp.full((1, tr), groups - 1, I32)
    for g in range(groups - 2, -1, -1):
        grp = jnp.where(gl[g] == m, g, grp)
    gate_g = 1.0 / functools.reduce(lambda a, b: a + b, [jnp.exp(x - m) for x in gl])
    sel = lg[SUBLANES + (groups - 1) * epg:SUBLANES + groups * epg]
    for g in range(groups - 2, -1, -1):
        sel = jnp.where(grp == g, lg[SUBLANES + g * epg:SUBLANES + (g + 1) * epg], sel)
    sub = lax.broadcasted_iota(I32, sel.shape, 0)
    v1 = jnp.max(sel, axis=0, keepdims=True)
    i1 = jnp.min(jnp.where(sel == v1, sub, epg), axis=0, keepdims=True)
    sel2 = jnp.where(sub == i1, -jnp.inf, sel)
    v2 = jnp.max(sel2, axis=0, keepdims=True)
    i2 = jnp.min(jnp.where(sel2 == v2, sub, epg), axis=0, keepdims=True)
    tt = jnp.exp(v2 - v1)
    g1 = gate_g / (1.0 + tt)
    g2 = gate_g * tt / (1.0 + tt)
    e1 = grp * epg + i1
    e2 = grp * epg + i2
    ne = groups * epg
    eidx = lax.broadcasted_iota(I32, (ne, tr), 0)
    oh1 = eidx == e1
    oh2 = eidx == e2
    cnt = jnp.where(oh1, 1.0, 0.0) + jnp.where(oh2, 1.0, 0.0)
    local = _dot(cnt.astype(BF16), tri_ref[...])
    before = run_s[:, 0:1] + local
    r1 = jnp.sum(jnp.where(oh1, before, 0.0), axis=0, keepdims=True)
    r2 = jnp.sum(jnp.where(oh2, before, 0.0), axis=0, keepdims=True)
    tile_cnt = jnp.broadcast_to(jnp.sum(cnt, axis=1, keepdims=True), run_s.shape)
    padded = jnp.ceil(tile_cnt * (1.0 / RUN_CHUNK)) * RUN_CHUNK
    offset = _cumsum_rows(padded) - padded
    where_local = offset[:, 0:1] + local
    p1 = jnp.sum(jnp.where(oh1, where_local, 0.0), axis=0, keepdims=True)
    p2 = jnp.sum(jnp.where(oh2, where_local, 0.0), axis=0, keepdims=True)
    base_ref[...] = run_s[...]
    tcnt_ref[...] = tile_cnt
    run_s[...] = run_s[...] + tile_cnt
    rows = lax.broadcasted_iota(I32, (SUBLANES, tr), 0)
    e_ref[...] = jnp.where(rows == 0, e1, jnp.where(rows == 1, e2, 0))
    g_ref[...] = jnp.where(rows == 0, g1, jnp.where(rows == 1, g2, 0.0))
    r_ref[...] = jnp.where(rows == 0, r1, jnp.where(rows == 1, r2, 0.0)).astype(I32)
    p_ref[...] = jnp.where(rows == 0, p1, jnp.where(rows == 1, p2, 0.0)).astype(I32)
    cnt_ref[...] = run_s[...]


def _route(lgt, init, *, groups, epg, tr):
    nr, n = lgt.shape
    assert n % tr == 0 and nr == SUBLANES + groups * epg
    ne = groups * epg
    tri = (jnp.arange(tr)[:, None] < jnp.arange(tr)[None, :]).astype(BF16)
    tok = pl.BlockSpec((SUBLANES, tr), lambda i: (0, i))
    per_tile = pl.BlockSpec((ne, LANES), lambda i: (i, 0))
    tile_tab = jax.ShapeDtypeStruct((n // tr * ne, LANES), F32)
    return pl.pallas_call(
        functools.partial(_route_kernel, groups=groups, epg=epg),
        grid=(n // tr,),
        in_specs=[pl.BlockSpec((nr, tr), lambda i: (0, i)), _const_spec((tr, tr)), _const_spec((ne, LANES))],
        out_specs=[tok, tok, tok, tok, pl.BlockSpec((ne, LANES), lambda i: (0, 0)), per_tile, per_tile],
        out_shape=[jax.ShapeDtypeStruct((SUBLANES, n), I32), jax.ShapeDtypeStruct((SUBLANES, n), F32),
                   jax.ShapeDtypeStruct((SUBLANES, n), I32), jax.ShapeDtypeStruct((SUBLANES, n), I32),
                   jax.ShapeDtypeStruct((ne, LANES), F32), tile_tab, tile_tab],
        scratch_shapes=[pltpu.VMEM((ne, LANES), F32)],
        compiler_params=_params("arbitrary"),
        name="route",
    )(lgt, tri, init)


ISSUE_UNROLL = 16
DISPATCH_SLOTS = 3


def _dispatch_kernel(seg_ref, dst_ref, *rest, s, bm, tiles):
    x_refs = rest[:len(tiles)]
    buf_ref, zero_s, xin_s, sem, isem, zsem = rest[len(tiles):]
    tp = dst_ref.shape[2]
    blk = bm * s

    def zero_fill():
        zero_s[...] = jnp.zeros(zero_s.shape, F32)

        def block_copy(b):
            return pltpu.make_async_copy(zero_s, buf_ref.at[pl.ds(pl.multiple_of(b * blk, blk), blk)], zsem)

        for e in range(seg_ref.shape[1]):
            @pl.when(seg_ref[1, e] > 0)
            def _():
                block_copy(seg_ref[0, e] // bm - 1).start()

            @pl.when(seg_ref[1, e] - seg_ref[2, e] > bm)
            def _():
                block_copy(seg_ref[0, e] // bm - 2).start()
        for e in range(seg_ref.shape[1]):
            @pl.when(seg_ref[1, e] > 0)
            def _():
                block_copy(0).wait()

            @pl.when(seg_ref[1, e] - seg_ref[2, e] > bm)
            def _():
                block_copy(0).wait()

        def tail_start(b, c):
            block_copy(b).start()
            return c

        def tail_wait(b, c):
            block_copy(b).wait()
            return c

        first_unused = seg_ref[0, seg_ref.shape[1] - 1] // bm
        n_blocks = buf_ref.shape[0] // blk
        lax.fori_loop(first_unused, n_blocks, tail_start, 0)
        lax.fori_loop(first_unused, n_blocks, tail_wait, 0)

    pl.when(pl.program_id(0) == 0)(zero_fill)

    i = pl.program_id(0)
    last = pl.num_programs(0) - 1
    rows = tp * s

    def tile_in(t, go):
        lo = 0
        for x_ref, nt in zip(x_refs, tiles):
            @pl.when((t >= lo) & (t < lo + nt))
            def _(x_ref=x_ref, lo=lo):
                cp = pltpu.make_async_copy(x_ref.at[pl.ds(pl.multiple_of((t - lo) * rows, rows), rows)],
                                           xin_s.at[t % DISPATCH_SLOTS], isem.at[t % DISPATCH_SLOTS])
                cp.start() if go else cp.wait()
            lo += nt

    def rows_out_wait(t):
        for kk in range(TOP_K):
            pltpu.make_async_copy(xin_s.at[t % DISPATCH_SLOTS], buf_ref.at[pl.ds(0, rows)],
                                  sem.at[t % DISPATCH_SLOTS]).wait()

    @pl.when(i == 0)
    def _():
        tile_in(i, True)

    @pl.when(i >= DISPATCH_SLOTS - 1)
    def _():
        rows_out_wait(i - (DISPATCH_SLOTS - 1))

    @pl.when(i < last)
    def _():
        tile_in(i + 1, True)

    tile_in(i, False)
    slot = i % DISPATCH_SLOTS

    def issue(r0, c):
        for u in range(ISSUE_UNROLL):
            r = r0 * ISSUE_UNROLL + u
            for kk in range(TOP_K):
                pltpu.make_async_copy(xin_s.at[slot, _token_tile(r, s)],
                                      buf_ref.at[_token_tile(dst_ref[0, kk, r], s)], sem.at[slot]
                                      ).start(priority=kk % 2)
        return c

    lax.fori_loop(0, tp // ISSUE_UNROLL, issue, 0)

    @pl.when(i == last)
    def _():
        for back in range(DISPATCH_SLOTS - 2, -1, -1):
            @pl.when(i >= back)
            def _():
                rows_out_wait(i - back)


def _dispatch(seg, dest3, xns, n_slots, *, tp, s, bm):
    assert tp % ISSUE_UNROLL == 0 and all(x.shape[0] % (tp * s) == 0 for x in xns)
    tiles = tuple(x.shape[0] // (tp * s) for x in xns)
    grid_spec = pltpu.PrefetchScalarGridSpec(
        num_scalar_prefetch=1,
        grid=(sum(tiles),),
        in_specs=[pl.BlockSpec((1, TOP_K, tp), lambda i, sg: (i, 0, 0), memory_space=pltpu.SMEM)]
        + [pl.BlockSpec(memory_space=pl.ANY)] * len(xns),
        out_specs=pl.BlockSpec(memory_space=pl.ANY),
        scratch_shapes=[pltpu.VMEM((bm * s, LANES), F32), pltpu.VMEM((DISPATCH_SLOTS, tp * s, LANES), F32),
                        pltpu.SemaphoreType.DMA((DISPATCH_SLOTS,)), pltpu.SemaphoreType.DMA((DISPATCH_SLOTS,)),
                        pltpu.SemaphoreType.DMA(())],
    )
    return pl.pallas_call(
        functools.partial(_dispatch_kernel, s=s, bm=bm, tiles=tiles),
        grid_spec=grid_spec,
        out_shape=jax.ShapeDtypeStruct((n_slots * s, LANES), F32),
        compiler_params=_params("arbitrary"),
        name="dispatch",
    )(seg, dest3, *xns)


def _expert_kernel(be_ref, nu_ref, x_ref, wg_ref, wu_ref, wd_ref, o_ref, wg_s, wu_s, wd_s):
    b = pl.program_id(0)
    s = wg_s.shape[0] // LANES
    prev = be_ref[jnp.maximum(b - 1, 0)]

    @pl.when((b == 0) | (be_ref[b] != prev))
    def _():
        wg_s[...] = wg_ref[0, 0].astype(BF16)
        wu_s[...] = wu_ref[0, 0].astype(BF16)
        wd_s[...] = wd_ref[0, 0].astype(BF16)

    @pl.when(b < nu_ref[0])
    def _():
        xb = _load_token_tiles(x_ref, x_ref.shape[0] // s, s).astype(BF16)
        gate = _dot(xb, wg_s[...])
        h = gate * _sigmoid(gate) * _dot(xb, wu_s[...])
        _store_token_tiles(o_ref, _dot(h.astype(BF16), wd_s[...]))

    @pl.when(b >= nu_ref[0])
    def _():
        o_ref[...] = jnp.zeros(o_ref.shape, F32)


def _experts(blk_expert, n_used, buf, w_gate, w_up, w_down, *, layer, bm):
    d, de = w_gate.shape[2:]
    s = d // LANES
    n_blocks = buf.shape[0] // (bm * s)
    rows = pl.BlockSpec((bm * s, LANES), lambda b, be, nu: (b, 0))
    used_rows = pl.BlockSpec((bm * s, LANES), lambda b, be, nu: (jnp.minimum(b, nu[0] - 1), 0))
    grid_spec = pltpu.PrefetchScalarGridSpec(
        num_scalar_prefetch=2,
        grid=(n_blocks,),
        in_specs=[used_rows,
                  pl.BlockSpec((1, 1, d, de), lambda b, be, nu: (layer, be[b], 0, 0)),
                  pl.BlockSpec((1, 1, d, de), lambda b, be, nu: (layer, be[b], 0, 0)),
                  pl.BlockSpec((1, 1, de, d), lambda b, be, nu: (layer, be[b], 0, 0))],
        out_specs=rows,
        scratch_shapes=[pltpu.VMEM((d, de), BF16), pltpu.VMEM((d, de), BF16), pltpu.VMEM((de, d), BF16)],
    )
    return pl.pallas_call(
        _expert_kernel,
        grid_spec=grid_spec,
        out_shape=jax.ShapeDtypeStruct(buf.shape, F32),
        compiler_params=_params("arbitrary"),
        name="experts",
    )(blk_expert, n_used, buf, w_gate, w_up, w_down)


def _combine_kernel(tab_ref, nxt_ref, pos_ref, gate_ref, x_ref, yb_ref, o_ref, stg_s, tt_s, sem, *, ahead):
    i = pl.program_id(0)
    tq, d = x_ref.shape
    s = d // LANES
    piece = RUN_CHUNK * s
    slot = i % 2

    def run_copies(ref, dst_slot, go):
        for e in range(ref.shape[2]):
            def body(j, c):
                src = pl.multiple_of((ref[0, 0, e] + j * RUN_CHUNK) * s, s)
                dst = pl.multiple_of((ref[0, 2, e] + j * RUN_CHUNK) * s, piece)
                cp = pltpu.make_async_copy(yb_ref.at[pl.ds(src, piece)],
                                           stg_s.at[dst_slot, pl.ds(dst, piece)], sem.at[dst_slot])
                cp.start() if go else cp.wait()
                return c
            lax.fori_loop(0, ref[0, 1, e], body, 0)

    if ahead:
        @pl.when(i == 0)
        def _():
            run_copies(tab_ref, 0, True)

        @pl.when(i + 1 < pl.num_programs(0))
        def _():
            run_copies(nxt_ref, 1 - slot, True)
    else:
        run_copies(tab_ref, slot, True)

    run_copies(tab_ref, slot, False)

    def assemble(r0, c):
        for u in range(ISSUE_UNROLL):
            r = r0 * ISSUE_UNROLL + u
            acc = gate_ref[0, 0, r] * stg_s[slot, _token_tile(pos_ref[0, 0, r], s), :]
            for kk in range(1, TOP_K):
                acc = acc + gate_ref[0, kk, r] * stg_s[slot, _token_tile(pos_ref[0, kk, r], s), :]
            tt_s[_token_tile(r, s), :] = acc
        return c

    lax.fori_loop(0, tq // ISSUE_UNROLL, assemble, 0)
    o_ref[...] = x_ref[...] + _load_token_tiles(tt_s, tq, s)


def _combine(tab, pos3, gates3, x1, yb, *, tq):
    n, d = x1.shape
    s = d // LANES
    nt = n // tq
    ne = tab.shape[2]
    assert tq % ISSUE_UNROLL == 0
    stage_rows = tq * TOP_K + ne * RUN_CHUNK
    smem = lambda shape, imap: pl.BlockSpec(shape, imap, memory_space=pltpu.SMEM)
    return pl.pallas_call(
        functools.partial(_combine_kernel, ahead=nt > 2),
        grid=(nt,),
        in_specs=[smem((1, 3, ne), lambda i: (i, 0, 0)),
                  smem((1, 3, ne), lambda i: (jnp.minimum(i + 1, nt - 1), 0, 0)),
                  smem((1, TOP_K, tq), lambda i: (i, 0, 0)),
                  smem((1, TOP_K, tq), lambda i: (i, 0, 0)),
                  pl.BlockSpec((tq, d), lambda i: (i, 0)),
                  pl.BlockSpec(memory_space=pl.ANY)],
        out_specs=pl.BlockSpec((tq, d), lambda i: (i, 0)),
        out_shape=jax.ShapeDtypeStruct((n, d), F32),
        scratch_shapes=[pltpu.VMEM((2, stage_rows * s, LANES), F32), pltpu.VMEM((tq * s, LANES), F32),
                        pltpu.SemaphoreType.DMA((2,))],
        compiler_params=_params("arbitrary"),
        name="combine",
    )(tab, tab, pos3, gates3, x1, yb)


def _moe(streams, w_gate, w_up, w_down, *, layer, groups, epg):
    d = streams[0][0].shape[1]
    ne = groups * epg
    routed = []
    taken = jnp.zeros((ne, LANES), F32)
    for x1, _, lgt in streams:
        rtile = min(512, x1.shape[0])
        routed.append(_route(lgt, taken, groups=groups, epg=epg, tr=rtile) + (rtile,))
        taken = routed[-1][4]
    n_rows = sum(x1.shape[0] for x1, _, _ in streams) * TOP_K
    counts = taken[:, 0].astype(I32)
    bm = MOE_BLOCK if n_rows >= 2 * ne * MOE_BLOCK else MOE_BLOCK_SMALL
    padded = (counts + RUN_CHUNK + bm - 1) // bm * bm
    pend = jnp.cumsum(padded)
    pstart = pend - padded
    n_blocks = -(-(n_rows + ne * RUN_CHUNK) // bm) + ne
    eids = jnp.arange(ne, dtype=I32)
    blk_row = jnp.arange(n_blocks, dtype=I32)[:, None] * bm
    blk_expert = jnp.minimum(jnp.sum((pend[None, :] <= blk_row).astype(I32), axis=1), ne - 1)
    n_used = (pend[-1:] // bm).astype(I32)
    seg_tab = jnp.stack([pend, padded, counts]).astype(I32)
    dtile = min(r[-1] for r in routed)
    dests = []
    for (x1, _, _), (e8, g8, r8, p8, cnt, base, tcnt, rtile) in zip(streams, routed):
        seg = jnp.sum(jnp.where(e8[:TOP_K, :, None] == eids, pstart, 0), axis=-1)
        dest = seg + r8[:TOP_K]
        dests.append(dest.reshape(TOP_K, x1.shape[0] // dtile, dtile).transpose(1, 0, 2))
    buf = _dispatch(seg_tab, jnp.concatenate(dests, axis=0), [xn for _, xn, _ in streams], n_blocks * bm,
                    tp=dtile, s=d // LANES, bm=bm)
    yb = _experts(blk_expert, n_used, buf, w_gate, w_up, w_down, layer=layer, bm=bm)
    outs = []
    for (x1, _, _), (e8, g8, r8, p8, cnt, base, tcnt, rtile) in zip(streams, routed):
        nrt = x1.shape[0] // rtile
        run_start = pstart[None, :] + base.reshape(nrt, ne, LANES)[:, :, 0].astype(I32)
        pieces = (tcnt.reshape(nrt, ne, LANES)[:, :, 0].astype(I32) + RUN_CHUNK - 1) // RUN_CHUNK
        stage = (jnp.cumsum(pieces, axis=1) - pieces) * RUN_CHUNK
        tab = jnp.stack([run_start, pieces, stage], axis=1)
        by_tile = lambda a: a[:TOP_K].reshape(TOP_K, nrt, rtile).transpose(1, 0, 2)
        outs.append(_combine(tab, by_tile(p8), by_tile(g8), x1, yb, tq=rtile))
    return outs


def _router_weights(wg, bg, we, be):
    d, groups = wg.shape
    epg = we.shape[2]
    assert groups <= SUBLANES and epg == SUBLANES
    pad = jnp.zeros((SUBLANES - groups, d), F32)
    wr = jnp.concatenate([wg.T, pad, we.transpose(0, 2, 1).reshape(groups * epg, d)], axis=0)
    rb = jnp.concatenate([bg, jnp.zeros((SUBLANES - groups,), F32), be.reshape(-1)])[:, None]
    hi = wr.astype(BF16)
    lo = (wr - hi.astype(F32)).astype(BF16)
    return hi, lo, rb, groups, epg


def kernel(x_prompt, x_sample, cache_attn_k, cache_attn_v, state_hgrn, rel_bias, norm_mix, norm_ffn,
           w_in_even, w_out_even, q_norm_gain, k_norm_gain, lam_q1, lam_k1, lam_q2, lam_k2, da_out_gain,
           hgrn_lb_logits, hgrn_out_gain, w_in_odd, sgu_v_gain, sgu_w, sgu_b, w_out_odd,
           router_group_w, router_group_b, router_expert_w, router_expert_b,
           expert_w_gate, expert_w_up, expert_w_down):
    bp, tp, d = x_prompt.shape
    bs, ts, _ = x_sample.shape
    depth = norm_mix.shape[0]
    _, _, past, da_heads, _, da_dh = cache_attn_k.shape
    da_dv = cache_attn_v.shape[-1]
    _, _, hg_heads, hg_dk, hg_dv = state_hgrn.shape
    width = da_heads * da_dv
    assert width == da_heads * 2 * da_dh == hg_heads * hg_dk == hg_heads * hg_dv
    assert da_dv == LANES and hg_dk == LANES and hg_dv == LANES

    lb_all = jnp.cumsum(jax.nn.softmax(hgrn_lb_logits.astype(F32), axis=0), axis=0)
    gid = jnp.arange(width) // da_dh
    pm = jnp.where(gid[:, None] == gid[None, :], 1.0 / da_dh, 0.0).astype(BF16)

    xs = {"p": x_prompt.reshape(bp * tp, d), "s": x_sample.reshape(bs * ts, d)}
    dims = {"p": (bp, tp), "s": (bs, ts)}
    outs = {"p": {}, "s": {}}
    kp_l, vp_l, ks_l, vs_l, sp_l, ss_l, sgu_l = [], [], [], [], [], [], []

    for layer in range(depth):
        j = layer // 2
        wrh, wrl, rbias, groups, epg = _router_weights(
            router_group_w[layer], router_group_b[layer], router_expert_w[layer], router_expert_b[layer])
        g_mix = norm_mix[layer][None, :]
        g_ffn = norm_ffn[layer][None, :]
        if layer % 2 == 0:
            lam_init = 0.8 - 0.6 * math.exp(-0.3 * layer)
            w_in_bf = w_in_even[j].astype(BF16)
            w_out_bf = w_out_even[j].astype(BF16)
            reps = width // da_dh
            qg = jnp.tile(q_norm_gain[j], reps)[None, :]
            kg = jnp.tile(k_norm_gain[j], reps)[None, :]
            lam4 = jnp.stack([lam_q1[j], lam_k1[j], lam_q2[j], lam_k2[j]])
            sub_gain = da_out_gain[j][None, :]
            hg_gain = hgrn_out_gain[j][None, :]
            lb = lb_all[j][None, :]
            for key in ("p", "s"):
                b, t = dims[key]
                x = xs[key]
                q, k, v, qh, kb, lf, ih, gs, k5, v4 = _in_even(
                    x, g_mix, w_in_bf, pm, qg, kg, lb, width=width, q_scale=da_dh ** -0.5 * LOG2E,
                    tm=min(512, b * t), batch=b, seq=t, heads=da_heads, dh=da_dh)
                if key == "p":
                    oa = _attn_prompt(q, k, v, rel_bias, lam4, sub_gain, batch=b, seq=t, heads=da_heads,
                                      dh=da_dh, lam_init=lam_init, qt=min(256, t))
                    s0 = jnp.zeros((b, hg_heads, hg_dk, hg_dv), F32)
                    ob, s_new = _hgrn(qh, kb, lf, ih, s0, hg_gain, batch=b, seq=t, heads=hg_heads,
                                      dk=hg_dk, dv=hg_dv, tb=min(512, t))
                    kp_l.append(k5)
                    vp_l.append(v4)
                    sp_l.append(s_new)
                else:
                    ck = cache_attn_k[j].reshape(b, past, width)
                    cv = cache_attn_v[j].reshape(b * past * da_heads, da_dv)
                    oa = _attn_sample(q, k, v, ck, cv, rel_bias, lam4, sub_gain, batch=b, t=t,
                                      heads=da_heads, dh=da_dh, lam_init=lam_init)
                    ob, s_new = _hgrn(qh, kb, lf, ih, state_hgrn[j], hg_gain, batch=b, seq=t,
                                      heads=hg_heads, dk=hg_dk, dv=hg_dv, tb=t)
                    ks_l.append(k5)
                    vs_l.append(v4)
                    ss_l.append(s_new)
                outs[key] = _out_even(oa, ob, gs, x, w_out_bf, g_ffn, wrh, wrl, rbias, tm=min(512, b * t))
        else:
            w_in_bf = w_in_odd[j].astype(BF16)
            w_out_bf = w_out_odd[j].astype(BF16)
            v_gain = sgu_v_gain[j][None, :]
            for key in ("p", "s"):
                b, t = dims[key]
                l = min(SGU_CHUNK, t)
                res = _odd_mixer(xs[key], g_mix, w_in_bf, v_gain, sgu_w[j][:, :l, :l], sgu_b[j][:, :l].T,
                                 w_out_bf, g_ffn, wrh, wrl, rbias, l=l, tm=min(512, b * t), emit_v=(key == "s"))
                outs[key] = res[:3]
                if key == "s":
                    sgu_l.append(res[3].reshape(b, t, -1))
        xs["p"], xs["s"] = _moe([outs["p"], outs["s"]], expert_w_gate, expert_w_up, expert_w_down,
                                layer=layer, groups=groups, epg=epg)

    return (xs["p"].reshape(bp, tp, d), xs["s"].reshape(bs, ts, d), jnp.stack(kp_l), jnp.stack(vp_l),
            jnp.stack(ks_l), jnp.stack(vs_l), jnp.stack(sp_l), jnp.stack(ss_l), jnp.stack(sgu_l))
```

```python
import functools
import math

import jax
import jax.numpy as jnp
from jax import lax
from jax.experimental import pallas as pl
from jax.experimental.pallas import tpu as pltpu

F32 = jnp.float32
BF16 = jnp.bfloat16
I32 = jnp.int32

EPS = 1e-6
LOG2E = math.log2(math.e)
CHUNK = 64
SGU_CHUNK = 128
REL_BUCKETS = 32
REL_MAX_DIST = 128
TOP_K = 2
MOE_BLOCK = 512
MOE_BLOCK_SMALL = 128
MOE_TILE = 1024
RUN_CHUNK = 32
HG_SUB = 16

LANES = 128
SUBLANES = 8
VMEM_LIMIT = 56 * 1024 * 1024

NT_DIMS = (((1,), (1,)), ((), ()))
TN_DIMS = (((0,), (0,)), ((), ()))


def _params(*sem):
    return pltpu.CompilerParams(dimension_semantics=sem, vmem_limit_bytes=VMEM_LIMIT)


def _const_spec(shape):
    nd = len(shape)
    return pl.BlockSpec(shape, lambda *_: (0,) * nd, pipeline_mode=pl.Buffered(1))


def _sigmoid(x):
    return 1.0 / (1.0 + jnp.exp(-x))


def _rms(x, g):
    return x * lax.rsqrt(jnp.mean(x * x, axis=-1, keepdims=True) + EPS) * g


def _dot(a, b):
    return jnp.dot(a, b, preferred_element_type=F32)


def rel_bucket(rel):
    half = REL_BUCKETS // 2
    max_exact = half // 2
    ret = (rel > 0).astype(I32) * half
    n = jnp.abs(rel)
    nf = jnp.maximum(n, 1).astype(F32)
    large = max_exact + (jnp.log(nf / max_exact) / math.log(REL_MAX_DIST / max_exact)
                         * (half - max_exact)).astype(I32)
    large = jnp.minimum(large, half - 1)
    return ret + jnp.where(n < max_exact, n, large)


def _in_even_kernel(x_ref, g_ref, w_ref, pm_ref, qg_ref, kg_ref, lb_ref,
                    q_ref, k_ref, v_ref, qh_ref, kb_ref, lf_ref, ih_ref, gs_ref, k5_ref, v4_ref, *, width,
                    q_scale):
    xn = _rms(x_ref[...], g_ref[...]).astype(BF16)

    def proj(c):
        return _dot(xn, w_ref[:, c * width:(c + 1) * width])

    def group_norm(y, gain):
        ms = _dot((y * y).astype(BF16), pm_ref[...])
        return y * lax.rsqrt(ms + EPS) * gain

    q_ref[...] = group_norm(proj(0), qg_ref[...]) * q_scale
    kn = group_norm(proj(1), kg_ref[...])
    k_ref[...] = kn
    bb, ts, heads, _, dh = k5_ref.shape
    for h in range(heads):
        for c in range(2):
            piece = kn[:, (2 * h + c) * dh:(2 * h + c + 1) * dh]
            k5_ref[:, :, h, c, :] = piece.reshape(bb, ts, dh)
    vv = proj(2)
    v_ref[...] = vv
    for h in range(heads):
        v4_ref[:, :, h, :] = vv[:, h * 2 * dh:(h + 1) * 2 * dh].reshape(bb, ts, 2 * dh)
    yq = proj(3)
    qh_ref[...] = yq * _sigmoid(yq)
    zf = proj(4)
    lb = lb_ref[...]
    lf_ref[...] = jnp.log(lb + (1.0 - lb) * _sigmoid(zf))
    kb_ref[...] = (1.0 - lb) * _sigmoid(-zf)
    ih_ref[...] = proj(5)
    yg = proj(6)
    gs_ref[...] = yg * _sigmoid(yg)


def _in_even(x, g_mix, w_bf, pm, qg, kg, lb, *, width, q_scale, tm, batch, seq, heads, dh):
    n, d = x.shape
    assert n % tm == 0 and (tm % seq == 0 or seq % tm == 0)
    row = lambda i: (i, 0)
    out = jax.ShapeDtypeStruct((n, width), F32)
    bb, ts = max(tm // seq, 1), min(tm, seq)
    per = seq // ts
    k5_spec = pl.BlockSpec((bb, ts, heads, 2, dh), lambda i: (i // per, i % per, 0, 0, 0))
    v4_spec = pl.BlockSpec((bb, ts, heads, 2 * dh), lambda i: (i // per, i % per, 0, 0))
    return pl.pallas_call(
        functools.partial(_in_even_kernel, width=width, q_scale=q_scale),
        grid=(n // tm,),
        in_specs=[pl.BlockSpec((tm, d), row), _const_spec((1, d)), _const_spec(w_bf.shape),
                  _const_spec(pm.shape), _const_spec((1, width)), _const_spec((1, width)),
                  _const_spec((1, width))],
        out_specs=[pl.BlockSpec((tm, width), row)] * 8 + [k5_spec, v4_spec],
        out_shape=[out] * 8 + [jax.ShapeDtypeStruct((batch, seq, heads, 2, dh), F32),
                               jax.ShapeDtypeStruct((batch, seq, heads, 2 * dh), F32)],
        compiler_params=_params("arbitrary"),
        name="in_even",
    )(x, g_mix, w_bf, pm, qg, kg, lb)


def _bias_from_buckets(bk, rb_ref, h):
    b = jnp.zeros(bk.shape, F32)
    for u in range(REL_BUCKETS):
        b = jnp.where(bk == u, rb_ref[u, h], b)
    return jnp.where(bk < 0, -jnp.inf, b)


def _lam(lam_ref, lam_init):
    r = lam_ref[...]
    s1 = jnp.sum(r[0:1] * r[1:2], axis=1, keepdims=True)
    s2 = jnp.sum(r[2:3] * r[3:4], axis=1, keepdims=True)
    return jnp.exp(s1) - jnp.exp(s2) + lam_init


def _split_components(q, dh):
    lane = lax.broadcasted_iota(I32, q.shape, 1)
    q0 = jnp.where(lane < dh, q, 0.0)
    q1 = jnp.where(lane >= dh, q, 0.0)
    return jnp.concatenate([q0, q1], axis=0).astype(BF16)


def _attn_prompt_kernel(rb_ref, far_ref, lam_ref, bk_ref, sg_ref, q_ref, k_ref, v_ref, o_ref, bias_s, *,
                        qt, dh, lam_init):
    h = pl.program_id(0)
    t = q_ref.shape[0]
    kb = k_ref[...].astype(BF16)
    vt = v_ref[...].T.astype(BF16)
    qtr = q_ref[...].T
    sub = lax.broadcasted_iota(I32, (2 * dh, qt), 0)

    @pl.when(pl.program_id(1) == 0)
    def _():
        for d in range(2):
            b = _bias_from_buckets(bk_ref[d], rb_ref, h) * LOG2E
            bias_s[d] = jnp.concatenate([b, b], axis=1)

    bias = [bias_s[0], bias_s[1]]
    far = rb_ref[far_ref[0], h] * LOG2E
    lam = _lam(lam_ref, lam_init)
    gain = sg_ref[...] * (1.0 - lam_init)
    for i in range(t // qt):
        qi = qtr[:, i * qt:(i + 1) * qt]
        qz = jnp.concatenate([jnp.where(sub < dh, qi, 0.0), jnp.where(sub >= dh, qi, 0.0)],
                             axis=1).astype(BF16)
        n = (i + 1) * qt
        parts = [(n - qt, n)]
        s = [_dot(kb[n - qt:n], qz) + bias[0]]
        shift = [0.0]
        if i >= 1:
            parts.append((n - 2 * qt, n - qt))
            s.append(_dot(kb[n - 2 * qt:n - qt], qz) + bias[1])
            shift.append(0.0)
        if i >= 2:
            parts.append((0, n - 2 * qt))
            s.append(_dot(kb[:n - 2 * qt], qz))
            shift.append(far)
        m = functools.reduce(jnp.maximum, [jnp.max(x, axis=0, keepdims=True) + c for x, c in zip(s, shift)])
        p = [jnp.exp2(x - (m - c)) for x, c in zip(s, shift)]
        l = functools.reduce(lambda a, b: a + b, [jnp.sum(x, axis=0, keepdims=True) for x in p])
        acc = functools.reduce(lambda a, b: a + b,
                               [_dot(vt[:, lo:hi], x.astype(BF16)) for (lo, hi), x in zip(parts, p)])
        o = acc / l
        out = o[:, :qt] - lam * o[:, qt:]
        out = out * lax.rsqrt(jnp.mean(out * out, axis=0, keepdims=True) + EPS) * gain
        o_ref[i * qt:(i + 1) * qt, :] = out.T


def _attn_prompt(q, k, v, rel_bias, lam4, sub_gain, *, batch, seq, heads, dh, lam_init, qt):
    n, w = q.shape
    dv = w // heads
    assert dv == 2 * dh and seq % qt == 0 and qt % CHUNK == 0
    kj = jnp.arange(qt, dtype=I32)[:, None]
    qi = jnp.arange(qt, dtype=I32)[None, :]
    bk0 = jnp.where((kj // CHUNK) <= (qi // CHUNK), rel_bucket(kj - qi), -1)
    bk1 = rel_bucket(kj - qi - qt)
    bk = jnp.stack([bk0, bk1]).astype(I32)
    assert qt + 1 >= REL_MAX_DIST
    far = rel_bucket(jnp.full((1,), -(qt + 1), I32))
    smem = pl.BlockSpec(memory_space=pltpu.SMEM)
    seq_blk = pl.BlockSpec((seq, dv), lambda h, b: (b, h))
    return pl.pallas_call(
        functools.partial(_attn_prompt_kernel, qt=qt, dh=dh, lam_init=lam_init),
        grid=(heads, batch),
        in_specs=[smem, smem, _const_spec(lam4.shape), _const_spec(bk.shape), _const_spec((dv, 1)),
                  seq_blk, seq_blk, seq_blk],
        out_specs=seq_blk,
        out_shape=jax.ShapeDtypeStruct((n, w), F32),
        scratch_shapes=[pltpu.VMEM((2, qt, 2 * qt), F32)],
        compiler_params=_params("arbitrary", "arbitrary"),
        name="attn_prompt",
    )(rel_bias, far, lam4, bk, sub_gain.T, q, k, v)


def _attn_sample_kernel(rb_ref, lam_ref, bkc_ref, bkn_ref, sg_ref, q_ref, kc_ref, vc_ref, kn_ref, vn_ref,
                        o_ref, bc_s, bn_s, *, t, dh, lam_init, heads):
    h = pl.program_id(1)
    qz = _split_components(q_ref[...], dh)

    @pl.when((pl.program_id(0) == 0) & (h == 0))
    def _():
        for hh in range(heads):
            bc = _bias_from_buckets(bkc_ref[...], rb_ref, hh) * LOG2E
            bn = _bias_from_buckets(bkn_ref[...], rb_ref, hh) * LOG2E
            bc_s[hh] = jnp.concatenate([bc, bc], axis=0)
            bn_s[hh] = jnp.concatenate([bn, bn], axis=0)

    sc = lax.dot_general(qz, kc_ref[0].astype(BF16), NT_DIMS, preferred_element_type=F32)
    sn = lax.dot_general(qz, kn_ref[...].astype(BF16), NT_DIMS, preferred_element_type=F32)
    sc = sc + bc_s[h]
    sn = sn + bn_s[h]
    m = jnp.maximum(jnp.max(sc, axis=1, keepdims=True), jnp.max(sn, axis=1, keepdims=True))
    pc = jnp.exp2(sc - m)
    pn = jnp.exp2(sn - m)
    l = jnp.sum(pc, axis=1, keepdims=True) + jnp.sum(pn, axis=1, keepdims=True)
    past = vc_ref.shape[0] // heads
    vc = vc_ref[pl.ds(h, past, stride=heads), :]
    acc = _dot(pc.astype(BF16), vc.astype(BF16)) + _dot(pn.astype(BF16), vn_ref[...].astype(BF16))
    o = acc / l
    out = o[:t] - _lam(lam_ref, lam_init) * o[t:]
    o_ref[...] = _rms(out, sg_ref[...]) * (1.0 - lam_init)


def _attn_sample(q, k_new, v_new, cache_k, cache_v, rel_bias, lam4, sub_gain, *, batch, t, heads, dh,
                 lam_init):
    n, w = q.shape
    dv = w // heads
    past = cache_k.shape[1]
    assert past % CHUNK == 0 and t <= CHUNK
    qpos = past + jnp.arange(t, dtype=I32)[:, None]
    bkc = rel_bucket(jnp.arange(past, dtype=I32)[None, :] - qpos).astype(I32)
    bkn = rel_bucket(past + jnp.arange(t, dtype=I32)[None, :] - qpos).astype(I32)
    smem = pl.BlockSpec(memory_space=pltpu.SMEM)
    new = pl.BlockSpec((t, dv), lambda b, h: (b, h))
    old_k = pl.BlockSpec((1, past, dv), lambda b, h: (b, 0, h))
    old_v = pl.BlockSpec((past * heads, dv), lambda b, h: (b, 0))
    return pl.pallas_call(
        functools.partial(_attn_sample_kernel, t=t, dh=dh, lam_init=lam_init, heads=heads),
        grid=(batch, heads),
        in_specs=[smem, _const_spec(lam4.shape), _const_spec(bkc.shape), _const_spec(bkn.shape),
                  _const_spec((1, dv)), new, old_k, old_v, new, new],
        out_specs=new,
        out_shape=jax.ShapeDtypeStruct((n, w), F32),
        scratch_shapes=[pltpu.VMEM((heads, 2 * t, past), F32), pltpu.VMEM((heads, 2 * t, t), F32)],
        compiler_params=_params("arbitrary", "arbitrary"),
        name="attn_sample",
    )(rel_bias, lam4, bkc, bkn, sub_gain, q, cache_k, cache_v, k_new, v_new)


def _cumsum_rows(x):
    c = x.shape[0]
    row = lax.broadcasted_iota(I32, x.shape, 0)
    s = 1
    while s < c:
        x = x + jnp.where(row >= s, pltpu.roll(x, s, axis=0), 0.0)
        s *= 2
    return x


def _hgrn_kernel(qh_ref, kb_ref, lf_ref, ih_ref, s0_ref, hg_ref, ob_ref, sf_ref, st_s, *,
                 heads, dk, dv, c, nsb):
    t = pl.program_id(1)

    @pl.when(t == 0)
    def _():
        for h in range(heads):
            st_s[h] = s0_ref[0, h].T

    tb = qh_ref.shape[0]
    row = lax.broadcasted_iota(I32, (c, c), 0)
    col = lax.broadcasted_iota(I32, (c, c), 1)
    causal = col <= row

    def chunk(ci, carry):
        r0 = pl.multiple_of(ci * c, c)
        for h in range(heads):
            rows = pl.ds(r0, c)
            q = qh_ref[rows, h * dk:(h + 1) * dk]
            k = kb_ref[rows, h * dk:(h + 1) * dk]
            v = ih_ref[rows, h * dv:(h + 1) * dv]
            b = _cumsum_rows(lf_ref[rows, h * dk:(h + 1) * dk])
            bl = b[c - 1:c]
            st = st_s[h]
            inter = lax.dot_general((q * jnp.exp(b)).astype(BF16), st.astype(BF16), NT_DIMS,
                                    preferred_element_type=F32)
            qs, ks = [], []
            for j in range(nsb):
                ref = b[j * HG_SUB + HG_SUB // 2:j * HG_SUB + HG_SUB // 2 + 1]
                qs.append(q * jnp.exp(b - ref))
                sub = slice(j * HG_SUB, (j + 1) * HG_SUB)
                ks.append(k[sub] * jnp.exp(ref - b[sub]))
            a_full = lax.dot_general(jnp.concatenate(qs, axis=0).astype(BF16),
                                     jnp.concatenate(ks, axis=0).astype(BF16), NT_DIMS,
                                     preferred_element_type=F32)
            att = jnp.zeros((c, c), F32)
            for j in range(nsb):
                att = jnp.where(col >= j * HG_SUB, a_full[j * c:(j + 1) * c], att)
            att = jnp.where(causal, att, 0.0)
            out = inter + _dot(att.astype(BF16), v.astype(BF16))
            ob_ref[rows, h * dv:(h + 1) * dv] = _rms(out, hg_ref[...])
            kdec = (k * jnp.exp(bl - b)).astype(BF16)
            st_s[h] = jnp.exp(bl) * st + lax.dot_general(v.astype(BF16), kdec, TN_DIMS,
                                                         preferred_element_type=F32)
        return carry

    lax.fori_loop(0, tb // c, chunk, 0, unroll=min(4, tb // c))

    @pl.when(t == pl.num_programs(1) - 1)
    def _():
        for h in range(heads):
            sf_ref[0, h] = st_s[h].T


def _hgrn(qh, kb, lf, ih, s0, hg_gain, *, batch, seq, heads, dk, dv, tb):
    n = qh.shape[0]
    c = min(CHUNK, seq)
    assert seq % tb == 0 and tb % c == 0 and c % HG_SUB == 0
    nt = seq // tb
    blk = lambda w: pl.BlockSpec((tb, w), lambda b, t: (b * nt + t, 0))
    st = pl.BlockSpec((1, heads, dk, dv), lambda b, t: (b, 0, 0, 0))
    return pl.pallas_call(
        functools.partial(_hgrn_kernel, heads=heads, dk=dk, dv=dv, c=c, nsb=c // HG_SUB),
        grid=(batch, nt),
        in_specs=[blk(heads * dk), blk(heads * dk), blk(heads * dk), blk(heads * dv), st,
                  _const_spec((1, dv))],
        out_specs=[blk(heads * dv), st],
        out_shape=[jax.ShapeDtypeStruct((n, heads * dv), F32),
                   jax.ShapeDtypeStruct((batch, heads, dk, dv), F32)],
        scratch_shapes=[pltpu.VMEM((heads, dv, dk), F32)],
        compiler_params=_params("arbitrary", "arbitrary"),
        name="hgrn",
    )(qh, kb, lf, ih, s0, hg_gain)


def _store_token_tiles(ref, x):
    rows, d = x.shape
    s = d // LANES
    for c in range(s):
        ref[pl.ds(c, rows, stride=s), :] = x[:, c * LANES:(c + 1) * LANES]


def _load_token_tiles(ref, rows, s):
    return jnp.concatenate([ref[pl.ds(c, rows, stride=s), :] for c in range(s)], axis=1)


def _token_tile(r, s):
    return pl.ds(pl.multiple_of(r * s, s), s)


def _ffn_prologue(x1, gf_ref, wrh_ref, wrl_ref, rbias_ref, x1_ref, xn_ref, lg_ref):
    x1_ref[...] = x1
    xn = _rms(x1, gf_ref[...])
    _store_token_tiles(xn_ref, xn)
    hi = xn.astype(BF16)
    lo = (xn - hi.astype(F32)).astype(BF16)
    nt = functools.partial(lax.dot_general, dimension_numbers=NT_DIMS, preferred_element_type=F32)
    lg_ref[...] = nt(wrh_ref[...], hi) + nt(wrh_ref[...], lo) + nt(wrl_ref[...], hi) + rbias_ref[...]


def _out_even_kernel(oa_ref, ob_ref, gs_ref, x_ref, w_ref, gf_ref, wrh_ref, wrl_ref, rbias_ref,
                     x1_ref, xn_ref, lg_ref):
    o = jnp.concatenate([oa_ref[...], ob_ref[...] * gs_ref[...]], axis=1).astype(BF16)
    x1 = x_ref[...] + _dot(o, w_ref[...])
    _ffn_prologue(x1, gf_ref, wrh_ref, wrl_ref, rbias_ref, x1_ref, xn_ref, lg_ref)


def _out_even(oa, ob, gs, x, w_bf, g_ffn, wrh, wrl, rbias, *, tm):
    n, d = x.shape
    w = oa.shape[1]
    nr = wrh.shape[0]
    row = lambda i: (i, 0)
    return pl.pallas_call(
        _out_even_kernel,
        grid=(n // tm,),
        in_specs=[pl.BlockSpec((tm, w), row), pl.BlockSpec((tm, w), row), pl.BlockSpec((tm, w), row),
                  pl.BlockSpec((tm, d), row), _const_spec(w_bf.shape), _const_spec((1, d)),
                  _const_spec(wrh.shape), _const_spec(wrl.shape), _const_spec(rbias.shape)],
        out_specs=[pl.BlockSpec((tm, d), row), pl.BlockSpec((tm * (d // LANES), LANES), row),
                   pl.BlockSpec((nr, tm), lambda i: (0, i))],
        out_shape=[jax.ShapeDtypeStruct((n, d), F32), jax.ShapeDtypeStruct((n * (d // LANES), LANES), F32),
                   jax.ShapeDtypeStruct((nr, n), F32)],
        compiler_params=_params("arbitrary"),
        name="out_even",
    )(oa, ob, gs, x, w_bf, g_ffn, wrh, wrl, rbias)


def _gelu(x):
    return 0.5 * x * (1.0 + jnp.tanh(math.sqrt(2.0 / math.pi) * (x + 0.044715 * (x * x * x))))


def _odd_kernel(x_ref, gm_ref, win_ref, vg_ref, wsp_ref, bsp_ref, wout_ref, gf_ref, wrh_ref, wrl_ref,
                rbias_ref, x1_ref, xn_ref, lg_ref, *rest, half, groups, l, emit_v):
    if emit_v:
        vn_ref, u_s, s_s = rest
    else:
        vn_ref = None
        u_s, vn_s, s_s = rest
    tm = x_ref.shape[0]
    x = x_ref[...]
    xn = _rms(x, gm_ref[...]).astype(BF16)
    cw = 512
    vbuf = vn_ref if emit_v else vn_s
    for cidx in range(half // cw):
        u_s[:, cidx * cw:(cidx + 1) * cw] = _gelu(_dot(xn, win_ref[:, cidx * cw:(cidx + 1) * cw]))
        vbuf[:, cidx * cw:(cidx + 1) * cw] = _gelu(
            _dot(xn, win_ref[:, half + cidx * cw:half + (cidx + 1) * cw]))
    vbuf[...] = _rms(vbuf[...], vg_ref[...])
    gw = half // groups
    row = lax.broadcasted_iota(I32, (l, l), 0)
    col = lax.broadcasted_iota(I32, (l, l), 1)
    for g in range(groups):
        wg = jnp.where(col <= row, wsp_ref[g], 0.0).astype(BF16)
        bg = bsp_ref[:, g:g + 1]
        for ci in range(tm // l):
            vv = vbuf[ci * l:(ci + 1) * l, g * gw:(g + 1) * gw].astype(BF16)
            s_s[ci * l:(ci + 1) * l, g * gw:(g + 1) * gw] = _dot(wg, vv) + bg
    y = _dot((u_s[...] * s_s[...]).astype(BF16), wout_ref[...])
    _ffn_prologue(x + y, gf_ref, wrh_ref, wrl_ref, rbias_ref, x1_ref, xn_ref, lg_ref)


def _odd_mixer(x, g_mix, win_bf, v_gain, wsp, bsp_t, wout_bf, g_ffn, wrh, wrl, rbias, *, l, tm, emit_v):
    n, d = x.shape
    half = wout_bf.shape[0]
    groups = wsp.shape[0]
    nr = wrh.shape[0]
    assert n % tm == 0 and tm % l == 0
    row = lambda i: (i, 0)
    out_specs = [pl.BlockSpec((tm, d), row), pl.BlockSpec((tm * (d // LANES), LANES), row),
                 pl.BlockSpec((nr, tm), lambda i: (0, i))]
    out_shape = [jax.ShapeDtypeStruct((n, d), F32), jax.ShapeDtypeStruct((n * (d // LANES), LANES), F32),
                 jax.ShapeDtypeStruct((nr, n), F32)]
    scratch = [pltpu.VMEM((tm, half), F32)]
    if emit_v:
        out_specs.append(pl.BlockSpec((tm, half), row))
        out_shape.append(jax.ShapeDtypeStruct((n, half), F32))
    else:
        scratch.append(pltpu.VMEM((tm, half), F32))
    scratch.append(pltpu.VMEM((tm, half), F32))
    return pl.pallas_call(
        functools.partial(_odd_kernel, half=half, groups=groups, l=l, emit_v=emit_v),
        grid=(n // tm,),
        in_specs=[pl.BlockSpec((tm, d), row), _const_spec((1, d)), _const_spec(win_bf.shape),
                  _const_spec((1, half)), _const_spec(wsp.shape), _const_spec(bsp_t.shape),
                  _const_spec(wout_bf.shape), _const_spec((1, d)), _const_spec(wrh.shape),
                  _const_spec(wrl.shape), _const_spec(rbias.shape)],
        out_specs=out_specs,
        out_shape=out_shape,
        scratch_shapes=scratch,
        compiler_params=_params("arbitrary"),
        name="odd_mixer",
    )(x, g_mix, win_bf, v_gain, wsp, bsp_t, wout_bf, g_ffn, wrh, wrl, rbias)


def _route_kernel(lg_ref, tri_ref, init_ref, e_ref, g_ref, r_ref, p_ref, cnt_ref, base_ref, tcnt_ref, run_s,
                  *, groups, epg):
    i = pl.program_id(0)

    @pl.when(i == 0)
    def _():
        run_s[...] = init_ref[...]

    lg = lg_ref[...]
    tr = lg.shape[1]
    gl = [lg[g:g + 1] for g in range(groups)]
    m = functools.reduce(jnp.maximum, gl)
    grp = jnp.full((1, tr), groups - 1, I32)
    for g in range(groups - 2, -1, -1):
        grp = jnp.where(gl[g] == m, g, grp)
    gate_g = 1.0 / functools.reduce(lambda a, b: a + b, [jnp.exp(x - m) for x in gl])
    sel = lg[SUBLANES + (groups - 1) * epg:SUBLANES + groups * epg]
    for g in range(groups - 2, -1, -1):
        sel = jnp.where(grp == g, lg[SUBLANES + g * epg:SUBLANES + (g + 1) * epg], sel)
    sub = lax.broadcasted_iota(I32, sel.shape, 0)
    v1 = jnp.max(sel, axis=0, keepdims=True)
    i1 = jnp.min(jnp.where(sel == v1, sub, epg), axis=0, keepdims=True)
    sel2 = jnp.where(sub == i1, -jnp.inf, sel)
    v2 = jnp.max(sel2, axis=0, keepdims=True)
    i2 = jnp.min(jnp.where(sel2 == v2, sub, epg), axis=0, keepdims=True)
    tt = jnp.exp(v2 - v1)
    g1 = gate_g / (1.0 + tt)
    g2 = gate_g * tt / (1.0 + tt)
    e1 = grp * epg + i1
    e2 = grp * epg + i2
    ne = groups * epg
    eidx = lax.broadcasted_iota(I32, (ne, tr), 0)
    oh1 = eidx == e1
    oh2 = eidx == e2
    cnt = jnp.where(oh1, 1.0, 0.0) + jnp.where(oh2, 1.0, 0.0)
    local = _dot(cnt.astype(BF16), tri_ref[...])
    before = run_s[:, 0:1] + local
    r1 = jnp.sum(jnp.where(oh1, before, 0.0), axis=0, keepdims=True)
    r2 = jnp.sum(jnp.where(oh2, before, 0.0), axis=0, keepdims=True)
    tile_cnt = jnp.broadcast_to(jnp.sum(cnt, axis=1, keepdims=True), run_s.shape)
    padded = jnp.ceil(tile_cnt * (1.0 / RUN_CHUNK)) * RUN_CHUNK
    offset = _cumsum_rows(padded) - padded
    where_local = offset[:, 0:1] + local
    p1 = jnp.sum(jnp.where(oh1, where_local, 0.0), axis=0, keepdims=True)
    p2 = jnp.sum(jnp.where(oh2, where_local, 0.0), axis=0, keepdims=True)
    base_ref[...] = run_s[...]
    tcnt_ref[...] = tile_cnt
    run_s[...] = run_s[...] + tile_cnt
    rows = lax.broadcasted_iota(I32, (SUBLANES, tr), 0)
    e_ref[...] = jnp.where(rows == 0, e1, jnp.where(rows == 1, e2, 0))
    g_ref[...] = jnp.where(rows == 0, g1, jnp.where(rows == 1, g2, 0.0))
    r_ref[...] = jnp.where(rows == 0, r1, jnp.where(rows == 1, r2, 0.0)).astype(I32)
    p_ref[...] = jnp.where(rows == 0, p1, jnp.where(rows == 1, p2, 0.0)).astype(I32)
    cnt_ref[...] = run_s[...]


def _route(lgt, init, *, groups, epg, tr):
    nr, n = lgt.shape
    assert n % tr == 0 and nr == SUBLANES + groups * epg
    ne = groups * epg
    tri = (jnp.arange(tr)[:, None] < jnp.arange(tr)[None, :]).astype(BF16)
    tok = pl.BlockSpec((SUBLANES, tr), lambda i: (0, i))
    per_tile = pl.BlockSpec((ne, LANES), lambda i: (i, 0))
    tile_tab = jax.ShapeDtypeStruct((n // tr * ne, LANES), F32)
    return pl.pallas_call(
        functools.partial(_route_kernel, groups=groups, epg=epg),
        grid=(n // tr,),
        in_specs=[pl.BlockSpec((nr, tr), lambda i: (0, i)), _const_spec((tr, tr)), _const_spec((ne, LANES))],
        out_specs=[tok, tok, tok, tok, pl.BlockSpec((ne, LANES), lambda i: (0, 0)), per_tile, per_tile],
        out_shape=[jax.ShapeDtypeStruct((SUBLANES, n), I32), jax.ShapeDtypeStruct((SUBLANES, n), F32),
                   jax.ShapeDtypeStruct((SUBLANES, n), I32), jax.ShapeDtypeStruct((SUBLANES, n), I32),
                   jax.ShapeDtypeStruct((ne, LANES), F32), tile_tab, tile_tab],
        scratch_shapes=[pltpu.VMEM((ne, LANES), F32)],
        compiler_params=_params("arbitrary"),
        name="route",
    )(lgt, tri, init)


ISSUE_UNROLL = 16
DISPATCH_SLOTS = 3


def _dispatch_kernel(seg_ref, dst_ref, *rest, s, bm, tiles):
    x_refs = rest[:len(tiles)]
    buf_ref, zero_s, xin_s, sem, isem, zsem = rest[len(tiles):]
    tp = dst_ref.shape[2]
    blk = bm * s

    def zero_fill():
        zero_s[...] = jnp.zeros(zero_s.shape, F32)

        def block_copy(b):
            return pltpu.make_async_copy(zero_s, buf_ref.at[pl.ds(pl.multiple_of(b * blk, blk), blk)], zsem)

        for e in range(seg_ref.shape[1]):
            @pl.when(seg_ref[1, e] > 0)
            def _():
                block_copy(seg_ref[0, e] // bm - 1).start()

            @pl.when(seg_ref[1, e] - seg_ref[2, e] > bm)
            def _():
                block_copy(seg_ref[0, e] // bm - 2).start()
        for e in range(seg_ref.shape[1]):
            @pl.when(seg_ref[1, e] > 0)
            def _():
                block_copy(0).wait()

            @pl.when(seg_ref[1, e] - seg_ref[2, e] > bm)
            def _():
                block_copy(0).wait()

        def tail_start(b, c):
            block_copy(b).start()
            return c

        def tail_wait(b, c):
            block_copy(b).wait()
            return c

        first_unused = seg_ref[0, seg_ref.shape[1] - 1] // bm
        n_blocks = buf_ref.shape[0] // blk
        lax.fori_loop(first_unused, n_blocks, tail_start, 0)
        lax.fori_loop(first_unused, n_blocks, tail_wait, 0)

    pl.when(pl.program_id(0) == 0)(zero_fill)

    i = pl.program_id(0)
    last = pl.num_programs(0) - 1
    rows = tp * s

    def tile_in(t, go):
        lo = 0
        for x_ref, nt in zip(x_refs, tiles):
            @pl.when((t >= lo) & (t < lo + nt))
            def _(x_ref=x_ref, lo=lo):
                cp = pltpu.make_async_copy(x_ref.at[pl.ds(pl.multiple_of((t - lo) * rows, rows), rows)],
                                           xin_s.at[t % DISPATCH_SLOTS], isem.at[t % DISPATCH_SLOTS])
                cp.start() if go else cp.wait()
            lo += nt

    def rows_out_wait(t):
        for kk in range(TOP_K):
            pltpu.make_async_copy(xin_s.at[t % DISPATCH_SLOTS], buf_ref.at[pl.ds(0, rows)],
                                  sem.at[t % DISPATCH_SLOTS]).wait()

    @pl.when(i == 0)
    def _():
        tile_in(i, True)

    @pl.when(i >= DISPATCH_SLOTS - 1)
    def _():
        rows_out_wait(i - (DISPATCH_SLOTS - 1))

    @pl.when(i < last)
    def _():
        tile_in(i + 1, True)

    tile_in(i, False)
    slot = i % DISPATCH_SLOTS

    def issue(r0, c):
        for u in range(ISSUE_UNROLL):
            r = r0 * ISSUE_UNROLL + u
            for kk in range(TOP_K):
                pltpu.make_async_copy(xin_s.at[slot, _token_tile(r, s)],
                                      buf_ref.at[_token_tile(dst_ref[0, kk, r], s)], sem.at[slot]
                                      ).start(priority=kk % 2)
        return c

    lax.fori_loop(0, tp // ISSUE_UNROLL, issue, 0)

    @pl.when(i == last)
    def _():
        for back in range(DISPATCH_SLOTS - 2, -1, -1):
            @pl.when(i >= back)
            def _():
                rows_out_wait(i - back)


def _dispatch(seg, dest3, xns, n_slots, *, tp, s, bm):
    assert tp % ISSUE_UNROLL == 0 and all(x.shape[0] % (tp * s) == 0 for x in xns)
    tiles = tuple(x.shape[0] // (tp * s) for x in xns)
    grid_spec = pltpu.PrefetchScalarGridSpec(
        num_scalar_prefetch=1,
        grid=(sum(tiles),),
        in_specs=[pl.BlockSpec((1, TOP_K, tp), lambda i, sg: (i, 0, 0), memory_space=pltpu.SMEM)]
        + [pl.BlockSpec(memory_space=pl.ANY)] * len(xns),
        out_specs=pl.BlockSpec(memory_space=pl.ANY),
        scratch_shapes=[pltpu.VMEM((bm * s, LANES), F32), pltpu.VMEM((DISPATCH_SLOTS, tp * s, LANES), F32),
                        pltpu.SemaphoreType.DMA((DISPATCH_SLOTS,)), pltpu.SemaphoreType.DMA((DISPATCH_SLOTS,)),
                        pltpu.SemaphoreType.DMA(())],
    )
    return pl.pallas_call(
        functools.partial(_dispatch_kernel, s=s, bm=bm, tiles=tiles),
        grid_spec=grid_spec,
        out_shape=jax.ShapeDtypeStruct((n_slots * s, LANES), F32),
        compiler_params=_params("arbitrary"),
        name="dispatch",
    )(seg, dest3, *xns)


def _expert_kernel(be_ref, nu_ref, x_ref, wg_ref, wu_ref, wd_ref, o_ref, wg_s, wu_s, wd_s):
    b = pl.program_id(0)
    s = wg_s.shape[0] // LANES
    prev = be_ref[jnp.maximum(b - 1, 0)]

    @pl.when((b == 0) | (be_ref[b] != prev))
    def _():
        wg_s[...] = wg_ref[0, 0].astype(BF16)
        wu_s[...] = wu_ref[0, 0].astype(BF16)
        wd_s[...] = wd_ref[0, 0].astype(BF16)

    @pl.when(b < nu_ref[0])
    def _():
        xb = _load_token_tiles(x_ref, x_ref.shape[0] // s, s).astype(BF16)
        gate = _dot(xb, wg_s[...])
        h = gate * _sigmoid(gate) * _dot(xb, wu_s[...])
        _store_token_tiles(o_ref, _dot(h.astype(BF16), wd_s[...]))

    @pl.when(b >= nu_ref[0])
    def _():
        o_ref[...] = jnp.zeros(o_ref.shape, F32)


def _experts(blk_expert, n_used, buf, w_gate, w_up, w_down, *, layer, bm):
    d, de = w_gate.shape[2:]
    s = d // LANES
    n_blocks = buf.shape[0] // (bm * s)
    rows = pl.BlockSpec((bm * s, LANES), lambda b, be, nu: (b, 0))
    used_rows = pl.BlockSpec((bm * s, LANES), lambda b, be, nu: (jnp.minimum(b, nu[0] - 1), 0))
    grid_spec = pltpu.PrefetchScalarGridSpec(
        num_scalar_prefetch=2,
        grid=(n_blocks,),
        in_specs=[used_rows,
                  pl.BlockSpec((1, 1, d, de), lambda b, be, nu: (layer, be[b], 0, 0)),
                  pl.BlockSpec((1, 1, d, de), lambda b, be, nu: (layer, be[b], 0, 0)),
                  pl.BlockSpec((1, 1, de, d), lambda b, be, nu: (layer, be[b], 0, 0))],
        out_specs=rows,
        scratch_shapes=[pltpu.VMEM((d, de), BF16), pltpu.VMEM((d, de), BF16), pltpu.VMEM((de, d), BF16)],
    )
    return pl.pallas_call(
        _expert_kernel,
        grid_spec=grid_spec,
        out_shape=jax.ShapeDtypeStruct(buf.shape, F32),
        compiler_params=_params("arbitrary"),
        name="experts",
    )(blk_expert, n_used, buf, w_gate, w_up, w_down)


def _combine_kernel(tab_ref, nxt_ref, pos_ref, gate_ref, x_ref, yb_ref, o_ref, stg_s, tt_s, sem, *, ahead):
    i = pl.program_id(0)
    tq, d = x_ref.shape
    s = d // LANES
    piece = RUN_CHUNK * s
    slot = i % 2

    def run_copies(ref, dst_slot, go):
        for e in range(ref.shape[2]):
            def body(j, c):
                src = pl.multiple_of((ref[0, 0, e] + j * RUN_CHUNK) * s, s)
                dst = pl.multiple_of((ref[0, 2, e] + j * RUN_CHUNK) * s, piece)
                cp = pltpu.make_async_copy(yb_ref.at[pl.ds(src, piece)],
                                           stg_s.at[dst_slot, pl.ds(dst, piece)], sem.at[dst_slot])
                cp.start() if go else cp.wait()
                return c
            lax.fori_loop(0, ref[0, 1, e], body, 0)

    if ahead:
        @pl.when(i == 0)
        def _():
            run_copies(tab_ref, 0, True)

        @pl.when(i + 1 < pl.num_programs(0))
        def _():
            run_copies(nxt_ref, 1 - slot, True)
    else:
        run_copies(tab_ref, slot, True)

    run_copies(tab_ref, slot, False)

    def assemble(r0, c):
        for u in range(ISSUE_UNROLL):
            r = r0 * ISSUE_UNROLL + u
            acc = gate_ref[0, 0, r] * stg_s[slot, _token_tile(pos_ref[0, 0, r], s), :]
            for kk in range(1, TOP_K):
                acc = acc + gate_ref[0, kk, r] * stg_s[slot, _token_tile(pos_ref[0, kk, r], s), :]
            tt_s[_token_tile(r, s), :] = acc
        return c

    lax.fori_loop(0, tq // ISSUE_UNROLL, assemble, 0)
    o_ref[...] = x_ref[...] + _load_token_tiles(tt_s, tq, s)


def _combine(tab, pos3, gates3, x1, yb, *, tq):
    n, d = x1.shape
    s = d // LANES
    nt = n // tq
    ne = tab.shape[2]
    assert tq % ISSUE_UNROLL == 0
    stage_rows = tq * TOP_K + ne * RUN_CHUNK
    smem = lambda shape, imap: pl.BlockSpec(shape, imap, memory_space=pltpu.SMEM)
    return pl.pallas_call(
        functools.partial(_combine_kernel, ahead=nt > 2),
        grid=(nt,),
        in_specs=[smem((1, 3, ne), lambda i: (i, 0, 0)),
                  smem((1, 3, ne), lambda i: (jnp.minimum(i + 1, nt - 1), 0, 0)),
                  smem((1, TOP_K, tq), lambda i: (i, 0, 0)),
                  smem((1, TOP_K, tq), lambda i: (i, 0, 0)),
                  pl.BlockSpec((tq, d), lambda i: (i, 0)),
                  pl.BlockSpec(memory_space=pl.ANY)],
        out_specs=pl.BlockSpec((tq, d), lambda i: (i, 0)),
        out_shape=jax.ShapeDtypeStruct((n, d), F32),
        scratch_shapes=[pltpu.VMEM((2, stage_rows * s, LANES), F32), pltpu.VMEM((tq * s, LANES), F32),
                        pltpu.SemaphoreType.DMA((2,))],
        compiler_params=_params("arbitrary"),
        name="combine",
    )(tab, tab, pos3, gates3, x1, yb)


def _moe(streams, w_gate, w_up, w_down, *, layer, groups, epg):
    d = streams[0][0].shape[1]
    ne = groups * epg
    routed = []
    taken = jnp.zeros((ne, LANES), F32)
    for x1, _, lgt in streams:
        rtile = min(MOE_TILE, x1.shape[0])
        routed.append(_route(lgt, taken, groups=groups, epg=epg, tr=rtile) + (rtile,))
        taken = routed[-1][4]
    n_rows = sum(x1.shape[0] for x1, _, _ in streams) * TOP_K
    counts = taken[:, 0].astype(I32)
    bm = MOE_BLOCK if n_rows >= 2 * ne * MOE_BLOCK else MOE_BLOCK_SMALL
    padded = (counts + RUN_CHUNK + bm - 1) // bm * bm
    pend = jnp.cumsum(padded)
    pstart = pend - padded
    n_blocks = -(-(n_rows + ne * RUN_CHUNK) // bm) + ne
    eids = jnp.arange(ne, dtype=I32)
    blk_row = jnp.arange(n_blocks, dtype=I32)[:, None] * bm
    blk_expert = jnp.minimum(jnp.sum((pend[None, :] <= blk_row).astype(I32), axis=1), ne - 1)
    n_used = (pend[-1:] // bm).astype(I32)
    seg_tab = jnp.stack([pend, padded, counts]).astype(I32)
    dtile = min(r[-1] for r in routed)
    dests = []
    for (x1, _, _), (e8, g8, r8, p8, cnt, base, tcnt, rtile) in zip(streams, routed):
        seg = jnp.sum(jnp.where(e8[:TOP_K, :, None] == eids, pstart, 0), axis=-1)
        dest = seg + r8[:TOP_K]
        dests.append(dest.reshape(TOP_K, x1.shape[0] // dtile, dtile).transpose(1, 0, 2))
    buf = _dispatch(seg_tab, jnp.concatenate(dests, axis=0), [xn for _, xn, _ in streams], n_blocks * bm,
                    tp=dtile, s=d // LANES, bm=bm)
    yb = _experts(blk_expert, n_used, buf, w_gate, w_up, w_down, layer=layer, bm=bm)
    outs = []
    for (x1, _, _), (e8, g8, r8, p8, cnt, base, tcnt, rtile) in zip(streams, routed):
        nrt = x1.shape[0] // rtile
        run_start = pstart[None, :] + base.reshape(nrt, ne, LANES)[:, :, 0].astype(I32)
        pieces = (tcnt.reshape(nrt, ne, LANES)[:, :, 0].astype(I32) + RUN_CHUNK - 1) // RUN_CHUNK
        stage = (jnp.cumsum(pieces, axis=1) - pieces) * RUN_CHUNK
        tab = jnp.stack([run_start, pieces, stage], axis=1)
        by_tile = lambda a: a[:TOP_K].reshape(TOP_K, nrt, rtile).transpose(1, 0, 2)
        outs.append(_combine(tab, by_tile(p8), by_tile(g8), x1, yb, tq=rtile))
    return outs


def _router_weights(wg, bg, we, be):
    d, groups = wg.shape
    epg = we.shape[2]
    assert groups <= SUBLANES and epg == SUBLANES
    pad = jnp.zeros((SUBLANES - groups, d), F32)
    wr = jnp.concatenate([wg.T, pad, we.transpose(0, 2, 1).reshape(groups * epg, d)], axis=0)
    rb = jnp.concatenate([bg, jnp.zeros((SUBLANES - groups,), F32), be.reshape(-1)])[:, None]
    hi = wr.astype(BF16)
    lo = (wr - hi.astype(F32)).astype(BF16)
    return hi, lo, rb, groups, epg


def kernel(x_prompt, x_sample, cache_attn_k, cache_attn_v, state_hgrn, rel_bias, norm_mix, norm_ffn,
           w_in_even, w_out_even, q_norm_gain, k_norm_gain, lam_q1, lam_k1, lam_q2, lam_k2, da_out_gain,
           hgrn_lb_logits, hgrn_out_gain, w_in_odd, sgu_v_gain, sgu_w, sgu_b, w_out_odd,
           router_group_w, router_group_b, router_expert_w, router_expert_b,
           expert_w_gate, expert_w_up, expert_w_down):
    bp, tp, d = x_prompt.shape
    bs, ts, _ = x_sample.shape
    depth = norm_mix.shape[0]
    _, _, past, da_heads, _, da_dh = cache_attn_k.shape
    da_dv = cache_attn_v.shape[-1]
    _, _, hg_heads, hg_dk, hg_dv = state_hgrn.shape
    width = da_heads * da_dv
    assert width == da_heads * 2 * da_dh == hg_heads * hg_dk == hg_heads * hg_dv
    assert da_dv == LANES and hg_dk == LANES and hg_dv == LANES

    lb_all = jnp.cumsum(jax.nn.softmax(hgrn_lb_logits.astype(F32), axis=0), axis=0)
    gid = jnp.arange(width) // da_dh
    pm = jnp.where(gid[:, None] == gid[None, :], 1.0 / da_dh, 0.0).astype(BF16)

    xs = {"p": x_prompt.reshape(bp * tp, d), "s": x_sample.reshape(bs * ts, d)}
    dims = {"p": (bp, tp), "s": (bs, ts)}
    outs = {"p": {}, "s": {}}
    kp_l, vp_l, ks_l, vs_l, sp_l, ss_l, sgu_l = [], [], [], [], [], [], []

    for layer in range(depth):
        j = layer // 2
        wrh, wrl, rbias, groups, epg = _router_weights(
            router_group_w[layer], router_group_b[layer], router_expert_w[layer], router_expert_b[layer])
        g_mix = norm_mix[layer][None, :]
        g_ffn = norm_ffn[layer][None, :]
        if layer % 2 == 0:
            lam_init = 0.8 - 0.6 * math.exp(-0.3 * layer)
            w_in_bf = w_in_even[j].astype(BF16)
            w_out_bf = w_out_even[j].astype(BF16)
            reps = width // da_dh
            qg = jnp.tile(q_norm_gain[j], reps)[None, :]
            kg = jnp.tile(k_norm_gain[j], reps)[None, :]
            lam4 = jnp.stack([lam_q1[j], lam_k1[j], lam_q2[j], lam_k2[j]])
            sub_gain = da_out_gain[j][None, :]
            hg_gain = hgrn_out_gain[j][None, :]
            lb = lb_all[j][None, :]
            for key in ("p", "s"):
                b, t = dims[key]
                x = xs[key]
                q, k, v, qh, kb, lf, ih, gs, k5, v4 = _in_even(
                    x, g_mix, w_in_bf, pm, qg, kg, lb, width=width, q_scale=da_dh ** -0.5 * LOG2E,
                    tm=min(512, b * t), batch=b, seq=t, heads=da_heads, dh=da_dh)
                if key == "p":
                    oa = _attn_prompt(q, k, v, rel_bias, lam4, sub_gain, batch=b, seq=t, heads=da_heads,
                                      dh=da_dh, lam_init=lam_init, qt=min(256, t))
                    s0 = jnp.zeros((b, hg_heads, hg_dk, hg_dv), F32)
                    ob, s_new = _hgrn(qh, kb, lf, ih, s0, hg_gain, batch=b, seq=t, heads=hg_heads,
                                      dk=hg_dk, dv=hg_dv, tb=min(512, t))
                    kp_l.append(k5)
                    vp_l.append(v4)
                    sp_l.append(s_new)
                else:
                    ck = cache_attn_k[j].reshape(b, past, width)
                    cv = cache_attn_v[j].reshape(b * past * da_heads, da_dv)
                    oa = _attn_sample(q, k, v, ck, cv, rel_bias, lam4, sub_gain, batch=b, t=t,
                                      heads=da_heads, dh=da_dh, lam_init=lam_init)
                    ob, s_new = _hgrn(qh, kb, lf, ih, state_hgrn[j], hg_gain, batch=b, seq=t,
                                      heads=hg_heads, dk=hg_dk, dv=hg_dv, tb=t)
                    ks_l.append(k5)
                    vs_l.append(v4)
                    ss_l.append(s_new)
                outs[key] = _out_even(oa, ob, gs, x, w_out_bf, g_ffn, wrh, wrl, rbias, tm=min(512, b * t))
        else:
            w_in_bf = w_in_odd[j].astype(BF16)
            w_out_bf = w_out_odd[j].astype(BF16)
            v_gain = sgu_v_gain[j][None, :]
            for key in ("p", "s"):
                b, t = dims[key]
                l = min(SGU_CHUNK, t)
                res = _odd_mixer(xs[key], g_mix, w_in_bf, v_gain, sgu_w[j][:, :l, :l], sgu_b[j][:, :l].T,
                                 w_out_bf, g_ffn, wrh, wrl, rbias, l=l, tm=min(512, b * t), emit_v=(key == "s"))
                outs[key] = res[:3]
                if key == "s":
                    sgu_l.append(res[3].reshape(b, t, -1))
        xs["p"], xs["s"] = _moe([outs["p"], outs["s"]], expert_w_gate, expert_w_up, expert_w_down,
                                layer=layer, groups=groups, epg=epg)

    return (xs["p"].reshape(bp, tp, d), xs["s"].reshape(bs, ts, d), jnp.stack(kp_l), jnp.stack(vp_l),
            jnp.stack(ks_l), jnp.stack(vs_l), jnp.stack(sp_l), jnp.stack(ss_l), jnp.stack(sgu_l))
```

```python
import functools
import math

import jax
import jax.numpy as jnp
from jax import lax
from jax.experimental import pallas as pl
from jax.experimental.pallas import tpu as pltpu

F32 = jnp.float32
BF16 = jnp.bfloat16
I32 = jnp.int32

EPS = 1e-6
LOG2E = math.log2(math.e)
CHUNK = 64
SGU_CHUNK = 128
REL_BUCKETS = 32
REL_MAX_DIST = 128
TOP_K = 2
MOE_BLOCK = 512
MOE_BLOCK_SMALL = 128
MOE_TILE = 1024
RUN_CHUNK = 32
HG_SUB = 16

LANES = 128
SUBLANES = 8
VMEM_LIMIT = 56 * 1024 * 1024

NT_DIMS = (((1,), (1,)), ((), ()))
TN_DIMS = (((0,), (0,)), ((), ()))


def _params(*sem):
    return pltpu.CompilerParams(dimension_semantics=sem, vmem_limit_bytes=VMEM_LIMIT)


def _const_spec(shape):
    nd = len(shape)
    return pl.BlockSpec(shape, lambda *_: (0,) * nd, pipeline_mode=pl.Buffered(1))


def _sigmoid(x):
    return 1.0 / (1.0 + jnp.exp(-x))


def _rms(x, g):
    return x * lax.rsqrt(jnp.mean(x * x, axis=-1, keepdims=True) + EPS) * g


def _dot(a, b):
    return jnp.dot(a, b, preferred_element_type=F32)


def rel_bucket(rel):
    half = REL_BUCKETS // 2
    max_exact = half // 2
    ret = (rel > 0).astype(I32) * half
    n = jnp.abs(rel)
    nf = jnp.maximum(n, 1).astype(F32)
    large = max_exact + (jnp.log(nf / max_exact) / math.log(REL_MAX_DIST / max_exact)
                         * (half - max_exact)).astype(I32)
    large = jnp.minimum(large, half - 1)
    return ret + jnp.where(n < max_exact, n, large)


def _in_even_kernel(x_ref, g_ref, w_ref, pm_ref, qg_ref, kg_ref, lb_ref,
                    q_ref, k_ref, v_ref, qh_ref, kb_ref, lf_ref, ih_ref, gs_ref, k5_ref, v4_ref, *, width,
                    q_scale):
    xn = _rms(x_ref[...], g_ref[...]).astype(BF16)

    def proj(c):
        return _dot(xn, w_ref[:, c * width:(c + 1) * width])

    def group_norm(y, gain):
        ms = _dot((y * y).astype(BF16), pm_ref[...])
        return y * lax.rsqrt(ms + EPS) * gain

    q_ref[...] = group_norm(proj(0), qg_ref[...]) * q_scale
    kn = group_norm(proj(1), kg_ref[...])
    k_ref[...] = kn
    bb, ts, heads, _, dh = k5_ref.shape
    for h in range(heads):
        for c in range(2):
            piece = kn[:, (2 * h + c) * dh:(2 * h + c + 1) * dh]
            k5_ref[:, :, h, c, :] = piece.reshape(bb, ts, dh)
    vv = proj(2)
    v_ref[...] = vv
    for h in range(heads):
        v4_ref[:, :, h, :] = vv[:, h * 2 * dh:(h + 1) * 2 * dh].reshape(bb, ts, 2 * dh)
    yq = proj(3)
    qh_ref[...] = yq * _sigmoid(yq)
    zf = proj(4)
    lb = lb_ref[...]
    lf_ref[...] = jnp.log(lb + (1.0 - lb) * _sigmoid(zf))
    kb_ref[...] = (1.0 - lb) * _sigmoid(-zf)
    ih_ref[...] = proj(5)
    yg = proj(6)
    gs_ref[...] = yg * _sigmoid(yg)


def _in_even(x, g_mix, w_bf, pm, qg, kg, lb, *, width, q_scale, tm, batch, seq, heads, dh):
    n, d = x.shape
    assert n % tm == 0 and (tm % seq == 0 or seq % tm == 0)
    row = lambda i: (i, 0)
    out = jax.ShapeDtypeStruct((n, width), F32)
    bb, ts = max(tm // seq, 1), min(tm, seq)
    per = seq // ts
    k5_spec = pl.BlockSpec((bb, ts, heads, 2, dh), lambda i: (i // per, i % per, 0, 0, 0))
    v4_spec = pl.BlockSpec((bb, ts, heads, 2 * dh), lambda i: (i // per, i % per, 0, 0))
    return pl.pallas_call(
        functools.partial(_in_even_kernel, width=width, q_scale=q_scale),
        grid=(n // tm,),
        in_specs=[pl.BlockSpec((tm, d), row), _const_spec((1, d)), _const_spec(w_bf.shape),
                  _const_spec(pm.shape), _const_spec((1, width)), _const_spec((1, width)),
                  _const_spec((1, width))],
        out_specs=[pl.BlockSpec((tm, width), row)] * 8 + [k5_spec, v4_spec],
        out_shape=[out] * 8 + [jax.ShapeDtypeStruct((batch, seq, heads, 2, dh), F32),
                               jax.ShapeDtypeStruct((batch, seq, heads, 2 * dh), F32)],
        compiler_params=_params("arbitrary"),
        name="in_even",
    )(x, g_mix, w_bf, pm, qg, kg, lb)


def _bias_from_buckets(bk, rb_ref, h):
    b = jnp.zeros(bk.shape, F32)
    for u in range(REL_BUCKETS):
        b = jnp.where(bk == u, rb_ref[u, h], b)
    return jnp.where(bk < 0, -jnp.inf, b)


def _lam(lam_ref, lam_init):
    r = lam_ref[...]
    s1 = jnp.sum(r[0:1] * r[1:2], axis=1, keepdims=True)
    s2 = jnp.sum(r[2:3] * r[3:4], axis=1, keepdims=True)
    return jnp.exp(s1) - jnp.exp(s2) + lam_init


def _split_components(q, dh):
    lane = lax.broadcasted_iota(I32, q.shape, 1)
    q0 = jnp.where(lane < dh, q, 0.0)
    q1 = jnp.where(lane >= dh, q, 0.0)
    return jnp.concatenate([q0, q1], axis=0).astype(BF16)


def _attn_prompt_kernel(rb_ref, far_ref, lam_ref, bk_ref, sg_ref, q_ref, k_ref, v_ref, o_ref, bias_s, *,
                        qt, dh, lam_init):
    h = pl.program_id(0)
    t = q_ref.shape[0]
    kb = k_ref[...].astype(BF16)
    vt = v_ref[...].T.astype(BF16)
    qtr = q_ref[...].T
    sub = lax.broadcasted_iota(I32, (2 * dh, qt), 0)

    @pl.when(pl.program_id(1) == 0)
    def _():
        for d in range(2):
            b = _bias_from_buckets(bk_ref[d], rb_ref, h) * LOG2E
            bias_s[d] = jnp.concatenate([b, b], axis=1)

    bias = [bias_s[0], bias_s[1]]
    far = rb_ref[far_ref[0], h] * LOG2E
    lam = _lam(lam_ref, lam_init)
    gain = sg_ref[...] * (1.0 - lam_init)
    for i in range(t // qt):
        qi = qtr[:, i * qt:(i + 1) * qt]
        qz = jnp.concatenate([jnp.where(sub < dh, qi, 0.0), jnp.where(sub >= dh, qi, 0.0)],
                             axis=1).astype(BF16)
        n = (i + 1) * qt
        parts = [(n - qt, n)]
        s = [_dot(kb[n - qt:n], qz) + bias[0]]
        shift = [0.0]
        if i >= 1:
            parts.append((n - 2 * qt, n - qt))
            s.append(_dot(kb[n - 2 * qt:n - qt], qz) + bias[1])
            shift.append(0.0)
        if i >= 2:
            parts.append((0, n - 2 * qt))
            s.append(_dot(kb[:n - 2 * qt], qz))
            shift.append(far)
        m = functools.reduce(jnp.maximum, [jnp.max(x, axis=0, keepdims=True) + c for x, c in zip(s, shift)])
        p = [jnp.exp2(x - (m - c)) for x, c in zip(s, shift)]
        l = functools.reduce(lambda a, b: a + b, [jnp.sum(x, axis=0, keepdims=True) for x in p])
        acc = functools.reduce(lambda a, b: a + b,
                               [_dot(vt[:, lo:hi], x.astype(BF16)) for (lo, hi), x in zip(parts, p)])
        o = acc / l
        out = o[:, :qt] - lam * o[:, qt:]
        out = out * lax.rsqrt(jnp.mean(out * out, axis=0, keepdims=True) + EPS) * gain
        o_ref[i * qt:(i + 1) * qt, :] = out.T


def _attn_prompt(q, k, v, rel_bias, lam4, sub_gain, *, batch, seq, heads, dh, lam_init, qt):
    n, w = q.shape
    dv = w // heads
    assert dv == 2 * dh and seq % qt == 0 and qt % CHUNK == 0
    kj = jnp.arange(qt, dtype=I32)[:, None]
    qi = jnp.arange(qt, dtype=I32)[None, :]
    bk0 = jnp.where((kj // CHUNK) <= (qi // CHUNK), rel_bucket(kj - qi), -1)
    bk1 = rel_bucket(kj - qi - qt)
    bk = jnp.stack([bk0, bk1]).astype(I32)
    assert qt + 1 >= REL_MAX_DIST
    far = rel_bucket(jnp.full((1,), -(qt + 1), I32))
    smem = pl.BlockSpec(memory_space=pltpu.SMEM)
    seq_blk = pl.BlockSpec((seq, dv), lambda h, b: (b, h))
    return pl.pallas_call(
        functools.partial(_attn_prompt_kernel, qt=qt, dh=dh, lam_init=lam_init),
        grid=(heads, batch),
        in_specs=[smem, smem, _const_spec(lam4.shape), _const_spec(bk.shape), _const_spec((dv, 1)),
                  seq_blk, seq_blk, seq_blk],
        out_specs=seq_blk,
        out_shape=jax.ShapeDtypeStruct((n, w), F32),
        scratch_shapes=[pltpu.VMEM((2, qt, 2 * qt), F32)],
        compiler_params=_params("arbitrary", "arbitrary"),
        name="attn_prompt",
    )(rel_bias, far, lam4, bk, sub_gain.T, q, k, v)


def _attn_sample_kernel(rb_ref, lam_ref, bkc_ref, bkn_ref, sg_ref, q_ref, kc_ref, vc_ref, kn_ref, vn_ref,
                        o_ref, bc_s, bn_s, *, t, dh, lam_init, heads):
    dv = 2 * dh

    @pl.when(pl.program_id(0) == 0)
    def _():
        for hh in range(heads):
            bc = _bias_from_buckets(bkc_ref[...], rb_ref, hh) * LOG2E
            bn = _bias_from_buckets(bkn_ref[...], rb_ref, hh) * LOG2E
            bc_s[hh] = jnp.concatenate([bc, bc], axis=0)
            bn_s[hh] = jnp.concatenate([bn, bn], axis=0)

    lam = _lam(lam_ref, lam_init)
    past = vc_ref.shape[0] // heads
    for h in range(heads):
        cols = slice(h * dv, (h + 1) * dv)
        qz = _split_components(q_ref[:, cols], dh)
        sc = lax.dot_general(qz, kc_ref[0, :, cols].astype(BF16), NT_DIMS, preferred_element_type=F32)
        sn = lax.dot_general(qz, kn_ref[:, cols].astype(BF16), NT_DIMS, preferred_element_type=F32)
        sc = sc + bc_s[h]
        sn = sn + bn_s[h]
        m = jnp.maximum(jnp.max(sc, axis=1, keepdims=True), jnp.max(sn, axis=1, keepdims=True))
        pc = jnp.exp2(sc - m)
        pn = jnp.exp2(sn - m)
        l = jnp.sum(pc, axis=1, keepdims=True) + jnp.sum(pn, axis=1, keepdims=True)
        vc = vc_ref[pl.ds(h, past, stride=heads), :]
        acc = (_dot(pc.astype(BF16), vc.astype(BF16))
               + _dot(pn.astype(BF16), vn_ref[:, cols].astype(BF16)))
        o = acc / l
        out = o[:t] - lam * o[t:]
        o_ref[:, cols] = _rms(out, sg_ref[...]) * (1.0 - lam_init)


def _attn_sample(q, k_new, v_new, cache_k, cache_v, rel_bias, lam4, sub_gain, *, batch, t, heads, dh,
                 lam_init):
    n, w = q.shape
    dv = w // heads
    past = cache_k.shape[1]
    assert past % CHUNK == 0 and t <= CHUNK
    qpos = past + jnp.arange(t, dtype=I32)[:, None]
    bkc = rel_bucket(jnp.arange(past, dtype=I32)[None, :] - qpos).astype(I32)
    bkn = rel_bucket(past + jnp.arange(t, dtype=I32)[None, :] - qpos).astype(I32)
    smem = pl.BlockSpec(memory_space=pltpu.SMEM)
    new = pl.BlockSpec((t, w), lambda b: (b, 0))
    old_k = pl.BlockSpec((1, past, w), lambda b: (b, 0, 0))
    old_v = pl.BlockSpec((past * heads, dv), lambda b: (b, 0))
    return pl.pallas_call(
        functools.partial(_attn_sample_kernel, t=t, dh=dh, lam_init=lam_init, heads=heads),
        grid=(batch,),
        in_specs=[smem, _const_spec(lam4.shape), _const_spec(bkc.shape), _const_spec(bkn.shape),
                  _const_spec((1, dv)), new, old_k, old_v, new, new],
        out_specs=new,
        out_shape=jax.ShapeDtypeStruct((n, w), F32),
        scratch_shapes=[pltpu.VMEM((heads, 2 * t, past), F32), pltpu.VMEM((heads, 2 * t, t), F32)],
        compiler_params=_params("arbitrary"),
        name="attn_sample",
    )(rel_bias, lam4, bkc, bkn, sub_gain, q, cache_k, cache_v, k_new, v_new)


def _cumsum_rows(x):
    c = x.shape[0]
    row = lax.broadcasted_iota(I32, x.shape, 0)
    s = 1
    while s < c:
        x = x + jnp.where(row >= s, pltpu.roll(x, s, axis=0), 0.0)
        s *= 2
    return x


def _hgrn_kernel(qh_ref, kb_ref, lf_ref, ih_ref, s0_ref, hg_ref, ob_ref, sf_ref, st_s, *,
                 heads, dk, dv, c, nsb):
    t = pl.program_id(1)

    @pl.when(t == 0)
    def _():
        for h in range(heads):
            st_s[h] = s0_ref[0, h].T

    tb = qh_ref.shape[0]
    row = lax.broadcasted_iota(I32, (c, c), 0)
    col = lax.broadcasted_iota(I32, (c, c), 1)
    causal = col <= row

    def chunk(ci, carry):
        r0 = pl.multiple_of(ci * c, c)
        for h in range(heads):
            rows = pl.ds(r0, c)
            q = qh_ref[rows, h * dk:(h + 1) * dk]
            k = kb_ref[rows, h * dk:(h + 1) * dk]
            v = ih_ref[rows, h * dv:(h + 1) * dv]
            b = _cumsum_rows(lf_ref[rows, h * dk:(h + 1) * dk])
            bl = b[c - 1:c]
            st = st_s[h]
            inter = lax.dot_general((q * jnp.exp(b)).astype(BF16), st.astype(BF16), NT_DIMS,
                                    preferred_element_type=F32)
            qs, ks = [], []
            for j in range(nsb):
                ref = b[j * HG_SUB + HG_SUB // 2:j * HG_SUB + HG_SUB // 2 + 1]
                qs.append(q * jnp.exp(b - ref))
                sub = slice(j * HG_SUB, (j + 1) * HG_SUB)
                ks.append(k[sub] * jnp.exp(ref - b[sub]))
            a_full = lax.dot_general(jnp.concatenate(qs, axis=0).astype(BF16),
                                     jnp.concatenate(ks, axis=0).astype(BF16), NT_DIMS,
                                     preferred_element_type=F32)
            att = jnp.zeros((c, c), F32)
            for j in range(nsb):
                att = jnp.where(col >= j * HG_SUB, a_full[j * c:(j + 1) * c], att)
            att = jnp.where(causal, att, 0.0)
            out = inter + _dot(att.astype(BF16), v.astype(BF16))
            ob_ref[rows, h * dv:(h + 1) * dv] = _rms(out, hg_ref[...])
            kdec = (k * jnp.exp(bl - b)).astype(BF16)
            st_s[h] = jnp.exp(bl) * st + lax.dot_general(v.astype(BF16), kdec, TN_DIMS,
                                                         preferred_element_type=F32)
        return carry

    lax.fori_loop(0, tb // c, chunk, 0, unroll=min(4, tb // c))

    @pl.when(t == pl.num_programs(1) - 1)
    def _():
        for h in range(heads):
            sf_ref[0, h] = st_s[h].T


def _hgrn(qh, kb, lf, ih, s0, hg_gain, *, batch, seq, heads, dk, dv, tb):
    n = qh.shape[0]
    c = min(CHUNK, seq)
    assert seq % tb == 0 and tb % c == 0 and c % HG_SUB == 0
    nt = seq // tb
    blk = lambda w: pl.BlockSpec((tb, w), lambda b, t: (b * nt + t, 0))
    st = pl.BlockSpec((1, heads, dk, dv), lambda b, t: (b, 0, 0, 0))
    return pl.pallas_call(
        functools.partial(_hgrn_kernel, heads=heads, dk=dk, dv=dv, c=c, nsb=c // HG_SUB),
        grid=(batch, nt),
        in_specs=[blk(heads * dk), blk(heads * dk), blk(heads * dk), blk(heads * dv), st,
                  _const_spec((1, dv))],
        out_specs=[blk(heads * dv), st],
        out_shape=[jax.ShapeDtypeStruct((n, heads * dv), F32),
                   jax.ShapeDtypeStruct((batch, heads, dk, dv), F32)],
        scratch_shapes=[pltpu.VMEM((heads, dv, dk), F32)],
        compiler_params=_params("arbitrary", "arbitrary"),
        name="hgrn",
    )(qh, kb, lf, ih, s0, hg_gain)


def _store_token_tiles(ref, x):
    rows, d = x.shape
    s = d // LANES
    for c in range(s):
        ref[pl.ds(c, rows, stride=s), :] = x[:, c * LANES:(c + 1) * LANES]


def _load_token_tiles(ref, rows, s):
    return jnp.concatenate([ref[pl.ds(c, rows, stride=s), :] for c in range(s)], axis=1)


def _token_tile(r, s):
    return pl.ds(pl.multiple_of(r * s, s), s)


def _ffn_prologue(x1, gf_ref, wrh_ref, wrl_ref, rbias_ref, x1_ref, xn_ref, lg_ref):
    x1_ref[...] = x1
    xn = _rms(x1, gf_ref[...])
    _store_token_tiles(xn_ref, xn)
    hi = xn.astype(BF16)
    lo = (xn - hi.astype(F32)).astype(BF16)
    nt = functools.partial(lax.dot_general, dimension_numbers=NT_DIMS, preferred_element_type=F32)
    lg_ref[...] = nt(wrh_ref[...], hi) + nt(wrh_ref[...], lo) + nt(wrl_ref[...], hi) + rbias_ref[...]


def _out_even_kernel(oa_ref, ob_ref, gs_ref, x_ref, w_ref, gf_ref, wrh_ref, wrl_ref, rbias_ref,
                     x1_ref, xn_ref, lg_ref):
    o = jnp.concatenate([oa_ref[...], ob_ref[...] * gs_ref[...]], axis=1).astype(BF16)
    x1 = x_ref[...] + _dot(o, w_ref[...])
    _ffn_prologue(x1, gf_ref, wrh_ref, wrl_ref, rbias_ref, x1_ref, xn_ref, lg_ref)


def _out_even(oa, ob, gs, x, w_bf, g_ffn, wrh, wrl, rbias, *, tm):
    n, d = x.shape
    w = oa.shape[1]
    nr = wrh.shape[0]
    row = lambda i: (i, 0)
    return pl.pallas_call(
        _out_even_kernel,
        grid=(n // tm,),
        in_specs=[pl.BlockSpec((tm, w), row), pl.BlockSpec((tm, w), row), pl.BlockSpec((tm, w), row),
                  pl.BlockSpec((tm, d), row), _const_spec(w_bf.shape), _const_spec((1, d)),
                  _const_spec(wrh.shape), _const_spec(wrl.shape), _const_spec(rbias.shape)],
        out_specs=[pl.BlockSpec((tm, d), row), pl.BlockSpec((tm * (d // LANES), LANES), row),
                   pl.BlockSpec((nr, tm), lambda i: (0, i))],
        out_shape=[jax.ShapeDtypeStruct((n, d), F32), jax.ShapeDtypeStruct((n * (d // LANES), LANES), F32),
                   jax.ShapeDtypeStruct((nr, n), F32)],
        compiler_params=_params("arbitrary"),
        name="out_even",
    )(oa, ob, gs, x, w_bf, g_ffn, wrh, wrl, rbias)


def _gelu(x):
    return 0.5 * x * (1.0 + jnp.tanh(math.sqrt(2.0 / math.pi) * (x + 0.044715 * (x * x * x))))


def _odd_kernel(x_ref, gm_ref, win_ref, vg_ref, wsp_ref, bsp_ref, wout_ref, gf_ref, wrh_ref, wrl_ref,
                rbias_ref, x1_ref, xn_ref, lg_ref, *rest, half, groups, l, emit_v):
    if emit_v:
        vn_ref, u_s, s_s = rest
    else:
        vn_ref = None
        u_s, vn_s, s_s = rest
    tm = x_ref.shape[0]
    x = x_ref[...]
    xn = _rms(x, gm_ref[...]).astype(BF16)
    cw = 512
    vbuf = vn_ref if emit_v else vn_s
    for cidx in range(half // cw):
        u_s[:, cidx * cw:(cidx + 1) * cw] = _gelu(_dot(xn, win_ref[:, cidx * cw:(cidx + 1) * cw]))
        vbuf[:, cidx * cw:(cidx + 1) * cw] = _gelu(
            _dot(xn, win_ref[:, half + cidx * cw:half + (cidx + 1) * cw]))
    vbuf[...] = _rms(vbuf[...], vg_ref[...])
    gw = half // groups
    row = lax.broadcasted_iota(I32, (l, l), 0)
    col = lax.broadcasted_iota(I32, (l, l), 1)
    for g in range(groups):
        wg = jnp.where(col <= row, wsp_ref[g], 0.0).astype(BF16)
        bg = bsp_ref[:, g:g + 1]
        for ci in range(tm // l):
            vv = vbuf[ci * l:(ci + 1) * l, g * gw:(g + 1) * gw].astype(BF16)
            s_s[ci * l:(ci + 1) * l, g * gw:(g + 1) * gw] = _dot(wg, vv) + bg
    y = _dot((u_s[...] * s_s[...]).astype(BF16), wout_ref[...])
    _ffn_prologue(x + y, gf_ref, wrh_ref, wrl_ref, rbias_ref, x1_ref, xn_ref, lg_ref)


def _odd_mixer(x, g_mix, win_bf, v_gain, wsp, bsp_t, wout_bf, g_ffn, wrh, wrl, rbias, *, l, tm, emit_v):
    n, d = x.shape
    half = wout_bf.shape[0]
    groups = wsp.shape[0]
    nr = wrh.shape[0]
    assert n % tm == 0 and tm % l == 0
    row = lambda i: (i, 0)
    out_specs = [pl.BlockSpec((tm, d), row), pl.BlockSpec((tm * (d // LANES), LANES), row),
                 pl.BlockSpec((nr, tm), lambda i: (0, i))]
    out_shape = [jax.ShapeDtypeStruct((n, d), F32), jax.ShapeDtypeStruct((n * (d // LANES), LANES), F32),
                 jax.ShapeDtypeStruct((nr, n), F32)]
    scratch = [pltpu.VMEM((tm, half), F32)]
    if emit_v:
        out_specs.append(pl.BlockSpec((tm, half), row))
        out_shape.append(jax.ShapeDtypeStruct((n, half), F32))
    else:
        scratch.append(pltpu.VMEM((tm, half), F32))
    scratch.append(pltpu.VMEM((tm, half), F32))
    return pl.pallas_call(
        functools.partial(_odd_kernel, half=half, groups=groups, l=l, emit_v=emit_v),
        grid=(n // tm,),
        in_specs=[pl.BlockSpec((tm, d), row), _const_spec((1, d)), _const_spec(win_bf.shape),
                  _const_spec((1, half)), _const_spec(wsp.shape), _const_spec(bsp_t.shape),
                  _const_spec(wout_bf.shape), _const_spec((1, d)), _const_spec(wrh.shape),
                  _const_spec(wrl.shape), _const_spec(rbias.shape)],
        out_specs=out_specs,
        out_shape=out_shape,
        scratch_shapes=scratch,
        compiler_params=_params("arbitrary"),
        name="odd_mixer",
    )(x, g_mix, win_bf, v_gain, wsp, bsp_t, wout_bf, g_ffn, wrh, wrl, rbias)


def _route_kernel(lg_ref, tri_ref, init_ref, e_ref, g_ref, r_ref, p_ref, cnt_ref, base_ref, tcnt_ref, run_s,
                  *, groups, epg):
    i = pl.program_id(0)

    @pl.when(i == 0)
    def _():
        run_s[...] = init_ref[...]

    lg = lg_ref[...]
    tr = lg.shape[1]
    gl = [lg[g:g + 1] for g in range(groups)]
    m = functools.reduce(jnp.maximum, gl)
    grp = jnp.full((1, tr), groups - 1, I32)
    for g in range(groups - 2, -1, -1):
        grp = jnp.where(gl[g] == m, g, grp)
    gate_g = 1.0 / functools.reduce(lambda a, b: a + b, [jnp.exp(x - m) for x in gl])
    sel = lg[SUBLANES + (groups - 1) * epg:SUBLANES + groups * epg]
    for g in range(groups - 2, -1, -1):
        sel = jnp.where(grp == g, lg[SUBLANES + g * epg:SUBLANES + (g + 1) * epg], sel)
    sub = lax.broadcasted_iota(I32, sel.shape, 0)
    v1 = jnp.max(sel, axis=0, keepdims=True)
    i1 = jnp.min(jnp.where(sel == v1, sub, epg), axis=0, keepdims=True)
    sel2 = jnp.where(sub == i1, -jnp.inf, sel)
    v2 = jnp.max(sel2, axis=0, keepdims=True)
    i2 = jnp.min(jnp.where(sel2 == v2, sub, epg), axis=0, keepdims=True)
    tt = jnp.exp(v2 - v1)
    g1 = gate_g / (1.0 + tt)
    g2 = gate_g * tt / (1.0 + tt)
    e1 = grp * epg + i1
    e2 = grp * epg + i2
    ne = groups * epg
    eidx = lax.broadcasted_iota(I32, (ne, tr), 0)
    oh1 = eidx == e1
    oh2 = eidx == e2
    cnt = jnp.where(oh1, 1.0, 0.0) + jnp.where(oh2, 1.0, 0.0)
    local = _dot(cnt.astype(BF16), tri_ref[...])
    before = run_s[:, 0:1] + local
    r1 = jnp.sum(jnp.where(oh1, before, 0.0), axis=0, keepdims=True)
    r2 = jnp.sum(jnp.where(oh2, before, 0.0), axis=0, keepdims=True)
    tile_cnt = jnp.broadcast_to(jnp.sum(cnt, axis=1, keepdims=True), run_s.shape)
    padded = jnp.ceil(tile_cnt * (1.0 / RUN_CHUNK)) * RUN_CHUNK
    offset = _cumsum_rows(padded) - padded
    where_local = offset[:, 0:1] + local
    p1 = jnp.sum(jnp.where(oh1, where_local, 0.0), axis=0, keepdims=True)
    p2 = jnp.sum(jnp.where(oh2, where_local, 0.0), axis=0, keepdims=True)
    base_ref[...] = run_s[...]
    tcnt_ref[...] = tile_cnt
    run_s[...] = run_s[...] + tile_cnt
    rows = lax.broadcasted_iota(I32, (SUBLANES, tr), 0)
    e_ref[...] = jnp.where(rows == 0, e1, jnp.where(rows == 1, e2, 0))
    g_ref[...] = jnp.where(rows == 0, g1, jnp.where(rows == 1, g2, 0.0))
    r_ref[...] = jnp.where(rows == 0, r1, jnp.where(rows == 1, r2, 0.0)).astype(I32)
    p_ref[...] = jnp.where(rows == 0, p1, jnp.where(rows == 1, p2, 0.0)).astype(I32)
    cnt_ref[...] = run_s[...]


def _route(lgt, init, *, groups, epg, tr):
    nr, n = lgt.shape
    assert n % tr == 0 and nr == SUBLANES + groups * epg
    ne = groups * epg
    tri = (jnp.arange(tr)[:, None] < jnp.arange(tr)[None, :]).astype(BF16)
    tok = pl.BlockSpec((SUBLANES, tr), lambda i: (0, i))
    per_tile = pl.BlockSpec((ne, LANES), lambda i: (i, 0))
    tile_tab = jax.ShapeDtypeStruct((n // tr * ne, LANES), F32)
    return pl.pallas_call(
        functools.partial(_route_kernel, groups=groups, epg=epg),
        grid=(n // tr,),
        in_specs=[pl.BlockSpec((nr, tr), lambda i: (0, i)), _const_spec((tr, tr)), _const_spec((ne, LANES))],
        out_specs=[tok, tok, tok, tok, pl.BlockSpec((ne, LANES), lambda i: (0, 0)), per_tile, per_tile],
        out_shape=[jax.ShapeDtypeStruct((SUBLANES, n), I32), jax.ShapeDtypeStruct((SUBLANES, n), F32),
                   jax.ShapeDtypeStruct((SUBLANES, n), I32), jax.ShapeDtypeStruct((SUBLANES, n), I32),
                   jax.ShapeDtypeStruct((ne, LANES), F32), tile_tab, tile_tab],
        scratch_shapes=[pltpu.VMEM((ne, LANES), F32)],
        compiler_params=_params("arbitrary"),
        name="route",
    )(lgt, tri, init)


ISSUE_UNROLL = 16
DISPATCH_SLOTS = 3


def _dispatch_kernel(seg_ref, dst_ref, *rest, s, bm, tiles):
    x_refs = rest[:len(tiles)]
    buf_ref, zero_s, xin_s, sem, isem, zsem = rest[len(tiles):]
    tp = dst_ref.shape[2]
    blk = bm * s

    def zero_fill():
        zero_s[...] = jnp.zeros(zero_s.shape, F32)

        def block_copy(b):
            return pltpu.make_async_copy(zero_s, buf_ref.at[pl.ds(pl.multiple_of(b * blk, blk), blk)], zsem)

        for e in range(seg_ref.shape[1]):
            @pl.when(seg_ref[1, e] > 0)
            def _():
                block_copy(seg_ref[0, e] // bm - 1).start()

            @pl.when(seg_ref[1, e] - seg_ref[2, e] > bm)
            def _():
                block_copy(seg_ref[0, e] // bm - 2).start()
        for e in range(seg_ref.shape[1]):
            @pl.when(seg_ref[1, e] > 0)
            def _():
                block_copy(0).wait()

            @pl.when(seg_ref[1, e] - seg_ref[2, e] > bm)
            def _():
                block_copy(0).wait()

        def tail_start(b, c):
            block_copy(b).start()
            return c

        def tail_wait(b, c):
            block_copy(b).wait()
            return c

        first_unused = seg_ref[0, seg_ref.shape[1] - 1] // bm
        n_blocks = buf_ref.shape[0] // blk
        lax.fori_loop(first_unused, n_blocks, tail_start, 0)
        lax.fori_loop(first_unused, n_blocks, tail_wait, 0)

    pl.when(pl.program_id(0) == 0)(zero_fill)

    i = pl.program_id(0)
    last = pl.num_programs(0) - 1
    rows = tp * s

    def tile_in(t, go):
        lo = 0
        for x_ref, nt in zip(x_refs, tiles):
            @pl.when((t >= lo) & (t < lo + nt))
            def _(x_ref=x_ref, lo=lo):
                cp = pltpu.make_async_copy(x_ref.at[pl.ds(pl.multiple_of((t - lo) * rows, rows), rows)],
                                           xin_s.at[t % DISPATCH_SLOTS], isem.at[t % DISPATCH_SLOTS])
                cp.start() if go else cp.wait()
            lo += nt

    def rows_out_wait(t):
        for kk in range(TOP_K):
            pltpu.make_async_copy(xin_s.at[t % DISPATCH_SLOTS], buf_ref.at[pl.ds(0, rows)],
                                  sem.at[t % DISPATCH_SLOTS]).wait()

    @pl.when(i == 0)
    def _():
        tile_in(i, True)

    @pl.when(i >= DISPATCH_SLOTS - 1)
    def _():
        rows_out_wait(i - (DISPATCH_SLOTS - 1))

    @pl.when(i < last)
    def _():
        tile_in(i + 1, True)

    tile_in(i, False)
    slot = i % DISPATCH_SLOTS

    def issue(r0, c):
        for u in range(ISSUE_UNROLL):
            r = r0 * ISSUE_UNROLL + u
            for kk in range(TOP_K):
                pltpu.make_async_copy(xin_s.at[slot, _token_tile(r, s)],
                                      buf_ref.at[_token_tile(dst_ref[0, kk, r], s)], sem.at[slot]
                                      ).start(priority=kk % 2)
        return c

    lax.fori_loop(0, tp // ISSUE_UNROLL, issue, 0)

    @pl.when(i == last)
    def _():
        for back in range(DISPATCH_SLOTS - 2, -1, -1):
            @pl.when(i >= back)
            def _():
                rows_out_wait(i - back)


def _dispatch(seg, dest3, xns, n_slots, *, tp, s, bm):
    assert tp % ISSUE_UNROLL == 0 and all(x.shape[0] % (tp * s) == 0 for x in xns)
    tiles = tuple(x.shape[0] // (tp * s) for x in xns)
    grid_spec = pltpu.PrefetchScalarGridSpec(
        num_scalar_prefetch=1,
        grid=(sum(tiles),),
        in_specs=[pl.BlockSpec((1, TOP_K, tp), lambda i, sg: (i, 0, 0), memory_space=pltpu.SMEM)]
        + [pl.BlockSpec(memory_space=pl.ANY)] * len(xns),
        out_specs=pl.BlockSpec(memory_space=pl.ANY),
        scratch_shapes=[pltpu.VMEM((bm * s, LANES), F32), pltpu.VMEM((DISPATCH_SLOTS, tp * s, LANES), F32),
                        pltpu.SemaphoreType.DMA((DISPATCH_SLOTS,)), pltpu.SemaphoreType.DMA((DISPATCH_SLOTS,)),
                        pltpu.SemaphoreType.DMA(())],
    )
    return pl.pallas_call(
        functools.partial(_dispatch_kernel, s=s, bm=bm, tiles=tiles),
        grid_spec=grid_spec,
        out_shape=jax.ShapeDtypeStruct((n_slots * s, LANES), F32),
        compiler_params=_params("arbitrary"),
        name="dispatch",
    )(seg, dest3, *xns)


def _expert_kernel(be_ref, nu_ref, x_ref, wg_ref, wu_ref, wd_ref, o_ref, wg_s, wu_s, wd_s):
    b = pl.program_id(0)
    s = wg_s.shape[0] // LANES
    prev = be_ref[jnp.maximum(b - 1, 0)]

    @pl.when((b == 0) | (be_ref[b] != prev))
    def _():
        wg_s[...] = wg_ref[0, 0].astype(BF16)
        wu_s[...] = wu_ref[0, 0].astype(BF16)
        wd_s[...] = wd_ref[0, 0].astype(BF16)

    @pl.when(b < nu_ref[0])
    def _():
        xb = _load_token_tiles(x_ref, x_ref.shape[0] // s, s).astype(BF16)
        gate = _dot(xb, wg_s[...])
        h = gate * _sigmoid(gate) * _dot(xb, wu_s[...])
        _store_token_tiles(o_ref, _dot(h.astype(BF16), wd_s[...]))

    @pl.when(b >= nu_ref[0])
    def _():
        o_ref[...] = jnp.zeros(o_ref.shape, F32)


def _experts(blk_expert, n_used, buf, w_gate, w_up, w_down, *, layer, bm):
    d, de = w_gate.shape[2:]
    s = d // LANES
    n_blocks = buf.shape[0] // (bm * s)
    rows = pl.BlockSpec((bm * s, LANES), lambda b, be, nu: (b, 0))
    used_rows = pl.BlockSpec((bm * s, LANES), lambda b, be, nu: (jnp.minimum(b, nu[0] - 1), 0))
    grid_spec = pltpu.PrefetchScalarGridSpec(
        num_scalar_prefetch=2,
        grid=(n_blocks,),
        in_specs=[used_rows,
                  pl.BlockSpec((1, 1, d, de), lambda b, be, nu: (layer, be[b], 0, 0)),
                  pl.BlockSpec((1, 1, d, de), lambda b, be, nu: (layer, be[b], 0, 0)),
                  pl.BlockSpec((1, 1, de, d), lambda b, be, nu: (layer, be[b], 0, 0))],
        out_specs=rows,
        scratch_shapes=[pltpu.VMEM((d, de), BF16), pltpu.VMEM((d, de), BF16), pltpu.VMEM((de, d), BF16)],
    )
    return pl.pallas_call(
        _expert_kernel,
        grid_spec=grid_spec,
        out_shape=jax.ShapeDtypeStruct(buf.shape, F32),
        compiler_params=_params("arbitrary"),
        name="experts",
    )(blk_expert, n_used, buf, w_gate, w_up, w_down)


def _combine_kernel(tab_ref, nxt_ref, pos_ref, gate_ref, x_ref, yb_ref, o_ref, stg_s, tt_s, sem, *, ahead):
    i = pl.program_id(0)
    tq, d = x_ref.shape
    s = d // LANES
    piece = RUN_CHUNK * s
    slot = i % 2

    def run_copies(ref, dst_slot, go):
        for e in range(ref.shape[2]):
            def body(j, c):
                src = pl.multiple_of((ref[0, 0, e] + j * RUN_CHUNK) * s, s)
                dst = pl.multiple_of((ref[0, 2, e] + j * RUN_CHUNK) * s, piece)
                cp = pltpu.make_async_copy(yb_ref.at[pl.ds(src, piece)],
                                           stg_s.at[dst_slot, pl.ds(dst, piece)], sem.at[dst_slot])
                cp.start() if go else cp.wait()
                return c
            lax.fori_loop(0, ref[0, 1, e], body, 0)

    if ahead:
        @pl.when(i == 0)
        def _():
            run_copies(tab_ref, 0, True)

        @pl.when(i + 1 < pl.num_programs(0))
        def _():
            run_copies(nxt_ref, 1 - slot, True)
    else:
        run_copies(tab_ref, slot, True)

    run_copies(tab_ref, slot, False)

    def assemble(r0, c):
        for u in range(ISSUE_UNROLL):
            r = r0 * ISSUE_UNROLL + u
            acc = gate_ref[0, 0, r] * stg_s[slot, _token_tile(pos_ref[0, 0, r], s), :]
            for kk in range(1, TOP_K):
                acc = acc + gate_ref[0, kk, r] * stg_s[slot, _token_tile(pos_ref[0, kk, r], s), :]
            tt_s[_token_tile(r, s), :] = acc
        return c

    lax.fori_loop(0, tq // ISSUE_UNROLL, assemble, 0)
    o_ref[...] = x_ref[...] + _load_token_tiles(tt_s, tq, s)


def _combine(tab, pos3, gates3, x1, yb, *, tq):
    n, d = x1.shape
    s = d // LANES
    nt = n // tq
    ne = tab.shape[2]
    assert tq % ISSUE_UNROLL == 0
    stage_rows = tq * TOP_K + ne * RUN_CHUNK
    smem = lambda shape, imap: pl.BlockSpec(shape, imap, memory_space=pltpu.SMEM)
    return pl.pallas_call(
        functools.partial(_combine_kernel, ahead=nt > 2),
        grid=(nt,),
        in_specs=[smem((1, 3, ne), lambda i: (i, 0, 0)),
                  smem((1, 3, ne), lambda i: (jnp.minimum(i + 1, nt - 1), 0, 0)),
                  smem((1, TOP_K, tq), lambda i: (i, 0, 0)),
                  smem((1, TOP_K, tq), lambda i: (i, 0, 0)),
                  pl.BlockSpec((tq, d), lambda i: (i, 0)),
                  pl.BlockSpec(memory_space=pl.ANY)],
        out_specs=pl.BlockSpec((tq, d), lambda i: (i, 0)),
        out_shape=jax.ShapeDtypeStruct((n, d), F32),
        scratch_shapes=[pltpu.VMEM((2, stage_rows * s, LANES), F32), pltpu.VMEM((tq * s, LANES), F32),
                        pltpu.SemaphoreType.DMA((2,))],
        compiler_params=_params("arbitrary"),
        name="combine",
    )(tab, tab, pos3, gates3, x1, yb)


def _moe(streams, w_gate, w_up, w_down, *, layer, groups, epg):
    d = streams[0][0].shape[1]
    ne = groups * epg
    routed = []
    taken = jnp.zeros((ne, LANES), F32)
    for x1, _, lgt in streams:
        rtile = min(MOE_TILE, x1.shape[0])
        routed.append(_route(lgt, taken, groups=groups, epg=epg, tr=rtile) + (rtile,))
        taken = routed[-1][4]
    n_rows = sum(x1.shape[0] for x1, _, _ in streams) * TOP_K
    counts = taken[:, 0].astype(I32)
    bm = MOE_BLOCK if n_rows >= 2 * ne * MOE_BLOCK else MOE_BLOCK_SMALL
    padded = (counts + RUN_CHUNK + bm - 1) // bm * bm
    pend = jnp.cumsum(padded)
    pstart = pend - padded
    n_blocks = -(-(n_rows + ne * RUN_CHUNK) // bm) + ne
    eids = jnp.arange(ne, dtype=I32)
    blk_row = jnp.arange(n_blocks, dtype=I32)[:, None] * bm
    blk_expert = jnp.minimum(jnp.sum((pend[None, :] <= blk_row).astype(I32), axis=1), ne - 1)
    n_used = (pend[-1:] // bm).astype(I32)
    seg_tab = jnp.stack([pend, padded, counts]).astype(I32)
    dtile = min(r[-1] for r in routed)
    dests = []
    for (x1, _, _), (e8, g8, r8, p8, cnt, base, tcnt, rtile) in zip(streams, routed):
        seg = jnp.sum(jnp.where(e8[:TOP_K, :, None] == eids, pstart, 0), axis=-1)
        dest = seg + r8[:TOP_K]
        dests.append(dest.reshape(TOP_K, x1.shape[0] // dtile, dtile).transpose(1, 0, 2))
    buf = _dispatch(seg_tab, jnp.concatenate(dests, axis=0), [xn for _, xn, _ in streams], n_blocks * bm,
                    tp=dtile, s=d // LANES, bm=bm)
    yb = _experts(blk_expert, n_used, buf, w_gate, w_up, w_down, layer=layer, bm=bm)
    outs = []
    for (x1, _, _), (e8, g8, r8, p8, cnt, base, tcnt, rtile) in zip(streams, routed):
        nrt = x1.shape[0] // rtile
        run_start = pstart[None, :] + base.reshape(nrt, ne, LANES)[:, :, 0].astype(I32)
        pieces = (tcnt.reshape(nrt, ne, LANES)[:, :, 0].astype(I32) + RUN_CHUNK - 1) // RUN_CHUNK
        stage = (jnp.cumsum(pieces, axis=1) - pieces) * RUN_CHUNK
        tab = jnp.stack([run_start, pieces, stage], axis=1)
        by_tile = lambda a: a[:TOP_K].reshape(TOP_K, nrt, rtile).transpose(1, 0, 2)
        outs.append(_combine(tab, by_tile(p8), by_tile(g8), x1, yb, tq=rtile))
    return outs


def _router_weights(wg, bg, we, be):
    d, groups = wg.shape
    epg = we.shape[2]
    assert groups <= SUBLANES and epg == SUBLANES
    pad = jnp.zeros((SUBLANES - groups, d), F32)
    wr = jnp.concatenate([wg.T, pad, we.transpose(0, 2, 1).reshape(groups * epg, d)], axis=0)
    rb = jnp.concatenate([bg, jnp.zeros((SUBLANES - groups,), F32), be.reshape(-1)])[:, None]
    hi = wr.astype(BF16)
    lo = (wr - hi.astype(F32)).astype(BF16)
    return hi, lo, rb, groups, epg


def kernel(x_prompt, x_sample, cache_attn_k, cache_attn_v, state_hgrn, rel_bias, norm_mix, norm_ffn,
           w_in_even, w_out_even, q_norm_gain, k_norm_gain, lam_q1, lam_k1, lam_q2, lam_k2, da_out_gain,
           hgrn_lb_logits, hgrn_out_gain, w_in_odd, sgu_v_gain, sgu_w, sgu_b, w_out_odd,
           router_group_w, router_group_b, router_expert_w, router_expert_b,
           expert_w_gate, expert_w_up, expert_w_down):
    bp, tp, d = x_prompt.shape
    bs, ts, _ = x_sample.shape
    depth = norm_mix.shape[0]
    _, _, past, da_heads, _, da_dh = cache_attn_k.shape
    da_dv = cache_attn_v.shape[-1]
    _, _, hg_heads, hg_dk, hg_dv = state_hgrn.shape
    width = da_heads * da_dv
    assert width == da_heads * 2 * da_dh == hg_heads * hg_dk == hg_heads * hg_dv
    assert da_dv == LANES and hg_dk == LANES and hg_dv == LANES

    lb_all = jnp.cumsum(jax.nn.softmax(hgrn_lb_logits.astype(F32), axis=0), axis=0)
    gid = jnp.arange(width) // da_dh
    pm = jnp.where(gid[:, None] == gid[None, :], 1.0 / da_dh, 0.0).astype(BF16)

    xs = {"p": x_prompt.reshape(bp * tp, d), "s": x_sample.reshape(bs * ts, d)}
    dims = {"p": (bp, tp), "s": (bs, ts)}
    outs = {"p": {}, "s": {}}
    kp_l, vp_l, ks_l, vs_l, sp_l, ss_l, sgu_l = [], [], [], [], [], [], []

    for layer in range(depth):
        j = layer // 2
        wrh, wrl, rbias, groups, epg = _router_weights(
            router_group_w[layer], router_group_b[layer], router_expert_w[layer], router_expert_b[layer])
        g_mix = norm_mix[layer][None, :]
        g_ffn = norm_ffn[layer][None, :]
        if layer % 2 == 0:
            lam_init = 0.8 - 0.6 * math.exp(-0.3 * layer)
            w_in_bf = w_in_even[j].astype(BF16)
            w_out_bf = w_out_even[j].astype(BF16)
            reps = width // da_dh
            qg = jnp.tile(q_norm_gain[j], reps)[None, :]
            kg = jnp.tile(k_norm_gain[j], reps)[None, :]
            lam4 = jnp.stack([lam_q1[j], lam_k1[j], lam_q2[j], lam_k2[j]])
            sub_gain = da_out_gain[j][None, :]
            hg_gain = hgrn_out_gain[j][None, :]
            lb = lb_all[j][None, :]
            for key in ("p", "s"):
                b, t = dims[key]
                x = xs[key]
                q, k, v, qh, kb, lf, ih, gs, k5, v4 = _in_even(
                    x, g_mix, w_in_bf, pm, qg, kg, lb, width=width, q_scale=da_dh ** -0.5 * LOG2E,
                    tm=min(512, b * t), batch=b, seq=t, heads=da_heads, dh=da_dh)
                if key == "p":
                    oa = _attn_prompt(q, k, v, rel_bias, lam4, sub_gain, batch=b, seq=t, heads=da_heads,
                                      dh=da_dh, lam_init=lam_init, qt=min(256, t))
                    s0 = jnp.zeros((b, hg_heads, hg_dk, hg_dv), F32)
                    ob, s_new = _hgrn(qh, kb, lf, ih, s0, hg_gain, batch=b, seq=t, heads=hg_heads,
                                      dk=hg_dk, dv=hg_dv, tb=min(512, t))
                    kp_l.append(k5)
                    vp_l.append(v4)
                    sp_l.append(s_new)
                else:
                    ck = cache_attn_k[j].reshape(b, past, width)
                    cv = cache_attn_v[j].reshape(b * past * da_heads, da_dv)
                    oa = _attn_sample(q, k, v, ck, cv, rel_bias, lam4, sub_gain, batch=b, t=t,
                                      heads=da_heads, dh=da_dh, lam_init=lam_init)
                    ob, s_new = _hgrn(qh, kb, lf, ih, state_hgrn[j], hg_gain, batch=b, seq=t,
                                      heads=hg_heads, dk=hg_dk, dv=hg_dv, tb=t)
                    ks_l.append(k5)
                    vs_l.append(v4)
                    ss_l.append(s_new)
                outs[key] = _out_even(oa, ob, gs, x, w_out_bf, g_ffn, wrh, wrl, rbias, tm=min(512, b * t))
        else:
            w_in_bf = w_in_odd[j].astype(BF16)
            w_out_bf = w_out_odd[j].astype(BF16)
            v_gain = sgu_v_gain[j][None, :]
            for key in ("p", "s"):
                b, t = dims[key]
                l = min(SGU_CHUNK, t)
                res = _odd_mixer(xs[key], g_mix, w_in_bf, v_gain, sgu_w[j][:, :l, :l], sgu_b[j][:, :l].T,
                                 w_out_bf, g_ffn, wrh, wrl, rbias, l=l, tm=min(512, b * t), emit_v=(key == "s"))
                outs[key] = res[:3]
                if key == "s":
                    sgu_l.append(res[3].reshape(b, t, -1))
        xs["p"], xs["s"] = _moe([outs["p"], outs["s"]], expert_w_gate, expert_w_up, expert_w_down,
                                layer=layer, groups=groups, epg=epg)

    return (xs["p"].reshape(bp, tp, d), xs["s"].reshape(bs, ts, d), jnp.stack(kp_l), jnp.stack(vp_l),
            jnp.stack(ks_l), jnp.stack(vs_l), jnp.stack(sp_l), jnp.stack(ss_l), jnp.stack(sgu_l))
```

```python
import functools
import math

import jax
import jax.numpy as jnp
from jax import lax
from jax.experimental import pallas as pl
from jax.experimental.pallas import tpu as pltpu

F32 = jnp.float32
BF16 = jnp.bfloat16
I32 = jnp.int32

EPS = 1e-6
LOG2E = math.log2(math.e)
CHUNK = 64
SGU_CHUNK = 128
REL_BUCKETS = 32
REL_MAX_DIST = 128
TOP_K = 2
MOE_BLOCK = 512
MOE_BLOCK_SMALL = 128
MOE_TILE = 1024
RUN_CHUNK = 32
HG_SUB = 16

LANES = 128
SUBLANES = 8
VMEM_LIMIT = 56 * 1024 * 1024

NT_DIMS = (((1,), (1,)), ((), ()))
TN_DIMS = (((0,), (0,)), ((), ()))


def _params(*sem):
    return pltpu.CompilerParams(dimension_semantics=sem, vmem_limit_bytes=VMEM_LIMIT)


def _const_spec(shape):
    nd = len(shape)
    return pl.BlockSpec(shape, lambda *_: (0,) * nd, pipeline_mode=pl.Buffered(1))


def _sigmoid(x):
    return 1.0 / (1.0 + jnp.exp(-x))


def _rms(x, g):
    return x * lax.rsqrt(jnp.mean(x * x, axis=-1, keepdims=True) + EPS) * g


def _dot(a, b):
    return jnp.dot(a, b, preferred_element_type=F32)


def rel_bucket(rel):
    half = REL_BUCKETS // 2
    max_exact = half // 2
    ret = (rel > 0).astype(I32) * half
    n = jnp.abs(rel)
    nf = jnp.maximum(n, 1).astype(F32)
    large = max_exact + (jnp.log(nf / max_exact) / math.log(REL_MAX_DIST / max_exact)
                         * (half - max_exact)).astype(I32)
    large = jnp.minimum(large, half - 1)
    return ret + jnp.where(n < max_exact, n, large)


def _in_even_kernel(x_ref, g_ref, w_ref, pm_ref, qg_ref, kg_ref, lb_ref,
                    q_ref, k_ref, v_ref, qh_ref, kb_ref, lf_ref, ih_ref, gs_ref, k5_ref, v4_ref, *, width,
                    q_scale):
    xn = _rms(x_ref[...], g_ref[...]).astype(BF16)

    def proj(c):
        return _dot(xn, w_ref[:, c * width:(c + 1) * width])

    def group_norm(y, gain):
        ms = _dot((y * y).astype(BF16), pm_ref[...])
        return y * lax.rsqrt(ms + EPS) * gain

    q_ref[...] = group_norm(proj(0), qg_ref[...]) * q_scale
    kn = group_norm(proj(1), kg_ref[...])
    k_ref[...] = kn
    bb, ts, heads, _, dh = k5_ref.shape
    for h in range(heads):
        for c in range(2):
            piece = kn[:, (2 * h + c) * dh:(2 * h + c + 1) * dh]
            k5_ref[:, :, h, c, :] = piece.reshape(bb, ts, dh)
    vv = proj(2)
    v_ref[...] = vv
    for h in range(heads):
        v4_ref[:, :, h, :] = vv[:, h * 2 * dh:(h + 1) * 2 * dh].reshape(bb, ts, 2 * dh)
    yq = proj(3)
    qh_ref[...] = yq * _sigmoid(yq)
    zf = proj(4)
    lb = lb_ref[...]
    lf_ref[...] = jnp.log(lb + (1.0 - lb) * _sigmoid(zf))
    kb_ref[...] = (1.0 - lb) * _sigmoid(-zf)
    ih_ref[...] = proj(5)
    yg = proj(6)
    gs_ref[...] = yg * _sigmoid(yg)


def _in_even(x, g_mix, w_bf, pm, qg, kg, lb, *, width, q_scale, tm, batch, seq, heads, dh):
    n, d = x.shape
    assert n % tm == 0 and (tm % seq == 0 or seq % tm == 0)
    row = lambda i: (i, 0)
    out = jax.ShapeDtypeStruct((n, width), F32)
    bb, ts = max(tm // seq, 1), min(tm, seq)
    per = seq // ts
    k5_spec = pl.BlockSpec((bb, ts, heads, 2, dh), lambda i: (i // per, i % per, 0, 0, 0))
    v4_spec = pl.BlockSpec((bb, ts, heads, 2 * dh), lambda i: (i // per, i % per, 0, 0))
    return pl.pallas_call(
        functools.partial(_in_even_kernel, width=width, q_scale=q_scale),
        grid=(n // tm,),
        in_specs=[pl.BlockSpec((tm, d), row), _const_spec((1, d)), _const_spec(w_bf.shape),
                  _const_spec(pm.shape), _const_spec((1, width)), _const_spec((1, width)),
                  _const_spec((1, width))],
        out_specs=[pl.BlockSpec((tm, width), row)] * 8 + [k5_spec, v4_spec],
        out_shape=[out] * 8 + [jax.ShapeDtypeStruct((batch, seq, heads, 2, dh), F32),
                               jax.ShapeDtypeStruct((batch, seq, heads, 2 * dh), F32)],
        compiler_params=_params("arbitrary"),
        name="in_even",
    )(x, g_mix, w_bf, pm, qg, kg, lb)


def _bias_from_buckets(bk, rb_ref, h):
    b = jnp.zeros(bk.shape, F32)
    for u in range(REL_BUCKETS):
        b = jnp.where(bk == u, rb_ref[u, h], b)
    return jnp.where(bk < 0, -jnp.inf, b)


def _lam(lam_ref, lam_init):
    r = lam_ref[...]
    s1 = jnp.sum(r[0:1] * r[1:2], axis=1, keepdims=True)
    s2 = jnp.sum(r[2:3] * r[3:4], axis=1, keepdims=True)
    return jnp.exp(s1) - jnp.exp(s2) + lam_init


def _split_components(q, dh):
    lane = lax.broadcasted_iota(I32, q.shape, 1)
    q0 = jnp.where(lane < dh, q, 0.0)
    q1 = jnp.where(lane >= dh, q, 0.0)
    return jnp.concatenate([q0, q1], axis=0).astype(BF16)


def _attn_prompt_kernel(rb_ref, far_ref, lam_ref, bk_ref, sg_ref, q_ref, k_ref, v_ref, o_ref, bias_s, *,
                        qt, dh, lam_init):
    h = pl.program_id(0)
    t = q_ref.shape[0]
    kb = k_ref[...].astype(BF16)
    vt = v_ref[...].T.astype(BF16)
    qtr = q_ref[...].T
    sub = lax.broadcasted_iota(I32, (2 * dh, qt), 0)

    @pl.when(pl.program_id(1) == 0)
    def _():
        for d in range(2):
            b = _bias_from_buckets(bk_ref[d], rb_ref, h) * LOG2E
            bias_s[d] = jnp.concatenate([b, b], axis=1)

    bias = [bias_s[0], bias_s[1]]
    far = rb_ref[far_ref[0], h] * LOG2E
    lam = _lam(lam_ref, lam_init)
    gain = sg_ref[...] * (1.0 - lam_init)
    for i in range(t // qt):
        qi = qtr[:, i * qt:(i + 1) * qt]
        qz = jnp.concatenate([jnp.where(sub < dh, qi, 0.0), jnp.where(sub >= dh, qi, 0.0)],
                             axis=1).astype(BF16)
        n = (i + 1) * qt
        parts = [(n - qt, n)]
        s = [_dot(kb[n - qt:n], qz) + bias[0]]
        shift = [0.0]
        if i >= 1:
            parts.append((n - 2 * qt, n - qt))
            s.append(_dot(kb[n - 2 * qt:n - qt], qz) + bias[1])
            shift.append(0.0)
        if i >= 2:
            parts.append((0, n - 2 * qt))
            s.append(_dot(kb[:n - 2 * qt], qz))
            shift.append(far)
        m = functools.reduce(jnp.maximum, [jnp.max(x, axis=0, keepdims=True) + c for x, c in zip(s, shift)])
        p = [jnp.exp2(x - (m - c)) for x, c in zip(s, shift)]
        l = functools.reduce(lambda a, b: a + b, [jnp.sum(x, axis=0, keepdims=True) for x in p])
        acc = functools.reduce(lambda a, b: a + b,
                               [_dot(vt[:, lo:hi], x.astype(BF16)) for (lo, hi), x in zip(parts, p)])
        o = acc / l
        out = o[:, :qt] - lam * o[:, qt:]
        out = out * lax.rsqrt(jnp.mean(out * out, axis=0, keepdims=True) + EPS) * gain
        o_ref[i * qt:(i + 1) * qt, :] = out.T


def _attn_prompt(q, k, v, rel_bias, lam4, sub_gain, *, batch, seq, heads, dh, lam_init, qt):
    n, w = q.shape
    dv = w // heads
    assert dv == 2 * dh and seq % qt == 0 and qt % CHUNK == 0
    kj = jnp.arange(qt, dtype=I32)[:, None]
    qi = jnp.arange(qt, dtype=I32)[None, :]
    bk0 = jnp.where((kj // CHUNK) <= (qi // CHUNK), rel_bucket(kj - qi), -1)
    bk1 = rel_bucket(kj - qi - qt)
    bk = jnp.stack([bk0, bk1]).astype(I32)
    assert qt + 1 >= REL_MAX_DIST
    far = rel_bucket(jnp.full((1,), -(qt + 1), I32))
    smem = pl.BlockSpec(memory_space=pltpu.SMEM)
    seq_blk = pl.BlockSpec((seq, dv), lambda h, b: (b, h))
    return pl.pallas_call(
        functools.partial(_attn_prompt_kernel, qt=qt, dh=dh, lam_init=lam_init),
        grid=(heads, batch),
        in_specs=[smem, smem, _const_spec(lam4.shape), _const_spec(bk.shape), _const_spec((dv, 1)),
                  seq_blk, seq_blk, seq_blk],
        out_specs=seq_blk,
        out_shape=jax.ShapeDtypeStruct((n, w), F32),
        scratch_shapes=[pltpu.VMEM((2, qt, 2 * qt), F32)],
        compiler_params=_params("arbitrary", "arbitrary"),
        name="attn_prompt",
    )(rel_bias, far, lam4, bk, sub_gain.T, q, k, v)


def _attn_sample_kernel(rb_ref, lam_ref, bkc_ref, bkn_ref, sg_ref, q_ref, kc_ref, vc_ref, kn_ref, vn_ref,
                        o_ref, bc_s, bn_s, *, t, dh, lam_init, heads):
    dv = 2 * dh

    @pl.when(pl.program_id(0) == 0)
    def _():
        for hh in range(heads):
            bc = _bias_from_buckets(bkc_ref[...], rb_ref, hh) * LOG2E
            bn = _bias_from_buckets(bkn_ref[...], rb_ref, hh) * LOG2E
            bc_s[hh] = jnp.concatenate([bc, bc], axis=0)
            bn_s[hh] = jnp.concatenate([bn, bn], axis=0)

    lam = _lam(lam_ref, lam_init)
    past = vc_ref.shape[0] // heads
    for h in range(heads):
        cols = slice(h * dv, (h + 1) * dv)
        qz = _split_components(q_ref[:, cols], dh)
        sc = lax.dot_general(qz, kc_ref[0, :, cols].astype(BF16), NT_DIMS, preferred_element_type=F32)
        sn = lax.dot_general(qz, kn_ref[:, cols].astype(BF16), NT_DIMS, preferred_element_type=F32)
        sc = sc + bc_s[h]
        sn = sn + bn_s[h]
        m = jnp.maximum(jnp.max(sc, axis=1, keepdims=True), jnp.max(sn, axis=1, keepdims=True))
        pc = jnp.exp2(sc - m)
        pn = jnp.exp2(sn - m)
        l = jnp.sum(pc, axis=1, keepdims=True) + jnp.sum(pn, axis=1, keepdims=True)
        vc = vc_ref[pl.ds(h, past, stride=heads), :]
        acc = (_dot(pc.astype(BF16), vc.astype(BF16))
               + _dot(pn.astype(BF16), vn_ref[:, cols].astype(BF16)))
        o = acc / l
        out = o[:t] - lam * o[t:]
        o_ref[:, cols] = _rms(out, sg_ref[...]) * (1.0 - lam_init)


def _attn_sample(q, k_new, v_new, cache_k, cache_v, rel_bias, lam4, sub_gain, *, batch, t, heads, dh,
                 lam_init):
    n, w = q.shape
    dv = w // heads
    past = cache_k.shape[1]
    assert past % CHUNK == 0 and t <= CHUNK
    qpos = past + jnp.arange(t, dtype=I32)[:, None]
    bkc = rel_bucket(jnp.arange(past, dtype=I32)[None, :] - qpos).astype(I32)
    bkn = rel_bucket(past + jnp.arange(t, dtype=I32)[None, :] - qpos).astype(I32)
    smem = pl.BlockSpec(memory_space=pltpu.SMEM)
    new = pl.BlockSpec((t, w), lambda b: (b, 0))
    old_k = pl.BlockSpec((1, past, w), lambda b: (b, 0, 0))
    old_v = pl.BlockSpec((past * heads, dv), lambda b: (b, 0))
    return pl.pallas_call(
        functools.partial(_attn_sample_kernel, t=t, dh=dh, lam_init=lam_init, heads=heads),
        grid=(batch,),
        in_specs=[smem, _const_spec(lam4.shape), _const_spec(bkc.shape), _const_spec(bkn.shape),
                  _const_spec((1, dv)), new, old_k, old_v, new, new],
        out_specs=new,
        out_shape=jax.ShapeDtypeStruct((n, w), F32),
        scratch_shapes=[pltpu.VMEM((heads, 2 * t, past), F32), pltpu.VMEM((heads, 2 * t, t), F32)],
        compiler_params=_params("arbitrary"),
        name="attn_sample",
    )(rel_bias, lam4, bkc, bkn, sub_gain, q, cache_k, cache_v, k_new, v_new)


def _cumsum_rows(x):
    c = x.shape[0]
    row = lax.broadcasted_iota(I32, x.shape, 0)
    s = 1
    while s < c:
        x = x + jnp.where(row >= s, pltpu.roll(x, s, axis=0), 0.0)
        s *= 2
    return x


def _hgrn_kernel(qh_ref, kb_ref, lf_ref, ih_ref, s0_ref, hg_ref, ob_ref, sf_ref, st_s, *,
                 heads, dk, dv, c, nsb):
    t = pl.program_id(1)

    @pl.when(t == 0)
    def _():
        for h in range(heads):
            st_s[h] = s0_ref[0, h].T

    tb = qh_ref.shape[0]
    row = lax.broadcasted_iota(I32, (c, c), 0)
    col = lax.broadcasted_iota(I32, (c, c), 1)
    causal = col <= row

    def chunk(ci, carry):
        r0 = pl.multiple_of(ci * c, c)
        for h in range(heads):
            rows = pl.ds(r0, c)
            q = qh_ref[rows, h * dk:(h + 1) * dk]
            k = kb_ref[rows, h * dk:(h + 1) * dk]
            v = ih_ref[rows, h * dv:(h + 1) * dv]
            b = _cumsum_rows(lf_ref[rows, h * dk:(h + 1) * dk])
            bl = b[c - 1:c]
            st = st_s[h]
            inter = lax.dot_general((q * jnp.exp(b)).astype(BF16), st.astype(BF16), NT_DIMS,
                                    preferred_element_type=F32)
            qs, ks = [], []
            for j in range(nsb):
                ref = b[j * HG_SUB + HG_SUB // 2:j * HG_SUB + HG_SUB // 2 + 1]
                qs.append(q * jnp.exp(b - ref))
                sub = slice(j * HG_SUB, (j + 1) * HG_SUB)
                ks.append(k[sub] * jnp.exp(ref - b[sub]))
            a_full = lax.dot_general(jnp.concatenate(qs, axis=0).astype(BF16),
                                     jnp.concatenate(ks, axis=0).astype(BF16), NT_DIMS,
                                     preferred_element_type=F32)
            att = jnp.zeros((c, c), F32)
            for j in range(nsb):
                att = jnp.where(col >= j * HG_SUB, a_full[j * c:(j + 1) * c], att)
            att = jnp.where(causal, att, 0.0)
            out = inter + _dot(att.astype(BF16), v.astype(BF16))
            ob_ref[rows, h * dv:(h + 1) * dv] = _rms(out, hg_ref[...])
            kdec = (k * jnp.exp(bl - b)).astype(BF16)
            st_s[h] = jnp.exp(bl) * st + lax.dot_general(v.astype(BF16), kdec, TN_DIMS,
                                                         preferred_element_type=F32)
        return carry

    lax.fori_loop(0, tb // c, chunk, 0, unroll=min(8, tb // c))

    @pl.when(t == pl.num_programs(1) - 1)
    def _():
        for h in range(heads):
            sf_ref[0, h] = st_s[h].T


def _hgrn(qh, kb, lf, ih, s0, hg_gain, *, batch, seq, heads, dk, dv, tb):
    n = qh.shape[0]
    c = min(CHUNK, seq)
    assert seq % tb == 0 and tb % c == 0 and c % HG_SUB == 0
    nt = seq // tb
    blk = lambda w: pl.BlockSpec((tb, w), lambda b, t: (b * nt + t, 0))
    st = pl.BlockSpec((1, heads, dk, dv), lambda b, t: (b, 0, 0, 0))
    return pl.pallas_call(
        functools.partial(_hgrn_kernel, heads=heads, dk=dk, dv=dv, c=c, nsb=c // HG_SUB),
        grid=(batch, nt),
        in_specs=[blk(heads * dk), blk(heads * dk), blk(heads * dk), blk(heads * dv), st,
                  _const_spec((1, dv))],
        out_specs=[blk(heads * dv), st],
        out_shape=[jax.ShapeDtypeStruct((n, heads * dv), F32),
                   jax.ShapeDtypeStruct((batch, heads, dk, dv), F32)],
        scratch_shapes=[pltpu.VMEM((heads, dv, dk), F32)],
        compiler_params=_params("arbitrary", "arbitrary"),
        name="hgrn",
    )(qh, kb, lf, ih, s0, hg_gain)


def _store_token_tiles(ref, x):
    rows, d = x.shape
    s = d // LANES
    for c in range(s):
        ref[pl.ds(c, rows, stride=s), :] = x[:, c * LANES:(c + 1) * LANES]


def _load_token_tiles(ref, rows, s):
    return jnp.concatenate([ref[pl.ds(c, rows, stride=s), :] for c in range(s)], axis=1)


def _token_tile(r, s):
    return pl.ds(pl.multiple_of(r * s, s), s)


def _ffn_prologue(x1, gf_ref, wrh_ref, wrl_ref, rbias_ref, x1_ref, xn_ref, lg_ref):
    x1_ref[...] = x1
    xn = _rms(x1, gf_ref[...])
    _store_token_tiles(xn_ref, xn)
    hi = xn.astype(BF16)
    lo = (xn - hi.astype(F32)).astype(BF16)
    nt = functools.partial(lax.dot_general, dimension_numbers=NT_DIMS, preferred_element_type=F32)
    lg_ref[...] = nt(wrh_ref[...], hi) + nt(wrh_ref[...], lo) + nt(wrl_ref[...], hi) + rbias_ref[...]


def _out_even_kernel(oa_ref, ob_ref, gs_ref, x_ref, w_ref, gf_ref, wrh_ref, wrl_ref, rbias_ref,
                     x1_ref, xn_ref, lg_ref):
    o = jnp.concatenate([oa_ref[...], ob_ref[...] * gs_ref[...]], axis=1).astype(BF16)
    x1 = x_ref[...] + _dot(o, w_ref[...])
    _ffn_prologue(x1, gf_ref, wrh_ref, wrl_ref, rbias_ref, x1_ref, xn_ref, lg_ref)


def _out_even(oa, ob, gs, x, w_bf, g_ffn, wrh, wrl, rbias, *, tm):
    n, d = x.shape
    w = oa.shape[1]
    nr = wrh.shape[0]
    row = lambda i: (i, 0)
    return pl.pallas_call(
        _out_even_kernel,
        grid=(n // tm,),
        in_specs=[pl.BlockSpec((tm, w), row), pl.BlockSpec((tm, w), row), pl.BlockSpec((tm, w), row),
                  pl.BlockSpec((tm, d), row), _const_spec(w_bf.shape), _const_spec((1, d)),
                  _const_spec(wrh.shape), _const_spec(wrl.shape), _const_spec(rbias.shape)],
        out_specs=[pl.BlockSpec((tm, d), row), pl.BlockSpec((tm * (d // LANES), LANES), row),
                   pl.BlockSpec((nr, tm), lambda i: (0, i))],
        out_shape=[jax.ShapeDtypeStruct((n, d), F32), jax.ShapeDtypeStruct((n * (d // LANES), LANES), F32),
                   jax.ShapeDtypeStruct((nr, n), F32)],
        compiler_params=_params("arbitrary"),
        name="out_even",
    )(oa, ob, gs, x, w_bf, g_ffn, wrh, wrl, rbias)


def _gelu(x):
    return 0.5 * x * (1.0 + jnp.tanh(math.sqrt(2.0 / math.pi) * (x + 0.044715 * (x * x * x))))


def _odd_kernel(x_ref, gm_ref, win_ref, vg_ref, wsp_ref, bsp_ref, wout_ref, gf_ref, wrh_ref, wrl_ref,
                rbias_ref, x1_ref, xn_ref, lg_ref, *rest, half, groups, l, emit_v):
    if emit_v:
        vn_ref, u_s, s_s = rest
    else:
        vn_ref = None
        u_s, vn_s, s_s = rest
    tm = x_ref.shape[0]
    x = x_ref[...]
    xn = _rms(x, gm_ref[...]).astype(BF16)
    cw = 512
    vbuf = vn_ref if emit_v else vn_s
    for cidx in range(half // cw):
        u_s[:, cidx * cw:(cidx + 1) * cw] = _gelu(_dot(xn, win_ref[:, cidx * cw:(cidx + 1) * cw]))
        vbuf[:, cidx * cw:(cidx + 1) * cw] = _gelu(
            _dot(xn, win_ref[:, half + cidx * cw:half + (cidx + 1) * cw]))
    vbuf[...] = _rms(vbuf[...], vg_ref[...])
    gw = half // groups
    row = lax.broadcasted_iota(I32, (l, l), 0)
    col = lax.broadcasted_iota(I32, (l, l), 1)
    for g in range(groups):
        wg = jnp.where(col <= row, wsp_ref[g], 0.0).astype(BF16)
        bg = bsp_ref[:, g:g + 1]
        for ci in range(tm // l):
            vv = vbuf[ci * l:(ci + 1) * l, g * gw:(g + 1) * gw].astype(BF16)
            s_s[ci * l:(ci + 1) * l, g * gw:(g + 1) * gw] = _dot(wg, vv) + bg
    y = _dot((u_s[...] * s_s[...]).astype(BF16), wout_ref[...])
    _ffn_prologue(x + y, gf_ref, wrh_ref, wrl_ref, rbias_ref, x1_ref, xn_ref, lg_ref)


def _odd_mixer(x, g_mix, win_bf, v_gain, wsp, bsp_t, wout_bf, g_ffn, wrh, wrl, rbias, *, l, tm, emit_v):
    n, d = x.shape
    half = wout_bf.shape[0]
    groups = wsp.shape[0]
    nr = wrh.shape[0]
    assert n % tm == 0 and tm % l == 0
    row = lambda i: (i, 0)
    out_specs = [pl.BlockSpec((tm, d), row), pl.BlockSpec((tm * (d // LANES), LANES), row),
                 pl.BlockSpec((nr, tm), lambda i: (0, i))]
    out_shape = [jax.ShapeDtypeStruct((n, d), F32), jax.ShapeDtypeStruct((n * (d // LANES), LANES), F32),
                 jax.ShapeDtypeStruct((nr, n), F32)]
    scratch = [pltpu.VMEM((tm, half), F32)]
    if emit_v:
        out_specs.append(pl.BlockSpec((tm, half), row))
        out_shape.append(jax.ShapeDtypeStruct((n, half), F32))
    else:
        scratch.append(pltpu.VMEM((tm, half), F32))
    scratch.append(pltpu.VMEM((tm, half), F32))
    return pl.pallas_call(
        functools.partial(_odd_kernel, half=half, groups=groups, l=l, emit_v=emit_v),
        grid=(n // tm,),
        in_specs=[pl.BlockSpec((tm, d), row), _const_spec((1, d)), _const_spec(win_bf.shape),
                  _const_spec((1, half)), _const_spec(wsp.shape), _const_spec(bsp_t.shape),
                  _const_spec(wout_bf.shape), _const_spec((1, d)), _const_spec(wrh.shape),
                  _const_spec(wrl.shape), _const_spec(rbias.shape)],
        out_specs=out_specs,
        out_shape=out_shape,
        scratch_shapes=scratch,
        compiler_params=_params("arbitrary"),
        name="odd_mixer",
    )(x, g_mix, win_bf, v_gain, wsp, bsp_t, wout_bf, g_ffn, wrh, wrl, rbias)


def _route_kernel(lg_ref, tri_ref, init_ref, e_ref, g_ref, r_ref, p_ref, cnt_ref, base_ref, tcnt_ref, run_s,
                  *, groups, epg):
    i = pl.program_id(0)

    @pl.when(i == 0)
    def _():
        run_s[...] = init_ref[...]

    lg = lg_ref[...]
    tr = lg.shape[1]
    gl = [lg[g:g + 1] for g in range(groups)]
    m = functools.reduce(jnp.maximum, gl)
    grp = jnp.full((1, tr), groups - 1, I32)
    for g in range(groups - 2, -1, -1):
        grp = jnp.where(gl[g] == m, g, grp)
    gate_g = 1.0 / functools.reduce(lambda a, b: a + b, [jnp.exp(x - m) for x in gl])
    sel = lg[SUBLANES + (groups - 1) * epg:SUBLANES + groups * epg]
    for g in range(groups - 2, -1, -1):
        sel = jnp.where(grp == g, lg[SUBLANES + g * epg:SUBLANES + (g + 1) * epg], sel)
    sub = lax.broadcasted_iota(I32, sel.shape, 0)
    v1 = jnp.max(sel, axis=0, keepdims=True)
    i1 = jnp.min(jnp.where(sel == v1, sub, epg), axis=0, keepdims=True)
    sel2 = jnp.where(sub == i1, -jnp.inf, sel)
    v2 = jnp.max(sel2, axis=0, keepdims=True)
    i2 = jnp.min(jnp.where(sel2 == v2, sub, epg), axis=0, keepdims=True)
    tt = jnp.exp(v2 - v1)
    g1 = gate_g / (1.0 + tt)
    g2 = gate_g * tt / (1.0 + tt)
    e1 = grp * epg + i1
    e2 = grp * epg + i2
    ne = groups * epg
    eidx = lax.broadcasted_iota(I32, (ne, tr), 0)
    oh1 = eidx == e1
    oh2 = eidx == e2
    cnt = jnp.where(oh1, 1.0, 0.0) + jnp.where(oh2, 1.0, 0.0)
    local = _dot(cnt.astype(BF16), tri_ref[...])
    before = run_s[:, 0:1] + local
    r1 = jnp.sum(jnp.where(oh1, before, 0.0), axis=0, keepdims=True)
    r2 = jnp.sum(jnp.where(oh2, before, 0.0), axis=0, keepdims=True)
    tile_cnt = jnp.broadcast_to(jnp.sum(cnt, axis=1, keepdims=True), run_s.shape)
    padded = jnp.ceil(tile_cnt * (1.0 / RUN_CHUNK)) * RUN_CHUNK
    offset = _cumsum_rows(padded) - padded
    where_local = offset[:, 0:1] + local
    p1 = jnp.sum(jnp.where(oh1, where_local, 0.0), axis=0, keepdims=True)
    p2 = jnp.sum(jnp.where(oh2, where_local, 0.0), axis=0, keepdims=True)
    base_ref[...] = run_s[...]
    tcnt_ref[...] = tile_cnt
    run_s[...] = run_s[...] + tile_cnt
    rows = lax.broadcasted_iota(I32, (SUBLANES, tr), 0)
    e_ref[...] = jnp.where(rows == 0, e1, jnp.where(rows == 1, e2, 0))
    g_ref[...] = jnp.where(rows == 0, g1, jnp.where(rows == 1, g2, 0.0))
    r_ref[...] = jnp.where(rows == 0, r1, jnp.where(rows == 1, r2, 0.0)).astype(I32)
    p_ref[...] = jnp.where(rows == 0, p1, jnp.where(rows == 1, p2, 0.0)).astype(I32)
    cnt_ref[...] = run_s[...]


def _route(lgt, init, *, groups, epg, tr):
    nr, n = lgt.shape
    assert n % tr == 0 and nr == SUBLANES + groups * epg
    ne = groups * epg
    tri = (jnp.arange(tr)[:, None] < jnp.arange(tr)[None, :]).astype(BF16)
    tok = pl.BlockSpec((SUBLANES, tr), lambda i: (0, i))
    per_tile = pl.BlockSpec((ne, LANES), lambda i: (i, 0))
    tile_tab = jax.ShapeDtypeStruct((n // tr * ne, LANES), F32)
    return pl.pallas_call(
        functools.partial(_route_kernel, groups=groups, epg=epg),
        grid=(n // tr,),
        in_specs=[pl.BlockSpec((nr, tr), lambda i: (0, i)), _const_spec((tr, tr)), _const_spec((ne, LANES))],
        out_specs=[tok, tok, tok, tok, pl.BlockSpec((ne, LANES), lambda i: (0, 0)), per_tile, per_tile],
        out_shape=[jax.ShapeDtypeStruct((SUBLANES, n), I32), jax.ShapeDtypeStruct((SUBLANES, n), F32),
                   jax.ShapeDtypeStruct((SUBLANES, n), I32), jax.ShapeDtypeStruct((SUBLANES, n), I32),
                   jax.ShapeDtypeStruct((ne, LANES), F32), tile_tab, tile_tab],
        scratch_shapes=[pltpu.VMEM((ne, LANES), F32)],
        compiler_params=_params("arbitrary"),
        name="route",
    )(lgt, tri, init)


ISSUE_UNROLL = 16
DISPATCH_SLOTS = 3


def _dispatch_kernel(seg_ref, dst_ref, *rest, s, bm, tiles):
    x_refs = rest[:len(tiles)]
    buf_ref, zero_s, xin_s, sem, isem, zsem = rest[len(tiles):]
    tp = dst_ref.shape[2]
    blk = bm * s

    def zero_fill():
        zero_s[...] = jnp.zeros(zero_s.shape, F32)

        def block_copy(b):
            return pltpu.make_async_copy(zero_s, buf_ref.at[pl.ds(pl.multiple_of(b * blk, blk), blk)], zsem)

        for e in range(seg_ref.shape[1]):
            @pl.when(seg_ref[1, e] > 0)
            def _():
                block_copy(seg_ref[0, e] // bm - 1).start()

            @pl.when(seg_ref[1, e] - seg_ref[2, e] > bm)
            def _():
                block_copy(seg_ref[0, e] // bm - 2).start()
        for e in range(seg_ref.shape[1]):
            @pl.when(seg_ref[1, e] > 0)
            def _():
                block_copy(0).wait()

            @pl.when(seg_ref[1, e] - seg_ref[2, e] > bm)
            def _():
                block_copy(0).wait()

        def tail_start(b, c):
            block_copy(b).start()
            return c

        def tail_wait(b, c):
            block_copy(b).wait()
            return c

        first_unused = seg_ref[0, seg_ref.shape[1] - 1] // bm
        n_blocks = buf_ref.shape[0] // blk
        lax.fori_loop(first_unused, n_blocks, tail_start, 0)
        lax.fori_loop(first_unused, n_blocks, tail_wait, 0)

    pl.when(pl.program_id(0) == 0)(zero_fill)

    i = pl.program_id(0)
    last = pl.num_programs(0) - 1
    rows = tp * s

    def tile_in(t, go):
        lo = 0
        for x_ref, nt in zip(x_refs, tiles):
            @pl.when((t >= lo) & (t < lo + nt))
            def _(x_ref=x_ref, lo=lo):
                cp = pltpu.make_async_copy(x_ref.at[pl.ds(pl.multiple_of((t - lo) * rows, rows), rows)],
                                           xin_s.at[t % DISPATCH_SLOTS], isem.at[t % DISPATCH_SLOTS])
                cp.start() if go else cp.wait()
            lo += nt

    def rows_out_wait(t):
        for kk in range(TOP_K):
            pltpu.make_async_copy(xin_s.at[t % DISPATCH_SLOTS], buf_ref.at[pl.ds(0, rows)],
                                  sem.at[t % DISPATCH_SLOTS]).wait()

    @pl.when(i == 0)
    def _():
        tile_in(i, True)

    @pl.when(i >= DISPATCH_SLOTS - 1)
    def _():
        rows_out_wait(i - (DISPATCH_SLOTS - 1))

    @pl.when(i < last)
    def _():
        tile_in(i + 1, True)

    tile_in(i, False)
    slot = i % DISPATCH_SLOTS

    def issue(r0, c):
        for u in range(ISSUE_UNROLL):
            r = r0 * ISSUE_UNROLL + u
            for kk in range(TOP_K):
                pltpu.make_async_copy(xin_s.at[slot, _token_tile(r, s)],
                                      buf_ref.at[_token_tile(dst_ref[0, kk, r], s)], sem.at[slot]
                                      ).start(priority=kk % 2)
        return c

    lax.fori_loop(0, tp // ISSUE_UNROLL, issue, 0)

    @pl.when(i == last)
    def _():
        for back in range(DISPATCH_SLOTS - 2, -1, -1):
            @pl.when(i >= back)
            def _():
                rows_out_wait(i - back)


def _dispatch(seg, dest3, xns, n_slots, *, tp, s, bm):
    assert tp % ISSUE_UNROLL == 0 and all(x.shape[0] % (tp * s) == 0 for x in xns)
    tiles = tuple(x.shape[0] // (tp * s) for x in xns)
    grid_spec = pltpu.PrefetchScalarGridSpec(
        num_scalar_prefetch=1,
        grid=(sum(tiles),),
        in_specs=[pl.BlockSpec((1, TOP_K, tp), lambda i, sg: (i, 0, 0), memory_space=pltpu.SMEM)]
        + [pl.BlockSpec(memory_space=pl.ANY)] * len(xns),
        out_specs=pl.BlockSpec(memory_space=pl.ANY),
        scratch_shapes=[pltpu.VMEM((bm * s, LANES), F32), pltpu.VMEM((DISPATCH_SLOTS, tp * s, LANES), F32),
                        pltpu.SemaphoreType.DMA((DISPATCH_SLOTS,)), pltpu.SemaphoreType.DMA((DISPATCH_SLOTS,)),
                        pltpu.SemaphoreType.DMA(())],
    )
    return pl.pallas_call(
        functools.partial(_dispatch_kernel, s=s, bm=bm, tiles=tiles),
        grid_spec=grid_spec,
        out_shape=jax.ShapeDtypeStruct((n_slots * s, LANES), F32),
        compiler_params=_params("arbitrary"),
        name="dispatch",
    )(seg, dest3, *xns)


def _expert_kernel(be_ref, nu_ref, x_ref, wg_ref, wu_ref, wd_ref, o_ref, wg_s, wu_s, wd_s):
    b = pl.program_id(0)
    s = wg_s.shape[0] // LANES
    prev = be_ref[jnp.maximum(b - 1, 0)]

    @pl.when((b == 0) | (be_ref[b] != prev))
    def _():
        wg_s[...] = wg_ref[0, 0].astype(BF16)
        wu_s[...] = wu_ref[0, 0].astype(BF16)
        wd_s[...] = wd_ref[0, 0].astype(BF16)

    @pl.when(b < nu_ref[0])
    def _():
        xb = _load_token_tiles(x_ref, x_ref.shape[0] // s, s).astype(BF16)
        gate = _dot(xb, wg_s[...])
        h = gate * _sigmoid(gate) * _dot(xb, wu_s[...])
        _store_token_tiles(o_ref, _dot(h.astype(BF16), wd_s[...]))

    @pl.when(b >= nu_ref[0])
    def _():
        o_ref[...] = jnp.zeros(o_ref.shape, F32)


def _experts(blk_expert, n_used, buf, w_gate, w_up, w_down, *, layer, bm):
    d, de = w_gate.shape[2:]
    s = d // LANES
    n_blocks = buf.shape[0] // (bm * s)
    rows = pl.BlockSpec((bm * s, LANES), lambda b, be, nu: (b, 0))
    used_rows = pl.BlockSpec((bm * s, LANES), lambda b, be, nu: (jnp.minimum(b, nu[0] - 1), 0))
    grid_spec = pltpu.PrefetchScalarGridSpec(
        num_scalar_prefetch=2,
        grid=(n_blocks,),
        in_specs=[used_rows,
                  pl.BlockSpec((1, 1, d, de), lambda b, be, nu: (layer, be[b], 0, 0)),
                  pl.BlockSpec((1, 1, d, de), lambda b, be, nu: (layer, be[b], 0, 0)),
                  pl.BlockSpec((1, 1, de, d), lambda b, be, nu: (layer, be[b], 0, 0))],
        out_specs=rows,
        scratch_shapes=[pltpu.VMEM((d, de), BF16), pltpu.VMEM((d, de), BF16), pltpu.VMEM((de, d), BF16)],
    )
    return pl.pallas_call(
        _expert_kernel,
        grid_spec=grid_spec,
        out_shape=jax.ShapeDtypeStruct(buf.shape, F32),
        compiler_params=_params("arbitrary"),
        name="experts",
    )(blk_expert, n_used, buf, w_gate, w_up, w_down)


def _combine_kernel(tab_ref, nxt_ref, pos_ref, gate_ref, x_ref, yb_ref, o_ref, stg_s, tt_s, sem, *, ahead):
    i = pl.program_id(0)
    tq, d = x_ref.shape
    s = d // LANES
    piece = RUN_CHUNK * s
    slot = i % 2

    def run_copies(ref, dst_slot, go):
        for e in range(ref.shape[2]):
            def body(j, c):
                src = pl.multiple_of((ref[0, 0, e] + j * RUN_CHUNK) * s, s)
                dst = pl.multiple_of((ref[0, 2, e] + j * RUN_CHUNK) * s, piece)
                cp = pltpu.make_async_copy(yb_ref.at[pl.ds(src, piece)],
                                           stg_s.at[dst_slot, pl.ds(dst, piece)], sem.at[dst_slot])
                cp.start() if go else cp.wait()
                return c
            lax.fori_loop(0, ref[0, 1, e], body, 0)

    if ahead:
        @pl.when(i == 0)
        def _():
            run_copies(tab_ref, 0, True)

        @pl.when(i + 1 < pl.num_programs(0))
        def _():
            run_copies(nxt_ref, 1 - slot, True)
    else:
        run_copies(tab_ref, slot, True)

    run_copies(tab_ref, slot, False)

    def assemble(r0, c):
        for u in range(ISSUE_UNROLL):
            r = r0 * ISSUE_UNROLL + u
            acc = gate_ref[0, 0, r] * stg_s[slot, _token_tile(pos_ref[0, 0, r], s), :]
            for kk in range(1, TOP_K):
                acc = acc + gate_ref[0, kk, r] * stg_s[slot, _token_tile(pos_ref[0, kk, r], s), :]
            tt_s[_token_tile(r, s), :] = acc
        return c

    lax.fori_loop(0, tq // ISSUE_UNROLL, assemble, 0)
    o_ref[...] = x_ref[...] + _load_token_tiles(tt_s, tq, s)


def _combine(tab, pos3, gates3, x1, yb, *, tq):
    n, d = x1.shape
    s = d // LANES
    nt = n // tq
    ne = tab.shape[2]
    assert tq % ISSUE_UNROLL == 0
    stage_rows = tq * TOP_K + ne * RUN_CHUNK
    smem = lambda shape, imap: pl.BlockSpec(shape, imap, memory_space=pltpu.SMEM)
    return pl.pallas_call(
        functools.partial(_combine_kernel, ahead=nt > 2),
        grid=(nt,),
        in_specs=[smem((1, 3, ne), lambda i: (i, 0, 0)),
                  smem((1, 3, ne), lambda i: (jnp.minimum(i + 1, nt - 1), 0, 0)),
                  smem((1, TOP_K, tq), lambda i: (i, 0, 0)),
                  smem((1, TOP_K, tq), lambda i: (i, 0, 0)),
                  pl.BlockSpec((tq, d), lambda i: (i, 0)),
                  pl.BlockSpec(memory_space=pl.ANY)],
        out_specs=pl.BlockSpec((tq, d), lambda i: (i, 0)),
        out_shape=jax.ShapeDtypeStruct((n, d), F32),
        scratch_shapes=[pltpu.VMEM((2, stage_rows * s, LANES), F32), pltpu.VMEM((tq * s, LANES), F32),
                        pltpu.SemaphoreType.DMA((2,))],
        compiler_params=_params("arbitrary"),
        name="combine",
    )(tab, tab, pos3, gates3, x1, yb)


def _moe(streams, w_gate, w_up, w_down, *, layer, groups, epg):
    d = streams[0][0].shape[1]
    ne = groups * epg
    routed = []
    taken = jnp.zeros((ne, LANES), F32)
    for x1, _, lgt in streams:
        rtile = min(MOE_TILE, x1.shape[0])
        routed.append(_route(lgt, taken, groups=groups, epg=epg, tr=rtile) + (rtile,))
        taken = routed[-1][4]
    n_rows = sum(x1.shape[0] for x1, _, _ in streams) * TOP_K
    counts = taken[:, 0].astype(I32)
    bm = MOE_BLOCK if n_rows >= 2 * ne * MOE_BLOCK else MOE_BLOCK_SMALL
    padded = (counts + RUN_CHUNK + bm - 1) // bm * bm
    pend = jnp.cumsum(padded)
    pstart = pend - padded
    n_blocks = -(-(n_rows + ne * RUN_CHUNK) // bm) + ne
    eids = jnp.arange(ne, dtype=I32)
    blk_row = jnp.arange(n_blocks, dtype=I32)[:, None] * bm
    blk_expert = jnp.minimum(jnp.sum((pend[None, :] <= blk_row).astype(I32), axis=1), ne - 1)
    n_used = (pend[-1:] // bm).astype(I32)
    seg_tab = jnp.stack([pend, padded, counts]).astype(I32)
    dtile = min(r[-1] for r in routed)
    dests = []
    for (x1, _, _), (e8, g8, r8, p8, cnt, base, tcnt, rtile) in zip(streams, routed):
        seg = jnp.sum(jnp.where(e8[:TOP_K, :, None] == eids, pstart, 0), axis=-1)
        dest = seg + r8[:TOP_K]
        dests.append(dest.reshape(TOP_K, x1.shape[0] // dtile, dtile).transpose(1, 0, 2))
    buf = _dispatch(seg_tab, jnp.concatenate(dests, axis=0), [xn for _, xn, _ in streams], n_blocks * bm,
                    tp=dtile, s=d // LANES, bm=bm)
    yb = _experts(blk_expert, n_used, buf, w_gate, w_up, w_down, layer=layer, bm=bm)
    outs = []
    for (x1, _, _), (e8, g8, r8, p8, cnt, base, tcnt, rtile) in zip(streams, routed):
        nrt = x1.shape[0] // rtile
        run_start = pstart[None, :] + base.reshape(nrt, ne, LANES)[:, :, 0].astype(I32)
        pieces = (tcnt.reshape(nrt, ne, LANES)[:, :, 0].astype(I32) + RUN_CHUNK - 1) // RUN_CHUNK
        stage = (jnp.cumsum(pieces, axis=1) - pieces) * RUN_CHUNK
        tab = jnp.stack([run_start, pieces, stage], axis=1)
        by_tile = lambda a: a[:TOP_K].reshape(TOP_K, nrt, rtile).transpose(1, 0, 2)
        outs.append(_combine(tab, by_tile(p8), by_tile(g8), x1, yb, tq=rtile))
    return outs


def _router_weights(wg, bg, we, be):
    d, groups = wg.shape
    epg = we.shape[2]
    assert groups <= SUBLANES and epg == SUBLANES
    pad = jnp.zeros((SUBLANES - groups, d), F32)
    wr = jnp.concatenate([wg.T, pad, we.transpose(0, 2, 1).reshape(groups * epg, d)], axis=0)
    rb = jnp.concatenate([bg, jnp.zeros((SUBLANES - groups,), F32), be.reshape(-1)])[:, None]
    hi = wr.astype(BF16)
    lo = (wr - hi.astype(F32)).astype(BF16)
    return hi, lo, rb, groups, epg


def kernel(x_prompt, x_sample, cache_attn_k, cache_attn_v, state_hgrn, rel_bias, norm_mix, norm_ffn,
           w_in_even, w_out_even, q_norm_gain, k_norm_gain, lam_q1, lam_k1, lam_q2, lam_k2, da_out_gain,
           hgrn_lb_logits, hgrn_out_gain, w_in_odd, sgu_v_gain, sgu_w, sgu_b, w_out_odd,
           router_group_w, router_group_b, router_expert_w, router_expert_b,
           expert_w_gate, expert_w_up, expert_w_down):
    bp, tp, d = x_prompt.shape
    bs, ts, _ = x_sample.shape
    depth = norm_mix.shape[0]
    _, _, past, da_heads, _, da_dh = cache_attn_k.shape
    da_dv = cache_attn_v.shape[-1]
    _, _, hg_heads, hg_dk, hg_dv = state_hgrn.shape
    width = da_heads * da_dv
    assert width == da_heads * 2 * da_dh == hg_heads * hg_dk == hg_heads * hg_dv
    assert da_dv == LANES and hg_dk == LANES and hg_dv == LANES

    lb_all = jnp.cumsum(jax.nn.softmax(hgrn_lb_logits.astype(F32), axis=0), axis=0)
    gid = jnp.arange(width) // da_dh
    pm = jnp.where(gid[:, None] == gid[None, :], 1.0 / da_dh, 0.0).astype(BF16)

    xs = {"p": x_prompt.reshape(bp * tp, d), "s": x_sample.reshape(bs * ts, d)}
    dims = {"p": (bp, tp), "s": (bs, ts)}
    outs = {"p": {}, "s": {}}
    kp_l, vp_l, ks_l, vs_l, sp_l, ss_l, sgu_l = [], [], [], [], [], [], []

    for layer in range(depth):
        j = layer // 2
        wrh, wrl, rbias, groups, epg = _router_weights(
            router_group_w[layer], router_group_b[layer], router_expert_w[layer], router_expert_b[layer])
        g_mix = norm_mix[layer][None, :]
        g_ffn = norm_ffn[layer][None, :]
        if layer % 2 == 0:
            lam_init = 0.8 - 0.6 * math.exp(-0.3 * layer)
            w_in_bf = w_in_even[j].astype(BF16)
            w_out_bf = w_out_even[j].astype(BF16)
            reps = width // da_dh
            qg = jnp.tile(q_norm_gain[j], reps)[None, :]
            kg = jnp.tile(k_norm_gain[j], reps)[None, :]
            lam4 = jnp.stack([lam_q1[j], lam_k1[j], lam_q2[j], lam_k2[j]])
            sub_gain = da_out_gain[j][None, :]
            hg_gain = hgrn_out_gain[j][None, :]
            lb = lb_all[j][None, :]
            for key in ("p", "s"):
                b, t = dims[key]
                x = xs[key]
                q, k, v, qh, kb, lf, ih, gs, k5, v4 = _in_even(
                    x, g_mix, w_in_bf, pm, qg, kg, lb, width=width, q_scale=da_dh ** -0.5 * LOG2E,
                    tm=min(512, b * t), batch=b, seq=t, heads=da_heads, dh=da_dh)
                if key == "p":
                    oa = _attn_prompt(q, k, v, rel_bias, lam4, sub_gain, batch=b, seq=t, heads=da_heads,
                                      dh=da_dh, lam_init=lam_init, qt=min(256, t))
                    s0 = jnp.zeros((b, hg_heads, hg_dk, hg_dv), F32)
                    ob, s_new = _hgrn(qh, kb, lf, ih, s0, hg_gain, batch=b, seq=t, heads=hg_heads,
                                      dk=hg_dk, dv=hg_dv, tb=min(512, t))
                    kp_l.append(k5)
                    vp_l.append(v4)
                    sp_l.append(s_new)
                else:
                    ck = cache_attn_k[j].reshape(b, past, width)
                    cv = cache_attn_v[j].reshape(b * past * da_heads, da_dv)
                    oa = _attn_sample(q, k, v, ck, cv, rel_bias, lam4, sub_gain, batch=b, t=t,
                                      heads=da_heads, dh=da_dh, lam_init=lam_init)
                    ob, s_new = _hgrn(qh, kb, lf, ih, state_hgrn[j], hg_gain, batch=b, seq=t,
                                      heads=hg_heads, dk=hg_dk, dv=hg_dv, tb=t)
                    ks_l.append(k5)
                    vs_l.append(v4)
                    ss_l.append(s_new)
                outs[key] = _out_even(oa, ob, gs, x, w_out_bf, g_ffn, wrh, wrl, rbias, tm=min(512, b * t))
        else:
            w_in_bf = w_in_odd[j].astype(BF16)
            w_out_bf = w_out_odd[j].astype(BF16)
            v_gain = sgu_v_gain[j][None, :]
            for key in ("p", "s"):
                b, t = dims[key]
                l = min(SGU_CHUNK, t)
                res = _odd_mixer(xs[key], g_mix, w_in_bf, v_gain, sgu_w[j][:, :l, :l], sgu_b[j][:, :l].T,
                                 w_out_bf, g_ffn, wrh, wrl, rbias, l=l, tm=min(512, b * t), emit_v=(key == "s"))
                outs[key] = res[:3]
                if key == "s":
                    sgu_l.append(res[3].reshape(b, t, -1))
        xs["p"], xs["s"] = _moe([outs["p"], outs["s"]], expert_w_gate, expert_w_up, expert_w_down,
                                layer=layer, groups=groups, epg=epg)

    return (xs["p"].reshape(bp, tp, d), xs["s"].reshape(bs, ts, d), jnp.stack(kp_l), jnp.stack(vp_l),
            jnp.stack(ks_l), jnp.stack(vs_l), jnp.stack(sp_l), jnp.stack(ss_l), jnp.stack(sgu_l))
```

```python
import functools
import math

import jax
import jax.numpy as jnp
from jax import lax
from jax.experimental import pallas as pl
from jax.experimental.pallas import tpu as pltpu

F32 = jnp.float32
BF16 = jnp.bfloat16
I32 = jnp.int32

EPS = 1e-6
LOG2E = math.log2(math.e)
CHUNK = 64
SGU_CHUNK = 128
REL_BUCKETS = 32
REL_MAX_DIST = 128
TOP_K = 2
MOE_BLOCK = 512
MOE_BLOCK_SMALL = 128
MOE_TILE = 1024
RUN_CHUNK = 32
HG_SUB = 16

LANES = 128
SUBLANES = 8
VMEM_LIMIT = 56 * 1024 * 1024

NT_DIMS = (((1,), (1,)), ((), ()))
TN_DIMS = (((0,), (0,)), ((), ()))


def _params(*sem):
    return pltpu.CompilerParams(dimension_semantics=sem, vmem_limit_bytes=VMEM_LIMIT)


def _const_spec(shape):
    nd = len(shape)
    return pl.BlockSpec(shape, lambda *_: (0,) * nd, pipeline_mode=pl.Buffered(1))


def _sigmoid(x):
    return 1.0 / (1.0 + jnp.exp(-x))


def _rms(x, g):
    return x * lax.rsqrt(jnp.mean(x * x, axis=-1, keepdims=True) + EPS) * g


def _dot(a, b):
    return jnp.dot(a, b, preferred_element_type=F32)


def rel_bucket(rel):
    half = REL_BUCKETS // 2
    max_exact = half // 2
    ret = (rel > 0).astype(I32) * half
    n = jnp.abs(rel)
    nf = jnp.maximum(n, 1).astype(F32)
    large = max_exact + (jnp.log(nf / max_exact) / math.log(REL_MAX_DIST / max_exact)
                         * (half - max_exact)).astype(I32)
    large = jnp.minimum(large, half - 1)
    return ret + jnp.where(n < max_exact, n, large)


def _in_even_kernel(x_ref, g_ref, w_ref, pm_ref, qg_ref, kg_ref, lb_ref,
                    q_ref, k_ref, v_ref, qh_ref, kb_ref, lf_ref, ih_ref, gs_ref, k5_ref, v4_ref, *, width,
                    q_scale):
    xn = _rms(x_ref[...], g_ref[...]).astype(BF16)

    def proj(c):
        return _dot(xn, w_ref[:, c * width:(c + 1) * width])

    def group_norm(y, gain):
        ms = _dot((y * y).astype(BF16), pm_ref[...])
        return y * lax.rsqrt(ms + EPS) * gain

    q_ref[...] = group_norm(proj(0), qg_ref[...]) * q_scale
    kn = group_norm(proj(1), kg_ref[...])
    k_ref[...] = kn
    bb, ts, heads, _, dh = k5_ref.shape
    for h in range(heads):
        for c in range(2):
            piece = kn[:, (2 * h + c) * dh:(2 * h + c + 1) * dh]
            k5_ref[:, :, h, c, :] = piece.reshape(bb, ts, dh)
    vv = proj(2)
    v_ref[...] = vv
    for h in range(heads):
        v4_ref[:, :, h, :] = vv[:, h * 2 * dh:(h + 1) * 2 * dh].reshape(bb, ts, 2 * dh)
    yq = proj(3)
    qh_ref[...] = yq * _sigmoid(yq)
    zf = proj(4)
    lb = lb_ref[...]
    lf_ref[...] = jnp.log(lb + (1.0 - lb) * _sigmoid(zf))
    kb_ref[...] = (1.0 - lb) * _sigmoid(-zf)
    ih_ref[...] = proj(5)
    yg = proj(6)
    gs_ref[...] = yg * _sigmoid(yg)


def _in_even(x, g_mix, w_bf, pm, qg, kg, lb, *, width, q_scale, tm, batch, seq, heads, dh):
    n, d = x.shape
    assert n % tm == 0 and (tm % seq == 0 or seq % tm == 0)
    row = lambda i: (i, 0)
    out = jax.ShapeDtypeStruct((n, width), F32)
    bb, ts = max(tm // seq, 1), min(tm, seq)
    per = seq // ts
    k5_spec = pl.BlockSpec((bb, ts, heads, 2, dh), lambda i: (i // per, i % per, 0, 0, 0))
    v4_spec = pl.BlockSpec((bb, ts, heads, 2 * dh), lambda i: (i // per, i % per, 0, 0))
    return pl.pallas_call(
        functools.partial(_in_even_kernel, width=width, q_scale=q_scale),
        grid=(n // tm,),
        in_specs=[pl.BlockSpec((tm, d), row), _const_spec((1, d)), _const_spec(w_bf.shape),
                  _const_spec(pm.shape), _const_spec((1, width)), _const_spec((1, width)),
                  _const_spec((1, width))],
        out_specs=[pl.BlockSpec((tm, width), row)] * 8 + [k5_spec, v4_spec],
        out_shape=[out] * 8 + [jax.ShapeDtypeStruct((batch, seq, heads, 2, dh), F32),
                               jax.ShapeDtypeStruct((batch, seq, heads, 2 * dh), F32)],
        compiler_params=_params("arbitrary"),
        name="in_even",
    )(x, g_mix, w_bf, pm, qg, kg, lb)


def _bias_from_buckets(bk, rb_ref, h):
    b = jnp.zeros(bk.shape, F32)
    for u in range(REL_BUCKETS):
        b = jnp.where(bk == u, rb_ref[u, h], b)
    return jnp.where(bk < 0, -jnp.inf, b)


def _lam(lam_ref, lam_init):
    r = lam_ref[...]
    s1 = jnp.sum(r[0:1] * r[1:2], axis=1, keepdims=True)
    s2 = jnp.sum(r[2:3] * r[3:4], axis=1, keepdims=True)
    return jnp.exp(s1) - jnp.exp(s2) + lam_init


def _split_components(q, dh):
    lane = lax.broadcasted_iota(I32, q.shape, 1)
    q0 = jnp.where(lane < dh, q, 0.0)
    q1 = jnp.where(lane >= dh, q, 0.0)
    return jnp.concatenate([q0, q1], axis=0).astype(BF16)


def _attn_prompt_kernel(rb_ref, far_ref, lam_ref, bk_ref, sg_ref, q_ref, k_ref, v_ref, o_ref, bias_s, *,
                        qt, dh, lam_init):
    h = pl.program_id(0)
    t = q_ref.shape[0]
    kb = k_ref[...].astype(BF16)
    vt = v_ref[...].T.astype(BF16)
    qtr = q_ref[...].T
    sub = lax.broadcasted_iota(I32, (2 * dh, qt), 0)

    @pl.when(pl.program_id(1) == 0)
    def _():
        for d in range(2):
            b = _bias_from_buckets(bk_ref[d], rb_ref, h) * LOG2E
            bias_s[d] = jnp.concatenate([b, b], axis=1)

    bias = [bias_s[0], bias_s[1]]
    far = rb_ref[far_ref[0], h] * LOG2E
    lam = _lam(lam_ref, lam_init)
    gain = sg_ref[...] * (1.0 - lam_init)
    for i in range(t // qt):
        qi = qtr[:, i * qt:(i + 1) * qt]
        qz = jnp.concatenate([jnp.where(sub < dh, qi, 0.0), jnp.where(sub >= dh, qi, 0.0)],
                             axis=1).astype(BF16)
        n = (i + 1) * qt
        parts = [(n - qt, n)]
        s = [_dot(kb[n - qt:n], qz) + bias[0]]
        shift = [0.0]
        if i >= 1:
            parts.append((n - 2 * qt, n - qt))
            s.append(_dot(kb[n - 2 * qt:n - qt], qz) + bias[1])
            shift.append(0.0)
        if i >= 2:
            parts.append((0, n - 2 * qt))
            s.append(_dot(kb[:n - 2 * qt], qz))
            shift.append(far)
        m = functools.reduce(jnp.maximum, [jnp.max(x, axis=0, keepdims=True) + c for x, c in zip(s, shift)])
        p = [jnp.exp2(x - (m - c)) for x, c in zip(s, shift)]
        l = functools.reduce(lambda a, b: a + b, [jnp.sum(x, axis=0, keepdims=True) for x in p])
        acc = functools.reduce(lambda a, b: a + b,
                               [_dot(vt[:, lo:hi], x.astype(BF16)) for (lo, hi), x in zip(parts, p)])
        o = acc / l
        out = o[:, :qt] - lam * o[:, qt:]
        out = out * lax.rsqrt(jnp.mean(out * out, axis=0, keepdims=True) + EPS) * gain
        o_ref[i * qt:(i + 1) * qt, :] = out.T


def _attn_prompt(q, k, v, rel_bias, lam4, sub_gain, *, batch, seq, heads, dh, lam_init, qt):
    n, w = q.shape
    dv = w // heads
    assert dv == 2 * dh and seq % qt == 0 and qt % CHUNK == 0
    kj = jnp.arange(qt, dtype=I32)[:, None]
    qi = jnp.arange(qt, dtype=I32)[None, :]
    bk0 = jnp.where((kj // CHUNK) <= (qi // CHUNK), rel_bucket(kj - qi), -1)
    bk1 = rel_bucket(kj - qi - qt)
    bk = jnp.stack([bk0, bk1]).astype(I32)
    assert qt + 1 >= REL_MAX_DIST
    far = rel_bucket(jnp.full((1,), -(qt + 1), I32))
    smem = pl.BlockSpec(memory_space=pltpu.SMEM)
    seq_blk = pl.BlockSpec((seq, dv), lambda h, b: (b, h))
    return pl.pallas_call(
        functools.partial(_attn_prompt_kernel, qt=qt, dh=dh, lam_init=lam_init),
        grid=(heads, batch),
        in_specs=[smem, smem, _const_spec(lam4.shape), _const_spec(bk.shape), _const_spec((dv, 1)),
                  seq_blk, seq_blk, seq_blk],
        out_specs=seq_blk,
        out_shape=jax.ShapeDtypeStruct((n, w), F32),
        scratch_shapes=[pltpu.VMEM((2, qt, 2 * qt), F32)],
        compiler_params=_params("arbitrary", "arbitrary"),
        name="attn_prompt",
    )(rel_bias, far, lam4, bk, sub_gain.T, q, k, v)


def _attn_sample_kernel(rb_ref, lam_ref, bkc_ref, bkn_ref, sg_ref, q_ref, kc_ref, vc_ref, kn_ref, vn_ref,
                        o_ref, bc_s, bn_s, *, t, dh, lam_init, heads):
    dv = 2 * dh

    @pl.when(pl.program_id(0) == 0)
    def _():
        for hh in range(heads):
            bc = _bias_from_buckets(bkc_ref[...], rb_ref, hh) * LOG2E
            bn = _bias_from_buckets(bkn_ref[...], rb_ref, hh) * LOG2E
            bc_s[hh] = jnp.concatenate([bc, bc], axis=0)
            bn_s[hh] = jnp.concatenate([bn, bn], axis=0)

    lam = _lam(lam_ref, lam_init)
    past = vc_ref.shape[0] // heads
    for h in range(heads):
        cols = slice(h * dv, (h + 1) * dv)
        qz = _split_components(q_ref[:, cols], dh)
        sc = lax.dot_general(qz, kc_ref[0, :, cols].astype(BF16), NT_DIMS, preferred_element_type=F32)
        sn = lax.dot_general(qz, kn_ref[:, cols].astype(BF16), NT_DIMS, preferred_element_type=F32)
        sc = sc + bc_s[h]
        sn = sn + bn_s[h]
        m = jnp.maximum(jnp.max(sc, axis=1, keepdims=True), jnp.max(sn, axis=1, keepdims=True))
        pc = jnp.exp2(sc - m)
        pn = jnp.exp2(sn - m)
        l = jnp.sum(pc, axis=1, keepdims=True) + jnp.sum(pn, axis=1, keepdims=True)
        vc = vc_ref[pl.ds(h, past, stride=heads), :]
        acc = (_dot(pc.astype(BF16), vc.astype(BF16))
               + _dot(pn.astype(BF16), vn_ref[:, cols].astype(BF16)))
        o = acc / l
        out = o[:t] - lam * o[t:]
        o_ref[:, cols] = _rms(out, sg_ref[...]) * (1.0 - lam_init)


def _attn_sample(q, k_new, v_new, cache_k, cache_v, rel_bias, lam4, sub_gain, *, batch, t, heads, dh,
                 lam_init):
    n, w = q.shape
    dv = w // heads
    past = cache_k.shape[1]
    assert past % CHUNK == 0 and t <= CHUNK
    qpos = past + jnp.arange(t, dtype=I32)[:, None]
    bkc = rel_bucket(jnp.arange(past, dtype=I32)[None, :] - qpos).astype(I32)
    bkn = rel_bucket(past + jnp.arange(t, dtype=I32)[None, :] - qpos).astype(I32)
    smem = pl.BlockSpec(memory_space=pltpu.SMEM)
    new = pl.BlockSpec((t, w), lambda b: (b, 0))
    old_k = pl.BlockSpec((1, past, w), lambda b: (b, 0, 0))
    old_v = pl.BlockSpec((past * heads, dv), lambda b: (b, 0))
    return pl.pallas_call(
        functools.partial(_attn_sample_kernel, t=t, dh=dh, lam_init=lam_init, heads=heads),
        grid=(batch,),
        in_specs=[smem, _const_spec(lam4.shape), _const_spec(bkc.shape), _const_spec(bkn.shape),
                  _const_spec((1, dv)), new, old_k, old_v, new, new],
        out_specs=new,
        out_shape=jax.ShapeDtypeStruct((n, w), F32),
        scratch_shapes=[pltpu.VMEM((heads, 2 * t, past), F32), pltpu.VMEM((heads, 2 * t, t), F32)],
        compiler_params=_params("arbitrary"),
        name="attn_sample",
    )(rel_bias, lam4, bkc, bkn, sub_gain, q, cache_k, cache_v, k_new, v_new)


def _cumsum_rows(x):
    c = x.shape[0]
    row = lax.broadcasted_iota(I32, x.shape, 0)
    s = 1
    while s < c:
        x = x + jnp.where(row >= s, pltpu.roll(x, s, axis=0), 0.0)
        s *= 2
    return x


def _hgrn_kernel(qh_ref, kb_ref, lf_ref, ih_ref, s0_ref, hg_ref, ob_ref, sf_ref, st_s, *,
                 heads, dk, dv, c, nsb):
    t = pl.program_id(1)

    @pl.when(t == 0)
    def _():
        for h in range(heads):
            st_s[h] = s0_ref[0, h].T

    tb = qh_ref.shape[0]
    row = lax.broadcasted_iota(I32, (c, c), 0)
    col = lax.broadcasted_iota(I32, (c, c), 1)
    causal = col <= row

    def chunk(ci, carry):
        r0 = pl.multiple_of(ci * c, c)
        for h in range(heads):
            rows = pl.ds(r0, c)
            q = qh_ref[rows, h * dk:(h + 1) * dk]
            k = kb_ref[rows, h * dk:(h + 1) * dk]
            v = ih_ref[rows, h * dv:(h + 1) * dv]
            b = _cumsum_rows(lf_ref[rows, h * dk:(h + 1) * dk])
            bl = b[c - 1:c]
            st = st_s[h]
            inter = lax.dot_general((q * jnp.exp(b)).astype(BF16), st.astype(BF16), NT_DIMS,
                                    preferred_element_type=F32)
            qs, ks = [], []
            for j in range(nsb):
                ref = b[j * HG_SUB + HG_SUB // 2:j * HG_SUB + HG_SUB // 2 + 1]
                qs.append(q * jnp.exp(b - ref))
                sub = slice(j * HG_SUB, (j + 1) * HG_SUB)
                ks.append(k[sub] * jnp.exp(ref - b[sub]))
            a_full = lax.dot_general(jnp.concatenate(qs, axis=0).astype(BF16),
                                     jnp.concatenate(ks, axis=0).astype(BF16), NT_DIMS,
                                     preferred_element_type=F32)
            att = jnp.zeros((c, c), F32)
            for j in range(nsb):
                att = jnp.where(col >= j * HG_SUB, a_full[j * c:(j + 1) * c], att)
            att = jnp.where(causal, att, 0.0)
            out = inter + _dot(att.astype(BF16), v.astype(BF16))
            ob_ref[rows, h * dv:(h + 1) * dv] = _rms(out, hg_ref[...])
            kdec = (k * jnp.exp(bl - b)).astype(BF16)
            st_s[h] = jnp.exp(bl) * st + lax.dot_general(v.astype(BF16), kdec, TN_DIMS,
                                                         preferred_element_type=F32)
        return carry

    lax.fori_loop(0, tb // c, chunk, 0, unroll=min(8, tb // c))

    @pl.when(t == pl.num_programs(1) - 1)
    def _():
        for h in range(heads):
            sf_ref[0, h] = st_s[h].T


def _hgrn(qh, kb, lf, ih, s0, hg_gain, *, batch, seq, heads, dk, dv, tb):
    n = qh.shape[0]
    c = min(CHUNK, seq)
    assert seq % tb == 0 and tb % c == 0 and c % HG_SUB == 0
    nt = seq // tb
    blk = lambda w: pl.BlockSpec((tb, w), lambda b, t: (b * nt + t, 0))
    st = pl.BlockSpec((1, heads, dk, dv), lambda b, t: (b, 0, 0, 0))
    return pl.pallas_call(
        functools.partial(_hgrn_kernel, heads=heads, dk=dk, dv=dv, c=c, nsb=c // HG_SUB),
        grid=(batch, nt),
        in_specs=[blk(heads * dk), blk(heads * dk), blk(heads * dk), blk(heads * dv), st,
                  _const_spec((1, dv))],
        out_specs=[blk(heads * dv), st],
        out_shape=[jax.ShapeDtypeStruct((n, heads * dv), F32),
                   jax.ShapeDtypeStruct((batch, heads, dk, dv), F32)],
        scratch_shapes=[pltpu.VMEM((heads, dv, dk), F32)],
        compiler_params=_params("arbitrary", "arbitrary"),
        name="hgrn",
    )(qh, kb, lf, ih, s0, hg_gain)


def _store_token_tiles(ref, x):
    rows, d = x.shape
    s = d // LANES
    for c in range(s):
        ref[pl.ds(c, rows, stride=s), :] = x[:, c * LANES:(c + 1) * LANES]


def _load_token_tiles(ref, rows, s):
    return jnp.concatenate([ref[pl.ds(c, rows, stride=s), :] for c in range(s)], axis=1)


def _token_tile(r, s):
    return pl.ds(pl.multiple_of(r * s, s), s)


def _ffn_prologue(x1, gf_ref, wrh_ref, wrl_ref, rbias_ref, x1_ref, xn_ref, lg_ref):
    x1_ref[...] = x1
    xn = _rms(x1, gf_ref[...])
    _store_token_tiles(xn_ref, xn)
    hi = xn.astype(BF16)
    lo = (xn - hi.astype(F32)).astype(BF16)
    nt = functools.partial(lax.dot_general, dimension_numbers=NT_DIMS, preferred_element_type=F32)
    lg_ref[...] = nt(wrh_ref[...], hi) + nt(wrh_ref[...], lo) + nt(wrl_ref[...], hi) + rbias_ref[...]


def _out_even_kernel(oa_ref, ob_ref, gs_ref, x_ref, w_ref, gf_ref, wrh_ref, wrl_ref, rbias_ref,
                     x1_ref, xn_ref, lg_ref):
    o = jnp.concatenate([oa_ref[...], ob_ref[...] * gs_ref[...]], axis=1).astype(BF16)
    x1 = x_ref[...] + _dot(o, w_ref[...])
    _ffn_prologue(x1, gf_ref, wrh_ref, wrl_ref, rbias_ref, x1_ref, xn_ref, lg_ref)


def _out_even(oa, ob, gs, x, w_bf, g_ffn, wrh, wrl, rbias, *, tm):
    n, d = x.shape
    w = oa.shape[1]
    nr = wrh.shape[0]
    row = lambda i: (i, 0)
    return pl.pallas_call(
        _out_even_kernel,
        grid=(n // tm,),
        in_specs=[pl.BlockSpec((tm, w), row), pl.BlockSpec((tm, w), row), pl.BlockSpec((tm, w), row),
                  pl.BlockSpec((tm, d), row), _const_spec(w_bf.shape), _const_spec((1, d)),
                  _const_spec(wrh.shape), _const_spec(wrl.shape), _const_spec(rbias.shape)],
        out_specs=[pl.BlockSpec((tm, d), row), pl.BlockSpec((tm * (d // LANES), LANES), row),
                   pl.BlockSpec((nr, tm), lambda i: (0, i))],
        out_shape=[jax.ShapeDtypeStruct((n, d), F32), jax.ShapeDtypeStruct((n * (d // LANES), LANES), F32),
                   jax.ShapeDtypeStruct((nr, n), F32)],
        compiler_params=_params("arbitrary"),
        name="out_even",
    )(oa, ob, gs, x, w_bf, g_ffn, wrh, wrl, rbias)


def _gelu(x):
    return 0.5 * x * (1.0 + jnp.tanh(math.sqrt(2.0 / math.pi) * (x + 0.044715 * (x * x * x))))


def _odd_kernel(x_ref, gm_ref, win_ref, vg_ref, wsp_ref, bsp_ref, wout_ref, gf_ref, wrh_ref, wrl_ref,
                rbias_ref, x1_ref, xn_ref, lg_ref, *rest, half, groups, l, emit_v):
    if emit_v:
        vn_ref, u_s, s_s = rest
    else:
        vn_ref = None
        u_s, vn_s, s_s = rest
    tm = x_ref.shape[0]
    x = x_ref[...]
    xn = _rms(x, gm_ref[...]).astype(BF16)
    cw = 512
    vbuf = vn_ref if emit_v else vn_s
    for cidx in range(half // cw):
        u_s[:, cidx * cw:(cidx + 1) * cw] = _gelu(_dot(xn, win_ref[:, cidx * cw:(cidx + 1) * cw]))
        vbuf[:, cidx * cw:(cidx + 1) * cw] = _gelu(
            _dot(xn, win_ref[:, half + cidx * cw:half + (cidx + 1) * cw]))
    vbuf[...] = _rms(vbuf[...], vg_ref[...])
    gw = half // groups
    row = lax.broadcasted_iota(I32, (l, l), 0)
    col = lax.broadcasted_iota(I32, (l, l), 1)
    for g in range(groups):
        wg = jnp.where(col <= row, wsp_ref[g], 0.0).astype(BF16)
        bg = bsp_ref[:, g:g + 1]
        for ci in range(tm // l):
            vv = vbuf[ci * l:(ci + 1) * l, g * gw:(g + 1) * gw].astype(BF16)
            s_s[ci * l:(ci + 1) * l, g * gw:(g + 1) * gw] = _dot(wg, vv) + bg
    y = _dot((u_s[...] * s_s[...]).astype(BF16), wout_ref[...])
    _ffn_prologue(x + y, gf_ref, wrh_ref, wrl_ref, rbias_ref, x1_ref, xn_ref, lg_ref)


def _odd_mixer(x, g_mix, win_bf, v_gain, wsp, bsp_t, wout_bf, g_ffn, wrh, wrl, rbias, *, l, tm, emit_v):
    n, d = x.shape
    half = wout_bf.shape[0]
    groups = wsp.shape[0]
    nr = wrh.shape[0]
    assert n % tm == 0 and tm % l == 0
    row = lambda i: (i, 0)
    out_specs = [pl.BlockSpec((tm, d), row), pl.BlockSpec((tm * (d // LANES), LANES), row),
                 pl.BlockSpec((nr, tm), lambda i: (0, i))]
    out_shape = [jax.ShapeDtypeStruct((n, d), F32), jax.ShapeDtypeStruct((n * (d // LANES), LANES), F32),
                 jax.ShapeDtypeStruct((nr, n), F32)]
    scratch = [pltpu.VMEM((tm, half), F32)]
    if emit_v:
        out_specs.append(pl.BlockSpec((tm, half), row))
        out_shape.append(jax.ShapeDtypeStruct((n, half), F32))
    else:
        scratch.append(pltpu.VMEM((tm, half), F32))
    scratch.append(pltpu.VMEM((tm, half), F32))
    return pl.pallas_call(
        functools.partial(_odd_kernel, half=half, groups=groups, l=l, emit_v=emit_v),
        grid=(n // tm,),
        in_specs=[pl.BlockSpec((tm, d), row), _const_spec((1, d)), _const_spec(win_bf.shape),
                  _const_spec((1, half)), _const_spec(wsp.shape), _const_spec(bsp_t.shape),
                  _const_spec(wout_bf.shape), _const_spec((1, d)), _const_spec(wrh.shape),
                  _const_spec(wrl.shape), _const_spec(rbias.shape)],
        out_specs=out_specs,
        out_shape=out_shape,
        scratch_shapes=scratch,
        compiler_params=_params("arbitrary"),
        name="odd_mixer",
    )(x, g_mix, win_bf, v_gain, wsp, bsp_t, wout_bf, g_ffn, wrh, wrl, rbias)


def _route_kernel(lg_ref, tri_ref, init_ref, e_ref, g_ref, r_ref, p_ref, cnt_ref, base_ref, tcnt_ref, run_s,
                  *, groups, epg):
    i = pl.program_id(0)

    @pl.when(i == 0)
    def _():
        run_s[...] = init_ref[...]

    lg = lg_ref[...]
    tr = lg.shape[1]
    gl = [lg[g:g + 1] for g in range(groups)]
    m = functools.reduce(jnp.maximum, gl)
    grp = jnp.full((1, tr), groups - 1, I32)
    for g in range(groups - 2, -1, -1):
        grp = jnp.where(gl[g] == m, g, grp)
    gate_g = 1.0 / functools.reduce(lambda a, b: a + b, [jnp.exp(x - m) for x in gl])
    sel = lg[SUBLANES + (groups - 1) * epg:SUBLANES + groups * epg]
    for g in range(groups - 2, -1, -1):
        sel = jnp.where(grp == g, lg[SUBLANES + g * epg:SUBLANES + (g + 1) * epg], sel)
    sub = lax.broadcasted_iota(I32, sel.shape, 0)
    v1 = jnp.max(sel, axis=0, keepdims=True)
    i1 = jnp.min(jnp.where(sel == v1, sub, epg), axis=0, keepdims=True)
    sel2 = jnp.where(sub == i1, -jnp.inf, sel)
    v2 = jnp.max(sel2, axis=0, keepdims=True)
    i2 = jnp.min(jnp.where(sel2 == v2, sub, epg), axis=0, keepdims=True)
    tt = jnp.exp(v2 - v1)
    g1 = gate_g / (1.0 + tt)
    g2 = gate_g * tt / (1.0 + tt)
    e1 = grp * epg + i1
    e2 = grp * epg + i2
    ne = groups * epg
    eidx = lax.broadcasted_iota(I32, (ne, tr), 0)
    oh1 = eidx == e1
    oh2 = eidx == e2
    cnt = jnp.where(oh1, 1.0, 0.0) + jnp.where(oh2, 1.0, 0.0)
    local = _dot(cnt.astype(BF16), tri_ref[...])
    before = run_s[:, 0:1] + local
    r1 = jnp.sum(jnp.where(oh1, before, 0.0), axis=0, keepdims=True)
    r2 = jnp.sum(jnp.where(oh2, before, 0.0), axis=0, keepdims=True)
    tile_cnt = jnp.broadcast_to(jnp.sum(cnt, axis=1, keepdims=True), run_s.shape)
    padded = jnp.ceil(tile_cnt * (1.0 / RUN_CHUNK)) * RUN_CHUNK
    offset = _cumsum_rows(padded) - padded
    where_local = offset[:, 0:1] + local
    p1 = jnp.sum(jnp.where(oh1, where_local, 0.0), axis=0, keepdims=True)
    p2 = jnp.sum(jnp.where(oh2, where_local, 0.0), axis=0, keepdims=True)
    base_ref[...] = run_s[...]
    tcnt_ref[...] = tile_cnt
    run_s[...] = run_s[...] + tile_cnt
    rows = lax.broadcasted_iota(I32, (SUBLANES, tr), 0)
    e_ref[...] = jnp.where(rows == 0, e1, jnp.where(rows == 1, e2, 0))
    g_ref[...] = jnp.where(rows == 0, g1, jnp.where(rows == 1, g2, 0.0))
    r_ref[...] = jnp.where(rows == 0, r1, jnp.where(rows == 1, r2, 0.0)).astype(I32)
    p_ref[...] = jnp.where(rows == 0, p1, jnp.where(rows == 1, p2, 0.0)).astype(I32)
    cnt_ref[...] = run_s[...]


def _route(lgt, init, *, groups, epg, tr):
    nr, n = lgt.shape
    assert n % tr == 0 and nr == SUBLANES + groups * epg
    ne = groups * epg
    tri = (jnp.arange(tr)[:, None] < jnp.arange(tr)[None, :]).astype(BF16)
    tok = pl.BlockSpec((SUBLANES, tr), lambda i: (0, i))
    per_tile = pl.BlockSpec((ne, LANES), lambda i: (i, 0))
    tile_tab = jax.ShapeDtypeStruct((n // tr * ne, LANES), F32)
    return pl.pallas_call(
        functools.partial(_route_kernel, groups=groups, epg=epg),
        grid=(n // tr,),
        in_specs=[pl.BlockSpec((nr, tr), lambda i: (0, i)), _const_spec((tr, tr)), _const_spec((ne, LANES))],
        out_specs=[tok, tok, tok, tok, pl.BlockSpec((ne, LANES), lambda i: (0, 0)), per_tile, per_tile],
        out_shape=[jax.ShapeDtypeStruct((SUBLANES, n), I32), jax.ShapeDtypeStruct((SUBLANES, n), F32),
                   jax.ShapeDtypeStruct((SUBLANES, n), I32), jax.ShapeDtypeStruct((SUBLANES, n), I32),
                   jax.ShapeDtypeStruct((ne, LANES), F32), tile_tab, tile_tab],
        scratch_shapes=[pltpu.VMEM((ne, LANES), F32)],
        compiler_params=_params("arbitrary"),
        name="route",
    )(lgt, tri, init)


ISSUE_UNROLL = 16
DISPATCH_SLOTS = 3


def _dispatch_kernel(seg_ref, dst_ref, *rest, s, bm, tiles):
    x_refs = rest[:len(tiles)]
    buf_ref, zero_s, xin_s, sem, isem, zsem = rest[len(tiles):]
    tp = dst_ref.shape[2]
    blk = bm * s

    def zero_fill():
        zero_s[...] = jnp.zeros(zero_s.shape, F32)

        def block_copy(b):
            return pltpu.make_async_copy(zero_s, buf_ref.at[pl.ds(pl.multiple_of(b * blk, blk), blk)], zsem)

        for e in range(seg_ref.shape[1]):
            @pl.when(seg_ref[1, e] > 0)
            def _():
                block_copy(seg_ref[0, e] // bm - 1).start()

            @pl.when(seg_ref[1, e] - seg_ref[2, e] > bm)
            def _():
                block_copy(seg_ref[0, e] // bm - 2).start()
        for e in range(seg_ref.shape[1]):
            @pl.when(seg_ref[1, e] > 0)
            def _():
                block_copy(0).wait()

            @pl.when(seg_ref[1, e] - seg_ref[2, e] > bm)
            def _():
                block_copy(0).wait()

        def tail_start(b, c):
            block_copy(b).start()
            return c

        def tail_wait(b, c):
            block_copy(b).wait()
            return c

        first_unused = seg_ref[0, seg_ref.shape[1] - 1] // bm
        n_blocks = buf_ref.shape[0] // blk
        lax.fori_loop(first_unused, n_blocks, tail_start, 0)
        lax.fori_loop(first_unused, n_blocks, tail_wait, 0)

    pl.when(pl.program_id(0) == 0)(zero_fill)

    i = pl.program_id(0)
    last = pl.num_programs(0) - 1
    rows = tp * s

    def tile_in(t, go):
        lo = 0
        for x_ref, nt in zip(x_refs, tiles):
            @pl.when((t >= lo) & (t < lo + nt))
            def _(x_ref=x_ref, lo=lo):
                cp = pltpu.make_async_copy(x_ref.at[pl.ds(pl.multiple_of((t - lo) * rows, rows), rows)],
                                           xin_s.at[t % DISPATCH_SLOTS], isem.at[t % DISPATCH_SLOTS])
                cp.start() if go else cp.wait()
            lo += nt

    def rows_out_wait(t):
        for kk in range(TOP_K):
            pltpu.make_async_copy(xin_s.at[t % DISPATCH_SLOTS], buf_ref.at[pl.ds(0, rows)],
                                  sem.at[t % DISPATCH_SLOTS]).wait()

    @pl.when(i == 0)
    def _():
        tile_in(i, True)

    @pl.when(i >= DISPATCH_SLOTS - 1)
    def _():
        rows_out_wait(i - (DISPATCH_SLOTS - 1))

    @pl.when(i < last)
    def _():
        tile_in(i + 1, True)

    tile_in(i, False)
    slot = i % DISPATCH_SLOTS

    def issue(r0, c):
        for u in range(ISSUE_UNROLL):
            r = r0 * ISSUE_UNROLL + u
            for kk in range(TOP_K):
                pltpu.make_async_copy(xin_s.at[slot, _token_tile(r, s)],
                                      buf_ref.at[_token_tile(dst_ref[0, kk, r], s)], sem.at[slot]
                                      ).start(priority=kk % 2)
        return c

    lax.fori_loop(0, tp // ISSUE_UNROLL, issue, 0)

    @pl.when(i == last)
    def _():
        for back in range(DISPATCH_SLOTS - 2, -1, -1):
            @pl.when(i >= back)
            def _():
                rows_out_wait(i - back)


def _dispatch(seg, dest3, xns, n_slots, *, tp, s, bm):
    assert tp % ISSUE_UNROLL == 0 and all(x.shape[0] % (tp * s) == 0 for x in xns)
    tiles = tuple(x.shape[0] // (tp * s) for x in xns)
    grid_spec = pltpu.PrefetchScalarGridSpec(
        num_scalar_prefetch=1,
        grid=(sum(tiles),),
        in_specs=[pl.BlockSpec((1, TOP_K, tp), lambda i, sg: (i, 0, 0), memory_space=pltpu.SMEM)]
        + [pl.BlockSpec(memory_space=pl.ANY)] * len(xns),
        out_specs=pl.BlockSpec(memory_space=pl.ANY),
        scratch_shapes=[pltpu.VMEM((bm * s, LANES), F32), pltpu.VMEM((DISPATCH_SLOTS, tp * s, LANES), F32),
                        pltpu.SemaphoreType.DMA((DISPATCH_SLOTS,)), pltpu.SemaphoreType.DMA((DISPATCH_SLOTS,)),
                        pltpu.SemaphoreType.DMA(())],
    )
    return pl.pallas_call(
        functools.partial(_dispatch_kernel, s=s, bm=bm, tiles=tiles),
        grid_spec=grid_spec,
        out_shape=jax.ShapeDtypeStruct((n_slots * s, LANES), F32),
        compiler_params=_params("arbitrary"),
        name="dispatch",
    )(seg, dest3, *xns)


def _expert_kernel(be_ref, nu_ref, x_ref, wg_ref, wu_ref, wd_ref, o_ref, wg_s, wu_s, wd_s):
    b = pl.program_id(0)
    s = wg_s.shape[0] // LANES
    prev = be_ref[jnp.maximum(b - 1, 0)]

    @pl.when((b == 0) | (be_ref[b] != prev))
    def _():
        wg_s[...] = wg_ref[0, 0].astype(BF16)
        wu_s[...] = wu_ref[0, 0].astype(BF16)
        wd_s[...] = wd_ref[0, 0].astype(BF16)

    @pl.when(b < nu_ref[0])
    def _():
        xb = _load_token_tiles(x_ref, x_ref.shape[0] // s, s).astype(BF16)
        gate = _dot(xb, wg_s[...])
        h = gate * _sigmoid(gate) * _dot(xb, wu_s[...])
        _store_token_tiles(o_ref, _dot(h.astype(BF16), wd_s[...]))

    @pl.when(b >= nu_ref[0])
    def _():
        o_ref[...] = jnp.zeros(o_ref.shape, F32)


def _experts(blk_expert, n_used, buf, w_gate, w_up, w_down, *, layer, bm):
    d, de = w_gate.shape[2:]
    s = d // LANES
    n_blocks = buf.shape[0] // (bm * s)
    rows = pl.BlockSpec((bm * s, LANES), lambda b, be, nu: (b, 0))
    used_rows = pl.BlockSpec((bm * s, LANES), lambda b, be, nu: (jnp.minimum(b, nu[0] - 1), 0))
    grid_spec = pltpu.PrefetchScalarGridSpec(
        num_scalar_prefetch=2,
        grid=(n_blocks,),
        in_specs=[used_rows,
                  pl.BlockSpec((1, 1, d, de), lambda b, be, nu: (layer, be[b], 0, 0)),
                  pl.BlockSpec((1, 1, d, de), lambda b, be, nu: (layer, be[b], 0, 0)),
                  pl.BlockSpec((1, 1, de, d), lambda b, be, nu: (layer, be[b], 0, 0))],
        out_specs=rows,
        scratch_shapes=[pltpu.VMEM((d, de), BF16), pltpu.VMEM((d, de), BF16), pltpu.VMEM((de, d), BF16)],
    )
    return pl.pallas_call(
        _expert_kernel,
        grid_spec=grid_spec,
        out_shape=jax.ShapeDtypeStruct(buf.shape, F32),
        compiler_params=_params("arbitrary"),
        name="experts",
    )(blk_expert, n_used, buf, w_gate, w_up, w_down)


def _combine_kernel(tab_ref, nxt_ref, pos_ref, gate_ref, x_ref, yb_ref, o_ref, stg_s, tt_s, sem, *, ahead):
    i = pl.program_id(0)
    tq, d = x_ref.shape
    s = d // LANES
    piece = RUN_CHUNK * s
    slot = i % 2

    def run_copies(ref, dst_slot, go):
        for e in range(ref.shape[2]):
            def body(j, c):
                src = pl.multiple_of((ref[0, 0, e] + j * RUN_CHUNK) * s, s)
                dst = pl.multiple_of((ref[0, 2, e] + j * RUN_CHUNK) * s, piece)
                cp = pltpu.make_async_copy(yb_ref.at[pl.ds(src, piece)],
                                           stg_s.at[dst_slot, pl.ds(dst, piece)], sem.at[dst_slot])
                cp.start() if go else cp.wait()
                return c
            lax.fori_loop(0, ref[0, 1, e], body, 0)

    if ahead:
        @pl.when(i == 0)
        def _():
            run_copies(tab_ref, 0, True)

        @pl.when(i + 1 < pl.num_programs(0))
        def _():
            run_copies(nxt_ref, 1 - slot, True)
    else:
        run_copies(tab_ref, slot, True)

    run_copies(tab_ref, slot, False)

    def assemble(r0, c):
        for u in range(ISSUE_UNROLL):
            r = r0 * ISSUE_UNROLL + u
            acc = gate_ref[0, 0, r] * stg_s[slot, _token_tile(pos_ref[0, 0, r], s), :]
            for kk in range(1, TOP_K):
                acc = acc + gate_ref[0, kk, r] * stg_s[slot, _token_tile(pos_ref[0, kk, r], s), :]
            tt_s[_token_tile(r, s), :] = acc
        return c

    lax.fori_loop(0, tq // ISSUE_UNROLL, assemble, 0)
    o_ref[...] = x_ref[...] + _load_token_tiles(tt_s, tq, s)


def _combine(tab, pos3, gates3, x1, yb, *, tq):
    n, d = x1.shape
    s = d // LANES
    nt = n // tq
    ne = tab.shape[2]
    assert tq % ISSUE_UNROLL == 0
    stage_rows = tq * TOP_K + ne * RUN_CHUNK
    smem = lambda shape, imap: pl.BlockSpec(shape, imap, memory_space=pltpu.SMEM)
    return pl.pallas_call(
        functools.partial(_combine_kernel, ahead=nt > 2),
        grid=(nt,),
        in_specs=[smem((1, 3, ne), lambda i: (i, 0, 0)),
                  smem((1, 3, ne), lambda i: (jnp.minimum(i + 1, nt - 1), 0, 0)),
                  smem((1, TOP_K, tq), lambda i: (i, 0, 0)),
                  smem((1, TOP_K, tq), lambda i: (i, 0, 0)),
                  pl.BlockSpec((tq, d), lambda i: (i, 0)),
                  pl.BlockSpec(memory_space=pl.ANY)],
        out_specs=pl.BlockSpec((tq, d), lambda i: (i, 0)),
        out_shape=jax.ShapeDtypeStruct((n, d), F32),
        scratch_shapes=[pltpu.VMEM((2, stage_rows * s, LANES), F32), pltpu.VMEM((tq * s, LANES), F32),
                        pltpu.SemaphoreType.DMA((2,))],
        compiler_params=_params("arbitrary"),
        name="combine",
    )(tab, tab, pos3, gates3, x1, yb)


def _moe(streams, w_gate, w_up, w_down, *, layer, groups, epg):
    d = streams[0][0].shape[1]
    ne = groups * epg
    routed = []
    taken = jnp.zeros((ne, LANES), F32)
    for x1, _, lgt in streams:
        rtile = min(MOE_TILE, x1.shape[0])
        routed.append(_route(lgt, taken, groups=groups, epg=epg, tr=rtile) + (rtile,))
        taken = routed[-1][4]
    n_rows = sum(x1.shape[0] for x1, _, _ in streams) * TOP_K
    counts = taken[:, 0].astype(I32)
    bm = MOE_BLOCK if n_rows >= 2 * ne * MOE_BLOCK else MOE_BLOCK_SMALL
    padded = (counts + RUN_CHUNK + bm - 1) // bm * bm
    pend = jnp.cumsum(padded)
    pstart = pend - padded
    n_blocks = -(-(n_rows + ne * RUN_CHUNK) // bm) + ne
    eids = jnp.arange(ne, dtype=I32)
    blk_row = jnp.arange(n_blocks, dtype=I32)[:, None] * bm
    blk_expert = jnp.minimum(jnp.sum((pend[None, :] <= blk_row).astype(I32), axis=1), ne - 1)
    n_used = (pend[-1:] // bm).astype(I32)
    seg_tab = jnp.stack([pend, padded, counts]).astype(I32)
    dtile = min(r[-1] for r in routed)
    dests = []
    for (x1, _, _), (e8, g8, r8, p8, cnt, base, tcnt, rtile) in zip(streams, routed):
        seg = jnp.sum(jnp.where(e8[:TOP_K, :, None] == eids, pstart, 0), axis=-1)
        dest = seg + r8[:TOP_K]
        dests.append(dest.reshape(TOP_K, x1.shape[0] // dtile, dtile).transpose(1, 0, 2))
    buf = _dispatch(seg_tab, jnp.concatenate(dests, axis=0), [xn for _, xn, _ in streams], n_blocks * bm,
                    tp=dtile, s=d // LANES, bm=bm)
    yb = _experts(blk_expert, n_used, buf, w_gate, w_up, w_down, layer=layer, bm=bm)
    outs = []
    for (x1, _, _), (e8, g8, r8, p8, cnt, base, tcnt, rtile) in zip(streams, routed):
        nrt = x1.shape[0] // rtile
        run_start = pstart[None, :] + base.reshape(nrt, ne, LANES)[:, :, 0].astype(I32)
        pieces = (tcnt.reshape(nrt, ne, LANES)[:, :, 0].astype(I32) + RUN_CHUNK - 1) // RUN_CHUNK
        stage = (jnp.cumsum(pieces, axis=1) - pieces) * RUN_CHUNK
        tab = jnp.stack([run_start, pieces, stage], axis=1)
        by_tile = lambda a: a[:TOP_K].reshape(TOP_K, nrt, rtile).transpose(1, 0, 2)
        outs.append(_combine(tab, by_tile(p8), by_tile(g8), x1, yb, tq=rtile))
    return outs


def _router_weights(wg, bg, we, be):
    d, groups = wg.shape
    epg = we.shape[2]
    assert groups <= SUBLANES and epg == SUBLANES
    pad = jnp.zeros((SUBLANES - groups, d), F32)
    wr = jnp.concatenate([wg.T, pad, we.transpose(0, 2, 1).reshape(groups * epg, d)], axis=0)
    rb = jnp.concatenate([bg, jnp.zeros((SUBLANES - groups,), F32), be.reshape(-1)])[:, None]
    hi = wr.astype(BF16)
    lo = (wr - hi.astype(F32)).astype(BF16)
    return hi, lo, rb, groups, epg


def kernel(x_prompt, x_sample, cache_attn_k, cache_attn_v, state_hgrn, rel_bias, norm_mix, norm_ffn,
           w_in_even, w_out_even, q_norm_gain, k_norm_gain, lam_q1, lam_k1, lam_q2, lam_k2, da_out_gain,
           hgrn_lb_logits, hgrn_out_gain, w_in_odd, sgu_v_gain, sgu_w, sgu_b, w_out_odd,
           router_group_w, router_group_b, router_expert_w, router_expert_b,
           expert_w_gate, expert_w_up, expert_w_down):
    bp, tp, d = x_prompt.shape
    bs, ts, _ = x_sample.shape
    depth = norm_mix.shape[0]
    _, _, past, da_heads, _, da_dh = cache_attn_k.shape
    da_dv = cache_attn_v.shape[-1]
    _, _, hg_heads, hg_dk, hg_dv = state_hgrn.shape
    width = da_heads * da_dv
    assert width == da_heads * 2 * da_dh == hg_heads * hg_dk == hg_heads * hg_dv
    assert da_dv == LANES and hg_dk == LANES and hg_dv == LANES

    lb_all = jnp.cumsum(jax.nn.softmax(hgrn_lb_logits.astype(F32), axis=0), axis=0)
    gid = jnp.arange(width) // da_dh
    pm = jnp.where(gid[:, None] == gid[None, :], 1.0 / da_dh, 0.0).astype(BF16)

    xs = {"p": x_prompt.reshape(bp * tp, d), "s": x_sample.reshape(bs * ts, d)}
    dims = {"p": (bp, tp), "s": (bs, ts)}
    outs = {"p": {}, "s": {}}
    kp_l, vp_l, ks_l, vs_l, sp_l, ss_l, sgu_l = [], [], [], [], [], [], []

    for layer in range(depth):
        j = layer // 2
        wrh, wrl, rbias, groups, epg = _router_weights(
            router_group_w[layer], router_group_b[layer], router_expert_w[layer], router_expert_b[layer])
        g_mix = norm_mix[layer][None, :]
        g_ffn = norm_ffn[layer][None, :]
        if layer % 2 == 0:
            lam_init = 0.8 - 0.6 * math.exp(-0.3 * layer)
            w_in_bf = w_in_even[j].astype(BF16)
            w_out_bf = w_out_even[j].astype(BF16)
            reps = width // da_dh
            qg = jnp.tile(q_norm_gain[j], reps)[None, :]
            kg = jnp.tile(k_norm_gain[j], reps)[None, :]
            lam4 = jnp.stack([lam_q1[j], lam_k1[j], lam_q2[j], lam_k2[j]])
            sub_gain = da_out_gain[j][None, :]
            hg_gain = hgrn_out_gain[j][None, :]
            lb = lb_all[j][None, :]
            for key in ("p", "s"):
                b, t = dims[key]
                x = xs[key]
                q, k, v, qh, kb, lf, ih, gs, k5, v4 = _in_even(
                    x, g_mix, w_in_bf, pm, qg, kg, lb, width=width, q_scale=da_dh ** -0.5 * LOG2E,
                    tm=min(512, b * t), batch=b, seq=t, heads=da_heads, dh=da_dh)
                if key == "p":
                    oa = _attn_prompt(q, k, v, rel_bias, lam4, sub_gain, batch=b, seq=t, heads=da_heads,
                                      dh=da_dh, lam_init=lam_init, qt=min(256, t))
                    s0 = jnp.zeros((b, hg_heads, hg_dk, hg_dv), F32)
                    ob, s_new = _hgrn(qh, kb, lf, ih, s0, hg_gain, batch=b, seq=t, heads=hg_heads,
                                      dk=hg_dk, dv=hg_dv, tb=min(1024, t))
                    kp_l.append(k5)
                    vp_l.append(v4)
                    sp_l.append(s_new)
                else:
                    ck = cache_attn_k[j].reshape(b, past, width)
                    cv = cache_attn_v[j].reshape(b * past * da_heads, da_dv)
                    oa = _attn_sample(q, k, v, ck, cv, rel_bias, lam4, sub_gain, batch=b, t=t,
                                      heads=da_heads, dh=da_dh, lam_init=lam_init)
                    ob, s_new = _hgrn(qh, kb, lf, ih, state_hgrn[j], hg_gain, batch=b, seq=t,
                                      heads=hg_heads, dk=hg_dk, dv=hg_dv, tb=t)
                    ks_l.append(k5)
                    vs_l.append(v4)
                    ss_l.append(s_new)
                outs[key] = _out_even(oa, ob, gs, x, w_out_bf, g_ffn, wrh, wrl, rbias, tm=min(512, b * t))
        else:
            w_in_bf = w_in_odd[j].astype(BF16)
            w_out_bf = w_out_odd[j].astype(BF16)
            v_gain = sgu_v_gain[j][None, :]
            for key in ("p", "s"):
                b, t = dims[key]
                l = min(SGU_CHUNK, t)
                res = _odd_mixer(xs[key], g_mix, w_in_bf, v_gain, sgu_w[j][:, :l, :l], sgu_b[j][:, :l].T,
                                 w_out_bf, g_ffn, wrh, wrl, rbias, l=l, tm=min(512, b * t), emit_v=(key == "s"))
                outs[key] = res[:3]
                if key == "s":
                    sgu_l.append(res[3].reshape(b, t, -1))
        xs["p"], xs["s"] = _moe([outs["p"], outs["s"]], expert_w_gate, expert_w_up, expert_w_down,
                                layer=layer, groups=groups, epg=epg)

    return (xs["p"].reshape(bp, tp, d), xs["s"].reshape(bs, ts, d), jnp.stack(kp_l), jnp.stack(vp_l),
            jnp.stack(ks_l), jnp.stack(vs_l), jnp.stack(sp_l), jnp.stack(ss_l), jnp.stack(sgu_l))
```

```python
import functools
import math

import jax
import jax.numpy as jnp
from jax import lax
from jax.experimental import pallas as pl
from jax.experimental.pallas import tpu as pltpu

F32 = jnp.float32
BF16 = jnp.bfloat16
I32 = jnp.int32

EPS = 1e-6
LOG2E = math.log2(math.e)
CHUNK = 64
SGU_CHUNK = 128
REL_BUCKETS = 32
REL_MAX_DIST = 128
TOP_K = 2
MOE_BLOCK = 512
MOE_BLOCK_SMALL = 128
MOE_TILE = 1024
RUN_CHUNK = 32
HG_SUB = 16

LANES = 128
SUBLANES = 8
VMEM_LIMIT = 56 * 1024 * 1024

NT_DIMS = (((1,), (1,)), ((), ()))
TN_DIMS = (((0,), (0,)), ((), ()))


def _params(*sem):
    return pltpu.CompilerParams(dimension_semantics=sem, vmem_limit_bytes=VMEM_LIMIT)


def _const_spec(shape):
    nd = len(shape)
    return pl.BlockSpec(shape, lambda *_: (0,) * nd, pipeline_mode=pl.Buffered(1))


def _sigmoid(x):
    return 1.0 / (1.0 + jnp.exp(-x))


def _rms(x, g):
    return x * lax.rsqrt(jnp.mean(x * x, axis=-1, keepdims=True) + EPS) * g


def _dot(a, b):
    return jnp.dot(a, b, preferred_element_type=F32)


def rel_bucket(rel):
    half = REL_BUCKETS // 2
    max_exact = half // 2
    ret = (rel > 0).astype(I32) * half
    n = jnp.abs(rel)
    nf = jnp.maximum(n, 1).astype(F32)
    large = max_exact + (jnp.log(nf / max_exact) / math.log(REL_MAX_DIST / max_exact)
                         * (half - max_exact)).astype(I32)
    large = jnp.minimum(large, half - 1)
    return ret + jnp.where(n < max_exact, n, large)


def _in_even_kernel(x_ref, g_ref, w_ref, pm_ref, qg_ref, kg_ref, lb_ref,
                    q_ref, k_ref, v_ref, qh_ref, kb_ref, lf_ref, ih_ref, gs_ref, k5_ref, v4_ref, *, width,
                    q_scale):
    xn = _rms(x_ref[...], g_ref[...]).astype(BF16)

    def proj(c):
        return _dot(xn, w_ref[:, c * width:(c + 1) * width])

    def group_norm(y, gain):
        ms = _dot((y * y).astype(BF16), pm_ref[...])
        return y * lax.rsqrt(ms + EPS) * gain

    q_ref[...] = group_norm(proj(0), qg_ref[...]) * q_scale
    kn = group_norm(proj(1), kg_ref[...])
    k_ref[...] = kn
    bb, ts, heads, _, dh = k5_ref.shape
    for h in range(heads):
        for c in range(2):
            piece = kn[:, (2 * h + c) * dh:(2 * h + c + 1) * dh]
            k5_ref[:, :, h, c, :] = piece.reshape(bb, ts, dh)
    vv = proj(2)
    v_ref[...] = vv
    for h in range(heads):
        v4_ref[:, :, h, :] = vv[:, h * 2 * dh:(h + 1) * 2 * dh].reshape(bb, ts, 2 * dh)
    yq = proj(3)
    qh_ref[...] = yq * _sigmoid(yq)
    zf = proj(4)
    lb = lb_ref[...]
    lf_ref[...] = jnp.log(lb + (1.0 - lb) * _sigmoid(zf))
    kb_ref[...] = (1.0 - lb) * _sigmoid(-zf)
    ih_ref[...] = proj(5)
    yg = proj(6)
    gs_ref[...] = yg * _sigmoid(yg)


def _in_even(x, g_mix, w_bf, pm, qg, kg, lb, *, width, q_scale, tm, batch, seq, heads, dh):
    n, d = x.shape
    assert n % tm == 0 and (tm % seq == 0 or seq % tm == 0)
    row = lambda i: (i, 0)
    out = jax.ShapeDtypeStruct((n, width), F32)
    bb, ts = max(tm // seq, 1), min(tm, seq)
    per = seq // ts
    k5_spec = pl.BlockSpec((bb, ts, heads, 2, dh), lambda i: (i // per, i % per, 0, 0, 0))
    v4_spec = pl.BlockSpec((bb, ts, heads, 2 * dh), lambda i: (i // per, i % per, 0, 0))
    return pl.pallas_call(
        functools.partial(_in_even_kernel, width=width, q_scale=q_scale),
        grid=(n // tm,),
        in_specs=[pl.BlockSpec((tm, d), row), _const_spec((1, d)), _const_spec(w_bf.shape),
                  _const_spec(pm.shape), _const_spec((1, width)), _const_spec((1, width)),
                  _const_spec((1, width))],
        out_specs=[pl.BlockSpec((tm, width), row)] * 8 + [k5_spec, v4_spec],
        out_shape=[out] * 8 + [jax.ShapeDtypeStruct((batch, seq, heads, 2, dh), F32),
                               jax.ShapeDtypeStruct((batch, seq, heads, 2 * dh), F32)],
        compiler_params=_params("arbitrary"),
        name="in_even",
    )(x, g_mix, w_bf, pm, qg, kg, lb)


def _bias_from_buckets(bk, rb_ref, h):
    b = jnp.zeros(bk.shape, F32)
    for u in range(REL_BUCKETS):
        b = jnp.where(bk == u, rb_ref[u, h], b)
    return jnp.where(bk < 0, -jnp.inf, b)


def _lam(lam_ref, lam_init):
    r = lam_ref[...]
    s1 = jnp.sum(r[0:1] * r[1:2], axis=1, keepdims=True)
    s2 = jnp.sum(r[2:3] * r[3:4], axis=1, keepdims=True)
    return jnp.exp(s1) - jnp.exp(s2) + lam_init


def _split_components(q, dh):
    lane = lax.broadcasted_iota(I32, q.shape, 1)
    q0 = jnp.where(lane < dh, q, 0.0)
    q1 = jnp.where(lane >= dh, q, 0.0)
    return jnp.concatenate([q0, q1], axis=0).astype(BF16)


def _attn_prompt_kernel(rb_ref, far_ref, lam_ref, bk_ref, sg_ref, q_ref, k_ref, v_ref, o_ref, bias_s, *,
                        qt, dh, lam_init):
    h = pl.program_id(0)
    t = q_ref.shape[0]
    kb = k_ref[...].astype(BF16)
    vt = v_ref[...].T.astype(BF16)
    qtr = q_ref[...].T
    sub = lax.broadcasted_iota(I32, (2 * dh, qt), 0)

    @pl.when(pl.program_id(1) == 0)
    def _():
        for d in range(2):
            b = _bias_from_buckets(bk_ref[d], rb_ref, h) * LOG2E
            bias_s[d] = jnp.concatenate([b, b], axis=1)

    bias = [bias_s[0], bias_s[1]]
    far = rb_ref[far_ref[0], h] * LOG2E
    lam = _lam(lam_ref, lam_init)
    gain = sg_ref[...] * (1.0 - lam_init)
    for i in range(t // qt):
        qi = qtr[:, i * qt:(i + 1) * qt]
        qz = jnp.concatenate([jnp.where(sub < dh, qi, 0.0), jnp.where(sub >= dh, qi, 0.0)],
                             axis=1).astype(BF16)
        n = (i + 1) * qt
        parts = [(n - qt, n)]
        s = [_dot(kb[n - qt:n], qz) + bias[0]]
        shift = [0.0]
        if i >= 1:
            parts.append((n - 2 * qt, n - qt))
            s.append(_dot(kb[n - 2 * qt:n - qt], qz) + bias[1])
            shift.append(0.0)
        if i >= 2:
            parts.append((0, n - 2 * qt))
            s.append(_dot(kb[:n - 2 * qt], qz))
            shift.append(far)
        m = functools.reduce(jnp.maximum, [jnp.max(x, axis=0, keepdims=True) + c for x, c in zip(s, shift)])
        p = [jnp.exp2(x - (m - c)) for x, c in zip(s, shift)]
        l = functools.reduce(lambda a, b: a + b, [jnp.sum(x, axis=0, keepdims=True) for x in p])
        acc = functools.reduce(lambda a, b: a + b,
                               [_dot(vt[:, lo:hi], x.astype(BF16)) for (lo, hi), x in zip(parts, p)])
        o = acc / l
        out = o[:, :qt] - lam * o[:, qt:]
        out = out * lax.rsqrt(jnp.mean(out * out, axis=0, keepdims=True) + EPS) * gain
        o_ref[i * qt:(i + 1) * qt, :] = out.T


def _attn_prompt(q, k, v, rel_bias, lam4, sub_gain, *, batch, seq, heads, dh, lam_init, qt):
    n, w = q.shape
    dv = w // heads
    assert dv == 2 * dh and seq % qt == 0 and qt % CHUNK == 0
    kj = jnp.arange(qt, dtype=I32)[:, None]
    qi = jnp.arange(qt, dtype=I32)[None, :]
    bk0 = jnp.where((kj // CHUNK) <= (qi // CHUNK), rel_bucket(kj - qi), -1)
    bk1 = rel_bucket(kj - qi - qt)
    bk = jnp.stack([bk0, bk1]).astype(I32)
    assert qt + 1 >= REL_MAX_DIST
    far = rel_bucket(jnp.full((1,), -(qt + 1), I32))
    smem = pl.BlockSpec(memory_space=pltpu.SMEM)
    seq_blk = pl.BlockSpec((seq, dv), lambda h, b: (b, h))
    return pl.pallas_call(
        functools.partial(_attn_prompt_kernel, qt=qt, dh=dh, lam_init=lam_init),
        grid=(heads, batch),
        in_specs=[smem, smem, _const_spec(lam4.shape), _const_spec(bk.shape), _const_spec((dv, 1)),
                  seq_blk, seq_blk, seq_blk],
        out_specs=seq_blk,
        out_shape=jax.ShapeDtypeStruct((n, w), F32),
        scratch_shapes=[pltpu.VMEM((2, qt, 2 * qt), F32)],
        compiler_params=_params("arbitrary", "arbitrary"),
        name="attn_prompt",
    )(rel_bias, far, lam4, bk, sub_gain.T, q, k, v)


def _attn_sample_kernel(rb_ref, lam_ref, bkc_ref, bkn_ref, sg_ref, q_ref, kc_ref, vc_ref, kn_ref, vn_ref,
                        o_ref, bc_s, bn_s, *, t, dh, lam_init, heads):
    dv = 2 * dh

    @pl.when(pl.program_id(0) == 0)
    def _():
        for hh in range(heads):
            bc = _bias_from_buckets(bkc_ref[...], rb_ref, hh) * LOG2E
            bn = _bias_from_buckets(bkn_ref[...], rb_ref, hh) * LOG2E
            bc_s[hh] = jnp.concatenate([bc, bc], axis=0)
            bn_s[hh] = jnp.concatenate([bn, bn], axis=0)

    lam = _lam(lam_ref, lam_init)
    past = vc_ref.shape[0] // heads
    for h in range(heads):
        cols = slice(h * dv, (h + 1) * dv)
        qz = _split_components(q_ref[:, cols], dh)
        sc = lax.dot_general(qz, kc_ref[0, :, cols].astype(BF16), NT_DIMS, preferred_element_type=F32)
        sn = lax.dot_general(qz, kn_ref[:, cols].astype(BF16), NT_DIMS, preferred_element_type=F32)
        sc = sc + bc_s[h]
        sn = sn + bn_s[h]
        m = jnp.maximum(jnp.max(sc, axis=1, keepdims=True), jnp.max(sn, axis=1, keepdims=True))
        pc = jnp.exp2(sc - m)
        pn = jnp.exp2(sn - m)
        l = jnp.sum(pc, axis=1, keepdims=True) + jnp.sum(pn, axis=1, keepdims=True)
        vc = vc_ref[pl.ds(h, past, stride=heads), :]
        acc = (_dot(pc.astype(BF16), vc.astype(BF16))
               + _dot(pn.astype(BF16), vn_ref[:, cols].astype(BF16)))
        o = acc / l
        out = o[:t] - lam * o[t:]
        o_ref[:, cols] = _rms(out, sg_ref[...]) * (1.0 - lam_init)


def _attn_sample(q, k_new, v_new, cache_k, cache_v, rel_bias, lam4, sub_gain, *, batch, t, heads, dh,
                 lam_init):
    n, w = q.shape
    dv = w // heads
    past = cache_k.shape[1]
    assert past % CHUNK == 0 and t <= CHUNK
    qpos = past + jnp.arange(t, dtype=I32)[:, None]
    bkc = rel_bucket(jnp.arange(past, dtype=I32)[None, :] - qpos).astype(I32)
    bkn = rel_bucket(past + jnp.arange(t, dtype=I32)[None, :] - qpos).astype(I32)
    smem = pl.BlockSpec(memory_space=pltpu.SMEM)
    new = pl.BlockSpec((t, w), lambda b: (b, 0))
    old_k = pl.BlockSpec((1, past, w), lambda b: (b, 0, 0))
    old_v = pl.BlockSpec((past * heads, dv), lambda b: (b, 0))
    return pl.pallas_call(
        functools.partial(_attn_sample_kernel, t=t, dh=dh, lam_init=lam_init, heads=heads),
        grid=(batch,),
        in_specs=[smem, _const_spec(lam4.shape), _const_spec(bkc.shape), _const_spec(bkn.shape),
                  _const_spec((1, dv)), new, old_k, old_v, new, new],
        out_specs=new,
        out_shape=jax.ShapeDtypeStruct((n, w), F32),
        scratch_shapes=[pltpu.VMEM((heads, 2 * t, past), F32), pltpu.VMEM((heads, 2 * t, t), F32)],
        compiler_params=_params("arbitrary"),
        name="attn_sample",
    )(rel_bias, lam4, bkc, bkn, sub_gain, q, cache_k, cache_v, k_new, v_new)


def _cumsum_rows(x):
    c = x.shape[0]
    row = lax.broadcasted_iota(I32, x.shape, 0)
    s = 1
    while s < c:
        x = x + jnp.where(row >= s, pltpu.roll(x, s, axis=0), 0.0)
        s *= 2
    return x


def _hgrn_kernel(qh_ref, kb_ref, lf_ref, ih_ref, s0_ref, hg_ref, ob_ref, sf_ref, st_s, *,
                 heads, dk, dv, c, nsb):
    t = pl.program_id(1)

    @pl.when(t == 0)
    def _():
        for h in range(heads):
            st_s[h] = s0_ref[0, h].T

    tb = qh_ref.shape[0]
    row = lax.broadcasted_iota(I32, (c, c), 0)
    col = lax.broadcasted_iota(I32, (c, c), 1)
    causal = col <= row

    def chunk(ci, carry):
        r0 = pl.multiple_of(ci * c, c)
        for h in range(heads):
            rows = pl.ds(r0, c)
            q = qh_ref[rows, h * dk:(h + 1) * dk]
            k = kb_ref[rows, h * dk:(h + 1) * dk]
            v = ih_ref[rows, h * dv:(h + 1) * dv]
            b = _cumsum_rows(lf_ref[rows, h * dk:(h + 1) * dk])
            bl = b[c - 1:c]
            st = st_s[h]
            inter = lax.dot_general((q * jnp.exp(b)).astype(BF16), st.astype(BF16), NT_DIMS,
                                    preferred_element_type=F32)
            qs, ks = [], []
            for j in range(nsb):
                ref = b[j * HG_SUB + HG_SUB // 2:j * HG_SUB + HG_SUB // 2 + 1]
                qs.append(q * jnp.exp(b - ref))
                sub = slice(j * HG_SUB, (j + 1) * HG_SUB)
                ks.append(k[sub] * jnp.exp(ref - b[sub]))
            a_full = lax.dot_general(jnp.concatenate(qs, axis=0).astype(BF16),
                                     jnp.concatenate(ks, axis=0).astype(BF16), NT_DIMS,
                                     preferred_element_type=F32)
            att = jnp.zeros((c, c), F32)
            for j in range(nsb):
                att = jnp.where(col >= j * HG_SUB, a_full[j * c:(j + 1) * c], att)
            att = jnp.where(causal, att, 0.0)
            out = inter + _dot(att.astype(BF16), v.astype(BF16))
            ob_ref[rows, h * dv:(h + 1) * dv] = _rms(out, hg_ref[...])
            kdec = (k * jnp.exp(bl - b)).astype(BF16)
            st_s[h] = jnp.exp(bl) * st + lax.dot_general(v.astype(BF16), kdec, TN_DIMS,
                                                         preferred_element_type=F32)
        return carry

    lax.fori_loop(0, tb // c, chunk, 0, unroll=min(8, tb // c))

    @pl.when(t == pl.num_programs(1) - 1)
    def _():
        for h in range(heads):
            sf_ref[0, h] = st_s[h].T


def _hgrn(qh, kb, lf, ih, s0, hg_gain, *, batch, seq, heads, dk, dv, tb):
    n = qh.shape[0]
    c = min(CHUNK, seq)
    assert seq % tb == 0 and tb % c == 0 and c % HG_SUB == 0
    nt = seq // tb
    blk = lambda w: pl.BlockSpec((tb, w), lambda b, t: (b * nt + t, 0))
    st = pl.BlockSpec((1, heads, dk, dv), lambda b, t: (b, 0, 0, 0))
    return pl.pallas_call(
        functools.partial(_hgrn_kernel, heads=heads, dk=dk, dv=dv, c=c, nsb=c // HG_SUB),
        grid=(batch, nt),
        in_specs=[blk(heads * dk), blk(heads * dk), blk(heads * dk), blk(heads * dv), st,
                  _const_spec((1, dv))],
        out_specs=[blk(heads * dv), st],
        out_shape=[jax.ShapeDtypeStruct((n, heads * dv), F32),
                   jax.ShapeDtypeStruct((batch, heads, dk, dv), F32)],
        scratch_shapes=[pltpu.VMEM((heads, dv, dk), F32)],
        compiler_params=_params("arbitrary", "arbitrary"),
        name="hgrn",
    )(qh, kb, lf, ih, s0, hg_gain)


def _store_token_tiles(ref, x):
    rows, d = x.shape
    s = d // LANES
    for c in range(s):
        ref[pl.ds(c, rows, stride=s), :] = x[:, c * LANES:(c + 1) * LANES]


def _load_token_tiles(ref, rows, s):
    return jnp.concatenate([ref[pl.ds(c, rows, stride=s), :] for c in range(s)], axis=1)


def _token_tile(r, s):
    return pl.ds(pl.multiple_of(r * s, s), s)


def _ffn_prologue(x1, gf_ref, wrh_ref, wrl_ref, rbias_ref, x1_ref, xn_ref, lg_ref):
    x1_ref[...] = x1
    xn = _rms(x1, gf_ref[...])
    _store_token_tiles(xn_ref, xn)
    hi = xn.astype(BF16)
    lo = (xn - hi.astype(F32)).astype(BF16)
    nt = functools.partial(lax.dot_general, dimension_numbers=NT_DIMS, preferred_element_type=F32)
    lg_ref[...] = nt(wrh_ref[...], hi) + nt(wrh_ref[...], lo) + nt(wrl_ref[...], hi) + rbias_ref[...]


def _out_even_kernel(oa_ref, ob_ref, gs_ref, x_ref, w_ref, gf_ref, wrh_ref, wrl_ref, rbias_ref,
                     x1_ref, xn_ref, lg_ref):
    o = jnp.concatenate([oa_ref[...], ob_ref[...] * gs_ref[...]], axis=1).astype(BF16)
    x1 = x_ref[...] + _dot(o, w_ref[...])
    _ffn_prologue(x1, gf_ref, wrh_ref, wrl_ref, rbias_ref, x1_ref, xn_ref, lg_ref)


def _out_even(oa, ob, gs, x, w_bf, g_ffn, wrh, wrl, rbias, *, tm):
    n, d = x.shape
    w = oa.shape[1]
    nr = wrh.shape[0]
    row = lambda i: (i, 0)
    return pl.pallas_call(
        _out_even_kernel,
        grid=(n // tm,),
        in_specs=[pl.BlockSpec((tm, w), row), pl.BlockSpec((tm, w), row), pl.BlockSpec((tm, w), row),
                  pl.BlockSpec((tm, d), row), _const_spec(w_bf.shape), _const_spec((1, d)),
                  _const_spec(wrh.shape), _const_spec(wrl.shape), _const_spec(rbias.shape)],
        out_specs=[pl.BlockSpec((tm, d), row), pl.BlockSpec((tm * (d // LANES), LANES), row),
                   pl.BlockSpec((nr, tm), lambda i: (0, i))],
        out_shape=[jax.ShapeDtypeStruct((n, d), F32), jax.ShapeDtypeStruct((n * (d // LANES), LANES), F32),
                   jax.ShapeDtypeStruct((nr, n), F32)],
        compiler_params=_params("arbitrary"),
        name="out_even",
    )(oa, ob, gs, x, w_bf, g_ffn, wrh, wrl, rbias)


def _gelu(x):
    return 0.5 * x * (1.0 + jnp.tanh(math.sqrt(2.0 / math.pi) * (x + 0.044715 * (x * x * x))))


def _odd_kernel(x_ref, gm_ref, win_ref, vg_ref, wsp_ref, bsp_ref, wout_ref, gf_ref, wrh_ref, wrl_ref,
                rbias_ref, x1_ref, xn_ref, lg_ref, *rest, half, groups, l, emit_v):
    if emit_v:
        vn_ref, u_s, s_s = rest
    else:
        vn_ref = None
        u_s, vn_s, s_s = rest
    tm = x_ref.shape[0]
    x = x_ref[...]
    xn = _rms(x, gm_ref[...]).astype(BF16)
    cw = 512
    vbuf = vn_ref if emit_v else vn_s
    for cidx in range(half // cw):
        u_s[:, cidx * cw:(cidx + 1) * cw] = _gelu(_dot(xn, win_ref[:, cidx * cw:(cidx + 1) * cw]))
        vbuf[:, cidx * cw:(cidx + 1) * cw] = _gelu(
            _dot(xn, win_ref[:, half + cidx * cw:half + (cidx + 1) * cw]))
    vbuf[...] = _rms(vbuf[...], vg_ref[...])
    gw = half // groups
    row = lax.broadcasted_iota(I32, (l, l), 0)
    col = lax.broadcasted_iota(I32, (l, l), 1)
    for g in range(groups):
        wg = jnp.where(col <= row, wsp_ref[g], 0.0).astype(BF16)
        bg = bsp_ref[:, g:g + 1]
        for ci in range(tm // l):
            vv = vbuf[ci * l:(ci + 1) * l, g * gw:(g + 1) * gw].astype(BF16)
            s_s[ci * l:(ci + 1) * l, g * gw:(g + 1) * gw] = _dot(wg, vv) + bg
    y = _dot((u_s[...] * s_s[...]).astype(BF16), wout_ref[...])
    _ffn_prologue(x + y, gf_ref, wrh_ref, wrl_ref, rbias_ref, x1_ref, xn_ref, lg_ref)


def _odd_mixer(x, g_mix, win_bf, v_gain, wsp, bsp_t, wout_bf, g_ffn, wrh, wrl, rbias, *, l, tm, emit_v):
    n, d = x.shape
    half = wout_bf.shape[0]
    groups = wsp.shape[0]
    nr = wrh.shape[0]
    assert n % tm == 0 and tm % l == 0
    row = lambda i: (i, 0)
    out_specs = [pl.BlockSpec((tm, d), row), pl.BlockSpec((tm * (d // LANES), LANES), row),
                 pl.BlockSpec((nr, tm), lambda i: (0, i))]
    out_shape = [jax.ShapeDtypeStruct((n, d), F32), jax.ShapeDtypeStruct((n * (d // LANES), LANES), F32),
                 jax.ShapeDtypeStruct((nr, n), F32)]
    scratch = [pltpu.VMEM((tm, half), F32)]
    if emit_v:
        out_specs.append(pl.BlockSpec((tm, half), row))
        out_shape.append(jax.ShapeDtypeStruct((n, half), F32))
    else:
        scratch.append(pltpu.VMEM((tm, half), F32))
    scratch.append(pltpu.VMEM((tm, half), F32))
    return pl.pallas_call(
        functools.partial(_odd_kernel, half=half, groups=groups, l=l, emit_v=emit_v),
        grid=(n // tm,),
        in_specs=[pl.BlockSpec((tm, d), row), _const_spec((1, d)), _const_spec(win_bf.shape),
                  _const_spec((1, half)), _const_spec(wsp.shape), _const_spec(bsp_t.shape),
                  _const_spec(wout_bf.shape), _const_spec((1, d)), _const_spec(wrh.shape),
                  _const_spec(wrl.shape), _const_spec(rbias.shape)],
        out_specs=out_specs,
        out_shape=out_shape,
        scratch_shapes=scratch,
        compiler_params=_params("arbitrary"),
        name="odd_mixer",
    )(x, g_mix, win_bf, v_gain, wsp, bsp_t, wout_bf, g_ffn, wrh, wrl, rbias)


def _route_kernel(lg_ref, tri_ref, init_ref, e_ref, g_ref, r_ref, p_ref, cnt_ref, base_ref, tcnt_ref, run_s,
                  *, groups, epg):
    i = pl.program_id(0)

    @pl.when(i == 0)
    def _():
        run_s[...] = init_ref[...]

    lg = lg_ref[...]
    tr = lg.shape[1]
    gl = [lg[g:g + 1] for g in range(groups)]
    m = functools.reduce(jnp.maximum, gl)
    grp = jnp.full((1, tr), groups - 1, I32)
    for g in range(groups - 2, -1, -1):
        grp = jnp.where(gl[g] == m, g, grp)
    gate_g = 1.0 / functools.reduce(lambda a, b: a + b, [jnp.exp(x - m) for x in gl])
    sel = lg[SUBLANES + (groups - 1) * epg:SUBLANES + groups * epg]
    for g in range(groups - 2, -1, -1):
        sel = jnp.where(grp == g, lg[SUBLANES + g * epg:SUBLANES + (g + 1) * epg], sel)
    sub = lax.broadcasted_iota(I32, sel.shape, 0)
    v1 = jnp.max(sel, axis=0, keepdims=True)
    i1 = jnp.min(jnp.where(sel == v1, sub, epg), axis=0, keepdims=True)
    sel2 = jnp.where(sub == i1, -jnp.inf, sel)
    v2 = jnp.max(sel2, axis=0, keepdims=True)
    i2 = jnp.min(jnp.where(sel2 == v2, sub, epg), axis=0, keepdims=True)
    tt = jnp.exp(v2 - v1)
    g1 = gate_g / (1.0 + tt)
    g2 = gate_g * tt / (1.0 + tt)
    e1 = grp * epg + i1
    e2 = grp * epg + i2
    ne = groups * epg
    eidx = lax.broadcasted_iota(I32, (ne, tr), 0)
    oh1 = eidx == e1
    oh2 = eidx == e2
    cnt = jnp.where(oh1, 1.0, 0.0) + jnp.where(oh2, 1.0, 0.0)
    local = _dot(cnt.astype(BF16), tri_ref[...])
    before = run_s[:, 0:1] + local
    r1 = jnp.sum(jnp.where(oh1, before, 0.0), axis=0, keepdims=True)
    r2 = jnp.sum(jnp.where(oh2, before, 0.0), axis=0, keepdims=True)
    tile_cnt = jnp.broadcast_to(jnp.sum(cnt, axis=1, keepdims=True), run_s.shape)
    padded = jnp.ceil(tile_cnt * (1.0 / RUN_CHUNK)) * RUN_CHUNK
    offset = _cumsum_rows(padded) - padded
    where_local = offset[:, 0:1] + local
    p1 = jnp.sum(jnp.where(oh1, where_local, 0.0), axis=0, keepdims=True)
    p2 = jnp.sum(jnp.where(oh2, where_local, 0.0), axis=0, keepdims=True)
    base_ref[...] = run_s[...]
    tcnt_ref[...] = tile_cnt
    run_s[...] = run_s[...] + tile_cnt
    rows = lax.broadcasted_iota(I32, (SUBLANES, tr), 0)
    e_ref[...] = jnp.where(rows == 0, e1, jnp.where(rows == 1, e2, 0))
    g_ref[...] = jnp.where(rows == 0, g1, jnp.where(rows == 1, g2, 0.0))
    r_ref[...] = jnp.where(rows == 0, r1, jnp.where(rows == 1, r2, 0.0)).astype(I32)
    p_ref[...] = jnp.where(rows == 0, p1, jnp.where(rows == 1, p2, 0.0)).astype(I32)
    cnt_ref[...] = run_s[...]


def _route(lgt, init, *, groups, epg, tr):
    nr, n = lgt.shape
    assert n % tr == 0 and nr == SUBLANES + groups * epg
    ne = groups * epg
    tri = (jnp.arange(tr)[:, None] < jnp.arange(tr)[None, :]).astype(BF16)
    tok = pl.BlockSpec((SUBLANES, tr), lambda i: (0, i))
    per_tile = pl.BlockSpec((ne, LANES), lambda i: (i, 0))
    tile_tab = jax.ShapeDtypeStruct((n // tr * ne, LANES), F32)
    return pl.pallas_call(
        functools.partial(_route_kernel, groups=groups, epg=epg),
        grid=(n // tr,),
        in_specs=[pl.BlockSpec((nr, tr), lambda i: (0, i)), _const_spec((tr, tr)), _const_spec((ne, LANES))],
        out_specs=[tok, tok, tok, tok, pl.BlockSpec((ne, LANES), lambda i: (0, 0)), per_tile, per_tile],
        out_shape=[jax.ShapeDtypeStruct((SUBLANES, n), I32), jax.ShapeDtypeStruct((SUBLANES, n), F32),
                   jax.ShapeDtypeStruct((SUBLANES, n), I32), jax.ShapeDtypeStruct((SUBLANES, n), I32),
                   jax.ShapeDtypeStruct((ne, LANES), F32), tile_tab, tile_tab],
        scratch_shapes=[pltpu.VMEM((ne, LANES), F32)],
        compiler_params=_params("arbitrary"),
        name="route",
    )(lgt, tri, init)


ISSUE_UNROLL = 16
DISPATCH_SLOTS = 3


def _dispatch_kernel(seg_ref, dst_ref, *rest, s, bm, tiles):
    x_refs = rest[:len(tiles)]
    buf_ref, zero_s, xin_s, sem, isem, zsem = rest[len(tiles):]
    tp = dst_ref.shape[2]
    blk = bm * s

    def zero_fill():
        zero_s[...] = jnp.zeros(zero_s.shape, F32)

        def block_copy(b):
            return pltpu.make_async_copy(zero_s, buf_ref.at[pl.ds(pl.multiple_of(b * blk, blk), blk)], zsem)

        for e in range(seg_ref.shape[1]):
            @pl.when(seg_ref[1, e] > 0)
            def _():
                block_copy(seg_ref[0, e] // bm - 1).start()

            @pl.when(seg_ref[1, e] - seg_ref[2, e] > bm)
            def _():
                block_copy(seg_ref[0, e] // bm - 2).start()
        for e in range(seg_ref.shape[1]):
            @pl.when(seg_ref[1, e] > 0)
            def _():
                block_copy(0).wait()

            @pl.when(seg_ref[1, e] - seg_ref[2, e] > bm)
            def _():
                block_copy(0).wait()

        def tail_start(b, c):
            block_copy(b).start()
            return c

        def tail_wait(b, c):
            block_copy(b).wait()
            return c

        first_unused = seg_ref[0, seg_ref.shape[1] - 1] // bm
        n_blocks = buf_ref.shape[0] // blk
        lax.fori_loop(first_unused, n_blocks, tail_start, 0)
        lax.fori_loop(first_unused, n_blocks, tail_wait, 0)

    pl.when(pl.program_id(0) == 0)(zero_fill)

    i = pl.program_id(0)
    last = pl.num_programs(0) - 1
    rows = tp * s

    def tile_in(t, go):
        lo = 0
        for x_ref, nt in zip(x_refs, tiles):
            @pl.when((t >= lo) & (t < lo + nt))
            def _(x_ref=x_ref, lo=lo):
                cp = pltpu.make_async_copy(x_ref.at[pl.ds(pl.multiple_of((t - lo) * rows, rows), rows)],
                                           xin_s.at[t % DISPATCH_SLOTS], isem.at[t % DISPATCH_SLOTS])
                cp.start() if go else cp.wait()
            lo += nt

    def rows_out_wait(t):
        for kk in range(TOP_K):
            pltpu.make_async_copy(xin_s.at[t % DISPATCH_SLOTS], buf_ref.at[pl.ds(0, rows)],
                                  sem.at[t % DISPATCH_SLOTS]).wait()

    @pl.when(i == 0)
    def _():
        tile_in(i, True)

    @pl.when(i >= DISPATCH_SLOTS - 1)
    def _():
        rows_out_wait(i - (DISPATCH_SLOTS - 1))

    @pl.when(i < last)
    def _():
        tile_in(i + 1, True)

    tile_in(i, False)
    slot = i % DISPATCH_SLOTS

    def issue(r0, c):
        for u in range(ISSUE_UNROLL):
            r = r0 * ISSUE_UNROLL + u
            for kk in range(TOP_K):
                pltpu.make_async_copy(xin_s.at[slot, _token_tile(r, s)],
                                      buf_ref.at[_token_tile(dst_ref[0, kk, r], s)], sem.at[slot]
                                      ).start(priority=kk % 2)
        return c

    lax.fori_loop(0, tp // ISSUE_UNROLL, issue, 0)

    @pl.when(i == last)
    def _():
        for back in range(DISPATCH_SLOTS - 2, -1, -1):
            @pl.when(i >= back)
            def _():
                rows_out_wait(i - back)


def _dispatch(seg, dest3, xns, n_slots, *, tp, s, bm):
    assert tp % ISSUE_UNROLL == 0 and all(x.shape[0] % (tp * s) == 0 for x in xns)
    tiles = tuple(x.shape[0] // (tp * s) for x in xns)
    grid_spec = pltpu.PrefetchScalarGridSpec(
        num_scalar_prefetch=1,
        grid=(sum(tiles),),
        in_specs=[pl.BlockSpec((1, TOP_K, tp), lambda i, sg: (i, 0, 0), memory_space=pltpu.SMEM)]
        + [pl.BlockSpec(memory_space=pl.ANY)] * len(xns),
        out_specs=pl.BlockSpec(memory_space=pl.ANY),
        scratch_shapes=[pltpu.VMEM((bm * s, LANES), F32), pltpu.VMEM((DISPATCH_SLOTS, tp * s, LANES), F32),
                        pltpu.SemaphoreType.DMA((DISPATCH_SLOTS,)), pltpu.SemaphoreType.DMA((DISPATCH_SLOTS,)),
                        pltpu.SemaphoreType.DMA(())],
    )
    return pl.pallas_call(
        functools.partial(_dispatch_kernel, s=s, bm=bm, tiles=tiles),
        grid_spec=grid_spec,
        out_shape=jax.ShapeDtypeStruct((n_slots * s, LANES), F32),
        compiler_params=_params("arbitrary"),
        name="dispatch",
    )(seg, dest3, *xns)


def _expert_kernel(be_ref, nu_ref, x_ref, wg_ref, wu_ref, wd_ref, o_ref, wg_s, wu_s, wd_s):
    b = pl.program_id(0)
    s = wg_s.shape[0] // LANES
    prev = be_ref[jnp.maximum(b - 1, 0)]

    @pl.when((b == 0) | (be_ref[b] != prev))
    def _():
        wg_s[...] = wg_ref[0, 0].astype(BF16)
        wu_s[...] = wu_ref[0, 0].astype(BF16)
        wd_s[...] = wd_ref[0, 0].astype(BF16)

    @pl.when(b < nu_ref[0])
    def _():
        xb = _load_token_tiles(x_ref, x_ref.shape[0] // s, s).astype(BF16)
        de = wg_s.shape[1]
        hw = de // 2 if de % (4 * LANES) == 0 else de
        y = None
        for c0 in range(0, de, hw):
            gate = _dot(xb, wg_s[:, c0:c0 + hw])
            h = gate * _sigmoid(gate) * _dot(xb, wu_s[:, c0:c0 + hw])
            part = _dot(h.astype(BF16), wd_s[c0:c0 + hw, :])
            y = part if y is None else y + part
        _store_token_tiles(o_ref, y)

    @pl.when(b >= nu_ref[0])
    def _():
        o_ref[...] = jnp.zeros(o_ref.shape, F32)


def _experts(blk_expert, n_used, buf, w_gate, w_up, w_down, *, layer, bm):
    d, de = w_gate.shape[2:]
    s = d // LANES
    n_blocks = buf.shape[0] // (bm * s)
    rows = pl.BlockSpec((bm * s, LANES), lambda b, be, nu: (b, 0))
    used_rows = pl.BlockSpec((bm * s, LANES), lambda b, be, nu: (jnp.minimum(b, nu[0] - 1), 0))
    grid_spec = pltpu.PrefetchScalarGridSpec(
        num_scalar_prefetch=2,
        grid=(n_blocks,),
        in_specs=[used_rows,
                  pl.BlockSpec((1, 1, d, de), lambda b, be, nu: (layer, be[b], 0, 0)),
                  pl.BlockSpec((1, 1, d, de), lambda b, be, nu: (layer, be[b], 0, 0)),
                  pl.BlockSpec((1, 1, de, d), lambda b, be, nu: (layer, be[b], 0, 0))],
        out_specs=rows,
        scratch_shapes=[pltpu.VMEM((d, de), BF16), pltpu.VMEM((d, de), BF16), pltpu.VMEM((de, d), BF16)],
    )
    return pl.pallas_call(
        _expert_kernel,
        grid_spec=grid_spec,
        out_shape=jax.ShapeDtypeStruct(buf.shape, F32),
        compiler_params=_params("arbitrary"),
        name="experts",
    )(blk_expert, n_used, buf, w_gate, w_up, w_down)


def _combine_kernel(tab_ref, nxt_ref, pos_ref, gate_ref, x_ref, yb_ref, o_ref, stg_s, tt_s, sem, *, ahead):
    i = pl.program_id(0)
    tq, d = x_ref.shape
    s = d // LANES
    piece = RUN_CHUNK * s
    slot = i % 2

    def run_copies(ref, dst_slot, go):
        for e in range(ref.shape[2]):
            def body(j, c):
                src = pl.multiple_of((ref[0, 0, e] + j * RUN_CHUNK) * s, s)
                dst = pl.multiple_of((ref[0, 2, e] + j * RUN_CHUNK) * s, piece)
                cp = pltpu.make_async_copy(yb_ref.at[pl.ds(src, piece)],
                                           stg_s.at[dst_slot, pl.ds(dst, piece)], sem.at[dst_slot])
                cp.start() if go else cp.wait()
                return c
            lax.fori_loop(0, ref[0, 1, e], body, 0)

    if ahead:
        @pl.when(i == 0)
        def _():
            run_copies(tab_ref, 0, True)

        @pl.when(i + 1 < pl.num_programs(0))
        def _():
            run_copies(nxt_ref, 1 - slot, True)
    else:
        run_copies(tab_ref, slot, True)

    run_copies(tab_ref, slot, False)

    def assemble(r0, c):
        for u in range(ISSUE_UNROLL):
            r = r0 * ISSUE_UNROLL + u
            acc = gate_ref[0, 0, r] * stg_s[slot, _token_tile(pos_ref[0, 0, r], s), :]
            for kk in range(1, TOP_K):
                acc = acc + gate_ref[0, kk, r] * stg_s[slot, _token_tile(pos_ref[0, kk, r], s), :]
            tt_s[_token_tile(r, s), :] = acc
        return c

    lax.fori_loop(0, tq // ISSUE_UNROLL, assemble, 0)
    o_ref[...] = x_ref[...] + _load_token_tiles(tt_s, tq, s)


def _combine(tab, pos3, gates3, x1, yb, *, tq):
    n, d = x1.shape
    s = d // LANES
    nt = n // tq
    ne = tab.shape[2]
    assert tq % ISSUE_UNROLL == 0
    stage_rows = tq * TOP_K + ne * RUN_CHUNK
    smem = lambda shape, imap: pl.BlockSpec(shape, imap, memory_space=pltpu.SMEM)
    return pl.pallas_call(
        functools.partial(_combine_kernel, ahead=nt > 2),
        grid=(nt,),
        in_specs=[smem((1, 3, ne), lambda i: (i, 0, 0)),
                  smem((1, 3, ne), lambda i: (jnp.minimum(i + 1, nt - 1), 0, 0)),
                  smem((1, TOP_K, tq), lambda i: (i, 0, 0)),
                  smem((1, TOP_K, tq), lambda i: (i, 0, 0)),
                  pl.BlockSpec((tq, d), lambda i: (i, 0)),
                  pl.BlockSpec(memory_space=pl.ANY)],
        out_specs=pl.BlockSpec((tq, d), lambda i: (i, 0)),
        out_shape=jax.ShapeDtypeStruct((n, d), F32),
        scratch_shapes=[pltpu.VMEM((2, stage_rows * s, LANES), F32), pltpu.VMEM((tq * s, LANES), F32),
                        pltpu.SemaphoreType.DMA((2,))],
        compiler_params=_params("arbitrary"),
        name="combine",
    )(tab, tab, pos3, gates3, x1, yb)


def _moe(streams, w_gate, w_up, w_down, *, layer, groups, epg):
    d = streams[0][0].shape[1]
    ne = groups * epg
    routed = []
    taken = jnp.zeros((ne, LANES), F32)
    for x1, _, lgt in streams:
        rtile = min(MOE_TILE, x1.shape[0])
        routed.append(_route(lgt, taken, groups=groups, epg=epg, tr=rtile) + (rtile,))
        taken = routed[-1][4]
    n_rows = sum(x1.shape[0] for x1, _, _ in streams) * TOP_K
    counts = taken[:, 0].astype(I32)
    bm = MOE_BLOCK if n_rows >= 2 * ne * MOE_BLOCK else MOE_BLOCK_SMALL
    padded = (counts + RUN_CHUNK + bm - 1) // bm * bm
    pend = jnp.cumsum(padded)
    pstart = pend - padded
    n_blocks = -(-(n_rows + ne * RUN_CHUNK) // bm) + ne
    eids = jnp.arange(ne, dtype=I32)
    blk_row = jnp.arange(n_blocks, dtype=I32)[:, None] * bm
    blk_expert = jnp.minimum(jnp.sum((pend[None, :] <= blk_row).astype(I32), axis=1), ne - 1)
    n_used = (pend[-1:] // bm).astype(I32)
    seg_tab = jnp.stack([pend, padded, counts]).astype(I32)
    dtile = min(r[-1] for r in routed)
    dests = []
    for (x1, _, _), (e8, g8, r8, p8, cnt, base, tcnt, rtile) in zip(streams, routed):
        seg = jnp.sum(jnp.where(e8[:TOP_K, :, None] == eids, pstart, 0), axis=-1)
        dest = seg + r8[:TOP_K]
        dests.append(dest.reshape(TOP_K, x1.shape[0] // dtile, dtile).transpose(1, 0, 2))
    buf = _dispatch(seg_tab, jnp.concatenate(dests, axis=0), [xn for _, xn, _ in streams], n_blocks * bm,
                    tp=dtile, s=d // LANES, bm=bm)
    yb = _experts(blk_expert, n_used, buf, w_gate, w_up, w_down, layer=layer, bm=bm)
    outs = []
    for (x1, _, _), (e8, g8, r8, p8, cnt, base, tcnt, rtile) in zip(streams, routed):
        nrt = x1.shape[0] // rtile
        run_start = pstart[None, :] + base.reshape(nrt, ne, LANES)[:, :, 0].astype(I32)
        pieces = (tcnt.reshape(nrt, ne, LANES)[:, :, 0].astype(I32) + RUN_CHUNK - 1) // RUN_CHUNK
        stage = (jnp.cumsum(pieces, axis=1) - pieces) * RUN_CHUNK
        tab = jnp.stack([run_start, pieces, stage], axis=1)
        by_tile = lambda a: a[:TOP_K].reshape(TOP_K, nrt, rtile).transpose(1, 0, 2)
        outs.append(_combine(tab, by_tile(p8), by_tile(g8), x1, yb, tq=rtile))
    return outs


def _router_weights(wg, bg, we, be):
    d, groups = wg.shape
    epg = we.shape[2]
    assert groups <= SUBLANES and epg == SUBLANES
    pad = jnp.zeros((SUBLANES - groups, d), F32)
    wr = jnp.concatenate([wg.T, pad, we.transpose(0, 2, 1).reshape(groups * epg, d)], axis=0)
    rb = jnp.concatenate([bg, jnp.zeros((SUBLANES - groups,), F32), be.reshape(-1)])[:, None]
    hi = wr.astype(BF16)
    lo = (wr - hi.astype(F32)).astype(BF16)
    return hi, lo, rb, groups, epg


def kernel(x_prompt, x_sample, cache_attn_k, cache_attn_v, state_hgrn, rel_bias, norm_mix, norm_ffn,
           w_in_even, w_out_even, q_norm_gain, k_norm_gain, lam_q1, lam_k1, lam_q2, lam_k2, da_out_gain,
           hgrn_lb_logits, hgrn_out_gain, w_in_odd, sgu_v_gain, sgu_w, sgu_b, w_out_odd,
           router_group_w, router_group_b, router_expert_w, router_expert_b,
           expert_w_gate, expert_w_up, expert_w_down):
    bp, tp, d = x_prompt.shape
    bs, ts, _ = x_sample.shape
    depth = norm_mix.shape[0]
    _, _, past, da_heads, _, da_dh = cache_attn_k.shape
    da_dv = cache_attn_v.shape[-1]
    _, _, hg_heads, hg_dk, hg_dv = state_hgrn.shape
    width = da_heads * da_dv
    assert width == da_heads * 2 * da_dh == hg_heads * hg_dk == hg_heads * hg_dv
    assert da_dv == LANES and hg_dk == LANES and hg_dv == LANES

    lb_all = jnp.cumsum(jax.nn.softmax(hgrn_lb_logits.astype(F32), axis=0), axis=0)
    gid = jnp.arange(width) // da_dh
    pm = jnp.where(gid[:, None] == gid[None, :], 1.0 / da_dh, 0.0).astype(BF16)

    xs = {"p": x_prompt.reshape(bp * tp, d), "s": x_sample.reshape(bs * ts, d)}
    dims = {"p": (bp, tp), "s": (bs, ts)}
    outs = {"p": {}, "s": {}}
    kp_l, vp_l, ks_l, vs_l, sp_l, ss_l, sgu_l = [], [], [], [], [], [], []

    for layer in range(depth):
        j = layer // 2
        wrh, wrl, rbias, groups, epg = _router_weights(
            router_group_w[layer], router_group_b[layer], router_expert_w[layer], router_expert_b[layer])
        g_mix = norm_mix[layer][None, :]
        g_ffn = norm_ffn[layer][None, :]
        if layer % 2 == 0:
            lam_init = 0.8 - 0.6 * math.exp(-0.3 * layer)
            w_in_bf = w_in_even[j].astype(BF16)
            w_out_bf = w_out_even[j].astype(BF16)
            reps = width // da_dh
            qg = jnp.tile(q_norm_gain[j], reps)[None, :]
            kg = jnp.tile(k_norm_gain[j], reps)[None, :]
            lam4 = jnp.stack([lam_q1[j], lam_k1[j], lam_q2[j], lam_k2[j]])
            sub_gain = da_out_gain[j][None, :]
            hg_gain = hgrn_out_gain[j][None, :]
            lb = lb_all[j][None, :]
            for key in ("p", "s"):
                b, t = dims[key]
                x = xs[key]
                q, k, v, qh, kb, lf, ih, gs, k5, v4 = _in_even(
                    x, g_mix, w_in_bf, pm, qg, kg, lb, width=width, q_scale=da_dh ** -0.5 * LOG2E,
                    tm=min(512, b * t), batch=b, seq=t, heads=da_heads, dh=da_dh)
                if key == "p":
                    oa = _attn_prompt(q, k, v, rel_bias, lam4, sub_gain, batch=b, seq=t, heads=da_heads,
                                      dh=da_dh, lam_init=lam_init, qt=min(256, t))
                    s0 = jnp.zeros((b, hg_heads, hg_dk, hg_dv), F32)
                    ob, s_new = _hgrn(qh, kb, lf, ih, s0, hg_gain, batch=b, seq=t, heads=hg_heads,
                                      dk=hg_dk, dv=hg_dv, tb=min(512, t))
                    kp_l.append(k5)
                    vp_l.append(v4)
                    sp_l.append(s_new)
                else:
                    ck = cache_attn_k[j].reshape(b, past, width)
                    cv = cache_attn_v[j].reshape(b * past * da_heads, da_dv)
                    oa = _attn_sample(q, k, v, ck, cv, rel_bias, lam4, sub_gain, batch=b, t=t,
                                      heads=da_heads, dh=da_dh, lam_init=lam_init)
                    ob, s_new = _hgrn(qh, kb, lf, ih, state_hgrn[j], hg_gain, batch=b, seq=t,
                                      heads=hg_heads, dk=hg_dk, dv=hg_dv, tb=t)
                    ks_l.append(k5)
                    vs_l.append(v4)
                    ss_l.append(s_new)
                outs[key] = _out_even(oa, ob, gs, x, w_out_bf, g_ffn, wrh, wrl, rbias, tm=min(512, b * t))
        else:
            w_in_bf = w_in_odd[j].astype(BF16)
            w_out_bf = w_out_odd[j].astype(BF16)
            v_gain = sgu_v_gain[j][None, :]
            for key in ("p", "s"):
                b, t = dims[key]
                l = min(SGU_CHUNK, t)
                res = _odd_mixer(xs[key], g_mix, w_in_bf, v_gain, sgu_w[j][:, :l, :l], sgu_b[j][:, :l].T,
                                 w_out_bf, g_ffn, wrh, wrl, rbias, l=l, tm=min(512, b * t), emit_v=(key == "s"))
                outs[key] = res[:3]
                if key == "s":
                    sgu_l.append(res[3].reshape(b, t, -1))
        xs["p"], xs["s"] = _moe([outs["p"], outs["s"]], expert_w_gate, expert_w_up, expert_w_down,
                                layer=layer, groups=groups, epg=epg)

    return (xs["p"].reshape(bp, tp, d), xs["s"].reshape(bs, ts, d), jnp.stack(kp_l), jnp.stack(vp_l),
            jnp.stack(ks_l), jnp.stack(vs_l), jnp.stack(sp_l), jnp.stack(ss_l), jnp.stack(sgu_l))
```

```python
import functools
import math

import jax
import jax.numpy as jnp
from jax import lax
from jax.experimental import pallas as pl
from jax.experimental.pallas import tpu as pltpu

F32 = jnp.float32
BF16 = jnp.bfloat16
I32 = jnp.int32

EPS = 1e-6
LOG2E = math.log2(math.e)
CHUNK = 64
SGU_CHUNK = 128
REL_BUCKETS = 32
REL_MAX_DIST = 128
TOP_K = 2
MOE_BLOCK = 512
MOE_BLOCK_SMALL = 128
MOE_TILE = 1024
RUN_CHUNK = 32
HG_SUB = 16

LANES = 128
SUBLANES = 8
VMEM_LIMIT = 56 * 1024 * 1024

NT_DIMS = (((1,), (1,)), ((), ()))
TN_DIMS = (((0,), (0,)), ((), ()))


def _params(*sem):
    return pltpu.CompilerParams(dimension_semantics=sem, vmem_limit_bytes=VMEM_LIMIT)


def _const_spec(shape):
    nd = len(shape)
    return pl.BlockSpec(shape, lambda *_: (0,) * nd, pipeline_mode=pl.Buffered(1))


def _sigmoid(x):
    return 1.0 / (1.0 + jnp.exp(-x))


def _rms(x, g):
    return x * lax.rsqrt(jnp.mean(x * x, axis=-1, keepdims=True) + EPS) * g


def _dot(a, b):
    return jnp.dot(a, b, preferred_element_type=F32)


def rel_bucket(rel):
    half = REL_BUCKETS // 2
    max_exact = half // 2
    ret = (rel > 0).astype(I32) * half
    n = jnp.abs(rel)
    nf = jnp.maximum(n, 1).astype(F32)
    large = max_exact + (jnp.log(nf / max_exact) / math.log(REL_MAX_DIST / max_exact)
                         * (half - max_exact)).astype(I32)
    large = jnp.minimum(large, half - 1)
    return ret + jnp.where(n < max_exact, n, large)


def _in_even_kernel(x_ref, g_ref, w_ref, pm_ref, qg_ref, kg_ref, lb_ref,
                    q_ref, k_ref, v_ref, qh_ref, kb_ref, lf_ref, ih_ref, gs_ref, k5_ref, v4_ref, *, width,
                    q_scale):
    xn = _rms(x_ref[...], g_ref[...]).astype(BF16)

    def proj(c):
        return _dot(xn, w_ref[:, c * width:(c + 1) * width])

    def group_norm(y, gain):
        ms = _dot((y * y).astype(BF16), pm_ref[...])
        return y * lax.rsqrt(ms + EPS) * gain

    q_ref[...] = group_norm(proj(0), qg_ref[...]) * q_scale
    kn = group_norm(proj(1), kg_ref[...])
    k_ref[...] = kn
    bb, ts, heads, _, dh = k5_ref.shape
    for h in range(heads):
        for c in range(2):
            piece = kn[:, (2 * h + c) * dh:(2 * h + c + 1) * dh]
            k5_ref[:, :, h, c, :] = piece.reshape(bb, ts, dh)
    vv = proj(2)
    v_ref[...] = vv
    for h in range(heads):
        v4_ref[:, :, h, :] = vv[:, h * 2 * dh:(h + 1) * 2 * dh].reshape(bb, ts, 2 * dh)
    yq = proj(3)
    qh_ref[...] = yq * _sigmoid(yq)
    zf = proj(4)
    lb = lb_ref[...]
    lf_ref[...] = jnp.log(lb + (1.0 - lb) * _sigmoid(zf))
    kb_ref[...] = (1.0 - lb) * _sigmoid(-zf)
    ih_ref[...] = proj(5)
    yg = proj(6)
    gs_ref[...] = yg * _sigmoid(yg)


def _in_even(x, g_mix, w_bf, pm, qg, kg, lb, *, width, q_scale, tm, batch, seq, heads, dh):
    n, d = x.shape
    assert n % tm == 0 and (tm % seq == 0 or seq % tm == 0)
    row = lambda i: (i, 0)
    out = jax.ShapeDtypeStruct((n, width), F32)
    bb, ts = max(tm // seq, 1), min(tm, seq)
    per = seq // ts
    k5_spec = pl.BlockSpec((bb, ts, heads, 2, dh), lambda i: (i // per, i % per, 0, 0, 0))
    v4_spec = pl.BlockSpec((bb, ts, heads, 2 * dh), lambda i: (i // per, i % per, 0, 0))
    return pl.pallas_call(
        functools.partial(_in_even_kernel, width=width, q_scale=q_scale),
        grid=(n // tm,),
        in_specs=[pl.BlockSpec((tm, d), row), _const_spec((1, d)), _const_spec(w_bf.shape),
                  _const_spec(pm.shape), _const_spec((1, width)), _const_spec((1, width)),
                  _const_spec((1, width))],
        out_specs=[pl.BlockSpec((tm, width), row)] * 8 + [k5_spec, v4_spec],
        out_shape=[out] * 8 + [jax.ShapeDtypeStruct((batch, seq, heads, 2, dh), F32),
                               jax.ShapeDtypeStruct((batch, seq, heads, 2 * dh), F32)],
        compiler_params=_params("arbitrary"),
        name="in_even",
    )(x, g_mix, w_bf, pm, qg, kg, lb)


def _bias_from_buckets(bk, rb_ref, h):
    b = jnp.zeros(bk.shape, F32)
    for u in range(REL_BUCKETS):
        b = jnp.where(bk == u, rb_ref[u, h], b)
    return jnp.where(bk < 0, -jnp.inf, b)


def _lam(lam_ref, lam_init):
    r = lam_ref[...]
    s1 = jnp.sum(r[0:1] * r[1:2], axis=1, keepdims=True)
    s2 = jnp.sum(r[2:3] * r[3:4], axis=1, keepdims=True)
    return jnp.exp(s1) - jnp.exp(s2) + lam_init


def _split_components(q, dh):
    lane = lax.broadcasted_iota(I32, q.shape, 1)
    q0 = jnp.where(lane < dh, q, 0.0)
    q1 = jnp.where(lane >= dh, q, 0.0)
    return jnp.concatenate([q0, q1], axis=0).astype(BF16)


def _attn_prompt_kernel(rb_ref, far_ref, lam_ref, bk_ref, sg_ref, q_ref, k_ref, v_ref, o_ref, bias_s, *,
                        qt, dh, lam_init):
    h = pl.program_id(0)
    t = q_ref.shape[0]
    kb = k_ref[...].astype(BF16)
    vt = v_ref[...].T.astype(BF16)
    qtr = q_ref[...].T
    sub = lax.broadcasted_iota(I32, (2 * dh, qt), 0)

    @pl.when(pl.program_id(1) == 0)
    def _():
        for d in range(2):
            b = _bias_from_buckets(bk_ref[d], rb_ref, h) * LOG2E
            bias_s[d] = jnp.concatenate([b, b], axis=1)

    bias = [bias_s[0], bias_s[1]]
    far = rb_ref[far_ref[0], h] * LOG2E
    lam = _lam(lam_ref, lam_init)
    gain = sg_ref[...] * (1.0 - lam_init)
    for i in range(t // qt):
        qi = qtr[:, i * qt:(i + 1) * qt]
        qz = jnp.concatenate([jnp.where(sub < dh, qi, 0.0), jnp.where(sub >= dh, qi, 0.0)],
                             axis=1).astype(BF16)
        n = (i + 1) * qt
        parts = [(n - qt, n)]
        s = [_dot(kb[n - qt:n], qz) + bias[0]]
        shift = [0.0]
        if i >= 1:
            parts.append((n - 2 * qt, n - qt))
            s.append(_dot(kb[n - 2 * qt:n - qt], qz) + bias[1])
            shift.append(0.0)
        if i >= 2:
            parts.append((0, n - 2 * qt))
            s.append(_dot(kb[:n - 2 * qt], qz))
            shift.append(far)
        m = functools.reduce(jnp.maximum, [jnp.max(x, axis=0, keepdims=True) + c for x, c in zip(s, shift)])
        p = [jnp.exp2(x - (m - c)) for x, c in zip(s, shift)]
        l = functools.reduce(lambda a, b: a + b, [jnp.sum(x, axis=0, keepdims=True) for x in p])
        acc = functools.reduce(lambda a, b: a + b,
                               [_dot(vt[:, lo:hi], x.astype(BF16)) for (lo, hi), x in zip(parts, p)])
        o = acc / l
        out = o[:, :qt] - lam * o[:, qt:]
        out = out * lax.rsqrt(jnp.mean(out * out, axis=0, keepdims=True) + EPS) * gain
        o_ref[i * qt:(i + 1) * qt, :] = out.T


def _attn_prompt(q, k, v, rel_bias, lam4, sub_gain, *, batch, seq, heads, dh, lam_init, qt):
    n, w = q.shape
    dv = w // heads
    assert dv == 2 * dh and seq % qt == 0 and qt % CHUNK == 0
    kj = jnp.arange(qt, dtype=I32)[:, None]
    qi = jnp.arange(qt, dtype=I32)[None, :]
    bk0 = jnp.where((kj // CHUNK) <= (qi // CHUNK), rel_bucket(kj - qi), -1)
    bk1 = rel_bucket(kj - qi - qt)
    bk = jnp.stack([bk0, bk1]).astype(I32)
    assert qt + 1 >= REL_MAX_DIST
    far = rel_bucket(jnp.full((1,), -(qt + 1), I32))
    smem = pl.BlockSpec(memory_space=pltpu.SMEM)
    seq_blk = pl.BlockSpec((seq, dv), lambda h, b: (b, h))
    return pl.pallas_call(
        functools.partial(_attn_prompt_kernel, qt=qt, dh=dh, lam_init=lam_init),
        grid=(heads, batch),
        in_specs=[smem, smem, _const_spec(lam4.shape), _const_spec(bk.shape), _const_spec((dv, 1)),
                  seq_blk, seq_blk, seq_blk],
        out_specs=seq_blk,
        out_shape=jax.ShapeDtypeStruct((n, w), F32),
        scratch_shapes=[pltpu.VMEM((2, qt, 2 * qt), F32)],
        compiler_params=_params("arbitrary", "arbitrary"),
        name="attn_prompt",
    )(rel_bias, far, lam4, bk, sub_gain.T, q, k, v)


def _attn_sample_kernel(rb_ref, lam_ref, bkc_ref, bkn_ref, sg_ref, q_ref, kc_ref, vc_ref, kn_ref, vn_ref,
                        o_ref, bc_s, bn_s, *, t, dh, lam_init, heads):
    dv = 2 * dh

    @pl.when(pl.program_id(0) == 0)
    def _():
        for hh in range(heads):
            bc = _bias_from_buckets(bkc_ref[...], rb_ref, hh) * LOG2E
            bn = _bias_from_buckets(bkn_ref[...], rb_ref, hh) * LOG2E
            bc_s[hh] = jnp.concatenate([bc, bc], axis=0)
            bn_s[hh] = jnp.concatenate([bn, bn], axis=0)

    lam = _lam(lam_ref, lam_init)
    past = vc_ref.shape[0] // heads
    for h in range(heads):
        cols = slice(h * dv, (h + 1) * dv)
        qz = _split_components(q_ref[:, cols], dh)
        sc = lax.dot_general(qz, kc_ref[0, :, cols].astype(BF16), NT_DIMS, preferred_element_type=F32)
        sn = lax.dot_general(qz, kn_ref[:, cols].astype(BF16), NT_DIMS, preferred_element_type=F32)
        sc = sc + bc_s[h]
        sn = sn + bn_s[h]
        m = jnp.maximum(jnp.max(sc, axis=1, keepdims=True), jnp.max(sn, axis=1, keepdims=True))
        pc = jnp.exp2(sc - m)
        pn = jnp.exp2(sn - m)
        l = jnp.sum(pc, axis=1, keepdims=True) + jnp.sum(pn, axis=1, keepdims=True)
        vc = vc_ref[pl.ds(h, past, stride=heads), :]
        acc = (_dot(pc.astype(BF16), vc.astype(BF16))
               + _dot(pn.astype(BF16), vn_ref[:, cols].astype(BF16)))
        o = acc / l
        out = o[:t] - lam * o[t:]
        o_ref[:, cols] = _rms(out, sg_ref[...]) * (1.0 - lam_init)


def _attn_sample(q, k_new, v_new, cache_k, cache_v, rel_bias, lam4, sub_gain, *, batch, t, heads, dh,
                 lam_init):
    n, w = q.shape
    dv = w // heads
    past = cache_k.shape[1]
    assert past % CHUNK == 0 and t <= CHUNK
    qpos = past + jnp.arange(t, dtype=I32)[:, None]
    bkc = rel_bucket(jnp.arange(past, dtype=I32)[None, :] - qpos).astype(I32)
    bkn = rel_bucket(past + jnp.arange(t, dtype=I32)[None, :] - qpos).astype(I32)
    smem = pl.BlockSpec(memory_space=pltpu.SMEM)
    new = pl.BlockSpec((t, w), lambda b: (b, 0))
    old_k = pl.BlockSpec((1, past, w), lambda b: (b, 0, 0))
    old_v = pl.BlockSpec((past * heads, dv), lambda b: (b, 0))
    return pl.pallas_call(
        functools.partial(_attn_sample_kernel, t=t, dh=dh, lam_init=lam_init, heads=heads),
        grid=(batch,),
        in_specs=[smem, _const_spec(lam4.shape), _const_spec(bkc.shape), _const_spec(bkn.shape),
                  _const_spec((1, dv)), new, old_k, old_v, new, new],
        out_specs=new,
        out_shape=jax.ShapeDtypeStruct((n, w), F32),
        scratch_shapes=[pltpu.VMEM((heads, 2 * t, past), F32), pltpu.VMEM((heads, 2 * t, t), F32)],
        compiler_params=_params("arbitrary"),
        name="attn_sample",
    )(rel_bias, lam4, bkc, bkn, sub_gain, q, cache_k, cache_v, k_new, v_new)


def _cumsum_rows(x):
    c = x.shape[0]
    row = lax.broadcasted_iota(I32, x.shape, 0)
    s = 1
    while s < c:
        x = x + jnp.where(row >= s, pltpu.roll(x, s, axis=0), 0.0)
        s *= 2
    return x


def _hgrn_kernel(qh_ref, kb_ref, lf_ref, ih_ref, s0_ref, hg_ref, ob_ref, sf_ref, st_s, *,
                 heads, dk, dv, c, nsb):
    t = pl.program_id(1)

    @pl.when(t == 0)
    def _():
        for h in range(heads):
            st_s[h] = s0_ref[0, h].T

    tb = qh_ref.shape[0]
    row = lax.broadcasted_iota(I32, (c, c), 0)
    col = lax.broadcasted_iota(I32, (c, c), 1)
    causal = col <= row

    def chunk(ci, carry):
        r0 = pl.multiple_of(ci * c, c)
        for h in range(heads):
            rows = pl.ds(r0, c)
            q = qh_ref[rows, h * dk:(h + 1) * dk]
            k = kb_ref[rows, h * dk:(h + 1) * dk]
            v = ih_ref[rows, h * dv:(h + 1) * dv]
            b = _cumsum_rows(lf_ref[rows, h * dk:(h + 1) * dk])
            bl = b[c - 1:c]
            st = st_s[h]
            inter = lax.dot_general((q * jnp.exp(b)).astype(BF16), st.astype(BF16), NT_DIMS,
                                    preferred_element_type=F32)
            qs, ks = [], []
            for j in range(nsb):
                ref = b[j * HG_SUB + HG_SUB // 2:j * HG_SUB + HG_SUB // 2 + 1]
                qs.append(q * jnp.exp(b - ref))
                sub = slice(j * HG_SUB, (j + 1) * HG_SUB)
                ks.append(k[sub] * jnp.exp(ref - b[sub]))
            a_full = lax.dot_general(jnp.concatenate(qs, axis=0).astype(BF16),
                                     jnp.concatenate(ks, axis=0).astype(BF16), NT_DIMS,
                                     preferred_element_type=F32)
            att = jnp.zeros((c, c), F32)
            for j in range(nsb):
                att = jnp.where(col >= j * HG_SUB, a_full[j * c:(j + 1) * c], att)
            att = jnp.where(causal, att, 0.0)
            out = inter + _dot(att.astype(BF16), v.astype(BF16))
            ob_ref[rows, h * dv:(h + 1) * dv] = _rms(out, hg_ref[...])
            kdec = (k * jnp.exp(bl - b)).astype(BF16)
            st_s[h] = jnp.exp(bl) * st + lax.dot_general(v.astype(BF16), kdec, TN_DIMS,
                                                         preferred_element_type=F32)
        return carry

    lax.fori_loop(0, tb // c, chunk, 0, unroll=min(8, tb // c))

    @pl.when(t == pl.num_programs(1) - 1)
    def _():
        for h in range(heads):
            sf_ref[0, h] = st_s[h].T


def _hgrn(qh, kb, lf, ih, s0, hg_gain, *, batch, seq, heads, dk, dv, tb):
    n = qh.shape[0]
    c = min(CHUNK, seq)
    assert seq % tb == 0 and tb % c == 0 and c % HG_SUB == 0
    nt = seq // tb
    blk = lambda w: pl.BlockSpec((tb, w), lambda b, t: (b * nt + t, 0))
    st = pl.BlockSpec((1, heads, dk, dv), lambda b, t: (b, 0, 0, 0))
    return pl.pallas_call(
        functools.partial(_hgrn_kernel, heads=heads, dk=dk, dv=dv, c=c, nsb=c // HG_SUB),
        grid=(batch, nt),
        in_specs=[blk(heads * dk), blk(heads * dk), blk(heads * dk), blk(heads * dv), st,
                  _const_spec((1, dv))],
        out_specs=[blk(heads * dv), st],
        out_shape=[jax.ShapeDtypeStruct((n, heads * dv), F32),
                   jax.ShapeDtypeStruct((batch, heads, dk, dv), F32)],
        scratch_shapes=[pltpu.VMEM((heads, dv, dk), F32)],
        compiler_params=_params("arbitrary", "arbitrary"),
        name="hgrn",
    )(qh, kb, lf, ih, s0, hg_gain)


def _store_token_tiles(ref, x):
    rows, d = x.shape
    s = d // LANES
    for c in range(s):
        ref[pl.ds(c, rows, stride=s), :] = x[:, c * LANES:(c + 1) * LANES]


def _load_token_tiles(ref, rows, s):
    return jnp.concatenate([ref[pl.ds(c, rows, stride=s), :] for c in range(s)], axis=1)


def _token_tile(r, s):
    return pl.ds(pl.multiple_of(r * s, s), s)


def _ffn_prologue(x1, gf_ref, wrh_ref, wrl_ref, rbias_ref, x1_ref, xn_ref, lg_ref):
    x1_ref[...] = x1
    xn = _rms(x1, gf_ref[...])
    _store_token_tiles(xn_ref, xn)
    hi = xn.astype(BF16)
    lo = (xn - hi.astype(F32)).astype(BF16)
    nt = functools.partial(lax.dot_general, dimension_numbers=NT_DIMS, preferred_element_type=F32)
    lg_ref[...] = nt(wrh_ref[...], hi) + nt(wrh_ref[...], lo) + nt(wrl_ref[...], hi) + rbias_ref[...]


def _out_even_kernel(oa_ref, ob_ref, gs_ref, x_ref, w_ref, gf_ref, wrh_ref, wrl_ref, rbias_ref,
                     x1_ref, xn_ref, lg_ref):
    o = jnp.concatenate([oa_ref[...], ob_ref[...] * gs_ref[...]], axis=1).astype(BF16)
    x1 = x_ref[...] + _dot(o, w_ref[...])
    _ffn_prologue(x1, gf_ref, wrh_ref, wrl_ref, rbias_ref, x1_ref, xn_ref, lg_ref)


def _out_even(oa, ob, gs, x, w_bf, g_ffn, wrh, wrl, rbias, *, tm):
    n, d = x.shape
    w = oa.shape[1]
    nr = wrh.shape[0]
    row = lambda i: (i, 0)
    return pl.pallas_call(
        _out_even_kernel,
        grid=(n // tm,),
        in_specs=[pl.BlockSpec((tm, w), row), pl.BlockSpec((tm, w), row), pl.BlockSpec((tm, w), row),
                  pl.BlockSpec((tm, d), row), _const_spec(w_bf.shape), _const_spec((1, d)),
                  _const_spec(wrh.shape), _const_spec(wrl.shape), _const_spec(rbias.shape)],
        out_specs=[pl.BlockSpec((tm, d), row), pl.BlockSpec((tm * (d // LANES), LANES), row),
                   pl.BlockSpec((nr, tm), lambda i: (0, i))],
        out_shape=[jax.ShapeDtypeStruct((n, d), F32), jax.ShapeDtypeStruct((n * (d // LANES), LANES), F32),
                   jax.ShapeDtypeStruct((nr, n), F32)],
        compiler_params=_params("arbitrary"),
        name="out_even",
    )(oa, ob, gs, x, w_bf, g_ffn, wrh, wrl, rbias)


def _gelu(x):
    return 0.5 * x * (1.0 + jnp.tanh(math.sqrt(2.0 / math.pi) * (x + 0.044715 * (x * x * x))))


def _odd_kernel(x_ref, gm_ref, win_ref, vg_ref, wsp_ref, bsp_ref, wout_ref, gf_ref, wrh_ref, wrl_ref,
                rbias_ref, x1_ref, xn_ref, lg_ref, *rest, half, groups, l, emit_v):
    if emit_v:
        vn_ref, u_s, s_s = rest
    else:
        vn_ref = None
        u_s, vn_s, s_s = rest
    tm = x_ref.shape[0]
    x = x_ref[...]
    xn = _rms(x, gm_ref[...]).astype(BF16)
    cw = 512
    vbuf = vn_ref if emit_v else vn_s
    for cidx in range(half // cw):
        u_s[:, cidx * cw:(cidx + 1) * cw] = _gelu(_dot(xn, win_ref[:, cidx * cw:(cidx + 1) * cw]))
        vbuf[:, cidx * cw:(cidx + 1) * cw] = _gelu(
            _dot(xn, win_ref[:, half + cidx * cw:half + (cidx + 1) * cw]))
    vbuf[...] = _rms(vbuf[...], vg_ref[...])
    gw = half // groups
    row = lax.broadcasted_iota(I32, (l, l), 0)
    col = lax.broadcasted_iota(I32, (l, l), 1)
    for g in range(groups):
        wg = jnp.where(col <= row, wsp_ref[g], 0.0).astype(BF16)
        bg = bsp_ref[:, g:g + 1]
        for ci in range(tm // l):
            vv = vbuf[ci * l:(ci + 1) * l, g * gw:(g + 1) * gw].astype(BF16)
            s_s[ci * l:(ci + 1) * l, g * gw:(g + 1) * gw] = _dot(wg, vv) + bg
    y = _dot((u_s[...] * s_s[...]).astype(BF16), wout_ref[...])
    _ffn_prologue(x + y, gf_ref, wrh_ref, wrl_ref, rbias_ref, x1_ref, xn_ref, lg_ref)


def _odd_mixer(x, g_mix, win_bf, v_gain, wsp, bsp_t, wout_bf, g_ffn, wrh, wrl, rbias, *, l, tm, emit_v):
    n, d = x.shape
    half = wout_bf.shape[0]
    groups = wsp.shape[0]
    nr = wrh.shape[0]
    assert n % tm == 0 and tm % l == 0
    row = lambda i: (i, 0)
    out_specs = [pl.BlockSpec((tm, d), row), pl.BlockSpec((tm * (d // LANES), LANES), row),
                 pl.BlockSpec((nr, tm), lambda i: (0, i))]
    out_shape = [jax.ShapeDtypeStruct((n, d), F32), jax.ShapeDtypeStruct((n * (d // LANES), LANES), F32),
                 jax.ShapeDtypeStruct((nr, n), F32)]
    scratch = [pltpu.VMEM((tm, half), F32)]
    if emit_v:
        out_specs.append(pl.BlockSpec((tm, half), row))
        out_shape.append(jax.ShapeDtypeStruct((n, half), F32))
    else:
        scratch.append(pltpu.VMEM((tm, half), F32))
    scratch.append(pltpu.VMEM((tm, half), F32))
    return pl.pallas_call(
        functools.partial(_odd_kernel, half=half, groups=groups, l=l, emit_v=emit_v),
        grid=(n // tm,),
        in_specs=[pl.BlockSpec((tm, d), row), _const_spec((1, d)), _const_spec(win_bf.shape),
                  _const_spec((1, half)), _const_spec(wsp.shape), _const_spec(bsp_t.shape),
                  _const_spec(wout_bf.shape), _const_spec((1, d)), _const_spec(wrh.shape),
                  _const_spec(wrl.shape), _const_spec(rbias.shape)],
        out_specs=out_specs,
        out_shape=out_shape,
        scratch_shapes=scratch,
        compiler_params=_params("arbitrary"),
        name="odd_mixer",
    )(x, g_mix, win_bf, v_gain, wsp, bsp_t, wout_bf, g_ffn, wrh, wrl, rbias)


def _route_kernel(lg_ref, tri_ref, init_ref, e_ref, g_ref, r_ref, p_ref, cnt_ref, base_ref, tcnt_ref, run_s,
                  *, groups, epg):
    i = pl.program_id(0)

    @pl.when(i == 0)
    def _():
        run_s[...] = init_ref[...]

    lg = lg_ref[...]
    tr = lg.shape[1]
    gl = [lg[g:g + 1] for g in range(groups)]
    m = functools.reduce(jnp.maximum, gl)
    grp = jnp.full((1, tr), groups - 1, I32)
    for g in range(groups - 2, -1, -1):
        grp = jnp.where(gl[g] == m, g, grp)
    gate_g = 1.0 / functools.reduce(lambda a, b: a + b, [jnp.exp(x - m) for x in gl])
    sel = lg[SUBLANES + (groups - 1) * epg:SUBLANES + groups * epg]
    for g in range(groups - 2, -1, -1):
        sel = jnp.where(grp == g, lg[SUBLANES + g * epg:SUBLANES + (g + 1) * epg], sel)
    sub = lax.broadcasted_iota(I32, sel.shape, 0)
    v1 = jnp.max(sel, axis=0, keepdims=True)
    i1 = jnp.min(jnp.where(sel == v1, sub, epg), axis=0, keepdims=True)
    sel2 = jnp.where(sub == i1, -jnp.inf, sel)
    v2 = jnp.max(sel2, axis=0, keepdims=True)
    i2 = jnp.min(jnp.where(sel2 == v2, sub, epg), axis=0, keepdims=True)
    tt = jnp.exp(v2 - v1)
    g1 = gate_g / (1.0 + tt)
    g2 = gate_g * tt / (1.0 + tt)
    e1 = grp * epg + i1
    e2 = grp * epg + i2
    ne = groups * epg
    eidx = lax.broadcasted_iota(I32, (ne, tr), 0)
    oh1 = eidx == e1
    oh2 = eidx == e2
    cnt = jnp.where(oh1, 1.0, 0.0) + jnp.where(oh2, 1.0, 0.0)
    local = _dot(cnt.astype(BF16), tri_ref[...])
    before = run_s[:, 0:1] + local
    r1 = jnp.sum(jnp.where(oh1, before, 0.0), axis=0, keepdims=True)
    r2 = jnp.sum(jnp.where(oh2, before, 0.0), axis=0, keepdims=True)
    tile_cnt = jnp.broadcast_to(jnp.sum(cnt, axis=1, keepdims=True), run_s.shape)
    padded = jnp.ceil(tile_cnt * (1.0 / RUN_CHUNK)) * RUN_CHUNK
    offset = _cumsum_rows(padded) - padded
    where_local = offset[:, 0:1] + local
    p1 = jnp.sum(jnp.where(oh1, where_local, 0.0), axis=0, keepdims=True)
    p2 = jnp.sum(jnp.where(oh2, where_local, 0.0), axis=0, keepdims=True)
    base_ref[...] = run_s[...]
    tcnt_ref[...] = tile_cnt
    run_s[...] = run_s[...] + tile_cnt
    rows = lax.broadcasted_iota(I32, (SUBLANES, tr), 0)
    e_ref[...] = jnp.where(rows == 0, e1, jnp.where(rows == 1, e2, 0))
    g_ref[...] = jnp.where(rows == 0, g1, jnp.where(rows == 1, g2, 0.0))
    r_ref[...] = jnp.where(rows == 0, r1, jnp.where(rows == 1, r2, 0.0)).astype(I32)
    p_ref[...] = jnp.where(rows == 0, p1, jnp.where(rows == 1, p2, 0.0)).astype(I32)
    cnt_ref[...] = run_s[...]


def _route(lgt, init, *, groups, epg, tr):
    nr, n = lgt.shape
    assert n % tr == 0 and nr == SUBLANES + groups * epg
    ne = groups * epg
    tri = (jnp.arange(tr)[:, None] < jnp.arange(tr)[None, :]).astype(BF16)
    tok = pl.BlockSpec((SUBLANES, tr), lambda i: (0, i))
    per_tile = pl.BlockSpec((ne, LANES), lambda i: (i, 0))
    tile_tab = jax.ShapeDtypeStruct((n // tr * ne, LANES), F32)
    return pl.pallas_call(
        functools.partial(_route_kernel, groups=groups, epg=epg),
        grid=(n // tr,),
        in_specs=[pl.BlockSpec((nr, tr), lambda i: (0, i)), _const_spec((tr, tr)), _const_spec((ne, LANES))],
        out_specs=[tok, tok, tok, tok, pl.BlockSpec((ne, LANES), lambda i: (0, 0)), per_tile, per_tile],
        out_shape=[jax.ShapeDtypeStruct((SUBLANES, n), I32), jax.ShapeDtypeStruct((SUBLANES, n), F32),
                   jax.ShapeDtypeStruct((SUBLANES, n), I32), jax.ShapeDtypeStruct((SUBLANES, n), I32),
                   jax.ShapeDtypeStruct((ne, LANES), F32), tile_tab, tile_tab],
        scratch_shapes=[pltpu.VMEM((ne, LANES), F32)],
        compiler_params=_params("arbitrary"),
        name="route",
    )(lgt, tri, init)


ISSUE_UNROLL = 16
DISPATCH_SLOTS = 3


def _dispatch_kernel(seg_ref, dst_ref, *rest, s, bm, tiles):
    x_refs = rest[:len(tiles)]
    buf_ref, zero_s, xin_s, sem, isem, zsem = rest[len(tiles):]
    tp = dst_ref.shape[2]
    blk = bm * s

    def zero_fill():
        zero_s[...] = jnp.zeros(zero_s.shape, F32)

        def block_copy(b):
            return pltpu.make_async_copy(zero_s, buf_ref.at[pl.ds(pl.multiple_of(b * blk, blk), blk)], zsem)

        for e in range(seg_ref.shape[1]):
            @pl.when(seg_ref[1, e] > 0)
            def _():
                block_copy(seg_ref[0, e] // bm - 1).start()

            @pl.when(seg_ref[1, e] - seg_ref[2, e] > bm)
            def _():
                block_copy(seg_ref[0, e] // bm - 2).start()
        for e in range(seg_ref.shape[1]):
            @pl.when(seg_ref[1, e] > 0)
            def _():
                block_copy(0).wait()

            @pl.when(seg_ref[1, e] - seg_ref[2, e] > bm)
            def _():
                block_copy(0).wait()

        def tail_start(b, c):
            block_copy(b).start()
            return c

        def tail_wait(b, c):
            block_copy(b).wait()
            return c

        first_unused = seg_ref[0, seg_ref.shape[1] - 1] // bm
        n_blocks = buf_ref.shape[0] // blk
        lax.fori_loop(first_unused, n_blocks, tail_start, 0)
        lax.fori_loop(first_unused, n_blocks, tail_wait, 0)

    pl.when(pl.program_id(0) == 0)(zero_fill)

    i = pl.program_id(0)
    last = pl.num_programs(0) - 1
    rows = tp * s

    def tile_in(t, go):
        lo = 0
        for x_ref, nt in zip(x_refs, tiles):
            @pl.when((t >= lo) & (t < lo + nt))
            def _(x_ref=x_ref, lo=lo):
                cp = pltpu.make_async_copy(x_ref.at[pl.ds(pl.multiple_of((t - lo) * rows, rows), rows)],
                                           xin_s.at[t % DISPATCH_SLOTS], isem.at[t % DISPATCH_SLOTS])
                cp.start() if go else cp.wait()
            lo += nt

    def rows_out_wait(t):
        for kk in range(TOP_K):
            pltpu.make_async_copy(xin_s.at[t % DISPATCH_SLOTS], buf_ref.at[pl.ds(0, rows)],
                                  sem.at[t % DISPATCH_SLOTS]).wait()

    @pl.when(i == 0)
    def _():
        tile_in(i, True)

    @pl.when(i >= DISPATCH_SLOTS - 1)
    def _():
        rows_out_wait(i - (DISPATCH_SLOTS - 1))

    @pl.when(i < last)
    def _():
        tile_in(i + 1, True)

    tile_in(i, False)
    slot = i % DISPATCH_SLOTS

    def issue(r0, c):
        for u in range(ISSUE_UNROLL):
            r = r0 * ISSUE_UNROLL + u
            for kk in range(TOP_K):
                pltpu.make_async_copy(xin_s.at[slot, _token_tile(r, s)],
                                      buf_ref.at[_token_tile(dst_ref[0, kk, r], s)], sem.at[slot]
                                      ).start(priority=kk % 2)
        return c

    lax.fori_loop(0, tp // ISSUE_UNROLL, issue, 0)

    @pl.when(i == last)
    def _():
        for back in range(DISPATCH_SLOTS - 2, -1, -1):
            @pl.when(i >= back)
            def _():
                rows_out_wait(i - back)


def _dispatch(seg, dest3, xns, n_slots, *, tp, s, bm):
    assert tp % ISSUE_UNROLL == 0 and all(x.shape[0] % (tp * s) == 0 for x in xns)
    tiles = tuple(x.shape[0] // (tp * s) for x in xns)
    grid_spec = pltpu.PrefetchScalarGridSpec(
        num_scalar_prefetch=1,
        grid=(sum(tiles),),
        in_specs=[pl.BlockSpec((1, TOP_K, tp), lambda i, sg: (i, 0, 0), memory_space=pltpu.SMEM)]
        + [pl.BlockSpec(memory_space=pl.ANY)] * len(xns),
        out_specs=pl.BlockSpec(memory_space=pl.ANY),
        scratch_shapes=[pltpu.VMEM((bm * s, LANES), F32), pltpu.VMEM((DISPATCH_SLOTS, tp * s, LANES), F32),
                        pltpu.SemaphoreType.DMA((DISPATCH_SLOTS,)), pltpu.SemaphoreType.DMA((DISPATCH_SLOTS,)),
                        pltpu.SemaphoreType.DMA(())],
    )
    return pl.pallas_call(
        functools.partial(_dispatch_kernel, s=s, bm=bm, tiles=tiles),
        grid_spec=grid_spec,
        out_shape=jax.ShapeDtypeStruct((n_slots * s, LANES), F32),
        compiler_params=_params("arbitrary"),
        name="dispatch",
    )(seg, dest3, *xns)


def _expert_kernel(be_ref, nu_ref, x_ref, wg_ref, wu_ref, wd_ref, o_ref, wg_s, wu_s, wd_s):
    b = pl.program_id(0)
    s = wg_s.shape[0] // LANES
    prev = be_ref[jnp.maximum(b - 1, 0)]

    @pl.when((b == 0) | (be_ref[b] != prev))
    def _():
        wg_s[...] = wg_ref[0, 0].astype(BF16)
        wu_s[...] = wu_ref[0, 0].astype(BF16)
        wd_s[...] = wd_ref[0, 0].astype(BF16)

    @pl.when(b < nu_ref[0])
    def _():
        xb = _load_token_tiles(x_ref, x_ref.shape[0] // s, s).astype(BF16)
        gate = _dot(xb, wg_s[...])
        h = gate * _sigmoid(gate) * _dot(xb, wu_s[...])
        _store_token_tiles(o_ref, _dot(h.astype(BF16), wd_s[...]))

    @pl.when(b >= nu_ref[0])
    def _():
        o_ref[...] = jnp.zeros(o_ref.shape, F32)


def _experts(blk_expert, n_used, buf, w_gate, w_up, w_down, *, layer, bm):
    d, de = w_gate.shape[2:]
    s = d // LANES
    n_blocks = buf.shape[0] // (bm * s)
    rows = pl.BlockSpec((bm * s, LANES), lambda b, be, nu: (b, 0))
    used_rows = pl.BlockSpec((bm * s, LANES), lambda b, be, nu: (jnp.minimum(b, nu[0] - 1), 0))
    grid_spec = pltpu.PrefetchScalarGridSpec(
        num_scalar_prefetch=2,
        grid=(n_blocks,),
        in_specs=[used_rows,
                  pl.BlockSpec((1, 1, d, de), lambda b, be, nu: (layer, be[b], 0, 0)),
                  pl.BlockSpec((1, 1, d, de), lambda b, be, nu: (layer, be[b], 0, 0)),
                  pl.BlockSpec((1, 1, de, d), lambda b, be, nu: (layer, be[b], 0, 0))],
        out_specs=rows,
        scratch_shapes=[pltpu.VMEM((d, de), BF16), pltpu.VMEM((d, de), BF16), pltpu.VMEM((de, d), BF16)],
    )
    return pl.pallas_call(
        _expert_kernel,
        grid_spec=grid_spec,
        out_shape=jax.ShapeDtypeStruct(buf.shape, F32),
        compiler_params=_params("arbitrary"),
        name="experts",
    )(blk_expert, n_used, buf, w_gate, w_up, w_down)


def _combine_kernel(tab_ref, nxt_ref, pos_ref, gate_ref, x_ref, yb_ref, o_ref, stg_s, tt_s, sem, *, ahead):
    i = pl.program_id(0)
    tq, d = x_ref.shape
    s = d // LANES
    piece = RUN_CHUNK * s
    slot = i % 2

    def run_copies(ref, dst_slot, go):
        for e in range(ref.shape[2]):
            def body(j, c):
                src = pl.multiple_of((ref[0, 0, e] + j * RUN_CHUNK) * s, s)
                dst = pl.multiple_of((ref[0, 2, e] + j * RUN_CHUNK) * s, piece)
                cp = pltpu.make_async_copy(yb_ref.at[pl.ds(src, piece)],
                                           stg_s.at[dst_slot, pl.ds(dst, piece)], sem.at[dst_slot])
                cp.start() if go else cp.wait()
                return c
            lax.fori_loop(0, ref[0, 1, e], body, 0)

    if ahead:
        @pl.when(i == 0)
        def _():
            run_copies(tab_ref, 0, True)

        @pl.when(i + 1 < pl.num_programs(0))
        def _():
            run_copies(nxt_ref, 1 - slot, True)
    else:
        run_copies(tab_ref, slot, True)

    run_copies(tab_ref, slot, False)

    def assemble(r0, c):
        for u in range(ISSUE_UNROLL):
            r = r0 * ISSUE_UNROLL + u
            acc = gate_ref[0, 0, r] * stg_s[slot, _token_tile(pos_ref[0, 0, r], s), :]
            for kk in range(1, TOP_K):
                acc = acc + gate_ref[0, kk, r] * stg_s[slot, _token_tile(pos_ref[0, kk, r], s), :]
            tt_s[_token_tile(r, s), :] = acc
        return c

    lax.fori_loop(0, tq // ISSUE_UNROLL, assemble, 0)
    o_ref[...] = x_ref[...] + _load_token_tiles(tt_s, tq, s)


def _combine(tab, pos3, gates3, x1, yb, *, tq):
    n, d = x1.shape
    s = d // LANES
    nt = n // tq
    ne = tab.shape[2]
    assert tq % ISSUE_UNROLL == 0
    stage_rows = tq * TOP_K + ne * RUN_CHUNK
    smem = lambda shape, imap: pl.BlockSpec(shape, imap, memory_space=pltpu.SMEM)
    return pl.pallas_call(
        functools.partial(_combine_kernel, ahead=nt > 2),
        grid=(nt,),
        in_specs=[smem((1, 3, ne), lambda i: (i, 0, 0)),
                  smem((1, 3, ne), lambda i: (jnp.minimum(i + 1, nt - 1), 0, 0)),
                  smem((1, TOP_K, tq), lambda i: (i, 0, 0)),
                  smem((1, TOP_K, tq), lambda i: (i, 0, 0)),
                  pl.BlockSpec((tq, d), lambda i: (i, 0)),
                  pl.BlockSpec(memory_space=pl.ANY)],
        out_specs=pl.BlockSpec((tq, d), lambda i: (i, 0)),
        out_shape=jax.ShapeDtypeStruct((n, d), F32),
        scratch_shapes=[pltpu.VMEM((2, stage_rows * s, LANES), F32), pltpu.VMEM((tq * s, LANES), F32),
                        pltpu.SemaphoreType.DMA((2,))],
        compiler_params=_params("arbitrary"),
        name="combine",
    )(tab, tab, pos3, gates3, x1, yb)


def _moe(streams, w_gate, w_up, w_down, *, layer, groups, epg):
    d = streams[0][0].shape[1]
    ne = groups * epg
    routed = []
    taken = jnp.zeros((ne, LANES), F32)
    for x1, _, lgt in streams:
        rtile = min(MOE_TILE, x1.shape[0])
        routed.append(_route(lgt, taken, groups=groups, epg=epg, tr=rtile) + (rtile,))
        taken = routed[-1][4]
    n_rows = sum(x1.shape[0] for x1, _, _ in streams) * TOP_K
    counts = taken[:, 0].astype(I32)
    bm = MOE_BLOCK if n_rows >= 2 * ne * MOE_BLOCK else MOE_BLOCK_SMALL
    padded = (counts + RUN_CHUNK + bm - 1) // bm * bm
    pend = jnp.cumsum(padded)
    pstart = pend - padded
    n_blocks = -(-(n_rows + ne * RUN_CHUNK) // bm) + ne
    eids = jnp.arange(ne, dtype=I32)
    blk_row = jnp.arange(n_blocks, dtype=I32)[:, None] * bm
    blk_expert = jnp.minimum(jnp.sum((pend[None, :] <= blk_row).astype(I32), axis=1), ne - 1)
    n_used = (pend[-1:] // bm).astype(I32)
    seg_tab = jnp.stack([pend, padded, counts]).astype(I32)
    dtile = min(r[-1] for r in routed)
    dests = []
    for (x1, _, _), (e8, g8, r8, p8, cnt, base, tcnt, rtile) in zip(streams, routed):
        seg = jnp.sum(jnp.where(e8[:TOP_K, :, None] == eids, pstart, 0), axis=-1)
        dest = seg + r8[:TOP_K]
        dests.append(dest.reshape(TOP_K, x1.shape[0] // dtile, dtile).transpose(1, 0, 2))
    buf = _dispatch(seg_tab, jnp.concatenate(dests, axis=0), [xn for _, xn, _ in streams], n_blocks * bm,
                    tp=dtile, s=d // LANES, bm=bm)
    yb = _experts(blk_expert, n_used, buf, w_gate, w_up, w_down, layer=layer, bm=bm)
    outs = []
    for (x1, _, _), (e8, g8, r8, p8, cnt, base, tcnt, rtile) in zip(streams, routed):
        nrt = x1.shape[0] // rtile
        run_start = pstart[None, :] + base.reshape(nrt, ne, LANES)[:, :, 0].astype(I32)
        pieces = (tcnt.reshape(nrt, ne, LANES)[:, :, 0].astype(I32) + RUN_CHUNK - 1) // RUN_CHUNK
        stage = (jnp.cumsum(pieces, axis=1) - pieces) * RUN_CHUNK
        tab = jnp.stack([run_start, pieces, stage], axis=1)
        by_tile = lambda a: a[:TOP_K].reshape(TOP_K, nrt, rtile).transpose(1, 0, 2)
        outs.append(_combine(tab, by_tile(p8), by_tile(g8), x1, yb, tq=rtile))
    return outs


def _router_weights(wg, bg, we, be):
    d, groups = wg.shape
    epg = we.shape[2]
    assert groups <= SUBLANES and epg == SUBLANES
    pad = jnp.zeros((SUBLANES - groups, d), F32)
    wr = jnp.concatenate([wg.T, pad, we.transpose(0, 2, 1).reshape(groups * epg, d)], axis=0)
    rb = jnp.concatenate([bg, jnp.zeros((SUBLANES - groups,), F32), be.reshape(-1)])[:, None]
    hi = wr.astype(BF16)
    lo = (wr - hi.astype(F32)).astype(BF16)
    return hi, lo, rb, groups, epg


def kernel(x_prompt, x_sample, cache_attn_k, cache_attn_v, state_hgrn, rel_bias, norm_mix, norm_ffn,
           w_in_even, w_out_even, q_norm_gain, k_norm_gain, lam_q1, lam_k1, lam_q2, lam_k2, da_out_gain,
           hgrn_lb_logits, hgrn_out_gain, w_in_odd, sgu_v_gain, sgu_w, sgu_b, w_out_odd,
           router_group_w, router_group_b, router_expert_w, router_expert_b,
           expert_w_gate, expert_w_up, expert_w_down):
    bp, tp, d = x_prompt.shape
    bs, ts, _ = x_sample.shape
    depth = norm_mix.shape[0]
    _, _, past, da_heads, _, da_dh = cache_attn_k.shape
    da_dv = cache_attn_v.shape[-1]
    _, _, hg_heads, hg_dk, hg_dv = state_hgrn.shape
    width = da_heads * da_dv
    assert width == da_heads * 2 * da_dh == hg_heads * hg_dk == hg_heads * hg_dv
    assert da_dv == LANES and hg_dk == LANES and hg_dv == LANES

    lb_all = jnp.cumsum(jax.nn.softmax(hgrn_lb_logits.astype(F32), axis=0), axis=0)
    gid = jnp.arange(width) // da_dh
    pm = jnp.where(gid[:, None] == gid[None, :], 1.0 / da_dh, 0.0).astype(BF16)

    xs = {"p": x_prompt.reshape(bp * tp, d), "s": x_sample.reshape(bs * ts, d)}
    dims = {"p": (bp, tp), "s": (bs, ts)}
    outs = {"p": {}, "s": {}}
    kp_l, vp_l, ks_l, vs_l, sp_l, ss_l, sgu_l = [], [], [], [], [], [], []

    for layer in range(depth):
        j = layer // 2
        wrh, wrl, rbias, groups, epg = _router_weights(
            router_group_w[layer], router_group_b[layer], router_expert_w[layer], router_expert_b[layer])
        g_mix = norm_mix[layer][None, :]
        g_ffn = norm_ffn[layer][None, :]
        if layer % 2 == 0:
            lam_init = 0.8 - 0.6 * math.exp(-0.3 * layer)
            w_in_bf = w_in_even[j].astype(BF16)
            w_out_bf = w_out_even[j].astype(BF16)
            reps = width // da_dh
            qg = jnp.tile(q_norm_gain[j], reps)[None, :]
            kg = jnp.tile(k_norm_gain[j], reps)[None, :]
            lam4 = jnp.stack([lam_q1[j], lam_k1[j], lam_q2[j], lam_k2[j]])
            sub_gain = da_out_gain[j][None, :]
            hg_gain = hgrn_out_gain[j][None, :]
            lb = lb_all[j][None, :]
            for key in ("p", "s"):
                b, t = dims[key]
                x = xs[key]
                q, k, v, qh, kb, lf, ih, gs, k5, v4 = _in_even(
                    x, g_mix, w_in_bf, pm, qg, kg, lb, width=width, q_scale=da_dh ** -0.5 * LOG2E,
                    tm=min(512, b * t), batch=b, seq=t, heads=da_heads, dh=da_dh)
                if key == "p":
                    oa = _attn_prompt(q, k, v, rel_bias, lam4, sub_gain, batch=b, seq=t, heads=da_heads,
                                      dh=da_dh, lam_init=lam_init, qt=min(256, t))
                    s0 = jnp.zeros((b, hg_heads, hg_dk, hg_dv), F32)
                    ob, s_new = _hgrn(qh, kb, lf, ih, s0, hg_gain, batch=b, seq=t, heads=hg_heads,
                                      dk=hg_dk, dv=hg_dv, tb=min(512, t))
                    kp_l.append(k5)
                    vp_l.append(v4)
                    sp_l.append(s_new)
                else:
                    ck = cache_attn_k[j].reshape(b, past, width)
                    cv = cache_attn_v[j].reshape(b * past * da_heads, da_dv)
                    oa = _attn_sample(q, k, v, ck, cv, rel_bias, lam4, sub_gain, batch=b, t=t,
                                      heads=da_heads, dh=da_dh, lam_init=lam_init)
                    ob, s_new = _hgrn(qh, kb, lf, ih, state_hgrn[j], hg_gain, batch=b, seq=t,
                                      heads=hg_heads, dk=hg_dk, dv=hg_dv, tb=t)
                    ks_l.append(k5)
                    vs_l.append(v4)
                    ss_l.append(s_new)
                outs[key] = _out_even(oa, ob, gs, x, w_out_bf, g_ffn, wrh, wrl, rbias, tm=min(1024, b * t))
        else:
            w_in_bf = w_in_odd[j].astype(BF16)
            w_out_bf = w_out_odd[j].astype(BF16)
            v_gain = sgu_v_gain[j][None, :]
            for key in ("p", "s"):
                b, t = dims[key]
                l = min(SGU_CHUNK, t)
                res = _odd_mixer(xs[key], g_mix, w_in_bf, v_gain, sgu_w[j][:, :l, :l], sgu_b[j][:, :l].T,
                                 w_out_bf, g_ffn, wrh, wrl, rbias, l=l, tm=min(512, b * t), emit_v=(key == "s"))
                outs[key] = res[:3]
                if key == "s":
                    sgu_l.append(res[3].reshape(b, t, -1))
        xs["p"], xs["s"] = _moe([outs["p"], outs["s"]], expert_w_gate, expert_w_up, expert_w_down,
                                layer=layer, groups=groups, epg=epg)

    return (xs["p"].reshape(bp, tp, d), xs["s"].reshape(bs, ts, d), jnp.stack(kp_l), jnp.stack(vp_l),
            jnp.stack(ks_l), jnp.stack(vs_l), jnp.stack(sp_l), jnp.stack(ss_l), jnp.stack(sgu_l))
```
